```python
import math
import jax
import jax.numpy as jnp
from jax import lax
import numpy as np

D_MODEL = 1024
BATCH = 16
SEQ = 2048
DEPTH = 4

N_EVEN = (DEPTH + 1) // 2
N_ODD = DEPTH // 2

GLA_HEADS = 4
GLA_DV = D_MODEL // (2 * GLA_HEADS)
GLA_DK = GLA_DV // 2
GLA_RANK = 16
GLA_GATE_NORM = 16.0
GLA_CHUNK = 16

HGRN_HEADS = 4
HGRN_DV = D_MODEL // (2 * HGRN_HEADS)
HGRN_DK = HGRN_DV // 2
HGRN_CHUNK = 16
HGRN_MIN_F = 1e-20

RET_HEADS = 4
RET_DK = D_MODEL // (2 * RET_HEADS)
RET_DV = (3 * D_MODEL) // (4 * RET_HEADS)
RET_CHUNK = 64
ROPE_BASE = 10000.0

S5_WIDTH = D_MODEL // 4
S5_GROUP_CH = 16
S5_GROUPS = S5_WIDTH // S5_GROUP_CH
S5_STATE = 64

FFN_DIM = ((8 * D_MODEL // 3 + 255) // 256) * 256
CONV_WIDTH = 3
EPS = 1e-6

EVEN_COLS = (GLA_HEADS * GLA_DK, GLA_HEADS * GLA_DK, GLA_HEADS * GLA_DV, GLA_HEADS * GLA_DV, GLA_RANK, GLA_RANK,
             HGRN_HEADS * HGRN_DK, HGRN_HEADS * HGRN_DK, HGRN_HEADS * HGRN_DK, HGRN_HEADS * HGRN_DV, HGRN_HEADS * HGRN_DV)
ODD_COLS = (RET_HEADS * RET_DK, RET_HEADS * RET_DK, RET_HEADS * RET_DV, RET_HEADS * RET_DV, S5_WIDTH)
EVEN_IN = sum(EVEN_COLS)
ODD_IN = sum(ODD_COLS)
EVEN_MIX = GLA_HEADS * GLA_DV + HGRN_HEADS * HGRN_DV
ODD_MIX = RET_HEADS * RET_DV + S5_WIDTH

kernel_name = 'bidir_hybrid_gla_hgrn2_retnet_s5_convffn'

F32 = jnp.float32


def _split(p, sizes):
    return jnp.split(p, np.cumsum(sizes)[:-1].tolist(), axis=-1)


def rmsnorm(x, g):
    xf = x.astype(F32)
    y = xf * lax.rsqrt(jnp.mean(xf * xf, axis=-1, keepdims=True) + EPS)
    return (y * g.astype(F32)).astype(x.dtype)


def to_heads(t, h):
    b, s, _ = t.shape
    return t.reshape(b, s, h, -1).transpose(0, 2, 1, 3).astype(F32)


def from_heads(t):
    return t.transpose(0, 2, 1, 3)


def head_rmsnorm(o, g):
    b, s, h, d = o.shape
    y = o * lax.rsqrt(jnp.mean(o * o, axis=-1, keepdims=True) + EPS)
    return (y * g.astype(F32).reshape(h, d)).reshape(b, s, h * d)


def head_groupnorm(o, g):
    b, s, h, d = o.shape
    mu = jnp.mean(o, axis=-1, keepdims=True)
    c = o - mu
    y = c * lax.rsqrt(jnp.mean(c * c, axis=-1, keepdims=True) + EPS)
    return (y * g.astype(F32).reshape(h, d)).reshape(b, s, h * d)


def chunk_gated_scan(q, k, v, log_a, chunk):
    b, h, s, dk = q.shape
    dv = v.shape[-1]
    dg = log_a.shape[-1]
    n = s // chunk
    q, k, v, log_a = (t.reshape(b, h, n, chunk, t.shape[-1]) for t in (q, k, v, log_a))
    cum = jnp.cumsum(log_a, axis=3)
    last = cum[:, :, :, -1:, :]
    pos = jnp.arange(chunk)
    lower = (pos[:, None] >= pos[None, :])[:, :, None]
    rel = cum[:, :, :, :, None, :] - cum[:, :, :, None, :, :]
    decay = jnp.where(lower, jnp.exp(jnp.where(lower, rel, 0.0)), 0.0)
    if dg == 1:
        scores = jnp.einsum('bhnid,bhnjd->bhnij', q, k) * decay[..., 0]
    else:
        scores = jnp.einsum('bhnid,bhnjd,bhnijd->bhnij', q, k, decay)
    o_intra = jnp.einsum('bhnij,bhnjv->bhniv', scores, v)
    q_in = q * jnp.exp(cum)
    k_out = k * jnp.exp(last - cum)
    d_state = jnp.einsum('bhnid,bhniv->bhndv', k_out, v)
    chunk_decay = jnp.exp(last[:, :, :, 0, :])

    def step(state, inp):
        g_c, ds_c = inp
        return g_c[..., None] * state + ds_c, state

    init = jnp.zeros((b, h, dk, dv), q.dtype)
    _, prev = lax.scan(step, init, (jnp.moveaxis(chunk_decay, 2, 0), jnp.moveaxis(d_state, 2, 0)))
    prev = jnp.moveaxis(prev, 0, 2)
    o_inter = jnp.einsum('bhnid,bhndv->bhniv', q_in, prev)
    return (o_intra + o_inter).reshape(b, h, s, dv)


def bidir_scan(q, k_f, k_b, v, la_f, la_b, chunk):
    flip = lambda t: jnp.flip(t, axis=2)
    fwd = chunk_gated_scan(q, k_f, v, la_f, chunk)
    bwd = chunk_gated_scan(flip(q), flip(k_b), flip(v), flip(la_b), chunk)
    return fwd + flip(bwd)


def gla_mixer(q, k, v, r, lr_f, lr_b, wa2, ba, norm_g):
    q = to_heads(q, GLA_HEADS)
    k = to_heads(k, GLA_HEADS) * (GLA_DK ** -0.5)
    v = to_heads(v, GLA_HEADS)

    def log_gate(lr, d):
        z = jnp.einsum('bsr,rk->bsk', lr, wa2[d]) + ba[d]
        return to_heads(jax.nn.log_sigmoid(z.astype(F32)) / GLA_GATE_NORM, GLA_HEADS)

    o = bidir_scan(q, k, k, v, log_gate(lr_f, 0), log_gate(lr_b, 1), GLA_CHUNK)
    o = head_rmsnorm(from_heads(o), norm_g)
    return o * jax.nn.silu(r.astype(F32))


def hgrn_lower_bounds(lb_logits):
    p = jax.nn.softmax(lb_logits.astype(F32), axis=1)
    return jnp.cumsum(p, axis=1) - p[:, :1]


def hgrn2_mixer(q, z_f, z_b, i, g, lb_f, lb_b, norm_g):
    q = jax.nn.silu(to_heads(q, HGRN_HEADS))
    v = to_heads(i, HGRN_HEADS)

    def gate(z, lb):
        z = z.astype(F32)
        f = lb + (1.0 - lb) * jax.nn.sigmoid(z)
        log_f = jnp.log(jnp.maximum(f, HGRN_MIN_F))
        key = (1.0 - lb) * jax.nn.sigmoid(-z)
        return to_heads(log_f, HGRN_HEADS), to_heads(key, HGRN_HEADS)

    la_f, k_f = gate(z_f, lb_f)
    la_b, k_b = gate(z_b, lb_b)
    o = bidir_scan(q, k_f, k_b, v, la_f, la_b, HGRN_CHUNK)
    o = head_rmsnorm(from_heads(o), norm_g)
    return o * jax.nn.silu(g.astype(F32))


def rotary(t):
    s, d = t.shape[2], t.shape[3]
    half = d // 2
    inv = ROPE_BASE ** (-jnp.arange(half, dtype=F32) / half)
    ang = jnp.arange(s, dtype=F32)[:, None] * inv[None, :]
    cos, sin = jnp.cos(ang), jnp.sin(ang)
    t1, t2 = t[..., :half], t[..., half:]
    return jnp.concatenate([t1 * cos - t2 * sin, t1 * sin + t2 * cos], axis=-1)


def retention_mixer(q, k, v, g, norm_g):
    q = rotary(to_heads(q, RET_HEADS))
    k = rotary(to_heads(k, RET_HEADS)) * (RET_DK ** -0.5)
    v = to_heads(v, RET_HEADS)
    b, h, s, _ = q.shape
    hidx = jnp.arange(RET_HEADS, dtype=F32)
    log_gamma_f = jnp.log1p(-jnp.exp2(-5.0 - hidx))
    log_gamma_b = jnp.log1p(-jnp.exp2(-5.5 - hidx))
    la_f = jnp.broadcast_to(log_gamma_f[None, :, None, None], (b, h, s, 1))
    la_b = jnp.broadcast_to(log_gamma_b[None, :, None, None], (b, h, s, 1))
    o = bidir_scan(q, k, k, v, la_f, la_b, RET_CHUNK)
    o = head_groupnorm(from_heads(o), norm_g)
    return o * jax.nn.silu(g.astype(F32))


def _complex_affine_combine(e1, e2):
    a1r, a1i, b1r, b1i = e1
    a2r, a2i, b2r, b2i = e2
    return (a1r * a2r - a1i * a2i,
            a1r * a2i + a1i * a2r,
            a2r * b1r - a2i * b1i + b2r,
            a2r * b1i + a2i * b1r + b2i)


def _s5_direction(ug, lam_re, lam_im, log_dt, b_re, b_im, reverse):
    s = ug.shape[1]
    lr = jnp.minimum(lam_re.astype(F32), -1e-4)
    li = lam_im.astype(F32)
    dt = jnp.exp(log_dt.astype(F32))[:, None]
    mag = jnp.exp(lr * dt)
    ar, ai = mag * jnp.cos(li * dt), mag * jnp.sin(li * dt)
    den = lr * lr + li * li
    nr = ar - 1.0
    cr = (nr * lr + ai * li) / den
    ci = (ai * lr - nr * li) / den
    br, bi = b_re.astype(F32), b_im.astype(F32)
    bbr = cr[..., None] * br - ci[..., None] * bi
    bbi = cr[..., None] * bi + ci[..., None] * br
    xr = jnp.einsum('bsgp,gnp->bsgn', ug, bbr)
    xi = jnp.einsum('bsgp,gnp->bsgn', ug, bbi)
    g_, n_ = ar.shape
    a_r = jnp.broadcast_to(ar[None, None], (1, s, g_, n_))
    a_i = jnp.broadcast_to(ai[None, None], (1, s, g_, n_))
    _, _, sr, si = lax.associative_scan(_complex_affine_combine, (a_r, a_i, xr, xi), reverse=reverse, axis=1)
    return sr, si


def s5_mixer(u, lam_re, lam_im, log_dt, b_re, b_im, c_re, c_im, d_skip, glu_w, glu_b):
    bsz, s, _ = u.shape
    uf = u.astype(F32)
    ug = uf.reshape(bsz, s, S5_GROUPS, S5_GROUP_CH)
    fr, fi = _s5_direction(ug, lam_re[0], lam_im[0], log_dt[0], b_re, b_im, False)
    rr, ri = _s5_direction(ug, lam_re[1], lam_im[1], log_dt[1], b_re, b_im, True)
    hr, hi = fr + rr, fi + ri
    y = (jnp.einsum('bsgn,gpn->bsgp', hr, c_re.astype(F32))
         - jnp.einsum('bsgn,gpn->bsgp', hi, c_im.astype(F32)))
    y = y.reshape(bsz, s, S5_WIDTH) + d_skip.astype(F32) * uf
    g = jax.nn.gelu(y)
    return g * jax.nn.sigmoid(jnp.einsum('bsc,ce->bse', g, glu_w.astype(F32)) + glu_b.astype(F32))


def even_mixer(h, w_in, w_out, wa2, ba, gla_g, lb_f, lb_b, hgrn_g):
    p = jnp.einsum('bsd,de->bse', h, w_in)
    gq, gk, gv, gr, glf, glb, hq, hzf, hzb, hi, hg = _split(p, EVEN_COLS)
    a = gla_mixer(gq, gk, gv, gr, glf, glb, wa2, ba, gla_g)
    bm = hgrn2_mixer(hq, hzf, hzb, hi, hg, lb_f, lb_b, hgrn_g)
    y = jnp.concatenate([a, bm], axis=-1).astype(h.dtype)
    return jnp.einsum('bse,ed->bsd', y, w_out)


def odd_mixer(h, w_in, w_out, ret_g, lam_re, lam_im, log_dt, b_re, b_im, c_re, c_im, d_skip, glu_w, glu_b):
    p = jnp.einsum('bsd,de->bse', h, w_in)
    rq, rk, rv, rg, su = _split(p, ODD_COLS)
    c = retention_mixer(rq, rk, rv, rg, ret_g)
    dm = s5_mixer(su, lam_re, lam_im, log_dt, b_re, b_im, c_re, c_im, d_skip, glu_w, glu_b)
    y = jnp.concatenate([c, dm], axis=-1).astype(h.dtype)
    return jnp.einsum('bse,ed->bsd', y, w_out)


def conv_ffn(h, w_up, conv_w, conv_b, w_down):
    u = jnp.einsum('bsd,df->bsf', h, w_up)
    s = u.shape[1]
    pad = CONV_WIDTH // 2
    up = jnp.pad(u, ((0, 0), (pad, pad), (0, 0)))
    c = conv_b + up[:, 0:s] * conv_w[0]
    for t in range(1, CONV_WIDTH):
        c = c + up[:, t:t + s] * conv_w[t]
    a, v = jnp.split(c, 2, axis=-1)
    return jnp.einsum('bsf,fd->bsd', jax.nn.silu(a) * v, w_down)


def _fwd_setup_inputs(seed: int = 0) -> dict:
    key = jax.random.key(seed)
    keys = iter(jax.random.split(key, 40))

    def nrm(shape, scale=1.0):
        return scale * jax.random.normal(next(keys), shape, F32)

    def gain(shape):
        return 1.0 + 0.01 * nrm(shape)

    gla_hk = GLA_HEADS * GLA_DK
    hgrn_hk = HGRN_HEADS * HGRN_DK
    x = nrm((BATCH, SEQ, D_MODEL))
    mix_norm_g = gain((DEPTH, D_MODEL))
    ffn_norm_g = gain((DEPTH, D_MODEL))
    final_norm_g = gain((D_MODEL,))
    w_in_even = nrm((N_EVEN, D_MODEL, EVEN_IN), D_MODEL ** -0.5)
    w_out_even = nrm((N_EVEN, EVEN_MIX, D_MODEL), EVEN_MIX ** -0.5)
    gla_wa2 = nrm((N_EVEN, 2, GLA_RANK, gla_hk), GLA_RANK ** -0.5)
    gla_ba = nrm((N_EVEN, 2, gla_hk), 0.1)
    gla_norm_g = gain((N_EVEN, GLA_HEADS * GLA_DV))
    hgrn_lb_logits = nrm((2, N_EVEN, hgrn_hk), 0.1)
    hgrn_norm_g = gain((N_EVEN, HGRN_HEADS * HGRN_DV))
    w_in_odd = nrm((N_ODD, D_MODEL, ODD_IN), D_MODEL ** -0.5)
    w_out_odd = nrm((N_ODD, ODD_MIX, D_MODEL), ODD_MIX ** -0.5)
    ret_norm_g = gain((N_ODD, RET_HEADS * RET_DV))
    s5_lam_re = -0.5 + 0.01 * nrm((N_ODD, 2, S5_GROUPS, S5_STATE))
    s5_lam_im = jnp.pi * jnp.arange(S5_STATE, dtype=F32) + 0.01 * nrm((N_ODD, 2, S5_GROUPS, S5_STATE))
    s5_log_dt = jax.random.uniform(next(keys), (N_ODD, 2, S5_GROUPS), F32, math.log(1e-3), math.log(1e-1))
    s5_b_re = nrm((N_ODD, S5_GROUPS, S5_STATE, S5_GROUP_CH), (2.0 * S5_GROUP_CH) ** -0.5)
    s5_b_im = nrm((N_ODD, S5_GROUPS, S5_STATE, S5_GROUP_CH), (2.0 * S5_GROUP_CH) ** -0.5)
    s5_c_re = nrm((N_ODD, S5_GROUPS, S5_GROUP_CH, S5_STATE), S5_STATE ** -0.5)
    s5_c_im = nrm((N_ODD, S5_GROUPS, S5_GROUP_CH, S5_STATE), S5_STATE ** -0.5)
    s5_d = nrm((N_ODD, S5_WIDTH))
    s5_glu_w = nrm((N_ODD, S5_WIDTH, S5_WIDTH), S5_WIDTH ** -0.5)
    s5_glu_b = nrm((N_ODD, S5_WIDTH), 0.01)
    ffn_w_up = nrm((DEPTH, D_MODEL, 2 * FFN_DIM), D_MODEL ** -0.5)
    ffn_conv_w = nrm((DEPTH, CONV_WIDTH, 2 * FFN_DIM), CONV_WIDTH ** -0.5)
    ffn_conv_b = nrm((DEPTH, 2 * FFN_DIM), 0.01)
    ffn_w_down = nrm((DEPTH, FFN_DIM, D_MODEL), FFN_DIM ** -0.5)
    return {'x': x, 'mix_norm_g': mix_norm_g, 'ffn_norm_g': ffn_norm_g, 'final_norm_g': final_norm_g,
            'w_in_even': w_in_even, 'w_out_even': w_out_even, 'gla_wa2': gla_wa2, 'gla_ba': gla_ba,
            'gla_norm_g': gla_norm_g, 'hgrn_lb_logits': hgrn_lb_logits, 'hgrn_norm_g': hgrn_norm_g,
            'w_in_odd': w_in_odd, 'w_out_odd': w_out_odd, 'ret_norm_g': ret_norm_g,
            's5_lam_re': s5_lam_re, 's5_lam_im': s5_lam_im, 's5_log_dt': s5_log_dt,
            's5_b_re': s5_b_re, 's5_b_im': s5_b_im, 's5_c_re': s5_c_re, 's5_c_im': s5_c_im,
            's5_d': s5_d, 's5_glu_w': s5_glu_w, 's5_glu_b': s5_glu_b,
            'ffn_w_up': ffn_w_up, 'ffn_conv_w': ffn_conv_w, 'ffn_conv_b': ffn_conv_b, 'ffn_w_down': ffn_w_down}


def _fwd_reference(x, mix_norm_g, ffn_norm_g, final_norm_g,
              w_in_even, w_out_even, gla_wa2, gla_ba, gla_norm_g, hgrn_lb_logits, hgrn_norm_g,
              w_in_odd, w_out_odd, ret_norm_g, s5_lam_re, s5_lam_im, s5_log_dt,
              s5_b_re, s5_b_im, s5_c_re, s5_c_im, s5_d, s5_glu_w, s5_glu_b,
              ffn_w_up, ffn_conv_w, ffn_conv_b, ffn_w_down):
    lbs = hgrn_lower_bounds(hgrn_lb_logits)
    for layer in range(DEPTH):
        j = layer // 2
        h = rmsnorm(x, mix_norm_g[layer])
        if layer % 2 == 0:
            mix = even_mixer(h, w_in_even[j], w_out_even[j], gla_wa2[j], gla_ba[j], gla_norm_g[j],
                             lbs[0, j], lbs[1, j], hgrn_norm_g[j])
        else:
            mix = odd_mixer(h, w_in_odd[j], w_out_odd[j], ret_norm_g[j], s5_lam_re[j], s5_lam_im[j],
                            s5_log_dt[j], s5_b_re[j], s5_b_im[j], s5_c_re[j], s5_c_im[j], s5_d[j],
                            s5_glu_w[j], s5_glu_b[j])
        x = x + mix.astype(x.dtype)
        hf = rmsnorm(x, ffn_norm_g[layer])
        x = x + conv_ffn(hf, ffn_w_up[layer], ffn_conv_w[layer], ffn_conv_b[layer], ffn_w_down[layer]).astype(x.dtype)
    return rmsnorm(x, final_norm_g)


import jax as _jax
import jax.numpy as _jnp

TWIN_FORMAT = 'train_step'
FWD_PARAMS = ['x', 'mix_norm_g', 'ffn_norm_g', 'final_norm_g', 'w_in_even', 'w_out_even', 'gla_wa2', 'gla_ba', 'gla_norm_g', 'hgrn_lb_logits', 'hgrn_norm_g', 'w_in_odd', 'w_out_odd', 'ret_norm_g', 's5_lam_re', 's5_lam_im', 's5_log_dt', 's5_b_re', 's5_b_im', 's5_c_re', 's5_c_im', 's5_d', 's5_glu_w', 's5_glu_b', 'ffn_w_up', 'ffn_conv_w', 'ffn_conv_b', 'ffn_w_down']
TWIN_WEIGHTS = ['mix_norm_g', 'ffn_norm_g', 'final_norm_g', 'w_in_even', 'w_out_even', 'gla_wa2', 'gla_ba', 'gla_norm_g', 'hgrn_lb_logits', 'hgrn_norm_g', 'w_in_odd', 'w_out_odd', 'ret_norm_g', 's5_lam_re', 's5_lam_im', 's5_log_dt', 's5_b_re', 's5_b_im', 's5_c_re', 's5_c_im', 's5_d', 's5_glu_w', 's5_glu_b', 'ffn_w_up', 'ffn_conv_w', 'ffn_conv_b', 'ffn_w_down']
TWIN_DIFF_INPUT = 'x'
TWIN_INPUTS = ['x', 'mix_norm_g', 'ffn_norm_g', 'final_norm_g', 'w_in_even', 'w_out_even', 'gla_wa2', 'gla_ba', 'gla_norm_g', 'hgrn_lb_logits', 'hgrn_norm_g', 'w_in_odd', 'w_out_odd', 'ret_norm_g', 's5_lam_re', 's5_lam_im', 's5_log_dt', 's5_b_re', 's5_b_im', 's5_c_re', 's5_c_im', 's5_d', 's5_glu_w', 's5_glu_b', 'ffn_w_up', 'ffn_conv_w', 'ffn_conv_b', 'ffn_w_down', 'loss_target', 'm_mix_norm_g', 'm_ffn_norm_g', 'm_final_norm_g', 'm_w_in_even', 'm_w_out_even', 'm_gla_wa2', 'm_gla_ba', 'm_gla_norm_g', 'm_hgrn_lb_logits', 'm_hgrn_norm_g', 'm_w_in_odd', 'm_w_out_odd', 'm_ret_norm_g', 'm_s5_lam_re', 'm_s5_lam_im', 'm_s5_log_dt', 'm_s5_b_re', 'm_s5_b_im', 'm_s5_c_re', 'm_s5_c_im', 'm_s5_d', 'm_s5_glu_w', 'm_s5_glu_b', 'm_ffn_w_up', 'm_ffn_conv_w', 'm_ffn_conv_b', 'm_ffn_w_down', 'v_mix_norm_g', 'v_ffn_norm_g', 'v_final_norm_g', 'v_w_in_even', 'v_w_out_even', 'v_gla_wa2', 'v_gla_ba', 'v_gla_norm_g', 'v_hgrn_lb_logits', 'v_hgrn_norm_g', 'v_w_in_odd', 'v_w_out_odd', 'v_ret_norm_g', 'v_s5_lam_re', 'v_s5_lam_im', 'v_s5_log_dt', 'v_s5_b_re', 'v_s5_b_im', 'v_s5_c_re', 'v_s5_c_im', 'v_s5_d', 'v_s5_glu_w', 'v_s5_glu_b', 'v_ffn_w_up', 'v_ffn_conv_w', 'v_ffn_conv_b', 'v_ffn_w_down']
TWIN_OUTPUTS = ['loss', 'grad_x', 'grad_mix_norm_g', 'grad_ffn_norm_g', 'grad_final_norm_g', 'grad_w_in_even', 'grad_w_out_even', 'grad_gla_wa2', 'grad_gla_ba', 'grad_gla_norm_g', 'grad_hgrn_lb_logits', 'grad_hgrn_norm_g', 'grad_w_in_odd', 'grad_w_out_odd', 'grad_ret_norm_g', 'grad_s5_lam_re', 'grad_s5_lam_im', 'grad_s5_log_dt', 'grad_s5_b_re', 'grad_s5_b_im', 'grad_s5_c_re', 'grad_s5_c_im', 'grad_s5_d', 'grad_s5_glu_w', 'grad_s5_glu_b', 'grad_ffn_w_up', 'grad_ffn_conv_w', 'grad_ffn_conv_b', 'grad_ffn_w_down', 'delta_mix_norm_g', 'delta_ffn_norm_g', 'delta_final_norm_g', 'delta_w_in_even', 'delta_w_out_even', 'delta_gla_wa2', 'delta_gla_ba', 'delta_gla_norm_g', 'delta_hgrn_lb_logits', 'delta_hgrn_norm_g', 'delta_w_in_odd', 'delta_w_out_odd', 'delta_ret_norm_g', 'delta_s5_lam_re', 'delta_s5_lam_im', 'delta_s5_log_dt', 'delta_s5_b_re', 'delta_s5_b_im', 'delta_s5_c_re', 'delta_s5_c_im', 'delta_s5_d', 'delta_s5_glu_w', 'delta_s5_glu_b', 'delta_ffn_w_up', 'delta_ffn_conv_w', 'delta_ffn_conv_b', 'delta_ffn_w_down', 'new_m_mix_norm_g', 'new_m_ffn_norm_g', 'new_m_final_norm_g', 'new_m_w_in_even', 'new_m_w_out_even', 'new_m_gla_wa2', 'new_m_gla_ba', 'new_m_gla_norm_g', 'new_m_hgrn_lb_logits', 'new_m_hgrn_norm_g', 'new_m_w_in_odd', 'new_m_w_out_odd', 'new_m_ret_norm_g', 'new_m_s5_lam_re', 'new_m_s5_lam_im', 'new_m_s5_log_dt', 'new_m_s5_b_re', 'new_m_s5_b_im', 'new_m_s5_c_re', 'new_m_s5_c_im', 'new_m_s5_d', 'new_m_s5_glu_w', 'new_m_s5_glu_b', 'new_m_ffn_w_up', 'new_m_ffn_conv_w', 'new_m_ffn_conv_b', 'new_m_ffn_w_down', 'new_v_mix_norm_g', 'new_v_ffn_norm_g', 'new_v_final_norm_g', 'new_v_w_in_even', 'new_v_w_out_even', 'new_v_gla_wa2', 'new_v_gla_ba', 'new_v_gla_norm_g', 'new_v_hgrn_lb_logits', 'new_v_hgrn_norm_g', 'new_v_w_in_odd', 'new_v_w_out_odd', 'new_v_ret_norm_g', 'new_v_s5_lam_re', 'new_v_s5_lam_im', 'new_v_s5_log_dt', 'new_v_s5_b_re', 'new_v_s5_b_im', 'new_v_s5_c_re', 'new_v_s5_c_im', 'new_v_s5_d', 'new_v_s5_glu_w', 'new_v_s5_glu_b', 'new_v_ffn_w_up', 'new_v_ffn_conv_w', 'new_v_ffn_conv_b', 'new_v_ffn_w_down']
TWIN_LEAF_KINDS = {'loss': 'loss', 'grad_x': 'grad_x', 'grad_mix_norm_g': 'grad_w', 'grad_ffn_norm_g': 'grad_w', 'grad_final_norm_g': 'grad_w', 'grad_w_in_even': 'grad_w', 'grad_w_out_even': 'grad_w', 'grad_gla_wa2': 'grad_w', 'grad_gla_ba': 'grad_w', 'grad_gla_norm_g': 'grad_w', 'grad_hgrn_lb_logits': 'grad_w', 'grad_hgrn_norm_g': 'grad_w', 'grad_w_in_odd': 'grad_w', 'grad_w_out_odd': 'grad_w', 'grad_ret_norm_g': 'grad_w', 'grad_s5_lam_re': 'grad_w', 'grad_s5_lam_im': 'grad_w', 'grad_s5_log_dt': 'grad_w', 'grad_s5_b_re': 'grad_w', 'grad_s5_b_im': 'grad_w', 'grad_s5_c_re': 'grad_w', 'grad_s5_c_im': 'grad_w', 'grad_s5_d': 'grad_w', 'grad_s5_glu_w': 'grad_w', 'grad_s5_glu_b': 'grad_w', 'grad_ffn_w_up': 'grad_w', 'grad_ffn_conv_w': 'grad_w', 'grad_ffn_conv_b': 'grad_w', 'grad_ffn_w_down': 'grad_w', 'delta_mix_norm_g': 'delta_w', 'delta_ffn_norm_g': 'delta_w', 'delta_final_norm_g': 'delta_w', 'delta_w_in_even': 'delta_w', 'delta_w_out_even': 'delta_w', 'delta_gla_wa2': 'delta_w', 'delta_gla_ba': 'delta_w', 'delta_gla_norm_g': 'delta_w', 'delta_hgrn_lb_logits': 'delta_w', 'delta_hgrn_norm_g': 'delta_w', 'delta_w_in_odd': 'delta_w', 'delta_w_out_odd': 'delta_w', 'delta_ret_norm_g': 'delta_w', 'delta_s5_lam_re': 'delta_w', 'delta_s5_lam_im': 'delta_w', 'delta_s5_log_dt': 'delta_w', 'delta_s5_b_re': 'delta_w', 'delta_s5_b_im': 'delta_w', 'delta_s5_c_re': 'delta_w', 'delta_s5_c_im': 'delta_w', 'delta_s5_d': 'delta_w', 'delta_s5_glu_w': 'delta_w', 'delta_s5_glu_b': 'delta_w', 'delta_ffn_w_up': 'delta_w', 'delta_ffn_conv_w': 'delta_w', 'delta_ffn_conv_b': 'delta_w', 'delta_ffn_w_down': 'delta_w', 'new_m_mix_norm_g': 'new_m', 'new_m_ffn_norm_g': 'new_m', 'new_m_final_norm_g': 'new_m', 'new_m_w_in_even': 'new_m', 'new_m_w_out_even': 'new_m', 'new_m_gla_wa2': 'new_m', 'new_m_gla_ba': 'new_m', 'new_m_gla_norm_g': 'new_m', 'new_m_hgrn_lb_logits': 'new_m', 'new_m_hgrn_norm_g': 'new_m', 'new_m_w_in_odd': 'new_m', 'new_m_w_out_odd': 'new_m', 'new_m_ret_norm_g': 'new_m', 'new_m_s5_lam_re': 'new_m', 'new_m_s5_lam_im': 'new_m', 'new_m_s5_log_dt': 'new_m', 'new_m_s5_b_re': 'new_m', 'new_m_s5_b_im': 'new_m', 'new_m_s5_c_re': 'new_m', 'new_m_s5_c_im': 'new_m', 'new_m_s5_d': 'new_m', 'new_m_s5_glu_w': 'new_m', 'new_m_s5_glu_b': 'new_m', 'new_m_ffn_w_up': 'new_m', 'new_m_ffn_conv_w': 'new_m', 'new_m_ffn_conv_b': 'new_m', 'new_m_ffn_w_down': 'new_m', 'new_v_mix_norm_g': 'new_v', 'new_v_ffn_norm_g': 'new_v', 'new_v_final_norm_g': 'new_v', 'new_v_w_in_even': 'new_v', 'new_v_w_out_even': 'new_v', 'new_v_gla_wa2': 'new_v', 'new_v_gla_ba': 'new_v', 'new_v_gla_norm_g': 'new_v', 'new_v_hgrn_lb_logits': 'new_v', 'new_v_hgrn_norm_g': 'new_v', 'new_v_w_in_odd': 'new_v', 'new_v_w_out_odd': 'new_v', 'new_v_ret_norm_g': 'new_v', 'new_v_s5_lam_re': 'new_v', 'new_v_s5_lam_im': 'new_v', 'new_v_s5_log_dt': 'new_v', 'new_v_s5_b_re': 'new_v', 'new_v_s5_b_im': 'new_v', 'new_v_s5_c_re': 'new_v', 'new_v_s5_c_im': 'new_v', 'new_v_s5_d': 'new_v', 'new_v_s5_glu_w': 'new_v', 'new_v_s5_glu_b': 'new_v', 'new_v_ffn_w_up': 'new_v', 'new_v_ffn_conv_w': 'new_v', 'new_v_ffn_conv_b': 'new_v', 'new_v_ffn_w_down': 'new_v'}


def _forward(args):
    return _fwd_reference(*[args[k] for k in FWD_PARAMS])


def _output_shape():
    out = _jax.eval_shape(lambda: _forward(_fwd_setup_inputs(0)))
    return out.shape, out.dtype

N_MICROBATCH = 1
ADAM_LR = 0.001
ADAM_B1 = 0.9
ADAM_B2 = 0.999
ADAM_EPS = 1e-08
ADAM_WD = 0.01
ADAM_STEP = 10
PER_EXAMPLE_BATCH_AXIS = {'x': 0, 'loss_target': 0}
SHARED_INPUTS = []
_WEIGHT_DTYPES = {'mix_norm_g': _jnp.float32, 'ffn_norm_g': _jnp.float32, 'final_norm_g': _jnp.float32, 'w_in_even': _jnp.float32, 'w_out_even': _jnp.float32, 'gla_wa2': _jnp.float32, 'gla_ba': _jnp.float32, 'gla_norm_g': _jnp.float32, 'hgrn_lb_logits': _jnp.float32, 'hgrn_norm_g': _jnp.float32, 'w_in_odd': _jnp.float32, 'w_out_odd': _jnp.float32, 'ret_norm_g': _jnp.float32, 's5_lam_re': _jnp.float32, 's5_lam_im': _jnp.float32, 's5_log_dt': _jnp.float32, 's5_b_re': _jnp.float32, 's5_b_im': _jnp.float32, 's5_c_re': _jnp.float32, 's5_c_im': _jnp.float32, 's5_d': _jnp.float32, 's5_glu_w': _jnp.float32, 's5_glu_b': _jnp.float32, 'ffn_w_up': _jnp.float32, 'ffn_conv_w': _jnp.float32, 'ffn_conv_b': _jnp.float32, 'ffn_w_down': _jnp.float32}
MOMENT_SCALE = {'mix_norm_g': 2.177410e-01, 'ffn_norm_g': 1.409519e-01, 'final_norm_g': 3.199188e+01, 'w_in_even': 1.355208e-01, 'w_out_even': 1.384015e-01, 'gla_wa2': 1.562090e-02, 'gla_ba': 6.511842e-02, 'gla_norm_g': 1.384869e-01, 'hgrn_lb_logits': 9.326948e-03, 'hgrn_norm_g': 1.417372e-01, 'w_in_odd': 9.642059e-02, 'w_out_odd': 8.267141e-02, 'ret_norm_g': 9.934318e-02, 's5_lam_re': 4.178582e-03, 's5_lam_im': 4.367690e-03, 's5_log_dt': 3.845477e+00, 's5_b_re': 4.069463e-03, 's5_b_im': 3.800534e-03, 's5_c_re': 5.361924e-03, 's5_c_im': 5.384083e-03, 's5_d': 5.961771e-02, 's5_glu_w': 1.627718e-02, 's5_glu_b': 2.447964e-02, 'ffn_w_up': 6.046525e-02, 'ffn_conv_w': 6.056077e-02, 'ffn_conv_b': 6.004648e-02, 'ffn_w_down': 9.862974e-02}


def _to_microbatches(a, axis):
    t = _jnp.moveaxis(a, axis, 0)
    t = t.reshape((N_MICROBATCH, t.shape[0] // N_MICROBATCH) + t.shape[1:])
    return _jnp.moveaxis(t, 1, axis + 1)


def setup_inputs(seed: int = 0) -> dict:
    inp = _fwd_setup_inputs(seed)
    key = _jax.random.fold_in(_jax.random.key(seed), 7919)
    shape, _ = _output_shape()
    out = dict(inp)
    out["loss_target"] = _jax.random.normal(_jax.random.fold_in(key, 0), shape, _jnp.float32)
    for i, name in enumerate(TWIN_WEIGHTS):
        w = inp[name].astype(_jnp.float32)
        if MOMENT_SCALE is None:
            s = _jnp.sqrt(_jnp.mean(_jnp.square(w)) + 1e-30)
        else:
            s = MOMENT_SCALE[name]
        km, kv = _jax.random.split(_jax.random.fold_in(key, i + 1))
        out[name] = w
        out["m_" + name] = s * _jax.random.normal(km, w.shape, _jnp.float32)
        out["v_" + name] = (s * s) * _jax.random.uniform(kv, w.shape, _jnp.float32, 0.5, 1.5)
    if N_MICROBATCH > 1:
        for name, axis in PER_EXAMPLE_BATCH_AXIS.items():
            out[name] = _to_microbatches(out[name], axis)
    return {'x': out['x'], 'mix_norm_g': out['mix_norm_g'], 'ffn_norm_g': out['ffn_norm_g'], 'final_norm_g': out['final_norm_g'], 'w_in_even': out['w_in_even'], 'w_out_even': out['w_out_even'], 'gla_wa2': out['gla_wa2'], 'gla_ba': out['gla_ba'], 'gla_norm_g': out['gla_norm_g'], 'hgrn_lb_logits': out['hgrn_lb_logits'], 'hgrn_norm_g': out['hgrn_norm_g'], 'w_in_odd': out['w_in_odd'], 'w_out_odd': out['w_out_odd'], 'ret_norm_g': out['ret_norm_g'], 's5_lam_re': out['s5_lam_re'], 's5_lam_im': out['s5_lam_im'], 's5_log_dt': out['s5_log_dt'], 's5_b_re': out['s5_b_re'], 's5_b_im': out['s5_b_im'], 's5_c_re': out['s5_c_re'], 's5_c_im': out['s5_c_im'], 's5_d': out['s5_d'], 's5_glu_w': out['s5_glu_w'], 's5_glu_b': out['s5_glu_b'], 'ffn_w_up': out['ffn_w_up'], 'ffn_conv_w': out['ffn_conv_w'], 'ffn_conv_b': out['ffn_conv_b'], 'ffn_w_down': out['ffn_w_down'], 'loss_target': out['loss_target'], 'm_mix_norm_g': out['m_mix_norm_g'], 'm_ffn_norm_g': out['m_ffn_norm_g'], 'm_final_norm_g': out['m_final_norm_g'], 'm_w_in_even': out['m_w_in_even'], 'm_w_out_even': out['m_w_out_even'], 'm_gla_wa2': out['m_gla_wa2'], 'm_gla_ba': out['m_gla_ba'], 'm_gla_norm_g': out['m_gla_norm_g'], 'm_hgrn_lb_logits': out['m_hgrn_lb_logits'], 'm_hgrn_norm_g': out['m_hgrn_norm_g'], 'm_w_in_odd': out['m_w_in_odd'], 'm_w_out_odd': out['m_w_out_odd'], 'm_ret_norm_g': out['m_ret_norm_g'], 'm_s5_lam_re': out['m_s5_lam_re'], 'm_s5_lam_im': out['m_s5_lam_im'], 'm_s5_log_dt': out['m_s5_log_dt'], 'm_s5_b_re': out['m_s5_b_re'], 'm_s5_b_im': out['m_s5_b_im'], 'm_s5_c_re': out['m_s5_c_re'], 'm_s5_c_im': out['m_s5_c_im'], 'm_s5_d': out['m_s5_d'], 'm_s5_glu_w': out['m_s5_glu_w'], 'm_s5_glu_b': out['m_s5_glu_b'], 'm_ffn_w_up': out['m_ffn_w_up'], 'm_ffn_conv_w': out['m_ffn_conv_w'], 'm_ffn_conv_b': out['m_ffn_conv_b'], 'm_ffn_w_down': out['m_ffn_w_down'], 'v_mix_norm_g': out['v_mix_norm_g'], 'v_ffn_norm_g': out['v_ffn_norm_g'], 'v_final_norm_g': out['v_final_norm_g'], 'v_w_in_even': out['v_w_in_even'], 'v_w_out_even': out['v_w_out_even'], 'v_gla_wa2': out['v_gla_wa2'], 'v_gla_ba': out['v_gla_ba'], 'v_gla_norm_g': out['v_gla_norm_g'], 'v_hgrn_lb_logits': out['v_hgrn_lb_logits'], 'v_hgrn_norm_g': out['v_hgrn_norm_g'], 'v_w_in_odd': out['v_w_in_odd'], 'v_w_out_odd': out['v_w_out_odd'], 'v_ret_norm_g': out['v_ret_norm_g'], 'v_s5_lam_re': out['v_s5_lam_re'], 'v_s5_lam_im': out['v_s5_lam_im'], 'v_s5_log_dt': out['v_s5_log_dt'], 'v_s5_b_re': out['v_s5_b_re'], 'v_s5_b_im': out['v_s5_b_im'], 'v_s5_c_re': out['v_s5_c_re'], 'v_s5_c_im': out['v_s5_c_im'], 'v_s5_d': out['v_s5_d'], 'v_s5_glu_w': out['v_s5_glu_w'], 'v_s5_glu_b': out['v_s5_glu_b'], 'v_ffn_w_up': out['v_ffn_w_up'], 'v_ffn_conv_w': out['v_ffn_conv_w'], 'v_ffn_conv_b': out['v_ffn_conv_b'], 'v_ffn_w_down': out['v_ffn_w_down']}


def _loss(weights, diff, rest, loss_target):
    with _jax.named_scope("forward"):
        args = {**rest, TWIN_DIFF_INPUT: diff, **{k: w.astype(_WEIGHT_DTYPES[k]) for k, w in weights.items()}}
        y = _forward(args)
    with _jax.named_scope("loss_head"):
        err = _jnp.square(y.astype(_jnp.float32) - loss_target)
        return 0.5 * _jnp.sum(_jnp.mean(err, axis=-1)) if err.ndim else 0.5 * err


def _adamw(w, g, m, v):
    m = ADAM_B1 * m + (1.0 - ADAM_B1) * g
    v = ADAM_B2 * v + (1.0 - ADAM_B2) * _jnp.square(g)
    m_hat = m / (1.0 - ADAM_B1 ** ADAM_STEP)
    v_hat = v / (1.0 - ADAM_B2 ** ADAM_STEP)
    delta = -ADAM_LR * (m_hat / (_jnp.sqrt(v_hat) + ADAM_EPS) + ADAM_WD * w)
    return delta, m, v


def reference(x, mix_norm_g, ffn_norm_g, final_norm_g, w_in_even, w_out_even, gla_wa2, gla_ba, gla_norm_g, hgrn_lb_logits, hgrn_norm_g, w_in_odd, w_out_odd, ret_norm_g, s5_lam_re, s5_lam_im, s5_log_dt, s5_b_re, s5_b_im, s5_c_re, s5_c_im, s5_d, s5_glu_w, s5_glu_b, ffn_w_up, ffn_conv_w, ffn_conv_b, ffn_w_down, loss_target, m_mix_norm_g, m_ffn_norm_g, m_final_norm_g, m_w_in_even, m_w_out_even, m_gla_wa2, m_gla_ba, m_gla_norm_g, m_hgrn_lb_logits, m_hgrn_norm_g, m_w_in_odd, m_w_out_odd, m_ret_norm_g, m_s5_lam_re, m_s5_lam_im, m_s5_log_dt, m_s5_b_re, m_s5_b_im, m_s5_c_re, m_s5_c_im, m_s5_d, m_s5_glu_w, m_s5_glu_b, m_ffn_w_up, m_ffn_conv_w, m_ffn_conv_b, m_ffn_w_down, v_mix_norm_g, v_ffn_norm_g, v_final_norm_g, v_w_in_even, v_w_out_even, v_gla_wa2, v_gla_ba, v_gla_norm_g, v_hgrn_lb_logits, v_hgrn_norm_g, v_w_in_odd, v_w_out_odd, v_ret_norm_g, v_s5_lam_re, v_s5_lam_im, v_s5_log_dt, v_s5_b_re, v_s5_b_im, v_s5_c_re, v_s5_c_im, v_s5_d, v_s5_glu_w, v_s5_glu_b, v_ffn_w_up, v_ffn_conv_w, v_ffn_conv_b, v_ffn_w_down):
    given = dict(x=x, mix_norm_g=mix_norm_g, ffn_norm_g=ffn_norm_g, final_norm_g=final_norm_g, w_in_even=w_in_even, w_out_even=w_out_even, gla_wa2=gla_wa2, gla_ba=gla_ba, gla_norm_g=gla_norm_g, hgrn_lb_logits=hgrn_lb_logits, hgrn_norm_g=hgrn_norm_g, w_in_odd=w_in_odd, w_out_odd=w_out_odd, ret_norm_g=ret_norm_g, s5_lam_re=s5_lam_re, s5_lam_im=s5_lam_im, s5_log_dt=s5_log_dt, s5_b_re=s5_b_re, s5_b_im=s5_b_im, s5_c_re=s5_c_re, s5_c_im=s5_c_im, s5_d=s5_d, s5_glu_w=s5_glu_w, s5_glu_b=s5_glu_b, ffn_w_up=ffn_w_up, ffn_conv_w=ffn_conv_w, ffn_conv_b=ffn_conv_b, ffn_w_down=ffn_w_down, loss_target=loss_target, m_mix_norm_g=m_mix_norm_g, m_ffn_norm_g=m_ffn_norm_g, m_final_norm_g=m_final_norm_g, m_w_in_even=m_w_in_even, m_w_out_even=m_w_out_even, m_gla_wa2=m_gla_wa2, m_gla_ba=m_gla_ba, m_gla_norm_g=m_gla_norm_g, m_hgrn_lb_logits=m_hgrn_lb_logits, m_hgrn_norm_g=m_hgrn_norm_g, m_w_in_odd=m_w_in_odd, m_w_out_odd=m_w_out_odd, m_ret_norm_g=m_ret_norm_g, m_s5_lam_re=m_s5_lam_re, m_s5_lam_im=m_s5_lam_im, m_s5_log_dt=m_s5_log_dt, m_s5_b_re=m_s5_b_re, m_s5_b_im=m_s5_b_im, m_s5_c_re=m_s5_c_re, m_s5_c_im=m_s5_c_im, m_s5_d=m_s5_d, m_s5_glu_w=m_s5_glu_w, m_s5_glu_b=m_s5_glu_b, m_ffn_w_up=m_ffn_w_up, m_ffn_conv_w=m_ffn_conv_w, m_ffn_conv_b=m_ffn_conv_b, m_ffn_w_down=m_ffn_w_down, v_mix_norm_g=v_mix_norm_g, v_ffn_norm_g=v_ffn_norm_g, v_final_norm_g=v_final_norm_g, v_w_in_even=v_w_in_even, v_w_out_even=v_w_out_even, v_gla_wa2=v_gla_wa2, v_gla_ba=v_gla_ba, v_gla_norm_g=v_gla_norm_g, v_hgrn_lb_logits=v_hgrn_lb_logits, v_hgrn_norm_g=v_hgrn_norm_g, v_w_in_odd=v_w_in_odd, v_w_out_odd=v_w_out_odd, v_ret_norm_g=v_ret_norm_g, v_s5_lam_re=v_s5_lam_re, v_s5_lam_im=v_s5_lam_im, v_s5_log_dt=v_s5_log_dt, v_s5_b_re=v_s5_b_re, v_s5_b_im=v_s5_b_im, v_s5_c_re=v_s5_c_re, v_s5_c_im=v_s5_c_im, v_s5_d=v_s5_d, v_s5_glu_w=v_s5_glu_w, v_s5_glu_b=v_s5_glu_b, v_ffn_w_up=v_ffn_w_up, v_ffn_conv_w=v_ffn_conv_w, v_ffn_conv_b=v_ffn_conv_b, v_ffn_w_down=v_ffn_w_down)
    weights = {n: given[n] for n in TWIN_WEIGHTS}
    shared = {n: given[n] for n in SHARED_INPUTS}
    per_example = {n: given[n] for n in ['x']}
    grad_fn = _jax.value_and_grad(_loss, argnums=(0, 1))

    def one_microbatch(ex, loss_target):
        ex = dict(ex)
        diff = ex.pop(TWIN_DIFF_INPUT)
        return grad_fn(weights, diff, {**shared, **ex}, loss_target)

    if N_MICROBATCH == 1:
        loss, (grad_w, grad_x) = one_microbatch(per_example, given["loss_target"])
    else:
        def body(carry, xs):
            loss_sum, grad_sum = carry
            l_k, (gw_k, gx_k) = one_microbatch(xs[0], xs[1])
            with _jax.named_scope("update"):
                return (loss_sum + l_k, _jax.tree.map(_jnp.add, grad_sum, gw_k)), gx_k

        init = (_jnp.zeros((), _jnp.float32), _jax.tree.map(_jnp.zeros_like, weights))
        (loss, grad_w), grad_x = _jax.lax.scan(body, init, (per_example, given["loss_target"]))
    with _jax.named_scope("update"):
        delta_w, new_m, new_v = {}, {}, {}
        for n in TWIN_WEIGHTS:
            delta_w[n], new_m[n], new_v[n] = _adamw(weights[n], grad_w[n], given["m_" + n], given["v_" + n])
    return (loss, grad_x, *[grad_w[n] for n in TWIN_WEIGHTS], *[delta_w[n] for n in TWIN_WEIGHTS],
            *[new_m[n] for n in TWIN_WEIGHTS], *[new_v[n] for n in TWIN_WEIGHTS])
```

```python
import functools
import math

import numpy as np
import jax
import jax.numpy as jnp
from jax import lax
from jax.experimental import pallas as pl
from jax.experimental.pallas import tpu as pltpu

F32 = jnp.float32
BF16 = jnp.bfloat16
HIGHEST = lax.Precision.HIGHEST
MESH = pl.DeviceIdType.MESH

D_MODEL = 1024
DEPTH = 4
N_EVEN = 2
N_ODD = 2
GLA_HEADS, GLA_DK, GLA_DV, GLA_RANK, GLA_GATE_NORM = 4, 64, 128, 16, 16.0
HGRN_HEADS, HGRN_DK, HGRN_DV, HGRN_MIN_F = 4, 64, 128, 1e-20
RET_HEADS, RET_DK, RET_DV = 4, 128, 192
ROPE_BASE = 10000.0
S5_WIDTH, S5_GROUP_CH, S5_GROUPS, S5_STATE = 256, 16, 16, 64
S5_N = S5_GROUPS * S5_STATE
FFN_DIM = 2816
EPS = 1e-6
EVEN_IN = 3360
ODD_IN = 2816
ADAM_LR, ADAM_B1, ADAM_B2, ADAM_EPS, ADAM_WD, ADAM_STEP = 0.001, 0.9, 0.999, 1e-08, 0.01, 10

N_DEV = 8
VMEM_LIMIT_BYTES = 56 * 1024 * 1024
ROW_TILE = 256
SCAN_CHUNK = 64
S5_SEGMENTS = 8
LANES = 128


def _params(*sem):
    return pltpu.CompilerParams(dimension_semantics=sem, vmem_limit_bytes=VMEM_LIMIT_BYTES)


def _divisor_tile(n, cap):
    best = None
    for t in range(LANES, min(n, cap) + 1, LANES):
        if n % t == 0:
            best = t
    return best if best is not None else n


def _mm_nn(name, x, w):
    M, K = x.shape
    N = w.shape[1]
    tn = _divisor_tile(N, 2048) if K * N * 2 > 8 * 2**20 else N
    tm = 256 if tn * 4 * 512 > 6 * 2**20 else 512
    assert M % tm == 0 and N % tn == 0

    def body(x_ref, w_ref, o_ref):
        o_ref[...] = jnp.dot(x_ref[...].astype(BF16), w_ref[...], preferred_element_type=F32)

    return pl.pallas_call(
        body, name=name, grid=(N // tn, M // tm),
        in_specs=[pl.BlockSpec((tm, K), lambda j, i: (i, 0)), pl.BlockSpec((K, tn), lambda j, i: (0, j))],
        out_specs=pl.BlockSpec((tm, tn), lambda j, i: (i, j)),
        out_shape=jax.ShapeDtypeStruct((M, N), F32),
        compiler_params=_params("parallel", "parallel"),
    )(x, w)


def _mm_nt(name, dy, w):
    M, N = dy.shape
    K = w.shape[0]
    tk = _divisor_tile(K, 1024) if K * N * 2 > 8 * 2**20 else K
    tm = 256 if N >= 4096 else 512
    assert M % tm == 0 and K % tk == 0

    def body(dy_ref, w_ref, o_ref):
        o_ref[...] = lax.dot_general(dy_ref[...].astype(BF16), w_ref[...], (((1,), (1,)), ((), ())),
                                     preferred_element_type=F32)

    return pl.pallas_call(
        body, name=name, grid=(K // tk, M // tm),
        in_specs=[pl.BlockSpec((tm, N), lambda j, i: (i, 0)), pl.BlockSpec((tk, N), lambda j, i: (j, 0))],
        out_specs=pl.BlockSpec((tm, tk), lambda j, i: (i, j)),
        out_shape=jax.ShapeDtypeStruct((M, K), F32),
        compiler_params=_params("parallel", "parallel"),
    )(dy, w)


def _mm_tn(name, x, dy):
    M, K = x.shape
    N = dy.shape[1]
    tk = _divisor_tile(K, 512)
    tn = _divisor_tile(N, 2048)
    tm = 512
    assert M % tm == 0 and K % tk == 0 and N % tn == 0

    def body(x_ref, dy_ref, o_ref):
        @pl.when(pl.program_id(2) == 0)
        def _():
            o_ref[...] = jnp.zeros_like(o_ref)

        o_ref[...] += lax.dot_general(x_ref[...].astype(BF16), dy_ref[...].astype(BF16), (((0,), (0,)), ((), ())),
                                      preferred_element_type=F32)

    return pl.pallas_call(
        body, name=name, grid=(K // tk, N // tn, M // tm),
        in_specs=[pl.BlockSpec((tm, tk), lambda a, b, m: (m, a)), pl.BlockSpec((tm, tn), lambda a, b, m: (m, b))],
        out_specs=pl.BlockSpec((tk, tn), lambda a, b, m: (a, b)),
        out_shape=jax.ShapeDtypeStruct((K, N), F32),
        compiler_params=_params("parallel", "parallel", "arbitrary"),
    )(x, dy)


def make_mm(name):
    @jax.custom_vjp
    def mm(x, w16, w_carrier):
        return _mm_nn(name + "_fwd", x, w16)

    def fwd(x, w16, w_carrier):
        return _mm_nn(name + "_fwd", x, w16), (x, w16)

    def bwd(res, dy):
        x, w16 = res
        return _mm_nt(name + "_dx", dy, w16), jnp.zeros_like(w16), _mm_tn(name + "_dw", x, dy)

    mm.defvjp(fwd, bwd)
    return mm


def _row_specs(rows, params, consts, tile):
    specs = [pl.BlockSpec((tile, r.shape[1]), lambda i: (i, 0)) for r in rows]
    specs += [pl.BlockSpec(p.shape, lambda i: (0, 0)) for p in params]
    specs += [pl.BlockSpec((tile, c.shape[1]), lambda i, n=c.shape[0] // tile: (i % n, 0)) for c in consts]
    return specs


def _row_fwd(name, f, out_widths, tile, rows, params, consts):
    T = rows[0].shape[0]
    nr, npar, ncon = len(rows), len(params), len(consts)

    def body(*refs):
        r = tuple(x[...] for x in refs[:nr])
        p = tuple(x[...] for x in refs[nr:nr + npar])
        c = tuple(x[...] for x in refs[nr + npar:nr + npar + ncon])
        outs = f(r, p, c)
        for o_ref, o in zip(refs[nr + npar + ncon:], outs, strict=True):
            o_ref[...] = o

    return pl.pallas_call(
        body, name=name + "_fwd", grid=(T // tile,),
        in_specs=_row_specs(rows, params, consts, tile),
        out_specs=[pl.BlockSpec((tile, w), lambda i: (i, 0)) for w in out_widths],
        out_shape=[jax.ShapeDtypeStruct((T, w), F32) for w in out_widths],
        compiler_params=_params("parallel"),
    )(*rows, *params, *consts)


def _row_bwd(name, f, out_widths, tile, rows, params, consts, gouts):
    T = rows[0].shape[0]
    nr, npar, ncon, nout = len(rows), len(params), len(consts), len(out_widths)

    def body(*refs):
        r = tuple(x[...] for x in refs[:nr])
        p = tuple(x[...] for x in refs[nr:nr + npar])
        c = tuple(x[...] for x in refs[nr + npar:nr + npar + ncon])
        k = nr + npar + ncon
        g = tuple(x[...] for x in refs[k:k + nout])
        dr_refs = refs[k + nout:k + nout + nr]
        dp_refs = refs[k + nout + nr:]
        _, vjp = jax.vjp(lambda r_, p_: tuple(f(r_, p_, c)), r, p)
        dr, dp = vjp(g)
        for ref, val in zip(dr_refs, dr, strict=True):
            ref[...] = val
        if npar:
            @pl.when(pl.program_id(0) == 0)
            def _():
                for ref in dp_refs:
                    ref[...] = jnp.zeros_like(ref)

            for ref, val in zip(dp_refs, dp, strict=True):
                ref[...] += val

    outs = pl.pallas_call(
        body, name=name + "_bwd", grid=(T // tile,),
        in_specs=_row_specs(rows, params, consts, tile) + [pl.BlockSpec((tile, w), lambda i: (i, 0)) for w in out_widths],
        out_specs=[pl.BlockSpec((tile, r.shape[1]), lambda i: (i, 0)) for r in rows]
        + [pl.BlockSpec(p.shape, lambda i: (0, 0)) for p in params],
        out_shape=[jax.ShapeDtypeStruct(r.shape, F32) for r in rows] + [jax.ShapeDtypeStruct(p.shape, F32) for p in params],
        compiler_params=_params("arbitrary"),
    )(*rows, *params, *consts, *gouts)
    return tuple(outs[:nr]), tuple(outs[nr:])


def make_rowop(name, f, out_widths, tile=ROW_TILE):
    @jax.custom_vjp
    def op(rows, params, consts):
        return tuple(_row_fwd(name, f, out_widths, tile, rows, params, consts))

    def fwd(rows, params, consts):
        return op(rows, params, consts), (rows, params, consts)

    def bwd(res, g):
        rows, params, consts = res
        dr, dp = _row_bwd(name, f, out_widths, tile, rows, params, consts, tuple(g))
        return dr, dp, tuple(jnp.zeros_like(c) for c in consts)

    op.defvjp(fwd, bwd)
    return op


def _rms(x, g):
    return x * lax.rsqrt(jnp.mean(x * x, axis=-1, keepdims=True) + EPS) * g


def _silu(x):
    return x * jax.nn.sigmoid(x)


def _bdot(a, b):
    return jnp.dot(a.astype(BF16), b.astype(BF16), preferred_element_type=F32)


def norm_f(rows, params, consts):
    return (_rms(rows[0], params[0]),)


def addnorm_f(rows, params, consts):
    x = rows[0] + rows[1]
    return x, _rms(x, params[0])


EVEN_GLA_END = 1568


def even_prep_f(rows, params, consts):
    (p,) = rows
    wa2f, wa2b, baf, bab, lbf, lbb = params
    gq = p[:, 0:256]
    gk = p[:, 256:512] * (GLA_DK ** -0.5)
    gv = p[:, 512:1024]
    gr = p[:, 1024:1536]
    glaf = jax.nn.log_sigmoid(_bdot(p[:, 1536:1552], wa2f) + baf) / GLA_GATE_NORM
    glab = jax.nn.log_sigmoid(_bdot(p[:, 1552:1568], wa2b) + bab) / GLA_GATE_NORM
    o = EVEN_GLA_END
    hq = _silu(p[:, o:o + 256])

    def gate(z, lb):
        f = lb + (1.0 - lb) * jax.nn.sigmoid(z)
        return jnp.log(jnp.maximum(f, HGRN_MIN_F)), (1.0 - lb) * jax.nn.sigmoid(-z)

    hlaf, hkf = gate(p[:, o + 256:o + 512], lbf)
    hlab, hkb = gate(p[:, o + 512:o + 768], lbb)
    hv = p[:, o + 768:o + 1280]
    hg = p[:, o + 1280:o + 1792]
    return gq, gk, gv, glaf, glab, gr, hq, hkf, hkb, hv, hlaf, hlab, hg


EVEN_PREP_WIDTHS = (256, 256, 512, 256, 256, 512, 256, 256, 256, 512, 256, 256, 512)


def _head_rms(o, g, heads, d):
    parts = []
    for h in range(heads):
        seg = o[:, h * d:(h + 1) * d]
        parts.append(seg * lax.rsqrt(jnp.mean(seg * seg, axis=-1, keepdims=True) + EPS))
    return jnp.concatenate(parts, axis=1) * g


def even_post_f(rows, params, consts):
    of, ob, hof, hob, gr, hg = rows
    a = _head_rms(of + ob, params[0], GLA_HEADS, GLA_DV) * _silu(gr)
    b = _head_rms(hof + hob, params[1], HGRN_HEADS, HGRN_DV) * _silu(hg)
    return (jnp.concatenate([a, b], axis=1),)


@jax.custom_vjp
def _roll_half(x):
    return pltpu.roll(x, RET_DK // 2, 1)


_roll_half.defvjp(lambda x: (_roll_half(x), None), lambda _, g: (_roll_half(g),))


def odd_prep_f(rows, params, consts):
    (p,) = rows
    cosf, sinf = consts

    def rot(t):
        parts = []
        for h in range(RET_HEADS):
            th = t[:, h * RET_DK:(h + 1) * RET_DK]
            parts.append(th * cosf + _roll_half(th) * sinf)
        return jnp.concatenate(parts, axis=1)

    rq = rot(p[:, 0:512])
    rk = rot(p[:, 512:1024]) * (RET_DK ** -0.5)
    return rq, rk, p[:, 1024:1792], p[:, 1792:2560], p[:, 2560:2816]


ODD_PREP_WIDTHS = (512, 512, 768, 768, 256)


def ret_post_f(rows, params, consts):
    of, ob, rg = rows
    o = of + ob
    parts = []
    for h in range(RET_HEADS):
        seg = o[:, h * RET_DV:(h + 1) * RET_DV]
        c = seg - jnp.mean(seg, axis=-1, keepdims=True)
        parts.append(c * lax.rsqrt(jnp.mean(c * c, axis=-1, keepdims=True) + EPS))
    return (jnp.concatenate(parts, axis=1) * params[0] * _silu(rg),)


def s5_post_f(rows, params, consts):
    hs, u = rows
    c_re, c_im, d_skip, glu_w, glu_b = params
    n = S5_N
    hr = hs[:, 0:n] + hs[:, 2 * n:3 * n]
    hi = hs[:, n:2 * n] + hs[:, 3 * n:4 * n]
    y = _bdot(hr, c_re) - _bdot(hi, c_im) + d_skip * u
    g = jax.nn.gelu(y)
    return (g * jax.nn.sigmoid(_bdot(g, glu_w) + glu_b),)


def loss_head(x, r, g, target, tile=ROW_TILE):
    T, D = x.shape

    def body(x_ref, r_ref, g_ref, t_ref, loss_ref, dx_ref, dg_ref):
        t = t_ref[...]

        def lf(xv, gv):
            e = _rms(xv, gv) - t
            row = jnp.sum(e * e, axis=-1, keepdims=True) * (0.5 / D)
            return jnp.sum(row, axis=0, keepdims=True)

        l, vjp = jax.vjp(lf, x_ref[...] + r_ref[...], g_ref[...])
        dx, dg = vjp(jnp.ones((1, 1), F32))
        dx_ref[...] = dx

        @pl.when(pl.program_id(0) == 0)
        def _():
            loss_ref[...] = jnp.zeros_like(loss_ref)
            dg_ref[...] = jnp.zeros_like(dg_ref)

        loss_ref[...] += jnp.broadcast_to(l, loss_ref.shape)
        dg_ref[...] += dg

    row = pl.BlockSpec((tile, D), lambda i: (i, 0))
    vec = pl.BlockSpec((1, D), lambda i: (0, 0))
    return pl.pallas_call(
        body, name="loss_head", grid=(T // tile,),
        in_specs=[row, row, vec, row],
        out_specs=[pl.BlockSpec((1, LANES), lambda i: (0, 0)), row, vec],
        out_shape=[jax.ShapeDtypeStruct((1, LANES), F32), jax.ShapeDtypeStruct((T, D), F32), jax.ShapeDtypeStruct((1, D), F32)],
        compiler_params=_params("arbitrary"),
    )(x, r, g, target)


SUBLANES = 8


def _halo_specs(width, tile, T):
    n8 = tile // SUBLANES
    last = T // SUBLANES - 1
    return [pl.BlockSpec((tile, width), lambda i: (i, 0)),
            pl.BlockSpec((SUBLANES, width), lambda i: (jnp.maximum(i * n8 - 1, 0), 0)),
            pl.BlockSpec((SUBLANES, width), lambda i: (jnp.minimum((i + 1) * n8, last), 0))]


def _shift_rows(x, prev_row, next_row, tile):
    row = lax.broadcasted_iota(jnp.int32, (tile, 1), 0)
    down = jnp.where(row == 0, prev_row, pltpu.roll(x, 1, 0))
    up = jnp.where(row == tile - 1, next_row, pltpu.roll(x, tile - 1, 0))
    return down, up


def _conv_fwd(name, u, cw, cb, S, tile):
    T, F2 = u.shape
    F = F2 // 2
    per_seq = S // tile

    def body(u_ref, up_ref, un_ref, cw_ref, cb_ref, c_ref, g_ref):
        pos = pl.program_id(0) % per_seq
        uv = u_ref[...]
        prev_row = jnp.where(pos == 0, 0.0, up_ref[SUBLANES - 1:SUBLANES, :])
        next_row = jnp.where(pos == per_seq - 1, 0.0, un_ref[0:1, :])
        down, up = _shift_rows(uv, prev_row, next_row, tile)
        c = cb_ref[...] + down * cw_ref[0:1, :]
        c = c + uv * cw_ref[1:2, :]
        c = c + up * cw_ref[2:3, :]
        c_ref[...] = c
        g_ref[...] = _silu(c[:, :F]) * c[:, F:]

    return pl.pallas_call(
        body, name=name + "_fwd", grid=(T // tile,),
        in_specs=_halo_specs(F2, tile, T) + [pl.BlockSpec((3, F2), lambda i: (0, 0)), pl.BlockSpec((1, F2), lambda i: (0, 0))],
        out_specs=[pl.BlockSpec((tile, F2), lambda i: (i, 0)), pl.BlockSpec((tile, F), lambda i: (i, 0))],
        out_shape=[jax.ShapeDtypeStruct((T, F2), F32), jax.ShapeDtypeStruct((T, F), F32)],
        compiler_params=_params("parallel"),
    )(u, u, u, cw, cb)


def _conv_bwd(name, u, c, dg, cw, S, tile):
    T, F2 = u.shape
    F = F2 // 2
    per_seq = S // tile

    def dact(cv, dgv):
        a, v = cv[:, :F], cv[:, F:]
        sg = jax.nn.sigmoid(a)
        return jnp.concatenate([dgv * v * (sg * (1.0 + a * (1.0 - sg))), dgv * (a * sg)], axis=1)

    def body(u_ref, up_ref, un_ref, c_ref, cp_ref, cn_ref, g_ref, gp_ref, gn_ref, cw_ref, du_ref, dw0_ref, dw1_ref, dw2_ref, db_ref):
        i = pl.program_id(0)
        pos = i % per_seq
        first, last = pos == 0, pos == per_seq - 1
        lo, hi = slice(SUBLANES - 1, SUBLANES), slice(0, 1)
        uv = u_ref[...]
        u_dn, u_up = _shift_rows(uv, jnp.where(first, 0.0, up_ref[lo, :]), jnp.where(last, 0.0, un_ref[hi, :]), tile)
        dc = dact(c_ref[...], g_ref[...])
        dc_prev = jnp.where(first, 0.0, dact(cp_ref[lo, :], gp_ref[lo, :]))
        dc_next = jnp.where(last, 0.0, dact(cn_ref[hi, :], gn_ref[hi, :]))
        dc_dn, dc_up = _shift_rows(dc, dc_prev, dc_next, tile)
        du = dc_up * cw_ref[0:1, :]
        du = du + dc * cw_ref[1:2, :]
        du_ref[...] = du + dc_dn * cw_ref[2:3, :]

        @pl.when(i == 0)
        def _():
            for ref in (dw0_ref, dw1_ref, dw2_ref, db_ref):
                ref[...] = jnp.zeros_like(ref)

        dw0_ref[...] += jnp.sum(dc * u_dn, axis=0, keepdims=True)
        dw1_ref[...] += jnp.sum(dc * uv, axis=0, keepdims=True)
        dw2_ref[...] += jnp.sum(dc * u_up, axis=0, keepdims=True)
        db_ref[...] += jnp.sum(dc, axis=0, keepdims=True)

    vec = pl.BlockSpec((1, F2), lambda i: (0, 0))
    du, dw0, dw1, dw2, db = pl.pallas_call(
        body, name=name + "_bwd", grid=(T // tile,),
        in_specs=_halo_specs(F2, tile, T) + _halo_specs(F2, tile, T) + _halo_specs(F, tile, T) + [pl.BlockSpec((3, F2), lambda i: (0, 0))],
        out_specs=[pl.BlockSpec((tile, F2), lambda i: (i, 0)), vec, vec, vec, vec],
        out_shape=[jax.ShapeDtypeStruct((T, F2), F32)] + [jax.ShapeDtypeStruct((1, F2), F32)] * 4,
        compiler_params=_params("arbitrary"),
    )(u, u, u, c, c, c, dg, dg, dg, cw)
    return du, jnp.concatenate([dw0, dw1, dw2], axis=0), db


def make_conv_act(name, S):
    @jax.custom_vjp
    def op(u, cw, cb):
        return _conv_fwd(name, u, cw, cb, S, ROW_TILE)[1]

    def fwd(u, cw, cb):
        c, g = _conv_fwd(name, u, cw, cb, S, ROW_TILE)
        return g, (u, c, cw)

    def bwd(res, dg):
        u, c, cw = res
        return _conv_bwd(name, u, c, dg, cw, S, ROW_TILE // 2)

    op.defvjp(fwd, bwd)
    return op


def _dot_nt(a, b):
    return lax.dot_general(a.astype(BF16), b.astype(BF16), (((1,), (1,)), ((), ())), preferred_element_type=F32)


def _dot_tn(a, b):
    return lax.dot_general(a.astype(BF16), b.astype(BF16), (((0,), (0,)), ((), ())), preferred_element_type=F32)


def _chunk_decays(la, direction, C, width, dk, const_lg):
    row = lax.broadcasted_iota(jnp.int32, (C, C), 0)
    col = lax.broadcasted_iota(jnp.int32, (C, C), 1)
    keep = (row >= col) if direction == 0 else (row <= col)
    ridx = lax.broadcasted_iota(jnp.int32, (C, 1), 0)
    if const_lg is None:
        cum = jnp.dot(keep.astype(F32), la, precision=HIGHEST, preferred_element_type=F32)
    else:
        lane_head = lax.broadcasted_iota(jnp.int32, (1, width), 1) // dk
        lg = jnp.zeros((1, width), F32)
        for h, val in enumerate(const_lg):
            lg = jnp.where(lane_head == h, val, lg)
        steps = (ridx + 1) if direction == 0 else (C - ridx)
        cum = steps.astype(F32) * lg
    exit_row = C - 1 if direction == 0 else 0
    mid = jnp.sum(jnp.where(ridx == C // 2, cum, 0.0), axis=0, keepdims=True)
    last = jnp.sum(jnp.where(ridx == exit_row, cum, 0.0), axis=0, keepdims=True)
    return keep, ridx == exit_row, cum, mid, last


def _scan_fwd(name, q, kf, kb, v, laf, lab, H, dk, dv, S, C, const_lg):
    T = q.shape[0]
    B, nc = T // S, S // C
    Wk, Wv = H * dk, H * dv
    learn = const_lg is None

    def body(*refs):
        if learn:
            qf_r, qb_r, kf_r, kb_r, vf_r, vb_r, laf_r, lab_r, of_r, ob_r, sf_r, sb_r, st = refs
            las = (laf_r[...], lab_r[...])
        else:
            qf_r, qb_r, kf_r, kb_r, vf_r, vb_r, of_r, ob_r, sf_r, sb_r, st = refs
            las = (None, None)

        @pl.when(pl.program_id(1) == 0)
        def _():
            st[...] = jnp.zeros_like(st)

        for d, (q_r, k_r, v_r, o_r, s_r) in enumerate(((qf_r, kf_r, vf_r, of_r, sf_r), (qb_r, kb_r, vb_r, ob_r, sb_r))):
            keep, _, cum, mid, last = _chunk_decays(las[d], d, C, Wk, dk, None if learn else const_lg[d])
            qe = q_r[...] * jnp.exp(cum - mid)
            ke = k_r[...] * jnp.exp(mid - cum)
            q_in = qe * jnp.exp(mid)
            k_out = ke * jnp.exp(last - mid)
            e_last = jnp.exp(last)
            vv = v_r[...]
            for h in range(H):
                ks, vs = slice(h * dk, (h + 1) * dk), slice(h * dv, (h + 1) * dv)
                a = jnp.where(keep, _dot_nt(qe[:, ks], ke[:, ks]), 0.0)
                state = st[d, h]
                o_r[:, vs] = _bdot(a, vv[:, vs]) + _dot_nt(q_in[:, ks], state)
                s_r[h * dv:(h + 1) * dv, :] = state
                st[d, h] = state * e_last[:, ks] + _dot_tn(vv[:, vs], k_out[:, ks])

    fpos = lambda b, c: (b * nc + c, 0)
    bpos = lambda b, c: (b * nc + nc - 1 - c, 0)
    kspec = lambda pos: pl.BlockSpec((C, Wk), pos)
    vspec = lambda pos: pl.BlockSpec((C, Wv), pos)
    sspec = lambda pos: pl.BlockSpec((None, Wv, dk), lambda b, c: pos(b, c) + (0,))
    ins = [q, q, kf, kb, v, v] + ([laf, lab] if learn else [])
    in_specs = [kspec(fpos), kspec(bpos), kspec(fpos), kspec(bpos), vspec(fpos), vspec(bpos)] + ([kspec(fpos), kspec(bpos)] if learn else [])
    return pl.pallas_call(
        body, name=name + "_fwd", grid=(B, nc), in_specs=in_specs,
        out_specs=[vspec(fpos), vspec(bpos), sspec(fpos), sspec(bpos)],
        out_shape=[jax.ShapeDtypeStruct((T, Wv), F32)] * 2 + [jax.ShapeDtypeStruct((B * nc, Wv, dk), F32)] * 2,
        scratch_shapes=[pltpu.VMEM((2, H, dv, dk), F32)],
        compiler_params=_params("parallel", "arbitrary"),
    )(*ins)


def _scan_bwd(name, q, kf, kb, v, laf, lab, sf, sb, dof, dob, H, dk, dv, S, C, const_lg):
    T = q.shape[0]
    B, nc = T // S, S // C
    Wk, Wv = H * dk, H * dv
    learn = const_lg is None

    def body(*refs):
        if learn:
            (qf_r, qb_r, kf_r, kb_r, vf_r, vb_r, laf_r, lab_r, sf_r, sb_r, dof_r, dob_r,
             dqf_r, dqb_r, dkf_r, dkb_r, dvf_r, dvb_r, dlaf_r, dlab_r, dst) = refs
            las, dlas = (laf_r[...], lab_r[...]), (dlaf_r, dlab_r)
        else:
            (qf_r, qb_r, kf_r, kb_r, vf_r, vb_r, sf_r, sb_r, dof_r, dob_r,
             dqf_r, dqb_r, dkf_r, dkb_r, dvf_r, dvb_r, dst) = refs
            las, dlas = (None, None), (None, None)

        @pl.when(pl.program_id(1) == 0)
        def _():
            dst[...] = jnp.zeros_like(dst)

        groups = ((qf_r, kf_r, vf_r, sf_r, dof_r, dqf_r, dkf_r, dvf_r), (qb_r, kb_r, vb_r, sb_r, dob_r, dqb_r, dkb_r, dvb_r))
        for d, (q_r, k_r, v_r, s_r, do_r, dq_r, dk_r, dv_r) in enumerate(groups):
            keep, is_exit, cum, mid, last = _chunk_decays(las[d], d, C, Wk, dk, None if learn else const_lg[d])
            eq, ek = jnp.exp(cum - mid), jnp.exp(mid - cum)
            e_in, e_out, e_last = jnp.exp(mid), jnp.exp(last - mid), jnp.exp(last)
            qe, ke = q_r[...] * eq, k_r[...] * ek
            q_in, k_out = qe * e_in, ke * e_out
            vv, do = v_r[...], do_r[...]
            dqe_parts, dke_parts, dlast_parts = [], [], []
            for h in range(H):
                ks, vs = slice(h * dk, (h + 1) * dk), slice(h * dv, (h + 1) * dv)
                a = jnp.where(keep, _dot_nt(qe[:, ks], ke[:, ks]), 0.0)
                dp = jnp.where(keep, _dot_nt(do[:, vs], vv[:, vs]), 0.0)
                s_prev = s_r[h * dv:(h + 1) * dv, :]
                ds = dst[d, h]
                dk_out = _bdot(vv[:, vs], ds)
                dqe_parts.append(_bdot(dp, ke[:, ks]) + _bdot(do[:, vs], s_prev) * e_in[:, ks])
                dke_parts.append(_dot_tn(dp, qe[:, ks]) + dk_out * e_out[:, ks])
                dv_r[:, vs] = _dot_tn(a, do[:, vs]) + _dot_nt(k_out[:, ks], ds)
                if learn:
                    dlast_parts.append(jnp.sum(dk_out * k_out[:, ks], axis=0, keepdims=True)
                                       + jnp.sum(ds * s_prev, axis=0, keepdims=True) * e_last[:, ks])
                dst[d, h] = ds * e_last[:, ks] + _dot_tn(do[:, vs], q_in[:, ks])
            dqe = jnp.concatenate(dqe_parts, axis=1)
            dke = jnp.concatenate(dke_parts, axis=1)
            dq_r[...] = dqe * eq
            dk_r[...] = dke * ek
            if learn:
                dcum = dqe * qe - dke * ke + jnp.where(is_exit, jnp.concatenate(dlast_parts, axis=1), 0.0)
                dlas[d][...] = lax.dot_general(keep.astype(F32), dcum, (((0,), (0,)), ((), ())), precision=HIGHEST,
                                               preferred_element_type=F32)

    fpos = lambda b, c: (b * nc + nc - 1 - c, 0)
    bpos = lambda b, c: (b * nc + c, 0)
    kspec = lambda pos: pl.BlockSpec((C, Wk), pos)
    vspec = lambda pos: pl.BlockSpec((C, Wv), pos)
    sspec = lambda pos: pl.BlockSpec((None, Wv, dk), lambda b, c: pos(b, c) + (0,))
    ins = [q, q, kf, kb, v, v] + ([laf, lab] if learn else []) + [sf, sb, dof, dob]
    in_specs = ([kspec(fpos), kspec(bpos), kspec(fpos), kspec(bpos), vspec(fpos), vspec(bpos)]
                + ([kspec(fpos), kspec(bpos)] if learn else []) + [sspec(fpos), sspec(bpos), vspec(fpos), vspec(bpos)])
    n_k = 6 if learn else 4
    out_specs = [kspec(fpos), kspec(bpos), kspec(fpos), kspec(bpos), vspec(fpos), vspec(bpos)] + ([kspec(fpos), kspec(bpos)] if learn else [])
    out_shape = [jax.ShapeDtypeStruct((T, Wk), F32)] * 4 + [jax.ShapeDtypeStruct((T, Wv), F32)] * 2 + ([jax.ShapeDtypeStruct((T, Wk), F32)] * 2 if learn else [])
    del n_k
    return pl.pallas_call(
        body, name=name + "_bwd", grid=(B, nc), in_specs=in_specs, out_specs=out_specs, out_shape=out_shape,
        scratch_shapes=[pltpu.VMEM((2, H, dv, dk), F32)],
        compiler_params=_params("parallel", "arbitrary"),
    )(*ins)


def make_scan(name, H, dk, dv, S, C=SCAN_CHUNK, const_lg=None):
    if const_lg is None:
        @jax.custom_vjp
        def op(q, kf, kb, v, laf, lab):
            return tuple(_scan_fwd(name, q, kf, kb, v, laf, lab, H, dk, dv, S, C, None)[:2])

        def fwd(q, kf, kb, v, laf, lab):
            of, ob, sf, sb = _scan_fwd(name, q, kf, kb, v, laf, lab, H, dk, dv, S, C, None)
            return (of, ob), (q, kf, kb, v, laf, lab, sf, sb)

        def bwd(res, g):
            q, kf, kb, v, laf, lab, sf, sb = res
            dqf, dqb, dkf, dkb, dvf, dvb, dlaf, dlab = _scan_bwd(name, q, kf, kb, v, laf, lab, sf, sb, g[0], g[1], H, dk, dv, S, C, None)
            return dqf + dqb, dkf, dkb, dvf + dvb, dlaf, dlab
    else:
        @jax.custom_vjp
        def op(q, kf, kb, v):
            return tuple(_scan_fwd(name, q, kf, kb, v, None, None, H, dk, dv, S, C, const_lg)[:2])

        def fwd(q, kf, kb, v):
            of, ob, sf, sb = _scan_fwd(name, q, kf, kb, v, None, None, H, dk, dv, S, C, const_lg)
            return (of, ob), (q, kf, kb, v, sf, sb)

        def bwd(res, g):
            q, kf, kb, v, sf, sb = res
            dqf, dqb, dkf, dkb, dvf, dvb = _scan_bwd(name, q, kf, kb, v, None, None, sf, sb, g[0], g[1], H, dk, dv, S, C, const_lg)
            return dqf + dqb, dkf, dkb, dvf + dvb

    op.defvjp(fwd, bwd)
    return op


def _s5_scan_call(name, X, A, S, dirs):
    T, W = X.shape
    N = W // 4
    B, nl = T // S, N // LANES
    n_it = S // S5_SEGMENTS
    assert n_it & (n_it - 1) == 0

    def cmul(ar, ai, br, bi):
        return ar * br - ai * bi, ar * bi + ai * br

    def body(x0r, x0i, x1r, x1i, a_ref, h0r, h0i, h1r, h1i):
        seg = lax.broadcasted_iota(jnp.int32, (S5_SEGMENTS, 1), 0)
        for k, (xr, xi, hr, hi) in enumerate(((x0r, x0i, h0r, h0i), (x1r, x1i, h1r, h1i))):
            back = dirs[k] == 1
            ar = jnp.broadcast_to(a_ref[2 * k:2 * k + 1, :], (S5_SEGMENTS, LANES))
            ai = jnp.broadcast_to(a_ref[2 * k + 1:2 * k + 2, :], (S5_SEGMENTS, LANES))
            rows_of = lambda i: pl.ds(pl.multiple_of(((n_it - 1 - i) if back else i) * S5_SEGMENTS, S5_SEGMENTS), S5_SEGMENTS)

            def local(i, carry):
                sr, si = carry
                rows = rows_of(i)
                pr, pi = cmul(ar, ai, sr, si)
                sr, si = pr + xr[rows, :], pi + xi[rows, :]
                hr[rows, :] = sr
                hi[rows, :] = si
                return sr, si

            zero = jnp.zeros((S5_SEGMENTS, LANES), F32)
            er, ei = lax.fori_loop(0, n_it, local, (zero, zero), unroll=8)
            pr, pi = ar, ai
            for _ in range(n_it.bit_length() - 1):
                pr, pi = cmul(pr, pi, pr, pi)
            shift = (S5_SEGMENTS - 1) if back else 1
            tr, ti = er, ei
            order = range(S5_SEGMENTS - 2, -1, -1) if back else range(1, S5_SEGMENTS)
            for r in order:
                nr, ni = cmul(pr, pi, pltpu.roll(tr, shift, 0), pltpu.roll(ti, shift, 0))
                tr = jnp.where(seg == r, er + nr, tr)
                ti = jnp.where(seg == r, ei + ni, ti)
            edge = (S5_SEGMENTS - 1) if back else 0
            cr = jnp.where(seg == edge, 0.0, pltpu.roll(tr, shift, 0))
            ci = jnp.where(seg == edge, 0.0, pltpu.roll(ti, shift, 0))

            def fix(i, carry):
                wr, wi = carry
                rows = rows_of(i)
                fr, fi = cmul(wr, wi, cr, ci)
                hr[rows, :] = hr[rows, :] + fr
                hi[rows, :] = hi[rows, :] + fi
                return cmul(wr, wi, ar, ai)

            lax.fori_loop(0, n_it, fix, (ar, ai), unroll=8)

    col = lambda g: pl.BlockSpec((S, LANES), lambda b, j: (b, g * nl + j))
    return pl.pallas_call(
        body, name=name, grid=(B, nl),
        in_specs=[col(0), col(1), col(2), col(3), pl.BlockSpec((4, LANES), lambda b, j: (0, j))],
        out_specs=[col(0)] * 4,
        out_shape=[jax.ShapeDtypeStruct((T, N), F32)] * 4,
        compiler_params=_params("parallel", "parallel"),
    )(X, X, X, X, A)


def _s5_decay_grad_call(name, lam, H, X, tile=ROW_TILE):
    T, W = X.shape
    N = W // 4

    def body(l_ref, h_ref, x_ref, p_ref):
        @pl.when(pl.program_id(0) == 0)
        def _():
            p_ref[...] = jnp.zeros_like(p_ref)

        for k in range(2):
            re, im = slice(2 * k * N, (2 * k + 1) * N), slice((2 * k + 1) * N, (2 * k + 2) * N)
            ur, ui = h_ref[:, re] - x_ref[:, re], h_ref[:, im] - x_ref[:, im]
            lr, li = l_ref[:, re], l_ref[:, im]
            p_ref[2 * k:2 * k + 1, :] += jnp.sum(lr * ur + li * ui, axis=0, keepdims=True)
            p_ref[2 * k + 1:2 * k + 2, :] += jnp.sum(li * ur - lr * ui, axis=0, keepdims=True)

    row = pl.BlockSpec((tile, W), lambda i: (i, 0))
    return pl.pallas_call(
        body, name=name, grid=(T // tile,), in_specs=[row, row, row],
        out_specs=pl.BlockSpec((4, N), lambda i: (0, 0)), out_shape=jax.ShapeDtypeStruct((4, N), F32),
        compiler_params=_params("arbitrary"),
    )(lam, H, X)


def make_s5_scan(name, S):
    @jax.custom_vjp
    def op(X, A):
        return jnp.concatenate(_s5_scan_call(name + "_fwd", X, A, S, (0, 1)), axis=1)

    def fwd(X, A):
        H = jnp.concatenate(_s5_scan_call(name + "_fwd", X, A, S, (0, 1)), axis=1)
        return H, (X, A, H)

    def bwd(res, G):
        X, A, H = res
        conj = A * jnp.array([[1.0], [-1.0], [1.0], [-1.0]], F32)
        lam = jnp.concatenate(_s5_scan_call(name + "_bwd", G, conj, S, (1, 0)), axis=1)
        P = _s5_decay_grad_call(name + "_dA", lam, H, X)
        ar, ai = A[0::2], A[1::2]
        pr, pi = P[0::2], P[1::2]
        den = ar * ar + ai * ai
        dar, dai = (pr * ar - pi * ai) / den, (pr * ai + pi * ar) / den
        return lam, jnp.stack([dar[0], dai[0], dar[1], dai[1]], axis=0)

    op.defvjp(fwd, bwd)
    return op


ANY = pl.BlockSpec(memory_space=pl.ANY)


def _place():
    x, y, c = lax.axis_index("x"), lax.axis_index("y"), lax.axis_index("c")
    return x, y, c, [(1 - x, y), (x, 1 - y), (1 - x, 1 - y)]


def all_gather(name, arrs):
    n = len(arrs)

    def body(*refs):
        ins, outs = refs[:n], refs[n:2 * n]
        send, recv, lsem = refs[2 * n:]
        x, y, c, chips = _place()
        me, sibling = (x, y, c), (x, y, 1 - c)

        def copy(a, k, block, to, src=None):
            slot = outs[a].at[4 * block[0] + 2 * block[1] + block[2]]
            return pltpu.make_async_remote_copy(src_ref=slot if src is None else src, dst_ref=slot, send_sem=send.at[a, k],
                                                recv_sem=recv.at[a, k], device_id=to, device_id_type=MESH)

        mine = [pltpu.make_async_copy(ins[a], outs[a].at[4 * x + 2 * y + c], lsem.at[a]) for a in range(n)]
        first = []
        for a in range(n):
            mine[a].start()
            first.append(copy(a, 0, me, sibling, src=ins[a]))
            first += [copy(a, 1 + j, me, (*chip, c), src=ins[a]) for j, chip in enumerate(chips)]
        for cp in first:
            cp.start()
        passed = []
        for a in range(n):
            for j, chip in enumerate(chips):
                copy(a, 1 + j, (*chip, c), me).wait_recv()
                fwd = copy(a, 4 + j, (*chip, c), sibling)
                fwd.start()
                passed.append(fwd)
        for a in range(n):
            copy(a, 0, sibling, me).wait_recv()
            for j, chip in enumerate(chips):
                copy(a, 4 + j, (*chip, 1 - c), me).wait_recv()
        for cp in first + passed:
            cp.wait_send()
        for cp in mine:
            cp.wait()

    return pl.pallas_call(
        body, name=name, in_specs=[ANY] * n, out_specs=[ANY] * n,
        out_shape=[jax.ShapeDtypeStruct((N_DEV,) + a.shape, a.dtype) for a in arrs],
        scratch_shapes=[pltpu.SemaphoreType.DMA((n, 7)), pltpu.SemaphoreType.DMA((n, 7)), pltpu.SemaphoreType.DMA((n,))],
    )(*arrs)


def exchange_pair(name, arrs):
    n = len(arrs)

    def body(*refs):
        ins, outs = refs[:n], refs[n:2 * n]
        send, recv = refs[2 * n:]
        x, y, c, _ = _place()
        copies = []
        for a in range(n):
            for k in range(4):
                copies.append(pltpu.make_async_remote_copy(
                    src_ref=ins[a].at[2 * k + 1 - c], dst_ref=outs[a].at[k], send_sem=send.at[a, k], recv_sem=recv.at[a, k],
                    device_id=(x, y, 1 - c), device_id_type=MESH))
        for cp in copies:
            cp.start()
        for cp in copies:
            cp.wait_recv()
        for cp in copies:
            cp.wait_send()

    return pl.pallas_call(
        body, name=name, in_specs=[ANY] * n, out_specs=[ANY] * n,
        out_shape=[jax.ShapeDtypeStruct((4,) + a.shape[1:], a.dtype) for a in arrs],
        scratch_shapes=[pltpu.SemaphoreType.DMA((n, 4)), pltpu.SemaphoreType.DMA((n, 4))],
    )(*arrs)


def exchange_chips(name, arrs):
    n = len(arrs)

    def body(*refs):
        ins, outs = refs[:n], refs[n:2 * n]
        send, recv = refs[2 * n:]
        x, y, c, chips = _place()
        copies = []
        for a in range(n):
            for j, (cx, cy) in enumerate(chips):
                copies.append(pltpu.make_async_remote_copy(
                    src_ref=ins[a].at[2 * cx + cy], dst_ref=outs[a].at[j], send_sem=send.at[a, j], recv_sem=recv.at[a, j],
                    device_id=(cx, cy, c), device_id_type=MESH))
        for cp in copies:
            cp.start()
        for cp in copies:
            cp.wait_recv()
        for cp in copies:
            cp.wait_send()

    return pl.pallas_call(
        body, name=name, in_specs=[ANY] * n, out_specs=[ANY] * n,
        out_shape=[jax.ShapeDtypeStruct((3,) + a.shape[1:], a.dtype) for a in arrs],
        scratch_shapes=[pltpu.SemaphoreType.DMA((n, 3)), pltpu.SemaphoreType.DMA((n, 3))],
    )(*arrs)


def _row_tile(rows, cols):
    cap = max(SUBLANES, (2**18 // cols) // SUBLANES * SUBLANES)
    if rows <= cap:
        return rows
    for t in range(cap, SUBLANES - 1, -SUBLANES):
        if rows % t == 0:
            return t
    return rows


def pair_add(name, own, got, place):
    _, R, C = own.shape
    rt = _row_tile(R, C)

    def body(p_ref, own_ref, got_ref, part_ref, mine_ref):
        s = own_ref[...] + got_ref[...].astype(F32)
        part_ref[...] = s.astype(BF16)

        @pl.when(pl.program_id(1) == p_ref[1])
        def _():
            mine_ref[...] = s

    return pl.pallas_call(
        body, name=name,
        grid_spec=pltpu.PrefetchScalarGridSpec(
            num_scalar_prefetch=1, grid=(R // rt, 4),
            in_specs=[pl.BlockSpec((None, rt, C), lambda r, k, p: (2 * k + p[0], r, 0)),
                      pl.BlockSpec((None, rt, C), lambda r, k, p: (k, r, 0))],
            out_specs=[pl.BlockSpec((None, rt, C), lambda r, k, p: (k, r, 0)),
                       pl.BlockSpec((rt, C), lambda r, k, p: (r, 0))]),
        out_shape=[jax.ShapeDtypeStruct((4, R, C), BF16), jax.ShapeDtypeStruct((R, C), F32)],
        compiler_params=_params("parallel", "arbitrary"),
    )(place, own, got)


def ordered_sum(name, parts):
    n, R, C = parts.shape
    rt = _row_tile(R, C)

    def body(p_ref, o_ref):
        s = p_ref[0]
        for k in range(1, n):
            s = s + p_ref[k]
        o_ref[...] = s

    return pl.pallas_call(
        body, name=name, grid=(R // rt,),
        in_specs=[pl.BlockSpec((n, rt, C), lambda r: (0, r, 0))], out_specs=pl.BlockSpec((rt, C), lambda r: (r, 0)),
        out_shape=jax.ShapeDtypeStruct((R, C), F32), compiler_params=_params("parallel"),
    )(parts)


def adamw(name, w, m, v, g_own, g_got=None):
    R, C = w.shape
    rt = _row_tile(R, C)
    bias1 = 1.0 - ADAM_B1 ** ADAM_STEP
    bias2 = 1.0 - ADAM_B2 ** ADAM_STEP

    def body(*refs):
        if g_got is None:
            w_ref, m_ref, v_ref, g_ref, go_ref, d_ref, mo_ref, vo_ref = refs
            g = g_ref[...]
        else:
            w_ref, m_ref, v_ref, g_ref, got_ref, go_ref, d_ref, mo_ref, vo_ref = refs
            g = g_ref[...]
            for j in range(3):
                g = g + got_ref[j].astype(F32)
        m_new = ADAM_B1 * m_ref[...] + (1.0 - ADAM_B1) * g
        v_new = ADAM_B2 * v_ref[...] + (1.0 - ADAM_B2) * (g * g)
        go_ref[...] = g
        mo_ref[...] = m_new
        vo_ref[...] = v_new
        d_ref[...] = -ADAM_LR * ((m_new / bias1) / (jnp.sqrt(v_new / bias2) + ADAM_EPS) + ADAM_WD * w_ref[...])

    row = pl.BlockSpec((rt, C), lambda r: (r, 0))
    ins = [w, m, v, g_own] + ([] if g_got is None else [g_got])
    in_specs = [row] * 4 + ([] if g_got is None else [pl.BlockSpec((3, rt, C), lambda r: (0, r, 0))])
    return pl.pallas_call(
        body, name=name, grid=(R // rt,), in_specs=in_specs, out_specs=[row] * 4,
        out_shape=[jax.ShapeDtypeStruct((R, C), F32)] * 4, compiler_params=_params("parallel"),
    )(*ins)


def _hgrn_lower_bounds(lb_logits):
    p = jax.nn.softmax(lb_logits, axis=1)
    return jnp.cumsum(p, axis=1) - p[:, :1]


def _s5_discretise(lam_re, lam_im, log_dt, b_re, b_im):
    lr = jnp.minimum(lam_re, -1e-4)
    li = lam_im
    dt = jnp.exp(log_dt)[:, None]
    mag = jnp.exp(lr * dt)
    ar, ai = mag * jnp.cos(li * dt), mag * jnp.sin(li * dt)
    den = lr * lr + li * li
    nr = ar - 1.0
    cr = (nr * lr + ai * li) / den
    ci = (ai * lr - nr * li) / den
    bbr = cr[..., None] * b_re - ci[..., None] * b_im
    bbi = cr[..., None] * b_im + ci[..., None] * b_re
    return ar, ai, bbr, bbi


def _block_diag(t):
    G, a, b = t.shape
    eye = jnp.eye(G, dtype=F32)
    return (t[:, :, None, :] * eye[:, None, :, None]).reshape(G * a, G * b)


def _rope_tables(S):
    half = RET_DK // 2
    inv = ROPE_BASE ** (-jnp.arange(half, dtype=F32) / half)
    ang = jnp.arange(S, dtype=F32)[:, None] * inv[None, :]
    cos, sin = jnp.cos(ang), jnp.sin(ang)
    return jnp.concatenate([cos, cos], axis=1), jnp.concatenate([-sin, sin], axis=1)


def _ret_log_decays():
    f = tuple(float(np.log1p(-np.exp2(np.float32(-5.0 - h)))) for h in range(RET_HEADS))
    b = tuple(float(np.log1p(-np.exp2(np.float32(-5.5 - h)))) for h in range(RET_HEADS))
    return f, b


def _segment_order(t, S):
    T, w = t.shape
    return t.reshape(T // S, S5_SEGMENTS, S // S5_SEGMENTS, w).transpose(0, 2, 1, 3).reshape(T, w)


def _time_order(t, S):
    T, w = t.shape
    return t.reshape(T // S, S // S5_SEGMENTS, S5_SEGMENTS, w).transpose(0, 2, 1, 3).reshape(T, w)


def trunk(x, carriers, small, big16, S):
    row = lambda t: t.reshape(1, -1)
    lbs = _hgrn_lower_bounds(small["hgrn_lb_logits"])
    cosf, sinf = _rope_tables(S)
    mm = lambda name, t, key, idx: make_mm(name)(t, big16[key][idx], carriers[key][idx])
    add = None
    for layer in range(DEPTH):
        j = layer // 2
        tag = f"l{layer}"
        g = row(small["mix_norm_g"][layer])
        if add is None:
            (h,) = make_rowop(tag + "_norm", norm_f, (D_MODEL,))((x,), (g,), ())
        else:
            x, h = make_rowop(tag + "_addnorm", addnorm_f, (D_MODEL, D_MODEL))((x, add), (g,), ())
        if layer % 2 == 0:
            p = mm(tag + "_in", h, "w_in_even", j)
            prm = (small["gla_wa2"][j, 0], small["gla_wa2"][j, 1], row(small["gla_ba"][j, 0]), row(small["gla_ba"][j, 1]),
                   row(lbs[0, j]), row(lbs[1, j]))
            gq, gk, gv, glaf, glab, gr, hq, hkf, hkb, hv, hlaf, hlab, hg = make_rowop(tag + "_prep", even_prep_f, EVEN_PREP_WIDTHS)((p,), prm, ())
            of, ob = make_scan(tag + "_gla", GLA_HEADS, GLA_DK, GLA_DV, S)(gq, gk, gk, gv, glaf, glab)
            hof, hob = make_scan(tag + "_hgrn", HGRN_HEADS, HGRN_DK, HGRN_DV, S)(hq, hkf, hkb, hv, hlaf, hlab)
            (y,) = make_rowop(tag + "_post", even_post_f, (D_MODEL,))(
                (of, ob, hof, hob, gr, hg), (row(small["gla_norm_g"][j]), row(small["hgrn_norm_g"][j])), ())
            mix = mm(tag + "_out", y, "w_out_even", j)
        else:
            p = mm(tag + "_in", h, "w_in_odd", j)
            rq, rk, rv, rg, su = make_rowop(tag + "_prep", odd_prep_f, ODD_PREP_WIDTHS)((p,), (), (cosf, sinf))
            of, ob = make_scan(tag + "_ret", RET_HEADS, RET_DK, RET_DV, S, const_lg=_ret_log_decays())(rq, rk, rk, rv)
            (cm,) = make_rowop(tag + "_retpost", ret_post_f, (RET_HEADS * RET_DV,))((of, ob, rg), (row(small["ret_norm_g"][j]),), ())
            disc = [_s5_discretise(small["s5_lam_re"][j, d], small["s5_lam_im"][j, d], small["s5_log_dt"][j, d],
                                   small["s5_b_re"][j], small["s5_b_im"][j]) for d in range(2)]
            bd = jnp.concatenate([_block_diag(jnp.swapaxes(t, 1, 2)) for d in range(2) for t in disc[d][2:]], axis=1)
            a4 = jnp.stack([t.reshape(-1) for d in range(2) for t in disc[d][:2]], axis=0)
            su_p = _segment_order(su, S)
            X = make_mm(tag + "_s5in")(su_p, lax.stop_gradient(bd).astype(BF16), bd)
            Hs = make_s5_scan(tag + "_s5scan", S)(X, a4)
            prm = (_block_diag(jnp.swapaxes(small["s5_c_re"][j], 1, 2)), _block_diag(jnp.swapaxes(small["s5_c_im"][j], 1, 2)),
                   row(small["s5_d"][j]), small["s5_glu_w"][j], row(small["s5_glu_b"][j]))
            (dm_p,) = make_rowop(tag + "_s5post", s5_post_f, (S5_WIDTH,))((Hs, su_p), prm, ())
            y = jnp.concatenate([cm, _time_order(dm_p, S)], axis=1)
            mix = mm(tag + "_out", y, "w_out_odd", j)
        x, hf = make_rowop(tag + "_ffnnorm", addnorm_f, (D_MODEL, D_MODEL))((x, mix), (row(small["ffn_norm_g"][layer]),), ())
        u = mm(tag + "_up", hf, "ffn_w_up", layer)
        act = make_conv_act(tag + "_conv", S)(u, small["ffn_conv_w"][layer], row(small["ffn_conv_b"][layer]))
        add = mm(tag + "_down", act, "ffn_w_down", layer)
    return x, add


BIG = {"w_in_even": 2, "w_out_even": 1, "w_in_odd": 2, "w_out_odd": 1, "ffn_w_up": 2, "ffn_w_down": 1}
SMALL_SHARDED = {"gla_wa2": 3, "gla_ba": 2, "hgrn_lb_logits": 2, "ret_norm_g": 1, "s5_d": 1, "s5_glu_w": 1, "s5_glu_b": 1,
                 "ffn_conv_w": 2}
REPLICATED = ("mix_norm_g", "ffn_norm_g", "final_norm_g", "gla_norm_g", "hgrn_norm_g", "s5_lam_re", "s5_lam_im", "s5_log_dt",
              "s5_b_re", "s5_b_im", "s5_c_re", "s5_c_im", "ffn_conv_b")
WEIGHTS = ("mix_norm_g", "ffn_norm_g", "final_norm_g", "w_in_even", "w_out_even", "gla_wa2", "gla_ba", "gla_norm_g",
           "hgrn_lb_logits", "hgrn_norm_g", "w_in_odd", "w_out_odd", "ret_norm_g", "s5_lam_re", "s5_lam_im", "s5_log_dt",
           "s5_b_re", "s5_b_im", "s5_c_re", "s5_c_im", "s5_d", "s5_glu_w", "s5_glu_b", "ffn_w_up", "ffn_conv_w", "ffn_conv_b",
           "ffn_w_down")
PACK_COLS = 512


def _unshard(g, axis):
    t = jnp.moveaxis(g, 0, axis)
    return t.reshape(t.shape[:axis] + (t.shape[axis] * t.shape[axis + 1],) + t.shape[axis + 2:])


def _to_blocks(full, axis):
    t = full.reshape(full.shape[:axis] + (N_DEV, full.shape[axis] // N_DEV) + full.shape[axis + 1:])
    return jnp.moveaxis(t, axis, 0)


def _pack(arrs):
    flat = jnp.concatenate([a.reshape(-1) for a in arrs])
    pad = (-flat.shape[0]) % (PACK_COLS * SUBLANES)
    return jnp.pad(flat, (0, pad)).reshape(-1, PACK_COLS)


def _unpack(packed, shapes):
    flat = packed.reshape(-1)
    out, o = [], 0
    for s in shapes:
        n = int(np.prod(s))
        out.append(flat[o:o + n].reshape(s))
        o += n
    return out


def _flat2d(a):
    return a.reshape(-1, a.shape[-1])


def kernel(x, mix_norm_g, ffn_norm_g, final_norm_g, w_in_even, w_out_even, gla_wa2, gla_ba, gla_norm_g, hgrn_lb_logits, hgrn_norm_g, w_in_odd, w_out_odd, ret_norm_g, s5_lam_re, s5_lam_im, s5_log_dt, s5_b_re, s5_b_im, s5_c_re, s5_c_im, s5_d, s5_glu_w, s5_glu_b, ffn_w_up, ffn_conv_w, ffn_conv_b, ffn_w_down, loss_target, m_mix_norm_g, m_ffn_norm_g, m_final_norm_g, m_w_in_even, m_w_out_even, m_gla_wa2, m_gla_ba, m_gla_norm_g, m_hgrn_lb_logits, m_hgrn_norm_g, m_w_in_odd, m_w_out_odd, m_ret_norm_g, m_s5_lam_re, m_s5_lam_im, m_s5_log_dt, m_s5_b_re, m_s5_b_im, m_s5_c_re, m_s5_c_im, m_s5_d, m_s5_glu_w, m_s5_glu_b, m_ffn_w_up, m_ffn_conv_w, m_ffn_conv_b, m_ffn_w_down, v_mix_norm_g, v_ffn_norm_g, v_final_norm_g, v_w_in_even, v_w_out_even, v_gla_wa2, v_gla_ba, v_gla_norm_g, v_hgrn_lb_logits, v_hgrn_norm_g, v_w_in_odd, v_w_out_odd, v_ret_norm_g, v_s5_lam_re, v_s5_lam_im, v_s5_log_dt, v_s5_b_re, v_s5_b_im, v_s5_c_re, v_s5_c_im, v_s5_d, v_s5_glu_w, v_s5_glu_b, v_ffn_w_up, v_ffn_conv_w, v_ffn_conv_b, v_ffn_w_down):
    args = locals()
    w = {n: args[n] for n in WEIGHTS}
    m = {n: args["m_" + n] for n in WEIGHTS}
    v = {n: args["v_" + n] for n in WEIGHTS}
    Bl, S, D = x.shape
    T = Bl * S
    ix, iy, ic = lax.axis_index("x"), lax.axis_index("y"), lax.axis_index("c")
    me = 4 * ix + 2 * iy + ic

    big_names, sm_names = list(BIG), list(SMALL_SHARDED)
    gathered = all_gather("gather_weights", [w[n].astype(BF16) for n in big_names] + [_pack([w[n] for n in sm_names])])
    big16 = {n: _unshard(g, BIG[n]) for n, g in zip(big_names, gathered[:-1])}
    sm_parts = [_unpack(gathered[-1][d], [w[n].shape for n in sm_names]) for d in range(N_DEV)]
    small = {n: _unshard(jnp.stack([sm_parts[d][i] for d in range(N_DEV)]), SMALL_SHARDED[n]) for i, n in enumerate(sm_names)}
    small.update({n: w[n] for n in REPLICATED})
    carriers = {n: [jnp.zeros(big16[n].shape[1:], F32) for _ in range(big16[n].shape[0])] for n in big_names}

    xt = x.reshape(T, D)
    diff_small = {n: small[n] for n in small if n != "final_norm_g"}
    (xs, add), pull = jax.vjp(lambda x_, c_, s_: trunk(x_, c_, s_, big16, S), xt, carriers, diff_small)
    loss_acc, dxf, dgf = loss_head(xs, add, small["final_norm_g"].reshape(1, D), loss_target.reshape(T, D))
    dx, g_big, g_small = pull((dxf, dxf))
    g_small["final_norm_g"] = dgf.reshape(D)
    loss = lax.psum(loss_acc[0, 0], ("x", "y", "c"))

    place = jnp.stack([ic, 2 * ix + iy]).astype(jnp.int32)
    own = [_to_blocks(jnp.stack(g_big[n]), BIG[n]) for n in big_names]
    own = [t.reshape(N_DEV, -1, t.shape[-1]) for t in own]
    got = exchange_pair("grads_to_sibling", [t.astype(BF16) for t in own])
    parts, mine = zip(*[pair_add("pair_add_" + n, o, g, place) for n, o, g in zip(big_names, own, got)])
    got3 = exchange_chips("grads_to_chips", list(parts))
    out = {}
    for n, mi, g3 in zip(big_names, mine, got3):
        res = adamw("adamw_" + n, _flat2d(w[n]), _flat2d(m[n]), _flat2d(v[n]), mi, g3)
        out[n] = [t.reshape(w[n].shape) for t in res]

    sm_all = sm_names + list(REPLICATED)
    (g_all,) = all_gather("gather_small_grads", [_pack([g_small[n] for n in sm_all])])
    g_sum = _unpack(ordered_sum("sum_small_grads", g_all), [g_small[n].shape for n in sm_all])
    g_loc = []
    for n, g in zip(sm_all, g_sum):
        if n in SMALL_SHARDED:
            ax = SMALL_SHARDED[n]
            size = w[n].shape[ax]
            g = lax.dynamic_slice_in_dim(g, me * size, size, axis=ax)
        g_loc.append(g)
    shapes = [w[n].shape for n in sm_all]
    res = adamw("adamw_small", _pack([w[n] for n in sm_all]), _pack([m[n] for n in sm_all]), _pack([v[n] for n in sm_all]), _pack(g_loc))
    res = [_unpack(r, shapes) for r in res]
    for k, n in enumerate(sm_all):
        out[n] = [r[k] for r in res]

    grads, deltas, new_m, new_v = ([out[n][k] for n in WEIGHTS] for k in range(4))
    return (loss, dx.reshape(Bl, S, D), *grads, *deltas, *new_m, *new_v)
```

```python
import functools
import math

import numpy as np
import jax
import jax.numpy as jnp
from jax import lax
from jax.experimental import pallas as pl
from jax.experimental.pallas import tpu as pltpu

F32 = jnp.float32
BF16 = jnp.bfloat16
HIGHEST = lax.Precision.HIGHEST
MESH = pl.DeviceIdType.MESH

D_MODEL = 1024
DEPTH = 4
N_EVEN = 2
N_ODD = 2
GLA_HEADS, GLA_DK, GLA_DV, GLA_RANK, GLA_GATE_NORM = 4, 64, 128, 16, 16.0
HGRN_HEADS, HGRN_DK, HGRN_DV, HGRN_MIN_F = 4, 64, 128, 1e-20
RET_HEADS, RET_DK, RET_DV = 4, 128, 192
ROPE_BASE = 10000.0
S5_WIDTH, S5_GROUP_CH, S5_GROUPS, S5_STATE = 256, 16, 16, 64
S5_N = S5_GROUPS * S5_STATE
FFN_DIM = 2816
EPS = 1e-6
EVEN_IN = 3360
ODD_IN = 2816
ADAM_LR, ADAM_B1, ADAM_B2, ADAM_EPS, ADAM_WD, ADAM_STEP = 0.001, 0.9, 0.999, 1e-08, 0.01, 10

N_DEV = 8
VMEM_LIMIT_BYTES = 56 * 1024 * 1024
ROW_TILE = 256
SCAN_CHUNK = 64
S5_SEGMENTS = 8
LANES = 128


def _params(*sem):
    return pltpu.CompilerParams(dimension_semantics=sem, vmem_limit_bytes=VMEM_LIMIT_BYTES)


def _divisor_tile(n, cap):
    best = None
    for t in range(LANES, min(n, cap) + 1, LANES):
        if n % t == 0:
            best = t
    return best if best is not None else n


def _mm_nn(name, x, w):
    M, K = x.shape
    N = w.shape[1]
    tn = _divisor_tile(N, 2048) if K * N * 2 > 8 * 2**20 else N
    tm = 256 if tn * 4 * 512 > 6 * 2**20 else 512
    assert M % tm == 0 and N % tn == 0

    def body(x_ref, w_ref, o_ref):
        o_ref[...] = jnp.dot(x_ref[...].astype(BF16), w_ref[...], preferred_element_type=F32)

    return pl.pallas_call(
        body, name=name, grid=(N // tn, M // tm),
        in_specs=[pl.BlockSpec((tm, K), lambda j, i: (i, 0)), pl.BlockSpec((K, tn), lambda j, i: (0, j))],
        out_specs=pl.BlockSpec((tm, tn), lambda j, i: (i, j)),
        out_shape=jax.ShapeDtypeStruct((M, N), F32),
        compiler_params=_params("parallel", "parallel"),
    )(x, w)


def _mm_nt(name, dy, w):
    M, N = dy.shape
    K = w.shape[0]
    tk = _divisor_tile(K, 1024) if K * N * 2 > 8 * 2**20 else K
    tm = 256 if N >= 4096 else 512
    assert M % tm == 0 and K % tk == 0

    def body(dy_ref, w_ref, o_ref):
        o_ref[...] = lax.dot_general(dy_ref[...].astype(BF16), w_ref[...], (((1,), (1,)), ((), ())),
                                     preferred_element_type=F32)

    return pl.pallas_call(
        body, name=name, grid=(K // tk, M // tm),
        in_specs=[pl.BlockSpec((tm, N), lambda j, i: (i, 0)), pl.BlockSpec((tk, N), lambda j, i: (j, 0))],
        out_specs=pl.BlockSpec((tm, tk), lambda j, i: (i, j)),
        out_shape=jax.ShapeDtypeStruct((M, K), F32),
        compiler_params=_params("parallel", "parallel"),
    )(dy, w)


MM_TN_VMEM_BUDGET = 40 * 2**20


def _pad_lanes(n):
    return -(-n // LANES) * LANES


def _mm_tn(name, x, dy, nblk, want16):
    M, K = x.shape
    N = dy.shape[1]
    n = N // nblk
    xb, yb = x.dtype.itemsize, dy.dtype.itemsize
    best = None
    for tk in [t for t in range(LANES, K + 1, LANES) if K % t == 0]:
        for tm in (512, 256):
            out_bytes = nblk * tk * _pad_lanes(n) * (6 if want16 else 4)
            vmem = 2 * out_bytes + 2 * tm * tk * xb + 2 * tm * _pad_lanes(N) * yb
            traffic = (K // tk) * M * N * yb + M * K * xb
            if vmem <= MM_TN_VMEM_BUDGET and M % tm == 0 and (best is None or (traffic, -tm) < best[0]):
                best = ((traffic, -tm), tk, tm)
    _, tk, tm = best
    last = M // tm - 1

    def body(x_ref, dy_ref, o32_ref, *o16_ref):
        m = pl.program_id(1)

        @pl.when(m == 0)
        def _():
            o32_ref[...] = jnp.zeros_like(o32_ref)

        dyv = dy_ref[...].astype(BF16)
        if nblk == 1:
            o32_ref[0] += lax.dot_general(x_ref[...].astype(BF16), dyv, (((0,), (0,)), ((), ())), preferred_element_type=F32)
        else:
            xt = x_ref[...].astype(F32).T.astype(BF16)
            for d in range(nblk):
                o32_ref[d] += jnp.dot(xt, dyv[:, d * n:(d + 1) * n], preferred_element_type=F32)
        if want16:
            @pl.when(m == last)
            def _():
                o16_ref[0][...] = o32_ref[...].astype(BF16)

    blk = pl.BlockSpec((nblk, tk, n), lambda a, m: (0, a, 0))
    return pl.pallas_call(
        body, name=name, grid=(K // tk, M // tm),
        in_specs=[pl.BlockSpec((tm, tk), lambda a, m: (m, a)), pl.BlockSpec((tm, N), lambda a, m: (m, 0))],
        out_specs=[blk, blk] if want16 else [blk],
        out_shape=[jax.ShapeDtypeStruct((nblk, K, n), F32)] + ([jax.ShapeDtypeStruct((nblk, K, n), BF16)] if want16 else []),
        compiler_params=_params("parallel", "arbitrary"),
    )(x, dy)


def _weight_grads(name, x, dy, col_sharded):
    if col_sharded:
        return _mm_tn(name, x, dy, N_DEV, True)
    d32, d16 = _mm_tn(name, x, dy, 1, True)
    K, N = d32.shape[1:]
    return d32.reshape(N_DEV, K // N_DEV, N), d16.reshape(N_DEV, K // N_DEV, N)


def make_mm(name, col_sharded=False):
    @jax.custom_vjp
    def mm(x, w16, c32, c16):
        return _mm_nn(name + "_fwd", x, w16)

    def fwd(x, w16, c32, c16):
        return _mm_nn(name + "_fwd", x, w16), (x, w16)

    def bwd(res, dy):
        x, w16 = res
        d32, d16 = _weight_grads(name + "_dw", x, dy, col_sharded)
        return _mm_nt(name + "_dx", dy, w16), jnp.zeros_like(w16), d32, d16

    mm.defvjp(fwd, bwd)
    return mm


def make_mm_f32w(name):
    @jax.custom_vjp
    def mm(x, w):
        return _mm_nn(name + "_fwd", x, w.astype(BF16))

    def fwd(x, w):
        w16 = w.astype(BF16)
        return _mm_nn(name + "_fwd", x, w16), (x, w16)

    def bwd(res, dy):
        x, w16 = res
        return _mm_nt(name + "_dx", dy, w16), _mm_tn(name + "_dw", x, dy, 1, False)[0][0]

    mm.defvjp(fwd, bwd)
    return mm


def _row_specs(rows, params, consts, tile):
    specs = [pl.BlockSpec((tile, r.shape[1]), lambda i: (i, 0)) for r in rows]
    specs += [pl.BlockSpec(p.shape, lambda i: (0, 0)) for p in params]
    specs += [pl.BlockSpec((tile, c.shape[1]), lambda i, n=c.shape[0] // tile: (i % n, 0)) for c in consts]
    return specs


def _row_fwd(name, f, out_widths, tile, rows, params, consts):
    T = rows[0].shape[0]
    nr, npar, ncon = len(rows), len(params), len(consts)

    def body(*refs):
        r = tuple(x[...] for x in refs[:nr])
        p = tuple(x[...] for x in refs[nr:nr + npar])
        c = tuple(x[...] for x in refs[nr + npar:nr + npar + ncon])
        outs = f(r, p, c)
        for o_ref, o in zip(refs[nr + npar + ncon:], outs, strict=True):
            o_ref[...] = o

    return pl.pallas_call(
        body, name=name + "_fwd", grid=(T // tile,),
        in_specs=_row_specs(rows, params, consts, tile),
        out_specs=[pl.BlockSpec((tile, w), lambda i: (i, 0)) for w in out_widths],
        out_shape=[jax.ShapeDtypeStruct((T, w), F32) for w in out_widths],
        compiler_params=_params("parallel"),
    )(*rows, *params, *consts)


def _row_bwd(name, f, out_widths, tile, rows, params, consts, gouts, dr_dtype=F32):
    T = rows[0].shape[0]
    nr, npar, ncon, nout = len(rows), len(params), len(consts), len(out_widths)

    def body(*refs):
        r = tuple(x[...] for x in refs[:nr])
        p = tuple(x[...] for x in refs[nr:nr + npar])
        c = tuple(x[...] for x in refs[nr + npar:nr + npar + ncon])
        k = nr + npar + ncon
        g = tuple(x[...] for x in refs[k:k + nout])
        dr_refs = refs[k + nout:k + nout + nr]
        dp_refs = refs[k + nout + nr:]
        _, vjp = jax.vjp(lambda r_, p_: tuple(f(r_, p_, c)), r, p)
        dr, dp = vjp(g)
        for ref, val in zip(dr_refs, dr, strict=True):
            ref[...] = val.astype(ref.dtype)
        if npar:
            @pl.when(pl.program_id(0) == 0)
            def _():
                for ref in dp_refs:
                    ref[...] = jnp.zeros_like(ref)

            for ref, val in zip(dp_refs, dp, strict=True):
                ref[...] += val

    outs = pl.pallas_call(
        body, name=name + "_bwd", grid=(T // tile,),
        in_specs=_row_specs(rows, params, consts, tile) + [pl.BlockSpec((tile, w), lambda i: (i, 0)) for w in out_widths],
        out_specs=[pl.BlockSpec((tile, r.shape[1]), lambda i: (i, 0)) for r in rows]
        + [pl.BlockSpec(p.shape, lambda i: (0, 0)) for p in params],
        out_shape=[jax.ShapeDtypeStruct(r.shape, dr_dtype) for r in rows] + [jax.ShapeDtypeStruct(p.shape, F32) for p in params],
        compiler_params=_params("arbitrary"),
    )(*rows, *params, *consts, *gouts)
    return tuple(outs[:nr]), tuple(outs[nr:])


def make_proj_stage(name, f, out_widths, tile=ROW_TILE):
    def run(x, w16, params, consts):
        p = _mm_nn(name + "_mm", x, w16)
        return p, tuple(_row_fwd(name, f, out_widths, tile, (p,), params, consts))

    @jax.custom_vjp
    def op(x, w16, c32, c16, params, consts):
        return run(x, w16, params, consts)[1]

    def fwd(x, w16, c32, c16, params, consts):
        p, outs = run(x, w16, params, consts)
        return outs, (x, w16, p, params, consts)

    def bwd(res, g):
        x, w16, p, params, consts = res
        (dp,), dparams = _row_bwd(name, f, out_widths, tile, (p,), params, consts, tuple(g), dr_dtype=BF16)
        d32, d16 = _weight_grads(name + "_dw", x, dp, True)
        return _mm_nt(name + "_dx", dp, w16), jnp.zeros_like(w16), d32, d16, dparams, tuple(jnp.zeros_like(c) for c in consts)

    op.defvjp(fwd, bwd)
    return op


def make_rowop(name, f, out_widths, tile=ROW_TILE):
    @jax.custom_vjp
    def op(rows, params, consts):
        return tuple(_row_fwd(name, f, out_widths, tile, rows, params, consts))

    def fwd(rows, params, consts):
        return op(rows, params, consts), (rows, params, consts)

    def bwd(res, g):
        rows, params, consts = res
        dr, dp = _row_bwd(name, f, out_widths, tile, rows, params, consts, tuple(g))
        return dr, dp, tuple(jnp.zeros_like(c) for c in consts)

    op.defvjp(fwd, bwd)
    return op


def _rms(x, g):
    return x * lax.rsqrt(jnp.mean(x * x, axis=-1, keepdims=True) + EPS) * g


def _silu(x):
    return x * jax.nn.sigmoid(x)


def _bdot(a, b):
    return jnp.dot(a.astype(BF16), b.astype(BF16), preferred_element_type=F32)


def norm_f(rows, params, consts):
    return (_rms(rows[0], params[0]),)


def addnorm_f(rows, params, consts):
    x = rows[0] + rows[1]
    return x, _rms(x, params[0])


EVEN_GLA_END = 1568


def even_prep_f(rows, params, consts):
    (p,) = rows
    wa2f, wa2b, baf, bab, lbf, lbb = params
    gq = p[:, 0:256]
    gk = p[:, 256:512] * (GLA_DK ** -0.5)
    gv = p[:, 512:1024]
    gr = p[:, 1024:1536]
    glaf = jax.nn.log_sigmoid(_bdot(p[:, 1536:1552], wa2f) + baf) / GLA_GATE_NORM
    glab = jax.nn.log_sigmoid(_bdot(p[:, 1552:1568], wa2b) + bab) / GLA_GATE_NORM
    o = EVEN_GLA_END
    hq = _silu(p[:, o:o + 256])

    def gate(z, lb):
        f = lb + (1.0 - lb) * jax.nn.sigmoid(z)
        return jnp.log(jnp.maximum(f, HGRN_MIN_F)), (1.0 - lb) * jax.nn.sigmoid(-z)

    hlaf, hkf = gate(p[:, o + 256:o + 512], lbf)
    hlab, hkb = gate(p[:, o + 512:o + 768], lbb)
    hv = p[:, o + 768:o + 1280]
    hg = p[:, o + 1280:o + 1792]
    return gq, gk, gv, glaf, glab, gr, hq, hkf, hkb, hv, hlaf, hlab, hg


EVEN_PREP_WIDTHS = (256, 256, 512, 256, 256, 512, 256, 256, 256, 512, 256, 256, 512)


def _head_rms(o, g, heads, d):
    parts = []
    for h in range(heads):
        seg = o[:, h * d:(h + 1) * d]
        parts.append(seg * lax.rsqrt(jnp.mean(seg * seg, axis=-1, keepdims=True) + EPS))
    return jnp.concatenate(parts, axis=1) * g


def even_post_f(rows, params, consts):
    of, ob, hof, hob, gr, hg = rows
    a = _head_rms(of + ob, params[0], GLA_HEADS, GLA_DV) * _silu(gr)
    b = _head_rms(hof + hob, params[1], HGRN_HEADS, HGRN_DV) * _silu(hg)
    return (jnp.concatenate([a, b], axis=1),)


@jax.custom_vjp
def _roll_half(x):
    return pltpu.roll(x, RET_DK // 2, 1)


_roll_half.defvjp(lambda x: (_roll_half(x), None), lambda _, g: (_roll_half(g),))


def odd_prep_f(rows, params, consts):
    (p,) = rows
    cosf, sinf = consts

    def rot(t):
        parts = []
        for h in range(RET_HEADS):
            th = t[:, h * RET_DK:(h + 1) * RET_DK]
            parts.append(th * cosf + _roll_half(th) * sinf)
        return jnp.concatenate(parts, axis=1)

    rq = rot(p[:, 0:512])
    rk = rot(p[:, 512:1024]) * (RET_DK ** -0.5)
    return rq, rk, p[:, 1024:1792], p[:, 1792:2560], p[:, 2560:2816]


ODD_PREP_WIDTHS = (512, 512, 768, 768, 256)


def ret_post_f(rows, params, consts):
    of, ob, rg = rows
    o = of + ob
    parts = []
    for h in range(RET_HEADS):
        seg = o[:, h * RET_DV:(h + 1) * RET_DV]
        c = seg - jnp.mean(seg, axis=-1, keepdims=True)
        parts.append(c * lax.rsqrt(jnp.mean(c * c, axis=-1, keepdims=True) + EPS))
    return (jnp.concatenate(parts, axis=1) * params[0] * _silu(rg),)


def s5_post_f(rows, params, consts):
    h0r, h0i, h1r, h1i, u = rows
    c_re, c_im, d_skip, glu_w, glu_b = params
    hr = h0r + h1r
    hi = h0i + h1i
    y = _bdot(hr, c_re) - _bdot(hi, c_im) + d_skip * u
    g = jax.nn.gelu(y)
    return (g * jax.nn.sigmoid(_bdot(g, glu_w) + glu_b),)


def loss_head(x, r, g, target, tile=ROW_TILE):
    T, D = x.shape

    def body(x_ref, r_ref, g_ref, t_ref, loss_ref, dx_ref, dg_ref):
        t = t_ref[...]

        def lf(xv, gv):
            e = _rms(xv, gv) - t
            row = jnp.sum(e * e, axis=-1, keepdims=True) * (0.5 / D)
            return jnp.sum(row, axis=0, keepdims=True)

        l, vjp = jax.vjp(lf, x_ref[...] + r_ref[...], g_ref[...])
        dx, dg = vjp(jnp.ones((1, 1), F32))
        dx_ref[...] = dx

        @pl.when(pl.program_id(0) == 0)
        def _():
            loss_ref[...] = jnp.zeros_like(loss_ref)
            dg_ref[...] = jnp.zeros_like(dg_ref)

        loss_ref[...] += jnp.broadcast_to(l, loss_ref.shape)
        dg_ref[...] += dg

    row = pl.BlockSpec((tile, D), lambda i: (i, 0))
    vec = pl.BlockSpec((1, D), lambda i: (0, 0))
    return pl.pallas_call(
        body, name="loss_head", grid=(T // tile,),
        in_specs=[row, row, vec, row],
        out_specs=[pl.BlockSpec((1, LANES), lambda i: (0, 0)), row, vec],
        out_shape=[jax.ShapeDtypeStruct((1, LANES), F32), jax.ShapeDtypeStruct((T, D), F32), jax.ShapeDtypeStruct((1, D), F32)],
        compiler_params=_params("arbitrary"),
    )(x, r, g, target)


SUBLANES = 8


def _halo_specs(width, tile, T):
    n8 = tile // SUBLANES
    last = T // SUBLANES - 1
    return [pl.BlockSpec((tile, width), lambda i: (i, 0)),
            pl.BlockSpec((SUBLANES, width), lambda i: (jnp.maximum(i * n8 - 1, 0), 0)),
            pl.BlockSpec((SUBLANES, width), lambda i: (jnp.minimum((i + 1) * n8, last), 0))]


def _shift_rows(x, prev_row, next_row, tile):
    row = lax.broadcasted_iota(jnp.int32, (tile, 1), 0)
    down = jnp.where(row == 0, prev_row, pltpu.roll(x, 1, 0))
    up = jnp.where(row == tile - 1, next_row, pltpu.roll(x, tile - 1, 0))
    return down, up


def _conv_fwd(name, u, cw, cb, S, tile):
    T, F2 = u.shape
    F = F2 // 2
    per_seq = S // tile

    def body(u_ref, up_ref, un_ref, cw_ref, cb_ref, c_ref, g_ref):
        pos = pl.program_id(0) % per_seq
        uv = u_ref[...]
        prev_row = jnp.where(pos == 0, 0.0, up_ref[SUBLANES - 1:SUBLANES, :])
        next_row = jnp.where(pos == per_seq - 1, 0.0, un_ref[0:1, :])
        down, up = _shift_rows(uv, prev_row, next_row, tile)
        c = cb_ref[...] + down * cw_ref[0:1, :]
        c = c + uv * cw_ref[1:2, :]
        c = c + up * cw_ref[2:3, :]
        c_ref[...] = c
        g_ref[...] = (_silu(c[:, :F]) * c[:, F:]).astype(BF16)

    return pl.pallas_call(
        body, name=name + "_fwd", grid=(T // tile,),
        in_specs=_halo_specs(F2, tile, T) + [pl.BlockSpec((3, F2), lambda i: (0, 0)), pl.BlockSpec((1, F2), lambda i: (0, 0))],
        out_specs=[pl.BlockSpec((tile, F2), lambda i: (i, 0)), pl.BlockSpec((tile, F), lambda i: (i, 0))],
        out_shape=[jax.ShapeDtypeStruct((T, F2), F32), jax.ShapeDtypeStruct((T, F), BF16)],
        compiler_params=_params("parallel"),
    )(u, u, u, cw, cb)


def _conv_bwd(name, u, c, dg, cw, S, tile):
    T, F2 = u.shape
    F = F2 // 2
    per_seq = S // tile

    def dact(cv, dgv):
        a, v = cv[:, :F], cv[:, F:]
        sg = jax.nn.sigmoid(a)
        return jnp.concatenate([dgv * v * (sg * (1.0 + a * (1.0 - sg))), dgv * (a * sg)], axis=1)

    def body(u_ref, up_ref, un_ref, c_ref, cp_ref, cn_ref, g_ref, gp_ref, gn_ref, cw_ref, du_ref, dw0_ref, dw1_ref, dw2_ref, db_ref):
        i = pl.program_id(0)
        pos = i % per_seq
        first, last = pos == 0, pos == per_seq - 1
        lo, hi = slice(SUBLANES - 1, SUBLANES), slice(0, 1)
        uv = u_ref[...]
        u_dn, u_up = _shift_rows(uv, jnp.where(first, 0.0, up_ref[lo, :]), jnp.where(last, 0.0, un_ref[hi, :]), tile)
        dc = dact(c_ref[...], g_ref[...])
        dc_prev = jnp.where(first, 0.0, dact(cp_ref[lo, :], gp_ref[lo, :]))
        dc_next = jnp.where(last, 0.0, dact(cn_ref[hi, :], gn_ref[hi, :]))
        dc_dn, dc_up = _shift_rows(dc, dc_prev, dc_next, tile)
        du = dc_up * cw_ref[0:1, :]
        du = du + dc * cw_ref[1:2, :]
        du_ref[...] = (du + dc_dn * cw_ref[2:3, :]).astype(BF16)

        @pl.when(i == 0)
        def _():
            for ref in (dw0_ref, dw1_ref, dw2_ref, db_ref):
                ref[...] = jnp.zeros_like(ref)

        dw0_ref[...] += jnp.sum(dc * u_dn, axis=0, keepdims=True)
        dw1_ref[...] += jnp.sum(dc * uv, axis=0, keepdims=True)
        dw2_ref[...] += jnp.sum(dc * u_up, axis=0, keepdims=True)
        db_ref[...] += jnp.sum(dc, axis=0, keepdims=True)

    vec = pl.BlockSpec((1, F2), lambda i: (0, 0))
    du, dw0, dw1, dw2, db = pl.pallas_call(
        body, name=name + "_bwd", grid=(T // tile,),
        in_specs=_halo_specs(F2, tile, T) + _halo_specs(F2, tile, T) + _halo_specs(F, tile, T) + [pl.BlockSpec((3, F2), lambda i: (0, 0))],
        out_specs=[pl.BlockSpec((tile, F2), lambda i: (i, 0)), vec, vec, vec, vec],
        out_shape=[jax.ShapeDtypeStruct((T, F2), BF16)] + [jax.ShapeDtypeStruct((1, F2), F32)] * 4,
        compiler_params=_params("arbitrary"),
    )(u, u, u, c, c, c, dg, dg, dg, cw)
    return du, jnp.concatenate([dw0, dw1, dw2], axis=0), db


def make_ffn(name, S):
    def run(x, wup16, wdn16, cw, cb):
        u = _mm_nn(name + "_up", x, wup16)
        c, g16 = _conv_fwd(name + "_conv", u, cw, cb, S, ROW_TILE)
        return u, c, g16, _mm_nn(name + "_down", g16, wdn16)

    @jax.custom_vjp
    def op(x, wup16, cu32, cu16, wdn16, cd32, cd16, cw, cb):
        return run(x, wup16, wdn16, cw, cb)[3]

    def fwd(x, wup16, cu32, cu16, wdn16, cd32, cd16, cw, cb):
        u, c, g16, out = run(x, wup16, wdn16, cw, cb)
        return out, (x, wup16, wdn16, u, c, g16, cw)

    def bwd(res, dout):
        x, wup16, wdn16, u, c, g16, cw = res
        dg = _mm_nt(name + "_down_dx", dout, wdn16)
        dd32, dd16 = _weight_grads(name + "_down_dw", g16, dout, False)
        du16, dcw, dcb = _conv_bwd(name + "_conv", u, c, dg, cw, S, ROW_TILE // 2)
        du32, du16w = _weight_grads(name + "_up_dw", x, du16, True)
        return (_mm_nt(name + "_up_dx", du16, wup16), jnp.zeros_like(wup16), du32, du16w, jnp.zeros_like(wdn16), dd32, dd16, dcw, dcb)

    op.defvjp(fwd, bwd)
    return op


def _dot_nt(a, b):
    return lax.dot_general(a.astype(BF16), b.astype(BF16), (((1,), (1,)), ((), ())), preferred_element_type=F32)


def _dot_tn(a, b):
    return lax.dot_general(a.astype(BF16), b.astype(BF16), (((0,), (0,)), ((), ())), preferred_element_type=F32)


def _chunk_decays(la, direction, C, width, dk, const_lg):
    row = lax.broadcasted_iota(jnp.int32, (C, C), 0)
    col = lax.broadcasted_iota(jnp.int32, (C, C), 1)
    keep = (row >= col) if direction == 0 else (row <= col)
    ridx = lax.broadcasted_iota(jnp.int32, (C, 1), 0)
    if const_lg is None:
        cum = jnp.dot(keep.astype(F32), la, precision=HIGHEST, preferred_element_type=F32)
    else:
        lane_head = lax.broadcasted_iota(jnp.int32, (1, width), 1) // dk
        lg = jnp.zeros((1, width), F32)
        for h, val in enumerate(const_lg):
            lg = jnp.where(lane_head == h, val, lg)
        steps = (ridx + 1) if direction == 0 else (C - ridx)
        cum = steps.astype(F32) * lg
    exit_row = C - 1 if direction == 0 else 0
    mid = jnp.sum(jnp.where(ridx == C // 2, cum, 0.0), axis=0, keepdims=True)
    last = jnp.sum(jnp.where(ridx == exit_row, cum, 0.0), axis=0, keepdims=True)
    return keep, ridx == exit_row, cum, mid, last


def _scan_fwd(name, q, kf, kb, v, laf, lab, H, dk, dv, S, C, const_lg):
    T = q.shape[0]
    B, nc = T // S, S // C
    Wk, Wv = H * dk, H * dv
    learn = const_lg is None

    def body(*refs):
        if learn:
            qf_r, qb_r, kf_r, kb_r, vf_r, vb_r, laf_r, lab_r, of_r, ob_r, sf_r, sb_r, st = refs
            las = (laf_r[...], lab_r[...])
        else:
            qf_r, qb_r, kf_r, kb_r, vf_r, vb_r, of_r, ob_r, sf_r, sb_r, st = refs
            las = (None, None)

        @pl.when(pl.program_id(1) == 0)
        def _():
            st[...] = jnp.zeros_like(st)

        for d, (q_r, k_r, v_r, o_r, s_r) in enumerate(((qf_r, kf_r, vf_r, of_r, sf_r), (qb_r, kb_r, vb_r, ob_r, sb_r))):
            keep, _, cum, mid, last = _chunk_decays(las[d], d, C, Wk, dk, None if learn else const_lg[d])
            qe = q_r[...] * jnp.exp(cum - mid)
            ke = k_r[...] * jnp.exp(mid - cum)
            q_in = qe * jnp.exp(mid)
            k_out = ke * jnp.exp(last - mid)
            e_last = jnp.exp(last)
            vv = v_r[...]
            for h in range(H):
                ks, vs = slice(h * dk, (h + 1) * dk), slice(h * dv, (h + 1) * dv)
                a = jnp.where(keep, _dot_nt(qe[:, ks], ke[:, ks]), 0.0)
                state = st[d, h]
                o_r[:, vs] = _bdot(a, vv[:, vs]) + _dot_nt(q_in[:, ks], state)
                s_r[h * dv:(h + 1) * dv, :] = state
                st[d, h] = state * e_last[:, ks] + _dot_tn(vv[:, vs], k_out[:, ks])

    fpos = lambda b, c: (b * nc + c, 0)
    bpos = lambda b, c: (b * nc + nc - 1 - c, 0)
    kspec = lambda pos: pl.BlockSpec((C, Wk), pos)
    vspec = lambda pos: pl.BlockSpec((C, Wv), pos)
    sspec = lambda pos: pl.BlockSpec((None, Wv, dk), lambda b, c: pos(b, c) + (0,))
    ins = [q, q, kf, kb, v, v] + ([laf, lab] if learn else [])
    in_specs = [kspec(fpos), kspec(bpos), kspec(fpos), kspec(bpos), vspec(fpos), vspec(bpos)] + ([kspec(fpos), kspec(bpos)] if learn else [])
    return pl.pallas_call(
        body, name=name + "_fwd", grid=(B, nc), in_specs=in_specs,
        out_specs=[vspec(fpos), vspec(bpos), sspec(fpos), sspec(bpos)],
        out_shape=[jax.ShapeDtypeStruct((T, Wv), F32)] * 2 + [jax.ShapeDtypeStruct((B * nc, Wv, dk), F32)] * 2,
        scratch_shapes=[pltpu.VMEM((2, H, dv, dk), F32)],
        compiler_params=_params("parallel", "arbitrary"),
    )(*ins)


def _scan_bwd(name, q, kf, kb, v, laf, lab, sf, sb, dof, dob, H, dk, dv, S, C, const_lg):
    T = q.shape[0]
    B, nc = T // S, S // C
    Wk, Wv = H * dk, H * dv
    learn = const_lg is None

    def body(*refs):
        if learn:
            (qf_r, qb_r, kf_r, kb_r, vf_r, vb_r, laf_r, lab_r, sf_r, sb_r, dof_r, dob_r,
             dqf_r, dqb_r, dkf_r, dkb_r, dvf_r, dvb_r, dlaf_r, dlab_r, dst) = refs
            las, dlas = (laf_r[...], lab_r[...]), (dlaf_r, dlab_r)
        else:
            (qf_r, qb_r, kf_r, kb_r, vf_r, vb_r, sf_r, sb_r, dof_r, dob_r,
             dqf_r, dqb_r, dkf_r, dkb_r, dvf_r, dvb_r, dst) = refs
            las, dlas = (None, None), (None, None)

        @pl.when(pl.program_id(1) == 0)
        def _():
            dst[...] = jnp.zeros_like(dst)

        groups = ((qf_r, kf_r, vf_r, sf_r, dof_r, dqf_r, dkf_r, dvf_r), (qb_r, kb_r, vb_r, sb_r, dob_r, dqb_r, dkb_r, dvb_r))
        for d, (q_r, k_r, v_r, s_r, do_r, dq_r, dk_r, dv_r) in enumerate(groups):
            keep, is_exit, cum, mid, last = _chunk_decays(las[d], d, C, Wk, dk, None if learn else const_lg[d])
            eq, ek = jnp.exp(cum - mid), jnp.exp(mid - cum)
            e_in, e_out, e_last = jnp.exp(mid), jnp.exp(last - mid), jnp.exp(last)
            qe, ke = q_r[...] * eq, k_r[...] * ek
            q_in, k_out = qe * e_in, ke * e_out
            vv, do = v_r[...], do_r[...]
            dqe_parts, dke_parts, dlast_parts = [], [], []
            for h in range(H):
                ks, vs = slice(h * dk, (h + 1) * dk), slice(h * dv, (h + 1) * dv)
                a = jnp.where(keep, _dot_nt(qe[:, ks], ke[:, ks]), 0.0)
                dp = jnp.where(keep, _dot_nt(do[:, vs], vv[:, vs]), 0.0)
                s_prev = s_r[h * dv:(h + 1) * dv, :]
                ds = dst[d, h]
                dk_out = _bdot(vv[:, vs], ds)
                dqe_parts.append(_bdot(dp, ke[:, ks]) + _bdot(do[:, vs], s_prev) * e_in[:, ks])
                dke_parts.append(_dot_tn(dp, qe[:, ks]) + dk_out * e_out[:, ks])
                dv_r[:, vs] = _dot_tn(a, do[:, vs]) + _dot_nt(k_out[:, ks], ds)
                if learn:
                    dlast_parts.append(jnp.sum(dk_out * k_out[:, ks], axis=0, keepdims=True)
                                       + jnp.sum(ds * s_prev, axis=0, keepdims=True) * e_last[:, ks])
                dst[d, h] = ds * e_last[:, ks] + _dot_tn(do[:, vs], q_in[:, ks])
            dqe = jnp.concatenate(dqe_parts, axis=1)
            dke = jnp.concatenate(dke_parts, axis=1)
            dq_r[...] = dqe * eq
            dk_r[...] = dke * ek
            if learn:
                dcum = dqe * qe - dke * ke + jnp.where(is_exit, jnp.concatenate(dlast_parts, axis=1), 0.0)
                dlas[d][...] = lax.dot_general(keep.astype(F32), dcum, (((0,), (0,)), ((), ())), precision=HIGHEST,
                                               preferred_element_type=F32)

    fpos = lambda b, c: (b * nc + nc - 1 - c, 0)
    bpos = lambda b, c: (b * nc + c, 0)
    kspec = lambda pos: pl.BlockSpec((C, Wk), pos)
    vspec = lambda pos: pl.BlockSpec((C, Wv), pos)
    sspec = lambda pos: pl.BlockSpec((None, Wv, dk), lambda b, c: pos(b, c) + (0,))
    ins = [q, q, kf, kb, v, v] + ([laf, lab] if learn else []) + [sf, sb, dof, dob]
    in_specs = ([kspec(fpos), kspec(bpos), kspec(fpos), kspec(bpos), vspec(fpos), vspec(bpos)]
                + ([kspec(fpos), kspec(bpos)] if learn else []) + [sspec(fpos), sspec(bpos), vspec(fpos), vspec(bpos)])
    n_k = 6 if learn else 4
    out_specs = [kspec(fpos), kspec(bpos), kspec(fpos), kspec(bpos), vspec(fpos), vspec(bpos)] + ([kspec(fpos), kspec(bpos)] if learn else [])
    out_shape = [jax.ShapeDtypeStruct((T, Wk), F32)] * 4 + [jax.ShapeDtypeStruct((T, Wv), F32)] * 2 + ([jax.ShapeDtypeStruct((T, Wk), F32)] * 2 if learn else [])
    del n_k
    return pl.pallas_call(
        body, name=name + "_bwd", grid=(B, nc), in_specs=in_specs, out_specs=out_specs, out_shape=out_shape,
        scratch_shapes=[pltpu.VMEM((2, H, dv, dk), F32)],
        compiler_params=_params("parallel", "arbitrary"),
    )(*ins)


def make_scan(name, H, dk, dv, S, C=SCAN_CHUNK, const_lg=None):
    if const_lg is None:
        @jax.custom_vjp
        def op(q, kf, kb, v, laf, lab):
            return tuple(_scan_fwd(name, q, kf, kb, v, laf, lab, H, dk, dv, S, C, None)[:2])

        def fwd(q, kf, kb, v, laf, lab):
            of, ob, sf, sb = _scan_fwd(name, q, kf, kb, v, laf, lab, H, dk, dv, S, C, None)
            return (of, ob), (q, kf, kb, v, laf, lab, sf, sb)

        def bwd(res, g):
            q, kf, kb, v, laf, lab, sf, sb = res
            dqf, dqb, dkf, dkb, dvf, dvb, dlaf, dlab = _scan_bwd(name, q, kf, kb, v, laf, lab, sf, sb, g[0], g[1], H, dk, dv, S, C, None)
            return dqf + dqb, dkf, dkb, dvf + dvb, dlaf, dlab
    else:
        @jax.custom_vjp
        def op(q, kf, kb, v):
            return tuple(_scan_fwd(name, q, kf, kb, v, None, None, H, dk, dv, S, C, const_lg)[:2])

        def fwd(q, kf, kb, v):
            of, ob, sf, sb = _scan_fwd(name, q, kf, kb, v, None, None, H, dk, dv, S, C, const_lg)
            return (of, ob), (q, kf, kb, v, sf, sb)

        def bwd(res, g):
            q, kf, kb, v, sf, sb = res
            dqf, dqb, dkf, dkb, dvf, dvb = _scan_bwd(name, q, kf, kb, v, None, None, sf, sb, g[0], g[1], H, dk, dv, S, C, const_lg)
            return dqf + dqb, dkf, dkb, dvf + dvb

    op.defvjp(fwd, bwd)
    return op


def _s5_scan_call(name, Xs, A, S, dirs):
    T, N = Xs[0].shape
    B, nl = T // S, N // LANES
    n_it = S // S5_SEGMENTS
    assert n_it & (n_it - 1) == 0

    def cmul(ar, ai, br, bi):
        return ar * br - ai * bi, ar * bi + ai * br

    def body(x0r, x0i, x1r, x1i, a_ref, h0r, h0i, h1r, h1i):
        seg = lax.broadcasted_iota(jnp.int32, (S5_SEGMENTS, 1), 0)
        for k, (xr, xi, hr, hi) in enumerate(((x0r, x0i, h0r, h0i), (x1r, x1i, h1r, h1i))):
            back = dirs[k] == 1
            ar = jnp.broadcast_to(a_ref[2 * k:2 * k + 1, :], (S5_SEGMENTS, LANES))
            ai = jnp.broadcast_to(a_ref[2 * k + 1:2 * k + 2, :], (S5_SEGMENTS, LANES))
            rows_of = lambda i: pl.ds((n_it - 1 - i) if back else i, S5_SEGMENTS, stride=n_it)

            def local(i, carry):
                sr, si = carry
                rows = rows_of(i)
                pr, pi = cmul(ar, ai, sr, si)
                sr, si = pr + xr[rows, :], pi + xi[rows, :]
                hr[rows, :] = sr
                hi[rows, :] = si
                return sr, si

            zero = jnp.zeros((S5_SEGMENTS, LANES), F32)
            er, ei = lax.fori_loop(0, n_it, local, (zero, zero), unroll=8)
            pr, pi = ar, ai
            for _ in range(n_it.bit_length() - 1):
                pr, pi = cmul(pr, pi, pr, pi)
            shift = (S5_SEGMENTS - 1) if back else 1
            tr, ti = er, ei
            order = range(S5_SEGMENTS - 2, -1, -1) if back else range(1, S5_SEGMENTS)
            for r in order:
                nr, ni = cmul(pr, pi, pltpu.roll(tr, shift, 0), pltpu.roll(ti, shift, 0))
                tr = jnp.where(seg == r, er + nr, tr)
                ti = jnp.where(seg == r, ei + ni, ti)
            edge = (S5_SEGMENTS - 1) if back else 0
            cr = jnp.where(seg == edge, 0.0, pltpu.roll(tr, shift, 0))
            ci = jnp.where(seg == edge, 0.0, pltpu.roll(ti, shift, 0))

            def fix(i, carry):
                wr, wi = carry
                rows = rows_of(i)
                fr, fi = cmul(wr, wi, cr, ci)
                hr[rows, :] = hr[rows, :] + fr
                hi[rows, :] = hi[rows, :] + fi
                return cmul(wr, wi, ar, ai)

            lax.fori_loop(0, n_it, fix, (ar, ai), unroll=8)

    col = pl.BlockSpec((S, LANES), lambda b, j: (b, j))
    return tuple(pl.pallas_call(
        body, name=name, grid=(B, nl),
        in_specs=[col] * 4 + [pl.BlockSpec((4, LANES), lambda b, j: (0, j))],
        out_specs=[col] * 4,
        out_shape=[jax.ShapeDtypeStruct((T, N), F32)] * 4,
        compiler_params=_params("parallel", "parallel"),
    )(*Xs, A))


def _s5_decay_grad_call(name, lam, H, X, tile=ROW_TILE):
    T, N = X[0].shape

    def body(*refs):
        l_refs, h_refs, x_refs, p_ref = refs[0:4], refs[4:8], refs[8:12], refs[12]

        @pl.when(pl.program_id(0) == 0)
        def _():
            p_ref[...] = jnp.zeros_like(p_ref)

        for k in range(2):
            re, im = 2 * k, 2 * k + 1
            ur, ui = h_refs[re][...] - x_refs[re][...], h_refs[im][...] - x_refs[im][...]
            lr, li = l_refs[re][...], l_refs[im][...]
            p_ref[re:re + 1, :] += jnp.sum(lr * ur + li * ui, axis=0, keepdims=True)
            p_ref[im:im + 1, :] += jnp.sum(li * ur - lr * ui, axis=0, keepdims=True)

    row = pl.BlockSpec((tile, N), lambda i: (i, 0))
    return pl.pallas_call(
        body, name=name, grid=(T // tile,), in_specs=[row] * 12,
        out_specs=pl.BlockSpec((4, N), lambda i: (0, 0)), out_shape=jax.ShapeDtypeStruct((4, N), F32),
        compiler_params=_params("arbitrary"),
    )(*lam, *H, *X)


def make_s5_scan(name, S):
    @jax.custom_vjp
    def op(X, A):
        return _s5_scan_call(name + "_fwd", X, A, S, (0, 1))

    def fwd(X, A):
        H = _s5_scan_call(name + "_fwd", X, A, S, (0, 1))
        return H, (X, A, H)

    def bwd(res, G):
        X, A, H = res
        conj = A * jnp.array([[1.0], [-1.0], [1.0], [-1.0]], F32)
        lam = _s5_scan_call(name + "_bwd", tuple(G), conj, S, (1, 0))
        P = _s5_decay_grad_call(name + "_dA", lam, H, X)
        ar, ai = A[0::2], A[1::2]
        pr, pi = P[0::2], P[1::2]
        den = ar * ar + ai * ai
        dar, dai = (pr * ar - pi * ai) / den, (pr * ai + pi * ar) / den
        return lam, jnp.stack([dar[0], dai[0], dar[1], dai[1]], axis=0)

    op.defvjp(fwd, bwd)
    return op


ANY = pl.BlockSpec(memory_space=pl.ANY)


def _place():
    x, y, c = lax.axis_index("x"), lax.axis_index("y"), lax.axis_index("c")
    return x, y, c, [(1 - x, y), (x, 1 - y), (1 - x, 1 - y)]


def all_gather(name, arrs):
    n = len(arrs)

    def body(*refs):
        ins, outs = refs[:n], refs[n:2 * n]
        send, recv, lsem = refs[2 * n:]
        x, y, c, chips = _place()
        me, sibling = (x, y, c), (x, y, 1 - c)

        def copy(a, k, block, to, src=None):
            slot = outs[a].at[4 * block[0] + 2 * block[1] + block[2]]
            return pltpu.make_async_remote_copy(src_ref=slot if src is None else src, dst_ref=slot, send_sem=send.at[a, k],
                                                recv_sem=recv.at[a, k], device_id=to, device_id_type=MESH)

        mine = [pltpu.make_async_copy(ins[a], outs[a].at[4 * x + 2 * y + c], lsem.at[a]) for a in range(n)]
        first = []
        for a in range(n):
            mine[a].start()
            first.append(copy(a, 0, me, sibling, src=ins[a]))
            first += [copy(a, 1 + j, me, (*chip, c), src=ins[a]) for j, chip in enumerate(chips)]
        for cp in first:
            cp.start()
        passed = []
        for a in range(n):
            for j, chip in enumerate(chips):
                copy(a, 1 + j, (*chip, c), me).wait_recv()
                fwd = copy(a, 4 + j, (*chip, c), sibling)
                fwd.start()
                passed.append(fwd)
        for a in range(n):
            copy(a, 0, sibling, me).wait_recv()
            for j, chip in enumerate(chips):
                copy(a, 4 + j, (*chip, 1 - c), me).wait_recv()
        for cp in first + passed:
            cp.wait_send()
        for cp in mine:
            cp.wait()

    return pl.pallas_call(
        body, name=name, in_specs=[ANY] * n, out_specs=[ANY] * n,
        out_shape=[jax.ShapeDtypeStruct((N_DEV,) + a.shape, a.dtype) for a in arrs],
        scratch_shapes=[pltpu.SemaphoreType.DMA((n, 7)), pltpu.SemaphoreType.DMA((n, 7)), pltpu.SemaphoreType.DMA((n,))],
    )(*arrs)


HBM = pl.BlockSpec(memory_space=pltpu.HBM)
SEM = pl.BlockSpec(memory_space=pltpu.SEMAPHORE)
EFFECT = pltpu.SideEffectType.DATAFLOW_SIDE_EFFECTING
N_PEERS = N_DEV - 1


def _spread_copies(srcs, lands, send, recv, scatter):
    x, y, c = lax.axis_index("x"), lax.axis_index("y"), lax.axis_index("c")
    me = 4 * x + 2 * y + c
    copies = []
    for a, (src, land) in enumerate(zip(srcs, lands, strict=True)):
        for k in range(1, N_DEV):
            px, py, pc = (1 - x if k & 4 else x), (1 - y if k & 2 else y), (1 - c if k & 1 else c)
            copies.append(pltpu.make_async_remote_copy(
                src_ref=src.at[4 * px + 2 * py + pc] if scatter else src, dst_ref=land.at[me],
                send_sem=send.at[a * N_PEERS + k - 1], recv_sem=recv.at[a * N_PEERS + k - 1],
                device_id=(px, py, pc), device_id_type=MESH))
    return copies


def _arrival_waits(srcs, lands, send, recv, scatter):
    x, y, c = lax.axis_index("x"), lax.axis_index("y"), lax.axis_index("c")
    waits = []
    for a, (src, land) in enumerate(zip(srcs, lands, strict=True)):
        for k in range(1, N_DEV):
            px, py, pc = (1 - x if k & 4 else x), (1 - y if k & 2 else y), (1 - c if k & 1 else c)
            peer = 4 * px + 2 * py + pc
            waits.append(pltpu.make_async_remote_copy(
                src_ref=src.at[peer] if scatter else src, dst_ref=land.at[peer],
                send_sem=send.at[a * N_PEERS + k - 1], recv_sem=recv.at[a * N_PEERS + k - 1],
                device_id=(px, py, pc), device_id_type=MESH))
    return waits


def spread_start(name, srcs, scatter, after):
    n = len(srcs)
    lands = [lax.empty((N_DEV,) + (s.shape[1:] if scatter else s.shape), s.dtype) for s in srcs]

    def body(*refs):
        ins, lnd = refs[:n], refs[n:2 * n]
        send, recv = refs[2 * n + 1], refs[2 * n + 2]
        token = refs[-1]
        for cp in _spread_copies(ins, lnd, send, recv, scatter):
            cp.start()
        token[...] = jnp.zeros_like(token)

    return pl.pallas_call(
        body, name=name,
        out_shape=(pltpu.SemaphoreType.DMA((N_PEERS * n,)), pltpu.SemaphoreType.DMA((N_PEERS * n,)))
        + tuple(pltpu.HBM(s.shape, s.dtype) for s in srcs) + tuple(pltpu.HBM(l.shape, l.dtype) for l in lands)
        + (jax.ShapeDtypeStruct((SUBLANES, LANES), F32),),
        in_specs=[HBM] * (2 * n) + [ANY],
        out_specs=(SEM, SEM) + (HBM,) * (2 * n) + (pl.BlockSpec(memory_space=pltpu.VMEM),),
        input_output_aliases={i: 2 + i for i in range(2 * n)},
        compiler_params=pltpu.CompilerParams(has_side_effects=EFFECT),
    )(*[pltpu.with_memory_space_constraint(s, pltpu.HBM) for s in srcs],
      *[pltpu.with_memory_space_constraint(l, pltpu.HBM) for l in lands], after)


def spread_wait(name, handle, scatter, after):
    send, recv = handle[0], handle[1]
    rest = handle[2:-1]
    n = len(rest) // 2
    srcs, lands = rest[:n], rest[n:]

    def body(*refs):
        ins, lnd = refs[:n], refs[n:2 * n]
        s, r = refs[2 * n], refs[2 * n + 1]
        for cp in _arrival_waits(ins, lnd, s, r, scatter):
            cp.wait_send()
            cp.wait_recv()

    outs = pl.pallas_call(
        body, name=name,
        out_shape=tuple(pltpu.HBM(s.shape, s.dtype) for s in srcs) + tuple(pltpu.HBM(l.shape, l.dtype) for l in lands),
        in_specs=[HBM] * (2 * n) + [SEM, SEM, ANY], out_specs=(HBM,) * (2 * n),
        input_output_aliases={i: i for i in range(2 * n)},
        compiler_params=pltpu.CompilerParams(has_side_effects=EFFECT),
    )(*srcs, *lands, send, recv, after)
    return list(outs[n:])


def _row_tile(rows, cols):
    cap = max(SUBLANES, (2**18 // cols) // SUBLANES * SUBLANES)
    if rows <= cap:
        return rows
    for t in range(cap, SUBLANES - 1, -SUBLANES):
        if rows % t == 0:
            return t
    return rows


def ordered_sum(name, parts):
    n, R, C = parts.shape
    rt = _row_tile(R, C)

    def body(p_ref, o_ref):
        s = p_ref[0]
        for k in range(1, n):
            s = s + p_ref[k]
        o_ref[...] = s

    return pl.pallas_call(
        body, name=name, grid=(R // rt,),
        in_specs=[pl.BlockSpec((n, rt, C), lambda r: (0, r, 0))], out_specs=pl.BlockSpec((rt, C), lambda r: (r, 0)),
        out_shape=jax.ShapeDtypeStruct((R, C), F32), compiler_params=_params("parallel"),
    )(parts)


def _adamw_update(w, m, v, g):
    bias1 = 1.0 - ADAM_B1 ** ADAM_STEP
    bias2 = 1.0 - ADAM_B2 ** ADAM_STEP
    m_new = ADAM_B1 * m + (1.0 - ADAM_B1) * g
    v_new = ADAM_B2 * v + (1.0 - ADAM_B2) * (g * g)
    delta = -ADAM_LR * ((m_new / bias1) / (jnp.sqrt(v_new / bias2) + ADAM_EPS) + ADAM_WD * w)
    return delta, m_new, v_new


def adamw(name, w, m, v, g):
    R, C = w.shape
    rt = _row_tile(R, C)

    def body(w_ref, m_ref, v_ref, g_ref, d_ref, mo_ref, vo_ref):
        d_ref[...], mo_ref[...], vo_ref[...] = _adamw_update(w_ref[...], m_ref[...], v_ref[...], g_ref[...])

    row = pl.BlockSpec((rt, C), lambda r: (r, 0))
    return pl.pallas_call(
        body, name=name, grid=(R // rt,), in_specs=[row] * 4, out_specs=[row] * 3,
        out_shape=[jax.ShapeDtypeStruct((R, C), F32)] * 3, compiler_params=_params("parallel"),
    )(w, m, v, g)


def adamw_sharded(name, layer, w, m, v, own, land, me, prev):
    _, R, C = own.shape
    rt = _row_tile(R, C)
    base = layer * (R // rt)

    def body(me_ref, w_ref, m_ref, v_ref, own_ref, land_ref, *rest):
        go_ref, d_ref, mo_ref, vo_ref = rest[-4:]
        g = own_ref[...]
        for k in range(N_DEV):
            g = g + jnp.where(me_ref[0] == k, 0.0, land_ref[k].astype(F32))
        go_ref[...] = g
        d_ref[...], mo_ref[...], vo_ref[...] = _adamw_update(w_ref[...], m_ref[...], v_ref[...], g)

    row = pl.BlockSpec((rt, C), lambda r, p: (base + r, 0))
    in_specs = [row, row, row, pl.BlockSpec((None, rt, C), lambda r, p: (p[0], r, 0)), pl.BlockSpec((N_DEV, rt, C), lambda r, p: (0, r, 0))]
    ins = [me, w, m, v, own, land]
    aliases = {}
    if prev is not None:
        in_specs += [ANY] * 4
        aliases = {len(ins) + k: k for k in range(4)}
        ins += list(prev)
    return pl.pallas_call(
        body, name=name,
        grid_spec=pltpu.PrefetchScalarGridSpec(num_scalar_prefetch=1, grid=(R // rt,), in_specs=in_specs, out_specs=[row] * 4),
        out_shape=[jax.ShapeDtypeStruct(w.shape, F32)] * 4, input_output_aliases=aliases,
        compiler_params=_params("arbitrary"),
    )(*ins)


def _hgrn_lower_bounds(lb_logits):
    p = jax.nn.softmax(lb_logits, axis=1)
    return jnp.cumsum(p, axis=1) - p[:, :1]


def _s5_discretise(lam_re, lam_im, log_dt, b_re, b_im):
    lr = jnp.minimum(lam_re, -1e-4)
    li = lam_im
    dt = jnp.exp(log_dt)[:, None]
    mag = jnp.exp(lr * dt)
    ar, ai = mag * jnp.cos(li * dt), mag * jnp.sin(li * dt)
    den = lr * lr + li * li
    nr = ar - 1.0
    cr = (nr * lr + ai * li) / den
    ci = (ai * lr - nr * li) / den
    bbr = cr[..., None] * b_re - ci[..., None] * b_im
    bbi = cr[..., None] * b_im + ci[..., None] * b_re
    return ar, ai, bbr, bbi


def _block_diag(t):
    G, a, b = t.shape
    eye = jnp.eye(G, dtype=F32)
    return (t[:, :, None, :] * eye[:, None, :, None]).reshape(G * a, G * b)


def _rope_tables(S):
    half = RET_DK // 2
    inv = ROPE_BASE ** (-jnp.arange(half, dtype=F32) / half)
    ang = jnp.arange(S, dtype=F32)[:, None] * inv[None, :]
    cos, sin = jnp.cos(ang), jnp.sin(ang)
    return jnp.concatenate([cos, cos], axis=1), jnp.concatenate([-sin, sin], axis=1)


def _ret_log_decays():
    f = tuple(float(np.log1p(-np.exp2(np.float32(-5.0 - h)))) for h in range(RET_HEADS))
    b = tuple(float(np.log1p(-np.exp2(np.float32(-5.5 - h)))) for h in range(RET_HEADS))
    return f, b


def assemble_weight(name, land, own, me, col_sharded):
    _, R, C = land.shape
    if col_sharded:
        tr = min(R, 256)

        def body(me_ref, land_ref, own_ref, o_ref):
            for d in range(N_DEV):
                o_ref[:, d * C:(d + 1) * C] = jnp.where(me_ref[0] == d, own_ref[...], land_ref[d])

        grid, out_shape = (R // tr,), (R, N_DEV * C)
        in_specs = [pl.BlockSpec((N_DEV, tr, C), lambda i, p: (0, i, 0)), pl.BlockSpec((tr, C), lambda i, p: (i, 0))]
        out_spec = pl.BlockSpec((tr, N_DEV * C), lambda i, p: (i, 0))
    else:
        def body(me_ref, land_ref, own_ref, o_ref):
            o_ref[...] = jnp.where(me_ref[0] == pl.program_id(0), own_ref[...], land_ref[...])

        grid, out_shape = (N_DEV,), (N_DEV * R, C)
        in_specs = [pl.BlockSpec((None, R, C), lambda d, p: (d, 0, 0)), pl.BlockSpec((R, C), lambda d, p: (0, 0))]
        out_spec = pl.BlockSpec((R, C), lambda d, p: (d, 0))
    return pl.pallas_call(
        body, name=name,
        grid_spec=pltpu.PrefetchScalarGridSpec(num_scalar_prefetch=1, grid=grid, in_specs=in_specs, out_specs=out_spec),
        out_shape=jax.ShapeDtypeStruct(out_shape, land.dtype), compiler_params=_params("parallel"),
    )(me, land, own)


def _row(t):
    return t.reshape(1, -1)


def mixer_stage(layer, S, resid, branch, w_in, w_out, carriers, small):
    j = layer // 2
    tag = f"l{layer}"
    g = _row(small["mix_norm_g"][layer])
    if branch is None:
        x = resid
        (h,) = make_rowop(tag + "_norm", norm_f, (D_MODEL,))((x,), (g,), ())
    else:
        x, h = make_rowop(tag + "_addnorm", addnorm_f, (D_MODEL, D_MODEL))((resid, branch), (g,), ())
    if layer % 2 == 0:
        lbs = _hgrn_lower_bounds(small["hgrn_lb_logits"])
        prm = (small["gla_wa2"][j, 0], small["gla_wa2"][j, 1], _row(small["gla_ba"][j, 0]), _row(small["gla_ba"][j, 1]),
               _row(lbs[0, j]), _row(lbs[1, j]))
        gq, gk, gv, glaf, glab, gr, hq, hkf, hkb, hv, hlaf, hlab, hg = make_proj_stage(tag + "_prep", even_prep_f, EVEN_PREP_WIDTHS)(
            h, w_in, carriers["in32"], carriers["in16"], prm, ())
        of, ob = make_scan(tag + "_gla", GLA_HEADS, GLA_DK, GLA_DV, S, 2 * SCAN_CHUNK)(gq, gk, gk, gv, glaf, glab)
        hof, hob = make_scan(tag + "_hgrn", HGRN_HEADS, HGRN_DK, HGRN_DV, S, SCAN_CHUNK)(hq, hkf, hkb, hv, hlaf, hlab)
        (y,) = make_rowop(tag + "_post", even_post_f, (D_MODEL,))(
            (of, ob, hof, hob, gr, hg), (_row(small["gla_norm_g"][j]), _row(small["hgrn_norm_g"][j])), ())
    else:
        cosf, sinf = _rope_tables(S)
        rq, rk, rv, rg, su = make_proj_stage(tag + "_prep", odd_prep_f, ODD_PREP_WIDTHS)(
            h, w_in, carriers["in32"], carriers["in16"], (), (cosf, sinf))
        of, ob = make_scan(tag + "_ret", RET_HEADS, RET_DK, RET_DV, S, 2 * SCAN_CHUNK, const_lg=_ret_log_decays())(rq, rk, rk, rv)
        (cm,) = make_rowop(tag + "_retpost", ret_post_f, (RET_HEADS * RET_DV,))((of, ob, rg), (_row(small["ret_norm_g"][j]),), ())
        disc = [_s5_discretise(small["s5_lam_re"][j, d], small["s5_lam_im"][j, d], small["s5_log_dt"][j, d],
                               small["s5_b_re"][j], small["s5_b_im"][j]) for d in range(2)]
        a4 = jnp.stack([t.reshape(-1) for d in range(2) for t in disc[d][:2]], axis=0)
        Xs = tuple(make_mm_f32w(f"{tag}_s5in{2 * d + i}")(su, _block_diag(jnp.swapaxes(disc[d][2 + i], 1, 2)))
                   for d in range(2) for i in range(2))
        Hs = make_s5_scan(tag + "_s5scan", S)(Xs, a4)
        prm = (_block_diag(jnp.swapaxes(small["s5_c_re"][j], 1, 2)), _block_diag(jnp.swapaxes(small["s5_c_im"][j], 1, 2)),
               _row(small["s5_d"][j]), small["s5_glu_w"][j], _row(small["s5_glu_b"][j]))
        (dm,) = make_rowop(tag + "_s5post", s5_post_f, (S5_WIDTH,))((*Hs, su), prm, ())
        y = jnp.concatenate([cm, dm], axis=1)
    return x, make_mm(tag + "_out")(y, w_out, carriers["out32"], carriers["out16"])


def ffn_stage(layer, S, resid, branch, w_up, w_down, carriers, small):
    tag = f"l{layer}"
    x, hf = make_rowop(tag + "_ffnnorm", addnorm_f, (D_MODEL, D_MODEL))((resid, branch), (_row(small["ffn_norm_g"][layer]),), ())
    out = make_ffn(tag + "_ffn", S)(hf, w_up, carriers["up32"], carriers["up16"], w_down, carriers["down32"], carriers["down16"],
                                   small["ffn_conv_w"][layer], _row(small["ffn_conv_b"][layer]))
    return x, out


BIG = {"w_in_even": 2, "w_out_even": 1, "w_in_odd": 2, "w_out_odd": 1, "ffn_w_up": 2, "ffn_w_down": 1}
SMALL_SHARDED = {"gla_wa2": 3, "gla_ba": 2, "hgrn_lb_logits": 2, "ret_norm_g": 1, "s5_d": 1, "s5_glu_w": 1, "s5_glu_b": 1,
                 "ffn_conv_w": 2}
REPLICATED = ("mix_norm_g", "ffn_norm_g", "final_norm_g", "gla_norm_g", "hgrn_norm_g", "s5_lam_re", "s5_lam_im", "s5_log_dt",
              "s5_b_re", "s5_b_im", "s5_c_re", "s5_c_im", "ffn_conv_b")
WEIGHTS = ("mix_norm_g", "ffn_norm_g", "final_norm_g", "w_in_even", "w_out_even", "gla_wa2", "gla_ba", "gla_norm_g",
           "hgrn_lb_logits", "hgrn_norm_g", "w_in_odd", "w_out_odd", "ret_norm_g", "s5_lam_re", "s5_lam_im", "s5_log_dt",
           "s5_b_re", "s5_b_im", "s5_c_re", "s5_c_im", "s5_d", "s5_glu_w", "s5_glu_b", "ffn_w_up", "ffn_conv_w", "ffn_conv_b",
           "ffn_w_down")
PACK_COLS = 512
MIXER_SMALL = (("mix_norm_g", "hgrn_lb_logits", "gla_wa2", "gla_ba", "gla_norm_g", "hgrn_norm_g"),
               ("mix_norm_g", "ret_norm_g", "s5_lam_re", "s5_lam_im", "s5_log_dt", "s5_b_re", "s5_b_im", "s5_c_re", "s5_c_im",
                "s5_d", "s5_glu_w", "s5_glu_b"))
FFN_SMALL = ("ffn_norm_g", "ffn_conv_w", "ffn_conv_b")


def _unshard(g, axis):
    t = jnp.moveaxis(g, 0, axis)
    return t.reshape(t.shape[:axis] + (t.shape[axis] * t.shape[axis + 1],) + t.shape[axis + 2:])


def _to_blocks(full, axis):
    t = full.reshape(full.shape[:axis] + (N_DEV, full.shape[axis] // N_DEV) + full.shape[axis + 1:])
    return jnp.moveaxis(t, axis, 0)


def _pack(arrs):
    flat = jnp.concatenate([a.reshape(-1) for a in arrs])
    pad = (-flat.shape[0]) % (PACK_COLS * SUBLANES)
    return jnp.pad(flat, (0, pad)).reshape(-1, PACK_COLS)


def _unpack(packed, shapes):
    flat = packed.reshape(-1)
    out, o = [], 0
    for s in shapes:
        n = int(np.prod(s))
        out.append(flat[o:o + n].reshape(s))
        o += n
    return out


def _flat2d(a):
    return a.reshape(-1, a.shape[-1])


def kernel(x, mix_norm_g, ffn_norm_g, final_norm_g, w_in_even, w_out_even, gla_wa2, gla_ba, gla_norm_g, hgrn_lb_logits, hgrn_norm_g, w_in_odd, w_out_odd, ret_norm_g, s5_lam_re, s5_lam_im, s5_log_dt, s5_b_re, s5_b_im, s5_c_re, s5_c_im, s5_d, s5_glu_w, s5_glu_b, ffn_w_up, ffn_conv_w, ffn_conv_b, ffn_w_down, loss_target, m_mix_norm_g, m_ffn_norm_g, m_final_norm_g, m_w_in_even, m_w_out_even, m_gla_wa2, m_gla_ba, m_gla_norm_g, m_hgrn_lb_logits, m_hgrn_norm_g, m_w_in_odd, m_w_out_odd, m_ret_norm_g, m_s5_lam_re, m_s5_lam_im, m_s5_log_dt, m_s5_b_re, m_s5_b_im, m_s5_c_re, m_s5_c_im, m_s5_d, m_s5_glu_w, m_s5_glu_b, m_ffn_w_up, m_ffn_conv_w, m_ffn_conv_b, m_ffn_w_down, v_mix_norm_g, v_ffn_norm_g, v_final_norm_g, v_w_in_even, v_w_out_even, v_gla_wa2, v_gla_ba, v_gla_norm_g, v_hgrn_lb_logits, v_hgrn_norm_g, v_w_in_odd, v_w_out_odd, v_ret_norm_g, v_s5_lam_re, v_s5_lam_im, v_s5_log_dt, v_s5_b_re, v_s5_b_im, v_s5_c_re, v_s5_c_im, v_s5_d, v_s5_glu_w, v_s5_glu_b, v_ffn_w_up, v_ffn_conv_w, v_ffn_conv_b, v_ffn_w_down):
    args = locals()
    w = {n: args[n] for n in WEIGHTS}
    m = {n: args["m_" + n] for n in WEIGHTS}
    v = {n: args["v_" + n] for n in WEIGHTS}
    Bl, S, D = x.shape
    T = Bl * S
    ix, iy, ic = lax.axis_index("x"), lax.axis_index("y"), lax.axis_index("c")
    me = 4 * ix + 2 * iy + ic

    xt = x.reshape(T, D)
    me1 = me.reshape(1).astype(jnp.int32)
    stages = []
    for layer in range(DEPTH):
        j = layer // 2
        kin, kout = ("w_in_even", "w_out_even") if layer % 2 == 0 else ("w_in_odd", "w_out_odd")
        stages.append((mixer_stage, layer, ("in", "out"), ((kin, j, True), (kout, j, False)), MIXER_SMALL[layer % 2]))
        stages.append((ffn_stage, layer, ("up", "down"), (("ffn_w_up", layer, True), ("ffn_w_down", layer, False)), FFN_SMALL))

    gather, after = [], xt
    for s, (_, _, _, projs, _) in enumerate(stages):
        handle = spread_start(f"gather{s}_start", [w[n][l].astype(BF16) for n, l, _ in projs], False, after)
        gather.append(handle)
        after = handle[-1]
    sm_names = list(SMALL_SHARDED)
    (sm_all8,) = all_gather("gather_small", [_pack([w[n] for n in sm_names])])
    sm_flat, small, o = sm_all8.reshape(N_DEV, -1), {}, 0
    for n in sm_names:
        size = int(np.prod(w[n].shape))
        small[n] = _unshard(sm_flat[:, o:o + size].reshape((N_DEV,) + w[n].shape), SMALL_SHARDED[n])
        o += size
    small.update({n: w[n] for n in REPLICATED})

    resid, branch, pulls = xt, None, []
    for s, (fn, layer, keys, projs, sm_keys) in enumerate(stages):
        lands = spread_wait(f"gather{s}_wait", gather[s], False, lax.stop_gradient(resid))
        full, carriers = [], {}
        for key, land, (n, l, col) in zip(keys, lands, projs, strict=True):
            full.append(assemble_weight(f"weight{s}_{key}", land, w[n][l].astype(BF16), me1, col))
            carriers[key + "32"] = jnp.zeros(land.shape, F32)
            carriers[key + "16"] = jnp.zeros(land.shape, BF16)
        sm = {n: small[n] for n in sm_keys}
        run = functools.partial(fn, layer, S)
        if branch is None:
            (resid, branch), pull = jax.vjp(lambda r, c, p, run=run, full=full: run(r, None, full[0], full[1], c, p), resid, carriers, sm)
        else:
            (resid, branch), pull = jax.vjp(lambda r, b, c, p, run=run, full=full: run(r, b, full[0], full[1], c, p), resid, branch, carriers, sm)
        pulls.append(pull)

    loss_acc, dxf, dgf = loss_head(resid, branch, small["final_norm_g"].reshape(1, D), loss_target.reshape(T, D))
    loss = lax.psum(loss_acc[0, 0], ("x", "y", "c"))
    g_small = {"final_norm_g": dgf.reshape(D)}
    d_resid, d_branch, token = dxf, dxf, None
    scatter, own32 = [None] * len(stages), [None] * len(stages)
    for s in reversed(range(len(stages))):
        if token is not None:
            d_resid = lax.dynamic_update_slice(d_resid, d_resid[:SUBLANES, :LANES] + token, (0, 0))
        if s == 0:
            d_resid, dcar, dsm = pulls[s]((d_resid, d_branch))
        else:
            d_resid, d_branch, dcar, dsm = pulls[s]((d_resid, d_branch))
        for n, g in dsm.items():
            g_small[n] = g_small[n] + g if n in g_small else g
        keys = stages[s][2]
        own32[s] = [dcar[k + "32"] for k in keys]
        scatter[s] = spread_start(f"grads{s}_start", [dcar[k + "16"] for k in keys], True, d_resid)
        token = scatter[s][-1]
    dx = d_resid

    out, chain = {}, {}
    for s in range(len(stages)):
        lands = spread_wait(f"grads{s}_wait", scatter[s], True, dx)
        for own, land, (n, l, _) in zip(own32[s], lands, stages[s][3], strict=True):
            chain[n] = adamw_sharded(f"adamw_{n}_{l}", l, _flat2d(w[n]), _flat2d(m[n]), _flat2d(v[n]), own, land, me1, chain.get(n))
    for n in BIG:
        out[n] = [t.reshape(w[n].shape) for t in chain[n]]

    sm_all = sm_names + list(REPLICATED)
    (g_all,) = all_gather("gather_small_grads", [_pack([g_small[n] for n in sm_all])])
    g_sum = _unpack(ordered_sum("sum_small_grads", g_all), [g_small[n].shape for n in sm_all])
    g_loc = []
    for n, g in zip(sm_all, g_sum):
        if n in SMALL_SHARDED:
            ax = SMALL_SHARDED[n]
            size = w[n].shape[ax]
            g = lax.dynamic_slice_in_dim(g, me * size, size, axis=ax)
        g_loc.append(g)
    shapes = [w[n].shape for n in sm_all]
    res = adamw("adamw_small", _pack([w[n] for n in sm_all]), _pack([m[n] for n in sm_all]), _pack([v[n] for n in sm_all]), _pack(g_loc))
    res = [g_loc] + [_unpack(r, shapes) for r in res]
    for k, n in enumerate(sm_all):
        out[n] = [r[k] for r in res]

    grads, deltas, new_m, new_v = ([out[n][k] for n in WEIGHTS] for k in range(4))
    return (loss, dx.reshape(Bl, S, D), *grads, *deltas, *new_m, *new_v)
```

```python
import functools
import math

import numpy as np
import jax
import jax.numpy as jnp
from jax import lax
from jax.experimental import pallas as pl
from jax.experimental.pallas import tpu as pltpu

F32 = jnp.float32
BF16 = jnp.bfloat16
HIGHEST = lax.Precision.HIGHEST
MESH = pl.DeviceIdType.MESH

D_MODEL = 1024
DEPTH = 4
N_EVEN = 2
N_ODD = 2
GLA_HEADS, GLA_DK, GLA_DV, GLA_RANK, GLA_GATE_NORM = 4, 64, 128, 16, 16.0
HGRN_HEADS, HGRN_DK, HGRN_DV, HGRN_MIN_F = 4, 64, 128, 1e-20
RET_HEADS, RET_DK, RET_DV = 4, 128, 192
ROPE_BASE = 10000.0
S5_WIDTH, S5_GROUP_CH, S5_GROUPS, S5_STATE = 256, 16, 16, 64
S5_N = S5_GROUPS * S5_STATE
FFN_DIM = 2816
EPS = 1e-6
EVEN_IN = 3360
ODD_IN = 2816
ADAM_LR, ADAM_B1, ADAM_B2, ADAM_EPS, ADAM_WD, ADAM_STEP = 0.001, 0.9, 0.999, 1e-08, 0.01, 10

N_DEV = 8
VMEM_LIMIT_BYTES = 56 * 1024 * 1024
ROW_TILE = 256
SCAN_CHUNK = 64
S5_SEGMENTS = 8
LANES = 128


def _params(*sem):
    return pltpu.CompilerParams(dimension_semantics=sem, vmem_limit_bytes=VMEM_LIMIT_BYTES)


def _divisor_tile(n, cap):
    best = None
    for t in range(LANES, min(n, cap) + 1, LANES):
        if n % t == 0:
            best = t
    return best if best is not None else n


def _mm_nn(name, x, w):
    M, K = x.shape
    N = w.shape[1]
    tn = _divisor_tile(N, 2048) if K * N * 2 > 8 * 2**20 else N
    tm = 256 if tn * 4 * 512 > 6 * 2**20 else 512
    assert M % tm == 0 and N % tn == 0

    def body(x_ref, w_ref, o_ref):
        o_ref[...] = jnp.dot(x_ref[...].astype(BF16), w_ref[...], preferred_element_type=F32)

    return pl.pallas_call(
        body, name=name, grid=(N // tn, M // tm),
        in_specs=[pl.BlockSpec((tm, K), lambda j, i: (i, 0)), pl.BlockSpec((K, tn), lambda j, i: (0, j))],
        out_specs=pl.BlockSpec((tm, tn), lambda j, i: (i, j)),
        out_shape=jax.ShapeDtypeStruct((M, N), F32),
        compiler_params=_params("parallel", "parallel"),
    )(x, w)


def _mm_nt(name, dy, w):
    M, N = dy.shape
    K = w.shape[0]
    tk = _divisor_tile(K, 1024) if K * N * 2 > 8 * 2**20 else K
    tm = 256 if N >= 4096 else 512
    assert M % tm == 0 and K % tk == 0

    def body(dy_ref, w_ref, o_ref):
        o_ref[...] = lax.dot_general(dy_ref[...].astype(BF16), w_ref[...], (((1,), (1,)), ((), ())),
                                     preferred_element_type=F32)

    return pl.pallas_call(
        body, name=name, grid=(K // tk, M // tm),
        in_specs=[pl.BlockSpec((tm, N), lambda j, i: (i, 0)), pl.BlockSpec((tk, N), lambda j, i: (j, 0))],
        out_specs=pl.BlockSpec((tm, tk), lambda j, i: (i, j)),
        out_shape=jax.ShapeDtypeStruct((M, K), F32),
        compiler_params=_params("parallel", "parallel"),
    )(dy, w)


MM_TN_VMEM_BUDGET = 40 * 2**20


def _pad_lanes(n):
    return -(-n // LANES) * LANES


def _mm_tn(name, x, dy, nblk, want16):
    M, K = x.shape
    N = dy.shape[1]
    n = N // nblk
    xb, yb = x.dtype.itemsize, dy.dtype.itemsize
    best = None
    for tk in [t for t in range(LANES, K + 1, LANES) if K % t == 0]:
        for tm in (512, 256):
            out_bytes = nblk * tk * _pad_lanes(n) * (6 if want16 else 4)
            vmem = 2 * out_bytes + 2 * tm * tk * xb + 2 * tm * _pad_lanes(N) * yb
            traffic = (K // tk) * M * N * yb + M * K * xb
            if vmem <= MM_TN_VMEM_BUDGET and M % tm == 0 and (best is None or (traffic, -tm) < best[0]):
                best = ((traffic, -tm), tk, tm)
    _, tk, tm = best
    last = M // tm - 1

    def body(x_ref, dy_ref, o32_ref, *o16_ref):
        m = pl.program_id(1)

        @pl.when(m == 0)
        def _():
            o32_ref[...] = jnp.zeros_like(o32_ref)

        dyv = dy_ref[...].astype(BF16)
        if nblk == 1:
            o32_ref[0] += lax.dot_general(x_ref[...].astype(BF16), dyv, (((0,), (0,)), ((), ())), preferred_element_type=F32)
        else:
            xt = x_ref[...].astype(F32).T.astype(BF16)
            for d in range(nblk):
                o32_ref[d] += jnp.dot(xt, dyv[:, d * n:(d + 1) * n], preferred_element_type=F32)
        if want16:
            @pl.when(m == last)
            def _():
                o16_ref[0][...] = o32_ref[...].astype(BF16)

    blk = pl.BlockSpec((nblk, tk, n), lambda a, m: (0, a, 0))
    return pl.pallas_call(
        body, name=name, grid=(K // tk, M // tm),
        in_specs=[pl.BlockSpec((tm, tk), lambda a, m: (m, a)), pl.BlockSpec((tm, N), lambda a, m: (m, 0))],
        out_specs=[blk, blk] if want16 else [blk],
        out_shape=[jax.ShapeDtypeStruct((nblk, K, n), F32)] + ([jax.ShapeDtypeStruct((nblk, K, n), BF16)] if want16 else []),
        compiler_params=_params("parallel", "arbitrary"),
    )(x, dy)


def _weight_grads(name, x, dy, col_sharded):
    if col_sharded:
        return _mm_tn(name, x, dy, N_DEV, True)
    d32, d16 = _mm_tn(name, x, dy, 1, True)
    K, N = d32.shape[1:]
    return d32.reshape(N_DEV, K // N_DEV, N), d16.reshape(N_DEV, K // N_DEV, N)


def make_mm(name, col_sharded=False):
    @jax.custom_vjp
    def mm(x, w16, c32, c16):
        return _mm_nn(name + "_fwd", x, w16)

    def fwd(x, w16, c32, c16):
        return _mm_nn(name + "_fwd", x, w16), (x, w16)

    def bwd(res, dy):
        x, w16 = res
        d32, d16 = _weight_grads(name + "_dw", x, dy, col_sharded)
        return _mm_nt(name + "_dx", dy, w16), jnp.zeros_like(w16), d32, d16

    mm.defvjp(fwd, bwd)
    return mm


def make_mm_f32w(name):
    @jax.custom_vjp
    def mm(x, w):
        return _mm_nn(name + "_fwd", x, w.astype(BF16))

    def fwd(x, w):
        w16 = w.astype(BF16)
        return _mm_nn(name + "_fwd", x, w16), (x, w16)

    def bwd(res, dy):
        x, w16 = res
        return _mm_nt(name + "_dx", dy, w16), _mm_tn(name + "_dw", x, dy, 1, False)[0][0]

    mm.defvjp(fwd, bwd)
    return mm


def _row_specs(rows, params, consts, tile):
    specs = [pl.BlockSpec((tile, r.shape[1]), lambda i: (i, 0)) for r in rows]
    specs += [pl.BlockSpec(p.shape, lambda i: (0, 0)) for p in params]
    specs += [pl.BlockSpec((tile, c.shape[1]), lambda i, n=c.shape[0] // tile: (i % n, 0)) for c in consts]
    return specs


def _row_fwd(name, f, out_widths, tile, rows, params, consts):
    T = rows[0].shape[0]
    nr, npar, ncon = len(rows), len(params), len(consts)

    def body(*refs):
        r = tuple(x[...] for x in refs[:nr])
        p = tuple(x[...] for x in refs[nr:nr + npar])
        c = tuple(x[...] for x in refs[nr + npar:nr + npar + ncon])
        outs = f(r, p, c)
        for o_ref, o in zip(refs[nr + npar + ncon:], outs, strict=True):
            o_ref[...] = o

    return pl.pallas_call(
        body, name=name + "_fwd", grid=(T // tile,),
        in_specs=_row_specs(rows, params, consts, tile),
        out_specs=[pl.BlockSpec((tile, w), lambda i: (i, 0)) for w in out_widths],
        out_shape=[jax.ShapeDtypeStruct((T, w), F32) for w in out_widths],
        compiler_params=_params("parallel"),
    )(*rows, *params, *consts)


def _row_bwd(name, f, out_widths, tile, rows, params, consts, gouts, dr_dtype=F32):
    T = rows[0].shape[0]
    nr, npar, ncon, nout = len(rows), len(params), len(consts), len(out_widths)

    def body(*refs):
        r = tuple(x[...] for x in refs[:nr])
        p = tuple(x[...] for x in refs[nr:nr + npar])
        c = tuple(x[...] for x in refs[nr + npar:nr + npar + ncon])
        k = nr + npar + ncon
        g = tuple(x[...] for x in refs[k:k + nout])
        dr_refs = refs[k + nout:k + nout + nr]
        dp_refs = refs[k + nout + nr:]
        _, vjp = jax.vjp(lambda r_, p_: tuple(f(r_, p_, c)), r, p)
        dr, dp = vjp(g)
        for ref, val in zip(dr_refs, dr, strict=True):
            ref[...] = val.astype(ref.dtype)
        if npar:
            @pl.when(pl.program_id(0) == 0)
            def _():
                for ref in dp_refs:
                    ref[...] = jnp.zeros_like(ref)

            for ref, val in zip(dp_refs, dp, strict=True):
                ref[...] += val

    outs = pl.pallas_call(
        body, name=name + "_bwd", grid=(T // tile,),
        in_specs=_row_specs(rows, params, consts, tile) + [pl.BlockSpec((tile, w), lambda i: (i, 0)) for w in out_widths],
        out_specs=[pl.BlockSpec((tile, r.shape[1]), lambda i: (i, 0)) for r in rows]
        + [pl.BlockSpec(p.shape, lambda i: (0, 0)) for p in params],
        out_shape=[jax.ShapeDtypeStruct(r.shape, dr_dtype) for r in rows] + [jax.ShapeDtypeStruct(p.shape, F32) for p in params],
        compiler_params=_params("arbitrary"),
    )(*rows, *params, *consts, *gouts)
    return tuple(outs[:nr]), tuple(outs[nr:])


def make_proj_stage(name, f, out_widths, tile=ROW_TILE):
    def run(x, w16, params, consts):
        p = _mm_nn(name + "_mm", x, w16)
        return p, tuple(_row_fwd(name, f, out_widths, tile, (p,), params, consts))

    @jax.custom_vjp
    def op(x, w16, c32, c16, params, consts):
        return run(x, w16, params, consts)[1]

    def fwd(x, w16, c32, c16, params, consts):
        p, outs = run(x, w16, params, consts)
        return outs, (x, w16, p, params, consts)

    def bwd(res, g):
        x, w16, p, params, consts = res
        (dp,), dparams = _row_bwd(name, f, out_widths, tile, (p,), params, consts, tuple(g), dr_dtype=BF16)
        d32, d16 = _weight_grads(name + "_dw", x, dp, True)
        return _mm_nt(name + "_dx", dp, w16), jnp.zeros_like(w16), d32, d16, dparams, tuple(jnp.zeros_like(c) for c in consts)

    op.defvjp(fwd, bwd)
    return op


def make_rowop(name, f, out_widths, tile=ROW_TILE):
    @jax.custom_vjp
    def op(rows, params, consts):
        return tuple(_row_fwd(name, f, out_widths, tile, rows, params, consts))

    def fwd(rows, params, consts):
        return op(rows, params, consts), (rows, params, consts)

    def bwd(res, g):
        rows, params, consts = res
        dr, dp = _row_bwd(name, f, out_widths, tile, rows, params, consts, tuple(g))
        return dr, dp, tuple(jnp.zeros_like(c) for c in consts)

    op.defvjp(fwd, bwd)
    return op


def _rms(x, g):
    return x * lax.rsqrt(jnp.mean(x * x, axis=-1, keepdims=True) + EPS) * g


def _silu(x):
    return x * jax.nn.sigmoid(x)


def _bdot(a, b):
    return jnp.dot(a.astype(BF16), b.astype(BF16), preferred_element_type=F32)


def norm_f(rows, params, consts):
    return (_rms(rows[0], params[0]),)


def addnorm_f(rows, params, consts):
    x = rows[0] + rows[1]
    return x, _rms(x, params[0])


EVEN_GLA_END = 1568


def even_prep_f(rows, params, consts):
    (p,) = rows
    wa2f, wa2b, baf, bab, lbf, lbb = params
    gq = p[:, 0:256]
    gk = p[:, 256:512] * (GLA_DK ** -0.5)
    gv = p[:, 512:1024]
    gr = p[:, 1024:1536]
    glaf = jax.nn.log_sigmoid(_bdot(p[:, 1536:1552], wa2f) + baf) / GLA_GATE_NORM
    glab = jax.nn.log_sigmoid(_bdot(p[:, 1552:1568], wa2b) + bab) / GLA_GATE_NORM
    o = EVEN_GLA_END
    hq = _silu(p[:, o:o + 256])

    def gate(z, lb):
        f = lb + (1.0 - lb) * jax.nn.sigmoid(z)
        return jnp.log(jnp.maximum(f, HGRN_MIN_F)), (1.0 - lb) * jax.nn.sigmoid(-z)

    hlaf, hkf = gate(p[:, o + 256:o + 512], lbf)
    hlab, hkb = gate(p[:, o + 512:o + 768], lbb)
    hv = p[:, o + 768:o + 1280]
    hg = p[:, o + 1280:o + 1792]
    return gq, gk, gv, glaf, glab, gr, hq, hkf, hkb, hv, hlaf, hlab, hg


EVEN_PREP_WIDTHS = (256, 256, 512, 256, 256, 512, 256, 256, 256, 512, 256, 256, 512)


def _head_rms(o, g, heads, d):
    parts = []
    for h in range(heads):
        seg = o[:, h * d:(h + 1) * d]
        parts.append(seg * lax.rsqrt(jnp.mean(seg * seg, axis=-1, keepdims=True) + EPS))
    return jnp.concatenate(parts, axis=1) * g


def even_post_f(rows, params, consts):
    of, ob, hof, hob, gr, hg = rows
    a = _head_rms(of + ob, params[0], GLA_HEADS, GLA_DV) * _silu(gr)
    b = _head_rms(hof + hob, params[1], HGRN_HEADS, HGRN_DV) * _silu(hg)
    return (jnp.concatenate([a, b], axis=1),)


@jax.custom_vjp
def _roll_half(x):
    return pltpu.roll(x, RET_DK // 2, 1)


_roll_half.defvjp(lambda x: (_roll_half(x), None), lambda _, g: (_roll_half(g),))


def odd_prep_f(rows, params, consts):
    (p,) = rows
    cosf, sinf = consts

    def rot(t):
        parts = []
        for h in range(RET_HEADS):
            th = t[:, h * RET_DK:(h + 1) * RET_DK]
            parts.append(th * cosf + _roll_half(th) * sinf)
        return jnp.concatenate(parts, axis=1)

    rq = rot(p[:, 0:512])
    rk = rot(p[:, 512:1024]) * (RET_DK ** -0.5)
    return rq, rk, p[:, 1024:1792], p[:, 1792:2560], p[:, 2560:2816]


ODD_PREP_WIDTHS = (512, 512, 768, 768, 256)


def ret_post_f(rows, params, consts):
    of, ob, rg = rows
    o = of + ob
    parts = []
    for h in range(RET_HEADS):
        seg = o[:, h * RET_DV:(h + 1) * RET_DV]
        c = seg - jnp.mean(seg, axis=-1, keepdims=True)
        parts.append(c * lax.rsqrt(jnp.mean(c * c, axis=-1, keepdims=True) + EPS))
    return (jnp.concatenate(parts, axis=1) * params[0] * _silu(rg),)


def s5_post_f(rows, params, consts):
    h0r, h0i, h1r, h1i, u = rows
    c_re, c_im, d_skip, glu_w, glu_b = params
    hr = h0r + h1r
    hi = h0i + h1i
    y = _bdot(hr, c_re) - _bdot(hi, c_im) + d_skip * u
    g = jax.nn.gelu(y)
    return (g * jax.nn.sigmoid(_bdot(g, glu_w) + glu_b),)


def loss_head(x, r, g, target, tile=ROW_TILE):
    T, D = x.shape

    def body(x_ref, r_ref, g_ref, t_ref, loss_ref, dx_ref, dg_ref):
        t = t_ref[...]

        def lf(xv, gv):
            e = _rms(xv, gv) - t
            row = jnp.sum(e * e, axis=-1, keepdims=True) * (0.5 / D)
            return jnp.sum(row, axis=0, keepdims=True)

        l, vjp = jax.vjp(lf, x_ref[...] + r_ref[...], g_ref[...])
        dx, dg = vjp(jnp.ones((1, 1), F32))
        dx_ref[...] = dx

        @pl.when(pl.program_id(0) == 0)
        def _():
            loss_ref[...] = jnp.zeros_like(loss_ref)
            dg_ref[...] = jnp.zeros_like(dg_ref)

        loss_ref[...] += jnp.broadcast_to(l, loss_ref.shape)
        dg_ref[...] += dg

    row = pl.BlockSpec((tile, D), lambda i: (i, 0))
    vec = pl.BlockSpec((1, D), lambda i: (0, 0))
    return pl.pallas_call(
        body, name="loss_head", grid=(T // tile,),
        in_specs=[row, row, vec, row],
        out_specs=[pl.BlockSpec((1, LANES), lambda i: (0, 0)), row, vec],
        out_shape=[jax.ShapeDtypeStruct((1, LANES), F32), jax.ShapeDtypeStruct((T, D), F32), jax.ShapeDtypeStruct((1, D), F32)],
        compiler_params=_params("arbitrary"),
    )(x, r, g, target)


SUBLANES = 8


def _halo_specs(width, tile, T):
    n8 = tile // SUBLANES
    last = T // SUBLANES - 1
    return [pl.BlockSpec((tile, width), lambda i: (i, 0)),
            pl.BlockSpec((SUBLANES, width), lambda i: (jnp.maximum(i * n8 - 1, 0), 0)),
            pl.BlockSpec((SUBLANES, width), lambda i: (jnp.minimum((i + 1) * n8, last), 0))]


def _shift_rows(x, prev_row, next_row, tile):
    row = lax.broadcasted_iota(jnp.int32, (tile, 1), 0)
    down = jnp.where(row == 0, prev_row, pltpu.roll(x, 1, 0))
    up = jnp.where(row == tile - 1, next_row, pltpu.roll(x, tile - 1, 0))
    return down, up


def _conv_fwd(name, u, cw, cb, S, tile):
    T, F2 = u.shape
    F = F2 // 2
    per_seq = S // tile

    def body(u_ref, up_ref, un_ref, cw_ref, cb_ref, g_ref):
        pos = pl.program_id(0) % per_seq
        uv = u_ref[...]
        prev_row = jnp.where(pos == 0, 0.0, up_ref[SUBLANES - 1:SUBLANES, :])
        next_row = jnp.where(pos == per_seq - 1, 0.0, un_ref[0:1, :])
        down, up = _shift_rows(uv, prev_row, next_row, tile)
        c = _conv_taps(down, uv, up, cw_ref, cb_ref)
        g_ref[...] = (_silu(c[:, :F]) * c[:, F:]).astype(BF16)

    return pl.pallas_call(
        body, name=name + "_fwd", grid=(T // tile,),
        in_specs=_halo_specs(F2, tile, T) + [pl.BlockSpec((3, F2), lambda i: (0, 0)), pl.BlockSpec((1, F2), lambda i: (0, 0))],
        out_specs=pl.BlockSpec((tile, F), lambda i: (i, 0)),
        out_shape=jax.ShapeDtypeStruct((T, F), BF16),
        compiler_params=_params("parallel"),
    )(u, u, u, cw, cb)


def _conv_taps(down, mid, up, cw_ref, cb_ref):
    c = cb_ref[...] + down * cw_ref[0:1, :]
    c = c + mid * cw_ref[1:2, :]
    return c + up * cw_ref[2:3, :]


def _conv_bwd(name, u, dg, cw, cb, S, tile):
    T, F2 = u.shape
    F = F2 // 2
    per_seq = S // tile

    def dact(cv, dgv):
        a, v = cv[:, :F], cv[:, F:]
        sg = jax.nn.sigmoid(a)
        return jnp.concatenate([dgv * v * (sg * (1.0 + a * (1.0 - sg))), dgv * (a * sg)], axis=1)

    def body(u_ref, up_ref, un_ref, g_ref, gp_ref, gn_ref, cw_ref, cb_ref, du_ref, dw0_ref, dw1_ref, dw2_ref, db_ref):
        i = pl.program_id(0)
        pos = i % per_seq
        first, last = pos == 0, pos == per_seq - 1
        lo, hi = slice(SUBLANES - 1, SUBLANES), slice(0, 1)
        uv = u_ref[...]
        u_m1, u_p1 = jnp.where(first, 0.0, up_ref[lo, :]), jnp.where(last, 0.0, un_ref[hi, :])
        u_dn, u_up = _shift_rows(uv, u_m1, u_p1, tile)
        dc = dact(_conv_taps(u_dn, uv, u_up, cw_ref, cb_ref), g_ref[...])
        c_m1 = _conv_taps(up_ref[SUBLANES - 2:SUBLANES - 1, :], u_m1, u_ref[0:1, :], cw_ref, cb_ref)
        c_p1 = _conv_taps(u_ref[tile - 1:tile, :], u_p1, un_ref[1:2, :], cw_ref, cb_ref)
        dc_prev = jnp.where(first, 0.0, dact(c_m1, gp_ref[lo, :]))
        dc_next = jnp.where(last, 0.0, dact(c_p1, gn_ref[hi, :]))
        dc_dn, dc_up = _shift_rows(dc, dc_prev, dc_next, tile)
        du = dc_up * cw_ref[0:1, :]
        du = du + dc * cw_ref[1:2, :]
        du_ref[...] = (du + dc_dn * cw_ref[2:3, :]).astype(BF16)

        @pl.when(i == 0)
        def _():
            for ref in (dw0_ref, dw1_ref, dw2_ref, db_ref):
                ref[...] = jnp.zeros_like(ref)

        dw0_ref[...] += jnp.sum(dc * u_dn, axis=0, keepdims=True)
        dw1_ref[...] += jnp.sum(dc * uv, axis=0, keepdims=True)
        dw2_ref[...] += jnp.sum(dc * u_up, axis=0, keepdims=True)
        db_ref[...] += jnp.sum(dc, axis=0, keepdims=True)

    vec = pl.BlockSpec((1, F2), lambda i: (0, 0))
    du, dw0, dw1, dw2, db = pl.pallas_call(
        body, name=name + "_bwd", grid=(T // tile,),
        in_specs=_halo_specs(F2, tile, T) + _halo_specs(F, tile, T) + [pl.BlockSpec((3, F2), lambda i: (0, 0)), vec],
        out_specs=[pl.BlockSpec((tile, F2), lambda i: (i, 0)), vec, vec, vec, vec],
        out_shape=[jax.ShapeDtypeStruct((T, F2), BF16)] + [jax.ShapeDtypeStruct((1, F2), F32)] * 4,
        compiler_params=_params("arbitrary"),
    )(u, u, u, dg, dg, dg, cw, cb)
    return du, jnp.concatenate([dw0, dw1, dw2], axis=0), db


def make_ffn(name, S):
    def run(x, wup16, wdn16, cw, cb):
        u = _mm_nn(name + "_up", x, wup16)
        g16 = _conv_fwd(name + "_conv", u, cw, cb, S, ROW_TILE)
        return u, g16, _mm_nn(name + "_down", g16, wdn16)

    @jax.custom_vjp
    def op(x, wup16, cu32, cu16, wdn16, cd32, cd16, cw, cb):
        return run(x, wup16, wdn16, cw, cb)[2]

    def fwd(x, wup16, cu32, cu16, wdn16, cd32, cd16, cw, cb):
        u, g16, out = run(x, wup16, wdn16, cw, cb)
        return out, (x, wup16, wdn16, u, g16, cw, cb)

    def bwd(res, dout):
        x, wup16, wdn16, u, g16, cw, cb = res
        dg = _mm_nt(name + "_down_dx", dout, wdn16)
        dd32, dd16 = _weight_grads(name + "_down_dw", g16, dout, False)
        du16, dcw, dcb = _conv_bwd(name + "_conv", u, dg, cw, cb, S, ROW_TILE)
        du32, du16w = _weight_grads(name + "_up_dw", x, du16, True)
        return (_mm_nt(name + "_up_dx", du16, wup16), jnp.zeros_like(wup16), du32, du16w, jnp.zeros_like(wdn16), dd32, dd16, dcw, dcb)

    op.defvjp(fwd, bwd)
    return op


def _dot_nt(a, b):
    return lax.dot_general(a.astype(BF16), b.astype(BF16), (((1,), (1,)), ((), ())), preferred_element_type=F32)


def _dot_tn(a, b):
    return lax.dot_general(a.astype(BF16), b.astype(BF16), (((0,), (0,)), ((), ())), preferred_element_type=F32)


def _chunk_decays(la, direction, C, width, dk, const_lg):
    row = lax.broadcasted_iota(jnp.int32, (C, C), 0)
    col = lax.broadcasted_iota(jnp.int32, (C, C), 1)
    keep = (row >= col) if direction == 0 else (row <= col)
    ridx = lax.broadcasted_iota(jnp.int32, (C, 1), 0)
    if const_lg is None:
        cum = jnp.dot(keep.astype(F32), la, precision=HIGHEST, preferred_element_type=F32)
    else:
        lane_head = lax.broadcasted_iota(jnp.int32, (1, width), 1) // dk
        lg = jnp.zeros((1, width), F32)
        for h, val in enumerate(const_lg):
            lg = jnp.where(lane_head == h, val, lg)
        steps = (ridx + 1) if direction == 0 else (C - ridx)
        cum = steps.astype(F32) * lg
    exit_row = C - 1 if direction == 0 else 0
    mid = jnp.sum(jnp.where(ridx == C // 2, cum, 0.0), axis=0, keepdims=True)
    last = jnp.sum(jnp.where(ridx == exit_row, cum, 0.0), axis=0, keepdims=True)
    return keep, ridx == exit_row, cum, mid, last


def _scan_fwd(name, q, kf, kb, v, laf, lab, H, dk, dv, S, C, const_lg):
    T = q.shape[0]
    B, nc = T // S, S // C
    Wk, Wv = H * dk, H * dv
    learn = const_lg is None

    def body(*refs):
        if learn:
            qf_r, qb_r, kf_r, kb_r, vf_r, vb_r, laf_r, lab_r, of_r, ob_r, sf_r, sb_r, st = refs
            las = (laf_r[...], lab_r[...])
        else:
            qf_r, qb_r, kf_r, kb_r, vf_r, vb_r, of_r, ob_r, sf_r, sb_r, st = refs
            las = (None, None)

        @pl.when(pl.program_id(1) == 0)
        def _():
            st[...] = jnp.zeros_like(st)

        for d, (q_r, k_r, v_r, o_r, s_r) in enumerate(((qf_r, kf_r, vf_r, of_r, sf_r), (qb_r, kb_r, vb_r, ob_r, sb_r))):
            keep, _, cum, mid, last = _chunk_decays(las[d], d, C, Wk, dk, None if learn else const_lg[d])
            qe = q_r[...] * jnp.exp(cum - mid)
            ke = k_r[...] * jnp.exp(mid - cum)
            q_in = qe * jnp.exp(mid)
            k_out = ke * jnp.exp(last - mid)
            e_last = jnp.exp(last)
            vv = v_r[...]
            for h in range(H):
                ks, vs = slice(h * dk, (h + 1) * dk), slice(h * dv, (h + 1) * dv)
                a = jnp.where(keep, _dot_nt(qe[:, ks], ke[:, ks]), 0.0)
                state = st[d, h]
                o_r[:, vs] = _bdot(a, vv[:, vs]) + _dot_nt(q_in[:, ks], state)
                s_r[h * dv:(h + 1) * dv, :] = state
                st[d, h] = state * e_last[:, ks] + _dot_tn(vv[:, vs], k_out[:, ks])

    fpos = lambda b, c: (b * nc + c, 0)
    bpos = lambda b, c: (b * nc + nc - 1 - c, 0)
    kspec = lambda pos: pl.BlockSpec((C, Wk), pos)
    vspec = lambda pos: pl.BlockSpec((C, Wv), pos)
    sspec = lambda pos: pl.BlockSpec((None, Wv, dk), lambda b, c: pos(b, c) + (0,))
    ins = [q, q, kf, kb, v, v] + ([laf, lab] if learn else [])
    in_specs = [kspec(fpos), kspec(bpos), kspec(fpos), kspec(bpos), vspec(fpos), vspec(bpos)] + ([kspec(fpos), kspec(bpos)] if learn else [])
    return pl.pallas_call(
        body, name=name + "_fwd", grid=(B, nc), in_specs=in_specs,
        out_specs=[vspec(fpos), vspec(bpos), sspec(fpos), sspec(bpos)],
        out_shape=[jax.ShapeDtypeStruct((T, Wv), F32)] * 2 + [jax.ShapeDtypeStruct((B * nc, Wv, dk), F32)] * 2,
        scratch_shapes=[pltpu.VMEM((2, H, dv, dk), F32)],
        compiler_params=_params("parallel", "arbitrary"),
    )(*ins)


def _scan_bwd(name, q, kf, kb, v, laf, lab, sf, sb, dof, dob, H, dk, dv, S, C, const_lg):
    T = q.shape[0]
    B, nc = T // S, S // C
    Wk, Wv = H * dk, H * dv
    learn = const_lg is None

    def body(*refs):
        if learn:
            (qf_r, qb_r, kf_r, kb_r, vf_r, vb_r, laf_r, lab_r, sf_r, sb_r, dof_r, dob_r,
             dqf_r, dqb_r, dkf_r, dkb_r, dvf_r, dvb_r, dlaf_r, dlab_r, dst) = refs
            las, dlas = (laf_r[...], lab_r[...]), (dlaf_r, dlab_r)
        else:
            (qf_r, qb_r, kf_r, kb_r, vf_r, vb_r, sf_r, sb_r, dof_r, dob_r,
             dqf_r, dqb_r, dkf_r, dkb_r, dvf_r, dvb_r, dst) = refs
            las, dlas = (None, None), (None, None)

        @pl.when(pl.program_id(1) == 0)
        def _():
            dst[...] = jnp.zeros_like(dst)

        groups = ((qf_r, kf_r, vf_r, sf_r, dof_r, dqf_r, dkf_r, dvf_r), (qb_r, kb_r, vb_r, sb_r, dob_r, dqb_r, dkb_r, dvb_r))
        for d, (q_r, k_r, v_r, s_r, do_r, dq_r, dk_r, dv_r) in enumerate(groups):
            keep, is_exit, cum, mid, last = _chunk_decays(las[d], d, C, Wk, dk, None if learn else const_lg[d])
            eq, ek = jnp.exp(cum - mid), jnp.exp(mid - cum)
            e_in, e_out, e_last = jnp.exp(mid), jnp.exp(last - mid), jnp.exp(last)
            qe, ke = q_r[...] * eq, k_r[...] * ek
            q_in, k_out = qe * e_in, ke * e_out
            vv, do = v_r[...], do_r[...]
            dqe_parts, dke_parts, dlast_parts = [], [], []
            for h in range(H):
                ks, vs = slice(h * dk, (h + 1) * dk), slice(h * dv, (h + 1) * dv)
                a = jnp.where(keep, _dot_nt(qe[:, ks], ke[:, ks]), 0.0)
                dp = jnp.where(keep, _dot_nt(do[:, vs], vv[:, vs]), 0.0)
                s_prev = s_r[h * dv:(h + 1) * dv, :]
                ds = dst[d, h]
                dk_out = _bdot(vv[:, vs], ds)
                dqe_parts.append(_bdot(dp, ke[:, ks]) + _bdot(do[:, vs], s_prev) * e_in[:, ks])
                dke_parts.append(_dot_tn(dp, qe[:, ks]) + dk_out * e_out[:, ks])
                dv_r[:, vs] = _dot_tn(a, do[:, vs]) + _dot_nt(k_out[:, ks], ds)
                if learn:
                    dlast_parts.append(jnp.sum(dk_out * k_out[:, ks], axis=0, keepdims=True)
                                       + jnp.sum(ds * s_prev, axis=0, keepdims=True) * e_last[:, ks])
                dst[d, h] = ds * e_last[:, ks] + _dot_tn(do[:, vs], q_in[:, ks])
            dqe = jnp.concatenate(dqe_parts, axis=1)
            dke = jnp.concatenate(dke_parts, axis=1)
            dq_r[...] = dqe * eq
            dk_r[...] = dke * ek
            if learn:
                dcum = dqe * qe - dke * ke + jnp.where(is_exit, jnp.concatenate(dlast_parts, axis=1), 0.0)
                dlas[d][...] = lax.dot_general(keep.astype(F32), dcum, (((0,), (0,)), ((), ())), precision=HIGHEST,
                                               preferred_element_type=F32)

    fpos = lambda b, c: (b * nc + nc - 1 - c, 0)
    bpos = lambda b, c: (b * nc + c, 0)
    kspec = lambda pos: pl.BlockSpec((C, Wk), pos)
    vspec = lambda pos: pl.BlockSpec((C, Wv), pos)
    sspec = lambda pos: pl.BlockSpec((None, Wv, dk), lambda b, c: pos(b, c) + (0,))
    ins = [q, q, kf, kb, v, v] + ([laf, lab] if learn else []) + [sf, sb, dof, dob]
    in_specs = ([kspec(fpos), kspec(bpos), kspec(fpos), kspec(bpos), vspec(fpos), vspec(bpos)]
                + ([kspec(fpos), kspec(bpos)] if learn else []) + [sspec(fpos), sspec(bpos), vspec(fpos), vspec(bpos)])
    n_k = 6 if learn else 4
    out_specs = [kspec(fpos), kspec(bpos), kspec(fpos), kspec(bpos), vspec(fpos), vspec(bpos)] + ([kspec(fpos), kspec(bpos)] if learn else [])
    out_shape = [jax.ShapeDtypeStruct((T, Wk), F32)] * 4 + [jax.ShapeDtypeStruct((T, Wv), F32)] * 2 + ([jax.ShapeDtypeStruct((T, Wk), F32)] * 2 if learn else [])
    del n_k
    return pl.pallas_call(
        body, name=name + "_bwd", grid=(B, nc), in_specs=in_specs, out_specs=out_specs, out_shape=out_shape,
        scratch_shapes=[pltpu.VMEM((2, H, dv, dk), F32)],
        compiler_params=_params("parallel", "arbitrary"),
    )(*ins)


def make_scan(name, H, dk, dv, S, C=SCAN_CHUNK, const_lg=None):
    if const_lg is None:
        @jax.custom_vjp
        def op(q, kf, kb, v, laf, lab):
            return tuple(_scan_fwd(name, q, kf, kb, v, laf, lab, H, dk, dv, S, C, None)[:2])

        def fwd(q, kf, kb, v, laf, lab):
            of, ob, sf, sb = _scan_fwd(name, q, kf, kb, v, laf, lab, H, dk, dv, S, C, None)
            return (of, ob), (q, kf, kb, v, laf, lab, sf, sb)

        def bwd(res, g):
            q, kf, kb, v, laf, lab, sf, sb = res
            dqf, dqb, dkf, dkb, dvf, dvb, dlaf, dlab = _scan_bwd(name, q, kf, kb, v, laf, lab, sf, sb, g[0], g[1], H, dk, dv, S, C, None)
            return dqf + dqb, dkf, dkb, dvf + dvb, dlaf, dlab
    else:
        @jax.custom_vjp
        def op(q, kf, kb, v):
            return tuple(_scan_fwd(name, q, kf, kb, v, None, None, H, dk, dv, S, C, const_lg)[:2])

        def fwd(q, kf, kb, v):
            of, ob, sf, sb = _scan_fwd(name, q, kf, kb, v, None, None, H, dk, dv, S, C, const_lg)
            return (of, ob), (q, kf, kb, v, sf, sb)

        def bwd(res, g):
            q, kf, kb, v, sf, sb = res
            dqf, dqb, dkf, dkb, dvf, dvb = _scan_bwd(name, q, kf, kb, v, None, None, sf, sb, g[0], g[1], H, dk, dv, S, C, const_lg)
            return dqf + dqb, dkf, dkb, dvf + dvb

    op.defvjp(fwd, bwd)
    return op


def _reorder_call(name, t, S, to_segments):
    T, w = t.shape
    n_it = S // S5_SEGMENTS

    def body(x_ref, o_ref):
        def step(i, carry):
            packed = pl.ds(pl.multiple_of(i * S5_SEGMENTS, S5_SEGMENTS), S5_SEGMENTS)
            spread = pl.ds(i, S5_SEGMENTS, stride=n_it)
            if to_segments:
                o_ref[packed, :] = x_ref[spread, :]
            else:
                o_ref[spread, :] = x_ref[packed, :]
            return carry

        lax.fori_loop(0, n_it, step, 0, unroll=8)

    blk = pl.BlockSpec((S, LANES), lambda b, j: (b, j))
    return pl.pallas_call(body, name=name, grid=(T // S, w // LANES), in_specs=[blk], out_specs=blk,
                          out_shape=jax.ShapeDtypeStruct(t.shape, t.dtype), compiler_params=_params("parallel", "parallel"))(t)


def make_reorder(name, S, to_segments):
    @jax.custom_vjp
    def op(t):
        return _reorder_call(name, t, S, to_segments)

    op.defvjp(lambda t: (_reorder_call(name, t, S, to_segments), None),
              lambda _, g: (_reorder_call(name + "_bwd", g, S, not to_segments),))
    return op


def _s5_scan_call(name, Xs, A, S, dirs):
    T, N = Xs[0].shape
    B, nl = T // S, N // LANES
    n_it = S // S5_SEGMENTS
    assert n_it & (n_it - 1) == 0

    def cmul(ar, ai, br, bi):
        return ar * br - ai * bi, ar * bi + ai * br

    def body(x0r, x0i, x1r, x1i, a_ref, h0r, h0i, h1r, h1i):
        seg = lax.broadcasted_iota(jnp.int32, (S5_SEGMENTS, 1), 0)
        for k, (xr, xi, hr, hi) in enumerate(((x0r, x0i, h0r, h0i), (x1r, x1i, h1r, h1i))):
            back = dirs[k] == 1
            ar = jnp.broadcast_to(a_ref[2 * k:2 * k + 1, :], (S5_SEGMENTS, LANES))
            ai = jnp.broadcast_to(a_ref[2 * k + 1:2 * k + 2, :], (S5_SEGMENTS, LANES))
            rows_of = lambda i: pl.ds(pl.multiple_of(((n_it - 1 - i) if back else i) * S5_SEGMENTS, S5_SEGMENTS), S5_SEGMENTS)

            def local(i, carry):
                sr, si = carry
                rows = rows_of(i)
                pr, pi = cmul(ar, ai, sr, si)
                sr, si = pr + xr[rows, :], pi + xi[rows, :]
                hr[rows, :] = sr
                hi[rows, :] = si
                return sr, si

            zero = jnp.zeros((S5_SEGMENTS, LANES), F32)
            er, ei = lax.fori_loop(0, n_it, local, (zero, zero), unroll=8)
            pr, pi = ar, ai
            for _ in range(n_it.bit_length() - 1):
                pr, pi = cmul(pr, pi, pr, pi)
            shift = (S5_SEGMENTS - 1) if back else 1
            tr, ti = er, ei
            order = range(S5_SEGMENTS - 2, -1, -1) if back else range(1, S5_SEGMENTS)
            for r in order:
                nr, ni = cmul(pr, pi, pltpu.roll(tr, shift, 0), pltpu.roll(ti, shift, 0))
                tr = jnp.where(seg == r, er + nr, tr)
                ti = jnp.where(seg == r, ei + ni, ti)
            edge = (S5_SEGMENTS - 1) if back else 0
            cr = jnp.where(seg == edge, 0.0, pltpu.roll(tr, shift, 0))
            ci = jnp.where(seg == edge, 0.0, pltpu.roll(ti, shift, 0))

            def fix(i, carry):
                wr, wi = carry
                rows = rows_of(i)
                fr, fi = cmul(wr, wi, cr, ci)
                hr[rows, :] = hr[rows, :] + fr
                hi[rows, :] = hi[rows, :] + fi
                return cmul(wr, wi, ar, ai)

            lax.fori_loop(0, n_it, fix, (ar, ai), unroll=8)

    col = pl.BlockSpec((S, LANES), lambda b, j: (b, j))
    return tuple(pl.pallas_call(
        body, name=name, grid=(B, nl),
        in_specs=[col] * 4 + [pl.BlockSpec((4, LANES), lambda b, j: (0, j))],
        out_specs=[col] * 4,
        out_shape=[jax.ShapeDtypeStruct((T, N), F32)] * 4,
        compiler_params=_params("parallel", "parallel"),
    )(*Xs, A))


def _s5_decay_grad_call(name, lam, H, X, tile=ROW_TILE):
    T, N = X[0].shape

    def body(*refs):
        l_refs, h_refs, x_refs, p_ref = refs[0:4], refs[4:8], refs[8:12], refs[12]

        @pl.when(pl.program_id(0) == 0)
        def _():
            p_ref[...] = jnp.zeros_like(p_ref)

        for k in range(2):
            re, im = 2 * k, 2 * k + 1
            ur, ui = h_refs[re][...] - x_refs[re][...], h_refs[im][...] - x_refs[im][...]
            lr, li = l_refs[re][...], l_refs[im][...]
            p_ref[re:re + 1, :] += jnp.sum(lr * ur + li * ui, axis=0, keepdims=True)
            p_ref[im:im + 1, :] += jnp.sum(li * ur - lr * ui, axis=0, keepdims=True)

    row = pl.BlockSpec((tile, N), lambda i: (i, 0))
    return pl.pallas_call(
        body, name=name, grid=(T // tile,), in_specs=[row] * 12,
        out_specs=pl.BlockSpec((4, N), lambda i: (0, 0)), out_shape=jax.ShapeDtypeStruct((4, N), F32),
        compiler_params=_params("arbitrary"),
    )(*lam, *H, *X)


def make_s5_scan(name, S):
    @jax.custom_vjp
    def op(X, A):
        return _s5_scan_call(name + "_fwd", X, A, S, (0, 1))

    def fwd(X, A):
        H = _s5_scan_call(name + "_fwd", X, A, S, (0, 1))
        return H, (X, A, H)

    def bwd(res, G):
        X, A, H = res
        conj = A * jnp.array([[1.0], [-1.0], [1.0], [-1.0]], F32)
        lam = _s5_scan_call(name + "_bwd", tuple(G), conj, S, (1, 0))
        P = _s5_decay_grad_call(name + "_dA", lam, H, X)
        ar, ai = A[0::2], A[1::2]
        pr, pi = P[0::2], P[1::2]
        den = ar * ar + ai * ai
        dar, dai = (pr * ar - pi * ai) / den, (pr * ai + pi * ar) / den
        return lam, jnp.stack([dar[0], dai[0], dar[1], dai[1]], axis=0)

    op.defvjp(fwd, bwd)
    return op


ANY = pl.BlockSpec(memory_space=pl.ANY)


def _place():
    x, y, c = lax.axis_index("x"), lax.axis_index("y"), lax.axis_index("c")
    return x, y, c, [(1 - x, y), (x, 1 - y), (1 - x, 1 - y)]


def all_gather(name, arrs):
    n = len(arrs)

    def body(*refs):
        ins, outs = refs[:n], refs[n:2 * n]
        send, recv, lsem = refs[2 * n:]
        x, y, c, chips = _place()
        me, sibling = (x, y, c), (x, y, 1 - c)

        def copy(a, k, block, to, src=None):
            slot = outs[a].at[4 * block[0] + 2 * block[1] + block[2]]
            return pltpu.make_async_remote_copy(src_ref=slot if src is None else src, dst_ref=slot, send_sem=send.at[a, k],
                                                recv_sem=recv.at[a, k], device_id=to, device_id_type=MESH)

        mine = [pltpu.make_async_copy(ins[a], outs[a].at[4 * x + 2 * y + c], lsem.at[a]) for a in range(n)]
        first = []
        for a in range(n):
            mine[a].start()
            first.append(copy(a, 0, me, sibling, src=ins[a]))
            first += [copy(a, 1 + j, me, (*chip, c), src=ins[a]) for j, chip in enumerate(chips)]
        for cp in first:
            cp.start()
        passed = []
        for a in range(n):
            for j, chip in enumerate(chips):
                copy(a, 1 + j, (*chip, c), me).wait_recv()
                fwd = copy(a, 4 + j, (*chip, c), sibling)
                fwd.start()
                passed.append(fwd)
        for a in range(n):
            copy(a, 0, sibling, me).wait_recv()
            for j, chip in enumerate(chips):
                copy(a, 4 + j, (*chip, 1 - c), me).wait_recv()
        for cp in first + passed:
            cp.wait_send()
        for cp in mine:
            cp.wait()

    return pl.pallas_call(
        body, name=name, in_specs=[ANY] * n, out_specs=[ANY] * n,
        out_shape=[jax.ShapeDtypeStruct((N_DEV,) + a.shape, a.dtype) for a in arrs],
        scratch_shapes=[pltpu.SemaphoreType.DMA((n, 7)), pltpu.SemaphoreType.DMA((n, 7)), pltpu.SemaphoreType.DMA((n,))],
    )(*arrs)


HBM = pl.BlockSpec(memory_space=pltpu.HBM)
SEM = pl.BlockSpec(memory_space=pltpu.SEMAPHORE)
EFFECT = pltpu.SideEffectType.DATAFLOW_SIDE_EFFECTING
GATHER_PEERS = (1, 2, 4, 6)
OTHER_CHIPS = (2, 4, 6)
COPIES_PER_ARRAY = {"scatter": N_DEV - 1, "gather": len(GATHER_PEERS), "forward": len(OTHER_CHIPS)}


def _split_plan(mode, srcs, lands, send, recv):
    x, y, c = lax.axis_index("x"), lax.axis_index("y"), lax.axis_index("c")

    def dev(k):
        return (1 - x if k & 4 else x), (1 - y if k & 2 else y), (1 - c if k & 1 else c)

    def idx(d):
        return 4 * d[0] + 2 * d[1] + d[2]

    me = idx((x, y, c))
    plan = []
    for a, land in enumerate(lands):
        if mode == "scatter":
            legs = [(srcs[a].at[idx(dev(k))], land.at[me], land.at[idx(dev(k))], dev(k)) for k in range(1, N_DEV)]
        elif mode == "gather":
            legs = [(srcs[a], land.at[me], land.at[idx(dev(k))], dev(k)) for k in GATHER_PEERS]
        else:
            legs = [(land.at[idx(dev(j))], land.at[idx(dev(j))], land.at[idx(dev(j ^ 1))], dev(1)) for j in OTHER_CHIPS]
        for i, (src, dst, arrival, to) in enumerate(legs):
            sem = a * len(legs) + i
            pair = tuple(pltpu.make_async_remote_copy(src_ref=src, dst_ref=d, send_sem=send.at[sem], recv_sem=recv.at[sem],
                                                      device_id=to, device_id_type=MESH) for d in (dst, arrival))
            plan.append(pair)
    return plan


def split_start(name, mode, srcs, lands, after):
    if lands is None:
        lands = [lax.empty((N_DEV,) + (s.shape[1:] if mode == "scatter" else s.shape), s.dtype) for s in srcs]
    ns, nl = len(srcs), len(lands)
    nsem = COPIES_PER_ARRAY[mode] * nl

    def body(*refs):
        ins, lnd = refs[:ns], refs[ns:ns + nl]
        send, recv = refs[ns + nl + 1], refs[ns + nl + 2]
        token = refs[-1]
        for out, _ in _split_plan(mode, ins, lnd, send, recv):
            out.start()
        token[...] = jnp.zeros_like(token)

    arrs = list(srcs) + list(lands)
    return pl.pallas_call(
        body, name=name,
        out_shape=(pltpu.SemaphoreType.DMA((nsem,)), pltpu.SemaphoreType.DMA((nsem,)))
        + tuple(pltpu.HBM(t.shape, t.dtype) for t in arrs) + (jax.ShapeDtypeStruct((SUBLANES, LANES), F32),),
        in_specs=[HBM] * len(arrs) + [ANY],
        out_specs=(SEM, SEM) + (HBM,) * len(arrs) + (pl.BlockSpec(memory_space=pltpu.VMEM),),
        input_output_aliases={i: 2 + i for i in range(len(arrs))},
        compiler_params=pltpu.CompilerParams(has_side_effects=EFFECT),
    )(*[pltpu.with_memory_space_constraint(t, pltpu.HBM) for t in arrs], after)


def split_wait(name, mode, handle, after):
    send, recv = handle[0], handle[1]
    arrs = list(handle[2:-1])
    nl = len(arrs) if mode == "forward" else len(arrs) // 2
    ns = len(arrs) - nl

    def body(*refs):
        ins, lnd = refs[:ns], refs[ns:ns + nl]
        s, r = refs[ns + nl], refs[ns + nl + 1]
        for out, arrival in _split_plan(mode, ins, lnd, s, r):
            out.wait_send()
            arrival.wait_recv()

    outs = pl.pallas_call(
        body, name=name,
        out_shape=tuple(pltpu.HBM(t.shape, t.dtype) for t in arrs),
        in_specs=[HBM] * len(arrs) + [SEM, SEM, ANY], out_specs=(HBM,) * len(arrs),
        input_output_aliases={i: i for i in range(len(arrs))},
        compiler_params=pltpu.CompilerParams(has_side_effects=EFFECT),
    )(*arrs, send, recv, after)
    return list(outs[ns:])


def _row_tile(rows, cols):
    cap = max(SUBLANES, (2**18 // cols) // SUBLANES * SUBLANES)
    if rows <= cap:
        return rows
    for t in range(cap, SUBLANES - 1, -SUBLANES):
        if rows % t == 0:
            return t
    return rows


def ordered_sum(name, parts):
    n, R, C = parts.shape
    rt = _row_tile(R, C)

    def body(p_ref, o_ref):
        s = p_ref[0]
        for k in range(1, n):
            s = s + p_ref[k]
        o_ref[...] = s

    return pl.pallas_call(
        body, name=name, grid=(R // rt,),
        in_specs=[pl.BlockSpec((n, rt, C), lambda r: (0, r, 0))], out_specs=pl.BlockSpec((rt, C), lambda r: (r, 0)),
        out_shape=jax.ShapeDtypeStruct((R, C), F32), compiler_params=_params("parallel"),
    )(parts)


def _adamw_update(w, m, v, g):
    bias1 = 1.0 - ADAM_B1 ** ADAM_STEP
    bias2 = 1.0 - ADAM_B2 ** ADAM_STEP
    m_new = ADAM_B1 * m + (1.0 - ADAM_B1) * g
    v_new = ADAM_B2 * v + (1.0 - ADAM_B2) * (g * g)
    delta = -ADAM_LR * ((m_new / bias1) / (jnp.sqrt(v_new / bias2) + ADAM_EPS) + ADAM_WD * w)
    return delta, m_new, v_new


def adamw(name, w, m, v, g):
    R, C = w.shape
    rt = _row_tile(R, C)

    def body(w_ref, m_ref, v_ref, g_ref, d_ref, mo_ref, vo_ref):
        d_ref[...], mo_ref[...], vo_ref[...] = _adamw_update(w_ref[...], m_ref[...], v_ref[...], g_ref[...])

    row = pl.BlockSpec((rt, C), lambda r: (r, 0))
    return pl.pallas_call(
        body, name=name, grid=(R // rt,), in_specs=[row] * 4, out_specs=[row] * 3,
        out_shape=[jax.ShapeDtypeStruct((R, C), F32)] * 3, compiler_params=_params("parallel"),
    )(w, m, v, g)


def adamw_sharded(name, layer, w, m, v, own, land, me, prev):
    _, R, C = own.shape
    rt = _row_tile(R, C)
    base = layer * (R // rt)

    def body(me_ref, w_ref, m_ref, v_ref, own_ref, land_ref, *rest):
        go_ref, d_ref, mo_ref, vo_ref = rest[-4:]
        g = own_ref[...]
        for k in range(N_DEV):
            g = g + jnp.where(me_ref[0] == k, 0.0, land_ref[k].astype(F32))
        go_ref[...] = g
        d_ref[...], mo_ref[...], vo_ref[...] = _adamw_update(w_ref[...], m_ref[...], v_ref[...], g)

    row = pl.BlockSpec((rt, C), lambda r, p: (base + r, 0))
    in_specs = [row, row, row, pl.BlockSpec((None, rt, C), lambda r, p: (p[0], r, 0)), pl.BlockSpec((N_DEV, rt, C), lambda r, p: (0, r, 0))]
    ins = [me, w, m, v, own, land]
    aliases = {}
    if prev is not None:
        in_specs += [ANY] * 4
        aliases = {len(ins) + k: k for k in range(4)}
        ins += list(prev)
    return pl.pallas_call(
        body, name=name,
        grid_spec=pltpu.PrefetchScalarGridSpec(num_scalar_prefetch=1, grid=(R // rt,), in_specs=in_specs, out_specs=[row] * 4),
        out_shape=[jax.ShapeDtypeStruct(w.shape, F32)] * 4, input_output_aliases=aliases,
        compiler_params=_params("arbitrary"),
    )(*ins)


def _hgrn_lower_bounds(lb_logits):
    p = jax.nn.softmax(lb_logits, axis=1)
    return jnp.cumsum(p, axis=1) - p[:, :1]


def _s5_discretise(lam_re, lam_im, log_dt, b_re, b_im):
    lr = jnp.minimum(lam_re, -1e-4)
    li = lam_im
    dt = jnp.exp(log_dt)[:, None]
    mag = jnp.exp(lr * dt)
    ar, ai = mag * jnp.cos(li * dt), mag * jnp.sin(li * dt)
    den = lr * lr + li * li
    nr = ar - 1.0
    cr = (nr * lr + ai * li) / den
    ci = (ai * lr - nr * li) / den
    bbr = cr[..., None] * b_re - ci[..., None] * b_im
    bbi = cr[..., None] * b_im + ci[..., None] * b_re
    return ar, ai, bbr, bbi


def _block_diag(t):
    G, a, b = t.shape
    eye = jnp.eye(G, dtype=F32)
    return (t[:, :, None, :] * eye[:, None, :, None]).reshape(G * a, G * b)


def _rope_tables(S):
    half = RET_DK // 2
    inv = ROPE_BASE ** (-jnp.arange(half, dtype=F32) / half)
    ang = jnp.arange(S, dtype=F32)[:, None] * inv[None, :]
    cos, sin = jnp.cos(ang), jnp.sin(ang)
    return jnp.concatenate([cos, cos], axis=1), jnp.concatenate([-sin, sin], axis=1)


def _ret_log_decays():
    f = tuple(float(np.log1p(-np.exp2(np.float32(-5.0 - h)))) for h in range(RET_HEADS))
    b = tuple(float(np.log1p(-np.exp2(np.float32(-5.5 - h)))) for h in range(RET_HEADS))
    return f, b


def assemble_weight(name, land, own, me, col_sharded):
    _, R, C = land.shape
    if col_sharded:
        tr = min(R, 256)

        def body(me_ref, land_ref, own_ref, o_ref):
            for d in range(N_DEV):
                o_ref[:, d * C:(d + 1) * C] = jnp.where(me_ref[0] == d, own_ref[...], land_ref[d])

        grid, out_shape = (R // tr,), (R, N_DEV * C)
        in_specs = [pl.BlockSpec((N_DEV, tr, C), lambda i, p: (0, i, 0)), pl.BlockSpec((tr, C), lambda i, p: (i, 0))]
        out_spec = pl.BlockSpec((tr, N_DEV * C), lambda i, p: (i, 0))
    else:
        def body(me_ref, land_ref, own_ref, o_ref):
            o_ref[...] = jnp.where(me_ref[0] == pl.program_id(0), own_ref[...], land_ref[...])

        grid, out_shape = (N_DEV,), (N_DEV * R, C)
        in_specs = [pl.BlockSpec((None, R, C), lambda d, p: (d, 0, 0)), pl.BlockSpec((R, C), lambda d, p: (0, 0))]
        out_spec = pl.BlockSpec((R, C), lambda d, p: (d, 0))
    return pl.pallas_call(
        body, name=name,
        grid_spec=pltpu.PrefetchScalarGridSpec(num_scalar_prefetch=1, grid=grid, in_specs=in_specs, out_specs=out_spec),
        out_shape=jax.ShapeDtypeStruct(out_shape, land.dtype), compiler_params=_params("parallel"),
    )(me, land, own)


def _row(t):
    return t.reshape(1, -1)


def mixer_stage(layer, S, resid, branch, w_in, w_out, carriers, small):
    j = layer // 2
    tag = f"l{layer}"
    g = _row(small["mix_norm_g"][layer])
    if branch is None:
        x = resid
        (h,) = make_rowop(tag + "_norm", norm_f, (D_MODEL,))((x,), (g,), ())
    else:
        x, h = make_rowop(tag + "_addnorm", addnorm_f, (D_MODEL, D_MODEL))((resid, branch), (g,), ())
    if layer % 2 == 0:
        lbs = _hgrn_lower_bounds(small["hgrn_lb_logits"])
        prm = (small["gla_wa2"][j, 0], small["gla_wa2"][j, 1], _row(small["gla_ba"][j, 0]), _row(small["gla_ba"][j, 1]),
               _row(lbs[0, j]), _row(lbs[1, j]))
        gq, gk, gv, glaf, glab, gr, hq, hkf, hkb, hv, hlaf, hlab, hg = make_proj_stage(tag + "_prep", even_prep_f, EVEN_PREP_WIDTHS)(
            h, w_in, carriers["in32"], carriers["in16"], prm, ())
        of, ob = make_scan(tag + "_gla", GLA_HEADS, GLA_DK, GLA_DV, S, 2 * SCAN_CHUNK)(gq, gk, gk, gv, glaf, glab)
        hof, hob = make_scan(tag + "_hgrn", HGRN_HEADS, HGRN_DK, HGRN_DV, S, SCAN_CHUNK)(hq, hkf, hkb, hv, hlaf, hlab)
        (y,) = make_rowop(tag + "_post", even_post_f, (D_MODEL,))(
            (of, ob, hof, hob, gr, hg), (_row(small["gla_norm_g"][j]), _row(small["hgrn_norm_g"][j])), ())
    else:
        cosf, sinf = _rope_tables(S)
        rq, rk, rv, rg, su = make_proj_stage(tag + "_prep", odd_prep_f, ODD_PREP_WIDTHS)(
            h, w_in, carriers["in32"], carriers["in16"], (), (cosf, sinf))
        of, ob = make_scan(tag + "_ret", RET_HEADS, RET_DK, RET_DV, S, 2 * SCAN_CHUNK, const_lg=_ret_log_decays())(rq, rk, rk, rv)
        (cm,) = make_rowop(tag + "_retpost", ret_post_f, (RET_HEADS * RET_DV,))((of, ob, rg), (_row(small["ret_norm_g"][j]),), ())
        disc = [_s5_discretise(small["s5_lam_re"][j, d], small["s5_lam_im"][j, d], small["s5_log_dt"][j, d],
                               small["s5_b_re"][j], small["s5_b_im"][j]) for d in range(2)]
        a4 = jnp.stack([t.reshape(-1) for d in range(2) for t in disc[d][:2]], axis=0)
        su_p = make_reorder(tag + "_s5seg", S, True)(su)
        Xs = tuple(make_mm_f32w(f"{tag}_s5in{2 * d + i}")(su_p, _block_diag(jnp.swapaxes(disc[d][2 + i], 1, 2)))
                   for d in range(2) for i in range(2))
        Hs = make_s5_scan(tag + "_s5scan", S)(Xs, a4)
        prm = (_block_diag(jnp.swapaxes(small["s5_c_re"][j], 1, 2)), _block_diag(jnp.swapaxes(small["s5_c_im"][j], 1, 2)),
               _row(small["s5_d"][j]), small["s5_glu_w"][j], _row(small["s5_glu_b"][j]))
        (dm_p,) = make_rowop(tag + "_s5post", s5_post_f, (S5_WIDTH,))((*Hs, su_p), prm, ())
        y = jnp.concatenate([cm, make_reorder(tag + "_s5time", S, False)(dm_p)], axis=1)
    return x, make_mm(tag + "_out")(y, w_out, carriers["out32"], carriers["out16"])


def ffn_stage(layer, S, resid, branch, w_up, w_down, carriers, small):
    tag = f"l{layer}"
    x, hf = make_rowop(tag + "_ffnnorm", addnorm_f, (D_MODEL, D_MODEL))((resid, branch), (_row(small["ffn_norm_g"][layer]),), ())
    out = make_ffn(tag + "_ffn", S)(hf, w_up, carriers["up32"], carriers["up16"], w_down, carriers["down32"], carriers["down16"],
                                   small["ffn_conv_w"][layer], _row(small["ffn_conv_b"][layer]))
    return x, out


BIG = {"w_in_even": 2, "w_out_even": 1, "w_in_odd": 2, "w_out_odd": 1, "ffn_w_up": 2, "ffn_w_down": 1}
SMALL_SHARDED = {"gla_wa2": 3, "gla_ba": 2, "hgrn_lb_logits": 2, "ret_norm_g": 1, "s5_d": 1, "s5_glu_w": 1, "s5_glu_b": 1,
                 "ffn_conv_w": 2}
REPLICATED = ("mix_norm_g", "ffn_norm_g", "final_norm_g", "gla_norm_g", "hgrn_norm_g", "s5_lam_re", "s5_lam_im", "s5_log_dt",
              "s5_b_re", "s5_b_im", "s5_c_re", "s5_c_im", "ffn_conv_b")
WEIGHTS = ("mix_norm_g", "ffn_norm_g", "final_norm_g", "w_in_even", "w_out_even", "gla_wa2", "gla_ba", "gla_norm_g",
           "hgrn_lb_logits", "hgrn_norm_g", "w_in_odd", "w_out_odd", "ret_norm_g", "s5_lam_re", "s5_lam_im", "s5_log_dt",
           "s5_b_re", "s5_b_im", "s5_c_re", "s5_c_im", "s5_d", "s5_glu_w", "s5_glu_b", "ffn_w_up", "ffn_conv_w", "ffn_conv_b",
           "ffn_w_down")
PACK_COLS = 512
MIXER_SMALL = (("mix_norm_g", "hgrn_lb_logits", "gla_wa2", "gla_ba", "gla_norm_g", "hgrn_norm_g"),
               ("mix_norm_g", "ret_norm_g", "s5_lam_re", "s5_lam_im", "s5_log_dt", "s5_b_re", "s5_b_im", "s5_c_re", "s5_c_im",
                "s5_d", "s5_glu_w", "s5_glu_b"))
FFN_SMALL = ("ffn_norm_g", "ffn_conv_w", "ffn_conv_b")


def _unshard(g, axis):
    t = jnp.moveaxis(g, 0, axis)
    return t.reshape(t.shape[:axis] + (t.shape[axis] * t.shape[axis + 1],) + t.shape[axis + 2:])


def _to_blocks(full, axis):
    t = full.reshape(full.shape[:axis] + (N_DEV, full.shape[axis] // N_DEV) + full.shape[axis + 1:])
    return jnp.moveaxis(t, axis, 0)


def _pack(arrs):
    flat = jnp.concatenate([a.reshape(-1) for a in arrs])
    pad = (-flat.shape[0]) % (PACK_COLS * SUBLANES)
    return jnp.pad(flat, (0, pad)).reshape(-1, PACK_COLS)


def _unpack(packed, shapes):
    flat = packed.reshape(-1)
    out, o = [], 0
    for s in shapes:
        n = int(np.prod(s))
        out.append(flat[o:o + n].reshape(s))
        o += n
    return out


def _flat2d(a):
    return a.reshape(-1, a.shape[-1])


def kernel(x, mix_norm_g, ffn_norm_g, final_norm_g, w_in_even, w_out_even, gla_wa2, gla_ba, gla_norm_g, hgrn_lb_logits, hgrn_norm_g, w_in_odd, w_out_odd, ret_norm_g, s5_lam_re, s5_lam_im, s5_log_dt, s5_b_re, s5_b_im, s5_c_re, s5_c_im, s5_d, s5_glu_w, s5_glu_b, ffn_w_up, ffn_conv_w, ffn_conv_b, ffn_w_down, loss_target, m_mix_norm_g, m_ffn_norm_g, m_final_norm_g, m_w_in_even, m_w_out_even, m_gla_wa2, m_gla_ba, m_gla_norm_g, m_hgrn_lb_logits, m_hgrn_norm_g, m_w_in_odd, m_w_out_odd, m_ret_norm_g, m_s5_lam_re, m_s5_lam_im, m_s5_log_dt, m_s5_b_re, m_s5_b_im, m_s5_c_re, m_s5_c_im, m_s5_d, m_s5_glu_w, m_s5_glu_b, m_ffn_w_up, m_ffn_conv_w, m_ffn_conv_b, m_ffn_w_down, v_mix_norm_g, v_ffn_norm_g, v_final_norm_g, v_w_in_even, v_w_out_even, v_gla_wa2, v_gla_ba, v_gla_norm_g, v_hgrn_lb_logits, v_hgrn_norm_g, v_w_in_odd, v_w_out_odd, v_ret_norm_g, v_s5_lam_re, v_s5_lam_im, v_s5_log_dt, v_s5_b_re, v_s5_b_im, v_s5_c_re, v_s5_c_im, v_s5_d, v_s5_glu_w, v_s5_glu_b, v_ffn_w_up, v_ffn_conv_w, v_ffn_conv_b, v_ffn_w_down):
    args = locals()
    w = {n: args[n] for n in WEIGHTS}
    m = {n: args["m_" + n] for n in WEIGHTS}
    v = {n: args["v_" + n] for n in WEIGHTS}
    Bl, S, D = x.shape
    T = Bl * S
    ix, iy, ic = lax.axis_index("x"), lax.axis_index("y"), lax.axis_index("c")
    me = 4 * ix + 2 * iy + ic

    xt = x.reshape(T, D)
    me1 = me.reshape(1).astype(jnp.int32)
    stages = []
    for layer in range(DEPTH):
        j = layer // 2
        kin, kout = ("w_in_even", "w_out_even") if layer % 2 == 0 else ("w_in_odd", "w_out_odd")
        stages.append((mixer_stage, layer, ("in", "out"), ((kin, j, True), (kout, j, False)), MIXER_SMALL[layer % 2]))
        stages.append((ffn_stage, layer, ("up", "down"), (("ffn_w_up", layer, True), ("ffn_w_down", layer, False)), FFN_SMALL))

    gather, after = [], xt
    for s, (_, _, _, projs, _) in enumerate(stages):
        handle = split_start(f"gather{s}_start", "gather", [w[n][l].astype(BF16) for n, l, _ in projs], None, after)
        gather.append(handle)
        after = handle[-1]
    sm_names = list(SMALL_SHARDED)
    (sm_all8,) = all_gather("gather_small", [_pack([w[n] for n in sm_names])])
    sm_flat, small, o = sm_all8.reshape(N_DEV, -1), {}, 0
    for n in sm_names:
        size = int(np.prod(w[n].shape))
        small[n] = _unshard(sm_flat[:, o:o + size].reshape((N_DEV,) + w[n].shape), SMALL_SHARDED[n])
        o += size
    small.update({n: w[n] for n in REPLICATED})

    resid, branch, pulls = xt, None, []
    for s, (fn, layer, keys, projs, sm_keys) in enumerate(stages):
        lands = split_wait(f"gather{s}_wait", "gather", gather[s], lax.stop_gradient(resid))
        second = split_start(f"forward{s}_start", "forward", [], lands, lax.stop_gradient(resid))
        lands = split_wait(f"forward{s}_wait", "forward", second, second[-1])
        full, carriers = [], {}
        for key, land, (n, l, col) in zip(keys, lands, projs, strict=True):
            full.append(assemble_weight(f"weight{s}_{key}", land, w[n][l].astype(BF16), me1, col))
            carriers[key + "32"] = jnp.zeros(land.shape, F32)
            carriers[key + "16"] = jnp.zeros(land.shape, BF16)
        sm = {n: small[n] for n in sm_keys}
        run = functools.partial(fn, layer, S)
        if branch is None:
            (resid, branch), pull = jax.vjp(lambda r, c, p, run=run, full=full: run(r, None, full[0], full[1], c, p), resid, carriers, sm)
        else:
            (resid, branch), pull = jax.vjp(lambda r, b, c, p, run=run, full=full: run(r, b, full[0], full[1], c, p), resid, branch, carriers, sm)
        pulls.append(pull)

    loss_acc, dxf, dgf = loss_head(resid, branch, small["final_norm_g"].reshape(1, D), loss_target.reshape(T, D))
    loss = lax.psum(loss_acc[0, 0], ("x", "y", "c"))
    g_small = {"final_norm_g": dgf.reshape(D)}
    d_resid, d_branch, token = dxf, dxf, None
    scatter, own32 = [None] * len(stages), [None] * len(stages)
    for s in reversed(range(len(stages))):
        if token is not None:
            d_resid = lax.dynamic_update_slice(d_resid, d_resid[:SUBLANES, :LANES] + token, (0, 0))
        if s == 0:
            d_resid, dcar, dsm = pulls[s]((d_resid, d_branch))
        else:
            d_resid, d_branch, dcar, dsm = pulls[s]((d_resid, d_branch))
        for n, g in dsm.items():
            g_small[n] = g_small[n] + g if n in g_small else g
        keys = stages[s][2]
        own32[s] = [dcar[k + "32"] for k in keys]
        scatter[s] = split_start(f"grads{s}_start", "scatter", [dcar[k + "16"] for k in keys], None, d_resid)
        token = scatter[s][-1]
    dx = d_resid

    out, chain = {}, {}
    for s in range(len(stages)):
        lands = split_wait(f"grads{s}_wait", "scatter", scatter[s], dx)
        for own, land, (n, l, _) in zip(own32[s], lands, stages[s][3], strict=True):
            chain[n] = adamw_sharded(f"adamw_{n}_{l}", l, _flat2d(w[n]), _flat2d(m[n]), _flat2d(v[n]), own, land, me1, chain.get(n))
    for n in BIG:
        out[n] = [t.reshape(w[n].shape) for t in chain[n]]

    sm_all = sm_names + list(REPLICATED)
    (g_all,) = all_gather("gather_small_grads", [_pack([g_small[n] for n in sm_all])])
    g_sum = _unpack(ordered_sum("sum_small_grads", g_all), [g_small[n].shape for n in sm_all])
    g_loc = []
    for n, g in zip(sm_all, g_sum):
        if n in SMALL_SHARDED:
            ax = SMALL_SHARDED[n]
            size = w[n].shape[ax]
            g = lax.dynamic_slice_in_dim(g, me * size, size, axis=ax)
        g_loc.append(g)
    shapes = [w[n].shape for n in sm_all]
    res = adamw("adamw_small", _pack([w[n] for n in sm_all]), _pack([m[n] for n in sm_all]), _pack([v[n] for n in sm_all]), _pack(g_loc))
    res = [g_loc] + [_unpack(r, shapes) for r in res]
    for k, n in enumerate(sm_all):
        out[n] = [r[k] for r in res]

    grads, deltas, new_m, new_v = ([out[n][k] for n in WEIGHTS] for k in range(4))
    return (loss, dx.reshape(Bl, S, D), *grads, *deltas, *new_m, *new_v)
```

```python
import functools
import math

import numpy as np
import jax
import jax.numpy as jnp
from jax import lax
from jax.experimental import pallas as pl
from jax.experimental.pallas import tpu as pltpu

F32 = jnp.float32
BF16 = jnp.bfloat16
HIGHEST = lax.Precision.HIGHEST
MESH = pl.DeviceIdType.MESH

D_MODEL = 1024
DEPTH = 4
N_EVEN = 2
N_ODD = 2
GLA_HEADS, GLA_DK, GLA_DV, GLA_RANK, GLA_GATE_NORM = 4, 64, 128, 16, 16.0
HGRN_HEADS, HGRN_DK, HGRN_DV, HGRN_MIN_F = 4, 64, 128, 1e-20
RET_HEADS, RET_DK, RET_DV = 4, 128, 192
ROPE_BASE = 10000.0
S5_WIDTH, S5_GROUP_CH, S5_GROUPS, S5_STATE = 256, 16, 16, 64
S5_N = S5_GROUPS * S5_STATE
FFN_DIM = 2816
EPS = 1e-6
EVEN_IN = 3360
ODD_IN = 2816
ADAM_LR, ADAM_B1, ADAM_B2, ADAM_EPS, ADAM_WD, ADAM_STEP = 0.001, 0.9, 0.999, 1e-08, 0.01, 10

N_DEV = 8
VMEM_LIMIT_BYTES = 56 * 1024 * 1024
ROW_TILE = 256
SCAN_CHUNK = 64
S5_SEGMENTS = 8
LANES = 128


def _params(*sem):
    return pltpu.CompilerParams(dimension_semantics=sem, vmem_limit_bytes=VMEM_LIMIT_BYTES)


def _divisor_tile(n, cap):
    best = None
    for t in range(LANES, min(n, cap) + 1, LANES):
        if n % t == 0:
            best = t
    return best if best is not None else n


def _mm_nn(name, x, w):
    M, K = x.shape
    N = w.shape[1]
    tn = _divisor_tile(N, 2048) if K * N * 2 > 8 * 2**20 else N
    tm = 256 if tn * 4 * 512 > 6 * 2**20 else 512
    assert M % tm == 0 and N % tn == 0

    def body(x_ref, w_ref, o_ref):
        o_ref[...] = jnp.dot(x_ref[...].astype(BF16), w_ref[...], preferred_element_type=F32)

    return pl.pallas_call(
        body, name=name, grid=(N // tn, M // tm),
        in_specs=[pl.BlockSpec((tm, K), lambda j, i: (i, 0)), pl.BlockSpec((K, tn), lambda j, i: (0, j))],
        out_specs=pl.BlockSpec((tm, tn), lambda j, i: (i, j)),
        out_shape=jax.ShapeDtypeStruct((M, N), F32),
        compiler_params=_params("parallel", "parallel"),
    )(x, w)


def _mm_nt(name, dy, w):
    M, N = dy.shape
    K = w.shape[0]
    tk = _divisor_tile(K, 1024) if K * N * 2 > 8 * 2**20 else K
    tm = 256 if N >= 4096 else 512
    assert M % tm == 0 and K % tk == 0

    def body(dy_ref, w_ref, o_ref):
        o_ref[...] = lax.dot_general(dy_ref[...].astype(BF16), w_ref[...], (((1,), (1,)), ((), ())),
                                     preferred_element_type=F32)

    return pl.pallas_call(
        body, name=name, grid=(K // tk, M // tm),
        in_specs=[pl.BlockSpec((tm, N), lambda j, i: (i, 0)), pl.BlockSpec((tk, N), lambda j, i: (j, 0))],
        out_specs=pl.BlockSpec((tm, tk), lambda j, i: (i, j)),
        out_shape=jax.ShapeDtypeStruct((M, K), F32),
        compiler_params=_params("parallel", "parallel"),
    )(dy, w)


MM_TN_VMEM_BUDGET = 40 * 2**20


def _pad_lanes(n):
    return -(-n // LANES) * LANES


def _mm_tn(name, x, dy, nblk, want16):
    M, K = x.shape
    N = dy.shape[1]
    n = N // nblk
    xb, yb = x.dtype.itemsize, dy.dtype.itemsize
    best = None
    for tk in [t for t in range(LANES, K + 1, LANES) if K % t == 0]:
        for tm in (512, 256):
            out_bytes = nblk * tk * _pad_lanes(n) * (6 if want16 else 4)
            vmem = 2 * out_bytes + 2 * tm * tk * xb + 2 * tm * _pad_lanes(N) * yb
            traffic = (K // tk) * M * N * yb + M * K * xb
            if vmem <= MM_TN_VMEM_BUDGET and M % tm == 0 and (best is None or (traffic, -tm) < best[0]):
                best = ((traffic, -tm), tk, tm)
    _, tk, tm = best
    last = M // tm - 1

    def body(x_ref, dy_ref, o32_ref, *o16_ref):
        m = pl.program_id(1)

        @pl.when(m == 0)
        def _():
            o32_ref[...] = jnp.zeros_like(o32_ref)

        dyv = dy_ref[...].astype(BF16)
        if nblk == 1:
            o32_ref[0] += lax.dot_general(x_ref[...].astype(BF16), dyv, (((0,), (0,)), ((), ())), preferred_element_type=F32)
        else:
            xt = x_ref[...].astype(F32).T.astype(BF16)
            for d in range(nblk):
                o32_ref[d] += jnp.dot(xt, dyv[:, d * n:(d + 1) * n], preferred_element_type=F32)
        if want16:
            @pl.when(m == last)
            def _():
                o16_ref[0][...] = o32_ref[...].astype(BF16)

    blk = pl.BlockSpec((nblk, tk, n), lambda a, m: (0, a, 0))
    return pl.pallas_call(
        body, name=name, grid=(K // tk, M // tm),
        in_specs=[pl.BlockSpec((tm, tk), lambda a, m: (m, a)), pl.BlockSpec((tm, N), lambda a, m: (m, 0))],
        out_specs=[blk, blk] if want16 else [blk],
        out_shape=[jax.ShapeDtypeStruct((nblk, K, n), F32)] + ([jax.ShapeDtypeStruct((nblk, K, n), BF16)] if want16 else []),
        compiler_params=_params("parallel", "arbitrary"),
    )(x, dy)


def _weight_grads(name, x, dy, col_sharded):
    if col_sharded:
        return _mm_tn(name, x, dy, N_DEV, True)
    d32, d16 = _mm_tn(name, x, dy, 1, True)
    K, N = d32.shape[1:]
    return d32.reshape(N_DEV, K // N_DEV, N), d16.reshape(N_DEV, K // N_DEV, N)


def make_mm(name, col_sharded=False):
    @jax.custom_vjp
    def mm(x, w16, c32, c16):
        return _mm_nn(name + "_fwd", x, w16)

    def fwd(x, w16, c32, c16):
        return _mm_nn(name + "_fwd", x, w16), (x, w16)

    def bwd(res, dy):
        x, w16 = res
        d32, d16 = _weight_grads(name + "_dw", x, dy, col_sharded)
        return _mm_nt(name + "_dx", dy, w16), jnp.zeros_like(w16), d32, d16

    mm.defvjp(fwd, bwd)
    return mm


def make_mm_f32w(name):
    @jax.custom_vjp
    def mm(x, w):
        return _mm_nn(name + "_fwd", x, w.astype(BF16))

    def fwd(x, w):
        w16 = w.astype(BF16)
        return _mm_nn(name + "_fwd", x, w16), (x, w16)

    def bwd(res, dy):
        x, w16 = res
        return _mm_nt(name + "_dx", dy, w16), _mm_tn(name + "_dw", x, dy, 1, False)[0][0]

    mm.defvjp(fwd, bwd)
    return mm


def _row_specs(rows, params, consts, tile):
    specs = [pl.BlockSpec((tile, r.shape[1]), lambda i: (i, 0)) for r in rows]
    specs += [pl.BlockSpec(p.shape, lambda i: (0, 0)) for p in params]
    specs += [pl.BlockSpec((tile, c.shape[1]), lambda i, n=c.shape[0] // tile: (i % n, 0)) for c in consts]
    return specs


def _row_fwd(name, f, out_widths, tile, rows, params, consts):
    T = rows[0].shape[0]
    nr, npar, ncon = len(rows), len(params), len(consts)

    def body(*refs):
        r = tuple(x[...] for x in refs[:nr])
        p = tuple(x[...] for x in refs[nr:nr + npar])
        c = tuple(x[...] for x in refs[nr + npar:nr + npar + ncon])
        outs = f(r, p, c)
        for o_ref, o in zip(refs[nr + npar + ncon:], outs, strict=True):
            o_ref[...] = o

    return pl.pallas_call(
        body, name=name + "_fwd", grid=(T // tile,),
        in_specs=_row_specs(rows, params, consts, tile),
        out_specs=[pl.BlockSpec((tile, w), lambda i: (i, 0)) for w in out_widths],
        out_shape=[jax.ShapeDtypeStruct((T, w), F32) for w in out_widths],
        compiler_params=_params("parallel"),
    )(*rows, *params, *consts)


def _row_bwd(name, f, out_widths, tile, rows, params, consts, gouts, dr_dtype=F32):
    T = rows[0].shape[0]
    nr, npar, ncon, nout = len(rows), len(params), len(consts), len(out_widths)

    def body(*refs):
        r = tuple(x[...] for x in refs[:nr])
        p = tuple(x[...] for x in refs[nr:nr + npar])
        c = tuple(x[...] for x in refs[nr + npar:nr + npar + ncon])
        k = nr + npar + ncon
        g = tuple(x[...] for x in refs[k:k + nout])
        dr_refs = refs[k + nout:k + nout + nr]
        dp_refs = refs[k + nout + nr:]
        _, vjp = jax.vjp(lambda r_, p_: tuple(f(r_, p_, c)), r, p)
        dr, dp = vjp(g)
        for ref, val in zip(dr_refs, dr, strict=True):
            ref[...] = val.astype(ref.dtype)
        if npar:
            @pl.when(pl.program_id(0) == 0)
            def _():
                for ref in dp_refs:
                    ref[...] = jnp.zeros_like(ref)

            for ref, val in zip(dp_refs, dp, strict=True):
                ref[...] += val

    outs = pl.pallas_call(
        body, name=name + "_bwd", grid=(T // tile,),
        in_specs=_row_specs(rows, params, consts, tile) + [pl.BlockSpec((tile, w), lambda i: (i, 0)) for w in out_widths],
        out_specs=[pl.BlockSpec((tile, r.shape[1]), lambda i: (i, 0)) for r in rows]
        + [pl.BlockSpec(p.shape, lambda i: (0, 0)) for p in params],
        out_shape=[jax.ShapeDtypeStruct(r.shape, dr_dtype) for r in rows] + [jax.ShapeDtypeStruct(p.shape, F32) for p in params],
        compiler_params=_params("arbitrary"),
    )(*rows, *params, *consts, *gouts)
    return tuple(outs[:nr]), tuple(outs[nr:])


def make_proj_stage(name, f, out_widths, tile=ROW_TILE):
    def run(x, w16, params, consts):
        p = _mm_nn(name + "_mm", x, w16)
        return p, tuple(_row_fwd(name, f, out_widths, tile, (p,), params, consts))

    @jax.custom_vjp
    def op(x, w16, c32, c16, params, consts):
        return run(x, w16, params, consts)[1]

    def fwd(x, w16, c32, c16, params, consts):
        p, outs = run(x, w16, params, consts)
        return outs, (x, w16, p, params, consts)

    def bwd(res, g):
        x, w16, p, params, consts = res
        (dp,), dparams = _row_bwd(name, f, out_widths, tile, (p,), params, consts, tuple(g), dr_dtype=BF16)
        d32, d16 = _weight_grads(name + "_dw", x, dp, True)
        return _mm_nt(name + "_dx", dp, w16), jnp.zeros_like(w16), d32, d16, dparams, tuple(jnp.zeros_like(c) for c in consts)

    op.defvjp(fwd, bwd)
    return op


def make_rowop(name, f, out_widths, tile=ROW_TILE):
    @jax.custom_vjp
    def op(rows, params, consts):
        return tuple(_row_fwd(name, f, out_widths, tile, rows, params, consts))

    def fwd(rows, params, consts):
        return op(rows, params, consts), (rows, params, consts)

    def bwd(res, g):
        rows, params, consts = res
        dr, dp = _row_bwd(name, f, out_widths, tile, rows, params, consts, tuple(g))
        return dr, dp, tuple(jnp.zeros_like(c) for c in consts)

    op.defvjp(fwd, bwd)
    return op


def _rms(x, g):
    return x * lax.rsqrt(jnp.mean(x * x, axis=-1, keepdims=True) + EPS) * g


def _silu(x):
    return x * jax.nn.sigmoid(x)


def _bdot(a, b):
    return jnp.dot(a.astype(BF16), b.astype(BF16), preferred_element_type=F32)


def norm_f(rows, params, consts):
    return (_rms(rows[0], params[0]),)


def addnorm_f(rows, params, consts):
    x = rows[0] + rows[1]
    return x, _rms(x, params[0])


EVEN_GLA_END = 1568


def even_prep_f(rows, params, consts):
    (p,) = rows
    wa2f, wa2b, baf, bab, lbf, lbb = params
    gq = p[:, 0:256]
    gk = p[:, 256:512] * (GLA_DK ** -0.5)
    gv = p[:, 512:1024]
    gr = p[:, 1024:1536]
    glaf = jax.nn.log_sigmoid(_bdot(p[:, 1536:1552], wa2f) + baf) / GLA_GATE_NORM
    glab = jax.nn.log_sigmoid(_bdot(p[:, 1552:1568], wa2b) + bab) / GLA_GATE_NORM
    o = EVEN_GLA_END
    hq = _silu(p[:, o:o + 256])

    def gate(z, lb):
        f = lb + (1.0 - lb) * jax.nn.sigmoid(z)
        return jnp.log(jnp.maximum(f, HGRN_MIN_F)), (1.0 - lb) * jax.nn.sigmoid(-z)

    hlaf, hkf = gate(p[:, o + 256:o + 512], lbf)
    hlab, hkb = gate(p[:, o + 512:o + 768], lbb)
    hv = p[:, o + 768:o + 1280]
    hg = p[:, o + 1280:o + 1792]
    return gq, gk, gv, glaf, glab, gr, hq, hkf, hkb, hv, hlaf, hlab, hg


EVEN_PREP_WIDTHS = (256, 256, 512, 256, 256, 512, 256, 256, 256, 512, 256, 256, 512)


def _head_rms(o, g, heads, d):
    parts = []
    for h in range(heads):
        seg = o[:, h * d:(h + 1) * d]
        parts.append(seg * lax.rsqrt(jnp.mean(seg * seg, axis=-1, keepdims=True) + EPS))
    return jnp.concatenate(parts, axis=1) * g


def even_post_f(rows, params, consts):
    of, ob, hof, hob, gr, hg = rows
    a = _head_rms(of + ob, params[0], GLA_HEADS, GLA_DV) * _silu(gr)
    b = _head_rms(hof + hob, params[1], HGRN_HEADS, HGRN_DV) * _silu(hg)
    return (jnp.concatenate([a, b], axis=1),)


@jax.custom_vjp
def _roll_half(x):
    return pltpu.roll(x, RET_DK // 2, 1)


_roll_half.defvjp(lambda x: (_roll_half(x), None), lambda _, g: (_roll_half(g),))


def odd_prep_f(rows, params, consts):
    (p,) = rows
    cosf, sinf = consts

    def rot(t):
        parts = []
        for h in range(RET_HEADS):
            th = t[:, h * RET_DK:(h + 1) * RET_DK]
            parts.append(th * cosf + _roll_half(th) * sinf)
        return jnp.concatenate(parts, axis=1)

    rq = rot(p[:, 0:512])
    rk = rot(p[:, 512:1024]) * (RET_DK ** -0.5)
    return rq, rk, p[:, 1024:1792], p[:, 1792:2560], p[:, 2560:2816]


ODD_PREP_WIDTHS = (512, 512, 768, 768, 256)


def ret_post_f(rows, params, consts):
    of, ob, rg = rows
    o = of + ob
    parts = []
    for h in range(RET_HEADS):
        seg = o[:, h * RET_DV:(h + 1) * RET_DV]
        c = seg - jnp.mean(seg, axis=-1, keepdims=True)
        parts.append(c * lax.rsqrt(jnp.mean(c * c, axis=-1, keepdims=True) + EPS))
    return (jnp.concatenate(parts, axis=1) * params[0] * _silu(rg),)


def s5_post_f(rows, params, consts):
    h0r, h0i, h1r, h1i, u = rows
    c_re, c_im, d_skip, glu_w, glu_b = params
    hr = h0r + h1r
    hi = h0i + h1i
    y = _bdot(hr, c_re) - _bdot(hi, c_im) + d_skip * u
    g = jax.nn.gelu(y)
    return (g * jax.nn.sigmoid(_bdot(g, glu_w) + glu_b),)


def loss_head(x, r, g, target, tile=ROW_TILE):
    T, D = x.shape

    def body(x_ref, r_ref, g_ref, t_ref, loss_ref, dx_ref, dg_ref):
        t = t_ref[...]

        def lf(xv, gv):
            e = _rms(xv, gv) - t
            row = jnp.sum(e * e, axis=-1, keepdims=True) * (0.5 / D)
            return jnp.sum(row, axis=0, keepdims=True)

        l, vjp = jax.vjp(lf, x_ref[...] + r_ref[...], g_ref[...])
        dx, dg = vjp(jnp.ones((1, 1), F32))
        dx_ref[...] = dx

        @pl.when(pl.program_id(0) == 0)
        def _():
            loss_ref[...] = jnp.zeros_like(loss_ref)
            dg_ref[...] = jnp.zeros_like(dg_ref)

        loss_ref[...] += jnp.broadcast_to(l, loss_ref.shape)
        dg_ref[...] += dg

    row = pl.BlockSpec((tile, D), lambda i: (i, 0))
    vec = pl.BlockSpec((1, D), lambda i: (0, 0))
    return pl.pallas_call(
        body, name="loss_head", grid=(T // tile,),
        in_specs=[row, row, vec, row],
        out_specs=[pl.BlockSpec((1, LANES), lambda i: (0, 0)), row, vec],
        out_shape=[jax.ShapeDtypeStruct((1, LANES), F32), jax.ShapeDtypeStruct((T, D), F32), jax.ShapeDtypeStruct((1, D), F32)],
        compiler_params=_params("arbitrary"),
    )(x, r, g, target)


SUBLANES = 8


def _halo_specs(width, tile, T):
    n8 = tile // SUBLANES
    last = T // SUBLANES - 1
    return [pl.BlockSpec((tile, width), lambda i: (i, 0)),
            pl.BlockSpec((SUBLANES, width), lambda i: (jnp.maximum(i * n8 - 1, 0), 0)),
            pl.BlockSpec((SUBLANES, width), lambda i: (jnp.minimum((i + 1) * n8, last), 0))]


def _shift_rows(x, prev_row, next_row, tile):
    row = lax.broadcasted_iota(jnp.int32, (tile, 1), 0)
    down = jnp.where(row == 0, prev_row, pltpu.roll(x, 1, 0))
    up = jnp.where(row == tile - 1, next_row, pltpu.roll(x, tile - 1, 0))
    return down, up


def _conv_fwd(name, u, cw, cb, S, tile):
    T, F2 = u.shape
    F = F2 // 2
    per_seq = S // tile

    def body(u_ref, up_ref, un_ref, cw_ref, cb_ref, g_ref):
        pos = pl.program_id(0) % per_seq
        uv = u_ref[...]
        prev_row = jnp.where(pos == 0, 0.0, up_ref[SUBLANES - 1:SUBLANES, :])
        next_row = jnp.where(pos == per_seq - 1, 0.0, un_ref[0:1, :])
        down, up = _shift_rows(uv, prev_row, next_row, tile)
        c = _conv_taps(down, uv, up, cw_ref, cb_ref)
        g_ref[...] = (_silu(c[:, :F]) * c[:, F:]).astype(BF16)

    return pl.pallas_call(
        body, name=name + "_fwd", grid=(T // tile,),
        in_specs=_halo_specs(F2, tile, T) + [pl.BlockSpec((3, F2), lambda i: (0, 0)), pl.BlockSpec((1, F2), lambda i: (0, 0))],
        out_specs=pl.BlockSpec((tile, F), lambda i: (i, 0)),
        out_shape=jax.ShapeDtypeStruct((T, F), BF16),
        compiler_params=_params("parallel"),
    )(u, u, u, cw, cb)


def _conv_taps(down, mid, up, cw_ref, cb_ref):
    c = cb_ref[...] + down * cw_ref[0:1, :]
    c = c + mid * cw_ref[1:2, :]
    return c + up * cw_ref[2:3, :]


def _conv_bwd(name, u, dg, cw, cb, S, tile):
    T, F2 = u.shape
    F = F2 // 2
    per_seq = S // tile

    def dact(cv, dgv):
        a, v = cv[:, :F], cv[:, F:]
        sg = jax.nn.sigmoid(a)
        return jnp.concatenate([dgv * v * (sg * (1.0 + a * (1.0 - sg))), dgv * (a * sg)], axis=1)

    def body(u_ref, up_ref, un_ref, g_ref, gp_ref, gn_ref, cw_ref, cb_ref, du_ref, dw0_ref, dw1_ref, dw2_ref, db_ref):
        i = pl.program_id(0)
        pos = i % per_seq
        first, last = pos == 0, pos == per_seq - 1
        lo, hi = slice(SUBLANES - 1, SUBLANES), slice(0, 1)
        uv = u_ref[...]
        u_m1, u_p1 = jnp.where(first, 0.0, up_ref[lo, :]), jnp.where(last, 0.0, un_ref[hi, :])
        u_dn, u_up = _shift_rows(uv, u_m1, u_p1, tile)
        dc = dact(_conv_taps(u_dn, uv, u_up, cw_ref, cb_ref), g_ref[...])
        c_m1 = _conv_taps(up_ref[SUBLANES - 2:SUBLANES - 1, :], u_m1, u_ref[0:1, :], cw_ref, cb_ref)
        c_p1 = _conv_taps(u_ref[tile - 1:tile, :], u_p1, un_ref[1:2, :], cw_ref, cb_ref)
        dc_prev = jnp.where(first, 0.0, dact(c_m1, gp_ref[lo, :]))
        dc_next = jnp.where(last, 0.0, dact(c_p1, gn_ref[hi, :]))
        dc_dn, dc_up = _shift_rows(dc, dc_prev, dc_next, tile)
        du = dc_up * cw_ref[0:1, :]
        du = du + dc * cw_ref[1:2, :]
        du_ref[...] = (du + dc_dn * cw_ref[2:3, :]).astype(BF16)

        @pl.when(i == 0)
        def _():
            for ref in (dw0_ref, dw1_ref, dw2_ref, db_ref):
                ref[...] = jnp.zeros_like(ref)

        dw0_ref[...] += jnp.sum(dc * u_dn, axis=0, keepdims=True)
        dw1_ref[...] += jnp.sum(dc * uv, axis=0, keepdims=True)
        dw2_ref[...] += jnp.sum(dc * u_up, axis=0, keepdims=True)
        db_ref[...] += jnp.sum(dc, axis=0, keepdims=True)

    vec = pl.BlockSpec((1, F2), lambda i: (0, 0))
    du, dw0, dw1, dw2, db = pl.pallas_call(
        body, name=name + "_bwd", grid=(T // tile,),
        in_specs=_halo_specs(F2, tile, T) + _halo_specs(F, tile, T) + [pl.BlockSpec((3, F2), lambda i: (0, 0)), vec],
        out_specs=[pl.BlockSpec((tile, F2), lambda i: (i, 0)), vec, vec, vec, vec],
        out_shape=[jax.ShapeDtypeStruct((T, F2), BF16)] + [jax.ShapeDtypeStruct((1, F2), F32)] * 4,
        compiler_params=_params("arbitrary"),
    )(u, u, u, dg, dg, dg, cw, cb)
    return du, jnp.concatenate([dw0, dw1, dw2], axis=0), db


def make_ffn(name, S):
    def run(x, wup16, wdn16, cw, cb):
        u = _mm_nn(name + "_up", x, wup16)
        g16 = _conv_fwd(name + "_conv", u, cw, cb, S, ROW_TILE)
        return u, g16, _mm_nn(name + "_down", g16, wdn16)

    @jax.custom_vjp
    def op(x, wup16, cu32, cu16, wdn16, cd32, cd16, cw, cb):
        return run(x, wup16, wdn16, cw, cb)[2]

    def fwd(x, wup16, cu32, cu16, wdn16, cd32, cd16, cw, cb):
        u, g16, out = run(x, wup16, wdn16, cw, cb)
        return out, (x, wup16, wdn16, u, g16, cw, cb)

    def bwd(res, dout):
        x, wup16, wdn16, u, g16, cw, cb = res
        dg = _mm_nt(name + "_down_dx", dout, wdn16)
        dd32, dd16 = _weight_grads(name + "_down_dw", g16, dout, False)
        du16, dcw, dcb = _conv_bwd(name + "_conv", u, dg, cw, cb, S, ROW_TILE)
        du32, du16w = _weight_grads(name + "_up_dw", x, du16, True)
        return (_mm_nt(name + "_up_dx", du16, wup16), jnp.zeros_like(wup16), du32, du16w, jnp.zeros_like(wdn16), dd32, dd16, dcw, dcb)

    op.defvjp(fwd, bwd)
    return op


def _dot_nt(a, b):
    return lax.dot_general(a.astype(BF16), b.astype(BF16), (((1,), (1,)), ((), ())), preferred_element_type=F32)


def _dot_tn(a, b):
    return lax.dot_general(a.astype(BF16), b.astype(BF16), (((0,), (0,)), ((), ())), preferred_element_type=F32)


def _chunk_decays(la, direction, C, width, dk, const_lg):
    row = lax.broadcasted_iota(jnp.int32, (C, C), 0)
    col = lax.broadcasted_iota(jnp.int32, (C, C), 1)
    keep = (row >= col) if direction == 0 else (row <= col)
    ridx = lax.broadcasted_iota(jnp.int32, (C, 1), 0)
    if const_lg is None:
        cum = jnp.dot(keep.astype(F32), la, precision=HIGHEST, preferred_element_type=F32)
    else:
        lane_head = lax.broadcasted_iota(jnp.int32, (1, width), 1) // dk
        lg = jnp.zeros((1, width), F32)
        for h, val in enumerate(const_lg):
            lg = jnp.where(lane_head == h, val, lg)
        steps = (ridx + 1) if direction == 0 else (C - ridx)
        cum = steps.astype(F32) * lg
    exit_row = C - 1 if direction == 0 else 0
    mid = jnp.sum(jnp.where(ridx == C // 2, cum, 0.0), axis=0, keepdims=True)
    last = jnp.sum(jnp.where(ridx == exit_row, cum, 0.0), axis=0, keepdims=True)
    return keep, ridx == exit_row, cum, mid, last


def _scan_fwd(name, q, kf, kb, v, laf, lab, H, dk, dv, S, C, const_lg):
    T = q.shape[0]
    B, nc = T // S, S // C
    Wk, Wv = H * dk, H * dv
    learn = const_lg is None

    def body(*refs):
        if learn:
            qf_r, qb_r, kf_r, kb_r, vf_r, vb_r, laf_r, lab_r, of_r, ob_r, sf_r, sb_r, st = refs
            las = (laf_r[...], lab_r[...])
        else:
            qf_r, qb_r, kf_r, kb_r, vf_r, vb_r, of_r, ob_r, sf_r, sb_r, st = refs
            las = (None, None)

        @pl.when(pl.program_id(1) == 0)
        def _():
            st[...] = jnp.zeros_like(st)

        for d, (q_r, k_r, v_r, o_r, s_r) in enumerate(((qf_r, kf_r, vf_r, of_r, sf_r), (qb_r, kb_r, vb_r, ob_r, sb_r))):
            keep, _, cum, mid, last = _chunk_decays(las[d], d, C, Wk, dk, None if learn else const_lg[d])
            qe = q_r[...] * jnp.exp(cum - mid)
            ke = k_r[...] * jnp.exp(mid - cum)
            q_in = qe * jnp.exp(mid)
            k_out = ke * jnp.exp(last - mid)
            e_last = jnp.exp(last)
            vv = v_r[...]
            for h in range(H):
                ks, vs = slice(h * dk, (h + 1) * dk), slice(h * dv, (h + 1) * dv)
                a = jnp.where(keep, _dot_nt(qe[:, ks], ke[:, ks]), 0.0)
                state = st[d, h]
                o_r[:, vs] = _bdot(a, vv[:, vs]) + _dot_nt(q_in[:, ks], state)
                s_r[h * dv:(h + 1) * dv, :] = state
                st[d, h] = state * e_last[:, ks] + _dot_tn(vv[:, vs], k_out[:, ks])

    fpos = lambda b, c: (b * nc + c, 0)
    bpos = lambda b, c: (b * nc + nc - 1 - c, 0)
    kspec = lambda pos: pl.BlockSpec((C, Wk), pos)
    vspec = lambda pos: pl.BlockSpec((C, Wv), pos)
    sspec = lambda pos: pl.BlockSpec((None, Wv, dk), lambda b, c: pos(b, c) + (0,))
    ins = [q, q, kf, kb, v, v] + ([laf, lab] if learn else [])
    in_specs = [kspec(fpos), kspec(bpos), kspec(fpos), kspec(bpos), vspec(fpos), vspec(bpos)] + ([kspec(fpos), kspec(bpos)] if learn else [])
    return pl.pallas_call(
        body, name=name + "_fwd", grid=(B, nc), in_specs=in_specs,
        out_specs=[vspec(fpos), vspec(bpos), sspec(fpos), sspec(bpos)],
        out_shape=[jax.ShapeDtypeStruct((T, Wv), F32)] * 2 + [jax.ShapeDtypeStruct((B * nc, Wv, dk), F32)] * 2,
        scratch_shapes=[pltpu.VMEM((2, H, dv, dk), F32)],
        compiler_params=_params("parallel", "arbitrary"),
    )(*ins)


def _scan_bwd(name, q, kf, kb, v, laf, lab, sf, sb, dof, dob, H, dk, dv, S, C, const_lg):
    T = q.shape[0]
    B, nc = T // S, S // C
    Wk, Wv = H * dk, H * dv
    learn = const_lg is None

    def body(*refs):
        if learn:
            (qf_r, qb_r, kf_r, kb_r, vf_r, vb_r, laf_r, lab_r, sf_r, sb_r, dof_r, dob_r,
             dqf_r, dqb_r, dkf_r, dkb_r, dvf_r, dvb_r, dlaf_r, dlab_r, dst) = refs
            las, dlas = (laf_r[...], lab_r[...]), (dlaf_r, dlab_r)
        else:
            (qf_r, qb_r, kf_r, kb_r, vf_r, vb_r, sf_r, sb_r, dof_r, dob_r,
             dqf_r, dqb_r, dkf_r, dkb_r, dvf_r, dvb_r, dst) = refs
            las, dlas = (None, None), (None, None)

        @pl.when(pl.program_id(1) == 0)
        def _():
            dst[...] = jnp.zeros_like(dst)

        groups = ((qf_r, kf_r, vf_r, sf_r, dof_r, dqf_r, dkf_r, dvf_r), (qb_r, kb_r, vb_r, sb_r, dob_r, dqb_r, dkb_r, dvb_r))
        for d, (q_r, k_r, v_r, s_r, do_r, dq_r, dk_r, dv_r) in enumerate(groups):
            keep, is_exit, cum, mid, last = _chunk_decays(las[d], d, C, Wk, dk, None if learn else const_lg[d])
            eq, ek = jnp.exp(cum - mid), jnp.exp(mid - cum)
            e_in, e_out, e_last = jnp.exp(mid), jnp.exp(last - mid), jnp.exp(last)
            qe, ke = q_r[...] * eq, k_r[...] * ek
            q_in, k_out = qe * e_in, ke * e_out
            vv, do = v_r[...], do_r[...]
            dqe_parts, dke_parts, dlast_parts = [], [], []
            for h in range(H):
                ks, vs = slice(h * dk, (h + 1) * dk), slice(h * dv, (h + 1) * dv)
                a = jnp.where(keep, _dot_nt(qe[:, ks], ke[:, ks]), 0.0)
                dp = jnp.where(keep, _dot_nt(do[:, vs], vv[:, vs]), 0.0)
                s_prev = s_r[h * dv:(h + 1) * dv, :]
                ds = dst[d, h]
                dk_out = _bdot(vv[:, vs], ds)
                dqe_parts.append(_bdot(dp, ke[:, ks]) + _bdot(do[:, vs], s_prev) * e_in[:, ks])
                dke_parts.append(_dot_tn(dp, qe[:, ks]) + dk_out * e_out[:, ks])
                dv_r[:, vs] = _dot_tn(a, do[:, vs]) + _dot_nt(k_out[:, ks], ds)
                if learn:
                    dlast_parts.append(jnp.sum(dk_out * k_out[:, ks], axis=0, keepdims=True)
                                       + jnp.sum(ds * s_prev, axis=0, keepdims=True) * e_last[:, ks])
                dst[d, h] = ds * e_last[:, ks] + _dot_tn(do[:, vs], q_in[:, ks])
            dqe = jnp.concatenate(dqe_parts, axis=1)
            dke = jnp.concatenate(dke_parts, axis=1)
            dq_r[...] = dqe * eq
            dk_r[...] = dke * ek
            if learn:
                dcum = dqe * qe - dke * ke + jnp.where(is_exit, jnp.concatenate(dlast_parts, axis=1), 0.0)
                dlas[d][...] = lax.dot_general(keep.astype(F32), dcum, (((0,), (0,)), ((), ())), precision=HIGHEST,
                                               preferred_element_type=F32)

    fpos = lambda b, c: (b * nc + nc - 1 - c, 0)
    bpos = lambda b, c: (b * nc + c, 0)
    kspec = lambda pos: pl.BlockSpec((C, Wk), pos)
    vspec = lambda pos: pl.BlockSpec((C, Wv), pos)
    sspec = lambda pos: pl.BlockSpec((None, Wv, dk), lambda b, c: pos(b, c) + (0,))
    ins = [q, q, kf, kb, v, v] + ([laf, lab] if learn else []) + [sf, sb, dof, dob]
    in_specs = ([kspec(fpos), kspec(bpos), kspec(fpos), kspec(bpos), vspec(fpos), vspec(bpos)]
                + ([kspec(fpos), kspec(bpos)] if learn else []) + [sspec(fpos), sspec(bpos), vspec(fpos), vspec(bpos)])
    n_k = 6 if learn else 4
    out_specs = [kspec(fpos), kspec(bpos), kspec(fpos), kspec(bpos), vspec(fpos), vspec(bpos)] + ([kspec(fpos), kspec(bpos)] if learn else [])
    out_shape = [jax.ShapeDtypeStruct((T, Wk), F32)] * 4 + [jax.ShapeDtypeStruct((T, Wv), F32)] * 2 + ([jax.ShapeDtypeStruct((T, Wk), F32)] * 2 if learn else [])
    del n_k
    return pl.pallas_call(
        body, name=name + "_bwd", grid=(B, nc), in_specs=in_specs, out_specs=out_specs, out_shape=out_shape,
        scratch_shapes=[pltpu.VMEM((2, H, dv, dk), F32)],
        compiler_params=_params("parallel", "arbitrary"),
    )(*ins)


def make_scan(name, H, dk, dv, S, C=SCAN_CHUNK, const_lg=None):
    if const_lg is None:
        @jax.custom_vjp
        def op(q, kf, kb, v, laf, lab):
            return tuple(_scan_fwd(name, q, kf, kb, v, laf, lab, H, dk, dv, S, C, None)[:2])

        def fwd(q, kf, kb, v, laf, lab):
            of, ob, sf, sb = _scan_fwd(name, q, kf, kb, v, laf, lab, H, dk, dv, S, C, None)
            return (of, ob), (q, kf, kb, v, laf, lab, sf, sb)

        def bwd(res, g):
            q, kf, kb, v, laf, lab, sf, sb = res
            dqf, dqb, dkf, dkb, dvf, dvb, dlaf, dlab = _scan_bwd(name, q, kf, kb, v, laf, lab, sf, sb, g[0], g[1], H, dk, dv, S, C, None)
            return dqf + dqb, dkf, dkb, dvf + dvb, dlaf, dlab
    else:
        @jax.custom_vjp
        def op(q, kf, kb, v):
            return tuple(_scan_fwd(name, q, kf, kb, v, None, None, H, dk, dv, S, C, const_lg)[:2])

        def fwd(q, kf, kb, v):
            of, ob, sf, sb = _scan_fwd(name, q, kf, kb, v, None, None, H, dk, dv, S, C, const_lg)
            return (of, ob), (q, kf, kb, v, sf, sb)

        def bwd(res, g):
            q, kf, kb, v, sf, sb = res
            dqf, dqb, dkf, dkb, dvf, dvb = _scan_bwd(name, q, kf, kb, v, None, None, sf, sb, g[0], g[1], H, dk, dv, S, C, const_lg)
            return dqf + dqb, dkf, dkb, dvf + dvb

    op.defvjp(fwd, bwd)
    return op


def _reorder_call(name, t, S, to_segments):
    T, w = t.shape
    n_it = S // S5_SEGMENTS

    def body(x_ref, o_ref):
        def step(i, carry):
            packed = pl.ds(pl.multiple_of(i * S5_SEGMENTS, S5_SEGMENTS), S5_SEGMENTS)
            spread = pl.ds(i, S5_SEGMENTS, stride=n_it)
            if to_segments:
                o_ref[packed, :] = x_ref[spread, :]
            else:
                o_ref[spread, :] = x_ref[packed, :]
            return carry

        lax.fori_loop(0, n_it, step, 0, unroll=8)

    blk = pl.BlockSpec((S, LANES), lambda b, j: (b, j))
    return pl.pallas_call(body, name=name, grid=(T // S, w // LANES), in_specs=[blk], out_specs=blk,
                          out_shape=jax.ShapeDtypeStruct(t.shape, t.dtype), compiler_params=_params("parallel", "parallel"))(t)


def make_reorder(name, S, to_segments):
    @jax.custom_vjp
    def op(t):
        return _reorder_call(name, t, S, to_segments)

    op.defvjp(lambda t: (_reorder_call(name, t, S, to_segments), None),
              lambda _, g: (_reorder_call(name + "_bwd", g, S, not to_segments),))
    return op


def _s5_scan_call(name, Xs, A, S, dirs):
    T, N = Xs[0].shape
    B, nl = T // S, N // LANES
    n_it = S // S5_SEGMENTS
    assert n_it & (n_it - 1) == 0

    def cmul(ar, ai, br, bi):
        return ar * br - ai * bi, ar * bi + ai * br

    def body(x0r, x0i, x1r, x1i, a_ref, h0r, h0i, h1r, h1i):
        seg = lax.broadcasted_iota(jnp.int32, (S5_SEGMENTS, 1), 0)
        for k, (xr, xi, hr, hi) in enumerate(((x0r, x0i, h0r, h0i), (x1r, x1i, h1r, h1i))):
            back = dirs[k] == 1
            ar = jnp.broadcast_to(a_ref[2 * k:2 * k + 1, :], (S5_SEGMENTS, LANES))
            ai = jnp.broadcast_to(a_ref[2 * k + 1:2 * k + 2, :], (S5_SEGMENTS, LANES))
            rows_of = lambda i: pl.ds(pl.multiple_of(((n_it - 1 - i) if back else i) * S5_SEGMENTS, S5_SEGMENTS), S5_SEGMENTS)

            def local(i, carry):
                sr, si = carry
                rows = rows_of(i)
                pr, pi = cmul(ar, ai, sr, si)
                sr, si = pr + xr[rows, :], pi + xi[rows, :]
                hr[rows, :] = sr
                hi[rows, :] = si
                return sr, si

            zero = jnp.zeros((S5_SEGMENTS, LANES), F32)
            er, ei = lax.fori_loop(0, n_it, local, (zero, zero), unroll=8)
            pr, pi = ar, ai
            for _ in range(n_it.bit_length() - 1):
                pr, pi = cmul(pr, pi, pr, pi)
            shift = (S5_SEGMENTS - 1) if back else 1
            tr, ti = er, ei
            order = range(S5_SEGMENTS - 2, -1, -1) if back else range(1, S5_SEGMENTS)
            for r in order:
                nr, ni = cmul(pr, pi, pltpu.roll(tr, shift, 0), pltpu.roll(ti, shift, 0))
                tr = jnp.where(seg == r, er + nr, tr)
                ti = jnp.where(seg == r, ei + ni, ti)
            edge = (S5_SEGMENTS - 1) if back else 0
            cr = jnp.where(seg == edge, 0.0, pltpu.roll(tr, shift, 0))
            ci = jnp.where(seg == edge, 0.0, pltpu.roll(ti, shift, 0))

            def fix(i, carry):
                wr, wi = carry
                rows = rows_of(i)
                fr, fi = cmul(wr, wi, cr, ci)
                hr[rows, :] = hr[rows, :] + fr
                hi[rows, :] = hi[rows, :] + fi
                return cmul(wr, wi, ar, ai)

            lax.fori_loop(0, n_it, fix, (ar, ai), unroll=8)

    col = pl.BlockSpec((S, LANES), lambda b, j: (b, j))
    return tuple(pl.pallas_call(
        body, name=name, grid=(B, nl),
        in_specs=[col] * 4 + [pl.BlockSpec((4, LANES), lambda b, j: (0, j))],
        out_specs=[col] * 4,
        out_shape=[jax.ShapeDtypeStruct((T, N), F32)] * 4,
        compiler_params=_params("parallel", "parallel"),
    )(*Xs, A))


def _s5_decay_grad_call(name, lam, H, X, tile=ROW_TILE):
    T, N = X[0].shape

    def body(*refs):
        l_refs, h_refs, x_refs, p_ref = refs[0:4], refs[4:8], refs[8:12], refs[12]

        @pl.when(pl.program_id(0) == 0)
        def _():
            p_ref[...] = jnp.zeros_like(p_ref)

        for k in range(2):
            re, im = 2 * k, 2 * k + 1
            ur, ui = h_refs[re][...] - x_refs[re][...], h_refs[im][...] - x_refs[im][...]
            lr, li = l_refs[re][...], l_refs[im][...]
            p_ref[re:re + 1, :] += jnp.sum(lr * ur + li * ui, axis=0, keepdims=True)
            p_ref[im:im + 1, :] += jnp.sum(li * ur - lr * ui, axis=0, keepdims=True)

    row = pl.BlockSpec((tile, N), lambda i: (i, 0))
    return pl.pallas_call(
        body, name=name, grid=(T // tile,), in_specs=[row] * 12,
        out_specs=pl.BlockSpec((4, N), lambda i: (0, 0)), out_shape=jax.ShapeDtypeStruct((4, N), F32),
        compiler_params=_params("arbitrary"),
    )(*lam, *H, *X)


def make_s5_scan(name, S):
    @jax.custom_vjp
    def op(X, A):
        return _s5_scan_call(name + "_fwd", X, A, S, (0, 1))

    def fwd(X, A):
        H = _s5_scan_call(name + "_fwd", X, A, S, (0, 1))
        return H, (X, A, H)

    def bwd(res, G):
        X, A, H = res
        conj = A * jnp.array([[1.0], [-1.0], [1.0], [-1.0]], F32)
        lam = _s5_scan_call(name + "_bwd", tuple(G), conj, S, (1, 0))
        P = _s5_decay_grad_call(name + "_dA", lam, H, X)
        ar, ai = A[0::2], A[1::2]
        pr, pi = P[0::2], P[1::2]
        den = ar * ar + ai * ai
        dar, dai = (pr * ar - pi * ai) / den, (pr * ai + pi * ar) / den
        return lam, jnp.stack([dar[0], dai[0], dar[1], dai[1]], axis=0)

    op.defvjp(fwd, bwd)
    return op


ANY = pl.BlockSpec(memory_space=pl.ANY)


def _place():
    x, y, c = lax.axis_index("x"), lax.axis_index("y"), lax.axis_index("c")
    return x, y, c, [(1 - x, y), (x, 1 - y), (1 - x, 1 - y)]


def all_gather(name, arrs):
    n = len(arrs)

    def body(*refs):
        ins, outs = refs[:n], refs[n:2 * n]
        send, recv, lsem = refs[2 * n:]
        x, y, c, chips = _place()
        me, sibling = (x, y, c), (x, y, 1 - c)

        def copy(a, k, block, to, src=None):
            slot = outs[a].at[4 * block[0] + 2 * block[1] + block[2]]
            return pltpu.make_async_remote_copy(src_ref=slot if src is None else src, dst_ref=slot, send_sem=send.at[a, k],
                                                recv_sem=recv.at[a, k], device_id=to, device_id_type=MESH)

        mine = [pltpu.make_async_copy(ins[a], outs[a].at[4 * x + 2 * y + c], lsem.at[a]) for a in range(n)]
        first = []
        for a in range(n):
            mine[a].start()
            first.append(copy(a, 0, me, sibling, src=ins[a]))
            first += [copy(a, 1 + j, me, (*chip, c), src=ins[a]) for j, chip in enumerate(chips)]
        for cp in first:
            cp.start()
        passed = []
        for a in range(n):
            for j, chip in enumerate(chips):
                copy(a, 1 + j, (*chip, c), me).wait_recv()
                fwd = copy(a, 4 + j, (*chip, c), sibling)
                fwd.start()
                passed.append(fwd)
        for a in range(n):
            copy(a, 0, sibling, me).wait_recv()
            for j, chip in enumerate(chips):
                copy(a, 4 + j, (*chip, 1 - c), me).wait_recv()
        for cp in first + passed:
            cp.wait_send()
        for cp in mine:
            cp.wait()

    return pl.pallas_call(
        body, name=name, in_specs=[ANY] * n, out_specs=[ANY] * n,
        out_shape=[jax.ShapeDtypeStruct((N_DEV,) + a.shape, a.dtype) for a in arrs],
        scratch_shapes=[pltpu.SemaphoreType.DMA((n, 7)), pltpu.SemaphoreType.DMA((n, 7)), pltpu.SemaphoreType.DMA((n,))],
    )(*arrs)


HBM = pl.BlockSpec(memory_space=pltpu.HBM)
SEM = pl.BlockSpec(memory_space=pltpu.SEMAPHORE)
EFFECT = pltpu.SideEffectType.DATAFLOW_SIDE_EFFECTING
GATHER_PEERS = (1, 2, 4, 6)
OTHER_CHIPS = (2, 4, 6)
COPIES_PER_ARRAY = {"scatter": N_DEV - 1, "gather": len(GATHER_PEERS), "forward": len(OTHER_CHIPS)}


def _split_plan(mode, srcs, lands, send, recv):
    x, y, c = lax.axis_index("x"), lax.axis_index("y"), lax.axis_index("c")

    def dev(k):
        return (1 - x if k & 4 else x), (1 - y if k & 2 else y), (1 - c if k & 1 else c)

    def idx(d):
        return 4 * d[0] + 2 * d[1] + d[2]

    me = idx((x, y, c))
    plan = []
    for a, land in enumerate(lands):
        if mode == "scatter":
            legs = [(srcs[a].at[idx(dev(k))], land.at[me], land.at[idx(dev(k))], dev(k)) for k in range(1, N_DEV)]
        elif mode == "gather":
            legs = [(srcs[a], land.at[me], land.at[idx(dev(k))], dev(k)) for k in GATHER_PEERS]
        else:
            legs = [(land.at[idx(dev(j))], land.at[idx(dev(j))], land.at[idx(dev(j ^ 1))], dev(1)) for j in OTHER_CHIPS]
        for i, (src, dst, arrival, to) in enumerate(legs):
            sem = a * len(legs) + i
            pair = tuple(pltpu.make_async_remote_copy(src_ref=src, dst_ref=d, send_sem=send.at[sem], recv_sem=recv.at[sem],
                                                      device_id=to, device_id_type=MESH) for d in (dst, arrival))
            plan.append(pair)
    return plan


def split_start(name, mode, srcs, lands, after):
    if lands is None:
        lands = [lax.empty((N_DEV,) + (s.shape[1:] if mode == "scatter" else s.shape), s.dtype) for s in srcs]
    ns, nl = len(srcs), len(lands)
    nsem = COPIES_PER_ARRAY[mode] * nl

    def body(*refs):
        ins, lnd = refs[:ns], refs[ns:ns + nl]
        send, recv = refs[ns + nl + 1], refs[ns + nl + 2]
        token = refs[-1]
        for out, _ in _split_plan(mode, ins, lnd, send, recv):
            out.start()
        token[...] = jnp.zeros_like(token)

    arrs = list(srcs) + list(lands)
    return pl.pallas_call(
        body, name=name,
        out_shape=(pltpu.SemaphoreType.DMA((nsem,)), pltpu.SemaphoreType.DMA((nsem,)))
        + tuple(pltpu.HBM(t.shape, t.dtype) for t in arrs) + (jax.ShapeDtypeStruct((SUBLANES, LANES), F32),),
        in_specs=[HBM] * len(arrs) + [ANY],
        out_specs=(SEM, SEM) + (HBM,) * len(arrs) + (pl.BlockSpec(memory_space=pltpu.VMEM),),
        input_output_aliases={i: 2 + i for i in range(len(arrs))},
        compiler_params=pltpu.CompilerParams(has_side_effects=EFFECT),
    )(*[pltpu.with_memory_space_constraint(t, pltpu.HBM) for t in arrs], after)


def split_wait(name, mode, handle, after):
    send, recv = handle[0], handle[1]
    arrs = list(handle[2:-1])
    nl = len(arrs) if mode == "forward" else len(arrs) // 2
    ns = len(arrs) - nl

    def body(*refs):
        ins, lnd = refs[:ns], refs[ns:ns + nl]
        s, r = refs[ns + nl], refs[ns + nl + 1]
        for out, arrival in _split_plan(mode, ins, lnd, s, r):
            out.wait_send()
            arrival.wait_recv()

    outs = pl.pallas_call(
        body, name=name,
        out_shape=tuple(pltpu.HBM(t.shape, t.dtype) for t in arrs),
        in_specs=[HBM] * len(arrs) + [SEM, SEM, ANY], out_specs=(HBM,) * len(arrs),
        input_output_aliases={i: i for i in range(len(arrs))},
        compiler_params=pltpu.CompilerParams(has_side_effects=EFFECT),
    )(*arrs, send, recv, after)
    return list(outs[ns:])


def _row_tile(rows, cols):
    cap = max(SUBLANES, (2**18 // cols) // SUBLANES * SUBLANES)
    if rows <= cap:
        return rows
    for t in range(cap, SUBLANES - 1, -SUBLANES):
        if rows % t == 0:
            return t
    return rows


def ordered_sum(name, parts):
    n, R, C = parts.shape
    rt = _row_tile(R, C)

    def body(p_ref, o_ref):
        s = p_ref[0]
        for k in range(1, n):
            s = s + p_ref[k]
        o_ref[...] = s

    return pl.pallas_call(
        body, name=name, grid=(R // rt,),
        in_specs=[pl.BlockSpec((n, rt, C), lambda r: (0, r, 0))], out_specs=pl.BlockSpec((rt, C), lambda r: (r, 0)),
        out_shape=jax.ShapeDtypeStruct((R, C), F32), compiler_params=_params("parallel"),
    )(parts)


def _adamw_update(w, m, v, g):
    bias1 = 1.0 - ADAM_B1 ** ADAM_STEP
    bias2 = 1.0 - ADAM_B2 ** ADAM_STEP
    m_new = ADAM_B1 * m + (1.0 - ADAM_B1) * g
    v_new = ADAM_B2 * v + (1.0 - ADAM_B2) * (g * g)
    delta = -ADAM_LR * ((m_new / bias1) / (jnp.sqrt(v_new / bias2) + ADAM_EPS) + ADAM_WD * w)
    return delta, m_new, v_new


def adamw(name, w, m, v, g):
    R, C = w.shape
    rt = _row_tile(R, C)

    def body(w_ref, m_ref, v_ref, g_ref, d_ref, mo_ref, vo_ref):
        d_ref[...], mo_ref[...], vo_ref[...] = _adamw_update(w_ref[...], m_ref[...], v_ref[...], g_ref[...])

    row = pl.BlockSpec((rt, C), lambda r: (r, 0))
    return pl.pallas_call(
        body, name=name, grid=(R // rt,), in_specs=[row] * 4, out_specs=[row] * 3,
        out_shape=[jax.ShapeDtypeStruct((R, C), F32)] * 3, compiler_params=_params("parallel"),
    )(w, m, v, g)


def adamw_sharded(name, layer, w, m, v, own, land, me, prev):
    _, R, C = own.shape
    rt = _row_tile(R, C)
    base = layer * (R // rt)

    def body(me_ref, w_ref, m_ref, v_ref, own_ref, land_ref, *rest):
        go_ref, d_ref, mo_ref, vo_ref = rest[-4:]
        g = own_ref[...]
        for k in range(N_DEV):
            g = g + jnp.where(me_ref[0] == k, 0.0, land_ref[k].astype(F32))
        go_ref[...] = g
        d_ref[...], mo_ref[...], vo_ref[...] = _adamw_update(w_ref[...], m_ref[...], v_ref[...], g)

    row = pl.BlockSpec((rt, C), lambda r, p: (base + r, 0))
    in_specs = [row, row, row, pl.BlockSpec((None, rt, C), lambda r, p: (p[0], r, 0)), pl.BlockSpec((N_DEV, rt, C), lambda r, p: (0, r, 0))]
    ins = [me, w, m, v, own, land]
    aliases = {}
    if prev is not None:
        in_specs += [ANY] * 4
        aliases = {len(ins) + k: k for k in range(4)}
        ins += list(prev)
    return pl.pallas_call(
        body, name=name,
        grid_spec=pltpu.PrefetchScalarGridSpec(num_scalar_prefetch=1, grid=(R // rt,), in_specs=in_specs, out_specs=[row] * 4),
        out_shape=[jax.ShapeDtypeStruct(w.shape, F32)] * 4, input_output_aliases=aliases,
        compiler_params=_params("arbitrary"),
    )(*ins)


def _hgrn_lower_bounds(lb_logits):
    p = jax.nn.softmax(lb_logits, axis=1)
    return jnp.cumsum(p, axis=1) - p[:, :1]


def _s5_discretise(lam_re, lam_im, log_dt, b_re, b_im):
    lr = jnp.minimum(lam_re, -1e-4)
    li = lam_im
    dt = jnp.exp(log_dt)[:, None]
    mag = jnp.exp(lr * dt)
    ar, ai = mag * jnp.cos(li * dt), mag * jnp.sin(li * dt)
    den = lr * lr + li * li
    nr = ar - 1.0
    cr = (nr * lr + ai * li) / den
    ci = (ai * lr - nr * li) / den
    bbr = cr[..., None] * b_re - ci[..., None] * b_im
    bbi = cr[..., None] * b_im + ci[..., None] * b_re
    return ar, ai, bbr, bbi


def _block_diag(t):
    G, a, b = t.shape
    eye = jnp.eye(G, dtype=F32)
    return (t[:, :, None, :] * eye[:, None, :, None]).reshape(G * a, G * b)


def _rope_tables(S):
    half = RET_DK // 2
    inv = ROPE_BASE ** (-jnp.arange(half, dtype=F32) / half)
    ang = jnp.arange(S, dtype=F32)[:, None] * inv[None, :]
    cos, sin = jnp.cos(ang), jnp.sin(ang)
    return jnp.concatenate([cos, cos], axis=1), jnp.concatenate([-sin, sin], axis=1)


def _ret_log_decays():
    f = tuple(float(np.log1p(-np.exp2(np.float32(-5.0 - h)))) for h in range(RET_HEADS))
    b = tuple(float(np.log1p(-np.exp2(np.float32(-5.5 - h)))) for h in range(RET_HEADS))
    return f, b


def assemble_weight(name, land, own, me, col_sharded):
    _, R, C = land.shape
    if col_sharded:
        tr = min(R, 256)

        def body(me_ref, land_ref, own_ref, o_ref):
            for d in range(N_DEV):
                o_ref[:, d * C:(d + 1) * C] = jnp.where(me_ref[0] == d, own_ref[...], land_ref[d])

        grid, out_shape = (R // tr,), (R, N_DEV * C)
        in_specs = [pl.BlockSpec((N_DEV, tr, C), lambda i, p: (0, i, 0)), pl.BlockSpec((tr, C), lambda i, p: (i, 0))]
        out_spec = pl.BlockSpec((tr, N_DEV * C), lambda i, p: (i, 0))
    else:
        def body(me_ref, land_ref, own_ref, o_ref):
            o_ref[...] = jnp.where(me_ref[0] == pl.program_id(0), own_ref[...], land_ref[...])

        grid, out_shape = (N_DEV,), (N_DEV * R, C)
        in_specs = [pl.BlockSpec((None, R, C), lambda d, p: (d, 0, 0)), pl.BlockSpec((R, C), lambda d, p: (0, 0))]
        out_spec = pl.BlockSpec((R, C), lambda d, p: (d, 0))
    return pl.pallas_call(
        body, name=name,
        grid_spec=pltpu.PrefetchScalarGridSpec(num_scalar_prefetch=1, grid=grid, in_specs=in_specs, out_specs=out_spec),
        out_shape=jax.ShapeDtypeStruct(out_shape, land.dtype), compiler_params=_params("parallel"),
    )(me, land, own)


def _row(t):
    return t.reshape(1, -1)


def mixer_stage(layer, S, resid, branch, w_in, w_out, carriers, small):
    j = layer // 2
    tag = f"l{layer}"
    g = _row(small["mix_norm_g"][layer])
    if branch is None:
        x = resid
        (h,) = make_rowop(tag + "_norm", norm_f, (D_MODEL,))((x,), (g,), ())
    else:
        x, h = make_rowop(tag + "_addnorm", addnorm_f, (D_MODEL, D_MODEL))((resid, branch), (g,), ())
    if layer % 2 == 0:
        lbs = _hgrn_lower_bounds(small["hgrn_lb_logits"])
        prm = (small["gla_wa2"][j, 0], small["gla_wa2"][j, 1], _row(small["gla_ba"][j, 0]), _row(small["gla_ba"][j, 1]),
               _row(lbs[0, j]), _row(lbs[1, j]))
        gq, gk, gv, glaf, glab, gr, hq, hkf, hkb, hv, hlaf, hlab, hg = make_proj_stage(tag + "_prep", even_prep_f, EVEN_PREP_WIDTHS)(
            h, w_in, carriers["in32"], carriers["in16"], prm, ())
        of, ob = make_scan(tag + "_gla", GLA_HEADS, GLA_DK, GLA_DV, S, 2 * SCAN_CHUNK)(gq, gk, gk, gv, glaf, glab)
        hof, hob = make_scan(tag + "_hgrn", HGRN_HEADS, HGRN_DK, HGRN_DV, S, SCAN_CHUNK)(hq, hkf, hkb, hv, hlaf, hlab)
        (y,) = make_rowop(tag + "_post", even_post_f, (D_MODEL,))(
            (of, ob, hof, hob, gr, hg), (_row(small["gla_norm_g"][j]), _row(small["hgrn_norm_g"][j])), ())
    else:
        cosf, sinf = _rope_tables(S)
        rq, rk, rv, rg, su = make_proj_stage(tag + "_prep", odd_prep_f, ODD_PREP_WIDTHS)(
            h, w_in, carriers["in32"], carriers["in16"], (), (cosf, sinf))
        of, ob = make_scan(tag + "_ret", RET_HEADS, RET_DK, RET_DV, S, 2 * SCAN_CHUNK, const_lg=_ret_log_decays())(rq, rk, rk, rv)
        (cm,) = make_rowop(tag + "_retpost", ret_post_f, (RET_HEADS * RET_DV,))((of, ob, rg), (_row(small["ret_norm_g"][j]),), ())
        disc = [_s5_discretise(small["s5_lam_re"][j, d], small["s5_lam_im"][j, d], small["s5_log_dt"][j, d],
                               small["s5_b_re"][j], small["s5_b_im"][j]) for d in range(2)]
        a4 = jnp.stack([t.reshape(-1) for d in range(2) for t in disc[d][:2]], axis=0)
        su_p = make_reorder(tag + "_s5seg", S, True)(su)
        Xs = tuple(make_mm_f32w(f"{tag}_s5in{2 * d + i}")(su_p, _block_diag(jnp.swapaxes(disc[d][2 + i], 1, 2)))
                   for d in range(2) for i in range(2))
        Hs = make_s5_scan(tag + "_s5scan", S)(Xs, a4)
        prm = (_block_diag(jnp.swapaxes(small["s5_c_re"][j], 1, 2)), _block_diag(jnp.swapaxes(small["s5_c_im"][j], 1, 2)),
               _row(small["s5_d"][j]), small["s5_glu_w"][j], _row(small["s5_glu_b"][j]))
        (dm_p,) = make_rowop(tag + "_s5post", s5_post_f, (S5_WIDTH,))((*Hs, su_p), prm, ())
        y = jnp.concatenate([cm, make_reorder(tag + "_s5time", S, False)(dm_p)], axis=1)
    return x, make_mm(tag + "_out")(y, w_out, carriers["out32"], carriers["out16"])


def ffn_stage(layer, S, resid, branch, w_up, w_down, carriers, small):
    tag = f"l{layer}"
    x, hf = make_rowop(tag + "_ffnnorm", addnorm_f, (D_MODEL, D_MODEL))((resid, branch), (_row(small["ffn_norm_g"][layer]),), ())
    out = make_ffn(tag + "_ffn", S)(hf, w_up, carriers["up32"], carriers["up16"], w_down, carriers["down32"], carriers["down16"],
                                   small["ffn_conv_w"][layer], _row(small["ffn_conv_b"][layer]))
    return x, out


BIG = {"w_in_even": 2, "w_out_even": 1, "w_in_odd": 2, "w_out_odd": 1, "ffn_w_up": 2, "ffn_w_down": 1}
SMALL_SHARDED = {"gla_wa2": 3, "gla_ba": 2, "hgrn_lb_logits": 2, "ret_norm_g": 1, "s5_d": 1, "s5_glu_w": 1, "s5_glu_b": 1,
                 "ffn_conv_w": 2}
REPLICATED = ("mix_norm_g", "ffn_norm_g", "final_norm_g", "gla_norm_g", "hgrn_norm_g", "s5_lam_re", "s5_lam_im", "s5_log_dt",
              "s5_b_re", "s5_b_im", "s5_c_re", "s5_c_im", "ffn_conv_b")
WEIGHTS = ("mix_norm_g", "ffn_norm_g", "final_norm_g", "w_in_even", "w_out_even", "gla_wa2", "gla_ba", "gla_norm_g",
           "hgrn_lb_logits", "hgrn_norm_g", "w_in_odd", "w_out_odd", "ret_norm_g", "s5_lam_re", "s5_lam_im", "s5_log_dt",
           "s5_b_re", "s5_b_im", "s5_c_re", "s5_c_im", "s5_d", "s5_glu_w", "s5_glu_b", "ffn_w_up", "ffn_conv_w", "ffn_conv_b",
           "ffn_w_down")
PACK_COLS = 512
MIXER_SMALL = (("mix_norm_g", "hgrn_lb_logits", "gla_wa2", "gla_ba", "gla_norm_g", "hgrn_norm_g"),
               ("mix_norm_g", "ret_norm_g", "s5_lam_re", "s5_lam_im", "s5_log_dt", "s5_b_re", "s5_b_im", "s5_c_re", "s5_c_im",
                "s5_d", "s5_glu_w", "s5_glu_b"))
FFN_SMALL = ("ffn_norm_g", "ffn_conv_w", "ffn_conv_b")


def _unshard(g, axis):
    t = jnp.moveaxis(g, 0, axis)
    return t.reshape(t.shape[:axis] + (t.shape[axis] * t.shape[axis + 1],) + t.shape[axis + 2:])


def _to_blocks(full, axis):
    t = full.reshape(full.shape[:axis] + (N_DEV, full.shape[axis] // N_DEV) + full.shape[axis + 1:])
    return jnp.moveaxis(t, axis, 0)


def _pack(arrs):
    flat = jnp.concatenate([a.reshape(-1) for a in arrs])
    pad = (-flat.shape[0]) % (PACK_COLS * SUBLANES)
    return jnp.pad(flat, (0, pad)).reshape(-1, PACK_COLS)


def _unpack(packed, shapes):
    flat = packed.reshape(-1)
    out, o = [], 0
    for s in shapes:
        n = int(np.prod(s))
        out.append(flat[o:o + n].reshape(s))
        o += n
    return out


def _flat2d(a):
    return a.reshape(-1, a.shape[-1])


def kernel(x, mix_norm_g, ffn_norm_g, final_norm_g, w_in_even, w_out_even, gla_wa2, gla_ba, gla_norm_g, hgrn_lb_logits, hgrn_norm_g, w_in_odd, w_out_odd, ret_norm_g, s5_lam_re, s5_lam_im, s5_log_dt, s5_b_re, s5_b_im, s5_c_re, s5_c_im, s5_d, s5_glu_w, s5_glu_b, ffn_w_up, ffn_conv_w, ffn_conv_b, ffn_w_down, loss_target, m_mix_norm_g, m_ffn_norm_g, m_final_norm_g, m_w_in_even, m_w_out_even, m_gla_wa2, m_gla_ba, m_gla_norm_g, m_hgrn_lb_logits, m_hgrn_norm_g, m_w_in_odd, m_w_out_odd, m_ret_norm_g, m_s5_lam_re, m_s5_lam_im, m_s5_log_dt, m_s5_b_re, m_s5_b_im, m_s5_c_re, m_s5_c_im, m_s5_d, m_s5_glu_w, m_s5_glu_b, m_ffn_w_up, m_ffn_conv_w, m_ffn_conv_b, m_ffn_w_down, v_mix_norm_g, v_ffn_norm_g, v_final_norm_g, v_w_in_even, v_w_out_even, v_gla_wa2, v_gla_ba, v_gla_norm_g, v_hgrn_lb_logits, v_hgrn_norm_g, v_w_in_odd, v_w_out_odd, v_ret_norm_g, v_s5_lam_re, v_s5_lam_im, v_s5_log_dt, v_s5_b_re, v_s5_b_im, v_s5_c_re, v_s5_c_im, v_s5_d, v_s5_glu_w, v_s5_glu_b, v_ffn_w_up, v_ffn_conv_w, v_ffn_conv_b, v_ffn_w_down):
    args = locals()
    w = {n: args[n] for n in WEIGHTS}
    m = {n: args["m_" + n] for n in WEIGHTS}
    v = {n: args["v_" + n] for n in WEIGHTS}
    Bl, S, D = x.shape
    T = Bl * S
    ix, iy, ic = lax.axis_index("x"), lax.axis_index("y"), lax.axis_index("c")
    me = 4 * ix + 2 * iy + ic

    xt = x.reshape(T, D)
    me1 = me.reshape(1).astype(jnp.int32)
    stages = []
    for layer in range(DEPTH):
        j = layer // 2
        kin, kout = ("w_in_even", "w_out_even") if layer % 2 == 0 else ("w_in_odd", "w_out_odd")
        stages.append((mixer_stage, layer, ("in", "out"), ((kin, j, True), (kout, j, False)), MIXER_SMALL[layer % 2]))
        stages.append((ffn_stage, layer, ("up", "down"), (("ffn_w_up", layer, True), ("ffn_w_down", layer, False)), FFN_SMALL))

    gather, after = [], xt
    for s, (_, _, _, projs, _) in enumerate(stages):
        handle = split_start(f"gather{s}_start", "gather", [w[n][l].astype(BF16) for n, l, _ in projs], None, after)
        gather.append(handle)
        after = handle[-1]
    sm_names = list(SMALL_SHARDED)
    (sm_all8,) = all_gather("gather_small", [_pack([w[n] for n in sm_names])])
    sm_flat, small, o = sm_all8.reshape(N_DEV, -1), {}, 0
    for n in sm_names:
        size = int(np.prod(w[n].shape))
        small[n] = _unshard(sm_flat[:, o:o + size].reshape((N_DEV,) + w[n].shape), SMALL_SHARDED[n])
        o += size
    small.update({n: w[n] for n in REPLICATED})
    small["mix_norm_g"] = small["mix_norm_g"] + after[0, 0]

    resid, branch, pulls = xt, None, []
    for s, (fn, layer, keys, projs, sm_keys) in enumerate(stages):
        lands = split_wait(f"gather{s}_wait", "gather", gather[s], lax.stop_gradient(resid))
        second = split_start(f"forward{s}_start", "forward", [], lands, lax.stop_gradient(resid))
        lands = split_wait(f"forward{s}_wait", "forward", second, second[-1])
        full, carriers = [], {}
        for key, land, (n, l, col) in zip(keys, lands, projs, strict=True):
            full.append(assemble_weight(f"weight{s}_{key}", land, w[n][l].astype(BF16), me1, col))
            carriers[key + "32"] = jnp.zeros(land.shape, F32)
            carriers[key + "16"] = jnp.zeros(land.shape, BF16)
        sm = {n: small[n] for n in sm_keys}
        run = functools.partial(fn, layer, S)
        if branch is None:
            (resid, branch), pull = jax.vjp(lambda r, c, p, run=run, full=full: run(r, None, full[0], full[1], c, p), resid, carriers, sm)
        else:
            (resid, branch), pull = jax.vjp(lambda r, b, c, p, run=run, full=full: run(r, b, full[0], full[1], c, p), resid, branch, carriers, sm)
        pulls.append(pull)

    loss_acc, dxf, dgf = loss_head(resid, branch, small["final_norm_g"].reshape(1, D), loss_target.reshape(T, D))
    loss = lax.psum(loss_acc[0, 0], ("x", "y", "c"))
    g_small = {"final_norm_g": dgf.reshape(D)}
    d_resid, d_branch, token = dxf, dxf, None
    scatter, own32 = [None] * len(stages), [None] * len(stages)
    for s in reversed(range(len(stages))):
        if token is not None:
            d_resid = lax.dynamic_update_slice(d_resid, d_resid[:SUBLANES, :LANES] + token, (0, 0))
        if s == 0:
            d_resid, dcar, dsm = pulls[s]((d_resid, d_branch))
        else:
            d_resid, d_branch, dcar, dsm = pulls[s]((d_resid, d_branch))
        for n, g in dsm.items():
            g_small[n] = g_small[n] + g if n in g_small else g
        keys = stages[s][2]
        own32[s] = [dcar[k + "32"] for k in keys]
        scatter[s] = split_start(f"grads{s}_start", "scatter", [dcar[k + "16"] for k in keys], None, d_resid)
        token = scatter[s][-1]
    dx = d_resid

    out, chain = {}, {}
    for s in range(len(stages)):
        lands = split_wait(f"grads{s}_wait", "scatter", scatter[s], dx)
        for own, land, (n, l, _) in zip(own32[s], lands, stages[s][3], strict=True):
            chain[n] = adamw_sharded(f"adamw_{n}_{l}", l, _flat2d(w[n]), _flat2d(m[n]), _flat2d(v[n]), own, land, me1, chain.get(n))
    for n in BIG:
        out[n] = [t.reshape(w[n].shape) for t in chain[n]]

    sm_all = sm_names + list(REPLICATED)
    (g_all,) = all_gather("gather_small_grads", [_pack([g_small[n] for n in sm_all])])
    g_sum = _unpack(ordered_sum("sum_small_grads", g_all), [g_small[n].shape for n in sm_all])
    g_loc = []
    for n, g in zip(sm_all, g_sum):
        if n in SMALL_SHARDED:
            ax = SMALL_SHARDED[n]
            size = w[n].shape[ax]
            g = lax.dynamic_slice_in_dim(g, me * size, size, axis=ax)
        g_loc.append(g)
    shapes = [w[n].shape for n in sm_all]
    res = adamw("adamw_small", _pack([w[n] for n in sm_all]), _pack([m[n] for n in sm_all]), _pack([v[n] for n in sm_all]), _pack(g_loc))
    res = [g_loc] + [_unpack(r, shapes) for r in res]
    for k, n in enumerate(sm_all):
        out[n] = [r[k] for r in res]

    grads, deltas, new_m, new_v = ([out[n][k] for n in WEIGHTS] for k in range(4))
    return (loss, dx.reshape(Bl, S, D), *grads, *deltas, *new_m, *new_v)
```

```python
import functools
import math

import numpy as np
import jax
import jax.numpy as jnp
from jax import lax
from jax.experimental import pallas as pl
from jax.experimental.pallas import tpu as pltpu

F32 = jnp.float32
BF16 = jnp.bfloat16
HIGHEST = lax.Precision.HIGHEST
MESH = pl.DeviceIdType.MESH

D_MODEL = 1024
DEPTH = 4
N_EVEN = 2
N_ODD = 2
GLA_HEADS, GLA_DK, GLA_DV, GLA_RANK, GLA_GATE_NORM = 4, 64, 128, 16, 16.0
HGRN_HEADS, HGRN_DK, HGRN_DV, HGRN_MIN_F = 4, 64, 128, 1e-20
RET_HEADS, RET_DK, RET_DV = 4, 128, 192
ROPE_BASE = 10000.0
S5_WIDTH, S5_GROUP_CH, S5_GROUPS, S5_STATE = 256, 16, 16, 64
S5_N = S5_GROUPS * S5_STATE
FFN_DIM = 2816
EPS = 1e-6
EVEN_IN = 3360
ODD_IN = 2816
ADAM_LR, ADAM_B1, ADAM_B2, ADAM_EPS, ADAM_WD, ADAM_STEP = 0.001, 0.9, 0.999, 1e-08, 0.01, 10

N_DEV = 8
VMEM_LIMIT_BYTES = 56 * 1024 * 1024
ROW_TILE = 256
SCAN_CHUNK = 64
S5_SEGMENTS = 8
LANES = 128


def _params(*sem):
    return pltpu.CompilerParams(dimension_semantics=sem, vmem_limit_bytes=VMEM_LIMIT_BYTES)


def _divisor_tile(n, cap):
    best = None
    for t in range(LANES, min(n, cap) + 1, LANES):
        if n % t == 0:
            best = t
    return best if best is not None else n


def _mm_nn(name, x, w):
    M, K = x.shape
    N = w.shape[1]
    tn = _divisor_tile(N, 2048) if K * N * 2 > 8 * 2**20 else N
    tm = 256 if tn * 4 * 512 > 6 * 2**20 else 512
    assert M % tm == 0 and N % tn == 0

    def body(x_ref, w_ref, o_ref):
        o_ref[...] = jnp.dot(x_ref[...].astype(BF16), w_ref[...], preferred_element_type=F32)

    return pl.pallas_call(
        body, name=name, grid=(N // tn, M // tm),
        in_specs=[pl.BlockSpec((tm, K), lambda j, i: (i, 0)), pl.BlockSpec((K, tn), lambda j, i: (0, j))],
        out_specs=pl.BlockSpec((tm, tn), lambda j, i: (i, j)),
        out_shape=jax.ShapeDtypeStruct((M, N), F32),
        compiler_params=_params("parallel", "parallel"),
    )(x, w)


def _mm_nt(name, dy, w):
    M, N = dy.shape
    K = w.shape[0]
    tk = _divisor_tile(K, 1024) if K * N * 2 > 8 * 2**20 else K
    tm = 256 if N >= 4096 else 512
    assert M % tm == 0 and K % tk == 0

    def body(dy_ref, w_ref, o_ref):
        o_ref[...] = lax.dot_general(dy_ref[...].astype(BF16), w_ref[...], (((1,), (1,)), ((), ())),
                                     preferred_element_type=F32)

    return pl.pallas_call(
        body, name=name, grid=(K // tk, M // tm),
        in_specs=[pl.BlockSpec((tm, N), lambda j, i: (i, 0)), pl.BlockSpec((tk, N), lambda j, i: (j, 0))],
        out_specs=pl.BlockSpec((tm, tk), lambda j, i: (i, j)),
        out_shape=jax.ShapeDtypeStruct((M, K), F32),
        compiler_params=_params("parallel", "parallel"),
    )(dy, w)


MM_TN_VMEM_BUDGET = 40 * 2**20


def _pad_lanes(n):
    return -(-n // LANES) * LANES


def _mm_tn(name, x, dy, nblk, want16):
    M, K = x.shape
    N = dy.shape[1]
    n = N // nblk
    xb, yb = x.dtype.itemsize, dy.dtype.itemsize
    best = None
    for tk in [t for t in range(LANES, K + 1, LANES) if K % t == 0]:
        for tm in (512, 256):
            out_bytes = nblk * tk * _pad_lanes(n) * (6 if want16 else 4)
            vmem = 2 * out_bytes + 2 * tm * tk * xb + 2 * tm * _pad_lanes(N) * yb
            traffic = (K // tk) * M * N * yb + M * K * xb
            if vmem <= MM_TN_VMEM_BUDGET and M % tm == 0 and (best is None or (traffic, -tm) < best[0]):
                best = ((traffic, -tm), tk, tm)
    _, tk, tm = best
    last = M // tm - 1

    def body(x_ref, dy_ref, o32_ref, *o16_ref):
        m = pl.program_id(1)

        @pl.when(m == 0)
        def _():
            o32_ref[...] = jnp.zeros_like(o32_ref)

        dyv = dy_ref[...].astype(BF16)
        if nblk == 1:
            o32_ref[0] += lax.dot_general(x_ref[...].astype(BF16), dyv, (((0,), (0,)), ((), ())), preferred_element_type=F32)
        else:
            xt = x_ref[...].astype(F32).T.astype(BF16)
            for d in range(nblk):
                o32_ref[d] += jnp.dot(xt, dyv[:, d * n:(d + 1) * n], preferred_element_type=F32)
        if want16:
            @pl.when(m == last)
            def _():
                o16_ref[0][...] = o32_ref[...].astype(BF16)

    blk = pl.BlockSpec((nblk, tk, n), lambda a, m: (0, a, 0))
    return pl.pallas_call(
        body, name=name, grid=(K // tk, M // tm),
        in_specs=[pl.BlockSpec((tm, tk), lambda a, m: (m, a)), pl.BlockSpec((tm, N), lambda a, m: (m, 0))],
        out_specs=[blk, blk] if want16 else [blk],
        out_shape=[jax.ShapeDtypeStruct((nblk, K, n), F32)] + ([jax.ShapeDtypeStruct((nblk, K, n), BF16)] if want16 else []),
        compiler_params=_params("parallel", "arbitrary"),
    )(x, dy)


def _weight_grads(name, x, dy, col_sharded):
    if col_sharded:
        return _mm_tn(name, x, dy, N_DEV, True)
    d32, d16 = _mm_tn(name, x, dy, 1, True)
    K, N = d32.shape[1:]
    return d32.reshape(N_DEV, K // N_DEV, N), d16.reshape(N_DEV, K // N_DEV, N)


def make_mm(name, col_sharded=False):
    @jax.custom_vjp
    def mm(x, w16, c32, c16):
        return _mm_nn(name + "_fwd", x, w16)

    def fwd(x, w16, c32, c16):
        return _mm_nn(name + "_fwd", x, w16), (x, w16)

    def bwd(res, dy):
        x, w16 = res
        d32, d16 = _weight_grads(name + "_dw", x, dy, col_sharded)
        return _mm_nt(name + "_dx", dy, w16), jnp.zeros_like(w16), d32, d16

    mm.defvjp(fwd, bwd)
    return mm


def make_mm_f32w(name):
    @jax.custom_vjp
    def mm(x, w):
        return _mm_nn(name + "_fwd", x, w.astype(BF16))

    def fwd(x, w):
        w16 = w.astype(BF16)
        return _mm_nn(name + "_fwd", x, w16), (x, w16)

    def bwd(res, dy):
        x, w16 = res
        return _mm_nt(name + "_dx", dy, w16), _mm_tn(name + "_dw", x, dy, 1, False)[0][0]

    mm.defvjp(fwd, bwd)
    return mm


def _row_specs(rows, params, consts, tile):
    specs = [pl.BlockSpec((tile, r.shape[1]), lambda i: (i, 0)) for r in rows]
    specs += [pl.BlockSpec(p.shape, lambda i: (0, 0)) for p in params]
    specs += [pl.BlockSpec((tile, c.shape[1]), lambda i, n=c.shape[0] // tile: (i % n, 0)) for c in consts]
    return specs


def _row_fwd(name, f, out_widths, tile, rows, params, consts):
    T = rows[0].shape[0]
    nr, npar, ncon = len(rows), len(params), len(consts)

    def body(*refs):
        r = tuple(x[...] for x in refs[:nr])
        p = tuple(x[...] for x in refs[nr:nr + npar])
        c = tuple(x[...] for x in refs[nr + npar:nr + npar + ncon])
        outs = f(r, p, c)
        for o_ref, o in zip(refs[nr + npar + ncon:], outs, strict=True):
            o_ref[...] = o

    return pl.pallas_call(
        body, name=name + "_fwd", grid=(T // tile,),
        in_specs=_row_specs(rows, params, consts, tile),
        out_specs=[pl.BlockSpec((tile, w), lambda i: (i, 0)) for w in out_widths],
        out_shape=[jax.ShapeDtypeStruct((T, w), F32) for w in out_widths],
        compiler_params=_params("parallel"),
    )(*rows, *params, *consts)


def _row_bwd(name, f, out_widths, tile, rows, params, consts, gouts, dr_dtype=F32):
    T = rows[0].shape[0]
    nr, npar, ncon, nout = len(rows), len(params), len(consts), len(out_widths)

    def body(*refs):
        r = tuple(x[...] for x in refs[:nr])
        p = tuple(x[...] for x in refs[nr:nr + npar])
        c = tuple(x[...] for x in refs[nr + npar:nr + npar + ncon])
        k = nr + npar + ncon
        g = tuple(x[...] for x in refs[k:k + nout])
        dr_refs = refs[k + nout:k + nout + nr]
        dp_refs = refs[k + nout + nr:]
        _, vjp = jax.vjp(lambda r_, p_: tuple(f(r_, p_, c)), r, p)
        dr, dp = vjp(g)
        for ref, val in zip(dr_refs, dr, strict=True):
            ref[...] = val.astype(ref.dtype)
        if npar:
            @pl.when(pl.program_id(0) == 0)
            def _():
                for ref in dp_refs:
                    ref[...] = jnp.zeros_like(ref)

            for ref, val in zip(dp_refs, dp, strict=True):
                ref[...] += val

    outs = pl.pallas_call(
        body, name=name + "_bwd", grid=(T // tile,),
        in_specs=_row_specs(rows, params, consts, tile) + [pl.BlockSpec((tile, w), lambda i: (i, 0)) for w in out_widths],
        out_specs=[pl.BlockSpec((tile, r.shape[1]), lambda i: (i, 0)) for r in rows]
        + [pl.BlockSpec(p.shape, lambda i: (0, 0)) for p in params],
        out_shape=[jax.ShapeDtypeStruct(r.shape, dr_dtype) for r in rows] + [jax.ShapeDtypeStruct(p.shape, F32) for p in params],
        compiler_params=_params("arbitrary"),
    )(*rows, *params, *consts, *gouts)
    return tuple(outs[:nr]), tuple(outs[nr:])


def make_proj_stage(name, f, out_widths, tile=ROW_TILE):
    def run(x, w16, params, consts):
        p = _mm_nn(name + "_mm", x, w16)
        return p, tuple(_row_fwd(name, f, out_widths, tile, (p,), params, consts))

    @jax.custom_vjp
    def op(x, w16, c32, c16, params, consts):
        return run(x, w16, params, consts)[1]

    def fwd(x, w16, c32, c16, params, consts):
        p, outs = run(x, w16, params, consts)
        return outs, (x, w16, p, params, consts)

    def bwd(res, g):
        x, w16, p, params, consts = res
        (dp,), dparams = _row_bwd(name, f, out_widths, tile, (p,), params, consts, tuple(g), dr_dtype=BF16)
        d32, d16 = _weight_grads(name + "_dw", x, dp, True)
        return _mm_nt(name + "_dx", dp, w16), jnp.zeros_like(w16), d32, d16, dparams, tuple(jnp.zeros_like(c) for c in consts)

    op.defvjp(fwd, bwd)
    return op


def make_rowop(name, f, out_widths, tile=ROW_TILE):
    @jax.custom_vjp
    def op(rows, params, consts):
        return tuple(_row_fwd(name, f, out_widths, tile, rows, params, consts))

    def fwd(rows, params, consts):
        return op(rows, params, consts), (rows, params, consts)

    def bwd(res, g):
        rows, params, consts = res
        dr, dp = _row_bwd(name, f, out_widths, tile, rows, params, consts, tuple(g))
        return dr, dp, tuple(jnp.zeros_like(c) for c in consts)

    op.defvjp(fwd, bwd)
    return op


def _rms(x, g):
    return x * lax.rsqrt(jnp.mean(x * x, axis=-1, keepdims=True) + EPS) * g


def _silu(x):
    return x * jax.nn.sigmoid(x)


def _bdot(a, b):
    return jnp.dot(a.astype(BF16), b.astype(BF16), preferred_element_type=F32)


def norm_f(rows, params, consts):
    return (_rms(rows[0], params[0]),)


def addnorm_f(rows, params, consts):
    x = rows[0] + rows[1]
    return x, _rms(x, params[0])


EVEN_GLA_END = 1568


def even_prep_f(rows, params, consts):
    (p,) = rows
    wa2f, wa2b, baf, bab, lbf, lbb = params
    gq = p[:, 0:256]
    gk = p[:, 256:512] * (GLA_DK ** -0.5)
    gv = p[:, 512:1024]
    gr = p[:, 1024:1536]
    glaf = jax.nn.log_sigmoid(_bdot(p[:, 1536:1552], wa2f) + baf) / GLA_GATE_NORM
    glab = jax.nn.log_sigmoid(_bdot(p[:, 1552:1568], wa2b) + bab) / GLA_GATE_NORM
    o = EVEN_GLA_END
    hq = _silu(p[:, o:o + 256])

    def gate(z, lb):
        f = lb + (1.0 - lb) * jax.nn.sigmoid(z)
        return jnp.log(jnp.maximum(f, HGRN_MIN_F)), (1.0 - lb) * jax.nn.sigmoid(-z)

    hlaf, hkf = gate(p[:, o + 256:o + 512], lbf)
    hlab, hkb = gate(p[:, o + 512:o + 768], lbb)
    hv = p[:, o + 768:o + 1280]
    hg = p[:, o + 1280:o + 1792]
    return gq, gk, gv, glaf, glab, gr, hq, hkf, hkb, hv, hlaf, hlab, hg


EVEN_PREP_WIDTHS = (256, 256, 512, 256, 256, 512, 256, 256, 256, 512, 256, 256, 512)


def _head_rms(o, g, heads, d):
    parts = []
    for h in range(heads):
        seg = o[:, h * d:(h + 1) * d]
        parts.append(seg * lax.rsqrt(jnp.mean(seg * seg, axis=-1, keepdims=True) + EPS))
    return jnp.concatenate(parts, axis=1) * g


def even_post_f(rows, params, consts):
    of, ob, hof, hob, gr, hg = rows
    a = _head_rms(of + ob, params[0], GLA_HEADS, GLA_DV) * _silu(gr)
    b = _head_rms(hof + hob, params[1], HGRN_HEADS, HGRN_DV) * _silu(hg)
    return (jnp.concatenate([a, b], axis=1),)


@jax.custom_vjp
def _roll_half(x):
    return pltpu.roll(x, RET_DK // 2, 1)


_roll_half.defvjp(lambda x: (_roll_half(x), None), lambda _, g: (_roll_half(g),))


def odd_prep_f(rows, params, consts):
    (p,) = rows
    cosf, sinf = consts

    def rot(t):
        parts = []
        for h in range(RET_HEADS):
            th = t[:, h * RET_DK:(h + 1) * RET_DK]
            parts.append(th * cosf + _roll_half(th) * sinf)
        return jnp.concatenate(parts, axis=1)

    rq = rot(p[:, 0:512])
    rk = rot(p[:, 512:1024]) * (RET_DK ** -0.5)
    return rq, rk, p[:, 1024:1792], p[:, 1792:2560], p[:, 2560:2816]


ODD_PREP_WIDTHS = (512, 512, 768, 768, 256)


def ret_post_f(rows, params, consts):
    of, ob, rg = rows
    o = of + ob
    parts = []
    for h in range(RET_HEADS):
        seg = o[:, h * RET_DV:(h + 1) * RET_DV]
        c = seg - jnp.mean(seg, axis=-1, keepdims=True)
        parts.append(c * lax.rsqrt(jnp.mean(c * c, axis=-1, keepdims=True) + EPS))
    return (jnp.concatenate(parts, axis=1) * params[0] * _silu(rg),)


def s5_post_f(rows, params, consts):
    h0r, h0i, h1r, h1i, u = rows
    c_re, c_im, d_skip, glu_w, glu_b = params
    hr = h0r + h1r
    hi = h0i + h1i
    y = _bdot(hr, c_re) - _bdot(hi, c_im) + d_skip * u
    g = jax.nn.gelu(y)
    return (g * jax.nn.sigmoid(_bdot(g, glu_w) + glu_b),)


def loss_head(x, r, g, target, tile=ROW_TILE):
    T, D = x.shape

    def body(x_ref, r_ref, g_ref, t_ref, loss_ref, dx_ref, dg_ref):
        t = t_ref[...]

        def lf(xv, gv):
            e = _rms(xv, gv) - t
            row = jnp.sum(e * e, axis=-1, keepdims=True) * (0.5 / D)
            return jnp.sum(row, axis=0, keepdims=True)

        l, vjp = jax.vjp(lf, x_ref[...] + r_ref[...], g_ref[...])
        dx, dg = vjp(jnp.ones((1, 1), F32))
        dx_ref[...] = dx

        @pl.when(pl.program_id(0) == 0)
        def _():
            loss_ref[...] = jnp.zeros_like(loss_ref)
            dg_ref[...] = jnp.zeros_like(dg_ref)

        loss_ref[...] += jnp.broadcast_to(l, loss_ref.shape)
        dg_ref[...] += dg

    row = pl.BlockSpec((tile, D), lambda i: (i, 0))
    vec = pl.BlockSpec((1, D), lambda i: (0, 0))
    return pl.pallas_call(
        body, name="loss_head", grid=(T // tile,),
        in_specs=[row, row, vec, row],
        out_specs=[pl.BlockSpec((1, LANES), lambda i: (0, 0)), row, vec],
        out_shape=[jax.ShapeDtypeStruct((1, LANES), F32), jax.ShapeDtypeStruct((T, D), F32), jax.ShapeDtypeStruct((1, D), F32)],
        compiler_params=_params("arbitrary"),
    )(x, r, g, target)


SUBLANES = 8


def _halo_specs(width, tile, T):
    n8 = tile // SUBLANES
    last = T // SUBLANES - 1
    return [pl.BlockSpec((tile, width), lambda i: (i, 0)),
            pl.BlockSpec((SUBLANES, width), lambda i: (jnp.maximum(i * n8 - 1, 0), 0)),
            pl.BlockSpec((SUBLANES, width), lambda i: (jnp.minimum((i + 1) * n8, last), 0))]


def _shift_rows(x, prev_row, next_row, tile):
    row = lax.broadcasted_iota(jnp.int32, (tile, 1), 0)
    down = jnp.where(row == 0, prev_row, pltpu.roll(x, 1, 0))
    up = jnp.where(row == tile - 1, next_row, pltpu.roll(x, tile - 1, 0))
    return down, up


def _conv_fwd(name, u, cw, cb, S, tile):
    T, F2 = u.shape
    F = F2 // 2
    per_seq = S // tile

    def body(u_ref, up_ref, un_ref, cw_ref, cb_ref, g_ref):
        pos = pl.program_id(0) % per_seq
        uv = u_ref[...]
        prev_row = jnp.where(pos == 0, 0.0, up_ref[SUBLANES - 1:SUBLANES, :])
        next_row = jnp.where(pos == per_seq - 1, 0.0, un_ref[0:1, :])
        down, up = _shift_rows(uv, prev_row, next_row, tile)
        c = _conv_taps(down, uv, up, cw_ref, cb_ref)
        g_ref[...] = (_silu(c[:, :F]) * c[:, F:]).astype(BF16)

    return pl.pallas_call(
        body, name=name + "_fwd", grid=(T // tile,),
        in_specs=_halo_specs(F2, tile, T) + [pl.BlockSpec((3, F2), lambda i: (0, 0)), pl.BlockSpec((1, F2), lambda i: (0, 0))],
        out_specs=pl.BlockSpec((tile, F), lambda i: (i, 0)),
        out_shape=jax.ShapeDtypeStruct((T, F), BF16),
        compiler_params=_params("parallel"),
    )(u, u, u, cw, cb)


def _conv_taps(down, mid, up, cw_ref, cb_ref):
    c = cb_ref[...] + down * cw_ref[0:1, :]
    c = c + mid * cw_ref[1:2, :]
    return c + up * cw_ref[2:3, :]


def _conv_bwd(name, u, dg, cw, cb, S, tile):
    T, F2 = u.shape
    F = F2 // 2
    per_seq = S // tile

    def dact(cv, dgv):
        a, v = cv[:, :F], cv[:, F:]
        sg = jax.nn.sigmoid(a)
        return jnp.concatenate([dgv * v * (sg * (1.0 + a * (1.0 - sg))), dgv * (a * sg)], axis=1)

    def body(u_ref, up_ref, un_ref, g_ref, gp_ref, gn_ref, cw_ref, cb_ref, du_ref, dw0_ref, dw1_ref, dw2_ref, db_ref):
        i = pl.program_id(0)
        pos = i % per_seq
        first, last = pos == 0, pos == per_seq - 1
        lo, hi = slice(SUBLANES - 1, SUBLANES), slice(0, 1)
        uv = u_ref[...]
        u_m1, u_p1 = jnp.where(first, 0.0, up_ref[lo, :]), jnp.where(last, 0.0, un_ref[hi, :])
        u_dn, u_up = _shift_rows(uv, u_m1, u_p1, tile)
        dc = dact(_conv_taps(u_dn, uv, u_up, cw_ref, cb_ref), g_ref[...])
        c_m1 = _conv_taps(up_ref[SUBLANES - 2:SUBLANES - 1, :], u_m1, u_ref[0:1, :], cw_ref, cb_ref)
        c_p1 = _conv_taps(u_ref[tile - 1:tile, :], u_p1, un_ref[1:2, :], cw_ref, cb_ref)
        dc_prev = jnp.where(first, 0.0, dact(c_m1, gp_ref[lo, :]))
        dc_next = jnp.where(last, 0.0, dact(c_p1, gn_ref[hi, :]))
        dc_dn, dc_up = _shift_rows(dc, dc_prev, dc_next, tile)
        du = dc_up * cw_ref[0:1, :]
        du = du + dc * cw_ref[1:2, :]
        du_ref[...] = (du + dc_dn * cw_ref[2:3, :]).astype(BF16)

        @pl.when(i == 0)
        def _():
            for ref in (dw0_ref, dw1_ref, dw2_ref, db_ref):
                ref[...] = jnp.zeros_like(ref)

        dw0_ref[...] += jnp.sum(dc * u_dn, axis=0, keepdims=True)
        dw1_ref[...] += jnp.sum(dc * uv, axis=0, keepdims=True)
        dw2_ref[...] += jnp.sum(dc * u_up, axis=0, keepdims=True)
        db_ref[...] += jnp.sum(dc, axis=0, keepdims=True)

    vec = pl.BlockSpec((1, F2), lambda i: (0, 0))
    du, dw0, dw1, dw2, db = pl.pallas_call(
        body, name=name + "_bwd", grid=(T // tile,),
        in_specs=_halo_specs(F2, tile, T) + _halo_specs(F, tile, T) + [pl.BlockSpec((3, F2), lambda i: (0, 0)), vec],
        out_specs=[pl.BlockSpec((tile, F2), lambda i: (i, 0)), vec, vec, vec, vec],
        out_shape=[jax.ShapeDtypeStruct((T, F2), BF16)] + [jax.ShapeDtypeStruct((1, F2), F32)] * 4,
        compiler_params=_params("arbitrary"),
    )(u, u, u, dg, dg, dg, cw, cb)
    return du, jnp.concatenate([dw0, dw1, dw2], axis=0), db


def make_ffn(name, S):
    def run(x, wup16, wdn16, cw, cb):
        u = _mm_nn(name + "_up", x, wup16)
        g16 = _conv_fwd(name + "_conv", u, cw, cb, S, ROW_TILE)
        return u, g16, _mm_nn(name + "_down", g16, wdn16)

    @jax.custom_vjp
    def op(x, wup16, cu32, cu16, wdn16, cd32, cd16, cw, cb):
        return run(x, wup16, wdn16, cw, cb)[2]

    def fwd(x, wup16, cu32, cu16, wdn16, cd32, cd16, cw, cb):
        u, g16, out = run(x, wup16, wdn16, cw, cb)
        return out, (x, wup16, wdn16, u, g16, cw, cb)

    def bwd(res, dout):
        x, wup16, wdn16, u, g16, cw, cb = res
        dg = _mm_nt(name + "_down_dx", dout, wdn16)
        dd32, dd16 = _weight_grads(name + "_down_dw", g16, dout, False)
        du16, dcw, dcb = _conv_bwd(name + "_conv", u, dg, cw, cb, S, ROW_TILE)
        du32, du16w = _weight_grads(name + "_up_dw", x, du16, True)
        return (_mm_nt(name + "_up_dx", du16, wup16), jnp.zeros_like(wup16), du32, du16w, jnp.zeros_like(wdn16), dd32, dd16, dcw, dcb)

    op.defvjp(fwd, bwd)
    return op


def _dot_nt(a, b):
    return lax.dot_general(a.astype(BF16), b.astype(BF16), (((1,), (1,)), ((), ())), preferred_element_type=F32)


def _dot_tn(a, b):
    return lax.dot_general(a.astype(BF16), b.astype(BF16), (((0,), (0,)), ((), ())), preferred_element_type=F32)


def _chunk_decays(la, direction, C, width, dk, const_lg):
    row = lax.broadcasted_iota(jnp.int32, (C, C), 0)
    col = lax.broadcasted_iota(jnp.int32, (C, C), 1)
    keep = (row >= col) if direction == 0 else (row <= col)
    ridx = lax.broadcasted_iota(jnp.int32, (C, 1), 0)
    if const_lg is None:
        cum = jnp.dot(keep.astype(F32), la, precision=HIGHEST, preferred_element_type=F32)
    else:
        lane_head = lax.broadcasted_iota(jnp.int32, (1, width), 1) // dk
        lg = jnp.zeros((1, width), F32)
        for h, val in enumerate(const_lg):
            lg = jnp.where(lane_head == h, val, lg)
        steps = (ridx + 1) if direction == 0 else (C - ridx)
        cum = steps.astype(F32) * lg
    exit_row = C - 1 if direction == 0 else 0
    mid = jnp.sum(jnp.where(ridx == C // 2, cum, 0.0), axis=0, keepdims=True)
    last = jnp.sum(jnp.where(ridx == exit_row, cum, 0.0), axis=0, keepdims=True)
    return keep, ridx == exit_row, cum, mid, last


def _scan_fwd(name, q, kf, kb, v, laf, lab, H, dk, dv, S, C, const_lg):
    T = q.shape[0]
    B, nc = T // S, S // C
    Wk, Wv = H * dk, H * dv
    learn = const_lg is None

    def body(*refs):
        @pl.when(pl.program_id(0) == 0)
        def _():
            refs[-1][...] = jnp.zeros_like(refs[-1])

        for b in range(B):
            one_sequence(*[r.at[b] for r in refs])

    def one_sequence(*refs):
        if learn:
            qf_r, qb_r, kf_r, kb_r, vf_r, vb_r, laf_r, lab_r, of_r, ob_r, sf_r, sb_r, st = refs
            las = (laf_r[...], lab_r[...])
        else:
            qf_r, qb_r, kf_r, kb_r, vf_r, vb_r, of_r, ob_r, sf_r, sb_r, st = refs
            las = (None, None)

        for d, (q_r, k_r, v_r, o_r, s_r) in enumerate(((qf_r, kf_r, vf_r, of_r, sf_r), (qb_r, kb_r, vb_r, ob_r, sb_r))):
            keep, _, cum, mid, last = _chunk_decays(las[d], d, C, Wk, dk, None if learn else const_lg[d])
            qe = q_r[...] * jnp.exp(cum - mid)
            ke = k_r[...] * jnp.exp(mid - cum)
            q_in = qe * jnp.exp(mid)
            k_out = ke * jnp.exp(last - mid)
            e_last = jnp.exp(last)
            vv = v_r[...]
            for h in range(H):
                ks, vs = slice(h * dk, (h + 1) * dk), slice(h * dv, (h + 1) * dv)
                a = jnp.where(keep, _dot_nt(qe[:, ks], ke[:, ks]), 0.0)
                state = st[d, h]
                o_r[:, vs] = _bdot(a, vv[:, vs]) + _dot_nt(q_in[:, ks], state)
                s_r[h * dv:(h + 1) * dv, :] = state
                st[d, h] = state * e_last[:, ks] + _dot_tn(vv[:, vs], k_out[:, ks])

    fpos = lambda c: c
    bpos = lambda c: nc - 1 - c
    kspec = lambda pos: pl.BlockSpec((B, C, Wk), lambda c: (0, pos(c), 0))
    vspec = lambda pos: pl.BlockSpec((B, C, Wv), lambda c: (0, pos(c), 0))
    sspec = lambda pos: pl.BlockSpec((B, None, Wv, dk), lambda c: (0, pos(c), 0, 0))
    seq = lambda t: t.reshape(B, S, t.shape[1])
    ins = [seq(t) for t in [q, q, kf, kb, v, v] + ([laf, lab] if learn else [])]
    in_specs = [kspec(fpos), kspec(bpos), kspec(fpos), kspec(bpos), vspec(fpos), vspec(bpos)] + ([kspec(fpos), kspec(bpos)] if learn else [])
    of, ob, sf, sb = pl.pallas_call(
        body, name=name + "_fwd", grid=(nc,), in_specs=in_specs,
        out_specs=[vspec(fpos), vspec(bpos), sspec(fpos), sspec(bpos)],
        out_shape=[jax.ShapeDtypeStruct((B, S, Wv), F32)] * 2 + [jax.ShapeDtypeStruct((B, nc, Wv, dk), F32)] * 2,
        scratch_shapes=[pltpu.VMEM((B, 2, H, dv, dk), F32)],
        compiler_params=_params("arbitrary"),
    )(*ins)
    return of.reshape(T, Wv), ob.reshape(T, Wv), sf, sb


def _scan_bwd(name, q, kf, kb, v, laf, lab, sf, sb, dof, dob, H, dk, dv, S, C, const_lg):
    T = q.shape[0]
    B, nc = T // S, S // C
    Wk, Wv = H * dk, H * dv
    learn = const_lg is None

    def body(*refs):
        @pl.when(pl.program_id(0) == 0)
        def _():
            refs[-1][...] = jnp.zeros_like(refs[-1])

        for b in range(B):
            one_sequence(*[r.at[b] for r in refs])

    def one_sequence(*refs):
        if learn:
            (qf_r, qb_r, kf_r, kb_r, vf_r, vb_r, laf_r, lab_r, sf_r, sb_r, dof_r, dob_r,
             dqf_r, dqb_r, dkf_r, dkb_r, dvf_r, dvb_r, dlaf_r, dlab_r, dst) = refs
            las, dlas = (laf_r[...], lab_r[...]), (dlaf_r, dlab_r)
        else:
            (qf_r, qb_r, kf_r, kb_r, vf_r, vb_r, sf_r, sb_r, dof_r, dob_r,
             dqf_r, dqb_r, dkf_r, dkb_r, dvf_r, dvb_r, dst) = refs
            las, dlas = (None, None), (None, None)

        groups = ((qf_r, kf_r, vf_r, sf_r, dof_r, dqf_r, dkf_r, dvf_r), (qb_r, kb_r, vb_r, sb_r, dob_r, dqb_r, dkb_r, dvb_r))
        for d, (q_r, k_r, v_r, s_r, do_r, dq_r, dk_r, dv_r) in enumerate(groups):
            keep, is_exit, cum, mid, last = _chunk_decays(las[d], d, C, Wk, dk, None if learn else const_lg[d])
            eq, ek = jnp.exp(cum - mid), jnp.exp(mid - cum)
            e_in, e_out, e_last = jnp.exp(mid), jnp.exp(last - mid), jnp.exp(last)
            qe, ke = q_r[...] * eq, k_r[...] * ek
            q_in, k_out = qe * e_in, ke * e_out
            vv, do = v_r[...], do_r[...]
            dqe_parts, dke_parts, dlast_parts = [], [], []
            for h in range(H):
                ks, vs = slice(h * dk, (h + 1) * dk), slice(h * dv, (h + 1) * dv)
                a = jnp.where(keep, _dot_nt(qe[:, ks], ke[:, ks]), 0.0)
                dp = jnp.where(keep, _dot_nt(do[:, vs], vv[:, vs]), 0.0)
                s_prev = s_r[h * dv:(h + 1) * dv, :]
                ds = dst[d, h]
                dk_out = _bdot(vv[:, vs], ds)
                dqe_parts.append(_bdot(dp, ke[:, ks]) + _bdot(do[:, vs], s_prev) * e_in[:, ks])
                dke_parts.append(_dot_tn(dp, qe[:, ks]) + dk_out * e_out[:, ks])
                dv_r[:, vs] = _dot_tn(a, do[:, vs]) + _dot_nt(k_out[:, ks], ds)
                if learn:
                    dlast_parts.append(jnp.sum(dk_out * k_out[:, ks], axis=0, keepdims=True)
                                       + jnp.sum(ds * s_prev, axis=0, keepdims=True) * e_last[:, ks])
                dst[d, h] = ds * e_last[:, ks] + _dot_tn(do[:, vs], q_in[:, ks])
            dqe = jnp.concatenate(dqe_parts, axis=1)
            dke = jnp.concatenate(dke_parts, axis=1)
            dq_r[...] = dqe * eq
            dk_r[...] = dke * ek
            if learn:
                dcum = dqe * qe - dke * ke + jnp.where(is_exit, jnp.concatenate(dlast_parts, axis=1), 0.0)
                dlas[d][...] = lax.dot_general(keep.astype(F32), dcum, (((0,), (0,)), ((), ())), precision=HIGHEST,
                                               preferred_element_type=F32)

    fpos = lambda c: nc - 1 - c
    bpos = lambda c: c
    kspec = lambda pos: pl.BlockSpec((B, C, Wk), lambda c: (0, pos(c), 0))
    vspec = lambda pos: pl.BlockSpec((B, C, Wv), lambda c: (0, pos(c), 0))
    sspec = lambda pos: pl.BlockSpec((B, None, Wv, dk), lambda c: (0, pos(c), 0, 0))
    seq = lambda t: t.reshape(B, S, t.shape[1])
    ins = [seq(t) for t in [q, q, kf, kb, v, v] + ([laf, lab] if learn else [])] + [sf, sb, seq(dof), seq(dob)]
    in_specs = ([kspec(fpos), kspec(bpos), kspec(fpos), kspec(bpos), vspec(fpos), vspec(bpos)]
                + ([kspec(fpos), kspec(bpos)] if learn else []) + [sspec(fpos), sspec(bpos), vspec(fpos), vspec(bpos)])
    out_specs = [kspec(fpos), kspec(bpos), kspec(fpos), kspec(bpos), vspec(fpos), vspec(bpos)] + ([kspec(fpos), kspec(bpos)] if learn else [])
    out_shape = ([jax.ShapeDtypeStruct((B, S, Wk), F32)] * 4 + [jax.ShapeDtypeStruct((B, S, Wv), F32)] * 2
                 + ([jax.ShapeDtypeStruct((B, S, Wk), F32)] * 2 if learn else []))
    outs = pl.pallas_call(
        body, name=name + "_bwd", grid=(nc,), in_specs=in_specs, out_specs=out_specs, out_shape=out_shape,
        scratch_shapes=[pltpu.VMEM((B, 2, H, dv, dk), F32)],
        compiler_params=_params("arbitrary"),
    )(*ins)
    return [t.reshape(T, t.shape[2]) for t in outs]


def make_scan(name, H, dk, dv, S, C=SCAN_CHUNK, const_lg=None):
    if const_lg is None:
        @jax.custom_vjp
        def op(q, kf, kb, v, laf, lab):
            return tuple(_scan_fwd(name, q, kf, kb, v, laf, lab, H, dk, dv, S, C, None)[:2])

        def fwd(q, kf, kb, v, laf, lab):
            of, ob, sf, sb = _scan_fwd(name, q, kf, kb, v, laf, lab, H, dk, dv, S, C, None)
            return (of, ob), (q, kf, kb, v, laf, lab, sf, sb)

        def bwd(res, g):
            q, kf, kb, v, laf, lab, sf, sb = res
            dqf, dqb, dkf, dkb, dvf, dvb, dlaf, dlab = _scan_bwd(name, q, kf, kb, v, laf, lab, sf, sb, g[0], g[1], H, dk, dv, S, C, None)
            return dqf + dqb, dkf, dkb, dvf + dvb, dlaf, dlab
    else:
        @jax.custom_vjp
        def op(q, kf, kb, v):
            return tuple(_scan_fwd(name, q, kf, kb, v, None, None, H, dk, dv, S, C, const_lg)[:2])

        def fwd(q, kf, kb, v):
            of, ob, sf, sb = _scan_fwd(name, q, kf, kb, v, None, None, H, dk, dv, S, C, const_lg)
            return (of, ob), (q, kf, kb, v, sf, sb)

        def bwd(res, g):
            q, kf, kb, v, sf, sb = res
            dqf, dqb, dkf, dkb, dvf, dvb = _scan_bwd(name, q, kf, kb, v, None, None, sf, sb, g[0], g[1], H, dk, dv, S, C, const_lg)
            return dqf + dqb, dkf, dkb, dvf + dvb

    op.defvjp(fwd, bwd)
    return op


def _reorder_call(name, t, S, to_segments):
    T, w = t.shape
    n_it = S // S5_SEGMENTS

    def body(x_ref, o_ref):
        def step(i, carry):
            packed = pl.ds(pl.multiple_of(i * S5_SEGMENTS, S5_SEGMENTS), S5_SEGMENTS)
            spread = pl.ds(i, S5_SEGMENTS, stride=n_it)
            if to_segments:
                o_ref[packed, :] = x_ref[spread, :]
            else:
                o_ref[spread, :] = x_ref[packed, :]
            return carry

        lax.fori_loop(0, n_it, step, 0, unroll=8)

    blk = pl.BlockSpec((S, LANES), lambda b, j: (b, j))
    return pl.pallas_call(body, name=name, grid=(T // S, w // LANES), in_specs=[blk], out_specs=blk,
                          out_shape=jax.ShapeDtypeStruct(t.shape, t.dtype), compiler_params=_params("parallel", "parallel"))(t)


def make_reorder(name, S, to_segments):
    @jax.custom_vjp
    def op(t):
        return _reorder_call(name, t, S, to_segments)

    op.defvjp(lambda t: (_reorder_call(name, t, S, to_segments), None),
              lambda _, g: (_reorder_call(name + "_bwd", g, S, not to_segments),))
    return op


def _s5_scan_call(name, Xs, A, S, dirs):
    T, N = Xs[0].shape
    B, nl = T // S, N // LANES
    n_it = S // S5_SEGMENTS
    assert n_it & (n_it - 1) == 0

    def cmul(ar, ai, br, bi):
        return ar * br - ai * bi, ar * bi + ai * br

    def body(x0r, x0i, x1r, x1i, a_ref, h0r, h0i, h1r, h1i):
        seg = lax.broadcasted_iota(jnp.int32, (S5_SEGMENTS, 1), 0)
        for k, (xr, xi, hr, hi) in enumerate(((x0r, x0i, h0r, h0i), (x1r, x1i, h1r, h1i))):
            back = dirs[k] == 1
            ar = jnp.broadcast_to(a_ref[2 * k:2 * k + 1, :], (S5_SEGMENTS, LANES))
            ai = jnp.broadcast_to(a_ref[2 * k + 1:2 * k + 2, :], (S5_SEGMENTS, LANES))
            rows_of = lambda i: pl.ds(pl.multiple_of(((n_it - 1 - i) if back else i) * S5_SEGMENTS, S5_SEGMENTS), S5_SEGMENTS)

            def local(i, carry):
                sr, si = carry
                rows = rows_of(i)
                pr, pi = cmul(ar, ai, sr, si)
                sr, si = pr + xr[rows, :], pi + xi[rows, :]
                hr[rows, :] = sr
                hi[rows, :] = si
                return sr, si

            zero = jnp.zeros((S5_SEGMENTS, LANES), F32)
            er, ei = lax.fori_loop(0, n_it, local, (zero, zero), unroll=8)
            pr, pi = ar, ai
            for _ in range(n_it.bit_length() - 1):
                pr, pi = cmul(pr, pi, pr, pi)
            shift = (S5_SEGMENTS - 1) if back else 1
            tr, ti = er, ei
            order = range(S5_SEGMENTS - 2, -1, -1) if back else range(1, S5_SEGMENTS)
            for r in order:
                nr, ni = cmul(pr, pi, pltpu.roll(tr, shift, 0), pltpu.roll(ti, shift, 0))
                tr = jnp.where(seg == r, er + nr, tr)
                ti = jnp.where(seg == r, ei + ni, ti)
            edge = (S5_SEGMENTS - 1) if back else 0
            cr = jnp.where(seg == edge, 0.0, pltpu.roll(tr, shift, 0))
            ci = jnp.where(seg == edge, 0.0, pltpu.roll(ti, shift, 0))

            def fix(i, carry):
                wr, wi = carry
                rows = rows_of(i)
                fr, fi = cmul(wr, wi, cr, ci)
                hr[rows, :] = hr[rows, :] + fr
                hi[rows, :] = hi[rows, :] + fi
                return cmul(wr, wi, ar, ai)

            lax.fori_loop(0, n_it, fix, (ar, ai), unroll=8)

    col = pl.BlockSpec((S, LANES), lambda b, j: (b, j))
    return tuple(pl.pallas_call(
        body, name=name, grid=(B, nl),
        in_specs=[col] * 4 + [pl.BlockSpec((4, LANES), lambda b, j: (0, j))],
        out_specs=[col] * 4,
        out_shape=[jax.ShapeDtypeStruct((T, N), F32)] * 4,
        compiler_params=_params("parallel", "parallel"),
    )(*Xs, A))


def _s5_decay_grad_call(name, lam, H, X, tile=ROW_TILE):
    T, N = X[0].shape

    def body(*refs):
        l_refs, h_refs, x_refs, p_ref = refs[0:4], refs[4:8], refs[8:12], refs[12]

        @pl.when(pl.program_id(0) == 0)
        def _():
            p_ref[...] = jnp.zeros_like(p_ref)

        for k in range(2):
            re, im = 2 * k, 2 * k + 1
            ur, ui = h_refs[re][...] - x_refs[re][...], h_refs[im][...] - x_refs[im][...]
            lr, li = l_refs[re][...], l_refs[im][...]
            p_ref[re:re + 1, :] += jnp.sum(lr * ur + li * ui, axis=0, keepdims=True)
            p_ref[im:im + 1, :] += jnp.sum(li * ur - lr * ui, axis=0, keepdims=True)

    row = pl.BlockSpec((tile, N), lambda i: (i, 0))
    return pl.pallas_call(
        body, name=name, grid=(T // tile,), in_specs=[row] * 12,
        out_specs=pl.BlockSpec((4, N), lambda i: (0, 0)), out_shape=jax.ShapeDtypeStruct((4, N), F32),
        compiler_params=_params("arbitrary"),
    )(*lam, *H, *X)


def make_s5_scan(name, S):
    @jax.custom_vjp
    def op(X, A):
        return _s5_scan_call(name + "_fwd", X, A, S, (0, 1))

    def fwd(X, A):
        H = _s5_scan_call(name + "_fwd", X, A, S, (0, 1))
        return H, (X, A, H)

    def bwd(res, G):
        X, A, H = res
        conj = A * jnp.array([[1.0], [-1.0], [1.0], [-1.0]], F32)
        lam = _s5_scan_call(name + "_bwd", tuple(G), conj, S, (1, 0))
        P = _s5_decay_grad_call(name + "_dA", lam, H, X)
        ar, ai = A[0::2], A[1::2]
        pr, pi = P[0::2], P[1::2]
        den = ar * ar + ai * ai
        dar, dai = (pr * ar - pi * ai) / den, (pr * ai + pi * ar) / den
        return lam, jnp.stack([dar[0], dai[0], dar[1], dai[1]], axis=0)

    op.defvjp(fwd, bwd)
    return op


ANY = pl.BlockSpec(memory_space=pl.ANY)


def _place():
    x, y, c = lax.axis_index("x"), lax.axis_index("y"), lax.axis_index("c")
    return x, y, c, [(1 - x, y), (x, 1 - y), (1 - x, 1 - y)]


def all_gather(name, arrs):
    n = len(arrs)

    def body(*refs):
        ins, outs = refs[:n], refs[n:2 * n]
        send, recv, lsem = refs[2 * n:]
        x, y, c, chips = _place()
        me, sibling = (x, y, c), (x, y, 1 - c)

        def copy(a, k, block, to, src=None):
            slot = outs[a].at[4 * block[0] + 2 * block[1] + block[2]]
            return pltpu.make_async_remote_copy(src_ref=slot if src is None else src, dst_ref=slot, send_sem=send.at[a, k],
                                                recv_sem=recv.at[a, k], device_id=to, device_id_type=MESH)

        mine = [pltpu.make_async_copy(ins[a], outs[a].at[4 * x + 2 * y + c], lsem.at[a]) for a in range(n)]
        first = []
        for a in range(n):
            mine[a].start()
            first.append(copy(a, 0, me, sibling, src=ins[a]))
            first += [copy(a, 1 + j, me, (*chip, c), src=ins[a]) for j, chip in enumerate(chips)]
        for cp in first:
            cp.start()
        passed = []
        for a in range(n):
            for j, chip in enumerate(chips):
                copy(a, 1 + j, (*chip, c), me).wait_recv()
                fwd = copy(a, 4 + j, (*chip, c), sibling)
                fwd.start()
                passed.append(fwd)
        for a in range(n):
            copy(a, 0, sibling, me).wait_recv()
            for j, chip in enumerate(chips):
                copy(a, 4 + j, (*chip, 1 - c), me).wait_recv()
        for cp in first + passed:
            cp.wait_send()
        for cp in mine:
            cp.wait()

    return pl.pallas_call(
        body, name=name, in_specs=[ANY] * n, out_specs=[ANY] * n,
        out_shape=[jax.ShapeDtypeStruct((N_DEV,) + a.shape, a.dtype) for a in arrs],
        scratch_shapes=[pltpu.SemaphoreType.DMA((n, 7)), pltpu.SemaphoreType.DMA((n, 7)), pltpu.SemaphoreType.DMA((n,))],
    )(*arrs)


HBM = pl.BlockSpec(memory_space=pltpu.HBM)
SEM = pl.BlockSpec(memory_space=pltpu.SEMAPHORE)
EFFECT = pltpu.SideEffectType.DATAFLOW_SIDE_EFFECTING
GATHER_PEERS = (1, 2, 4, 6)
OTHER_CHIPS = (2, 4, 6)
COPIES_PER_ARRAY = {"scatter": N_DEV - 1, "gather": len(GATHER_PEERS), "forward": len(OTHER_CHIPS)}


def _split_plan(mode, srcs, lands, send, recv):
    x, y, c = lax.axis_index("x"), lax.axis_index("y"), lax.axis_index("c")

    def dev(k):
        return (1 - x if k & 4 else x), (1 - y if k & 2 else y), (1 - c if k & 1 else c)

    def idx(d):
        return 4 * d[0] + 2 * d[1] + d[2]

    me = idx((x, y, c))
    plan = []
    for a, land in enumerate(lands):
        if mode == "scatter":
            legs = [(srcs[a].at[idx(dev(k))], land.at[me], land.at[idx(dev(k))], dev(k)) for k in range(1, N_DEV)]
        elif mode == "gather":
            legs = [(srcs[a], land.at[me], land.at[idx(dev(k))], dev(k)) for k in GATHER_PEERS]
        else:
            legs = [(land.at[idx(dev(j))], land.at[idx(dev(j))], land.at[idx(dev(j ^ 1))], dev(1)) for j in OTHER_CHIPS]
        for i, (src, dst, arrival, to) in enumerate(legs):
            sem = a * len(legs) + i
            pair = tuple(pltpu.make_async_remote_copy(src_ref=src, dst_ref=d, send_sem=send.at[sem], recv_sem=recv.at[sem],
                                                      device_id=to, device_id_type=MESH) for d in (dst, arrival))
            plan.append(pair)
    return plan


def split_start(name, mode, srcs, lands, after):
    if lands is None:
        lands = [lax.empty((N_DEV,) + (s.shape[1:] if mode == "scatter" else s.shape), s.dtype) for s in srcs]
    ns, nl = len(srcs), len(lands)
    nsem = COPIES_PER_ARRAY[mode] * nl

    def body(*refs):
        ins, lnd = refs[:ns], refs[ns:ns + nl]
        send, recv = refs[ns + nl + 1], refs[ns + nl + 2]
        token = refs[-1]
        for out, _ in _split_plan(mode, ins, lnd, send, recv):
            out.start()
        token[...] = jnp.zeros_like(token)

    arrs = list(srcs) + list(lands)
    return pl.pallas_call(
        body, name=name,
        out_shape=(pltpu.SemaphoreType.DMA((nsem,)), pltpu.SemaphoreType.DMA((nsem,)))
        + tuple(pltpu.HBM(t.shape, t.dtype) for t in arrs) + (jax.ShapeDtypeStruct((SUBLANES, LANES), F32),),
        in_specs=[HBM] * len(arrs) + [ANY],
        out_specs=(SEM, SEM) + (HBM,) * len(arrs) + (pl.BlockSpec(memory_space=pltpu.VMEM),),
        input_output_aliases={i: 2 + i for i in range(len(arrs))},
        compiler_params=pltpu.CompilerParams(has_side_effects=EFFECT),
    )(*[pltpu.with_memory_space_constraint(t, pltpu.HBM) for t in arrs], after)


def split_wait(name, mode, handle, after):
    send, recv = handle[0], handle[1]
    arrs = list(handle[2:-1])
    nl = len(arrs) if mode == "forward" else len(arrs) // 2
    ns = len(arrs) - nl

    def body(*refs):
        ins, lnd = refs[:ns], refs[ns:ns + nl]
        s, r = refs[ns + nl], refs[ns + nl + 1]
        for out, arrival in _split_plan(mode, ins, lnd, s, r):
            out.wait_send()
            arrival.wait_recv()

    outs = pl.pallas_call(
        body, name=name,
        out_shape=tuple(pltpu.HBM(t.shape, t.dtype) for t in arrs),
        in_specs=[HBM] * len(arrs) + [SEM, SEM, ANY], out_specs=(HBM,) * len(arrs),
        input_output_aliases={i: i for i in range(len(arrs))},
        compiler_params=pltpu.CompilerParams(has_side_effects=EFFECT),
    )(*arrs, send, recv, after)
    return list(outs[ns:])


def _row_tile(rows, cols):
    cap = max(SUBLANES, (2**18 // cols) // SUBLANES * SUBLANES)
    if rows <= cap:
        return rows
    for t in range(cap, SUBLANES - 1, -SUBLANES):
        if rows % t == 0:
            return t
    return rows


def ordered_sum(name, parts):
    n, R, C = parts.shape
    rt = _row_tile(R, C)

    def body(p_ref, o_ref):
        s = p_ref[0]
        for k in range(1, n):
            s = s + p_ref[k]
        o_ref[...] = s

    return pl.pallas_call(
        body, name=name, grid=(R // rt,),
        in_specs=[pl.BlockSpec((n, rt, C), lambda r: (0, r, 0))], out_specs=pl.BlockSpec((rt, C), lambda r: (r, 0)),
        out_shape=jax.ShapeDtypeStruct((R, C), F32), compiler_params=_params("parallel"),
    )(parts)


def _adamw_update(w, m, v, g):
    bias1 = 1.0 - ADAM_B1 ** ADAM_STEP
    bias2 = 1.0 - ADAM_B2 ** ADAM_STEP
    m_new = ADAM_B1 * m + (1.0 - ADAM_B1) * g
    v_new = ADAM_B2 * v + (1.0 - ADAM_B2) * (g * g)
    delta = -ADAM_LR * ((m_new / bias1) / (jnp.sqrt(v_new / bias2) + ADAM_EPS) + ADAM_WD * w)
    return delta, m_new, v_new


def adamw(name, w, m, v, g):
    R, C = w.shape
    rt = _row_tile(R, C)

    def body(w_ref, m_ref, v_ref, g_ref, d_ref, mo_ref, vo_ref):
        d_ref[...], mo_ref[...], vo_ref[...] = _adamw_update(w_ref[...], m_ref[...], v_ref[...], g_ref[...])

    row = pl.BlockSpec((rt, C), lambda r: (r, 0))
    return pl.pallas_call(
        body, name=name, grid=(R // rt,), in_specs=[row] * 4, out_specs=[row] * 3,
        out_shape=[jax.ShapeDtypeStruct((R, C), F32)] * 3, compiler_params=_params("parallel"),
    )(w, m, v, g)


def adamw_sharded(name, layer, w, m, v, own, land, me, prev):
    _, R, C = own.shape
    rt = _row_tile(R, C)
    base = layer * (R // rt)

    def body(me_ref, w_ref, m_ref, v_ref, own_ref, land_ref, *rest):
        go_ref, d_ref, mo_ref, vo_ref = rest[-4:]
        g = own_ref[...]
        for k in range(N_DEV):
            g = g + jnp.where(me_ref[0] == k, 0.0, land_ref[k].astype(F32))
        go_ref[...] = g
        d_ref[...], mo_ref[...], vo_ref[...] = _adamw_update(w_ref[...], m_ref[...], v_ref[...], g)

    row = pl.BlockSpec((rt, C), lambda r, p: (base + r, 0))
    in_specs = [row, row, row, pl.BlockSpec((None, rt, C), lambda r, p: (p[0], r, 0)), pl.BlockSpec((N_DEV, rt, C), lambda r, p: (0, r, 0))]
    ins = [me, w, m, v, own, land]
    aliases = {}
    if prev is not None:
        in_specs += [ANY] * 4
        aliases = {len(ins) + k: k for k in range(4)}
        ins += list(prev)
    return pl.pallas_call(
        body, name=name,
        grid_spec=pltpu.PrefetchScalarGridSpec(num_scalar_prefetch=1, grid=(R // rt,), in_specs=in_specs, out_specs=[row] * 4),
        out_shape=[jax.ShapeDtypeStruct(w.shape, F32)] * 4, input_output_aliases=aliases,
        compiler_params=_params("arbitrary"),
    )(*ins)


def _hgrn_lower_bounds(lb_logits):
    p = jax.nn.softmax(lb_logits, axis=1)
    return jnp.cumsum(p, axis=1) - p[:, :1]


def _s5_discretise(lam_re, lam_im, log_dt, b_re, b_im):
    lr = jnp.minimum(lam_re, -1e-4)
    li = lam_im
    dt = jnp.exp(log_dt)[:, None]
    mag = jnp.exp(lr * dt)
    ar, ai = mag * jnp.cos(li * dt), mag * jnp.sin(li * dt)
    den = lr * lr + li * li
    nr = ar - 1.0
    cr = (nr * lr + ai * li) / den
    ci = (ai * lr - nr * li) / den
    bbr = cr[..., None] * b_re - ci[..., None] * b_im
    bbi = cr[..., None] * b_im + ci[..., None] * b_re
    return ar, ai, bbr, bbi


def _block_diag(t):
    G, a, b = t.shape
    eye = jnp.eye(G, dtype=F32)
    return (t[:, :, None, :] * eye[:, None, :, None]).reshape(G * a, G * b)


def _rope_tables(S):
    half = RET_DK // 2
    inv = ROPE_BASE ** (-jnp.arange(half, dtype=F32) / half)
    ang = jnp.arange(S, dtype=F32)[:, None] * inv[None, :]
    cos, sin = jnp.cos(ang), jnp.sin(ang)
    return jnp.concatenate([cos, cos], axis=1), jnp.concatenate([-sin, sin], axis=1)


def _ret_log_decays():
    f = tuple(float(np.log1p(-np.exp2(np.float32(-5.0 - h)))) for h in range(RET_HEADS))
    b = tuple(float(np.log1p(-np.exp2(np.float32(-5.5 - h)))) for h in range(RET_HEADS))
    return f, b


def assemble_weight(name, land, own, me, col_sharded):
    _, R, C = land.shape
    if col_sharded:
        tr = min(R, 256)

        def body(me_ref, land_ref, own_ref, o_ref):
            for d in range(N_DEV):
                o_ref[:, d * C:(d + 1) * C] = jnp.where(me_ref[0] == d, own_ref[...], land_ref[d])

        grid, out_shape = (R // tr,), (R, N_DEV * C)
        in_specs = [pl.BlockSpec((N_DEV, tr, C), lambda i, p: (0, i, 0)), pl.BlockSpec((tr, C), lambda i, p: (i, 0))]
        out_spec = pl.BlockSpec((tr, N_DEV * C), lambda i, p: (i, 0))
    else:
        def body(me_ref, land_ref, own_ref, o_ref):
            o_ref[...] = jnp.where(me_ref[0] == pl.program_id(0), own_ref[...], land_ref[...])

        grid, out_shape = (N_DEV,), (N_DEV * R, C)
        in_specs = [pl.BlockSpec((None, R, C), lambda d, p: (d, 0, 0)), pl.BlockSpec((R, C), lambda d, p: (0, 0))]
        out_spec = pl.BlockSpec((R, C), lambda d, p: (d, 0))
    return pl.pallas_call(
        body, name=name,
        grid_spec=pltpu.PrefetchScalarGridSpec(num_scalar_prefetch=1, grid=grid, in_specs=in_specs, out_specs=out_spec),
        out_shape=jax.ShapeDtypeStruct(out_shape, land.dtype), compiler_params=_params("parallel"),
    )(me, land, own)


def _row(t):
    return t.reshape(1, -1)


def mixer_stage(layer, S, resid, branch, w_in, w_out, carriers, small):
    j = layer // 2
    tag = f"l{layer}"
    g = _row(small["mix_norm_g"][layer])
    if branch is None:
        x = resid
        (h,) = make_rowop(tag + "_norm", norm_f, (D_MODEL,))((x,), (g,), ())
    else:
        x, h = make_rowop(tag + "_addnorm", addnorm_f, (D_MODEL, D_MODEL))((resid, branch), (g,), ())
    if layer % 2 == 0:
        lbs = _hgrn_lower_bounds(small["hgrn_lb_logits"])
        prm = (small["gla_wa2"][j, 0], small["gla_wa2"][j, 1], _row(small["gla_ba"][j, 0]), _row(small["gla_ba"][j, 1]),
               _row(lbs[0, j]), _row(lbs[1, j]))
        gq, gk, gv, glaf, glab, gr, hq, hkf, hkb, hv, hlaf, hlab, hg = make_proj_stage(tag + "_prep", even_prep_f, EVEN_PREP_WIDTHS)(
            h, w_in, carriers["in32"], carriers["in16"], prm, ())
        of, ob = make_scan(tag + "_gla", GLA_HEADS, GLA_DK, GLA_DV, S, 2 * SCAN_CHUNK)(gq, gk, gk, gv, glaf, glab)
        hof, hob = make_scan(tag + "_hgrn", HGRN_HEADS, HGRN_DK, HGRN_DV, S, SCAN_CHUNK)(hq, hkf, hkb, hv, hlaf, hlab)
        (y,) = make_rowop(tag + "_post", even_post_f, (D_MODEL,))(
            (of, ob, hof, hob, gr, hg), (_row(small["gla_norm_g"][j]), _row(small["hgrn_norm_g"][j])), ())
    else:
        cosf, sinf = _rope_tables(S)
        rq, rk, rv, rg, su = make_proj_stage(tag + "_prep", odd_prep_f, ODD_PREP_WIDTHS)(
            h, w_in, carriers["in32"], carriers["in16"], (), (cosf, sinf))
        of, ob = make_scan(tag + "_ret", RET_HEADS, RET_DK, RET_DV, S, 2 * SCAN_CHUNK, const_lg=_ret_log_decays())(rq, rk, rk, rv)
        (cm,) = make_rowop(tag + "_retpost", ret_post_f, (RET_HEADS * RET_DV,))((of, ob, rg), (_row(small["ret_norm_g"][j]),), ())
        disc = [_s5_discretise(small["s5_lam_re"][j, d], small["s5_lam_im"][j, d], small["s5_log_dt"][j, d],
                               small["s5_b_re"][j], small["s5_b_im"][j]) for d in range(2)]
        a4 = jnp.stack([t.reshape(-1) for d in range(2) for t in disc[d][:2]], axis=0)
        su_p = make_reorder(tag + "_s5seg", S, True)(su)
        Xs = tuple(make_mm_f32w(f"{tag}_s5in{2 * d + i}")(su_p, _block_diag(jnp.swapaxes(disc[d][2 + i], 1, 2)))
                   for d in range(2) for i in range(2))
        Hs = make_s5_scan(tag + "_s5scan", S)(Xs, a4)
        prm = (_block_diag(jnp.swapaxes(small["s5_c_re"][j], 1, 2)), _block_diag(jnp.swapaxes(small["s5_c_im"][j], 1, 2)),
               _row(small["s5_d"][j]), small["s5_glu_w"][j], _row(small["s5_glu_b"][j]))
        (dm_p,) = make_rowop(tag + "_s5post", s5_post_f, (S5_WIDTH,))((*Hs, su_p), prm, ())
        y = jnp.concatenate([cm, make_reorder(tag + "_s5time", S, False)(dm_p)], axis=1)
    return x, make_mm(tag + "_out")(y, w_out, carriers["out32"], carriers["out16"])


def ffn_stage(layer, S, resid, branch, w_up, w_down, carriers, small):
    tag = f"l{layer}"
    x, hf = make_rowop(tag + "_ffnnorm", addnorm_f, (D_MODEL, D_MODEL))((resid, branch), (_row(small["ffn_norm_g"][layer]),), ())
    out = make_ffn(tag + "_ffn", S)(hf, w_up, carriers["up32"], carriers["up16"], w_down, carriers["down32"], carriers["down16"],
                                   small["ffn_conv_w"][layer], _row(small["ffn_conv_b"][layer]))
    return x, out


BIG = {"w_in_even": 2, "w_out_even": 1, "w_in_odd": 2, "w_out_odd": 1, "ffn_w_up": 2, "ffn_w_down": 1}
SMALL_SHARDED = {"gla_wa2": 3, "gla_ba": 2, "hgrn_lb_logits": 2, "ret_norm_g": 1, "s5_d": 1, "s5_glu_w": 1, "s5_glu_b": 1,
                 "ffn_conv_w": 2}
REPLICATED = ("mix_norm_g", "ffn_norm_g", "final_norm_g", "gla_norm_g", "hgrn_norm_g", "s5_lam_re", "s5_lam_im", "s5_log_dt",
              "s5_b_re", "s5_b_im", "s5_c_re", "s5_c_im", "ffn_conv_b")
WEIGHTS = ("mix_norm_g", "ffn_norm_g", "final_norm_g", "w_in_even", "w_out_even", "gla_wa2", "gla_ba", "gla_norm_g",
           "hgrn_lb_logits", "hgrn_norm_g", "w_in_odd", "w_out_odd", "ret_norm_g", "s5_lam_re", "s5_lam_im", "s5_log_dt",
           "s5_b_re", "s5_b_im", "s5_c_re", "s5_c_im", "s5_d", "s5_glu_w", "s5_glu_b", "ffn_w_up", "ffn_conv_w", "ffn_conv_b",
           "ffn_w_down")
PACK_COLS = 512
MIXER_SMALL = (("mix_norm_g", "hgrn_lb_logits", "gla_wa2", "gla_ba", "gla_norm_g", "hgrn_norm_g"),
               ("mix_norm_g", "ret_norm_g", "s5_lam_re", "s5_lam_im", "s5_log_dt", "s5_b_re", "s5_b_im", "s5_c_re", "s5_c_im",
                "s5_d", "s5_glu_w", "s5_glu_b"))
FFN_SMALL = ("ffn_norm_g", "ffn_conv_w", "ffn_conv_b")


def _unshard(g, axis):
    t = jnp.moveaxis(g, 0, axis)
    return t.reshape(t.shape[:axis] + (t.shape[axis] * t.shape[axis + 1],) + t.shape[axis + 2:])


def _to_blocks(full, axis):
    t = full.reshape(full.shape[:axis] + (N_DEV, full.shape[axis] // N_DEV) + full.shape[axis + 1:])
    return jnp.moveaxis(t, axis, 0)


def _pack(arrs):
    flat = jnp.concatenate([a.reshape(-1) for a in arrs])
    pad = (-flat.shape[0]) % (PACK_COLS * SUBLANES)
    return jnp.pad(flat, (0, pad)).reshape(-1, PACK_COLS)


def _unpack(packed, shapes):
    flat = packed.reshape(-1)
    out, o = [], 0
    for s in shapes:
        n = int(np.prod(s))
        out.append(flat[o:o + n].reshape(s))
        o += n
    return out


def _flat2d(a):
    return a.reshape(-1, a.shape[-1])


def kernel(x, mix_norm_g, ffn_norm_g, final_norm_g, w_in_even, w_out_even, gla_wa2, gla_ba, gla_norm_g, hgrn_lb_logits, hgrn_norm_g, w_in_odd, w_out_odd, ret_norm_g, s5_lam_re, s5_lam_im, s5_log_dt, s5_b_re, s5_b_im, s5_c_re, s5_c_im, s5_d, s5_glu_w, s5_glu_b, ffn_w_up, ffn_conv_w, ffn_conv_b, ffn_w_down, loss_target, m_mix_norm_g, m_ffn_norm_g, m_final_norm_g, m_w_in_even, m_w_out_even, m_gla_wa2, m_gla_ba, m_gla_norm_g, m_hgrn_lb_logits, m_hgrn_norm_g, m_w_in_odd, m_w_out_odd, m_ret_norm_g, m_s5_lam_re, m_s5_lam_im, m_s5_log_dt, m_s5_b_re, m_s5_b_im, m_s5_c_re, m_s5_c_im, m_s5_d, m_s5_glu_w, m_s5_glu_b, m_ffn_w_up, m_ffn_conv_w, m_ffn_conv_b, m_ffn_w_down, v_mix_norm_g, v_ffn_norm_g, v_final_norm_g, v_w_in_even, v_w_out_even, v_gla_wa2, v_gla_ba, v_gla_norm_g, v_hgrn_lb_logits, v_hgrn_norm_g, v_w_in_odd, v_w_out_odd, v_ret_norm_g, v_s5_lam_re, v_s5_lam_im, v_s5_log_dt, v_s5_b_re, v_s5_b_im, v_s5_c_re, v_s5_c_im, v_s5_d, v_s5_glu_w, v_s5_glu_b, v_ffn_w_up, v_ffn_conv_w, v_ffn_conv_b, v_ffn_w_down):
    args = locals()
    w = {n: args[n] for n in WEIGHTS}
    m = {n: args["m_" + n] for n in WEIGHTS}
    v = {n: args["v_" + n] for n in WEIGHTS}
    Bl, S, D = x.shape
    T = Bl * S
    ix, iy, ic = lax.axis_index("x"), lax.axis_index("y"), lax.axis_index("c")
    me = 4 * ix + 2 * iy + ic

    xt = x.reshape(T, D)
    me1 = me.reshape(1).astype(jnp.int32)
    stages = []
    for layer in range(DEPTH):
        j = layer // 2
        kin, kout = ("w_in_even", "w_out_even") if layer % 2 == 0 else ("w_in_odd", "w_out_odd")
        stages.append((mixer_stage, layer, ("in", "out"), ((kin, j, True), (kout, j, False)), MIXER_SMALL[layer % 2]))
        stages.append((ffn_stage, layer, ("up", "down"), (("ffn_w_up", layer, True), ("ffn_w_down", layer, False)), FFN_SMALL))

    gather, after = [], xt
    for s, (_, _, _, projs, _) in enumerate(stages):
        handle = split_start(f"gather{s}_start", "gather", [w[n][l].astype(BF16) for n, l, _ in projs], None, after)
        gather.append(handle)
        after = handle[-1]
    sm_names = list(SMALL_SHARDED)
    (sm_all8,) = all_gather("gather_small", [_pack([w[n] for n in sm_names])])
    sm_flat, small, o = sm_all8.reshape(N_DEV, -1), {}, 0
    for n in sm_names:
        size = int(np.prod(w[n].shape))
        small[n] = _unshard(sm_flat[:, o:o + size].reshape((N_DEV,) + w[n].shape), SMALL_SHARDED[n])
        o += size
    small.update({n: w[n] for n in REPLICATED})
    small["mix_norm_g"] = small["mix_norm_g"] + after[0, 0]

    resid, branch, pulls = xt, None, []
    for s, (fn, layer, keys, projs, sm_keys) in enumerate(stages):
        lands = split_wait(f"gather{s}_wait", "gather", gather[s], lax.stop_gradient(resid))
        second = split_start(f"forward{s}_start", "forward", [], lands, lax.stop_gradient(resid))
        lands = split_wait(f"forward{s}_wait", "forward", second, second[-1])
        full, carriers = [], {}
        for key, land, (n, l, col) in zip(keys, lands, projs, strict=True):
            full.append(assemble_weight(f"weight{s}_{key}", land, w[n][l].astype(BF16), me1, col))
            carriers[key + "32"] = jnp.zeros(land.shape, F32)
            carriers[key + "16"] = jnp.zeros(land.shape, BF16)
        sm = {n: small[n] for n in sm_keys}
        run = functools.partial(fn, layer, S)
        if branch is None:
            (resid, branch), pull = jax.vjp(lambda r, c, p, run=run, full=full: run(r, None, full[0], full[1], c, p), resid, carriers, sm)
        else:
            (resid, branch), pull = jax.vjp(lambda r, b, c, p, run=run, full=full: run(r, b, full[0], full[1], c, p), resid, branch, carriers, sm)
        pulls.append(pull)

    loss_acc, dxf, dgf = loss_head(resid, branch, small["final_norm_g"].reshape(1, D), loss_target.reshape(T, D))
    loss = lax.psum(loss_acc[0, 0], ("x", "y", "c"))
    g_small = {"final_norm_g": dgf.reshape(D)}
    d_resid, d_branch, token = dxf, dxf, None
    scatter, own32 = [None] * len(stages), [None] * len(stages)
    for s in reversed(range(len(stages))):
        if token is not None:
            d_resid = lax.dynamic_update_slice(d_resid, d_resid[:SUBLANES, :LANES] + token, (0, 0))
        if s == 0:
            d_resid, dcar, dsm = pulls[s]((d_resid, d_branch))
        else:
            d_resid, d_branch, dcar, dsm = pulls[s]((d_resid, d_branch))
        for n, g in dsm.items():
            g_small[n] = g_small[n] + g if n in g_small else g
        keys = stages[s][2]
        own32[s] = [dcar[k + "32"] for k in keys]
        scatter[s] = split_start(f"grads{s}_start", "scatter", [dcar[k + "16"] for k in keys], None, d_resid)
        token = scatter[s][-1]
    dx = d_resid

    out = {}
    sm_all = sm_names + list(REPLICATED)
    (g_all,) = all_gather("gather_small_grads", [_pack([g_small[n] for n in sm_all])])
    g_sum = _unpack(ordered_sum("sum_small_grads", g_all), [g_small[n].shape for n in sm_all])
    g_loc = []
    for n, g in zip(sm_all, g_sum):
        if n in SMALL_SHARDED:
            ax = SMALL_SHARDED[n]
            size = w[n].shape[ax]
            g = lax.dynamic_slice_in_dim(g, me * size, size, axis=ax)
        g_loc.append(g)
    shapes = [w[n].shape for n in sm_all]
    res = adamw("adamw_small", _pack([w[n] for n in sm_all]), _pack([m[n] for n in sm_all]), _pack([v[n] for n in sm_all]), _pack(g_loc))
    last_small = res[0]
    res = [g_loc] + [_unpack(r, shapes) for r in res]
    for k, n in enumerate(sm_all):
        out[n] = [r[k] for r in res]

    chain, behind = {}, last_small
    for s in reversed(range(len(stages))):
        lands = split_wait(f"grads{s}_wait", "scatter", scatter[s], behind if s == 0 else dx)
        for own, land, (n, l, _) in zip(own32[s], lands, stages[s][3], strict=True):
            chain[n] = adamw_sharded(f"adamw_{n}_{l}", l, _flat2d(w[n]), _flat2d(m[n]), _flat2d(v[n]), own, land, me1, chain.get(n))
            behind = chain[n][0]
    for n in BIG:
        out[n] = [t.reshape(w[n].shape) for t in chain[n]]

    grads, deltas, new_m, new_v = ([out[n][k] for n in WEIGHTS] for k in range(4))
    return (loss, dx.reshape(Bl, S, D), *grads, *deltas, *new_m, *new_v)
```

```python
import functools
import math

import numpy as np
import jax
import jax.numpy as jnp
from jax import lax
from jax.experimental import pallas as pl
from jax.experimental.pallas import tpu as pltpu

F32 = jnp.float32
BF16 = jnp.bfloat16
HIGHEST = lax.Precision.HIGHEST
MESH = pl.DeviceIdType.MESH

D_MODEL = 1024
DEPTH = 4
N_EVEN = 2
N_ODD = 2
GLA_HEADS, GLA_DK, GLA_DV, GLA_RANK, GLA_GATE_NORM = 4, 64, 128, 16, 16.0
HGRN_HEADS, HGRN_DK, HGRN_DV, HGRN_MIN_F = 4, 64, 128, 1e-20
RET_HEADS, RET_DK, RET_DV = 4, 128, 192
ROPE_BASE = 10000.0
S5_WIDTH, S5_GROUP_CH, S5_GROUPS, S5_STATE = 256, 16, 16, 64
S5_N = S5_GROUPS * S5_STATE
FFN_DIM = 2816
EPS = 1e-6
EVEN_IN = 3360
ODD_IN = 2816
ADAM_LR, ADAM_B1, ADAM_B2, ADAM_EPS, ADAM_WD, ADAM_STEP = 0.001, 0.9, 0.999, 1e-08, 0.01, 10

N_DEV = 8
VMEM_LIMIT_BYTES = 56 * 1024 * 1024
ROW_TILE = 256
SCAN_CHUNK = 64
S5_SEGMENTS = 8
LANES = 128


def _params(*sem):
    return pltpu.CompilerParams(dimension_semantics=sem, vmem_limit_bytes=VMEM_LIMIT_BYTES)


def _divisor_tile(n, cap):
    best = None
    for t in range(LANES, min(n, cap) + 1, LANES):
        if n % t == 0:
            best = t
    return best if best is not None else n


def _mm_nn(name, x, w):
    M, K = x.shape
    N = w.shape[1]
    tn = _divisor_tile(N, 2048) if K * N * 2 > 8 * 2**20 else N
    tm = 256 if tn * 4 * 512 > 6 * 2**20 else 512
    assert M % tm == 0 and N % tn == 0

    def body(x_ref, w_ref, o_ref):
        o_ref[...] = jnp.dot(x_ref[...].astype(BF16), w_ref[...], preferred_element_type=F32)

    return pl.pallas_call(
        body, name=name, grid=(N // tn, M // tm),
        in_specs=[pl.BlockSpec((tm, K), lambda j, i: (i, 0)), pl.BlockSpec((K, tn), lambda j, i: (0, j))],
        out_specs=pl.BlockSpec((tm, tn), lambda j, i: (i, j)),
        out_shape=jax.ShapeDtypeStruct((M, N), F32),
        compiler_params=_params("parallel", "parallel"),
    )(x, w)


def _mm_nt(name, dy, w):
    M, N = dy.shape
    K = w.shape[0]
    tk = _divisor_tile(K, 1024) if K * N * 2 > 8 * 2**20 else K
    tm = 256 if N >= 4096 else 512
    assert M % tm == 0 and K % tk == 0

    def body(dy_ref, w_ref, o_ref):
        o_ref[...] = lax.dot_general(dy_ref[...].astype(BF16), w_ref[...], (((1,), (1,)), ((), ())),
                                     preferred_element_type=F32)

    return pl.pallas_call(
        body, name=name, grid=(K // tk, M // tm),
        in_specs=[pl.BlockSpec((tm, N), lambda j, i: (i, 0)), pl.BlockSpec((tk, N), lambda j, i: (j, 0))],
        out_specs=pl.BlockSpec((tm, tk), lambda j, i: (i, j)),
        out_shape=jax.ShapeDtypeStruct((M, K), F32),
        compiler_params=_params("parallel", "parallel"),
    )(dy, w)


MM_TN_VMEM_BUDGET = 40 * 2**20


def _pad_lanes(n):
    return -(-n // LANES) * LANES


def _mm_tn(name, x, dy, nblk, want16):
    M, K = x.shape
    N = dy.shape[1]
    n = N // nblk
    xb, yb = x.dtype.itemsize, dy.dtype.itemsize
    best = None
    for tk in [t for t in range(LANES, K + 1, LANES) if K % t == 0]:
        for tm in (512, 256):
            out_bytes = nblk * tk * _pad_lanes(n) * (6 if want16 else 4)
            vmem = 2 * out_bytes + 2 * tm * tk * xb + 2 * tm * _pad_lanes(N) * yb
            traffic = (K // tk) * M * N * yb + M * K * xb
            if vmem <= MM_TN_VMEM_BUDGET and M % tm == 0 and (best is None or (traffic, -tm) < best[0]):
                best = ((traffic, -tm), tk, tm)
    _, tk, tm = best
    last = M // tm - 1

    def body(x_ref, dy_ref, o32_ref, *o16_ref):
        m = pl.program_id(1)

        @pl.when(m == 0)
        def _():
            o32_ref[...] = jnp.zeros_like(o32_ref)

        dyv = dy_ref[...].astype(BF16)
        if nblk == 1:
            o32_ref[0] += lax.dot_general(x_ref[...].astype(BF16), dyv, (((0,), (0,)), ((), ())), preferred_element_type=F32)
        else:
            xt = x_ref[...].astype(F32).T.astype(BF16)
            for d in range(nblk):
                o32_ref[d] += jnp.dot(xt, dyv[:, d * n:(d + 1) * n], preferred_element_type=F32)
        if want16:
            @pl.when(m == last)
            def _():
                o16_ref[0][...] = o32_ref[...].astype(BF16)

    blk = pl.BlockSpec((nblk, tk, n), lambda a, m: (0, a, 0))
    return pl.pallas_call(
        body, name=name, grid=(K // tk, M // tm),
        in_specs=[pl.BlockSpec((tm, tk), lambda a, m: (m, a)), pl.BlockSpec((tm, N), lambda a, m: (m, 0))],
        out_specs=[blk, blk] if want16 else [blk],
        out_shape=[jax.ShapeDtypeStruct((nblk, K, n), F32)] + ([jax.ShapeDtypeStruct((nblk, K, n), BF16)] if want16 else []),
        compiler_params=_params("parallel", "arbitrary"),
    )(x, dy)


def _weight_grads(name, x, dy, col_sharded):
    if col_sharded:
        return _mm_tn(name, x, dy, N_DEV, True)
    d32, d16 = _mm_tn(name, x, dy, 1, True)
    K, N = d32.shape[1:]
    return d32.reshape(N_DEV, K // N_DEV, N), d16.reshape(N_DEV, K // N_DEV, N)


def make_mm(name, col_sharded=False):
    @jax.custom_vjp
    def mm(x, w16, c32, c16):
        return _mm_nn(name + "_fwd", x, w16)

    def fwd(x, w16, c32, c16):
        return _mm_nn(name + "_fwd", x, w16), (x, w16)

    def bwd(res, dy):
        x, w16 = res
        d32, d16 = _weight_grads(name + "_dw", x, dy, col_sharded)
        return _mm_nt(name + "_dx", dy, w16), jnp.zeros_like(w16), d32, d16

    mm.defvjp(fwd, bwd)
    return mm


def make_mm_f32w(name):
    @jax.custom_vjp
    def mm(x, w):
        return _mm_nn(name + "_fwd", x, w.astype(BF16))

    def fwd(x, w):
        w16 = w.astype(BF16)
        return _mm_nn(name + "_fwd", x, w16), (x, w16)

    def bwd(res, dy):
        x, w16 = res
        return _mm_nt(name + "_dx", dy, w16), _mm_tn(name + "_dw", x, dy, 1, False)[0][0]

    mm.defvjp(fwd, bwd)
    return mm


def _row_specs(rows, params, consts, tile):
    specs = [pl.BlockSpec((tile, r.shape[1]), lambda i: (i, 0)) for r in rows]
    specs += [pl.BlockSpec(p.shape, lambda i: (0, 0)) for p in params]
    specs += [pl.BlockSpec((tile, c.shape[1]), lambda i, n=c.shape[0] // tile: (i % n, 0)) for c in consts]
    return specs


def _row_fwd(name, f, out_widths, tile, rows, params, consts):
    T = rows[0].shape[0]
    nr, npar, ncon = len(rows), len(params), len(consts)

    def body(*refs):
        r = tuple(x[...] for x in refs[:nr])
        p = tuple(x[...] for x in refs[nr:nr + npar])
        c = tuple(x[...] for x in refs[nr + npar:nr + npar + ncon])
        outs = f(r, p, c)
        for o_ref, o in zip(refs[nr + npar + ncon:], outs, strict=True):
            o_ref[...] = o

    return pl.pallas_call(
        body, name=name + "_fwd", grid=(T // tile,),
        in_specs=_row_specs(rows, params, consts, tile),
        out_specs=[pl.BlockSpec((tile, w), lambda i: (i, 0)) for w in out_widths],
        out_shape=[jax.ShapeDtypeStruct((T, w), F32) for w in out_widths],
        compiler_params=_params("parallel"),
    )(*rows, *params, *consts)


def _row_bwd(name, f, out_widths, tile, rows, params, consts, gouts, dr_dtype=F32):
    T = rows[0].shape[0]
    nr, npar, ncon, nout = len(rows), len(params), len(consts), len(out_widths)

    def body(*refs):
        r = tuple(x[...] for x in refs[:nr])
        p = tuple(x[...] for x in refs[nr:nr + npar])
        c = tuple(x[...] for x in refs[nr + npar:nr + npar + ncon])
        k = nr + npar + ncon
        g = tuple(x[...] for x in refs[k:k + nout])
        dr_refs = refs[k + nout:k + nout + nr]
        dp_refs = refs[k + nout + nr:]
        _, vjp = jax.vjp(lambda r_, p_: tuple(f(r_, p_, c)), r, p)
        dr, dp = vjp(g)
        for ref, val in zip(dr_refs, dr, strict=True):
            ref[...] = val.astype(ref.dtype)
        if npar:
            @pl.when(pl.program_id(0) == 0)
            def _():
                for ref in dp_refs:
                    ref[...] = jnp.zeros_like(ref)

            for ref, val in zip(dp_refs, dp, strict=True):
                ref[...] += val

    outs = pl.pallas_call(
        body, name=name + "_bwd", grid=(T // tile,),
        in_specs=_row_specs(rows, params, consts, tile) + [pl.BlockSpec((tile, w), lambda i: (i, 0)) for w in out_widths],
        out_specs=[pl.BlockSpec((tile, r.shape[1]), lambda i: (i, 0)) for r in rows]
        + [pl.BlockSpec(p.shape, lambda i: (0, 0)) for p in params],
        out_shape=[jax.ShapeDtypeStruct(r.shape, dr_dtype) for r in rows] + [jax.ShapeDtypeStruct(p.shape, F32) for p in params],
        compiler_params=_params("arbitrary"),
    )(*rows, *params, *consts, *gouts)
    return tuple(outs[:nr]), tuple(outs[nr:])


def make_proj_stage(name, f, out_widths, tile=ROW_TILE):
    def run(x, w16, params, consts):
        p = _mm_nn(name + "_mm", x, w16)
        return p, tuple(_row_fwd(name, f, out_widths, tile, (p,), params, consts))

    @jax.custom_vjp
    def op(x, w16, c32, c16, params, consts):
        return run(x, w16, params, consts)[1]

    def fwd(x, w16, c32, c16, params, consts):
        p, outs = run(x, w16, params, consts)
        return outs, (x, w16, p, params, consts)

    def bwd(res, g):
        x, w16, p, params, consts = res
        (dp,), dparams = _row_bwd(name, f, out_widths, tile, (p,), params, consts, tuple(g), dr_dtype=BF16)
        d32, d16 = _weight_grads(name + "_dw", x, dp, True)
        return _mm_nt(name + "_dx", dp, w16), jnp.zeros_like(w16), d32, d16, dparams, tuple(jnp.zeros_like(c) for c in consts)

    op.defvjp(fwd, bwd)
    return op


def make_rowop(name, f, out_widths, tile=ROW_TILE):
    @jax.custom_vjp
    def op(rows, params, consts):
        return tuple(_row_fwd(name, f, out_widths, tile, rows, params, consts))

    def fwd(rows, params, consts):
        return op(rows, params, consts), (rows, params, consts)

    def bwd(res, g):
        rows, params, consts = res
        dr, dp = _row_bwd(name, f, out_widths, tile, rows, params, consts, tuple(g))
        return dr, dp, tuple(jnp.zeros_like(c) for c in consts)

    op.defvjp(fwd, bwd)
    return op


def _rms(x, g):
    return x * lax.rsqrt(jnp.mean(x * x, axis=-1, keepdims=True) + EPS) * g


def _silu(x):
    return x * jax.nn.sigmoid(x)


def _bdot(a, b):
    return jnp.dot(a.astype(BF16), b.astype(BF16), preferred_element_type=F32)


def norm_f(rows, params, consts):
    return (_rms(rows[0], params[0]),)


def addnorm_f(rows, params, consts):
    x = rows[0] + rows[1]
    return x, _rms(x, params[0])


EVEN_GLA_END = 1568


def even_prep_f(rows, params, consts):
    (p,) = rows
    wa2f, wa2b, baf, bab, lbf, lbb = params
    gq = p[:, 0:256]
    gk = p[:, 256:512] * (GLA_DK ** -0.5)
    gv = p[:, 512:1024]
    gr = p[:, 1024:1536]
    glaf = jax.nn.log_sigmoid(_bdot(p[:, 1536:1552], wa2f) + baf) / GLA_GATE_NORM
    glab = jax.nn.log_sigmoid(_bdot(p[:, 1552:1568], wa2b) + bab) / GLA_GATE_NORM
    o = EVEN_GLA_END
    hq = _silu(p[:, o:o + 256])

    def gate(z, lb):
        f = lb + (1.0 - lb) * jax.nn.sigmoid(z)
        return jnp.log(jnp.maximum(f, HGRN_MIN_F)), (1.0 - lb) * jax.nn.sigmoid(-z)

    hlaf, hkf = gate(p[:, o + 256:o + 512], lbf)
    hlab, hkb = gate(p[:, o + 512:o + 768], lbb)
    hv = p[:, o + 768:o + 1280]
    hg = p[:, o + 1280:o + 1792]
    return gq, gk, gv, glaf, glab, gr, hq, hkf, hkb, hv, hlaf, hlab, hg


EVEN_PREP_WIDTHS = (256, 256, 512, 256, 256, 512, 256, 256, 256, 512, 256, 256, 512)


def _head_rms(o, g, heads, d):
    parts = []
    for h in range(heads):
        seg = o[:, h * d:(h + 1) * d]
        parts.append(seg * lax.rsqrt(jnp.mean(seg * seg, axis=-1, keepdims=True) + EPS))
    return jnp.concatenate(parts, axis=1) * g


def even_post_f(rows, params, consts):
    of, ob, hof, hob, gr, hg = rows
    a = _head_rms(of + ob, params[0], GLA_HEADS, GLA_DV) * _silu(gr)
    b = _head_rms(hof + hob, params[1], HGRN_HEADS, HGRN_DV) * _silu(hg)
    return (jnp.concatenate([a, b], axis=1),)


@jax.custom_vjp
def _roll_half(x):
    return pltpu.roll(x, RET_DK // 2, 1)


_roll_half.defvjp(lambda x: (_roll_half(x), None), lambda _, g: (_roll_half(g),))


def odd_prep_f(rows, params, consts):
    (p,) = rows
    cosf, sinf = consts

    def rot(t):
        parts = []
        for h in range(RET_HEADS):
            th = t[:, h * RET_DK:(h + 1) * RET_DK]
            parts.append(th * cosf + _roll_half(th) * sinf)
        return jnp.concatenate(parts, axis=1)

    rq = rot(p[:, 0:512])
    rk = rot(p[:, 512:1024]) * (RET_DK ** -0.5)
    return rq, rk, p[:, 1024:1792], p[:, 1792:2560], p[:, 2560:2816]


ODD_PREP_WIDTHS = (512, 512, 768, 768, 256)


def ret_post_f(rows, params, consts):
    of, ob, rg = rows
    o = of + ob
    parts = []
    for h in range(RET_HEADS):
        seg = o[:, h * RET_DV:(h + 1) * RET_DV]
        c = seg - jnp.mean(seg, axis=-1, keepdims=True)
        parts.append(c * lax.rsqrt(jnp.mean(c * c, axis=-1, keepdims=True) + EPS))
    return (jnp.concatenate(parts, axis=1) * params[0] * _silu(rg),)


def s5_post_f(rows, params, consts):
    h0r, h0i, h1r, h1i, u = rows
    c_re, c_im, d_skip, glu_w, glu_b = params
    hr = h0r + h1r
    hi = h0i + h1i
    y = _bdot(hr, c_re) - _bdot(hi, c_im) + d_skip * u
    g = jax.nn.gelu(y)
    return (g * jax.nn.sigmoid(_bdot(g, glu_w) + glu_b),)


def loss_head(x, r, g, target, tile=ROW_TILE):
    T, D = x.shape

    def body(x_ref, r_ref, g_ref, t_ref, loss_ref, dx_ref, dg_ref):
        t = t_ref[...]

        def lf(xv, gv):
            e = _rms(xv, gv) - t
            row = jnp.sum(e * e, axis=-1, keepdims=True) * (0.5 / D)
            return jnp.sum(row, axis=0, keepdims=True)

        l, vjp = jax.vjp(lf, x_ref[...] + r_ref[...], g_ref[...])
        dx, dg = vjp(jnp.ones((1, 1), F32))
        dx_ref[...] = dx

        @pl.when(pl.program_id(0) == 0)
        def _():
            loss_ref[...] = jnp.zeros_like(loss_ref)
            dg_ref[...] = jnp.zeros_like(dg_ref)

        loss_ref[...] += jnp.broadcast_to(l, loss_ref.shape)
        dg_ref[...] += dg

    row = pl.BlockSpec((tile, D), lambda i: (i, 0))
    vec = pl.BlockSpec((1, D), lambda i: (0, 0))
    return pl.pallas_call(
        body, name="loss_head", grid=(T // tile,),
        in_specs=[row, row, vec, row],
        out_specs=[pl.BlockSpec((1, LANES), lambda i: (0, 0)), row, vec],
        out_shape=[jax.ShapeDtypeStruct((1, LANES), F32), jax.ShapeDtypeStruct((T, D), F32), jax.ShapeDtypeStruct((1, D), F32)],
        compiler_params=_params("arbitrary"),
    )(x, r, g, target)


SUBLANES = 8


def _halo_specs(width, tile, T):
    n8 = tile // SUBLANES
    last = T // SUBLANES - 1
    return [pl.BlockSpec((tile, width), lambda i: (i, 0)),
            pl.BlockSpec((SUBLANES, width), lambda i: (jnp.maximum(i * n8 - 1, 0), 0)),
            pl.BlockSpec((SUBLANES, width), lambda i: (jnp.minimum((i + 1) * n8, last), 0))]


def _shift_rows(x, prev_row, next_row, tile):
    row = lax.broadcasted_iota(jnp.int32, (tile, 1), 0)
    down = jnp.where(row == 0, prev_row, pltpu.roll(x, 1, 0))
    up = jnp.where(row == tile - 1, next_row, pltpu.roll(x, tile - 1, 0))
    return down, up


def _conv_fwd(name, u, cw, cb, S, tile):
    T, F2 = u.shape
    F = F2 // 2
    per_seq = S // tile

    def body(u_ref, up_ref, un_ref, cw_ref, cb_ref, g_ref):
        pos = pl.program_id(0) % per_seq
        uv = u_ref[...]
        prev_row = jnp.where(pos == 0, 0.0, up_ref[SUBLANES - 1:SUBLANES, :])
        next_row = jnp.where(pos == per_seq - 1, 0.0, un_ref[0:1, :])
        down, up = _shift_rows(uv, prev_row, next_row, tile)
        c = _conv_taps(down, uv, up, cw_ref, cb_ref)
        g_ref[...] = (_silu(c[:, :F]) * c[:, F:]).astype(BF16)

    return pl.pallas_call(
        body, name=name + "_fwd", grid=(T // tile,),
        in_specs=_halo_specs(F2, tile, T) + [pl.BlockSpec((3, F2), lambda i: (0, 0)), pl.BlockSpec((1, F2), lambda i: (0, 0))],
        out_specs=pl.BlockSpec((tile, F), lambda i: (i, 0)),
        out_shape=jax.ShapeDtypeStruct((T, F), BF16),
        compiler_params=_params("parallel"),
    )(u, u, u, cw, cb)


def _conv_taps(down, mid, up, cw_ref, cb_ref):
    c = cb_ref[...] + down * cw_ref[0:1, :]
    c = c + mid * cw_ref[1:2, :]
    return c + up * cw_ref[2:3, :]


def _conv_bwd(name, u, dg, cw, cb, S, tile):
    T, F2 = u.shape
    F = F2 // 2
    per_seq = S // tile

    def dact(cv, dgv):
        a, v = cv[:, :F], cv[:, F:]
        sg = jax.nn.sigmoid(a)
        return jnp.concatenate([dgv * v * (sg * (1.0 + a * (1.0 - sg))), dgv * (a * sg)], axis=1)

    def body(u_ref, up_ref, un_ref, g_ref, gp_ref, gn_ref, cw_ref, cb_ref, du_ref, dw0_ref, dw1_ref, dw2_ref, db_ref):
        i = pl.program_id(0)
        pos = i % per_seq
        first, last = pos == 0, pos == per_seq - 1
        lo, hi = slice(SUBLANES - 1, SUBLANES), slice(0, 1)
        uv = u_ref[...]
        u_m1, u_p1 = jnp.where(first, 0.0, up_ref[lo, :]), jnp.where(last, 0.0, un_ref[hi, :])
        u_dn, u_up = _shift_rows(uv, u_m1, u_p1, tile)
        dc = dact(_conv_taps(u_dn, uv, u_up, cw_ref, cb_ref), g_ref[...])
        c_m1 = _conv_taps(up_ref[SUBLANES - 2:SUBLANES - 1, :], u_m1, u_ref[0:1, :], cw_ref, cb_ref)
        c_p1 = _conv_taps(u_ref[tile - 1:tile, :], u_p1, un_ref[1:2, :], cw_ref, cb_ref)
        dc_prev = jnp.where(first, 0.0, dact(c_m1, gp_ref[lo, :]))
        dc_next = jnp.where(last, 0.0, dact(c_p1, gn_ref[hi, :]))
        dc_dn, dc_up = _shift_rows(dc, dc_prev, dc_next, tile)
        du = dc_up * cw_ref[0:1, :]
        du = du + dc * cw_ref[1:2, :]
        du_ref[...] = (du + dc_dn * cw_ref[2:3, :]).astype(BF16)

        @pl.when(i == 0)
        def _():
            for ref in (dw0_ref, dw1_ref, dw2_ref, db_ref):
                ref[...] = jnp.zeros_like(ref)

        dw0_ref[...] += jnp.sum(dc * u_dn, axis=0, keepdims=True)
        dw1_ref[...] += jnp.sum(dc * uv, axis=0, keepdims=True)
        dw2_ref[...] += jnp.sum(dc * u_up, axis=0, keepdims=True)
        db_ref[...] += jnp.sum(dc, axis=0, keepdims=True)

    vec = pl.BlockSpec((1, F2), lambda i: (0, 0))
    du, dw0, dw1, dw2, db = pl.pallas_call(
        body, name=name + "_bwd", grid=(T // tile,),
        in_specs=_halo_specs(F2, tile, T) + _halo_specs(F, tile, T) + [pl.BlockSpec((3, F2), lambda i: (0, 0)), vec],
        out_specs=[pl.BlockSpec((tile, F2), lambda i: (i, 0)), vec, vec, vec, vec],
        out_shape=[jax.ShapeDtypeStruct((T, F2), BF16)] + [jax.ShapeDtypeStruct((1, F2), F32)] * 4,
        compiler_params=_params("arbitrary"),
    )(u, u, u, dg, dg, dg, cw, cb)
    return du, jnp.concatenate([dw0, dw1, dw2], axis=0), db


def make_ffn(name, S):
    def run(x, wup16, wdn16, cw, cb):
        u = _mm_nn(name + "_up", x, wup16)
        g16 = _conv_fwd(name + "_conv", u, cw, cb, S, ROW_TILE)
        return u, g16, _mm_nn(name + "_down", g16, wdn16)

    @jax.custom_vjp
    def op(x, wup16, cu32, cu16, wdn16, cd32, cd16, cw, cb):
        return run(x, wup16, wdn16, cw, cb)[2]

    def fwd(x, wup16, cu32, cu16, wdn16, cd32, cd16, cw, cb):
        u, g16, out = run(x, wup16, wdn16, cw, cb)
        return out, (x, wup16, wdn16, u, g16, cw, cb)

    def bwd(res, dout):
        x, wup16, wdn16, u, g16, cw, cb = res
        dg = _mm_nt(name + "_down_dx", dout, wdn16)
        dd32, dd16 = _weight_grads(name + "_down_dw", g16, dout, False)
        du16, dcw, dcb = _conv_bwd(name + "_conv", u, dg, cw, cb, S, ROW_TILE)
        du32, du16w = _weight_grads(name + "_up_dw", x, du16, True)
        return (_mm_nt(name + "_up_dx", du16, wup16), jnp.zeros_like(wup16), du32, du16w, jnp.zeros_like(wdn16), dd32, dd16, dcw, dcb)

    op.defvjp(fwd, bwd)
    return op


def _dot_nt(a, b):
    return lax.dot_general(a.astype(BF16), b.astype(BF16), (((1,), (1,)), ((), ())), preferred_element_type=F32)


def _dot_tn(a, b):
    return lax.dot_general(a.astype(BF16), b.astype(BF16), (((0,), (0,)), ((), ())), preferred_element_type=F32)


def _chunk_decays(la, direction, C, width, dk, const_lg):
    row = lax.broadcasted_iota(jnp.int32, (C, C), 0)
    col = lax.broadcasted_iota(jnp.int32, (C, C), 1)
    keep = (row >= col) if direction == 0 else (row <= col)
    ridx = lax.broadcasted_iota(jnp.int32, (C, 1), 0)
    if const_lg is None:
        cum = jnp.dot(keep.astype(F32), la, precision=HIGHEST, preferred_element_type=F32)
    else:
        lane_head = lax.broadcasted_iota(jnp.int32, (1, width), 1) // dk
        lg = jnp.zeros((1, width), F32)
        for h, val in enumerate(const_lg):
            lg = jnp.where(lane_head == h, val, lg)
        steps = (ridx + 1) if direction == 0 else (C - ridx)
        cum = steps.astype(F32) * lg
    exit_row = C - 1 if direction == 0 else 0
    mid = jnp.sum(jnp.where(ridx == C // 2, cum, 0.0), axis=0, keepdims=True)
    last = jnp.sum(jnp.where(ridx == exit_row, cum, 0.0), axis=0, keepdims=True)
    return keep, ridx == exit_row, cum, mid, last


def _scan_fwd(name, q, kf, kb, v, laf, lab, H, dk, dv, S, C, const_lg):
    T = q.shape[0]
    B, nc = T // S, S // C
    Wk, Wv = H * dk, H * dv
    learn = const_lg is None

    def body(*refs):
        @pl.when(pl.program_id(0) == 0)
        def _():
            refs[-1][...] = jnp.zeros_like(refs[-1])

        for b in range(B):
            one_sequence(*[r.at[b] for r in refs])

    def one_sequence(*refs):
        if learn:
            qf_r, qb_r, kf_r, kb_r, vf_r, vb_r, laf_r, lab_r, of_r, ob_r, sf_r, sb_r, st = refs
            las = (laf_r[...], lab_r[...])
        else:
            qf_r, qb_r, kf_r, kb_r, vf_r, vb_r, of_r, ob_r, sf_r, sb_r, st = refs
            las = (None, None)

        for d, (q_r, k_r, v_r, o_r, s_r) in enumerate(((qf_r, kf_r, vf_r, of_r, sf_r), (qb_r, kb_r, vb_r, ob_r, sb_r))):
            keep, _, cum, mid, last = _chunk_decays(las[d], d, C, Wk, dk, None if learn else const_lg[d])
            qe = q_r[...] * jnp.exp(cum - mid)
            ke = k_r[...] * jnp.exp(mid - cum)
            q_in = qe * jnp.exp(mid)
            k_out = ke * jnp.exp(last - mid)
            e_last = jnp.exp(last)
            vv = v_r[...]
            for h in range(H):
                ks, vs = slice(h * dk, (h + 1) * dk), slice(h * dv, (h + 1) * dv)
                a = jnp.where(keep, _dot_nt(qe[:, ks], ke[:, ks]), 0.0)
                state = st[d, h]
                o_r[:, vs] = _bdot(a, vv[:, vs]) + _dot_nt(q_in[:, ks], state)
                s_r[h * dv:(h + 1) * dv, :] = state
                st[d, h] = state * e_last[:, ks] + _dot_tn(vv[:, vs], k_out[:, ks])

    fpos = lambda c: c
    bpos = lambda c: nc - 1 - c
    kspec = lambda pos: pl.BlockSpec((B, C, Wk), lambda c: (0, pos(c), 0))
    vspec = lambda pos: pl.BlockSpec((B, C, Wv), lambda c: (0, pos(c), 0))
    sspec = lambda pos: pl.BlockSpec((B, None, Wv, dk), lambda c: (0, pos(c), 0, 0))
    seq = lambda t: t.reshape(B, S, t.shape[1])
    ins = [seq(t) for t in [q, q, kf, kb, v, v] + ([laf, lab] if learn else [])]
    in_specs = [kspec(fpos), kspec(bpos), kspec(fpos), kspec(bpos), vspec(fpos), vspec(bpos)] + ([kspec(fpos), kspec(bpos)] if learn else [])
    of, ob, sf, sb = pl.pallas_call(
        body, name=name + "_fwd", grid=(nc,), in_specs=in_specs,
        out_specs=[vspec(fpos), vspec(bpos), sspec(fpos), sspec(bpos)],
        out_shape=[jax.ShapeDtypeStruct((B, S, Wv), F32)] * 2 + [jax.ShapeDtypeStruct((B, nc, Wv, dk), F32)] * 2,
        scratch_shapes=[pltpu.VMEM((B, 2, H, dv, dk), F32)],
        compiler_params=_params("arbitrary"),
    )(*ins)
    return of.reshape(T, Wv), ob.reshape(T, Wv), sf, sb


def _scan_bwd(name, q, kf, kb, v, laf, lab, sf, sb, dof, dob, H, dk, dv, S, C, const_lg):
    T = q.shape[0]
    B, nc = T // S, S // C
    Wk, Wv = H * dk, H * dv
    learn = const_lg is None

    def body(*refs):
        @pl.when(pl.program_id(0) == 0)
        def _():
            refs[-1][...] = jnp.zeros_like(refs[-1])

        for b in range(B):
            one_sequence(*[r.at[b] for r in refs])

    def one_sequence(*refs):
        if learn:
            (qf_r, qb_r, kf_r, kb_r, vf_r, vb_r, laf_r, lab_r, sf_r, sb_r, dof_r, dob_r,
             dqf_r, dqb_r, dkf_r, dkb_r, dvf_r, dvb_r, dlaf_r, dlab_r, dst) = refs
            las, dlas = (laf_r[...], lab_r[...]), (dlaf_r, dlab_r)
        else:
            (qf_r, qb_r, kf_r, kb_r, vf_r, vb_r, sf_r, sb_r, dof_r, dob_r,
             dqf_r, dqb_r, dkf_r, dkb_r, dvf_r, dvb_r, dst) = refs
            las, dlas = (None, None), (None, None)

        groups = ((qf_r, kf_r, vf_r, sf_r, dof_r, dqf_r, dkf_r, dvf_r), (qb_r, kb_r, vb_r, sb_r, dob_r, dqb_r, dkb_r, dvb_r))
        for d, (q_r, k_r, v_r, s_r, do_r, dq_r, dk_r, dv_r) in enumerate(groups):
            keep, is_exit, cum, mid, last = _chunk_decays(las[d], d, C, Wk, dk, None if learn else const_lg[d])
            eq, ek = jnp.exp(cum - mid), jnp.exp(mid - cum)
            e_in, e_out, e_last = jnp.exp(mid), jnp.exp(last - mid), jnp.exp(last)
            qe, ke = q_r[...] * eq, k_r[...] * ek
            q_in, k_out = qe * e_in, ke * e_out
            vv, do = v_r[...], do_r[...]
            dqe_parts, dke_parts, dlast_parts = [], [], []
            for h in range(H):
                ks, vs = slice(h * dk, (h + 1) * dk), slice(h * dv, (h + 1) * dv)
                a = jnp.where(keep, _dot_nt(qe[:, ks], ke[:, ks]), 0.0)
                dp = jnp.where(keep, _dot_nt(do[:, vs], vv[:, vs]), 0.0)
                s_prev = s_r[h * dv:(h + 1) * dv, :]
                ds = dst[d, h]
                dk_out = _bdot(vv[:, vs], ds)
                dqe_parts.append(_bdot(dp, ke[:, ks]) + _bdot(do[:, vs], s_prev) * e_in[:, ks])
                dke_parts.append(_dot_tn(dp, qe[:, ks]) + dk_out * e_out[:, ks])
                dv_r[:, vs] = _dot_tn(a, do[:, vs]) + _dot_nt(k_out[:, ks], ds)
                if learn:
                    dlast_parts.append(jnp.sum(dk_out * k_out[:, ks], axis=0, keepdims=True)
                                       + jnp.sum(ds * s_prev, axis=0, keepdims=True) * e_last[:, ks])
                dst[d, h] = ds * e_last[:, ks] + _dot_tn(do[:, vs], q_in[:, ks])
            dqe = jnp.concatenate(dqe_parts, axis=1)
            dke = jnp.concatenate(dke_parts, axis=1)
            dq_r[...] = dqe * eq
            dk_r[...] = dke * ek
            if learn:
                dcum = dqe * qe - dke * ke + jnp.where(is_exit, jnp.concatenate(dlast_parts, axis=1), 0.0)
                dlas[d][...] = lax.dot_general(keep.astype(F32), dcum, (((0,), (0,)), ((), ())), precision=HIGHEST,
                                               preferred_element_type=F32)

    fpos = lambda c: nc - 1 - c
    bpos = lambda c: c
    kspec = lambda pos: pl.BlockSpec((B, C, Wk), lambda c: (0, pos(c), 0))
    vspec = lambda pos: pl.BlockSpec((B, C, Wv), lambda c: (0, pos(c), 0))
    sspec = lambda pos: pl.BlockSpec((B, None, Wv, dk), lambda c: (0, pos(c), 0, 0))
    seq = lambda t: t.reshape(B, S, t.shape[1])
    ins = [seq(t) for t in [q, q, kf, kb, v, v] + ([laf, lab] if learn else [])] + [sf, sb, seq(dof), seq(dob)]
    in_specs = ([kspec(fpos), kspec(bpos), kspec(fpos), kspec(bpos), vspec(fpos), vspec(bpos)]
                + ([kspec(fpos), kspec(bpos)] if learn else []) + [sspec(fpos), sspec(bpos), vspec(fpos), vspec(bpos)])
    out_specs = [kspec(fpos), kspec(bpos), kspec(fpos), kspec(bpos), vspec(fpos), vspec(bpos)] + ([kspec(fpos), kspec(bpos)] if learn else [])
    out_shape = ([jax.ShapeDtypeStruct((B, S, Wk), F32)] * 4 + [jax.ShapeDtypeStruct((B, S, Wv), F32)] * 2
                 + ([jax.ShapeDtypeStruct((B, S, Wk), F32)] * 2 if learn else []))
    outs = pl.pallas_call(
        body, name=name + "_bwd", grid=(nc,), in_specs=in_specs, out_specs=out_specs, out_shape=out_shape,
        scratch_shapes=[pltpu.VMEM((B, 2, H, dv, dk), F32)],
        compiler_params=_params("arbitrary"),
    )(*ins)
    return [t.reshape(T, t.shape[2]) for t in outs]


def make_scan(name, H, dk, dv, S, C=SCAN_CHUNK, const_lg=None):
    if const_lg is None:
        @jax.custom_vjp
        def op(q, kf, kb, v, laf, lab):
            return tuple(_scan_fwd(name, q, kf, kb, v, laf, lab, H, dk, dv, S, C, None)[:2])

        def fwd(q, kf, kb, v, laf, lab):
            of, ob, sf, sb = _scan_fwd(name, q, kf, kb, v, laf, lab, H, dk, dv, S, C, None)
            return (of, ob), (q, kf, kb, v, laf, lab, sf, sb)

        def bwd(res, g):
            q, kf, kb, v, laf, lab, sf, sb = res
            dqf, dqb, dkf, dkb, dvf, dvb, dlaf, dlab = _scan_bwd(name, q, kf, kb, v, laf, lab, sf, sb, g[0], g[1], H, dk, dv, S, C, None)
            return dqf + dqb, dkf, dkb, dvf + dvb, dlaf, dlab
    else:
        @jax.custom_vjp
        def op(q, kf, kb, v):
            return tuple(_scan_fwd(name, q, kf, kb, v, None, None, H, dk, dv, S, C, const_lg)[:2])

        def fwd(q, kf, kb, v):
            of, ob, sf, sb = _scan_fwd(name, q, kf, kb, v, None, None, H, dk, dv, S, C, const_lg)
            return (of, ob), (q, kf, kb, v, sf, sb)

        def bwd(res, g):
            q, kf, kb, v, sf, sb = res
            dqf, dqb, dkf, dkb, dvf, dvb = _scan_bwd(name, q, kf, kb, v, None, None, sf, sb, g[0], g[1], H, dk, dv, S, C, const_lg)
            return dqf + dqb, dkf, dkb, dvf + dvb

    op.defvjp(fwd, bwd)
    return op


def _reorder_call(name, t, S, to_segments):
    T, w = t.shape
    n_it = S // S5_SEGMENTS

    def body(x_ref, o_ref):
        def step(i, carry):
            packed = pl.ds(pl.multiple_of(i * S5_SEGMENTS, S5_SEGMENTS), S5_SEGMENTS)
            spread = pl.ds(i, S5_SEGMENTS, stride=n_it)
            if to_segments:
                o_ref[packed, :] = x_ref[spread, :]
            else:
                o_ref[spread, :] = x_ref[packed, :]
            return carry

        lax.fori_loop(0, n_it, step, 0, unroll=8)

    blk = pl.BlockSpec((S, LANES), lambda b, j: (b, j))
    return pl.pallas_call(body, name=name, grid=(T // S, w // LANES), in_specs=[blk], out_specs=blk,
                          out_shape=jax.ShapeDtypeStruct(t.shape, t.dtype), compiler_params=_params("parallel", "parallel"))(t)


def make_reorder(name, S, to_segments):
    @jax.custom_vjp
    def op(t):
        return _reorder_call(name, t, S, to_segments)

    op.defvjp(lambda t: (_reorder_call(name, t, S, to_segments), None),
              lambda _, g: (_reorder_call(name + "_bwd", g, S, not to_segments),))
    return op


def _s5_scan_call(name, Xs, A, S, dirs, prev=None):
    with_p = prev is not None
    T, N = Xs[0].shape
    B, nl = T // S, N // LANES
    n_it = S // S5_SEGMENTS
    assert n_it & (n_it - 1) == 0

    def cmul(ar, ai, br, bi):
        return ar * br - ai * bi, ar * bi + ai * br

    def body(*refs):
        x, a_ref = refs[0:4], refs[4]
        if with_p:
            h_prev, x_prev, h, p_ref = refs[5:9], refs[9:13], refs[13:17], refs[17]
        else:
            h = refs[5:9]
        seg = lax.broadcasted_iota(jnp.int32, (S5_SEGMENTS, 1), 0)
        zero = jnp.zeros((S5_SEGMENTS, LANES), F32)
        a = [(jnp.broadcast_to(a_ref[2 * k:2 * k + 1, :], (S5_SEGMENTS, LANES)),
              jnp.broadcast_to(a_ref[2 * k + 1:2 * k + 2, :], (S5_SEGMENTS, LANES))) for k in range(2)]

        def rows_of(k, i):
            return pl.ds(pl.multiple_of(((n_it - 1 - i) if dirs[k] == 1 else i) * S5_SEGMENTS, S5_SEGMENTS), S5_SEGMENTS)

        def local(i, carry):
            out = []
            for k, (sr, si) in enumerate(carry):
                rows = rows_of(k, i)
                pr, pi = cmul(*a[k], sr, si)
                sr, si = pr + x[2 * k][rows, :], pi + x[2 * k + 1][rows, :]
                h[2 * k][rows, :] = sr
                h[2 * k + 1][rows, :] = si
                out.append((sr, si))
            return tuple(out)

        ends = lax.fori_loop(0, n_it, local, ((zero, zero), (zero, zero)), unroll=8)
        inherit = []
        for k, (er, ei) in enumerate(ends):
            back = dirs[k] == 1
            pr, pi = a[k]
            for _ in range(n_it.bit_length() - 1):
                pr, pi = cmul(pr, pi, pr, pi)
            shift = (S5_SEGMENTS - 1) if back else 1
            tr, ti = er, ei
            for r in (range(S5_SEGMENTS - 2, -1, -1) if back else range(1, S5_SEGMENTS)):
                nr, ni = cmul(pr, pi, pltpu.roll(tr, shift, 0), pltpu.roll(ti, shift, 0))
                tr = jnp.where(seg == r, er + nr, tr)
                ti = jnp.where(seg == r, ei + ni, ti)
            edge = (S5_SEGMENTS - 1) if back else 0
            inherit.append((jnp.where(seg == edge, 0.0, pltpu.roll(tr, shift, 0)), jnp.where(seg == edge, 0.0, pltpu.roll(ti, shift, 0))))

        def fix(i, carry):
            powers, sums = carry
            new_powers, new_sums = [], []
            for k in range(2):
                rows = rows_of(k, i)
                fr, fi = cmul(*powers[k], *inherit[k])
                sr, si = h[2 * k][rows, :] + fr, h[2 * k + 1][rows, :] + fi
                h[2 * k][rows, :] = sr
                h[2 * k + 1][rows, :] = si
                new_powers.append(cmul(*powers[k], *a[k]))
                if with_p:
                    ur = h_prev[2 * k][rows, :] - x_prev[2 * k][rows, :]
                    ui = h_prev[2 * k + 1][rows, :] - x_prev[2 * k + 1][rows, :]
                    new_sums.append((sums[k][0] + sr * ur + si * ui, sums[k][1] + si * ur - sr * ui))
            return tuple(new_powers), tuple(new_sums)

        _, sums = lax.fori_loop(0, n_it, fix, ((a[0], a[1]), ((zero, zero), (zero, zero)) if with_p else ()), unroll=8)
        if with_p:
            for k in range(2):
                p_ref[2 * k:2 * k + 1, :] = jnp.sum(sums[k][0], axis=0, keepdims=True)
                p_ref[2 * k + 1:2 * k + 2, :] = jnp.sum(sums[k][1], axis=0, keepdims=True)

    col = pl.BlockSpec((S, LANES), lambda b, j: (b, j))
    outs = pl.pallas_call(
        body, name=name, grid=(B, nl),
        in_specs=[col] * 4 + [pl.BlockSpec((4, LANES), lambda b, j: (0, j))] + ([col] * 8 if with_p else []),
        out_specs=[col] * 4 + ([pl.BlockSpec((None, 4, LANES), lambda b, j: (b, 0, j))] if with_p else []),
        out_shape=[jax.ShapeDtypeStruct((T, N), F32)] * 4 + ([jax.ShapeDtypeStruct((B, 4, N), F32)] if with_p else []),
        compiler_params=_params("parallel", "parallel"),
    )(*Xs, A, *(prev[0] + prev[1] if with_p else ()))
    return (tuple(outs[:4]), outs[4]) if with_p else tuple(outs)


def make_s5_scan(name, S):
    @jax.custom_vjp
    def op(X, A):
        return _s5_scan_call(name + "_fwd", X, A, S, (0, 1))

    def fwd(X, A):
        H = _s5_scan_call(name + "_fwd", X, A, S, (0, 1))
        return H, (X, A, H)

    def bwd(res, G):
        X, A, H = res
        conj = A * jnp.array([[1.0], [-1.0], [1.0], [-1.0]], F32)
        lam, P = _s5_scan_call(name + "_bwd", tuple(G), conj, S, (1, 0), prev=(tuple(H), tuple(X)))
        P = jnp.sum(P, axis=0)
        ar, ai = A[0::2], A[1::2]
        pr, pi = P[0::2], P[1::2]
        den = ar * ar + ai * ai
        dar, dai = (pr * ar - pi * ai) / den, (pr * ai + pi * ar) / den
        return lam, jnp.stack([dar[0], dai[0], dar[1], dai[1]], axis=0)

    op.defvjp(fwd, bwd)
    return op


ANY = pl.BlockSpec(memory_space=pl.ANY)


def _place():
    x, y, c = lax.axis_index("x"), lax.axis_index("y"), lax.axis_index("c")
    return x, y, c, [(1 - x, y), (x, 1 - y), (1 - x, 1 - y)]


def all_gather(name, arrs):
    n = len(arrs)

    def body(*refs):
        ins, outs = refs[:n], refs[n:2 * n]
        send, recv, lsem = refs[2 * n:]
        x, y, c, chips = _place()
        me, sibling = (x, y, c), (x, y, 1 - c)

        def copy(a, k, block, to, src=None):
            slot = outs[a].at[4 * block[0] + 2 * block[1] + block[2]]
            return pltpu.make_async_remote_copy(src_ref=slot if src is None else src, dst_ref=slot, send_sem=send.at[a, k],
                                                recv_sem=recv.at[a, k], device_id=to, device_id_type=MESH)

        mine = [pltpu.make_async_copy(ins[a], outs[a].at[4 * x + 2 * y + c], lsem.at[a]) for a in range(n)]
        first = []
        for a in range(n):
            mine[a].start()
            first.append(copy(a, 0, me, sibling, src=ins[a]))
            first += [copy(a, 1 + j, me, (*chip, c), src=ins[a]) for j, chip in enumerate(chips)]
        for cp in first:
            cp.start()
        passed = []
        for a in range(n):
            for j, chip in enumerate(chips):
                copy(a, 1 + j, (*chip, c), me).wait_recv()
                fwd = copy(a, 4 + j, (*chip, c), sibling)
                fwd.start()
                passed.append(fwd)
        for a in range(n):
            copy(a, 0, sibling, me).wait_recv()
            for j, chip in enumerate(chips):
                copy(a, 4 + j, (*chip, 1 - c), me).wait_recv()
        for cp in first + passed:
            cp.wait_send()
        for cp in mine:
            cp.wait()

    return pl.pallas_call(
        body, name=name, in_specs=[ANY] * n, out_specs=[ANY] * n,
        out_shape=[jax.ShapeDtypeStruct((N_DEV,) + a.shape, a.dtype) for a in arrs],
        scratch_shapes=[pltpu.SemaphoreType.DMA((n, 7)), pltpu.SemaphoreType.DMA((n, 7)), pltpu.SemaphoreType.DMA((n,))],
    )(*arrs)


HBM = pl.BlockSpec(memory_space=pltpu.HBM)
SEM = pl.BlockSpec(memory_space=pltpu.SEMAPHORE)
EFFECT = pltpu.SideEffectType.DATAFLOW_SIDE_EFFECTING
GATHER_PEERS = (1, 2, 4, 6)
OTHER_CHIPS = (2, 4, 6)
COPIES_PER_ARRAY = {"scatter": N_DEV - 1, "gather": len(GATHER_PEERS), "forward": len(OTHER_CHIPS)}


def _split_plan(mode, srcs, lands, send, recv):
    x, y, c = lax.axis_index("x"), lax.axis_index("y"), lax.axis_index("c")

    def dev(k):
        return (1 - x if k & 4 else x), (1 - y if k & 2 else y), (1 - c if k & 1 else c)

    def idx(d):
        return 4 * d[0] + 2 * d[1] + d[2]

    me = idx((x, y, c))
    plan = []
    for a, land in enumerate(lands):
        if mode == "scatter":
            legs = [(srcs[a].at[idx(dev(k))], land.at[me], land.at[idx(dev(k))], dev(k)) for k in range(1, N_DEV)]
        elif mode == "gather":
            legs = [(srcs[a], land.at[me], land.at[idx(dev(k))], dev(k)) for k in GATHER_PEERS]
        else:
            legs = [(land.at[idx(dev(j))], land.at[idx(dev(j))], land.at[idx(dev(j ^ 1))], dev(1)) for j in OTHER_CHIPS]
        for i, (src, dst, arrival, to) in enumerate(legs):
            sem = a * len(legs) + i
            pair = tuple(pltpu.make_async_remote_copy(src_ref=src, dst_ref=d, send_sem=send.at[sem], recv_sem=recv.at[sem],
                                                      device_id=to, device_id_type=MESH) for d in (dst, arrival))
            plan.append(pair)
    return plan


def split_start(name, mode, srcs, lands, after):
    if lands is None:
        lands = [lax.empty((N_DEV,) + (s.shape[1:] if mode == "scatter" else s.shape), s.dtype) for s in srcs]
    ns, nl = len(srcs), len(lands)
    nsem = COPIES_PER_ARRAY[mode] * nl

    def body(*refs):
        ins, lnd = refs[:ns], refs[ns:ns + nl]
        send, recv = refs[ns + nl + 1], refs[ns + nl + 2]
        token = refs[-1]
        for out, _ in _split_plan(mode, ins, lnd, send, recv):
            out.start()
        token[...] = jnp.zeros_like(token)

    arrs = list(srcs) + list(lands)
    return pl.pallas_call(
        body, name=name,
        out_shape=(pltpu.SemaphoreType.DMA((nsem,)), pltpu.SemaphoreType.DMA((nsem,)))
        + tuple(pltpu.HBM(t.shape, t.dtype) for t in arrs) + (jax.ShapeDtypeStruct((SUBLANES, LANES), F32),),
        in_specs=[HBM] * len(arrs) + [ANY],
        out_specs=(SEM, SEM) + (HBM,) * len(arrs) + (pl.BlockSpec(memory_space=pltpu.VMEM),),
        input_output_aliases={i: 2 + i for i in range(len(arrs))},
        compiler_params=pltpu.CompilerParams(has_side_effects=EFFECT),
    )(*[pltpu.with_memory_space_constraint(t, pltpu.HBM) for t in arrs], after)


def split_wait(name, mode, handle, after):
    send, recv = handle[0], handle[1]
    arrs = list(handle[2:-1])
    nl = len(arrs) if mode == "forward" else len(arrs) // 2
    ns = len(arrs) - nl

    def body(*refs):
        ins, lnd = refs[:ns], refs[ns:ns + nl]
        s, r = refs[ns + nl], refs[ns + nl + 1]
        for out, arrival in _split_plan(mode, ins, lnd, s, r):
            out.wait_send()
            arrival.wait_recv()

    outs = pl.pallas_call(
        body, name=name,
        out_shape=tuple(pltpu.HBM(t.shape, t.dtype) for t in arrs),
        in_specs=[HBM] * len(arrs) + [SEM, SEM, ANY], out_specs=(HBM,) * len(arrs),
        input_output_aliases={i: i for i in range(len(arrs))},
        compiler_params=pltpu.CompilerParams(has_side_effects=EFFECT),
    )(*arrs, send, recv, after)
    return list(outs[ns:])


def _row_tile(rows, cols):
    cap = max(SUBLANES, (2**18 // cols) // SUBLANES * SUBLANES)
    if rows <= cap:
        return rows
    for t in range(cap, SUBLANES - 1, -SUBLANES):
        if rows % t == 0:
            return t
    return rows


def ordered_sum(name, parts):
    n, R, C = parts.shape
    rt = _row_tile(R, C)

    def body(p_ref, o_ref):
        s = p_ref[0]
        for k in range(1, n):
            s = s + p_ref[k]
        o_ref[...] = s

    return pl.pallas_call(
        body, name=name, grid=(R // rt,),
        in_specs=[pl.BlockSpec((n, rt, C), lambda r: (0, r, 0))], out_specs=pl.BlockSpec((rt, C), lambda r: (r, 0)),
        out_shape=jax.ShapeDtypeStruct((R, C), F32), compiler_params=_params("parallel"),
    )(parts)


def _adamw_update(w, m, v, g):
    bias1 = 1.0 - ADAM_B1 ** ADAM_STEP
    bias2 = 1.0 - ADAM_B2 ** ADAM_STEP
    m_new = ADAM_B1 * m + (1.0 - ADAM_B1) * g
    v_new = ADAM_B2 * v + (1.0 - ADAM_B2) * (g * g)
    delta = -ADAM_LR * ((m_new / bias1) / (jnp.sqrt(v_new / bias2) + ADAM_EPS) + ADAM_WD * w)
    return delta, m_new, v_new


def adamw(name, w, m, v, g):
    R, C = w.shape
    rt = _row_tile(R, C)

    def body(w_ref, m_ref, v_ref, g_ref, d_ref, mo_ref, vo_ref):
        d_ref[...], mo_ref[...], vo_ref[...] = _adamw_update(w_ref[...], m_ref[...], v_ref[...], g_ref[...])

    row = pl.BlockSpec((rt, C), lambda r: (r, 0))
    return pl.pallas_call(
        body, name=name, grid=(R // rt,), in_specs=[row] * 4, out_specs=[row] * 3,
        out_shape=[jax.ShapeDtypeStruct((R, C), F32)] * 3, compiler_params=_params("parallel"),
    )(w, m, v, g)


def adamw_sharded(name, layer, w, m, v, own, land, me, prev):
    _, R, C = own.shape
    rt = _row_tile(R, C)
    base = layer * (R // rt)

    def body(me_ref, w_ref, m_ref, v_ref, own_ref, land_ref, *rest):
        go_ref, d_ref, mo_ref, vo_ref = rest[-4:]
        g = own_ref[...]
        for k in range(N_DEV):
            g = g + jnp.where(me_ref[0] == k, 0.0, land_ref[k].astype(F32))
        go_ref[...] = g
        d_ref[...], mo_ref[...], vo_ref[...] = _adamw_update(w_ref[...], m_ref[...], v_ref[...], g)

    row = pl.BlockSpec((rt, C), lambda r, p: (base + r, 0))
    in_specs = [row, row, row, pl.BlockSpec((None, rt, C), lambda r, p: (p[0], r, 0)), pl.BlockSpec((N_DEV, rt, C), lambda r, p: (0, r, 0))]
    ins = [me, w, m, v, own, land]
    aliases = {}
    if prev is not None:
        in_specs += [ANY] * 4
        aliases = {len(ins) + k: k for k in range(4)}
        ins += list(prev)
    return pl.pallas_call(
        body, name=name,
        grid_spec=pltpu.PrefetchScalarGridSpec(num_scalar_prefetch=1, grid=(R // rt,), in_specs=in_specs, out_specs=[row] * 4),
        out_shape=[jax.ShapeDtypeStruct(w.shape, F32)] * 4, input_output_aliases=aliases,
        compiler_params=_params("arbitrary"),
    )(*ins)


def _hgrn_lower_bounds(lb_logits):
    p = jax.nn.softmax(lb_logits, axis=1)
    return jnp.cumsum(p, axis=1) - p[:, :1]


def _s5_discretise(lam_re, lam_im, log_dt, b_re, b_im):
    lr = jnp.minimum(lam_re, -1e-4)
    li = lam_im
    dt = jnp.exp(log_dt)[:, None]
    mag = jnp.exp(lr * dt)
    ar, ai = mag * jnp.cos(li * dt), mag * jnp.sin(li * dt)
    den = lr * lr + li * li
    nr = ar - 1.0
    cr = (nr * lr + ai * li) / den
    ci = (ai * lr - nr * li) / den
    bbr = cr[..., None] * b_re - ci[..., None] * b_im
    bbi = cr[..., None] * b_im + ci[..., None] * b_re
    return ar, ai, bbr, bbi


def _block_diag(t):
    G, a, b = t.shape
    eye = jnp.eye(G, dtype=F32)
    return (t[:, :, None, :] * eye[:, None, :, None]).reshape(G * a, G * b)


def _rope_tables(S):
    half = RET_DK // 2
    inv = ROPE_BASE ** (-jnp.arange(half, dtype=F32) / half)
    ang = jnp.arange(S, dtype=F32)[:, None] * inv[None, :]
    cos, sin = jnp.cos(ang), jnp.sin(ang)
    return jnp.concatenate([cos, cos], axis=1), jnp.concatenate([-sin, sin], axis=1)


def _ret_log_decays():
    f = tuple(float(np.log1p(-np.exp2(np.float32(-5.0 - h)))) for h in range(RET_HEADS))
    b = tuple(float(np.log1p(-np.exp2(np.float32(-5.5 - h)))) for h in range(RET_HEADS))
    return f, b


def assemble_weight(name, land, own, me, col_sharded):
    _, R, C = land.shape
    if col_sharded:
        tr = min(R, 256)

        def body(me_ref, land_ref, own_ref, o_ref):
            for d in range(N_DEV):
                o_ref[:, d * C:(d + 1) * C] = jnp.where(me_ref[0] == d, own_ref[...], land_ref[d])

        grid, out_shape = (R // tr,), (R, N_DEV * C)
        in_specs = [pl.BlockSpec((N_DEV, tr, C), lambda i, p: (0, i, 0)), pl.BlockSpec((tr, C), lambda i, p: (i, 0))]
        out_spec = pl.BlockSpec((tr, N_DEV * C), lambda i, p: (i, 0))
    else:
        def body(me_ref, land_ref, own_ref, o_ref):
            o_ref[...] = jnp.where(me_ref[0] == pl.program_id(0), own_ref[...], land_ref[...])

        grid, out_shape = (N_DEV,), (N_DEV * R, C)
        in_specs = [pl.BlockSpec((None, R, C), lambda d, p: (d, 0, 0)), pl.BlockSpec((R, C), lambda d, p: (0, 0))]
        out_spec = pl.BlockSpec((R, C), lambda d, p: (d, 0))
    return pl.pallas_call(
        body, name=name,
        grid_spec=pltpu.PrefetchScalarGridSpec(num_scalar_prefetch=1, grid=grid, in_specs=in_specs, out_specs=out_spec),
        out_shape=jax.ShapeDtypeStruct(out_shape, land.dtype), compiler_params=_params("parallel"),
    )(me, land, own)


def _row(t):
    return t.reshape(1, -1)


def mixer_stage(layer, S, resid, branch, w_in, w_out, carriers, small):
    j = layer // 2
    tag = f"l{layer}"
    g = _row(small["mix_norm_g"][layer])
    if branch is None:
        x = resid
        (h,) = make_rowop(tag + "_norm", norm_f, (D_MODEL,))((x,), (g,), ())
    else:
        x, h = make_rowop(tag + "_addnorm", addnorm_f, (D_MODEL, D_MODEL))((resid, branch), (g,), ())
    if layer % 2 == 0:
        lbs = _hgrn_lower_bounds(small["hgrn_lb_logits"])
        prm = (small["gla_wa2"][j, 0], small["gla_wa2"][j, 1], _row(small["gla_ba"][j, 0]), _row(small["gla_ba"][j, 1]),
               _row(lbs[0, j]), _row(lbs[1, j]))
        gq, gk, gv, glaf, glab, gr, hq, hkf, hkb, hv, hlaf, hlab, hg = make_proj_stage(tag + "_prep", even_prep_f, EVEN_PREP_WIDTHS)(
            h, w_in, carriers["in32"], carriers["in16"], prm, ())
        of, ob = make_scan(tag + "_gla", GLA_HEADS, GLA_DK, GLA_DV, S, 2 * SCAN_CHUNK)(gq, gk, gk, gv, glaf, glab)
        hof, hob = make_scan(tag + "_hgrn", HGRN_HEADS, HGRN_DK, HGRN_DV, S, SCAN_CHUNK)(hq, hkf, hkb, hv, hlaf, hlab)
        (y,) = make_rowop(tag + "_post", even_post_f, (D_MODEL,))(
            (of, ob, hof, hob, gr, hg), (_row(small["gla_norm_g"][j]), _row(small["hgrn_norm_g"][j])), ())
    else:
        cosf, sinf = _rope_tables(S)
        rq, rk, rv, rg, su = make_proj_stage(tag + "_prep", odd_prep_f, ODD_PREP_WIDTHS)(
            h, w_in, carriers["in32"], carriers["in16"], (), (cosf, sinf))
        of, ob = make_scan(tag + "_ret", RET_HEADS, RET_DK, RET_DV, S, 2 * SCAN_CHUNK, const_lg=_ret_log_decays())(rq, rk, rk, rv)
        (cm,) = make_rowop(tag + "_retpost", ret_post_f, (RET_HEADS * RET_DV,))((of, ob, rg), (_row(small["ret_norm_g"][j]),), ())
        disc = [_s5_discretise(small["s5_lam_re"][j, d], small["s5_lam_im"][j, d], small["s5_log_dt"][j, d],
                               small["s5_b_re"][j], small["s5_b_im"][j]) for d in range(2)]
        a4 = jnp.stack([t.reshape(-1) for d in range(2) for t in disc[d][:2]], axis=0)
        su_p = make_reorder(tag + "_s5seg", S, True)(su)
        Xs = tuple(make_mm_f32w(f"{tag}_s5in{2 * d + i}")(su_p, _block_diag(jnp.swapaxes(disc[d][2 + i], 1, 2)))
                   for d in range(2) for i in range(2))
        Hs = make_s5_scan(tag + "_s5scan", S)(Xs, a4)
        prm = (_block_diag(jnp.swapaxes(small["s5_c_re"][j], 1, 2)), _block_diag(jnp.swapaxes(small["s5_c_im"][j], 1, 2)),
               _row(small["s5_d"][j]), small["s5_glu_w"][j], _row(small["s5_glu_b"][j]))
        (dm_p,) = make_rowop(tag + "_s5post", s5_post_f, (S5_WIDTH,))((*Hs, su_p), prm, ())
        y = jnp.concatenate([cm, make_reorder(tag + "_s5time", S, False)(dm_p)], axis=1)
    return x, make_mm(tag + "_out")(y, w_out, carriers["out32"], carriers["out16"])


def ffn_stage(layer, S, resid, branch, w_up, w_down, carriers, small):
    tag = f"l{layer}"
    x, hf = make_rowop(tag + "_ffnnorm", addnorm_f, (D_MODEL, D_MODEL))((resid, branch), (_row(small["ffn_norm_g"][layer]),), ())
    out = make_ffn(tag + "_ffn", S)(hf, w_up, carriers["up32"], carriers["up16"], w_down, carriers["down32"], carriers["down16"],
                                   small["ffn_conv_w"][layer], _row(small["ffn_conv_b"][layer]))
    return x, out


BIG = {"w_in_even": 2, "w_out_even": 1, "w_in_odd": 2, "w_out_odd": 1, "ffn_w_up": 2, "ffn_w_down": 1}
SMALL_SHARDED = {"gla_wa2": 3, "gla_ba": 2, "hgrn_lb_logits": 2, "ret_norm_g": 1, "s5_d": 1, "s5_glu_w": 1, "s5_glu_b": 1,
                 "ffn_conv_w": 2}
REPLICATED = ("mix_norm_g", "ffn_norm_g", "final_norm_g", "gla_norm_g", "hgrn_norm_g", "s5_lam_re", "s5_lam_im", "s5_log_dt",
              "s5_b_re", "s5_b_im", "s5_c_re", "s5_c_im", "ffn_conv_b")
WEIGHTS = ("mix_norm_g", "ffn_norm_g", "final_norm_g", "w_in_even", "w_out_even", "gla_wa2", "gla_ba", "gla_norm_g",
           "hgrn_lb_logits", "hgrn_norm_g", "w_in_odd", "w_out_odd", "ret_norm_g", "s5_lam_re", "s5_lam_im", "s5_log_dt",
           "s5_b_re", "s5_b_im", "s5_c_re", "s5_c_im", "s5_d", "s5_glu_w", "s5_glu_b", "ffn_w_up", "ffn_conv_w", "ffn_conv_b",
           "ffn_w_down")
PACK_COLS = 512
MIXER_SMALL = (("mix_norm_g", "hgrn_lb_logits", "gla_wa2", "gla_ba", "gla_norm_g", "hgrn_norm_g"),
               ("mix_norm_g", "ret_norm_g", "s5_lam_re", "s5_lam_im", "s5_log_dt", "s5_b_re", "s5_b_im", "s5_c_re", "s5_c_im",
                "s5_d", "s5_glu_w", "s5_glu_b"))
FFN_SMALL = ("ffn_norm_g", "ffn_conv_w", "ffn_conv_b")


def _unshard(g, axis):
    t = jnp.moveaxis(g, 0, axis)
    return t.reshape(t.shape[:axis] + (t.shape[axis] * t.shape[axis + 1],) + t.shape[axis + 2:])


def _to_blocks(full, axis):
    t = full.reshape(full.shape[:axis] + (N_DEV, full.shape[axis] // N_DEV) + full.shape[axis + 1:])
    return jnp.moveaxis(t, axis, 0)


def _pack(arrs):
    flat = jnp.concatenate([a.reshape(-1) for a in arrs])
    pad = (-flat.shape[0]) % (PACK_COLS * SUBLANES)
    return jnp.pad(flat, (0, pad)).reshape(-1, PACK_COLS)


def _unpack(packed, shapes):
    flat = packed.reshape(-1)
    out, o = [], 0
    for s in shapes:
        n = int(np.prod(s))
        out.append(flat[o:o + n].reshape(s))
        o += n
    return out


def _flat2d(a):
    return a.reshape(-1, a.shape[-1])


def kernel(x, mix_norm_g, ffn_norm_g, final_norm_g, w_in_even, w_out_even, gla_wa2, gla_ba, gla_norm_g, hgrn_lb_logits, hgrn_norm_g, w_in_odd, w_out_odd, ret_norm_g, s5_lam_re, s5_lam_im, s5_log_dt, s5_b_re, s5_b_im, s5_c_re, s5_c_im, s5_d, s5_glu_w, s5_glu_b, ffn_w_up, ffn_conv_w, ffn_conv_b, ffn_w_down, loss_target, m_mix_norm_g, m_ffn_norm_g, m_final_norm_g, m_w_in_even, m_w_out_even, m_gla_wa2, m_gla_ba, m_gla_norm_g, m_hgrn_lb_logits, m_hgrn_norm_g, m_w_in_odd, m_w_out_odd, m_ret_norm_g, m_s5_lam_re, m_s5_lam_im, m_s5_log_dt, m_s5_b_re, m_s5_b_im, m_s5_c_re, m_s5_c_im, m_s5_d, m_s5_glu_w, m_s5_glu_b, m_ffn_w_up, m_ffn_conv_w, m_ffn_conv_b, m_ffn_w_down, v_mix_norm_g, v_ffn_norm_g, v_final_norm_g, v_w_in_even, v_w_out_even, v_gla_wa2, v_gla_ba, v_gla_norm_g, v_hgrn_lb_logits, v_hgrn_norm_g, v_w_in_odd, v_w_out_odd, v_ret_norm_g, v_s5_lam_re, v_s5_lam_im, v_s5_log_dt, v_s5_b_re, v_s5_b_im, v_s5_c_re, v_s5_c_im, v_s5_d, v_s5_glu_w, v_s5_glu_b, v_ffn_w_up, v_ffn_conv_w, v_ffn_conv_b, v_ffn_w_down):
    args = locals()
    w = {n: args[n] for n in WEIGHTS}
    m = {n: args["m_" + n] for n in WEIGHTS}
    v = {n: args["v_" + n] for n in WEIGHTS}
    Bl, S, D = x.shape
    T = Bl * S
    ix, iy, ic = lax.axis_index("x"), lax.axis_index("y"), lax.axis_index("c")
    me = 4 * ix + 2 * iy + ic

    xt = x.reshape(T, D)
    me1 = me.reshape(1).astype(jnp.int32)
    stages = []
    for layer in range(DEPTH):
        j = layer // 2
        kin, kout = ("w_in_even", "w_out_even") if layer % 2 == 0 else ("w_in_odd", "w_out_odd")
        stages.append((mixer_stage, layer, ("in", "out"), ((kin, j, True), (kout, j, False)), MIXER_SMALL[layer % 2]))
        stages.append((ffn_stage, layer, ("up", "down"), (("ffn_w_up", layer, True), ("ffn_w_down", layer, False)), FFN_SMALL))

    gather, after = [], xt
    for s, (_, _, _, projs, _) in enumerate(stages):
        handle = split_start(f"gather{s}_start", "gather", [w[n][l].astype(BF16) for n, l, _ in projs], None, after)
        gather.append(handle)
        after = handle[-1]
    sm_names = list(SMALL_SHARDED)
    (sm_all8,) = all_gather("gather_small", [_pack([w[n] for n in sm_names])])
    sm_flat, small, o = sm_all8.reshape(N_DEV, -1), {}, 0
    for n in sm_names:
        size = int(np.prod(w[n].shape))
        small[n] = _unshard(sm_flat[:, o:o + size].reshape((N_DEV,) + w[n].shape), SMALL_SHARDED[n])
        o += size
    small.update({n: w[n] for n in REPLICATED})
    small["mix_norm_g"] = small["mix_norm_g"] + after[0, 0]

    resid, branch, pulls = xt, None, []
    for s, (fn, layer, keys, projs, sm_keys) in enumerate(stages):
        lands = split_wait(f"gather{s}_wait", "gather", gather[s], lax.stop_gradient(resid))
        second = split_start(f"forward{s}_start", "forward", [], lands, lax.stop_gradient(resid))
        lands = split_wait(f"forward{s}_wait", "forward", second, second[-1])
        full, carriers = [], {}
        for key, land, (n, l, col) in zip(keys, lands, projs, strict=True):
            full.append(assemble_weight(f"weight{s}_{key}", land, w[n][l].astype(BF16), me1, col))
            carriers[key + "32"] = jnp.zeros(land.shape, F32)
            carriers[key + "16"] = jnp.zeros(land.shape, BF16)
        sm = {n: small[n] for n in sm_keys}
        run = functools.partial(fn, layer, S)
        if branch is None:
            (resid, branch), pull = jax.vjp(lambda r, c, p, run=run, full=full: run(r, None, full[0], full[1], c, p), resid, carriers, sm)
        else:
            (resid, branch), pull = jax.vjp(lambda r, b, c, p, run=run, full=full: run(r, b, full[0], full[1], c, p), resid, branch, carriers, sm)
        pulls.append(pull)

    loss_acc, dxf, dgf = loss_head(resid, branch, small["final_norm_g"].reshape(1, D), loss_target.reshape(T, D))
    loss = lax.psum(loss_acc[0, 0], ("x", "y", "c"))
    g_small = {"final_norm_g": dgf.reshape(D)}
    d_resid, d_branch, token = dxf, dxf, None
    scatter, own32 = [None] * len(stages), [None] * len(stages)
    for s in reversed(range(len(stages))):
        if token is not None:
            d_resid = lax.dynamic_update_slice(d_resid, d_resid[:SUBLANES, :LANES] + token, (0, 0))
        if s == 0:
            d_resid, dcar, dsm = pulls[s]((d_resid, d_branch))
        else:
            d_resid, d_branch, dcar, dsm = pulls[s]((d_resid, d_branch))
        for n, g in dsm.items():
            g_small[n] = g_small[n] + g if n in g_small else g
        keys = stages[s][2]
        own32[s] = [dcar[k + "32"] for k in keys]
        scatter[s] = split_start(f"grads{s}_start", "scatter", [dcar[k + "16"] for k in keys], None, d_resid)
        token = scatter[s][-1]
    dx = d_resid

    out = {}
    sm_all = sm_names + list(REPLICATED)
    (g_all,) = all_gather("gather_small_grads", [_pack([g_small[n] for n in sm_all])])
    g_sum = _unpack(ordered_sum("sum_small_grads", g_all), [g_small[n].shape for n in sm_all])
    g_loc = []
    for n, g in zip(sm_all, g_sum):
        if n in SMALL_SHARDED:
            ax = SMALL_SHARDED[n]
            size = w[n].shape[ax]
            g = lax.dynamic_slice_in_dim(g, me * size, size, axis=ax)
        g_loc.append(g)
    shapes = [w[n].shape for n in sm_all]
    res = adamw("adamw_small", _pack([w[n] for n in sm_all]), _pack([m[n] for n in sm_all]), _pack([v[n] for n in sm_all]), _pack(g_loc))
    last_small = res[0]
    res = [g_loc] + [_unpack(r, shapes) for r in res]
    for k, n in enumerate(sm_all):
        out[n] = [r[k] for r in res]

    chain, behind = {}, last_small
    for s in reversed(range(len(stages))):
        lands = split_wait(f"grads{s}_wait", "scatter", scatter[s], behind if s == 0 else dx)
        for own, land, (n, l, _) in zip(own32[s], lands, stages[s][3], strict=True):
            chain[n] = adamw_sharded(f"adamw_{n}_{l}", l, _flat2d(w[n]), _flat2d(m[n]), _flat2d(v[n]), own, land, me1, chain.get(n))
            behind = chain[n][0]
    for n in BIG:
        out[n] = [t.reshape(w[n].shape) for t in chain[n]]

    grads, deltas, new_m, new_v = ([out[n][k] for n in WEIGHTS] for k in range(4))
    return (loss, dx.reshape(Bl, S, D), *grads, *deltas, *new_m, *new_v)
```

```python
import functools
import math

import numpy as np
import jax
import jax.numpy as jnp
from jax import lax
from jax.experimental import pallas as pl
from jax.experimental.pallas import tpu as pltpu

F32 = jnp.float32
BF16 = jnp.bfloat16
HIGHEST = lax.Precision.HIGHEST
MESH = pl.DeviceIdType.MESH

D_MODEL = 1024
DEPTH = 4
N_EVEN = 2
N_ODD = 2
GLA_HEADS, GLA_DK, GLA_DV, GLA_RANK, GLA_GATE_NORM = 4, 64, 128, 16, 16.0
HGRN_HEADS, HGRN_DK, HGRN_DV, HGRN_MIN_F = 4, 64, 128, 1e-20
RET_HEADS, RET_DK, RET_DV = 4, 128, 192
ROPE_BASE = 10000.0
S5_WIDTH, S5_GROUP_CH, S5_GROUPS, S5_STATE = 256, 16, 16, 64
S5_N = S5_GROUPS * S5_STATE
FFN_DIM = 2816
EPS = 1e-6
EVEN_IN = 3360
ODD_IN = 2816
ADAM_LR, ADAM_B1, ADAM_B2, ADAM_EPS, ADAM_WD, ADAM_STEP = 0.001, 0.9, 0.999, 1e-08, 0.01, 10

N_DEV = 8
VMEM_LIMIT_BYTES = 56 * 1024 * 1024
ROW_TILE = 256
SCAN_CHUNK = 64
S5_SEGMENTS = 8
LANES = 128


def _params(*sem):
    return pltpu.CompilerParams(dimension_semantics=sem, vmem_limit_bytes=VMEM_LIMIT_BYTES)


def _divisor_tile(n, cap):
    best = None
    for t in range(LANES, min(n, cap) + 1, LANES):
        if n % t == 0:
            best = t
    return best if best is not None else n


def _mm_nn(name, x, w):
    M, K = x.shape
    N = w.shape[1]
    tn = _divisor_tile(N, 2048) if K * N * 2 > 8 * 2**20 else N
    tm = 256 if tn * 4 * 512 > 6 * 2**20 else 512
    assert M % tm == 0 and N % tn == 0

    def body(x_ref, w_ref, o_ref):
        o_ref[...] = jnp.dot(x_ref[...].astype(BF16), w_ref[...], preferred_element_type=F32)

    return pl.pallas_call(
        body, name=name, grid=(N // tn, M // tm),
        in_specs=[pl.BlockSpec((tm, K), lambda j, i: (i, 0)), pl.BlockSpec((K, tn), lambda j, i: (0, j))],
        out_specs=pl.BlockSpec((tm, tn), lambda j, i: (i, j)),
        out_shape=jax.ShapeDtypeStruct((M, N), F32),
        compiler_params=_params("parallel", "parallel"),
    )(x, w)


def _mm_nt(name, dy, w):
    M, N = dy.shape
    K = w.shape[0]
    tk = _divisor_tile(K, 1024) if K * N * 2 > 8 * 2**20 else K
    tm = 256 if N >= 4096 else 512
    assert M % tm == 0 and K % tk == 0

    def body(dy_ref, w_ref, o_ref):
        o_ref[...] = lax.dot_general(dy_ref[...].astype(BF16), w_ref[...], (((1,), (1,)), ((), ())),
                                     preferred_element_type=F32)

    return pl.pallas_call(
        body, name=name, grid=(K // tk, M // tm),
        in_specs=[pl.BlockSpec((tm, N), lambda j, i: (i, 0)), pl.BlockSpec((tk, N), lambda j, i: (j, 0))],
        out_specs=pl.BlockSpec((tm, tk), lambda j, i: (i, j)),
        out_shape=jax.ShapeDtypeStruct((M, K), F32),
        compiler_params=_params("parallel", "parallel"),
    )(dy, w)


MM_TN_VMEM_BUDGET = 46 * 2**20


def _pad_lanes(n):
    return -(-n // LANES) * LANES


def _mm_tn(name, x, dy, nblk, want16):
    M, K = x.shape
    N = dy.shape[1]
    n = N // nblk
    xb, yb = x.dtype.itemsize, dy.dtype.itemsize
    best = None
    for tk in [t for t in range(LANES, K + 1, LANES) if K % t == 0]:
        for tm in (512, 256):
            out_bytes = nblk * tk * _pad_lanes(n) * (6 if want16 else 4)
            vmem = 2 * out_bytes + 2 * tm * tk * xb + 2 * tm * _pad_lanes(N) * yb
            traffic = (K // tk) * M * N * yb + M * K * xb
            if vmem <= MM_TN_VMEM_BUDGET and M % tm == 0 and (best is None or (traffic, -tm) < best[0]):
                best = ((traffic, -tm), tk, tm)
    _, tk, tm = best
    last = M // tm - 1

    def body(x_ref, dy_ref, o32_ref, *o16_ref):
        m = pl.program_id(1)

        @pl.when(m == 0)
        def _():
            o32_ref[...] = jnp.zeros_like(o32_ref)

        dyv = dy_ref[...].astype(BF16)
        if nblk == 1:
            o32_ref[0] += lax.dot_general(x_ref[...].astype(BF16), dyv, (((0,), (0,)), ((), ())), preferred_element_type=F32)
        else:
            xt = x_ref[...].astype(F32).T.astype(BF16)
            for d in range(nblk):
                o32_ref[d] += jnp.dot(xt, dyv[:, d * n:(d + 1) * n], preferred_element_type=F32)
        if want16:
            @pl.when(m == last)
            def _():
                o16_ref[0][...] = o32_ref[...].astype(BF16)

    blk = pl.BlockSpec((nblk, tk, n), lambda a, m: (0, a, 0))
    return pl.pallas_call(
        body, name=name, grid=(K // tk, M // tm),
        in_specs=[pl.BlockSpec((tm, tk), lambda a, m: (m, a)), pl.BlockSpec((tm, N), lambda a, m: (m, 0))],
        out_specs=[blk, blk] if want16 else [blk],
        out_shape=[jax.ShapeDtypeStruct((nblk, K, n), F32)] + ([jax.ShapeDtypeStruct((nblk, K, n), BF16)] if want16 else []),
        compiler_params=_params("parallel", "arbitrary"),
    )(x, dy)


def _weight_grads(name, x, dy, col_sharded):
    if col_sharded:
        return _mm_tn(name, x, dy, N_DEV, True)
    d32, d16 = _mm_tn(name, x, dy, 1, True)
    K, N = d32.shape[1:]
    return d32.reshape(N_DEV, K // N_DEV, N), d16.reshape(N_DEV, K // N_DEV, N)


def make_mm(name, col_sharded=False):
    @jax.custom_vjp
    def mm(x, w16, c32, c16):
        return _mm_nn(name + "_fwd", x, w16)

    def fwd(x, w16, c32, c16):
        return _mm_nn(name + "_fwd", x, w16), (x, w16)

    def bwd(res, dy):
        x, w16 = res
        d32, d16 = _weight_grads(name + "_dw", x, dy, col_sharded)
        return _mm_nt(name + "_dx", dy, w16), jnp.zeros_like(w16), d32, d16

    mm.defvjp(fwd, bwd)
    return mm


def make_mm_f32w(name):
    @jax.custom_vjp
    def mm(x, w):
        return _mm_nn(name + "_fwd", x, w.astype(BF16))

    def fwd(x, w):
        w16 = w.astype(BF16)
        return _mm_nn(name + "_fwd", x, w16), (x, w16)

    def bwd(res, dy):
        x, w16 = res
        return _mm_nt(name + "_dx", dy, w16), _mm_tn(name + "_dw", x, dy, 1, False)[0][0]

    mm.defvjp(fwd, bwd)
    return mm


def _row_specs(rows, params, consts, tile):
    specs = [pl.BlockSpec((tile, r.shape[1]), lambda i: (i, 0)) for r in rows]
    specs += [pl.BlockSpec(p.shape, lambda i: (0, 0)) for p in params]
    specs += [pl.BlockSpec((tile, c.shape[1]), lambda i, n=c.shape[0] // tile: (i % n, 0)) for c in consts]
    return specs


def _row_fwd(name, f, out_widths, tile, rows, params, consts):
    T = rows[0].shape[0]
    nr, npar, ncon = len(rows), len(params), len(consts)

    def body(*refs):
        r = tuple(x[...] for x in refs[:nr])
        p = tuple(x[...] for x in refs[nr:nr + npar])
        c = tuple(x[...] for x in refs[nr + npar:nr + npar + ncon])
        outs = f(r, p, c)
        for o_ref, o in zip(refs[nr + npar + ncon:], outs, strict=True):
            o_ref[...] = o

    return pl.pallas_call(
        body, name=name + "_fwd", grid=(T // tile,),
        in_specs=_row_specs(rows, params, consts, tile),
        out_specs=[pl.BlockSpec((tile, w), lambda i: (i, 0)) for w in out_widths],
        out_shape=[jax.ShapeDtypeStruct((T, w), F32) for w in out_widths],
        compiler_params=_params("parallel"),
    )(*rows, *params, *consts)


def _row_bwd(name, f, out_widths, tile, rows, params, consts, gouts, dr_dtype=F32):
    T = rows[0].shape[0]
    nr, npar, ncon, nout = len(rows), len(params), len(consts), len(out_widths)

    def body(*refs):
        r = tuple(x[...] for x in refs[:nr])
        p = tuple(x[...] for x in refs[nr:nr + npar])
        c = tuple(x[...] for x in refs[nr + npar:nr + npar + ncon])
        k = nr + npar + ncon
        g = tuple(x[...] for x in refs[k:k + nout])
        dr_refs = refs[k + nout:k + nout + nr]
        dp_refs = refs[k + nout + nr:]
        _, vjp = jax.vjp(lambda r_, p_: tuple(f(r_, p_, c)), r, p)
        dr, dp = vjp(g)
        for ref, val in zip(dr_refs, dr, strict=True):
            ref[...] = val.astype(ref.dtype)
        if npar:
            @pl.when(pl.program_id(0) == 0)
            def _():
                for ref in dp_refs:
                    ref[...] = jnp.zeros_like(ref)

            for ref, val in zip(dp_refs, dp, strict=True):
                ref[...] += val

    outs = pl.pallas_call(
        body, name=name + "_bwd", grid=(T // tile,),
        in_specs=_row_specs(rows, params, consts, tile) + [pl.BlockSpec((tile, w), lambda i: (i, 0)) for w in out_widths],
        out_specs=[pl.BlockSpec((tile, r.shape[1]), lambda i: (i, 0)) for r in rows]
        + [pl.BlockSpec(p.shape, lambda i: (0, 0)) for p in params],
        out_shape=[jax.ShapeDtypeStruct(r.shape, dr_dtype) for r in rows] + [jax.ShapeDtypeStruct(p.shape, F32) for p in params],
        compiler_params=_params("arbitrary"),
    )(*rows, *params, *consts, *gouts)
    return tuple(outs[:nr]), tuple(outs[nr:])


def make_proj_stage(name, f, out_widths, tile=ROW_TILE):
    def run(x, w16, params, consts):
        p = _mm_nn(name + "_mm", x, w16)
        return p, tuple(_row_fwd(name, f, out_widths, tile, (p,), params, consts))

    @jax.custom_vjp
    def op(x, w16, c32, c16, params, consts):
        return run(x, w16, params, consts)[1]

    def fwd(x, w16, c32, c16, params, consts):
        p, outs = run(x, w16, params, consts)
        return outs, (x, w16, p, params, consts)

    def bwd(res, g):
        x, w16, p, params, consts = res
        (dp,), dparams = _row_bwd(name, f, out_widths, tile, (p,), params, consts, tuple(g), dr_dtype=BF16)
        d32, d16 = _weight_grads(name + "_dw", x, dp, True)
        return _mm_nt(name + "_dx", dp, w16), jnp.zeros_like(w16), d32, d16, dparams, tuple(jnp.zeros_like(c) for c in consts)

    op.defvjp(fwd, bwd)
    return op


def make_rowop(name, f, out_widths, tile=ROW_TILE):
    @jax.custom_vjp
    def op(rows, params, consts):
        return tuple(_row_fwd(name, f, out_widths, tile, rows, params, consts))

    def fwd(rows, params, consts):
        return op(rows, params, consts), (rows, params, consts)

    def bwd(res, g):
        rows, params, consts = res
        dr, dp = _row_bwd(name, f, out_widths, tile, rows, params, consts, tuple(g))
        return dr, dp, tuple(jnp.zeros_like(c) for c in consts)

    op.defvjp(fwd, bwd)
    return op


def _rms(x, g):
    return x * lax.rsqrt(jnp.mean(x * x, axis=-1, keepdims=True) + EPS) * g


def _silu(x):
    return x * jax.nn.sigmoid(x)


def _bdot(a, b):
    return jnp.dot(a.astype(BF16), b.astype(BF16), preferred_element_type=F32)


def norm_f(rows, params, consts):
    return (_rms(rows[0], params[0]),)


def addnorm_f(rows, params, consts):
    x = rows[0] + rows[1]
    return x, _rms(x, params[0])


EVEN_GLA_END = 1568


def even_prep_f(rows, params, consts):
    (p,) = rows
    wa2f, wa2b, baf, bab, lbf, lbb = params
    gq = p[:, 0:256]
    gk = p[:, 256:512] * (GLA_DK ** -0.5)
    gv = p[:, 512:1024]
    gr = p[:, 1024:1536]
    glaf = jax.nn.log_sigmoid(_bdot(p[:, 1536:1552], wa2f) + baf) / GLA_GATE_NORM
    glab = jax.nn.log_sigmoid(_bdot(p[:, 1552:1568], wa2b) + bab) / GLA_GATE_NORM
    o = EVEN_GLA_END
    hq = _silu(p[:, o:o + 256])

    def gate(z, lb):
        f = lb + (1.0 - lb) * jax.nn.sigmoid(z)
        return jnp.log(jnp.maximum(f, HGRN_MIN_F)), (1.0 - lb) * jax.nn.sigmoid(-z)

    hlaf, hkf = gate(p[:, o + 256:o + 512], lbf)
    hlab, hkb = gate(p[:, o + 512:o + 768], lbb)
    hv = p[:, o + 768:o + 1280]
    hg = p[:, o + 1280:o + 1792]
    return gq, gk, gv, glaf, glab, gr, hq, hkf, hkb, hv, hlaf, hlab, hg


EVEN_PREP_WIDTHS = (256, 256, 512, 256, 256, 512, 256, 256, 256, 512, 256, 256, 512)


def _head_rms(o, g, heads, d):
    parts = []
    for h in range(heads):
        seg = o[:, h * d:(h + 1) * d]
        parts.append(seg * lax.rsqrt(jnp.mean(seg * seg, axis=-1, keepdims=True) + EPS))
    return jnp.concatenate(parts, axis=1) * g


def even_post_f(rows, params, consts):
    of, ob, hof, hob, gr, hg = rows
    a = _head_rms(of + ob, params[0], GLA_HEADS, GLA_DV) * _silu(gr)
    b = _head_rms(hof + hob, params[1], HGRN_HEADS, HGRN_DV) * _silu(hg)
    return (jnp.concatenate([a, b], axis=1),)


@jax.custom_vjp
def _roll_half(x):
    return pltpu.roll(x, RET_DK // 2, 1)


_roll_half.defvjp(lambda x: (_roll_half(x), None), lambda _, g: (_roll_half(g),))


def odd_prep_f(rows, params, consts):
    (p,) = rows
    cosf, sinf = consts

    def rot(t):
        parts = []
        for h in range(RET_HEADS):
            th = t[:, h * RET_DK:(h + 1) * RET_DK]
            parts.append(th * cosf + _roll_half(th) * sinf)
        return jnp.concatenate(parts, axis=1)

    rq = rot(p[:, 0:512])
    rk = rot(p[:, 512:1024]) * (RET_DK ** -0.5)
    return rq, rk, p[:, 1024:1792], p[:, 1792:2560], p[:, 2560:2816]


ODD_PREP_WIDTHS = (512, 512, 768, 768, 256)


def ret_post_f(rows, params, consts):
    of, ob, rg = rows
    o = of + ob
    parts = []
    for h in range(RET_HEADS):
        seg = o[:, h * RET_DV:(h + 1) * RET_DV]
        c = seg - jnp.mean(seg, axis=-1, keepdims=True)
        parts.append(c * lax.rsqrt(jnp.mean(c * c, axis=-1, keepdims=True) + EPS))
    return (jnp.concatenate(parts, axis=1) * params[0] * _silu(rg),)


def s5_post_f(rows, params, consts):
    h0r, h0i, h1r, h1i, u = rows
    c_re, c_im, d_skip, glu_w, glu_b = params
    hr = h0r + h1r
    hi = h0i + h1i
    y = _bdot(hr, c_re) - _bdot(hi, c_im) + d_skip * u
    g = jax.nn.gelu(y)
    return (g * jax.nn.sigmoid(_bdot(g, glu_w) + glu_b),)


def loss_head(x, r, g, target, tile=ROW_TILE):
    T, D = x.shape

    def body(x_ref, r_ref, g_ref, t_ref, loss_ref, dx_ref, dg_ref):
        t = t_ref[...]

        def lf(xv, gv):
            e = _rms(xv, gv) - t
            row = jnp.sum(e * e, axis=-1, keepdims=True) * (0.5 / D)
            return jnp.sum(row, axis=0, keepdims=True)

        l, vjp = jax.vjp(lf, x_ref[...] + r_ref[...], g_ref[...])
        dx, dg = vjp(jnp.ones((1, 1), F32))
        dx_ref[...] = dx

        @pl.when(pl.program_id(0) == 0)
        def _():
            loss_ref[...] = jnp.zeros_like(loss_ref)
            dg_ref[...] = jnp.zeros_like(dg_ref)

        loss_ref[...] += jnp.broadcast_to(l, loss_ref.shape)
        dg_ref[...] += dg

    row = pl.BlockSpec((tile, D), lambda i: (i, 0))
    vec = pl.BlockSpec((1, D), lambda i: (0, 0))
    return pl.pallas_call(
        body, name="loss_head", grid=(T // tile,),
        in_specs=[row, row, vec, row],
        out_specs=[pl.BlockSpec((1, LANES), lambda i: (0, 0)), row, vec],
        out_shape=[jax.ShapeDtypeStruct((1, LANES), F32), jax.ShapeDtypeStruct((T, D), F32), jax.ShapeDtypeStruct((1, D), F32)],
        compiler_params=_params("arbitrary"),
    )(x, r, g, target)


SUBLANES = 8


def _halo_specs(width, tile, T):
    n8 = tile // SUBLANES
    last = T // SUBLANES - 1
    return [pl.BlockSpec((tile, width), lambda i: (i, 0)),
            pl.BlockSpec((SUBLANES, width), lambda i: (jnp.maximum(i * n8 - 1, 0), 0)),
            pl.BlockSpec((SUBLANES, width), lambda i: (jnp.minimum((i + 1) * n8, last), 0))]


def _shift_rows(x, prev_row, next_row, tile):
    row = lax.broadcasted_iota(jnp.int32, (tile, 1), 0)
    down = jnp.where(row == 0, prev_row, pltpu.roll(x, 1, 0))
    up = jnp.where(row == tile - 1, next_row, pltpu.roll(x, tile - 1, 0))
    return down, up


def _conv_fwd(name, u, cw, cb, S, tile):
    T, F2 = u.shape
    F = F2 // 2
    per_seq = S // tile

    def body(u_ref, up_ref, un_ref, cw_ref, cb_ref, g_ref):
        pos = pl.program_id(0) % per_seq
        uv = u_ref[...]
        prev_row = jnp.where(pos == 0, 0.0, up_ref[SUBLANES - 1:SUBLANES, :])
        next_row = jnp.where(pos == per_seq - 1, 0.0, un_ref[0:1, :])
        down, up = _shift_rows(uv, prev_row, next_row, tile)
        c = _conv_taps(down, uv, up, cw_ref, cb_ref)
        g_ref[...] = (_silu(c[:, :F]) * c[:, F:]).astype(BF16)

    return pl.pallas_call(
        body, name=name + "_fwd", grid=(T // tile,),
        in_specs=_halo_specs(F2, tile, T) + [pl.BlockSpec((3, F2), lambda i: (0, 0)), pl.BlockSpec((1, F2), lambda i: (0, 0))],
        out_specs=pl.BlockSpec((tile, F), lambda i: (i, 0)),
        out_shape=jax.ShapeDtypeStruct((T, F), BF16),
        compiler_params=_params("parallel"),
    )(u, u, u, cw, cb)


def _conv_taps(down, mid, up, cw_ref, cb_ref):
    c = cb_ref[...] + down * cw_ref[0:1, :]
    c = c + mid * cw_ref[1:2, :]
    return c + up * cw_ref[2:3, :]


def _conv_bwd(name, u, dg, cw, cb, S, tile):
    T, F2 = u.shape
    F = F2 // 2
    per_seq = S // tile

    def dact(cv, dgv):
        a, v = cv[:, :F], cv[:, F:]
        sg = jax.nn.sigmoid(a)
        return jnp.concatenate([dgv * v * (sg * (1.0 + a * (1.0 - sg))), dgv * (a * sg)], axis=1)

    def body(u_ref, up_ref, un_ref, g_ref, gp_ref, gn_ref, cw_ref, cb_ref, du_ref, dw0_ref, dw1_ref, dw2_ref, db_ref):
        i = pl.program_id(0)
        pos = i % per_seq
        first, last = pos == 0, pos == per_seq - 1
        lo, hi = slice(SUBLANES - 1, SUBLANES), slice(0, 1)
        uv = u_ref[...]
        u_m1, u_p1 = jnp.where(first, 0.0, up_ref[lo, :]), jnp.where(last, 0.0, un_ref[hi, :])
        u_dn, u_up = _shift_rows(uv, u_m1, u_p1, tile)
        dc = dact(_conv_taps(u_dn, uv, u_up, cw_ref, cb_ref), g_ref[...])
        c_m1 = _conv_taps(up_ref[SUBLANES - 2:SUBLANES - 1, :], u_m1, u_ref[0:1, :], cw_ref, cb_ref)
        c_p1 = _conv_taps(u_ref[tile - 1:tile, :], u_p1, un_ref[1:2, :], cw_ref, cb_ref)
        dc_prev = jnp.where(first, 0.0, dact(c_m1, gp_ref[lo, :]))
        dc_next = jnp.where(last, 0.0, dact(c_p1, gn_ref[hi, :]))
        dc_dn, dc_up = _shift_rows(dc, dc_prev, dc_next, tile)
        du = dc_up * cw_ref[0:1, :]
        du = du + dc * cw_ref[1:2, :]
        du_ref[...] = (du + dc_dn * cw_ref[2:3, :]).astype(BF16)

        @pl.when(i == 0)
        def _():
            for ref in (dw0_ref, dw1_ref, dw2_ref, db_ref):
                ref[...] = jnp.zeros_like(ref)

        dw0_ref[...] += jnp.sum(dc * u_dn, axis=0, keepdims=True)
        dw1_ref[...] += jnp.sum(dc * uv, axis=0, keepdims=True)
        dw2_ref[...] += jnp.sum(dc * u_up, axis=0, keepdims=True)
        db_ref[...] += jnp.sum(dc, axis=0, keepdims=True)

    vec = pl.BlockSpec((1, F2), lambda i: (0, 0))
    du, dw0, dw1, dw2, db = pl.pallas_call(
        body, name=name + "_bwd", grid=(T // tile,),
        in_specs=_halo_specs(F2, tile, T) + _halo_specs(F, tile, T) + [pl.BlockSpec((3, F2), lambda i: (0, 0)), vec],
        out_specs=[pl.BlockSpec((tile, F2), lambda i: (i, 0)), vec, vec, vec, vec],
        out_shape=[jax.ShapeDtypeStruct((T, F2), BF16)] + [jax.ShapeDtypeStruct((1, F2), F32)] * 4,
        compiler_params=_params("arbitrary"),
    )(u, u, u, dg, dg, dg, cw, cb)
    return du, jnp.concatenate([dw0, dw1, dw2], axis=0), db


def make_ffn(name, S):
    def run(x, wup16, wdn16, cw, cb):
        u = _mm_nn(name + "_up", x, wup16)
        g16 = _conv_fwd(name + "_conv", u, cw, cb, S, ROW_TILE)
        return u, g16, _mm_nn(name + "_down", g16, wdn16)

    @jax.custom_vjp
    def op(x, wup16, cu32, cu16, wdn16, cd32, cd16, cw, cb):
        return run(x, wup16, wdn16, cw, cb)[2]

    def fwd(x, wup16, cu32, cu16, wdn16, cd32, cd16, cw, cb):
        u, g16, out = run(x, wup16, wdn16, cw, cb)
        return out, (x, wup16, wdn16, u, g16, cw, cb)

    def bwd(res, dout):
        x, wup16, wdn16, u, g16, cw, cb = res
        dg = _mm_nt(name + "_down_dx", dout, wdn16)
        dd32, dd16 = _weight_grads(name + "_down_dw", g16, dout, False)
        du16, dcw, dcb = _conv_bwd(name + "_conv", u, dg, cw, cb, S, ROW_TILE)
        du32, du16w = _weight_grads(name + "_up_dw", x, du16, True)
        return (_mm_nt(name + "_up_dx", du16, wup16), jnp.zeros_like(wup16), du32, du16w, jnp.zeros_like(wdn16), dd32, dd16, dcw, dcb)

    op.defvjp(fwd, bwd)
    return op


def _dot_nt(a, b):
    return lax.dot_general(a.astype(BF16), b.astype(BF16), (((1,), (1,)), ((), ())), preferred_element_type=F32)


def _dot_tn(a, b):
    return lax.dot_general(a.astype(BF16), b.astype(BF16), (((0,), (0,)), ((), ())), preferred_element_type=F32)


def _chunk_decays(la, direction, C, width, dk, const_lg):
    row = lax.broadcasted_iota(jnp.int32, (C, C), 0)
    col = lax.broadcasted_iota(jnp.int32, (C, C), 1)
    keep = (row >= col) if direction == 0 else (row <= col)
    ridx = lax.broadcasted_iota(jnp.int32, (C, 1), 0)
    if const_lg is None:
        cum = jnp.dot(keep.astype(F32), la, precision=HIGHEST, preferred_element_type=F32)
    else:
        lane_head = lax.broadcasted_iota(jnp.int32, (1, width), 1) // dk
        lg = jnp.zeros((1, width), F32)
        for h, val in enumerate(const_lg):
            lg = jnp.where(lane_head == h, val, lg)
        steps = (ridx + 1) if direction == 0 else (C - ridx)
        cum = steps.astype(F32) * lg
    exit_row = C - 1 if direction == 0 else 0
    mid = jnp.sum(jnp.where(ridx == C // 2, cum, 0.0), axis=0, keepdims=True)
    last = jnp.sum(jnp.where(ridx == exit_row, cum, 0.0), axis=0, keepdims=True)
    return keep, ridx == exit_row, cum, mid, last


def _scan_fwd(name, q, kf, kb, v, laf, lab, H, dk, dv, S, C, const_lg):
    T = q.shape[0]
    B, nc = T // S, S // C
    Wk, Wv = H * dk, H * dv
    learn = const_lg is None

    def body(*refs):
        @pl.when(pl.program_id(0) == 0)
        def _():
            refs[-1][...] = jnp.zeros_like(refs[-1])

        for b in range(B):
            one_sequence(*[r.at[b] for r in refs])

    def one_sequence(*refs):
        if learn:
            qf_r, qb_r, kf_r, kb_r, vf_r, vb_r, laf_r, lab_r, of_r, ob_r, sf_r, sb_r, st = refs
            las = (laf_r[...], lab_r[...])
        else:
            qf_r, qb_r, kf_r, kb_r, vf_r, vb_r, of_r, ob_r, sf_r, sb_r, st = refs
            las = (None, None)

        for d, (q_r, k_r, v_r, o_r, s_r) in enumerate(((qf_r, kf_r, vf_r, of_r, sf_r), (qb_r, kb_r, vb_r, ob_r, sb_r))):
            keep, _, cum, mid, last = _chunk_decays(las[d], d, C, Wk, dk, None if learn else const_lg[d])
            qe = q_r[...] * jnp.exp(cum - mid)
            ke = k_r[...] * jnp.exp(mid - cum)
            q_in = qe * jnp.exp(mid)
            k_out = ke * jnp.exp(last - mid)
            e_last = jnp.exp(last)
            vv = v_r[...]
            for h in range(H):
                ks, vs = slice(h * dk, (h + 1) * dk), slice(h * dv, (h + 1) * dv)
                a = jnp.where(keep, _dot_nt(qe[:, ks], ke[:, ks]), 0.0)
                state = st[d, h]
                o_r[:, vs] = _bdot(a, vv[:, vs]) + _dot_nt(q_in[:, ks], state)
                s_r[h * dv:(h + 1) * dv, :] = state
                st[d, h] = state * e_last[:, ks] + _dot_tn(vv[:, vs], k_out[:, ks])

    fpos = lambda c: c
    bpos = lambda c: nc - 1 - c
    kspec = lambda pos: pl.BlockSpec((B, C, Wk), lambda c: (0, pos(c), 0))
    vspec = lambda pos: pl.BlockSpec((B, C, Wv), lambda c: (0, pos(c), 0))
    sspec = lambda pos: pl.BlockSpec((B, None, Wv, dk), lambda c: (0, pos(c), 0, 0))
    seq = lambda t: t.reshape(B, S, t.shape[1])
    ins = [seq(t) for t in [q, q, kf, kb, v, v] + ([laf, lab] if learn else [])]
    in_specs = [kspec(fpos), kspec(bpos), kspec(fpos), kspec(bpos), vspec(fpos), vspec(bpos)] + ([kspec(fpos), kspec(bpos)] if learn else [])
    of, ob, sf, sb = pl.pallas_call(
        body, name=name + "_fwd", grid=(nc,), in_specs=in_specs,
        out_specs=[vspec(fpos), vspec(bpos), sspec(fpos), sspec(bpos)],
        out_shape=[jax.ShapeDtypeStruct((B, S, Wv), F32)] * 2 + [jax.ShapeDtypeStruct((B, nc, Wv, dk), F32)] * 2,
        scratch_shapes=[pltpu.VMEM((B, 2, H, dv, dk), F32)],
        compiler_params=_params("arbitrary"),
    )(*ins)
    return of.reshape(T, Wv), ob.reshape(T, Wv), sf, sb


def _scan_bwd(name, q, kf, kb, v, laf, lab, sf, sb, dof, dob, H, dk, dv, S, C, const_lg):
    T = q.shape[0]
    B, nc = T // S, S // C
    Wk, Wv = H * dk, H * dv
    learn = const_lg is None

    def body(*refs):
        @pl.when(pl.program_id(0) == 0)
        def _():
            refs[-1][...] = jnp.zeros_like(refs[-1])

        for b in range(B):
            one_sequence(*[r.at[b] for r in refs])

    def one_sequence(*refs):
        if learn:
            (qf_r, qb_r, kf_r, kb_r, vf_r, vb_r, laf_r, lab_r, sf_r, sb_r, dof_r, dob_r,
             dqf_r, dqb_r, dkf_r, dkb_r, dvf_r, dvb_r, dlaf_r, dlab_r, dst) = refs
            las, dlas = (laf_r[...], lab_r[...]), (dlaf_r, dlab_r)
        else:
            (qf_r, qb_r, kf_r, kb_r, vf_r, vb_r, sf_r, sb_r, dof_r, dob_r,
             dqf_r, dqb_r, dkf_r, dkb_r, dvf_r, dvb_r, dst) = refs
            las, dlas = (None, None), (None, None)

        groups = ((qf_r, kf_r, vf_r, sf_r, dof_r, dqf_r, dkf_r, dvf_r), (qb_r, kb_r, vb_r, sb_r, dob_r, dqb_r, dkb_r, dvb_r))
        for d, (q_r, k_r, v_r, s_r, do_r, dq_r, dk_r, dv_r) in enumerate(groups):
            keep, is_exit, cum, mid, last = _chunk_decays(las[d], d, C, Wk, dk, None if learn else const_lg[d])
            eq, ek = jnp.exp(cum - mid), jnp.exp(mid - cum)
            e_in, e_out, e_last = jnp.exp(mid), jnp.exp(last - mid), jnp.exp(last)
            qe, ke = q_r[...] * eq, k_r[...] * ek
            q_in, k_out = qe * e_in, ke * e_out
            vv, do = v_r[...], do_r[...]
            dqe_parts, dke_parts, dlast_parts = [], [], []
            for h in range(H):
                ks, vs = slice(h * dk, (h + 1) * dk), slice(h * dv, (h + 1) * dv)
                a = jnp.where(keep, _dot_nt(qe[:, ks], ke[:, ks]), 0.0)
                dp = jnp.where(keep, _dot_nt(do[:, vs], vv[:, vs]), 0.0)
                s_prev = s_r[h * dv:(h + 1) * dv, :]
                ds = dst[d, h]
                dk_out = _bdot(vv[:, vs], ds)
                dqe_parts.append(_bdot(dp, ke[:, ks]) + _bdot(do[:, vs], s_prev) * e_in[:, ks])
                dke_parts.append(_dot_tn(dp, qe[:, ks]) + dk_out * e_out[:, ks])
                dv_r[:, vs] = _dot_tn(a, do[:, vs]) + _dot_nt(k_out[:, ks], ds)
                if learn:
                    dlast_parts.append(jnp.sum(dk_out * k_out[:, ks], axis=0, keepdims=True)
                                       + jnp.sum(ds * s_prev, axis=0, keepdims=True) * e_last[:, ks])
                dst[d, h] = ds * e_last[:, ks] + _dot_tn(do[:, vs], q_in[:, ks])
            dqe = jnp.concatenate(dqe_parts, axis=1)
            dke = jnp.concatenate(dke_parts, axis=1)
            dq_r[...] = dqe * eq
            dk_r[...] = dke * ek
            if learn:
                dcum = dqe * qe - dke * ke + jnp.where(is_exit, jnp.concatenate(dlast_parts, axis=1), 0.0)
                dlas[d][...] = lax.dot_general(keep.astype(F32), dcum, (((0,), (0,)), ((), ())), precision=HIGHEST,
                                               preferred_element_type=F32)

    fpos = lambda c: nc - 1 - c
    bpos = lambda c: c
    kspec = lambda pos: pl.BlockSpec((B, C, Wk), lambda c: (0, pos(c), 0))
    vspec = lambda pos: pl.BlockSpec((B, C, Wv), lambda c: (0, pos(c), 0))
    sspec = lambda pos: pl.BlockSpec((B, None, Wv, dk), lambda c: (0, pos(c), 0, 0))
    seq = lambda t: t.reshape(B, S, t.shape[1])
    ins = [seq(t) for t in [q, q, kf, kb, v, v] + ([laf, lab] if learn else [])] + [sf, sb, seq(dof), seq(dob)]
    in_specs = ([kspec(fpos), kspec(bpos), kspec(fpos), kspec(bpos), vspec(fpos), vspec(bpos)]
                + ([kspec(fpos), kspec(bpos)] if learn else []) + [sspec(fpos), sspec(bpos), vspec(fpos), vspec(bpos)])
    out_specs = [kspec(fpos), kspec(bpos), kspec(fpos), kspec(bpos), vspec(fpos), vspec(bpos)] + ([kspec(fpos), kspec(bpos)] if learn else [])
    out_shape = ([jax.ShapeDtypeStruct((B, S, Wk), F32)] * 4 + [jax.ShapeDtypeStruct((B, S, Wv), F32)] * 2
                 + ([jax.ShapeDtypeStruct((B, S, Wk), F32)] * 2 if learn else []))
    outs = pl.pallas_call(
        body, name=name + "_bwd", grid=(nc,), in_specs=in_specs, out_specs=out_specs, out_shape=out_shape,
        scratch_shapes=[pltpu.VMEM((B, 2, H, dv, dk), F32)],
        compiler_params=_params("arbitrary"),
    )(*ins)
    return [t.reshape(T, t.shape[2]) for t in outs]


def make_scan(name, H, dk, dv, S, C=SCAN_CHUNK, const_lg=None):
    if const_lg is None:
        @jax.custom_vjp
        def op(q, kf, kb, v, laf, lab):
            return tuple(_scan_fwd(name, q, kf, kb, v, laf, lab, H, dk, dv, S, C, None)[:2])

        def fwd(q, kf, kb, v, laf, lab):
            of, ob, sf, sb = _scan_fwd(name, q, kf, kb, v, laf, lab, H, dk, dv, S, C, None)
            return (of, ob), (q, kf, kb, v, laf, lab, sf, sb)

        def bwd(res, g):
            q, kf, kb, v, laf, lab, sf, sb = res
            dqf, dqb, dkf, dkb, dvf, dvb, dlaf, dlab = _scan_bwd(name, q, kf, kb, v, laf, lab, sf, sb, g[0], g[1], H, dk, dv, S, C, None)
            return dqf + dqb, dkf, dkb, dvf + dvb, dlaf, dlab
    else:
        @jax.custom_vjp
        def op(q, kf, kb, v):
            return tuple(_scan_fwd(name, q, kf, kb, v, None, None, H, dk, dv, S, C, const_lg)[:2])

        def fwd(q, kf, kb, v):
            of, ob, sf, sb = _scan_fwd(name, q, kf, kb, v, None, None, H, dk, dv, S, C, const_lg)
            return (of, ob), (q, kf, kb, v, sf, sb)

        def bwd(res, g):
            q, kf, kb, v, sf, sb = res
            dqf, dqb, dkf, dkb, dvf, dvb = _scan_bwd(name, q, kf, kb, v, None, None, sf, sb, g[0], g[1], H, dk, dv, S, C, const_lg)
            return dqf + dqb, dkf, dkb, dvf + dvb

    op.defvjp(fwd, bwd)
    return op


def _reorder_call(name, t, S, to_segments):
    T, w = t.shape
    n_it = S // S5_SEGMENTS

    def body(x_ref, o_ref):
        def step(i, carry):
            packed = pl.ds(pl.multiple_of(i * S5_SEGMENTS, S5_SEGMENTS), S5_SEGMENTS)
            spread = pl.ds(i, S5_SEGMENTS, stride=n_it)
            if to_segments:
                o_ref[packed, :] = x_ref[spread, :]
            else:
                o_ref[spread, :] = x_ref[packed, :]
            return carry

        lax.fori_loop(0, n_it, step, 0, unroll=8)

    blk = pl.BlockSpec((S, LANES), lambda b, j: (b, j))
    return pl.pallas_call(body, name=name, grid=(T // S, w // LANES), in_specs=[blk], out_specs=blk,
                          out_shape=jax.ShapeDtypeStruct(t.shape, t.dtype), compiler_params=_params("parallel", "parallel"))(t)


def make_reorder(name, S, to_segments):
    @jax.custom_vjp
    def op(t):
        return _reorder_call(name, t, S, to_segments)

    op.defvjp(lambda t: (_reorder_call(name, t, S, to_segments), None),
              lambda _, g: (_reorder_call(name + "_bwd", g, S, not to_segments),))
    return op


def _s5_scan_call(name, Xs, A, S, dirs, prev=None):
    with_p = prev is not None
    T, N = Xs[0].shape
    B, nl = T // S, N // LANES
    n_it = S // S5_SEGMENTS
    assert n_it & (n_it - 1) == 0

    def cmul(ar, ai, br, bi):
        return ar * br - ai * bi, ar * bi + ai * br

    def body(*refs):
        x, a_ref = refs[0:4], refs[4]
        if with_p:
            h_prev, x_prev, h, p_ref = refs[5:9], refs[9:13], refs[13:17], refs[17]
        else:
            h = refs[5:9]
        seg = lax.broadcasted_iota(jnp.int32, (S5_SEGMENTS, 1), 0)
        zero = jnp.zeros((S5_SEGMENTS, LANES), F32)
        a = [(jnp.broadcast_to(a_ref[2 * k:2 * k + 1, :], (S5_SEGMENTS, LANES)),
              jnp.broadcast_to(a_ref[2 * k + 1:2 * k + 2, :], (S5_SEGMENTS, LANES))) for k in range(2)]

        def rows_of(k, i):
            return pl.ds(pl.multiple_of(((n_it - 1 - i) if dirs[k] == 1 else i) * S5_SEGMENTS, S5_SEGMENTS), S5_SEGMENTS)

        def local(i, carry):
            out = []
            for k, (sr, si) in enumerate(carry):
                rows = rows_of(k, i)
                pr, pi = cmul(*a[k], sr, si)
                sr, si = pr + x[2 * k][rows, :], pi + x[2 * k + 1][rows, :]
                h[2 * k][rows, :] = sr
                h[2 * k + 1][rows, :] = si
                out.append((sr, si))
            return tuple(out)

        ends = lax.fori_loop(0, n_it, local, ((zero, zero), (zero, zero)), unroll=8)
        inherit = []
        for k, (er, ei) in enumerate(ends):
            back = dirs[k] == 1
            pr, pi = a[k]
            for _ in range(n_it.bit_length() - 1):
                pr, pi = cmul(pr, pi, pr, pi)
            shift = (S5_SEGMENTS - 1) if back else 1
            tr, ti = er, ei
            for r in (range(S5_SEGMENTS - 2, -1, -1) if back else range(1, S5_SEGMENTS)):
                nr, ni = cmul(pr, pi, pltpu.roll(tr, shift, 0), pltpu.roll(ti, shift, 0))
                tr = jnp.where(seg == r, er + nr, tr)
                ti = jnp.where(seg == r, ei + ni, ti)
            edge = (S5_SEGMENTS - 1) if back else 0
            inherit.append((jnp.where(seg == edge, 0.0, pltpu.roll(tr, shift, 0)), jnp.where(seg == edge, 0.0, pltpu.roll(ti, shift, 0))))

        def fix(i, carry):
            powers, sums = carry
            new_powers, new_sums = [], []
            for k in range(2):
                rows = rows_of(k, i)
                fr, fi = cmul(*powers[k], *inherit[k])
                sr, si = h[2 * k][rows, :] + fr, h[2 * k + 1][rows, :] + fi
                h[2 * k][rows, :] = sr
                h[2 * k + 1][rows, :] = si
                new_powers.append(cmul(*powers[k], *a[k]))
                if with_p:
                    ur = h_prev[2 * k][rows, :] - x_prev[2 * k][rows, :]
                    ui = h_prev[2 * k + 1][rows, :] - x_prev[2 * k + 1][rows, :]
                    new_sums.append((sums[k][0] + sr * ur + si * ui, sums[k][1] + si * ur - sr * ui))
            return tuple(new_powers), tuple(new_sums)

        _, sums = lax.fori_loop(0, n_it, fix, ((a[0], a[1]), ((zero, zero), (zero, zero)) if with_p else ()), unroll=8)
        if with_p:
            for k in range(2):
                p_ref[2 * k:2 * k + 1, :] = jnp.sum(sums[k][0], axis=0, keepdims=True)
                p_ref[2 * k + 1:2 * k + 2, :] = jnp.sum(sums[k][1], axis=0, keepdims=True)

    col = pl.BlockSpec((S, LANES), lambda b, j: (b, j))
    outs = pl.pallas_call(
        body, name=name, grid=(B, nl),
        in_specs=[col] * 4 + [pl.BlockSpec((4, LANES), lambda b, j: (0, j))] + ([col] * 8 if with_p else []),
        out_specs=[col] * 4 + ([pl.BlockSpec((None, 4, LANES), lambda b, j: (b, 0, j))] if with_p else []),
        out_shape=[jax.ShapeDtypeStruct((T, N), F32)] * 4 + ([jax.ShapeDtypeStruct((B, 4, N), F32)] if with_p else []),
        compiler_params=_params("parallel", "parallel"),
    )(*Xs, A, *(prev[0] + prev[1] if with_p else ()))
    return (tuple(outs[:4]), outs[4]) if with_p else tuple(outs)


def make_s5_scan(name, S):
    @jax.custom_vjp
    def op(X, A):
        return _s5_scan_call(name + "_fwd", X, A, S, (0, 1))

    def fwd(X, A):
        H = _s5_scan_call(name + "_fwd", X, A, S, (0, 1))
        return H, (X, A, H)

    def bwd(res, G):
        X, A, H = res
        conj = A * jnp.array([[1.0], [-1.0], [1.0], [-1.0]], F32)
        lam, P = _s5_scan_call(name + "_bwd", tuple(G), conj, S, (1, 0), prev=(tuple(H), tuple(X)))
        P = jnp.sum(P, axis=0)
        ar, ai = A[0::2], A[1::2]
        pr, pi = P[0::2], P[1::2]
        den = ar * ar + ai * ai
        dar, dai = (pr * ar - pi * ai) / den, (pr * ai + pi * ar) / den
        return lam, jnp.stack([dar[0], dai[0], dar[1], dai[1]], axis=0)

    op.defvjp(fwd, bwd)
    return op


ANY = pl.BlockSpec(memory_space=pl.ANY)


def _place():
    x, y, c = lax.axis_index("x"), lax.axis_index("y"), lax.axis_index("c")
    return x, y, c, [(1 - x, y), (x, 1 - y), (1 - x, 1 - y)]


def all_gather(name, arrs):
    n = len(arrs)

    def body(*refs):
        ins, outs = refs[:n], refs[n:2 * n]
        send, recv, lsem = refs[2 * n:]
        x, y, c, chips = _place()
        me, sibling = (x, y, c), (x, y, 1 - c)

        def copy(a, k, block, to, src=None):
            slot = outs[a].at[4 * block[0] + 2 * block[1] + block[2]]
            return pltpu.make_async_remote_copy(src_ref=slot if src is None else src, dst_ref=slot, send_sem=send.at[a, k],
                                                recv_sem=recv.at[a, k], device_id=to, device_id_type=MESH)

        mine = [pltpu.make_async_copy(ins[a], outs[a].at[4 * x + 2 * y + c], lsem.at[a]) for a in range(n)]
        first = []
        for a in range(n):
            mine[a].start()
            first.append(copy(a, 0, me, sibling, src=ins[a]))
            first += [copy(a, 1 + j, me, (*chip, c), src=ins[a]) for j, chip in enumerate(chips)]
        for cp in first:
            cp.start()
        passed = []
        for a in range(n):
            for j, chip in enumerate(chips):
                copy(a, 1 + j, (*chip, c), me).wait_recv()
                fwd = copy(a, 4 + j, (*chip, c), sibling)
                fwd.start()
                passed.append(fwd)
        for a in range(n):
            copy(a, 0, sibling, me).wait_recv()
            for j, chip in enumerate(chips):
                copy(a, 4 + j, (*chip, 1 - c), me).wait_recv()
        for cp in first + passed:
            cp.wait_send()
        for cp in mine:
            cp.wait()

    return pl.pallas_call(
        body, name=name, in_specs=[ANY] * n, out_specs=[ANY] * n,
        out_shape=[jax.ShapeDtypeStruct((N_DEV,) + a.shape, a.dtype) for a in arrs],
        scratch_shapes=[pltpu.SemaphoreType.DMA((n, 7)), pltpu.SemaphoreType.DMA((n, 7)), pltpu.SemaphoreType.DMA((n,))],
    )(*arrs)


HBM = pl.BlockSpec(memory_space=pltpu.HBM)
SEM = pl.BlockSpec(memory_space=pltpu.SEMAPHORE)
EFFECT = pltpu.SideEffectType.DATAFLOW_SIDE_EFFECTING
GATHER_PEERS = (1, 2, 4, 6)
OTHER_CHIPS = (2, 4, 6)
COPIES_PER_ARRAY = {"scatter": N_DEV - 1, "gather": len(GATHER_PEERS), "forward": len(OTHER_CHIPS)}


def _split_plan(mode, srcs, lands, send, recv):
    x, y, c = lax.axis_index("x"), lax.axis_index("y"), lax.axis_index("c")

    def dev(k):
        return (1 - x if k & 4 else x), (1 - y if k & 2 else y), (1 - c if k & 1 else c)

    def idx(d):
        return 4 * d[0] + 2 * d[1] + d[2]

    me = idx((x, y, c))
    plan = []
    for a, land in enumerate(lands):
        if mode == "scatter":
            legs = [(srcs[a].at[idx(dev(k))], land.at[me], land.at[idx(dev(k))], dev(k)) for k in range(1, N_DEV)]
        elif mode == "gather":
            legs = [(srcs[a], land.at[me], land.at[idx(dev(k))], dev(k)) for k in GATHER_PEERS]
        else:
            legs = [(land.at[idx(dev(j))], land.at[idx(dev(j))], land.at[idx(dev(j ^ 1))], dev(1)) for j in OTHER_CHIPS]
        for i, (src, dst, arrival, to) in enumerate(legs):
            sem = a * len(legs) + i
            pair = tuple(pltpu.make_async_remote_copy(src_ref=src, dst_ref=d, send_sem=send.at[sem], recv_sem=recv.at[sem],
                                                      device_id=to, device_id_type=MESH) for d in (dst, arrival))
            plan.append(pair)
    return plan


def split_start(name, mode, srcs, lands, after):
    if lands is None:
        lands = [lax.empty((N_DEV,) + (s.shape[1:] if mode == "scatter" else s.shape), s.dtype) for s in srcs]
    ns, nl = len(srcs), len(lands)
    nsem = COPIES_PER_ARRAY[mode] * nl

    def body(*refs):
        ins, lnd = refs[:ns], refs[ns:ns + nl]
        send, recv = refs[ns + nl + 1], refs[ns + nl + 2]
        token = refs[-1]
        for out, _ in _split_plan(mode, ins, lnd, send, recv):
            out.start()
        token[...] = jnp.zeros_like(token)

    arrs = list(srcs) + list(lands)
    return pl.pallas_call(
        body, name=name,
        out_shape=(pltpu.SemaphoreType.DMA((nsem,)), pltpu.SemaphoreType.DMA((nsem,)))
        + tuple(pltpu.HBM(t.shape, t.dtype) for t in arrs) + (jax.ShapeDtypeStruct((SUBLANES, LANES), F32),),
        in_specs=[HBM] * len(arrs) + [ANY],
        out_specs=(SEM, SEM) + (HBM,) * len(arrs) + (pl.BlockSpec(memory_space=pltpu.VMEM),),
        input_output_aliases={i: 2 + i for i in range(len(arrs))},
        compiler_params=pltpu.CompilerParams(has_side_effects=EFFECT),
    )(*[pltpu.with_memory_space_constraint(t, pltpu.HBM) for t in arrs], after)


def split_wait(name, mode, handle, after):
    send, recv = handle[0], handle[1]
    arrs = list(handle[2:-1])
    nl = len(arrs) if mode == "forward" else len(arrs) // 2
    ns = len(arrs) - nl

    def body(*refs):
        ins, lnd = refs[:ns], refs[ns:ns + nl]
        s, r = refs[ns + nl], refs[ns + nl + 1]
        for out, arrival in _split_plan(mode, ins, lnd, s, r):
            out.wait_send()
            arrival.wait_recv()

    outs = pl.pallas_call(
        body, name=name,
        out_shape=tuple(pltpu.HBM(t.shape, t.dtype) for t in arrs),
        in_specs=[HBM] * len(arrs) + [SEM, SEM, ANY], out_specs=(HBM,) * len(arrs),
        input_output_aliases={i: i for i in range(len(arrs))},
        compiler_params=pltpu.CompilerParams(has_side_effects=EFFECT),
    )(*arrs, send, recv, after)
    return list(outs[ns:])


def _row_tile(rows, cols):
    cap = max(SUBLANES, (2**18 // cols) // SUBLANES * SUBLANES)
    if rows <= cap:
        return rows
    for t in range(cap, SUBLANES - 1, -SUBLANES):
        if rows % t == 0:
            return t
    return rows


def ordered_sum(name, parts):
    n, R, C = parts.shape
    rt = _row_tile(R, C)

    def body(p_ref, o_ref):
        s = p_ref[0]
        for k in range(1, n):
            s = s + p_ref[k]
        o_ref[...] = s

    return pl.pallas_call(
        body, name=name, grid=(R // rt,),
        in_specs=[pl.BlockSpec((n, rt, C), lambda r: (0, r, 0))], out_specs=pl.BlockSpec((rt, C), lambda r: (r, 0)),
        out_shape=jax.ShapeDtypeStruct((R, C), F32), compiler_params=_params("parallel"),
    )(parts)


def _adamw_update(w, m, v, g):
    bias1 = 1.0 - ADAM_B1 ** ADAM_STEP
    bias2 = 1.0 - ADAM_B2 ** ADAM_STEP
    m_new = ADAM_B1 * m + (1.0 - ADAM_B1) * g
    v_new = ADAM_B2 * v + (1.0 - ADAM_B2) * (g * g)
    delta = -ADAM_LR * ((m_new / bias1) / (jnp.sqrt(v_new / bias2) + ADAM_EPS) + ADAM_WD * w)
    return delta, m_new, v_new


def adamw(name, w, m, v, g):
    R, C = w.shape
    rt = _row_tile(R, C)

    def body(w_ref, m_ref, v_ref, g_ref, d_ref, mo_ref, vo_ref):
        d_ref[...], mo_ref[...], vo_ref[...] = _adamw_update(w_ref[...], m_ref[...], v_ref[...], g_ref[...])

    row = pl.BlockSpec((rt, C), lambda r: (r, 0))
    return pl.pallas_call(
        body, name=name, grid=(R // rt,), in_specs=[row] * 4, out_specs=[row] * 3,
        out_shape=[jax.ShapeDtypeStruct((R, C), F32)] * 3, compiler_params=_params("parallel"),
    )(w, m, v, g)


def adamw_sharded(name, layer, w, m, v, own, land, me, prev):
    _, R, C = own.shape
    rt = _row_tile(R, C)
    base = layer * (R // rt)

    def body(me_ref, w_ref, m_ref, v_ref, own_ref, land_ref, *rest):
        go_ref, d_ref, mo_ref, vo_ref = rest[-4:]
        g = own_ref[...]
        for k in range(N_DEV):
            g = g + jnp.where(me_ref[0] == k, 0.0, land_ref[k].astype(F32))
        go_ref[...] = g
        d_ref[...], mo_ref[...], vo_ref[...] = _adamw_update(w_ref[...], m_ref[...], v_ref[...], g)

    row = pl.BlockSpec((rt, C), lambda r, p: (base + r, 0))
    in_specs = [row, row, row, pl.BlockSpec((None, rt, C), lambda r, p: (p[0], r, 0)), pl.BlockSpec((N_DEV, rt, C), lambda r, p: (0, r, 0))]
    ins = [me, w, m, v, own, land]
    aliases = {}
    if prev is not None:
        in_specs += [ANY] * 4
        aliases = {len(ins) + k: k for k in range(4)}
        ins += list(prev)
    return pl.pallas_call(
        body, name=name,
        grid_spec=pltpu.PrefetchScalarGridSpec(num_scalar_prefetch=1, grid=(R // rt,), in_specs=in_specs, out_specs=[row] * 4),
        out_shape=[jax.ShapeDtypeStruct(w.shape, F32)] * 4, input_output_aliases=aliases,
        compiler_params=_params("arbitrary"),
    )(*ins)


def _hgrn_lower_bounds(lb_logits):
    p = jax.nn.softmax(lb_logits, axis=1)
    return jnp.cumsum(p, axis=1) - p[:, :1]


def _s5_discretise(lam_re, lam_im, log_dt, b_re, b_im):
    lr = jnp.minimum(lam_re, -1e-4)
    li = lam_im
    dt = jnp.exp(log_dt)[:, None]
    mag = jnp.exp(lr * dt)
    ar, ai = mag * jnp.cos(li * dt), mag * jnp.sin(li * dt)
    den = lr * lr + li * li
    nr = ar - 1.0
    cr = (nr * lr + ai * li) / den
    ci = (ai * lr - nr * li) / den
    bbr = cr[..., None] * b_re - ci[..., None] * b_im
    bbi = cr[..., None] * b_im + ci[..., None] * b_re
    return ar, ai, bbr, bbi


def _block_diag(t):
    G, a, b = t.shape
    eye = jnp.eye(G, dtype=F32)
    return (t[:, :, None, :] * eye[:, None, :, None]).reshape(G * a, G * b)


def _rope_tables(S):
    half = RET_DK // 2
    inv = ROPE_BASE ** (-jnp.arange(half, dtype=F32) / half)
    ang = jnp.arange(S, dtype=F32)[:, None] * inv[None, :]
    cos, sin = jnp.cos(ang), jnp.sin(ang)
    return jnp.concatenate([cos, cos], axis=1), jnp.concatenate([-sin, sin], axis=1)


def _ret_log_decays():
    f = tuple(float(np.log1p(-np.exp2(np.float32(-5.0 - h)))) for h in range(RET_HEADS))
    b = tuple(float(np.log1p(-np.exp2(np.float32(-5.5 - h)))) for h in range(RET_HEADS))
    return f, b


def assemble_weight(name, land, own, me, col_sharded):
    _, R, C = land.shape
    if col_sharded:
        tr = min(R, 256)

        def body(me_ref, land_ref, own_ref, o_ref):
            for d in range(N_DEV):
                o_ref[:, d * C:(d + 1) * C] = jnp.where(me_ref[0] == d, own_ref[...], land_ref[d])

        grid, out_shape = (R // tr,), (R, N_DEV * C)
        in_specs = [pl.BlockSpec((N_DEV, tr, C), lambda i, p: (0, i, 0)), pl.BlockSpec((tr, C), lambda i, p: (i, 0))]
        out_spec = pl.BlockSpec((tr, N_DEV * C), lambda i, p: (i, 0))
    else:
        def body(me_ref, land_ref, own_ref, o_ref):
            o_ref[...] = jnp.where(me_ref[0] == pl.program_id(0), own_ref[...], land_ref[...])

        grid, out_shape = (N_DEV,), (N_DEV * R, C)
        in_specs = [pl.BlockSpec((None, R, C), lambda d, p: (d, 0, 0)), pl.BlockSpec((R, C), lambda d, p: (0, 0))]
        out_spec = pl.BlockSpec((R, C), lambda d, p: (d, 0))
    return pl.pallas_call(
        body, name=name,
        grid_spec=pltpu.PrefetchScalarGridSpec(num_scalar_prefetch=1, grid=grid, in_specs=in_specs, out_specs=out_spec),
        out_shape=jax.ShapeDtypeStruct(out_shape, land.dtype), compiler_params=_params("parallel"),
    )(me, land, own)


def _row(t):
    return t.reshape(1, -1)


def mixer_stage(layer, S, resid, branch, w_in, w_out, carriers, small):
    j = layer // 2
    tag = f"l{layer}"
    g = _row(small["mix_norm_g"][layer])
    if branch is None:
        x = resid
        (h,) = make_rowop(tag + "_norm", norm_f, (D_MODEL,))((x,), (g,), ())
    else:
        x, h = make_rowop(tag + "_addnorm", addnorm_f, (D_MODEL, D_MODEL))((resid, branch), (g,), ())
    if layer % 2 == 0:
        lbs = _hgrn_lower_bounds(small["hgrn_lb_logits"])
        prm = (small["gla_wa2"][j, 0], small["gla_wa2"][j, 1], _row(small["gla_ba"][j, 0]), _row(small["gla_ba"][j, 1]),
               _row(lbs[0, j]), _row(lbs[1, j]))
        gq, gk, gv, glaf, glab, gr, hq, hkf, hkb, hv, hlaf, hlab, hg = make_proj_stage(tag + "_prep", even_prep_f, EVEN_PREP_WIDTHS)(
            h, w_in, carriers["in32"], carriers["in16"], prm, ())
        of, ob = make_scan(tag + "_gla", GLA_HEADS, GLA_DK, GLA_DV, S, 2 * SCAN_CHUNK)(gq, gk, gk, gv, glaf, glab)
        hof, hob = make_scan(tag + "_hgrn", HGRN_HEADS, HGRN_DK, HGRN_DV, S, SCAN_CHUNK)(hq, hkf, hkb, hv, hlaf, hlab)
        (y,) = make_rowop(tag + "_post", even_post_f, (D_MODEL,))(
            (of, ob, hof, hob, gr, hg), (_row(small["gla_norm_g"][j]), _row(small["hgrn_norm_g"][j])), ())
    else:
        cosf, sinf = _rope_tables(S)
        rq, rk, rv, rg, su = make_proj_stage(tag + "_prep", odd_prep_f, ODD_PREP_WIDTHS)(
            h, w_in, carriers["in32"], carriers["in16"], (), (cosf, sinf))
        of, ob = make_scan(tag + "_ret", RET_HEADS, RET_DK, RET_DV, S, 2 * SCAN_CHUNK, const_lg=_ret_log_decays())(rq, rk, rk, rv)
        (cm,) = make_rowop(tag + "_retpost", ret_post_f, (RET_HEADS * RET_DV,))((of, ob, rg), (_row(small["ret_norm_g"][j]),), ())
        disc = [_s5_discretise(small["s5_lam_re"][j, d], small["s5_lam_im"][j, d], small["s5_log_dt"][j, d],
                               small["s5_b_re"][j], small["s5_b_im"][j]) for d in range(2)]
        a4 = jnp.stack([t.reshape(-1) for d in range(2) for t in disc[d][:2]], axis=0)
        su_p = make_reorder(tag + "_s5seg", S, True)(su)
        Xs = tuple(make_mm_f32w(f"{tag}_s5in{2 * d + i}")(su_p, _block_diag(jnp.swapaxes(disc[d][2 + i], 1, 2)))
                   for d in range(2) for i in range(2))
        Hs = make_s5_scan(tag + "_s5scan", S)(Xs, a4)
        prm = (_block_diag(jnp.swapaxes(small["s5_c_re"][j], 1, 2)), _block_diag(jnp.swapaxes(small["s5_c_im"][j], 1, 2)),
               _row(small["s5_d"][j]), small["s5_glu_w"][j], _row(small["s5_glu_b"][j]))
        (dm_p,) = make_rowop(tag + "_s5post", s5_post_f, (S5_WIDTH,))((*Hs, su_p), prm, ())
        y = jnp.concatenate([cm, make_reorder(tag + "_s5time", S, False)(dm_p)], axis=1)
    return x, make_mm(tag + "_out")(y, w_out, carriers["out32"], carriers["out16"])


def ffn_stage(layer, S, resid, branch, w_up, w_down, carriers, small):
    tag = f"l{layer}"
    x, hf = make_rowop(tag + "_ffnnorm", addnorm_f, (D_MODEL, D_MODEL))((resid, branch), (_row(small["ffn_norm_g"][layer]),), ())
    out = make_ffn(tag + "_ffn", S)(hf, w_up, carriers["up32"], carriers["up16"], w_down, carriers["down32"], carriers["down16"],
                                   small["ffn_conv_w"][layer], _row(small["ffn_conv_b"][layer]))
    return x, out


BIG = {"w_in_even": 2, "w_out_even": 1, "w_in_odd": 2, "w_out_odd": 1, "ffn_w_up": 2, "ffn_w_down": 1}
SMALL_SHARDED = {"gla_wa2": 3, "gla_ba": 2, "hgrn_lb_logits": 2, "ret_norm_g": 1, "s5_d": 1, "s5_glu_w": 1, "s5_glu_b": 1,
                 "ffn_conv_w": 2}
REPLICATED = ("mix_norm_g", "ffn_norm_g", "final_norm_g", "gla_norm_g", "hgrn_norm_g", "s5_lam_re", "s5_lam_im", "s5_log_dt",
              "s5_b_re", "s5_b_im", "s5_c_re", "s5_c_im", "ffn_conv_b")
WEIGHTS = ("mix_norm_g", "ffn_norm_g", "final_norm_g", "w_in_even", "w_out_even", "gla_wa2", "gla_ba", "gla_norm_g",
           "hgrn_lb_logits", "hgrn_norm_g", "w_in_odd", "w_out_odd", "ret_norm_g", "s5_lam_re", "s5_lam_im", "s5_log_dt",
           "s5_b_re", "s5_b_im", "s5_c_re", "s5_c_im", "s5_d", "s5_glu_w", "s5_glu_b", "ffn_w_up", "ffn_conv_w", "ffn_conv_b",
           "ffn_w_down")
PACK_COLS = LANES
MIXER_SMALL = (("mix_norm_g", "hgrn_lb_logits", "gla_wa2", "gla_ba", "gla_norm_g", "hgrn_norm_g"),
               ("mix_norm_g", "ret_norm_g", "s5_lam_re", "s5_lam_im", "s5_log_dt", "s5_b_re", "s5_b_im", "s5_c_re", "s5_c_im",
                "s5_d", "s5_glu_w", "s5_glu_b"))
FFN_SMALL = ("ffn_norm_g", "ffn_conv_w", "ffn_conv_b")


def _unshard(g, axis):
    t = jnp.moveaxis(g, 0, axis)
    return t.reshape(t.shape[:axis] + (t.shape[axis] * t.shape[axis + 1],) + t.shape[axis + 2:])


def _to_blocks(full, axis):
    t = full.reshape(full.shape[:axis] + (N_DEV, full.shape[axis] // N_DEV) + full.shape[axis + 1:])
    return jnp.moveaxis(t, axis, 0)


def _pack_rows(shape):
    return -(-int(np.prod(shape)) // (SUBLANES * PACK_COLS)) * SUBLANES


def _pack(arrs):
    parts = []
    for a in arrs:
        rows = _pack_rows(a.shape)
        parts.append(jnp.pad(a.reshape(-1), (0, rows * PACK_COLS - a.size)).reshape(rows, PACK_COLS))
    return jnp.concatenate(parts, axis=0)


def _unpack(packed, shapes):
    lead = packed.shape[:-2]
    out, r = [], 0
    for s in shapes:
        rows, n = _pack_rows(s), int(np.prod(s))
        piece = packed[..., r:r + rows, :].reshape(lead + (rows * PACK_COLS,))
        out.append(piece[..., :n].reshape(lead + tuple(s)))
        r += rows
    return out


def _flat2d(a):
    return a.reshape(-1, a.shape[-1])


def kernel(x, mix_norm_g, ffn_norm_g, final_norm_g, w_in_even, w_out_even, gla_wa2, gla_ba, gla_norm_g, hgrn_lb_logits, hgrn_norm_g, w_in_odd, w_out_odd, ret_norm_g, s5_lam_re, s5_lam_im, s5_log_dt, s5_b_re, s5_b_im, s5_c_re, s5_c_im, s5_d, s5_glu_w, s5_glu_b, ffn_w_up, ffn_conv_w, ffn_conv_b, ffn_w_down, loss_target, m_mix_norm_g, m_ffn_norm_g, m_final_norm_g, m_w_in_even, m_w_out_even, m_gla_wa2, m_gla_ba, m_gla_norm_g, m_hgrn_lb_logits, m_hgrn_norm_g, m_w_in_odd, m_w_out_odd, m_ret_norm_g, m_s5_lam_re, m_s5_lam_im, m_s5_log_dt, m_s5_b_re, m_s5_b_im, m_s5_c_re, m_s5_c_im, m_s5_d, m_s5_glu_w, m_s5_glu_b, m_ffn_w_up, m_ffn_conv_w, m_ffn_conv_b, m_ffn_w_down, v_mix_norm_g, v_ffn_norm_g, v_final_norm_g, v_w_in_even, v_w_out_even, v_gla_wa2, v_gla_ba, v_gla_norm_g, v_hgrn_lb_logits, v_hgrn_norm_g, v_w_in_odd, v_w_out_odd, v_ret_norm_g, v_s5_lam_re, v_s5_lam_im, v_s5_log_dt, v_s5_b_re, v_s5_b_im, v_s5_c_re, v_s5_c_im, v_s5_d, v_s5_glu_w, v_s5_glu_b, v_ffn_w_up, v_ffn_conv_w, v_ffn_conv_b, v_ffn_w_down):
    args = locals()
    w = {n: args[n] for n in WEIGHTS}
    m = {n: args["m_" + n] for n in WEIGHTS}
    v = {n: args["v_" + n] for n in WEIGHTS}
    Bl, S, D = x.shape
    T = Bl * S
    ix, iy, ic = lax.axis_index("x"), lax.axis_index("y"), lax.axis_index("c")
    me = 4 * ix + 2 * iy + ic

    xt = x.reshape(T, D)
    me1 = me.reshape(1).astype(jnp.int32)
    stages = []
    for layer in range(DEPTH):
        j = layer // 2
        kin, kout = ("w_in_even", "w_out_even") if layer % 2 == 0 else ("w_in_odd", "w_out_odd")
        stages.append((mixer_stage, layer, ("in", "out"), ((kin, j, True), (kout, j, False)), MIXER_SMALL[layer % 2]))
        stages.append((ffn_stage, layer, ("up", "down"), (("ffn_w_up", layer, True), ("ffn_w_down", layer, False)), FFN_SMALL))

    gather, after = [], xt
    for s, (_, _, _, projs, _) in enumerate(stages):
        handle = split_start(f"gather{s}_start", "gather", [w[n][l].astype(BF16) for n, l, _ in projs], None, after)
        gather.append(handle)
        after = handle[-1]
    sm_names = list(SMALL_SHARDED)
    (sm_all8,) = all_gather("gather_small", [_pack([w[n] for n in sm_names])])
    small = {n: _unshard(t, SMALL_SHARDED[n]) for n, t in zip(sm_names, _unpack(sm_all8, [w[n].shape for n in sm_names]))}
    small.update({n: w[n] for n in REPLICATED})
    small["mix_norm_g"] = small["mix_norm_g"] + after[0, 0]

    resid, branch, pulls = xt, None, []
    second = [None] * len(stages)

    def second_level(s, after):
        lands = split_wait(f"gather{s}_wait", "gather", gather[s], after)
        second[s] = split_start(f"forward{s}_start", "forward", [], lands, after)

    for s, (fn, layer, keys, projs, sm_keys) in enumerate(stages):
        here = lax.stop_gradient(resid)
        if second[s] is None:
            second_level(s, here)
        lands = split_wait(f"forward{s}_wait", "forward", second[s], second[s][-1])
        sm = {n: small[n] for n in sm_keys}
        if 1 <= s < len(stages) - 1:
            second_level(s + 1, here)
            norm = "mix_norm_g" if fn is mixer_stage else "ffn_norm_g"
            sm[norm] = sm[norm] + second[s + 1][-1][0, 0]
        full, carriers = [], {}
        for key, land, (n, l, col) in zip(keys, lands, projs, strict=True):
            full.append(assemble_weight(f"weight{s}_{key}", land, w[n][l].astype(BF16), me1, col))
            carriers[key + "32"] = jnp.zeros(land.shape, F32)
            carriers[key + "16"] = jnp.zeros(land.shape, BF16)
        run = functools.partial(fn, layer, S)
        if branch is None:
            (resid, branch), pull = jax.vjp(lambda r, c, p, run=run, full=full: run(r, None, full[0], full[1], c, p), resid, carriers, sm)
        else:
            (resid, branch), pull = jax.vjp(lambda r, b, c, p, run=run, full=full: run(r, b, full[0], full[1], c, p), resid, branch, carriers, sm)
        pulls.append(pull)

    loss_acc, dxf, dgf = loss_head(resid, branch, small["final_norm_g"].reshape(1, D), loss_target.reshape(T, D))
    loss = lax.psum(loss_acc[0, 0], ("x", "y", "c"))
    g_small = {"final_norm_g": dgf.reshape(D)}
    d_resid, d_branch, token = dxf, dxf, None
    scatter, own32 = [None] * len(stages), [None] * len(stages)
    for s in reversed(range(len(stages))):
        if token is not None:
            d_resid = lax.dynamic_update_slice(d_resid, d_resid[:SUBLANES, :LANES] + token, (0, 0))
        if s == 0:
            d_resid, dcar, dsm = pulls[s]((d_resid, d_branch))
        else:
            d_resid, d_branch, dcar, dsm = pulls[s]((d_resid, d_branch))
        for n, g in dsm.items():
            g_small[n] = g_small[n] + g if n in g_small else g
        keys = stages[s][2]
        own32[s] = [dcar[k + "32"] for k in keys]
        scatter[s] = split_start(f"grads{s}_start", "scatter", [dcar[k + "16"] for k in keys], None, d_resid)
        token = scatter[s][-1]
    dx = d_resid

    out = {}
    sm_all = sm_names + list(REPLICATED)
    (g_all,) = all_gather("gather_small_grads", [_pack([g_small[n] for n in sm_all])])
    g_sum = _unpack(ordered_sum("sum_small_grads", g_all), [g_small[n].shape for n in sm_all])
    g_loc = []
    for n, g in zip(sm_all, g_sum):
        if n in SMALL_SHARDED:
            ax = SMALL_SHARDED[n]
            size = w[n].shape[ax]
            g = lax.dynamic_slice_in_dim(g, me * size, size, axis=ax)
        g_loc.append(g)
    shapes = [w[n].shape for n in sm_all]
    res = adamw("adamw_small", _pack([w[n] for n in sm_all]), _pack([m[n] for n in sm_all]), _pack([v[n] for n in sm_all]), _pack(g_loc))
    last_small = res[0]
    res = [g_loc] + [_unpack(r, shapes) for r in res]
    for k, n in enumerate(sm_all):
        out[n] = [r[k] for r in res]

    chain, behind = {}, last_small
    for s in reversed(range(len(stages))):
        lands = split_wait(f"grads{s}_wait", "scatter", scatter[s], behind if s == 0 else dx)
        for own, land, (n, l, _) in zip(own32[s], lands, stages[s][3], strict=True):
            chain[n] = adamw_sharded(f"adamw_{n}_{l}", l, _flat2d(w[n]), _flat2d(m[n]), _flat2d(v[n]), own, land, me1, chain.get(n))
            behind = chain[n][0]
    for n in BIG:
        out[n] = [t.reshape(w[n].shape) for t in chain[n]]

    grads, deltas, new_m, new_v = ([out[n][k] for n in WEIGHTS] for k in range(4))
    return (loss, dx.reshape(Bl, S, D), *grads, *deltas, *new_m, *new_v)
```

```python
import functools
import math

import numpy as np
import jax
import jax.numpy as jnp
from jax import lax
from jax.experimental import pallas as pl
from jax.experimental.pallas import tpu as pltpu

F32 = jnp.float32
BF16 = jnp.bfloat16
HIGHEST = lax.Precision.HIGHEST
MESH = pl.DeviceIdType.MESH

D_MODEL = 1024
DEPTH = 4
N_EVEN = 2
N_ODD = 2
GLA_HEADS, GLA_DK, GLA_DV, GLA_RANK, GLA_GATE_NORM = 4, 64, 128, 16, 16.0
HGRN_HEADS, HGRN_DK, HGRN_DV, HGRN_MIN_F = 4, 64, 128, 1e-20
RET_HEADS, RET_DK, RET_DV = 4, 128, 192
ROPE_BASE = 10000.0
S5_WIDTH, S5_GROUP_CH, S5_GROUPS, S5_STATE = 256, 16, 16, 64
S5_N = S5_GROUPS * S5_STATE
FFN_DIM = 2816
EPS = 1e-6
EVEN_IN = 3360
ODD_IN = 2816
ADAM_LR, ADAM_B1, ADAM_B2, ADAM_EPS, ADAM_WD, ADAM_STEP = 0.001, 0.9, 0.999, 1e-08, 0.01, 10

N_DEV = 8
VMEM_LIMIT_BYTES = 56 * 1024 * 1024
ROW_TILE = 256
SCAN_CHUNK = 64
S5_SEGMENTS = 8
LANES = 128


def _params(*sem):
    return pltpu.CompilerParams(dimension_semantics=sem, vmem_limit_bytes=VMEM_LIMIT_BYTES)


def _divisor_tile(n, cap):
    best = None
    for t in range(LANES, min(n, cap) + 1, LANES):
        if n % t == 0:
            best = t
    return best if best is not None else n


def _mm_nn(name, x, w):
    M, K = x.shape
    N = w.shape[1]
    tn = _divisor_tile(N, 2048) if K * N * 2 > 8 * 2**20 else N
    tm = 256 if tn * 4 * 512 > 6 * 2**20 else 512
    assert M % tm == 0 and N % tn == 0

    def body(x_ref, w_ref, o_ref):
        o_ref[...] = jnp.dot(x_ref[...].astype(BF16), w_ref[...], preferred_element_type=F32)

    return pl.pallas_call(
        body, name=name, grid=(N // tn, M // tm),
        in_specs=[pl.BlockSpec((tm, K), lambda j, i: (i, 0)), pl.BlockSpec((K, tn), lambda j, i: (0, j))],
        out_specs=pl.BlockSpec((tm, tn), lambda j, i: (i, j)),
        out_shape=jax.ShapeDtypeStruct((M, N), F32),
        compiler_params=_params("parallel", "parallel"),
    )(x, w)


def _mm_nt(name, dy, w):
    M, N = dy.shape
    K = w.shape[0]
    tk = _divisor_tile(K, 1024) if K * N * 2 > 8 * 2**20 else K
    tm = 256 if N >= 4096 else 512
    assert M % tm == 0 and K % tk == 0

    def body(dy_ref, w_ref, o_ref):
        o_ref[...] = lax.dot_general(dy_ref[...].astype(BF16), w_ref[...], (((1,), (1,)), ((), ())),
                                     preferred_element_type=F32)

    return pl.pallas_call(
        body, name=name, grid=(K // tk, M // tm),
        in_specs=[pl.BlockSpec((tm, N), lambda j, i: (i, 0)), pl.BlockSpec((tk, N), lambda j, i: (j, 0))],
        out_specs=pl.BlockSpec((tm, tk), lambda j, i: (i, j)),
        out_shape=jax.ShapeDtypeStruct((M, K), F32),
        compiler_params=_params("parallel", "parallel"),
    )(dy, w)


MM_TN_VMEM_BUDGET = 46 * 2**20


def _pad_lanes(n):
    return -(-n // LANES) * LANES


def _mm_tn(name, x, dy, nblk, want16):
    M, K = x.shape
    N = dy.shape[1]
    n = N // nblk
    xb, yb = x.dtype.itemsize, dy.dtype.itemsize
    best = None
    for tk in [t for t in range(LANES, K + 1, LANES) if K % t == 0]:
        for tm in (512, 256):
            out_bytes = nblk * tk * _pad_lanes(n) * (6 if want16 else 4)
            vmem = 2 * out_bytes + 2 * tm * tk * xb + 2 * tm * _pad_lanes(N) * yb
            traffic = (K // tk) * M * N * yb + M * K * xb
            if vmem <= MM_TN_VMEM_BUDGET and M % tm == 0 and (best is None or (traffic, -tm) < best[0]):
                best = ((traffic, -tm), tk, tm)
    _, tk, tm = best
    last = M // tm - 1

    def body(x_ref, dy_ref, o32_ref, *o16_ref):
        m = pl.program_id(1)

        @pl.when(m == 0)
        def _():
            o32_ref[...] = jnp.zeros_like(o32_ref)

        dyv = dy_ref[...].astype(BF16)
        if nblk == 1:
            o32_ref[0] += lax.dot_general(x_ref[...].astype(BF16), dyv, (((0,), (0,)), ((), ())), preferred_element_type=F32)
        else:
            xt = x_ref[...].astype(F32).T.astype(BF16)
            for d in range(nblk):
                o32_ref[d] += jnp.dot(xt, dyv[:, d * n:(d + 1) * n], preferred_element_type=F32)
        if want16:
            @pl.when(m == last)
            def _():
                o16_ref[0][...] = o32_ref[...].astype(BF16)

    blk = pl.BlockSpec((nblk, tk, n), lambda a, m: (0, a, 0))
    return pl.pallas_call(
        body, name=name, grid=(K // tk, M // tm),
        in_specs=[pl.BlockSpec((tm, tk), lambda a, m: (m, a)), pl.BlockSpec((tm, N), lambda a, m: (m, 0))],
        out_specs=[blk, blk] if want16 else [blk],
        out_shape=[jax.ShapeDtypeStruct((nblk, K, n), F32)] + ([jax.ShapeDtypeStruct((nblk, K, n), BF16)] if want16 else []),
        compiler_params=_params("parallel", "arbitrary"),
    )(x, dy)


def _weight_grads(name, x, dy, col_sharded):
    if col_sharded:
        return _mm_tn(name, x, dy, N_DEV, True)
    d32, d16 = _mm_tn(name, x, dy, 1, True)
    K, N = d32.shape[1:]
    return d32.reshape(N_DEV, K // N_DEV, N), d16.reshape(N_DEV, K // N_DEV, N)


def make_mm(name, col_sharded=False):
    @jax.custom_vjp
    def mm(x, w16, c32, c16):
        return _mm_nn(name + "_fwd", x, w16)

    def fwd(x, w16, c32, c16):
        return _mm_nn(name + "_fwd", x, w16), (x, w16)

    def bwd(res, dy):
        x, w16 = res
        d32, d16 = _weight_grads(name + "_dw", x, dy, col_sharded)
        return _mm_nt(name + "_dx", dy, w16), jnp.zeros_like(w16), d32, d16

    mm.defvjp(fwd, bwd)
    return mm


def make_mm_f32w(name):
    @jax.custom_vjp
    def mm(x, w):
        return _mm_nn(name + "_fwd", x, w.astype(BF16))

    def fwd(x, w):
        w16 = w.astype(BF16)
        return _mm_nn(name + "_fwd", x, w16), (x, w16)

    def bwd(res, dy):
        x, w16 = res
        return _mm_nt(name + "_dx", dy, w16), _mm_tn(name + "_dw", x, dy, 1, False)[0][0]

    mm.defvjp(fwd, bwd)
    return mm


def _row_specs(rows, params, consts, tile):
    specs = [pl.BlockSpec((tile, r.shape[1]), lambda i: (i, 0)) for r in rows]
    specs += [pl.BlockSpec(p.shape, lambda i: (0, 0)) for p in params]
    specs += [pl.BlockSpec((tile, c.shape[1]), lambda i, n=c.shape[0] // tile: (i % n, 0)) for c in consts]
    return specs


def _row_fwd(name, f, out_widths, tile, rows, params, consts):
    T = rows[0].shape[0]
    nr, npar, ncon = len(rows), len(params), len(consts)

    def body(*refs):
        r = tuple(x[...] for x in refs[:nr])
        p = tuple(x[...] for x in refs[nr:nr + npar])
        c = tuple(x[...] for x in refs[nr + npar:nr + npar + ncon])
        outs = f(r, p, c)
        for o_ref, o in zip(refs[nr + npar + ncon:], outs, strict=True):
            o_ref[...] = o

    return pl.pallas_call(
        body, name=name + "_fwd", grid=(T // tile,),
        in_specs=_row_specs(rows, params, consts, tile),
        out_specs=[pl.BlockSpec((tile, w), lambda i: (i, 0)) for w in out_widths],
        out_shape=[jax.ShapeDtypeStruct((T, w), F32) for w in out_widths],
        compiler_params=_params("parallel"),
    )(*rows, *params, *consts)


def _row_bwd(name, f, out_widths, tile, rows, params, consts, gouts, dr_dtype=F32):
    T = rows[0].shape[0]
    nr, npar, ncon, nout = len(rows), len(params), len(consts), len(out_widths)

    def body(*refs):
        r = tuple(x[...] for x in refs[:nr])
        p = tuple(x[...] for x in refs[nr:nr + npar])
        c = tuple(x[...] for x in refs[nr + npar:nr + npar + ncon])
        k = nr + npar + ncon
        g = tuple(x[...] for x in refs[k:k + nout])
        dr_refs = refs[k + nout:k + nout + nr]
        dp_refs = refs[k + nout + nr:]
        _, vjp = jax.vjp(lambda r_, p_: tuple(f(r_, p_, c)), r, p)
        dr, dp = vjp(g)
        for ref, val in zip(dr_refs, dr, strict=True):
            ref[...] = val.astype(ref.dtype)
        if npar:
            @pl.when(pl.program_id(0) == 0)
            def _():
                for ref in dp_refs:
                    ref[...] = jnp.zeros_like(ref)

            for ref, val in zip(dp_refs, dp, strict=True):
                ref[...] += val

    outs = pl.pallas_call(
        body, name=name + "_bwd", grid=(T // tile,),
        in_specs=_row_specs(rows, params, consts, tile) + [pl.BlockSpec((tile, w), lambda i: (i, 0)) for w in out_widths],
        out_specs=[pl.BlockSpec((tile, r.shape[1]), lambda i: (i, 0)) for r in rows]
        + [pl.BlockSpec(p.shape, lambda i: (0, 0)) for p in params],
        out_shape=[jax.ShapeDtypeStruct(r.shape, dr_dtype) for r in rows] + [jax.ShapeDtypeStruct(p.shape, F32) for p in params],
        compiler_params=_params("arbitrary"),
    )(*rows, *params, *consts, *gouts)
    return tuple(outs[:nr]), tuple(outs[nr:])


def make_proj_stage(name, f, out_widths, tile=ROW_TILE):
    def run(x, w16, params, consts):
        p = _mm_nn(name + "_mm", x, w16)
        return p, tuple(_row_fwd(name, f, out_widths, tile, (p,), params, consts))

    @jax.custom_vjp
    def op(x, w16, c32, c16, params, consts):
        return run(x, w16, params, consts)[1]

    def fwd(x, w16, c32, c16, params, consts):
        p, outs = run(x, w16, params, consts)
        return outs, (x, w16, p, params, consts)

    def bwd(res, g):
        x, w16, p, params, consts = res
        (dp,), dparams = _row_bwd(name, f, out_widths, tile, (p,), params, consts, tuple(g), dr_dtype=BF16)
        d32, d16 = _weight_grads(name + "_dw", x, dp, True)
        return _mm_nt(name + "_dx", dp, w16), jnp.zeros_like(w16), d32, d16, dparams, tuple(jnp.zeros_like(c) for c in consts)

    op.defvjp(fwd, bwd)
    return op


def make_rowop(name, f, out_widths, tile=ROW_TILE):
    @jax.custom_vjp
    def op(rows, params, consts):
        return tuple(_row_fwd(name, f, out_widths, tile, rows, params, consts))

    def fwd(rows, params, consts):
        return op(rows, params, consts), (rows, params, consts)

    def bwd(res, g):
        rows, params, consts = res
        dr, dp = _row_bwd(name, f, out_widths, tile, rows, params, consts, tuple(g))
        return dr, dp, tuple(jnp.zeros_like(c) for c in consts)

    op.defvjp(fwd, bwd)
    return op


def _rms(x, g):
    return x * lax.rsqrt(jnp.mean(x * x, axis=-1, keepdims=True) + EPS) * g


def _silu(x):
    return x * jax.nn.sigmoid(x)


def _bdot(a, b):
    return jnp.dot(a.astype(BF16), b.astype(BF16), preferred_element_type=F32)


def norm_f(rows, params, consts):
    return (_rms(rows[0], params[0]),)


def addnorm_f(rows, params, consts):
    x = rows[0] + rows[1]
    return x, _rms(x, params[0])


EVEN_GLA_END = 1568


def even_prep_f(rows, params, consts):
    (p,) = rows
    wa2f, wa2b, baf, bab, lbf, lbb = params
    gq = p[:, 0:256]
    gk = p[:, 256:512] * (GLA_DK ** -0.5)
    gv = p[:, 512:1024]
    gr = p[:, 1024:1536]
    glaf = jax.nn.log_sigmoid(_bdot(p[:, 1536:1552], wa2f) + baf) / GLA_GATE_NORM
    glab = jax.nn.log_sigmoid(_bdot(p[:, 1552:1568], wa2b) + bab) / GLA_GATE_NORM
    o = EVEN_GLA_END
    hq = _silu(p[:, o:o + 256])

    def gate(z, lb):
        f = lb + (1.0 - lb) * jax.nn.sigmoid(z)
        return jnp.log(jnp.maximum(f, HGRN_MIN_F)), (1.0 - lb) * jax.nn.sigmoid(-z)

    hlaf, hkf = gate(p[:, o + 256:o + 512], lbf)
    hlab, hkb = gate(p[:, o + 512:o + 768], lbb)
    hv = p[:, o + 768:o + 1280]
    hg = p[:, o + 1280:o + 1792]
    return gq, gk, gv, gq, gk, gv, glaf, glab, gr, hq, hkf, hv, hq, hkb, hv, hlaf, hlab, hg


EVEN_PREP_WIDTHS = (256, 256, 512, 256, 256, 512, 256, 256, 512, 256, 256, 512, 256, 256, 512, 256, 256, 512)


def _head_rms(o, g, heads, d):
    parts = []
    for h in range(heads):
        seg = o[:, h * d:(h + 1) * d]
        parts.append(seg * lax.rsqrt(jnp.mean(seg * seg, axis=-1, keepdims=True) + EPS))
    return jnp.concatenate(parts, axis=1) * g


def even_post_f(rows, params, consts):
    of, ob, hof, hob, gr, hg = rows
    a = _head_rms(of + ob, params[0], GLA_HEADS, GLA_DV) * _silu(gr)
    b = _head_rms(hof + hob, params[1], HGRN_HEADS, HGRN_DV) * _silu(hg)
    return (jnp.concatenate([a, b], axis=1),)


@jax.custom_vjp
def _roll_half(x):
    return pltpu.roll(x, RET_DK // 2, 1)


_roll_half.defvjp(lambda x: (_roll_half(x), None), lambda _, g: (_roll_half(g),))


def odd_prep_f(rows, params, consts):
    (p,) = rows
    cosf, sinf = consts

    def rot(t):
        parts = []
        for h in range(RET_HEADS):
            th = t[:, h * RET_DK:(h + 1) * RET_DK]
            parts.append(th * cosf + _roll_half(th) * sinf)
        return jnp.concatenate(parts, axis=1)

    rq = rot(p[:, 0:512])
    rk = rot(p[:, 512:1024]) * (RET_DK ** -0.5)
    rv = p[:, 1024:1792]
    return rq, rk, rv, rq, rk, rv, p[:, 1792:2560], p[:, 2560:2816]


ODD_PREP_WIDTHS = (512, 512, 768, 512, 512, 768, 768, 256)


def ret_post_f(rows, params, consts):
    of, ob, rg = rows
    o = of + ob
    parts = []
    for h in range(RET_HEADS):
        seg = o[:, h * RET_DV:(h + 1) * RET_DV]
        c = seg - jnp.mean(seg, axis=-1, keepdims=True)
        parts.append(c * lax.rsqrt(jnp.mean(c * c, axis=-1, keepdims=True) + EPS))
    return (jnp.concatenate(parts, axis=1) * params[0] * _silu(rg),)


def s5_post_f(rows, params, consts):
    h0r, h0i, h1r, h1i, u = rows
    c_re, c_im, d_skip, glu_w, glu_b = params
    hr = h0r + h1r
    hi = h0i + h1i
    y = _bdot(hr, c_re) - _bdot(hi, c_im) + d_skip * u
    g = jax.nn.gelu(y)
    return (g * jax.nn.sigmoid(_bdot(g, glu_w) + glu_b),)


def loss_head(x, r, g, target, tile=ROW_TILE):
    T, D = x.shape

    def body(x_ref, r_ref, g_ref, t_ref, loss_ref, dx_ref, dg_ref):
        t = t_ref[...]

        def lf(xv, gv):
            e = _rms(xv, gv) - t
            row = jnp.sum(e * e, axis=-1, keepdims=True) * (0.5 / D)
            return jnp.sum(row, axis=0, keepdims=True)

        l, vjp = jax.vjp(lf, x_ref[...] + r_ref[...], g_ref[...])
        dx, dg = vjp(jnp.ones((1, 1), F32))
        dx_ref[...] = dx

        @pl.when(pl.program_id(0) == 0)
        def _():
            loss_ref[...] = jnp.zeros_like(loss_ref)
            dg_ref[...] = jnp.zeros_like(dg_ref)

        loss_ref[...] += jnp.broadcast_to(l, loss_ref.shape)
        dg_ref[...] += dg

    row = pl.BlockSpec((tile, D), lambda i: (i, 0))
    vec = pl.BlockSpec((1, D), lambda i: (0, 0))
    return pl.pallas_call(
        body, name="loss_head", grid=(T // tile,),
        in_specs=[row, row, vec, row],
        out_specs=[pl.BlockSpec((1, LANES), lambda i: (0, 0)), row, vec],
        out_shape=[jax.ShapeDtypeStruct((1, LANES), F32), jax.ShapeDtypeStruct((T, D), F32), jax.ShapeDtypeStruct((1, D), F32)],
        compiler_params=_params("arbitrary"),
    )(x, r, g, target)


SUBLANES = 8


def _halo_specs(width, tile, T):
    n8 = tile // SUBLANES
    last = T // SUBLANES - 1
    return [pl.BlockSpec((tile, width), lambda i: (i, 0)),
            pl.BlockSpec((SUBLANES, width), lambda i: (jnp.maximum(i * n8 - 1, 0), 0)),
            pl.BlockSpec((SUBLANES, width), lambda i: (jnp.minimum((i + 1) * n8, last), 0))]


def _shift_rows(x, prev_row, next_row, tile):
    row = lax.broadcasted_iota(jnp.int32, (tile, 1), 0)
    down = jnp.where(row == 0, prev_row, pltpu.roll(x, 1, 0))
    up = jnp.where(row == tile - 1, next_row, pltpu.roll(x, tile - 1, 0))
    return down, up


def _conv_fwd(name, u, cw, cb, S, tile):
    T, F2 = u.shape
    F = F2 // 2
    per_seq = S // tile

    def body(u_ref, up_ref, un_ref, cw_ref, cb_ref, g_ref):
        pos = pl.program_id(0) % per_seq
        uv = u_ref[...]
        prev_row = jnp.where(pos == 0, 0.0, up_ref[SUBLANES - 1:SUBLANES, :])
        next_row = jnp.where(pos == per_seq - 1, 0.0, un_ref[0:1, :])
        down, up = _shift_rows(uv, prev_row, next_row, tile)
        c = _conv_taps(down, uv, up, cw_ref, cb_ref)
        g_ref[...] = (_silu(c[:, :F]) * c[:, F:]).astype(BF16)

    return pl.pallas_call(
        body, name=name + "_fwd", grid=(T // tile,),
        in_specs=_halo_specs(F2, tile, T) + [pl.BlockSpec((3, F2), lambda i: (0, 0)), pl.BlockSpec((1, F2), lambda i: (0, 0))],
        out_specs=pl.BlockSpec((tile, F), lambda i: (i, 0)),
        out_shape=jax.ShapeDtypeStruct((T, F), BF16),
        compiler_params=_params("parallel"),
    )(u, u, u, cw, cb)


def _conv_taps(down, mid, up, cw_ref, cb_ref):
    c = cb_ref[...] + down * cw_ref[0:1, :]
    c = c + mid * cw_ref[1:2, :]
    return c + up * cw_ref[2:3, :]


def _conv_bwd(name, u, dg, cw, cb, S, tile):
    T, F2 = u.shape
    F = F2 // 2
    per_seq = S // tile

    def dact(cv, dgv):
        a, v = cv[:, :F], cv[:, F:]
        sg = jax.nn.sigmoid(a)
        return jnp.concatenate([dgv * v * (sg * (1.0 + a * (1.0 - sg))), dgv * (a * sg)], axis=1)

    def body(u_ref, up_ref, un_ref, g_ref, gp_ref, gn_ref, cw_ref, cb_ref, du_ref, dw0_ref, dw1_ref, dw2_ref, db_ref):
        i = pl.program_id(0)
        pos = i % per_seq
        first, last = pos == 0, pos == per_seq - 1
        lo, hi = slice(SUBLANES - 1, SUBLANES), slice(0, 1)
        uv = u_ref[...]
        u_m1, u_p1 = jnp.where(first, 0.0, up_ref[lo, :]), jnp.where(last, 0.0, un_ref[hi, :])
        u_dn, u_up = _shift_rows(uv, u_m1, u_p1, tile)
        dc = dact(_conv_taps(u_dn, uv, u_up, cw_ref, cb_ref), g_ref[...])
        c_m1 = _conv_taps(up_ref[SUBLANES - 2:SUBLANES - 1, :], u_m1, u_ref[0:1, :], cw_ref, cb_ref)
        c_p1 = _conv_taps(u_ref[tile - 1:tile, :], u_p1, un_ref[1:2, :], cw_ref, cb_ref)
        dc_prev = jnp.where(first, 0.0, dact(c_m1, gp_ref[lo, :]))
        dc_next = jnp.where(last, 0.0, dact(c_p1, gn_ref[hi, :]))
        dc_dn, dc_up = _shift_rows(dc, dc_prev, dc_next, tile)
        du = dc_up * cw_ref[0:1, :]
        du = du + dc * cw_ref[1:2, :]
        du_ref[...] = (du + dc_dn * cw_ref[2:3, :]).astype(BF16)

        @pl.when(i == 0)
        def _():
            for ref in (dw0_ref, dw1_ref, dw2_ref, db_ref):
                ref[...] = jnp.zeros_like(ref)

        dw0_ref[...] += jnp.sum(dc * u_dn, axis=0, keepdims=True)
        dw1_ref[...] += jnp.sum(dc * uv, axis=0, keepdims=True)
        dw2_ref[...] += jnp.sum(dc * u_up, axis=0, keepdims=True)
        db_ref[...] += jnp.sum(dc, axis=0, keepdims=True)

    vec = pl.BlockSpec((1, F2), lambda i: (0, 0))
    du, dw0, dw1, dw2, db = pl.pallas_call(
        body, name=name + "_bwd", grid=(T // tile,),
        in_specs=_halo_specs(F2, tile, T) + _halo_specs(F, tile, T) + [pl.BlockSpec((3, F2), lambda i: (0, 0)), vec],
        out_specs=[pl.BlockSpec((tile, F2), lambda i: (i, 0)), vec, vec, vec, vec],
        out_shape=[jax.ShapeDtypeStruct((T, F2), BF16)] + [jax.ShapeDtypeStruct((1, F2), F32)] * 4,
        compiler_params=_params("arbitrary"),
    )(u, u, u, dg, dg, dg, cw, cb)
    return du, jnp.concatenate([dw0, dw1, dw2], axis=0), db


def make_ffn(name, S):
    def run(x, wup16, wdn16, cw, cb):
        u = _mm_nn(name + "_up", x, wup16)
        g16 = _conv_fwd(name + "_conv", u, cw, cb, S, ROW_TILE)
        return u, g16, _mm_nn(name + "_down", g16, wdn16)

    @jax.custom_vjp
    def op(x, wup16, cu32, cu16, wdn16, cd32, cd16, cw, cb):
        return run(x, wup16, wdn16, cw, cb)[2]

    def fwd(x, wup16, cu32, cu16, wdn16, cd32, cd16, cw, cb):
        u, g16, out = run(x, wup16, wdn16, cw, cb)
        return out, (x, wup16, wdn16, u, g16, cw, cb)

    def bwd(res, dout):
        x, wup16, wdn16, u, g16, cw, cb = res
        dg = _mm_nt(name + "_down_dx", dout, wdn16)
        dd32, dd16 = _weight_grads(name + "_down_dw", g16, dout, False)
        du16, dcw, dcb = _conv_bwd(name + "_conv", u, dg, cw, cb, S, ROW_TILE)
        du32, du16w = _weight_grads(name + "_up_dw", x, du16, True)
        return (_mm_nt(name + "_up_dx", du16, wup16), jnp.zeros_like(wup16), du32, du16w, jnp.zeros_like(wdn16), dd32, dd16, dcw, dcb)

    op.defvjp(fwd, bwd)
    return op


def _dot_nt(a, b):
    return lax.dot_general(a.astype(BF16), b.astype(BF16), (((1,), (1,)), ((), ())), preferred_element_type=F32)


def _dot_tn(a, b):
    return lax.dot_general(a.astype(BF16), b.astype(BF16), (((0,), (0,)), ((), ())), preferred_element_type=F32)


def _chunk_decays(la, direction, C, width, dk, const_lg):
    row = lax.broadcasted_iota(jnp.int32, (C, C), 0)
    col = lax.broadcasted_iota(jnp.int32, (C, C), 1)
    keep = (row >= col) if direction == 0 else (row <= col)
    ridx = lax.broadcasted_iota(jnp.int32, (C, 1), 0)
    if const_lg is None:
        cum = jnp.dot(keep.astype(F32), la, precision=HIGHEST, preferred_element_type=F32)
    else:
        lane_head = lax.broadcasted_iota(jnp.int32, (1, width), 1) // dk
        lg = jnp.zeros((1, width), F32)
        for h, val in enumerate(const_lg):
            lg = jnp.where(lane_head == h, val, lg)
        steps = (ridx + 1) if direction == 0 else (C - ridx)
        cum = steps.astype(F32) * lg
    exit_row = C - 1 if direction == 0 else 0
    mid = jnp.sum(jnp.where(ridx == C // 2, cum, 0.0), axis=0, keepdims=True)
    last = jnp.sum(jnp.where(ridx == exit_row, cum, 0.0), axis=0, keepdims=True)
    return keep, ridx == exit_row, cum, mid, last


def _scan_fwd(name, qkv, laf, lab, H, dk, dv, S, C, const_lg):
    qf, kf, vf, qb, kb, vb = qkv
    T = qf.shape[0]
    B, nc = T // S, S // C
    Wk, Wv = H * dk, H * dv
    learn = const_lg is None

    def body(*refs):
        @pl.when(pl.program_id(0) == 0)
        def _():
            refs[-1][...] = jnp.zeros_like(refs[-1])

        for b in range(B):
            one_sequence(*[r.at[b] for r in refs])

    def one_sequence(*refs):
        if learn:
            qf_r, qb_r, kf_r, kb_r, vf_r, vb_r, laf_r, lab_r, of_r, ob_r, sf_r, sb_r, st = refs
            las = (laf_r[...], lab_r[...])
        else:
            qf_r, qb_r, kf_r, kb_r, vf_r, vb_r, of_r, ob_r, sf_r, sb_r, st = refs
            las = (None, None)

        for d, (q_r, k_r, v_r, o_r, s_r) in enumerate(((qf_r, kf_r, vf_r, of_r, sf_r), (qb_r, kb_r, vb_r, ob_r, sb_r))):
            keep, _, cum, mid, last = _chunk_decays(las[d], d, C, Wk, dk, None if learn else const_lg[d])
            qe = q_r[...] * jnp.exp(cum - mid)
            ke = k_r[...] * jnp.exp(mid - cum)
            q_in = qe * jnp.exp(mid)
            k_out = ke * jnp.exp(last - mid)
            e_last = jnp.exp(last)
            vv = v_r[...]
            for h in range(H):
                ks, vs = slice(h * dk, (h + 1) * dk), slice(h * dv, (h + 1) * dv)
                a = jnp.where(keep, _dot_nt(qe[:, ks], ke[:, ks]), 0.0)
                state = st[d, h]
                o_r[:, vs] = _bdot(a, vv[:, vs]) + _dot_nt(q_in[:, ks], state)
                s_r[h * dv:(h + 1) * dv, :] = state
                st[d, h] = state * e_last[:, ks] + _dot_tn(vv[:, vs], k_out[:, ks])

    fpos = lambda c: c
    bpos = lambda c: nc - 1 - c
    kspec = lambda pos: pl.BlockSpec((B, C, Wk), lambda c: (0, pos(c), 0))
    vspec = lambda pos: pl.BlockSpec((B, C, Wv), lambda c: (0, pos(c), 0))
    sspec = lambda pos: pl.BlockSpec((B, None, Wv, dk), lambda c: (0, pos(c), 0, 0))
    seq = lambda t: t.reshape(B, S, t.shape[1])
    ins = [seq(t) for t in [qf, qb, kf, kb, vf, vb] + ([laf, lab] if learn else [])]
    in_specs = [kspec(fpos), kspec(bpos), kspec(fpos), kspec(bpos), vspec(fpos), vspec(bpos)] + ([kspec(fpos), kspec(bpos)] if learn else [])
    of, ob, sf, sb = pl.pallas_call(
        body, name=name + "_fwd", grid=(nc,), in_specs=in_specs,
        out_specs=[vspec(fpos), vspec(bpos), sspec(fpos), sspec(bpos)],
        out_shape=[jax.ShapeDtypeStruct((B, S, Wv), F32)] * 2 + [jax.ShapeDtypeStruct((B, nc, Wv, dk), F32)] * 2,
        scratch_shapes=[pltpu.VMEM((B, 2, H, dv, dk), F32)],
        compiler_params=_params("arbitrary"),
    )(*ins)
    return of.reshape(T, Wv), ob.reshape(T, Wv), sf, sb


def _scan_bwd(name, qkv, laf, lab, sf, sb, dof, dob, H, dk, dv, S, C, const_lg):
    qf, kf, vf, qb, kb, vb = qkv
    T = qf.shape[0]
    B, nc = T // S, S // C
    Wk, Wv = H * dk, H * dv
    learn = const_lg is None

    def body(*refs):
        @pl.when(pl.program_id(0) == 0)
        def _():
            refs[-1][...] = jnp.zeros_like(refs[-1])

        for b in range(B):
            one_sequence(*[r.at[b] for r in refs])

    def one_sequence(*refs):
        if learn:
            (qf_r, qb_r, kf_r, kb_r, vf_r, vb_r, laf_r, lab_r, sf_r, sb_r, dof_r, dob_r,
             dqf_r, dqb_r, dkf_r, dkb_r, dvf_r, dvb_r, dlaf_r, dlab_r, dst) = refs
            las, dlas = (laf_r[...], lab_r[...]), (dlaf_r, dlab_r)
        else:
            (qf_r, qb_r, kf_r, kb_r, vf_r, vb_r, sf_r, sb_r, dof_r, dob_r,
             dqf_r, dqb_r, dkf_r, dkb_r, dvf_r, dvb_r, dst) = refs
            las, dlas = (None, None), (None, None)

        groups = ((qf_r, kf_r, vf_r, sf_r, dof_r, dqf_r, dkf_r, dvf_r), (qb_r, kb_r, vb_r, sb_r, dob_r, dqb_r, dkb_r, dvb_r))
        for d, (q_r, k_r, v_r, s_r, do_r, dq_r, dk_r, dv_r) in enumerate(groups):
            keep, is_exit, cum, mid, last = _chunk_decays(las[d], d, C, Wk, dk, None if learn else const_lg[d])
            eq, ek = jnp.exp(cum - mid), jnp.exp(mid - cum)
            e_in, e_out, e_last = jnp.exp(mid), jnp.exp(last - mid), jnp.exp(last)
            qe, ke = q_r[...] * eq, k_r[...] * ek
            q_in, k_out = qe * e_in, ke * e_out
            vv, do = v_r[...], do_r[...]
            dqe_parts, dke_parts, dlast_parts = [], [], []
            for h in range(H):
                ks, vs = slice(h * dk, (h + 1) * dk), slice(h * dv, (h + 1) * dv)
                a = jnp.where(keep, _dot_nt(qe[:, ks], ke[:, ks]), 0.0)
                dp = jnp.where(keep, _dot_nt(do[:, vs], vv[:, vs]), 0.0)
                s_prev = s_r[h * dv:(h + 1) * dv, :]
                ds = dst[d, h]
                dk_out = _bdot(vv[:, vs], ds)
                dqe_parts.append(_bdot(dp, ke[:, ks]) + _bdot(do[:, vs], s_prev) * e_in[:, ks])
                dke_parts.append(_dot_tn(dp, qe[:, ks]) + dk_out * e_out[:, ks])
                dv_r[:, vs] = _dot_tn(a, do[:, vs]) + _dot_nt(k_out[:, ks], ds)
                if learn:
                    dlast_parts.append(jnp.sum(dk_out * k_out[:, ks], axis=0, keepdims=True)
                                       + jnp.sum(ds * s_prev, axis=0, keepdims=True) * e_last[:, ks])
                dst[d, h] = ds * e_last[:, ks] + _dot_tn(do[:, vs], q_in[:, ks])
            dqe = jnp.concatenate(dqe_parts, axis=1)
            dke = jnp.concatenate(dke_parts, axis=1)
            dq_r[...] = dqe * eq
            dk_r[...] = dke * ek
            if learn:
                dcum = dqe * qe - dke * ke + jnp.where(is_exit, jnp.concatenate(dlast_parts, axis=1), 0.0)
                dlas[d][...] = lax.dot_general(keep.astype(F32), dcum, (((0,), (0,)), ((), ())), precision=HIGHEST,
                                               preferred_element_type=F32)

    fpos = lambda c: nc - 1 - c
    bpos = lambda c: c
    kspec = lambda pos: pl.BlockSpec((B, C, Wk), lambda c: (0, pos(c), 0))
    vspec = lambda pos: pl.BlockSpec((B, C, Wv), lambda c: (0, pos(c), 0))
    sspec = lambda pos: pl.BlockSpec((B, None, Wv, dk), lambda c: (0, pos(c), 0, 0))
    seq = lambda t: t.reshape(B, S, t.shape[1])
    ins = [seq(t) for t in [qf, qb, kf, kb, vf, vb] + ([laf, lab] if learn else [])] + [sf, sb, seq(dof), seq(dob)]
    in_specs = ([kspec(fpos), kspec(bpos), kspec(fpos), kspec(bpos), vspec(fpos), vspec(bpos)]
                + ([kspec(fpos), kspec(bpos)] if learn else []) + [sspec(fpos), sspec(bpos), vspec(fpos), vspec(bpos)])
    out_specs = [kspec(fpos), kspec(bpos), kspec(fpos), kspec(bpos), vspec(fpos), vspec(bpos)] + ([kspec(fpos), kspec(bpos)] if learn else [])
    out_shape = ([jax.ShapeDtypeStruct((B, S, Wk), F32)] * 4 + [jax.ShapeDtypeStruct((B, S, Wv), F32)] * 2
                 + ([jax.ShapeDtypeStruct((B, S, Wk), F32)] * 2 if learn else []))
    outs = pl.pallas_call(
        body, name=name + "_bwd", grid=(nc,), in_specs=in_specs, out_specs=out_specs, out_shape=out_shape,
        scratch_shapes=[pltpu.VMEM((B, 2, H, dv, dk), F32)],
        compiler_params=_params("arbitrary"),
    )(*ins)
    return [t.reshape(T, t.shape[2]) for t in outs]


def make_scan(name, H, dk, dv, S, C=SCAN_CHUNK, const_lg=None):
    if const_lg is None:
        @jax.custom_vjp
        def op(qkv, laf, lab):
            return tuple(_scan_fwd(name, qkv, laf, lab, H, dk, dv, S, C, None)[:2])

        def fwd(qkv, laf, lab):
            of, ob, sf, sb = _scan_fwd(name, qkv, laf, lab, H, dk, dv, S, C, None)
            return (of, ob), (qkv, laf, lab, sf, sb)

        def bwd(res, g):
            qkv, laf, lab, sf, sb = res
            dqf, dqb, dkf, dkb, dvf, dvb, dlaf, dlab = _scan_bwd(name, qkv, laf, lab, sf, sb, g[0], g[1], H, dk, dv, S, C, None)
            return (dqf, dkf, dvf, dqb, dkb, dvb), dlaf, dlab
    else:
        @jax.custom_vjp
        def op(qkv):
            return tuple(_scan_fwd(name, qkv, None, None, H, dk, dv, S, C, const_lg)[:2])

        def fwd(qkv):
            of, ob, sf, sb = _scan_fwd(name, qkv, None, None, H, dk, dv, S, C, const_lg)
            return (of, ob), (qkv, sf, sb)

        def bwd(res, g):
            qkv, sf, sb = res
            dqf, dqb, dkf, dkb, dvf, dvb = _scan_bwd(name, qkv, None, None, sf, sb, g[0], g[1], H, dk, dv, S, C, const_lg)
            return ((dqf, dkf, dvf, dqb, dkb, dvb),)

    op.defvjp(fwd, bwd)
    return op


def _reorder_call(name, t, S, to_segments):
    T, w = t.shape
    n_it = S // S5_SEGMENTS

    def body(x_ref, o_ref):
        def step(i, carry):
            packed = pl.ds(pl.multiple_of(i * S5_SEGMENTS, S5_SEGMENTS), S5_SEGMENTS)
            spread = pl.ds(i, S5_SEGMENTS, stride=n_it)
            if to_segments:
                o_ref[packed, :] = x_ref[spread, :]
            else:
                o_ref[spread, :] = x_ref[packed, :]
            return carry

        lax.fori_loop(0, n_it, step, 0, unroll=8)

    blk = pl.BlockSpec((S, LANES), lambda b, j: (b, j))
    return pl.pallas_call(body, name=name, grid=(T // S, w // LANES), in_specs=[blk], out_specs=blk,
                          out_shape=jax.ShapeDtypeStruct(t.shape, t.dtype), compiler_params=_params("parallel", "parallel"))(t)


def make_reorder(name, S, to_segments):
    @jax.custom_vjp
    def op(t):
        return _reorder_call(name, t, S, to_segments)

    op.defvjp(lambda t: (_reorder_call(name, t, S, to_segments), None),
              lambda _, g: (_reorder_call(name + "_bwd", g, S, not to_segments),))
    return op


def _s5_scan_call(name, Xs, A, S, dirs, prev=None):
    with_p = prev is not None
    T, N = Xs[0].shape
    B, nl = T // S, N // LANES
    n_it = S // S5_SEGMENTS
    assert n_it & (n_it - 1) == 0

    def cmul(ar, ai, br, bi):
        return ar * br - ai * bi, ar * bi + ai * br

    def body(*refs):
        x, a_ref = refs[0:4], refs[4]
        if with_p:
            h_prev, x_prev, h, p_ref = refs[5:9], refs[9:13], refs[13:17], refs[17]
        else:
            h = refs[5:9]
        seg = lax.broadcasted_iota(jnp.int32, (S5_SEGMENTS, 1), 0)
        zero = jnp.zeros((S5_SEGMENTS, LANES), F32)
        a = [(jnp.broadcast_to(a_ref[2 * k:2 * k + 1, :], (S5_SEGMENTS, LANES)),
              jnp.broadcast_to(a_ref[2 * k + 1:2 * k + 2, :], (S5_SEGMENTS, LANES))) for k in range(2)]

        def rows_of(k, i):
            return pl.ds(pl.multiple_of(((n_it - 1 - i) if dirs[k] == 1 else i) * S5_SEGMENTS, S5_SEGMENTS), S5_SEGMENTS)

        def local(i, carry):
            out = []
            for k, (sr, si) in enumerate(carry):
                rows = rows_of(k, i)
                pr, pi = cmul(*a[k], sr, si)
                sr, si = pr + x[2 * k][rows, :], pi + x[2 * k + 1][rows, :]
                h[2 * k][rows, :] = sr
                h[2 * k + 1][rows, :] = si
                out.append((sr, si))
            return tuple(out)

        ends = lax.fori_loop(0, n_it, local, ((zero, zero), (zero, zero)), unroll=8)
        inherit = []
        for k, (er, ei) in enumerate(ends):
            back = dirs[k] == 1
            pr, pi = a[k]
            for _ in range(n_it.bit_length() - 1):
                pr, pi = cmul(pr, pi, pr, pi)
            shift = (S5_SEGMENTS - 1) if back else 1
            tr, ti = er, ei
            for r in (range(S5_SEGMENTS - 2, -1, -1) if back else range(1, S5_SEGMENTS)):
                nr, ni = cmul(pr, pi, pltpu.roll(tr, shift, 0), pltpu.roll(ti, shift, 0))
                tr = jnp.where(seg == r, er + nr, tr)
                ti = jnp.where(seg == r, ei + ni, ti)
            edge = (S5_SEGMENTS - 1) if back else 0
            inherit.append((jnp.where(seg == edge, 0.0, pltpu.roll(tr, shift, 0)), jnp.where(seg == edge, 0.0, pltpu.roll(ti, shift, 0))))

        def fix(i, carry):
            powers, sums = carry
            new_powers, new_sums = [], []
            for k in range(2):
                rows = rows_of(k, i)
                fr, fi = cmul(*powers[k], *inherit[k])
                sr, si = h[2 * k][rows, :] + fr, h[2 * k + 1][rows, :] + fi
                h[2 * k][rows, :] = sr
                h[2 * k + 1][rows, :] = si
                new_powers.append(cmul(*powers[k], *a[k]))
                if with_p:
                    ur = h_prev[2 * k][rows, :] - x_prev[2 * k][rows, :]
                    ui = h_prev[2 * k + 1][rows, :] - x_prev[2 * k + 1][rows, :]
                    new_sums.append((sums[k][0] + sr * ur + si * ui, sums[k][1] + si * ur - sr * ui))
            return tuple(new_powers), tuple(new_sums)

        _, sums = lax.fori_loop(0, n_it, fix, ((a[0], a[1]), ((zero, zero), (zero, zero)) if with_p else ()), unroll=8)
        if with_p:
            for k in range(2):
                p_ref[2 * k:2 * k + 1, :] = jnp.sum(sums[k][0], axis=0, keepdims=True)
                p_ref[2 * k + 1:2 * k + 2, :] = jnp.sum(sums[k][1], axis=0, keepdims=True)

    col = pl.BlockSpec((S, LANES), lambda b, j: (b, j))
    outs = pl.pallas_call(
        body, name=name, grid=(B, nl),
        in_specs=[col] * 4 + [pl.BlockSpec((4, LANES), lambda b, j: (0, j))] + ([col] * 8 if with_p else []),
        out_specs=[col] * 4 + ([pl.BlockSpec((None, 4, LANES), lambda b, j: (b, 0, j))] if with_p else []),
        out_shape=[jax.ShapeDtypeStruct((T, N), F32)] * 4 + ([jax.ShapeDtypeStruct((B, 4, N), F32)] if with_p else []),
        compiler_params=_params("parallel", "parallel"),
    )(*Xs, A, *(prev[0] + prev[1] if with_p else ()))
    return (tuple(outs[:4]), outs[4]) if with_p else tuple(outs)


def make_s5_scan(name, S):
    @jax.custom_vjp
    def op(X, A):
        return _s5_scan_call(name + "_fwd", X, A, S, (0, 1))

    def fwd(X, A):
        H = _s5_scan_call(name + "_fwd", X, A, S, (0, 1))
        return H, (X, A, H)

    def bwd(res, G):
        X, A, H = res
        conj = A * jnp.array([[1.0], [-1.0], [1.0], [-1.0]], F32)
        lam, P = _s5_scan_call(name + "_bwd", tuple(G), conj, S, (1, 0), prev=(tuple(H), tuple(X)))
        P = jnp.sum(P, axis=0)
        ar, ai = A[0::2], A[1::2]
        pr, pi = P[0::2], P[1::2]
        den = ar * ar + ai * ai
        dar, dai = (pr * ar - pi * ai) / den, (pr * ai + pi * ar) / den
        return lam, jnp.stack([dar[0], dai[0], dar[1], dai[1]], axis=0)

    op.defvjp(fwd, bwd)
    return op


ANY = pl.BlockSpec(memory_space=pl.ANY)


def _place():
    x, y, c = lax.axis_index("x"), lax.axis_index("y"), lax.axis_index("c")
    return x, y, c, [(1 - x, y), (x, 1 - y), (1 - x, 1 - y)]


def all_gather(name, arrs):
    n = len(arrs)

    def body(*refs):
        ins, outs = refs[:n], refs[n:2 * n]
        send, recv, lsem = refs[2 * n:]
        x, y, c, chips = _place()
        me, sibling = (x, y, c), (x, y, 1 - c)

        def copy(a, k, block, to, src=None):
            slot = outs[a].at[4 * block[0] + 2 * block[1] + block[2]]
            return pltpu.make_async_remote_copy(src_ref=slot if src is None else src, dst_ref=slot, send_sem=send.at[a, k],
                                                recv_sem=recv.at[a, k], device_id=to, device_id_type=MESH)

        mine = [pltpu.make_async_copy(ins[a], outs[a].at[4 * x + 2 * y + c], lsem.at[a]) for a in range(n)]
        first = []
        for a in range(n):
            mine[a].start()
            first.append(copy(a, 0, me, sibling, src=ins[a]))
            first += [copy(a, 1 + j, me, (*chip, c), src=ins[a]) for j, chip in enumerate(chips)]
        for cp in first:
            cp.start()
        passed = []
        for a in range(n):
            for j, chip in enumerate(chips):
                copy(a, 1 + j, (*chip, c), me).wait_recv()
                fwd = copy(a, 4 + j, (*chip, c), sibling)
                fwd.start()
                passed.append(fwd)
        for a in range(n):
            copy(a, 0, sibling, me).wait_recv()
            for j, chip in enumerate(chips):
                copy(a, 4 + j, (*chip, 1 - c), me).wait_recv()
        for cp in first + passed:
            cp.wait_send()
        for cp in mine:
            cp.wait()

    return pl.pallas_call(
        body, name=name, in_specs=[ANY] * n, out_specs=[ANY] * n,
        out_shape=[jax.ShapeDtypeStruct((N_DEV,) + a.shape, a.dtype) for a in arrs],
        scratch_shapes=[pltpu.SemaphoreType.DMA((n, 7)), pltpu.SemaphoreType.DMA((n, 7)), pltpu.SemaphoreType.DMA((n,))],
    )(*arrs)


HBM = pl.BlockSpec(memory_space=pltpu.HBM)
SEM = pl.BlockSpec(memory_space=pltpu.SEMAPHORE)
EFFECT = pltpu.SideEffectType.DATAFLOW_SIDE_EFFECTING
GATHER_PEERS = (1, 2, 4, 6)
OTHER_CHIPS = (2, 4, 6)
COPIES_PER_ARRAY = {"scatter": N_DEV - 1, "gather": len(GATHER_PEERS), "forward": len(OTHER_CHIPS)}


def _split_plan(mode, srcs, lands, send, recv):
    x, y, c = lax.axis_index("x"), lax.axis_index("y"), lax.axis_index("c")

    def dev(k):
        return (1 - x if k & 4 else x), (1 - y if k & 2 else y), (1 - c if k & 1 else c)

    def idx(d):
        return 4 * d[0] + 2 * d[1] + d[2]

    me = idx((x, y, c))
    plan = []
    for a, land in enumerate(lands):
        if mode == "scatter":
            legs = [(srcs[a].at[idx(dev(k))], land.at[me], land.at[idx(dev(k))], dev(k)) for k in range(1, N_DEV)]
        elif mode == "gather":
            legs = [(srcs[a], land.at[me], land.at[idx(dev(k))], dev(k)) for k in GATHER_PEERS]
        else:
            legs = [(land.at[idx(dev(j))], land.at[idx(dev(j))], land.at[idx(dev(j ^ 1))], dev(1)) for j in OTHER_CHIPS]
        for i, (src, dst, arrival, to) in enumerate(legs):
            sem = a * len(legs) + i
            pair = tuple(pltpu.make_async_remote_copy(src_ref=src, dst_ref=d, send_sem=send.at[sem], recv_sem=recv.at[sem],
                                                      device_id=to, device_id_type=MESH) for d in (dst, arrival))
            plan.append(pair)
    return plan


def split_start(name, mode, srcs, lands, after):
    if lands is None:
        lands = [lax.empty((N_DEV,) + (s.shape[1:] if mode == "scatter" else s.shape), s.dtype) for s in srcs]
    ns, nl = len(srcs), len(lands)
    nsem = COPIES_PER_ARRAY[mode] * nl

    def body(*refs):
        ins, lnd = refs[:ns], refs[ns:ns + nl]
        send, recv = refs[ns + nl + 1], refs[ns + nl + 2]
        token = refs[-1]
        for out, _ in _split_plan(mode, ins, lnd, send, recv):
            out.start()
        token[...] = jnp.zeros_like(token)

    arrs = list(srcs) + list(lands)
    return pl.pallas_call(
        body, name=name,
        out_shape=(pltpu.SemaphoreType.DMA((nsem,)), pltpu.SemaphoreType.DMA((nsem,)))
        + tuple(pltpu.HBM(t.shape, t.dtype) for t in arrs) + (jax.ShapeDtypeStruct((SUBLANES, LANES), F32),),
        in_specs=[HBM] * len(arrs) + [ANY],
        out_specs=(SEM, SEM) + (HBM,) * len(arrs) + (pl.BlockSpec(memory_space=pltpu.VMEM),),
        input_output_aliases={i: 2 + i for i in range(len(arrs))},
        compiler_params=pltpu.CompilerParams(has_side_effects=EFFECT),
    )(*[pltpu.with_memory_space_constraint(t, pltpu.HBM) for t in arrs], after)


def split_wait(name, mode, handle, after):
    send, recv = handle[0], handle[1]
    arrs = list(handle[2:-1])
    nl = len(arrs) if mode == "forward" else len(arrs) // 2
    ns = len(arrs) - nl

    def body(*refs):
        ins, lnd = refs[:ns], refs[ns:ns + nl]
        s, r = refs[ns + nl], refs[ns + nl + 1]
        for out, arrival in _split_plan(mode, ins, lnd, s, r):
            out.wait_send()
            arrival.wait_recv()

    outs = pl.pallas_call(
        body, name=name,
        out_shape=tuple(pltpu.HBM(t.shape, t.dtype) for t in arrs),
        in_specs=[HBM] * len(arrs) + [SEM, SEM, ANY], out_specs=(HBM,) * len(arrs),
        input_output_aliases={i: i for i in range(len(arrs))},
        compiler_params=pltpu.CompilerParams(has_side_effects=EFFECT),
    )(*arrs, send, recv, after)
    return list(outs[ns:])


def _row_tile(rows, cols):
    cap = max(SUBLANES, (2**18 // cols) // SUBLANES * SUBLANES)
    if rows <= cap:
        return rows
    for t in range(cap, SUBLANES - 1, -SUBLANES):
        if rows % t == 0:
            return t
    return rows


def ordered_sum(name, parts):
    n, R, C = parts.shape
    rt = _row_tile(R, C)

    def body(p_ref, o_ref):
        s = p_ref[0]
        for k in range(1, n):
            s = s + p_ref[k]
        o_ref[...] = s

    return pl.pallas_call(
        body, name=name, grid=(R // rt,),
        in_specs=[pl.BlockSpec((n, rt, C), lambda r: (0, r, 0))], out_specs=pl.BlockSpec((rt, C), lambda r: (r, 0)),
        out_shape=jax.ShapeDtypeStruct((R, C), F32), compiler_params=_params("parallel"),
    )(parts)


def _adamw_update(w, m, v, g):
    bias1 = 1.0 - ADAM_B1 ** ADAM_STEP
    bias2 = 1.0 - ADAM_B2 ** ADAM_STEP
    m_new = ADAM_B1 * m + (1.0 - ADAM_B1) * g
    v_new = ADAM_B2 * v + (1.0 - ADAM_B2) * (g * g)
    delta = -ADAM_LR * ((m_new / bias1) / (jnp.sqrt(v_new / bias2) + ADAM_EPS) + ADAM_WD * w)
    return delta, m_new, v_new


def adamw(name, w, m, v, g):
    R, C = w.shape
    rt = _row_tile(R, C)

    def body(w_ref, m_ref, v_ref, g_ref, d_ref, mo_ref, vo_ref):
        d_ref[...], mo_ref[...], vo_ref[...] = _adamw_update(w_ref[...], m_ref[...], v_ref[...], g_ref[...])

    row = pl.BlockSpec((rt, C), lambda r: (r, 0))
    return pl.pallas_call(
        body, name=name, grid=(R // rt,), in_specs=[row] * 4, out_specs=[row] * 3,
        out_shape=[jax.ShapeDtypeStruct((R, C), F32)] * 3, compiler_params=_params("parallel"),
    )(w, m, v, g)


def adamw_sharded(name, layer, w, m, v, own, land, me, prev):
    _, R, C = own.shape
    rt = _row_tile(R, C)
    base = layer * (R // rt)

    def body(me_ref, w_ref, m_ref, v_ref, own_ref, land_ref, *rest):
        go_ref, d_ref, mo_ref, vo_ref = rest[-4:]
        g = own_ref[...]
        for k in range(N_DEV):
            g = g + jnp.where(me_ref[0] == k, 0.0, land_ref[k].astype(F32))
        go_ref[...] = g
        d_ref[...], mo_ref[...], vo_ref[...] = _adamw_update(w_ref[...], m_ref[...], v_ref[...], g)

    row = pl.BlockSpec((rt, C), lambda r, p: (base + r, 0))
    in_specs = [row, row, row, pl.BlockSpec((None, rt, C), lambda r, p: (p[0], r, 0)), pl.BlockSpec((N_DEV, rt, C), lambda r, p: (0, r, 0))]
    ins = [me, w, m, v, own, land]
    aliases = {}
    if prev is not None:
        in_specs += [ANY] * 4
        aliases = {len(ins) + k: k for k in range(4)}
        ins += list(prev)
    return pl.pallas_call(
        body, name=name,
        grid_spec=pltpu.PrefetchScalarGridSpec(num_scalar_prefetch=1, grid=(R // rt,), in_specs=in_specs, out_specs=[row] * 4),
        out_shape=[jax.ShapeDtypeStruct(w.shape, F32)] * 4, input_output_aliases=aliases,
        compiler_params=_params("arbitrary"),
    )(*ins)


def _hgrn_lower_bounds(lb_logits):
    p = jax.nn.softmax(lb_logits, axis=1)
    return jnp.cumsum(p, axis=1) - p[:, :1]


def _s5_discretise(lam_re, lam_im, log_dt, b_re, b_im):
    lr = jnp.minimum(lam_re, -1e-4)
    li = lam_im
    dt = jnp.exp(log_dt)[:, None]
    mag = jnp.exp(lr * dt)
    ar, ai = mag * jnp.cos(li * dt), mag * jnp.sin(li * dt)
    den = lr * lr + li * li
    nr = ar - 1.0
    cr = (nr * lr + ai * li) / den
    ci = (ai * lr - nr * li) / den
    bbr = cr[..., None] * b_re - ci[..., None] * b_im
    bbi = cr[..., None] * b_im + ci[..., None] * b_re
    return ar, ai, bbr, bbi


def _block_diag(t):
    G, a, b = t.shape
    eye = jnp.eye(G, dtype=F32)
    return (t[:, :, None, :] * eye[:, None, :, None]).reshape(G * a, G * b)


def _rope_tables(S):
    half = RET_DK // 2
    inv = ROPE_BASE ** (-jnp.arange(half, dtype=F32) / half)
    ang = jnp.arange(S, dtype=F32)[:, None] * inv[None, :]
    cos, sin = jnp.cos(ang), jnp.sin(ang)
    return jnp.concatenate([cos, cos], axis=1), jnp.concatenate([-sin, sin], axis=1)


def _ret_log_decays():
    f = tuple(float(np.log1p(-np.exp2(np.float32(-5.0 - h)))) for h in range(RET_HEADS))
    b = tuple(float(np.log1p(-np.exp2(np.float32(-5.5 - h)))) for h in range(RET_HEADS))
    return f, b


def assemble_weight(name, land, own, me, col_sharded):
    _, R, C = land.shape
    if col_sharded:
        tr = min(R, 256)

        def body(me_ref, land_ref, own_ref, o_ref):
            for d in range(N_DEV):
                o_ref[:, d * C:(d + 1) * C] = jnp.where(me_ref[0] == d, own_ref[...], land_ref[d])

        grid, out_shape = (R // tr,), (R, N_DEV * C)
        in_specs = [pl.BlockSpec((N_DEV, tr, C), lambda i, p: (0, i, 0)), pl.BlockSpec((tr, C), lambda i, p: (i, 0))]
        out_spec = pl.BlockSpec((tr, N_DEV * C), lambda i, p: (i, 0))
    else:
        def body(me_ref, land_ref, own_ref, o_ref):
            o_ref[...] = jnp.where(me_ref[0] == pl.program_id(0), own_ref[...], land_ref[...])

        grid, out_shape = (N_DEV,), (N_DEV * R, C)
        in_specs = [pl.BlockSpec((None, R, C), lambda d, p: (d, 0, 0)), pl.BlockSpec((R, C), lambda d, p: (0, 0))]
        out_spec = pl.BlockSpec((R, C), lambda d, p: (d, 0))
    return pl.pallas_call(
        body, name=name,
        grid_spec=pltpu.PrefetchScalarGridSpec(num_scalar_prefetch=1, grid=grid, in_specs=in_specs, out_specs=out_spec),
        out_shape=jax.ShapeDtypeStruct(out_shape, land.dtype), compiler_params=_params("parallel"),
    )(me, land, own)


def _row(t):
    return t.reshape(1, -1)


def mixer_stage(layer, S, resid, branch, w_in, w_out, carriers, small):
    j = layer // 2
    tag = f"l{layer}"
    g = _row(small["mix_norm_g"][layer])
    if branch is None:
        x = resid
        (h,) = make_rowop(tag + "_norm", norm_f, (D_MODEL,))((x,), (g,), ())
    else:
        x, h = make_rowop(tag + "_addnorm", addnorm_f, (D_MODEL, D_MODEL))((resid, branch), (g,), ())
    if layer % 2 == 0:
        lbs = _hgrn_lower_bounds(small["hgrn_lb_logits"])
        prm = (small["gla_wa2"][j, 0], small["gla_wa2"][j, 1], _row(small["gla_ba"][j, 0]), _row(small["gla_ba"][j, 1]),
               _row(lbs[0, j]), _row(lbs[1, j]))
        outs = make_proj_stage(tag + "_prep", even_prep_f, EVEN_PREP_WIDTHS)(h, w_in, carriers["in32"], carriers["in16"], prm, ())
        gla_qkv, (glaf, glab, gr), hgrn_qkv, (hlaf, hlab, hg) = outs[0:6], outs[6:9], outs[9:15], outs[15:18]
        of, ob = make_scan(tag + "_gla", GLA_HEADS, GLA_DK, GLA_DV, S, 2 * SCAN_CHUNK)(gla_qkv, glaf, glab)
        hof, hob = make_scan(tag + "_hgrn", HGRN_HEADS, HGRN_DK, HGRN_DV, S, SCAN_CHUNK)(hgrn_qkv, hlaf, hlab)
        (y,) = make_rowop(tag + "_post", even_post_f, (D_MODEL,))(
            (of, ob, hof, hob, gr, hg), (_row(small["gla_norm_g"][j]), _row(small["hgrn_norm_g"][j])), ())
    else:
        cosf, sinf = _rope_tables(S)
        outs = make_proj_stage(tag + "_prep", odd_prep_f, ODD_PREP_WIDTHS)(h, w_in, carriers["in32"], carriers["in16"], (), (cosf, sinf))
        ret_qkv, rg, su = outs[0:6], outs[6], outs[7]
        of, ob = make_scan(tag + "_ret", RET_HEADS, RET_DK, RET_DV, S, 2 * SCAN_CHUNK, const_lg=_ret_log_decays())(ret_qkv)
        (cm,) = make_rowop(tag + "_retpost", ret_post_f, (RET_HEADS * RET_DV,))((of, ob, rg), (_row(small["ret_norm_g"][j]),), ())
        disc = [_s5_discretise(small["s5_lam_re"][j, d], small["s5_lam_im"][j, d], small["s5_log_dt"][j, d],
                               small["s5_b_re"][j], small["s5_b_im"][j]) for d in range(2)]
        a4 = jnp.stack([t.reshape(-1) for d in range(2) for t in disc[d][:2]], axis=0)
        su_p = make_reorder(tag + "_s5seg", S, True)(su)
        Xs = tuple(make_mm_f32w(f"{tag}_s5in{2 * d + i}")(su_p, _block_diag(jnp.swapaxes(disc[d][2 + i], 1, 2)))
                   for d in range(2) for i in range(2))
        Hs = make_s5_scan(tag + "_s5scan", S)(Xs, a4)
        prm = (_block_diag(jnp.swapaxes(small["s5_c_re"][j], 1, 2)), _block_diag(jnp.swapaxes(small["s5_c_im"][j], 1, 2)),
               _row(small["s5_d"][j]), small["s5_glu_w"][j], _row(small["s5_glu_b"][j]))
        (dm_p,) = make_rowop(tag + "_s5post", s5_post_f, (S5_WIDTH,))((*Hs, su_p), prm, ())
        y = jnp.concatenate([cm, make_reorder(tag + "_s5time", S, False)(dm_p)], axis=1)
    return x, make_mm(tag + "_out")(y, w_out, carriers["out32"], carriers["out16"])


def ffn_stage(layer, S, resid, branch, w_up, w_down, carriers, small):
    tag = f"l{layer}"
    x, hf = make_rowop(tag + "_ffnnorm", addnorm_f, (D_MODEL, D_MODEL))((resid, branch), (_row(small["ffn_norm_g"][layer]),), ())
    out = make_ffn(tag + "_ffn", S)(hf, w_up, carriers["up32"], carriers["up16"], w_down, carriers["down32"], carriers["down16"],
                                   small["ffn_conv_w"][layer], _row(small["ffn_conv_b"][layer]))
    return x, out


BIG = {"w_in_even": 2, "w_out_even": 1, "w_in_odd": 2, "w_out_odd": 1, "ffn_w_up": 2, "ffn_w_down": 1}
SMALL_SHARDED = {"gla_wa2": 3, "gla_ba": 2, "hgrn_lb_logits": 2, "ret_norm_g": 1, "s5_d": 1, "s5_glu_w": 1, "s5_glu_b": 1,
                 "ffn_conv_w": 2}
REPLICATED = ("mix_norm_g", "ffn_norm_g", "final_norm_g", "gla_norm_g", "hgrn_norm_g", "s5_lam_re", "s5_lam_im", "s5_log_dt",
              "s5_b_re", "s5_b_im", "s5_c_re", "s5_c_im", "ffn_conv_b")
WEIGHTS = ("mix_norm_g", "ffn_norm_g", "final_norm_g", "w_in_even", "w_out_even", "gla_wa2", "gla_ba", "gla_norm_g",
           "hgrn_lb_logits", "hgrn_norm_g", "w_in_odd", "w_out_odd", "ret_norm_g", "s5_lam_re", "s5_lam_im", "s5_log_dt",
           "s5_b_re", "s5_b_im", "s5_c_re", "s5_c_im", "s5_d", "s5_glu_w", "s5_glu_b", "ffn_w_up", "ffn_conv_w", "ffn_conv_b",
           "ffn_w_down")
PACK_COLS = LANES
MIXER_SMALL = (("mix_norm_g", "hgrn_lb_logits", "gla_wa2", "gla_ba", "gla_norm_g", "hgrn_norm_g"),
               ("mix_norm_g", "ret_norm_g", "s5_lam_re", "s5_lam_im", "s5_log_dt", "s5_b_re", "s5_b_im", "s5_c_re", "s5_c_im",
                "s5_d", "s5_glu_w", "s5_glu_b"))
FFN_SMALL = ("ffn_norm_g", "ffn_conv_w", "ffn_conv_b")


def _unshard(g, axis):
    t = jnp.moveaxis(g, 0, axis)
    return t.reshape(t.shape[:axis] + (t.shape[axis] * t.shape[axis + 1],) + t.shape[axis + 2:])


def _to_blocks(full, axis):
    t = full.reshape(full.shape[:axis] + (N_DEV, full.shape[axis] // N_DEV) + full.shape[axis + 1:])
    return jnp.moveaxis(t, axis, 0)


def _pack_rows(shape):
    return -(-int(np.prod(shape)) // (SUBLANES * PACK_COLS)) * SUBLANES


def _pack(arrs):
    parts = []
    for a in arrs:
        rows = _pack_rows(a.shape)
        parts.append(jnp.pad(a.reshape(-1), (0, rows * PACK_COLS - a.size)).reshape(rows, PACK_COLS))
    return jnp.concatenate(parts, axis=0)


def _unpack(packed, shapes):
    lead = packed.shape[:-2]
    out, r = [], 0
    for s in shapes:
        rows, n = _pack_rows(s), int(np.prod(s))
        piece = packed[..., r:r + rows, :].reshape(lead + (rows * PACK_COLS,))
        out.append(piece[..., :n].reshape(lead + tuple(s)))
        r += rows
    return out


def _flat2d(a):
    return a.reshape(-1, a.shape[-1])


def kernel(x, mix_norm_g, ffn_norm_g, final_norm_g, w_in_even, w_out_even, gla_wa2, gla_ba, gla_norm_g, hgrn_lb_logits, hgrn_norm_g, w_in_odd, w_out_odd, ret_norm_g, s5_lam_re, s5_lam_im, s5_log_dt, s5_b_re, s5_b_im, s5_c_re, s5_c_im, s5_d, s5_glu_w, s5_glu_b, ffn_w_up, ffn_conv_w, ffn_conv_b, ffn_w_down, loss_target, m_mix_norm_g, m_ffn_norm_g, m_final_norm_g, m_w_in_even, m_w_out_even, m_gla_wa2, m_gla_ba, m_gla_norm_g, m_hgrn_lb_logits, m_hgrn_norm_g, m_w_in_odd, m_w_out_odd, m_ret_norm_g, m_s5_lam_re, m_s5_lam_im, m_s5_log_dt, m_s5_b_re, m_s5_b_im, m_s5_c_re, m_s5_c_im, m_s5_d, m_s5_glu_w, m_s5_glu_b, m_ffn_w_up, m_ffn_conv_w, m_ffn_conv_b, m_ffn_w_down, v_mix_norm_g, v_ffn_norm_g, v_final_norm_g, v_w_in_even, v_w_out_even, v_gla_wa2, v_gla_ba, v_gla_norm_g, v_hgrn_lb_logits, v_hgrn_norm_g, v_w_in_odd, v_w_out_odd, v_ret_norm_g, v_s5_lam_re, v_s5_lam_im, v_s5_log_dt, v_s5_b_re, v_s5_b_im, v_s5_c_re, v_s5_c_im, v_s5_d, v_s5_glu_w, v_s5_glu_b, v_ffn_w_up, v_ffn_conv_w, v_ffn_conv_b, v_ffn_w_down):
    args = locals()
    w = {n: args[n] for n in WEIGHTS}
    m = {n: args["m_" + n] for n in WEIGHTS}
    v = {n: args["v_" + n] for n in WEIGHTS}
    Bl, S, D = x.shape
    T = Bl * S
    ix, iy, ic = lax.axis_index("x"), lax.axis_index("y"), lax.axis_index("c")
    me = 4 * ix + 2 * iy + ic

    xt = x.reshape(T, D)
    me1 = me.reshape(1).astype(jnp.int32)
    stages = []
    for layer in range(DEPTH):
        j = layer // 2
        kin, kout = ("w_in_even", "w_out_even") if layer % 2 == 0 else ("w_in_odd", "w_out_odd")
        stages.append((mixer_stage, layer, ("in", "out"), ((kin, j, True), (kout, j, False)), MIXER_SMALL[layer % 2]))
        stages.append((ffn_stage, layer, ("up", "down"), (("ffn_w_up", layer, True), ("ffn_w_down", layer, False)), FFN_SMALL))

    gather, after = [], xt
    for s, (_, _, _, projs, _) in enumerate(stages):
        handle = split_start(f"gather{s}_start", "gather", [w[n][l].astype(BF16) for n, l, _ in projs], None, after)
        gather.append(handle)
        after = handle[-1]
    sm_names = list(SMALL_SHARDED)
    (sm_all8,) = all_gather("gather_small", [_pack([w[n] for n in sm_names])])
    small = {n: _unshard(t, SMALL_SHARDED[n]) for n, t in zip(sm_names, _unpack(sm_all8, [w[n].shape for n in sm_names]))}
    small.update({n: w[n] for n in REPLICATED})
    small["mix_norm_g"] = small["mix_norm_g"] + after[0, 0]

    resid, branch, pulls = xt, None, []
    second = [None] * len(stages)

    def second_level(s, after):
        lands = split_wait(f"gather{s}_wait", "gather", gather[s], after)
        second[s] = split_start(f"forward{s}_start", "forward", [], lands, after)

    for s, (fn, layer, keys, projs, sm_keys) in enumerate(stages):
        here = lax.stop_gradient(resid)
        if second[s] is None:
            second_level(s, here)
        lands = split_wait(f"forward{s}_wait", "forward", second[s], second[s][-1])
        sm = {n: small[n] for n in sm_keys}
        if 2 <= s < len(stages) - 1:
            second_level(s + 1, here)
            norm = "mix_norm_g" if fn is mixer_stage else "ffn_norm_g"
            sm[norm] = sm[norm] + second[s + 1][-1][0, 0]
        full, carriers = [], {}
        for key, land, (n, l, col) in zip(keys, lands, projs, strict=True):
            full.append(assemble_weight(f"weight{s}_{key}", land, w[n][l].astype(BF16), me1, col))
            carriers[key + "32"] = jnp.zeros(land.shape, F32)
            carriers[key + "16"] = jnp.zeros(land.shape, BF16)
        run = functools.partial(fn, layer, S)
        if branch is None:
            (resid, branch), pull = jax.vjp(lambda r, c, p, run=run, full=full: run(r, None, full[0], full[1], c, p), resid, carriers, sm)
        else:
            (resid, branch), pull = jax.vjp(lambda r, b, c, p, run=run, full=full: run(r, b, full[0], full[1], c, p), resid, branch, carriers, sm)
        pulls.append(pull)

    loss_acc, dxf, dgf = loss_head(resid, branch, small["final_norm_g"].reshape(1, D), loss_target.reshape(T, D))
    loss = lax.psum(loss_acc[0, 0], ("x", "y", "c"))
    g_small = {"final_norm_g": dgf.reshape(D)}
    d_resid, d_branch, token = dxf, dxf, None
    scatter, own32 = [None] * len(stages), [None] * len(stages)
    for s in reversed(range(len(stages))):
        if token is not None:
            d_resid = lax.dynamic_update_slice(d_resid, d_resid[:SUBLANES, :LANES] + token, (0, 0))
        if s == 0:
            d_resid, dcar, dsm = pulls[s]((d_resid, d_branch))
        else:
            d_resid, d_branch, dcar, dsm = pulls[s]((d_resid, d_branch))
        for n, g in dsm.items():
            g_small[n] = g_small[n] + g if n in g_small else g
        keys = stages[s][2]
        own32[s] = [dcar[k + "32"] for k in keys]
        scatter[s] = split_start(f"grads{s}_start", "scatter", [dcar[k + "16"] for k in keys], None, d_resid)
        token = scatter[s][-1]
    dx = d_resid

    out = {}
    sm_all = sm_names + list(REPLICATED)
    (g_all,) = all_gather("gather_small_grads", [_pack([g_small[n] for n in sm_all])])
    g_sum = _unpack(ordered_sum("sum_small_grads", g_all), [g_small[n].shape for n in sm_all])
    g_loc = []
    for n, g in zip(sm_all, g_sum):
        if n in SMALL_SHARDED:
            ax = SMALL_SHARDED[n]
            size = w[n].shape[ax]
            g = lax.dynamic_slice_in_dim(g, me * size, size, axis=ax)
        g_loc.append(g)
    shapes = [w[n].shape for n in sm_all]
    res = adamw("adamw_small", _pack([w[n] for n in sm_all]), _pack([m[n] for n in sm_all]), _pack([v[n] for n in sm_all]), _pack(g_loc))
    last_small = res[0]
    res = [g_loc] + [_unpack(r, shapes) for r in res]
    for k, n in enumerate(sm_all):
        out[n] = [r[k] for r in res]

    chain, behind = {}, last_small
    for s in reversed(range(len(stages))):
        lands = split_wait(f"grads{s}_wait", "scatter", scatter[s], behind if s == 0 else dx)
        for own, land, (n, l, _) in zip(own32[s], lands, stages[s][3], strict=True):
            chain[n] = adamw_sharded(f"adamw_{n}_{l}", l, _flat2d(w[n]), _flat2d(m[n]), _flat2d(v[n]), own, land, me1, chain.get(n))
            behind = chain[n][0]
    for n in BIG:
        out[n] = [t.reshape(w[n].shape) for t in chain[n]]

    grads, deltas, new_m, new_v = ([out[n][k] for n in WEIGHTS] for k in range(4))
    return (loss, dx.reshape(Bl, S, D), *grads, *deltas, *new_m, *new_v)
```

```python
import functools
import math

import numpy as np
import jax
import jax.numpy as jnp
from jax import lax
from jax.experimental import pallas as pl
from jax.experimental.pallas import tpu as pltpu

F32 = jnp.float32
BF16 = jnp.bfloat16
HIGHEST = lax.Precision.HIGHEST
MESH = pl.DeviceIdType.MESH

D_MODEL = 1024
DEPTH = 4
N_EVEN = 2
N_ODD = 2
GLA_HEADS, GLA_DK, GLA_DV, GLA_RANK, GLA_GATE_NORM = 4, 64, 128, 16, 16.0
HGRN_HEADS, HGRN_DK, HGRN_DV, HGRN_MIN_F = 4, 64, 128, 1e-20
RET_HEADS, RET_DK, RET_DV = 4, 128, 192
ROPE_BASE = 10000.0
S5_WIDTH, S5_GROUP_CH, S5_GROUPS, S5_STATE = 256, 16, 16, 64
S5_N = S5_GROUPS * S5_STATE
FFN_DIM = 2816
EPS = 1e-6
EVEN_IN = 3360
ODD_IN = 2816
ADAM_LR, ADAM_B1, ADAM_B2, ADAM_EPS, ADAM_WD, ADAM_STEP = 0.001, 0.9, 0.999, 1e-08, 0.01, 10

N_DEV = 8
VMEM_LIMIT_BYTES = 56 * 1024 * 1024
ROW_TILE = 256
SCAN_CHUNK = 64
S5_SEGMENTS = 8
LANES = 128


def _params(*sem):
    return pltpu.CompilerParams(dimension_semantics=sem, vmem_limit_bytes=VMEM_LIMIT_BYTES)


def _divisor_tile(n, cap):
    best = None
    for t in range(LANES, min(n, cap) + 1, LANES):
        if n % t == 0:
            best = t
    return best if best is not None else n


def _mm_nn(name, x, w):
    M, K = x.shape
    N = w.shape[1]
    tn = _divisor_tile(N, 3072) if K * N * 2 > 8 * 2**20 else N
    tm = 256 if tn * 4 * 512 > 6 * 2**20 else 512
    assert M % tm == 0 and N % tn == 0

    def body(x_ref, w_ref, o_ref):
        o_ref[...] = jnp.dot(x_ref[...].astype(BF16), w_ref[...], preferred_element_type=F32)

    return pl.pallas_call(
        body, name=name, grid=(N // tn, M // tm),
        in_specs=[pl.BlockSpec((tm, K), lambda j, i: (i, 0)), pl.BlockSpec((K, tn), lambda j, i: (0, j))],
        out_specs=pl.BlockSpec((tm, tn), lambda j, i: (i, j)),
        out_shape=jax.ShapeDtypeStruct((M, N), F32),
        compiler_params=_params("parallel", "parallel"),
    )(x, w)


def _mm_nt(name, dy, w):
    M, N = dy.shape
    K = w.shape[0]
    tk = _divisor_tile(K, 1024) if K * N * 2 > 8 * 2**20 else K
    tm = 256 if N >= 4096 else 512
    assert M % tm == 0 and K % tk == 0

    def body(dy_ref, w_ref, o_ref):
        o_ref[...] = lax.dot_general(dy_ref[...].astype(BF16), w_ref[...], (((1,), (1,)), ((), ())),
                                     preferred_element_type=F32)

    return pl.pallas_call(
        body, name=name, grid=(K // tk, M // tm),
        in_specs=[pl.BlockSpec((tm, N), lambda j, i: (i, 0)), pl.BlockSpec((tk, N), lambda j, i: (j, 0))],
        out_specs=pl.BlockSpec((tm, tk), lambda j, i: (i, j)),
        out_shape=jax.ShapeDtypeStruct((M, K), F32),
        compiler_params=_params("parallel", "parallel"),
    )(dy, w)


MM_TN_VMEM_BUDGET = 46 * 2**20


def _pad_lanes(n):
    return -(-n // LANES) * LANES


def _mm_tn(name, x, dy, nblk, want16):
    M, K = x.shape
    N = dy.shape[1]
    n = N // nblk
    xb, yb = x.dtype.itemsize, dy.dtype.itemsize
    best = None
    for tk in [t for t in range(LANES, K + 1, LANES) if K % t == 0]:
        for tm in (512, 256):
            out_bytes = nblk * tk * _pad_lanes(n) * (6 if want16 else 4)
            vmem = 2 * out_bytes + 2 * tm * tk * xb + 2 * tm * _pad_lanes(N) * yb
            traffic = (K // tk) * M * N * yb + M * K * xb
            if vmem <= MM_TN_VMEM_BUDGET and M % tm == 0 and (best is None or (traffic, -tm) < best[0]):
                best = ((traffic, -tm), tk, tm)
    _, tk, tm = best
    last = M // tm - 1

    def body(x_ref, dy_ref, o32_ref, *o16_ref):
        m = pl.program_id(1)

        @pl.when(m == 0)
        def _():
            o32_ref[...] = jnp.zeros_like(o32_ref)

        dyv = dy_ref[...].astype(BF16)
        if nblk == 1:
            o32_ref[0] += lax.dot_general(x_ref[...].astype(BF16), dyv, (((0,), (0,)), ((), ())), preferred_element_type=F32)
        else:
            xt = x_ref[...].astype(F32).T.astype(BF16)
            for d in range(nblk):
                o32_ref[d] += jnp.dot(xt, dyv[:, d * n:(d + 1) * n], preferred_element_type=F32)
        if want16:
            @pl.when(m == last)
            def _():
                o16_ref[0][...] = o32_ref[...].astype(BF16)

    blk = pl.BlockSpec((nblk, tk, n), lambda a, m: (0, a, 0))
    return pl.pallas_call(
        body, name=name, grid=(K // tk, M // tm),
        in_specs=[pl.BlockSpec((tm, tk), lambda a, m: (m, a)), pl.BlockSpec((tm, N), lambda a, m: (m, 0))],
        out_specs=[blk, blk] if want16 else [blk],
        out_shape=[jax.ShapeDtypeStruct((nblk, K, n), F32)] + ([jax.ShapeDtypeStruct((nblk, K, n), BF16)] if want16 else []),
        compiler_params=_params("parallel", "arbitrary"),
    )(x, dy)


def _weight_grads(name, x, dy, col_sharded):
    if col_sharded:
        return _mm_tn(name, x, dy, N_DEV, True)
    d32, d16 = _mm_tn(name, x, dy, 1, True)
    K, N = d32.shape[1:]
    return d32.reshape(N_DEV, K // N_DEV, N), d16.reshape(N_DEV, K // N_DEV, N)


def make_mm(name, col_sharded=False):
    @jax.custom_vjp
    def mm(x, w16, c32, c16):
        return _mm_nn(name + "_fwd", x, w16)

    def fwd(x, w16, c32, c16):
        return _mm_nn(name + "_fwd", x, w16), (x, w16)

    def bwd(res, dy):
        x, w16 = res
        d32, d16 = _weight_grads(name + "_dw", x, dy, col_sharded)
        return _mm_nt(name + "_dx", dy, w16), jnp.zeros_like(w16), d32, d16

    mm.defvjp(fwd, bwd)
    return mm


def make_mm_groups(name, G, tm=512):
    def products(x, w16):
        M, K = x.shape
        N = w16.shape[1] // G

        def body(x_ref, w_ref, *o_refs):
            xv = x_ref[...].astype(BF16)
            for g, o_ref in enumerate(o_refs):
                o_ref[...] = jnp.dot(xv, w_ref[:, g * N:(g + 1) * N], preferred_element_type=F32)

        row = pl.BlockSpec((tm, N), lambda i: (i, 0))
        return tuple(pl.pallas_call(
            body, name=name + "_fwd", grid=(M // tm,),
            in_specs=[pl.BlockSpec((tm, K), lambda i: (i, 0)), pl.BlockSpec((K, G * N), lambda i: (0, 0))],
            out_specs=[row] * G, out_shape=[jax.ShapeDtypeStruct((M, N), F32)] * G, compiler_params=_params("parallel"),
        )(x, w16))

    def grads(x, w16, dys):
        M, K = x.shape
        N = w16.shape[1] // G

        def body(x_ref, w_ref, *refs):
            dy_refs, dx_ref, dw_ref = refs[:G], refs[G], refs[G + 1]

            @pl.when(pl.program_id(0) == 0)
            def _():
                dw_ref[...] = jnp.zeros_like(dw_ref)

            xv = x_ref[...].astype(BF16)
            dx = jnp.zeros(dx_ref.shape, F32)
            for g in range(G):
                cols = slice(g * N, (g + 1) * N)
                dyv = dy_refs[g][...].astype(BF16)
                dx = dx + lax.dot_general(dyv, w_ref[:, cols], (((1,), (1,)), ((), ())), preferred_element_type=F32)
                dw_ref[:, cols] += lax.dot_general(xv, dyv, (((0,), (0,)), ((), ())), preferred_element_type=F32)
            dx_ref[...] = dx

        row = pl.BlockSpec((tm, N), lambda i: (i, 0))
        return pl.pallas_call(
            body, name=name + "_bwd", grid=(M // tm,),
            in_specs=[pl.BlockSpec((tm, K), lambda i: (i, 0)), pl.BlockSpec((K, G * N), lambda i: (0, 0))] + [row] * G,
            out_specs=[pl.BlockSpec((tm, K), lambda i: (i, 0)), pl.BlockSpec((K, G * N), lambda i: (0, 0))],
            out_shape=[jax.ShapeDtypeStruct((M, K), F32), jax.ShapeDtypeStruct((K, G * N), F32)],
            compiler_params=_params("arbitrary"),
        )(x, w16, *dys)

    @jax.custom_vjp
    def mm(x, w):
        return products(x, w.astype(BF16))

    def fwd(x, w):
        w16 = w.astype(BF16)
        return products(x, w16), (x, w16)

    def bwd(res, dys):
        x, w16 = res
        return tuple(grads(x, w16, tuple(dys)))

    mm.defvjp(fwd, bwd)
    return mm


def _row_specs(rows, params, consts, tile):
    specs = [pl.BlockSpec((tile, r.shape[1]), lambda i: (i, 0)) for r in rows]
    specs += [pl.BlockSpec(p.shape, lambda i: (0, 0)) for p in params]
    specs += [pl.BlockSpec((tile, c.shape[1]), lambda i, n=c.shape[0] // tile: (i % n, 0)) for c in consts]
    return specs


def _row_fwd(name, f, out_widths, tile, rows, params, consts):
    T = rows[0].shape[0]
    nr, npar, ncon = len(rows), len(params), len(consts)

    def body(*refs):
        r = tuple(x[...] for x in refs[:nr])
        p = tuple(x[...] for x in refs[nr:nr + npar])
        c = tuple(x[...] for x in refs[nr + npar:nr + npar + ncon])
        outs = f(r, p, c)
        for o_ref, o in zip(refs[nr + npar + ncon:], outs, strict=True):
            o_ref[...] = o

    return pl.pallas_call(
        body, name=name + "_fwd", grid=(T // tile,),
        in_specs=_row_specs(rows, params, consts, tile),
        out_specs=[pl.BlockSpec((tile, w), lambda i: (i, 0)) for w in out_widths],
        out_shape=[jax.ShapeDtypeStruct((T, w), F32) for w in out_widths],
        compiler_params=_params("parallel"),
    )(*rows, *params, *consts)


def _row_bwd(name, f, out_widths, tile, rows, params, consts, gouts, dr_dtype=F32):
    T = rows[0].shape[0]
    nr, npar, ncon, nout = len(rows), len(params), len(consts), len(out_widths)

    def body(*refs):
        r = tuple(x[...] for x in refs[:nr])
        p = tuple(x[...] for x in refs[nr:nr + npar])
        c = tuple(x[...] for x in refs[nr + npar:nr + npar + ncon])
        k = nr + npar + ncon
        g = tuple(x[...] for x in refs[k:k + nout])
        dr_refs = refs[k + nout:k + nout + nr]
        dp_refs = refs[k + nout + nr:]
        _, vjp = jax.vjp(lambda r_, p_: tuple(f(r_, p_, c)), r, p)
        dr, dp = vjp(g)
        for ref, val in zip(dr_refs, dr, strict=True):
            ref[...] = val.astype(ref.dtype)
        if npar:
            @pl.when(pl.program_id(0) == 0)
            def _():
                for ref in dp_refs:
                    ref[...] = jnp.zeros_like(ref)

            for ref, val in zip(dp_refs, dp, strict=True):
                ref[...] += val

    outs = pl.pallas_call(
        body, name=name + "_bwd", grid=(T // tile,),
        in_specs=_row_specs(rows, params, consts, tile) + [pl.BlockSpec((tile, w), lambda i: (i, 0)) for w in out_widths],
        out_specs=[pl.BlockSpec((tile, r.shape[1]), lambda i: (i, 0)) for r in rows]
        + [pl.BlockSpec(p.shape, lambda i: (0, 0)) for p in params],
        out_shape=[jax.ShapeDtypeStruct(r.shape, dr_dtype) for r in rows] + [jax.ShapeDtypeStruct(p.shape, F32) for p in params],
        compiler_params=_params("arbitrary"),
    )(*rows, *params, *consts, *gouts)
    return tuple(outs[:nr]), tuple(outs[nr:])


def make_proj_stage(name, f, out_widths, tile=ROW_TILE):
    def run(x, w16, params, consts):
        p = _mm_nn(name + "_mm", x, w16)
        return p, tuple(_row_fwd(name, f, out_widths, tile, (p,), params, consts))

    @jax.custom_vjp
    def op(x, w16, c32, c16, params, consts):
        return run(x, w16, params, consts)[1]

    def fwd(x, w16, c32, c16, params, consts):
        p, outs = run(x, w16, params, consts)
        return outs, (x, w16, p, params, consts)

    def bwd(res, g):
        x, w16, p, params, consts = res
        (dp,), dparams = _row_bwd(name, f, out_widths, tile, (p,), params, consts, tuple(g), dr_dtype=BF16)
        d32, d16 = _weight_grads(name + "_dw", x, dp, True)
        return _mm_nt(name + "_dx", dp, w16), jnp.zeros_like(w16), d32, d16, dparams, tuple(jnp.zeros_like(c) for c in consts)

    op.defvjp(fwd, bwd)
    return op


def make_rowop(name, f, out_widths, tile=ROW_TILE):
    @jax.custom_vjp
    def op(rows, params, consts):
        return tuple(_row_fwd(name, f, out_widths, tile, rows, params, consts))

    def fwd(rows, params, consts):
        return op(rows, params, consts), (rows, params, consts)

    def bwd(res, g):
        rows, params, consts = res
        dr, dp = _row_bwd(name, f, out_widths, tile, rows, params, consts, tuple(g))
        return dr, dp, tuple(jnp.zeros_like(c) for c in consts)

    op.defvjp(fwd, bwd)
    return op


def _rms(x, g):
    return x * lax.rsqrt(jnp.mean(x * x, axis=-1, keepdims=True) + EPS) * g


def _silu(x):
    return x * jax.nn.sigmoid(x)


def _bdot(a, b):
    return jnp.dot(a.astype(BF16), b.astype(BF16), preferred_element_type=F32)


def norm_f(rows, params, consts):
    return (_rms(rows[0], params[0]),)


def addnorm_f(rows, params, consts):
    x = rows[0] + rows[1]
    return x, _rms(x, params[0])


EVEN_GLA_END = 1568


def even_prep_f(rows, params, consts):
    (p,) = rows
    wa2f, wa2b, baf, bab, lbf, lbb = params
    gq = p[:, 0:256]
    gk = p[:, 256:512] * (GLA_DK ** -0.5)
    gv = p[:, 512:1024]
    gr = p[:, 1024:1536]
    glaf = jax.nn.log_sigmoid(_bdot(p[:, 1536:1552], wa2f) + baf) / GLA_GATE_NORM
    glab = jax.nn.log_sigmoid(_bdot(p[:, 1552:1568], wa2b) + bab) / GLA_GATE_NORM
    o = EVEN_GLA_END
    hq = _silu(p[:, o:o + 256])

    def gate(z, lb):
        f = lb + (1.0 - lb) * jax.nn.sigmoid(z)
        return jnp.log(jnp.maximum(f, HGRN_MIN_F)), (1.0 - lb) * jax.nn.sigmoid(-z)

    hlaf, hkf = gate(p[:, o + 256:o + 512], lbf)
    hlab, hkb = gate(p[:, o + 512:o + 768], lbb)
    hv = p[:, o + 768:o + 1280]
    hg = p[:, o + 1280:o + 1792]
    return gq, gk, gv, gq, gk, gv, glaf, glab, gr, hq, hkf, hv, hq, hkb, hv, hlaf, hlab, hg


EVEN_PREP_WIDTHS = (256, 256, 512, 256, 256, 512, 256, 256, 512, 256, 256, 512, 256, 256, 512, 256, 256, 512)


def _head_rms(o, g, heads, d):
    parts = []
    for h in range(heads):
        seg = o[:, h * d:(h + 1) * d]
        parts.append(seg * lax.rsqrt(jnp.mean(seg * seg, axis=-1, keepdims=True) + EPS))
    return jnp.concatenate(parts, axis=1) * g


def even_post_f(rows, params, consts):
    of, ob, hof, hob, gr, hg = rows
    a = _head_rms(of + ob, params[0], GLA_HEADS, GLA_DV) * _silu(gr)
    b = _head_rms(hof + hob, params[1], HGRN_HEADS, HGRN_DV) * _silu(hg)
    return (jnp.concatenate([a, b], axis=1),)


@jax.custom_vjp
def _roll_half(x):
    return pltpu.roll(x, RET_DK // 2, 1)


_roll_half.defvjp(lambda x: (_roll_half(x), None), lambda _, g: (_roll_half(g),))


def odd_prep_f(rows, params, consts):
    (p,) = rows
    cosf, sinf = consts

    def rot(t):
        parts = []
        for h in range(RET_HEADS):
            th = t[:, h * RET_DK:(h + 1) * RET_DK]
            parts.append(th * cosf + _roll_half(th) * sinf)
        return jnp.concatenate(parts, axis=1)

    rq = rot(p[:, 0:512])
    rk = rot(p[:, 512:1024]) * (RET_DK ** -0.5)
    rv = p[:, 1024:1792]
    return rq, rk, rv, rq, rk, rv, p[:, 1792:2560], p[:, 2560:2816]


ODD_PREP_WIDTHS = (512, 512, 768, 512, 512, 768, 768, 256)


def ret_post_f(rows, params, consts):
    of, ob, rg = rows
    o = of + ob
    parts = []
    for h in range(RET_HEADS):
        seg = o[:, h * RET_DV:(h + 1) * RET_DV]
        c = seg - jnp.mean(seg, axis=-1, keepdims=True)
        parts.append(c * lax.rsqrt(jnp.mean(c * c, axis=-1, keepdims=True) + EPS))
    return (jnp.concatenate(parts, axis=1) * params[0] * _silu(rg),)


def s5_post_f(rows, params, consts):
    h0r, h0i, h1r, h1i, u = rows
    c_re, c_im, d_skip, glu_w, glu_b = params
    hr = h0r + h1r
    hi = h0i + h1i
    y = _bdot(hr, c_re) - _bdot(hi, c_im) + d_skip * u
    g = jax.nn.gelu(y)
    return (g * jax.nn.sigmoid(_bdot(g, glu_w) + glu_b),)


def loss_head(x, r, g, target, tile=ROW_TILE):
    T, D = x.shape

    def body(x_ref, r_ref, g_ref, t_ref, loss_ref, dx_ref, dg_ref):
        t = t_ref[...]

        def lf(xv, gv):
            e = _rms(xv, gv) - t
            row = jnp.sum(e * e, axis=-1, keepdims=True) * (0.5 / D)
            return jnp.sum(row, axis=0, keepdims=True)

        l, vjp = jax.vjp(lf, x_ref[...] + r_ref[...], g_ref[...])
        dx, dg = vjp(jnp.ones((1, 1), F32))
        dx_ref[...] = dx

        @pl.when(pl.program_id(0) == 0)
        def _():
            loss_ref[...] = jnp.zeros_like(loss_ref)
            dg_ref[...] = jnp.zeros_like(dg_ref)

        loss_ref[...] += jnp.broadcast_to(l, loss_ref.shape)
        dg_ref[...] += dg

    row = pl.BlockSpec((tile, D), lambda i: (i, 0))
    vec = pl.BlockSpec((1, D), lambda i: (0, 0))
    return pl.pallas_call(
        body, name="loss_head", grid=(T // tile,),
        in_specs=[row, row, vec, row],
        out_specs=[pl.BlockSpec((1, LANES), lambda i: (0, 0)), row, vec],
        out_shape=[jax.ShapeDtypeStruct((1, LANES), F32), jax.ShapeDtypeStruct((T, D), F32), jax.ShapeDtypeStruct((1, D), F32)],
        compiler_params=_params("arbitrary"),
    )(x, r, g, target)


SUBLANES = 8


def _halo_specs(width, tile, T):
    n8 = tile // SUBLANES
    last = T // SUBLANES - 1
    return [pl.BlockSpec((tile, width), lambda i: (i, 0)),
            pl.BlockSpec((SUBLANES, width), lambda i: (jnp.maximum(i * n8 - 1, 0), 0)),
            pl.BlockSpec((SUBLANES, width), lambda i: (jnp.minimum((i + 1) * n8, last), 0))]


def _shift_rows(x, prev_row, next_row, tile):
    row = lax.broadcasted_iota(jnp.int32, (tile, 1), 0)
    down = jnp.where(row == 0, prev_row, pltpu.roll(x, 1, 0))
    up = jnp.where(row == tile - 1, next_row, pltpu.roll(x, tile - 1, 0))
    return down, up


def _conv_fwd(name, u, cw, cb, S, tile):
    T, F2 = u.shape
    F = F2 // 2
    per_seq = S // tile

    def body(u_ref, up_ref, un_ref, cw_ref, cb_ref, g_ref):
        pos = pl.program_id(0) % per_seq
        uv = u_ref[...]
        prev_row = jnp.where(pos == 0, 0.0, up_ref[SUBLANES - 1:SUBLANES, :])
        next_row = jnp.where(pos == per_seq - 1, 0.0, un_ref[0:1, :])
        down, up = _shift_rows(uv, prev_row, next_row, tile)
        c = _conv_taps(down, uv, up, cw_ref, cb_ref)
        g_ref[...] = (_silu(c[:, :F]) * c[:, F:]).astype(BF16)

    return pl.pallas_call(
        body, name=name + "_fwd", grid=(T // tile,),
        in_specs=_halo_specs(F2, tile, T) + [pl.BlockSpec((3, F2), lambda i: (0, 0)), pl.BlockSpec((1, F2), lambda i: (0, 0))],
        out_specs=pl.BlockSpec((tile, F), lambda i: (i, 0)),
        out_shape=jax.ShapeDtypeStruct((T, F), BF16),
        compiler_params=_params("parallel"),
    )(u, u, u, cw, cb)


def _conv_taps(down, mid, up, cw_ref, cb_ref):
    c = cb_ref[...] + down * cw_ref[0:1, :]
    c = c + mid * cw_ref[1:2, :]
    return c + up * cw_ref[2:3, :]


def _conv_bwd(name, u, dg, cw, cb, S, tile):
    T, F2 = u.shape
    F = F2 // 2
    per_seq = S // tile

    def dact(cv, dgv):
        a, v = cv[:, :F], cv[:, F:]
        sg = jax.nn.sigmoid(a)
        return jnp.concatenate([dgv * v * (sg * (1.0 + a * (1.0 - sg))), dgv * (a * sg)], axis=1)

    def body(u_ref, up_ref, un_ref, g_ref, gp_ref, gn_ref, cw_ref, cb_ref, du_ref, dw0_ref, dw1_ref, dw2_ref, db_ref):
        i = pl.program_id(0)
        pos = i % per_seq
        first, last = pos == 0, pos == per_seq - 1
        lo, hi = slice(SUBLANES - 1, SUBLANES), slice(0, 1)
        uv = u_ref[...]
        u_m1, u_p1 = jnp.where(first, 0.0, up_ref[lo, :]), jnp.where(last, 0.0, un_ref[hi, :])
        u_dn, u_up = _shift_rows(uv, u_m1, u_p1, tile)
        dc = dact(_conv_taps(u_dn, uv, u_up, cw_ref, cb_ref), g_ref[...])
        c_m1 = _conv_taps(up_ref[SUBLANES - 2:SUBLANES - 1, :], u_m1, u_ref[0:1, :], cw_ref, cb_ref)
        c_p1 = _conv_taps(u_ref[tile - 1:tile, :], u_p1, un_ref[1:2, :], cw_ref, cb_ref)
        dc_prev = jnp.where(first, 0.0, dact(c_m1, gp_ref[lo, :]))
        dc_next = jnp.where(last, 0.0, dact(c_p1, gn_ref[hi, :]))
        dc_dn, dc_up = _shift_rows(dc, dc_prev, dc_next, tile)
        du = dc_up * cw_ref[0:1, :]
        du = du + dc * cw_ref[1:2, :]
        du_ref[...] = (du + dc_dn * cw_ref[2:3, :]).astype(BF16)

        @pl.when(i == 0)
        def _():
            for ref in (dw0_ref, dw1_ref, dw2_ref, db_ref):
                ref[...] = jnp.zeros_like(ref)

        dw0_ref[...] += jnp.sum(dc * u_dn, axis=0, keepdims=True)
        dw1_ref[...] += jnp.sum(dc * uv, axis=0, keepdims=True)
        dw2_ref[...] += jnp.sum(dc * u_up, axis=0, keepdims=True)
        db_ref[...] += jnp.sum(dc, axis=0, keepdims=True)

    vec = pl.BlockSpec((1, F2), lambda i: (0, 0))
    du, dw0, dw1, dw2, db = pl.pallas_call(
        body, name=name + "_bwd", grid=(T // tile,),
        in_specs=_halo_specs(F2, tile, T) + _halo_specs(F, tile, T) + [pl.BlockSpec((3, F2), lambda i: (0, 0)), vec],
        out_specs=[pl.BlockSpec((tile, F2), lambda i: (i, 0)), vec, vec, vec, vec],
        out_shape=[jax.ShapeDtypeStruct((T, F2), BF16)] + [jax.ShapeDtypeStruct((1, F2), F32)] * 4,
        compiler_params=_params("arbitrary"),
    )(u, u, u, dg, dg, dg, cw, cb)
    return du, jnp.concatenate([dw0, dw1, dw2], axis=0), db


def make_ffn(name, S):
    def run(x, wup16, wdn16, cw, cb):
        u = _mm_nn(name + "_up", x, wup16)
        g16 = _conv_fwd(name + "_conv", u, cw, cb, S, ROW_TILE)
        return u, g16, _mm_nn(name + "_down", g16, wdn16)

    @jax.custom_vjp
    def op(x, wup16, cu32, cu16, wdn16, cd32, cd16, cw, cb):
        return run(x, wup16, wdn16, cw, cb)[2]

    def fwd(x, wup16, cu32, cu16, wdn16, cd32, cd16, cw, cb):
        u, g16, out = run(x, wup16, wdn16, cw, cb)
        return out, (x, wup16, wdn16, u, g16, cw, cb)

    def bwd(res, dout):
        x, wup16, wdn16, u, g16, cw, cb = res
        dg = _mm_nt(name + "_down_dx", dout, wdn16)
        dd32, dd16 = _weight_grads(name + "_down_dw", g16, dout, False)
        du16, dcw, dcb = _conv_bwd(name + "_conv", u, dg, cw, cb, S, ROW_TILE)
        du32, du16w = _weight_grads(name + "_up_dw", x, du16, True)
        return (_mm_nt(name + "_up_dx", du16, wup16), jnp.zeros_like(wup16), du32, du16w, jnp.zeros_like(wdn16), dd32, dd16, dcw, dcb)

    op.defvjp(fwd, bwd)
    return op


def _dot_nt(a, b):
    return lax.dot_general(a.astype(BF16), b.astype(BF16), (((1,), (1,)), ((), ())), preferred_element_type=F32)


def _dot_tn(a, b):
    return lax.dot_general(a.astype(BF16), b.astype(BF16), (((0,), (0,)), ((), ())), preferred_element_type=F32)


def _chunk_decays(la, direction, C, width, dk, const_lg):
    row = lax.broadcasted_iota(jnp.int32, (C, C), 0)
    col = lax.broadcasted_iota(jnp.int32, (C, C), 1)
    keep = (row >= col) if direction == 0 else (row <= col)
    ridx = lax.broadcasted_iota(jnp.int32, (C, 1), 0)
    if const_lg is None:
        cum = jnp.dot(keep.astype(F32), la, precision=HIGHEST, preferred_element_type=F32)
    else:
        lane_head = lax.broadcasted_iota(jnp.int32, (1, width), 1) // dk
        lg = jnp.zeros((1, width), F32)
        for h, val in enumerate(const_lg):
            lg = jnp.where(lane_head == h, val, lg)
        steps = (ridx + 1) if direction == 0 else (C - ridx)
        cum = steps.astype(F32) * lg
    exit_row = C - 1 if direction == 0 else 0
    mid = jnp.sum(jnp.where(ridx == C // 2, cum, 0.0), axis=0, keepdims=True)
    last = jnp.sum(jnp.where(ridx == exit_row, cum, 0.0), axis=0, keepdims=True)
    return keep, ridx == exit_row, cum, mid, last


def _scan_fwd(name, qkv, laf, lab, H, dk, dv, S, C, const_lg):
    qf, kf, vf, qb, kb, vb = qkv
    T = qf.shape[0]
    B, nc = T // S, S // C
    Wk, Wv = H * dk, H * dv
    learn = const_lg is None

    def body(*refs):
        @pl.when(pl.program_id(0) == 0)
        def _():
            refs[-1][...] = jnp.zeros_like(refs[-1])

        for b in range(B):
            one_sequence(*[r.at[b] for r in refs])

    def one_sequence(*refs):
        if learn:
            qf_r, qb_r, kf_r, kb_r, vf_r, vb_r, laf_r, lab_r, of_r, ob_r, sf_r, sb_r, st = refs
            las = (laf_r[...], lab_r[...])
        else:
            qf_r, qb_r, kf_r, kb_r, vf_r, vb_r, of_r, ob_r, sf_r, sb_r, st = refs
            las = (None, None)

        for d, (q_r, k_r, v_r, o_r, s_r) in enumerate(((qf_r, kf_r, vf_r, of_r, sf_r), (qb_r, kb_r, vb_r, ob_r, sb_r))):
            keep, _, cum, mid, last = _chunk_decays(las[d], d, C, Wk, dk, None if learn else const_lg[d])
            qe = q_r[...] * jnp.exp(cum - mid)
            ke = k_r[...] * jnp.exp(mid - cum)
            q_in = qe * jnp.exp(mid)
            k_out = ke * jnp.exp(last - mid)
            e_last = jnp.exp(last)
            vv = v_r[...]
            for h in range(H):
                ks, vs = slice(h * dk, (h + 1) * dk), slice(h * dv, (h + 1) * dv)
                a = jnp.where(keep, _dot_nt(qe[:, ks], ke[:, ks]), 0.0)
                state = st[d, h]
                o_r[:, vs] = _bdot(a, vv[:, vs]) + _dot_nt(q_in[:, ks], state)
                s_r[h * dv:(h + 1) * dv, :] = state
                st[d, h] = state * e_last[:, ks] + _dot_tn(vv[:, vs], k_out[:, ks])

    fpos = lambda c: c
    bpos = lambda c: nc - 1 - c
    kspec = lambda pos: pl.BlockSpec((B, C, Wk), lambda c: (0, pos(c), 0))
    vspec = lambda pos: pl.BlockSpec((B, C, Wv), lambda c: (0, pos(c), 0))
    sspec = lambda pos: pl.BlockSpec((B, None, Wv, dk), lambda c: (0, pos(c), 0, 0))
    seq = lambda t: t.reshape(B, S, t.shape[1])
    ins = [seq(t) for t in [qf, qb, kf, kb, vf, vb] + ([laf, lab] if learn else [])]
    in_specs = [kspec(fpos), kspec(bpos), kspec(fpos), kspec(bpos), vspec(fpos), vspec(bpos)] + ([kspec(fpos), kspec(bpos)] if learn else [])
    of, ob, sf, sb = pl.pallas_call(
        body, name=name + "_fwd", grid=(nc,), in_specs=in_specs,
        out_specs=[vspec(fpos), vspec(bpos), sspec(fpos), sspec(bpos)],
        out_shape=[jax.ShapeDtypeStruct((B, S, Wv), F32)] * 2 + [jax.ShapeDtypeStruct((B, nc, Wv, dk), F32)] * 2,
        scratch_shapes=[pltpu.VMEM((B, 2, H, dv, dk), F32)],
        compiler_params=_params("arbitrary"),
    )(*ins)
    return of.reshape(T, Wv), ob.reshape(T, Wv), sf, sb


def _scan_bwd(name, qkv, laf, lab, sf, sb, dof, dob, H, dk, dv, S, C, const_lg):
    qf, kf, vf, qb, kb, vb = qkv
    T = qf.shape[0]
    B, nc = T // S, S // C
    Wk, Wv = H * dk, H * dv
    learn = const_lg is None

    def body(*refs):
        @pl.when(pl.program_id(0) == 0)
        def _():
            refs[-1][...] = jnp.zeros_like(refs[-1])

        for b in range(B):
            one_sequence(*[r.at[b] for r in refs])

    def one_sequence(*refs):
        if learn:
            (qf_r, qb_r, kf_r, kb_r, vf_r, vb_r, laf_r, lab_r, sf_r, sb_r, dof_r, dob_r,
             dqf_r, dqb_r, dkf_r, dkb_r, dvf_r, dvb_r, dlaf_r, dlab_r, dst) = refs
            las, dlas = (laf_r[...], lab_r[...]), (dlaf_r, dlab_r)
        else:
            (qf_r, qb_r, kf_r, kb_r, vf_r, vb_r, sf_r, sb_r, dof_r, dob_r,
             dqf_r, dqb_r, dkf_r, dkb_r, dvf_r, dvb_r, dst) = refs
            las, dlas = (None, None), (None, None)

        groups = ((qf_r, kf_r, vf_r, sf_r, dof_r, dqf_r, dkf_r, dvf_r), (qb_r, kb_r, vb_r, sb_r, dob_r, dqb_r, dkb_r, dvb_r))
        for d, (q_r, k_r, v_r, s_r, do_r, dq_r, dk_r, dv_r) in enumerate(groups):
            keep, is_exit, cum, mid, last = _chunk_decays(las[d], d, C, Wk, dk, None if learn else const_lg[d])
            eq, ek = jnp.exp(cum - mid), jnp.exp(mid - cum)
            e_in, e_out, e_last = jnp.exp(mid), jnp.exp(last - mid), jnp.exp(last)
            qe, ke = q_r[...] * eq, k_r[...] * ek
            q_in, k_out = qe * e_in, ke * e_out
            vv, do = v_r[...], do_r[...]
            dqe_parts, dke_parts, dlast_parts = [], [], []
            for h in range(H):
                ks, vs = slice(h * dk, (h + 1) * dk), slice(h * dv, (h + 1) * dv)
                a = jnp.where(keep, _dot_nt(qe[:, ks], ke[:, ks]), 0.0)
                dp = jnp.where(keep, _dot_nt(do[:, vs], vv[:, vs]), 0.0)
                s_prev = s_r[h * dv:(h + 1) * dv, :]
                ds = dst[d, h]
                dk_out = _bdot(vv[:, vs], ds)
                dqe_parts.append(_bdot(dp, ke[:, ks]) + _bdot(do[:, vs], s_prev) * e_in[:, ks])
                dke_parts.append(_dot_tn(dp, qe[:, ks]) + dk_out * e_out[:, ks])
                dv_r[:, vs] = _dot_tn(a, do[:, vs]) + _dot_nt(k_out[:, ks], ds)
                if learn:
                    dlast_parts.append(jnp.sum(dk_out * k_out[:, ks], axis=0, keepdims=True)
                                       + jnp.sum(ds * s_prev, axis=0, keepdims=True) * e_last[:, ks])
                dst[d, h] = ds * e_last[:, ks] + _dot_tn(do[:, vs], q_in[:, ks])
            dqe = jnp.concatenate(dqe_parts, axis=1)
            dke = jnp.concatenate(dke_parts, axis=1)
            dq_r[...] = dqe * eq
            dk_r[...] = dke * ek
            if learn:
                dcum = dqe * qe - dke * ke + jnp.where(is_exit, jnp.concatenate(dlast_parts, axis=1), 0.0)
                dlas[d][...] = lax.dot_general(keep.astype(F32), dcum, (((0,), (0,)), ((), ())), precision=HIGHEST,
                                               preferred_element_type=F32)

    fpos = lambda c: nc - 1 - c
    bpos = lambda c: c
    kspec = lambda pos: pl.BlockSpec((B, C, Wk), lambda c: (0, pos(c), 0))
    vspec = lambda pos: pl.BlockSpec((B, C, Wv), lambda c: (0, pos(c), 0))
    sspec = lambda pos: pl.BlockSpec((B, None, Wv, dk), lambda c: (0, pos(c), 0, 0))
    seq = lambda t: t.reshape(B, S, t.shape[1])
    ins = [seq(t) for t in [qf, qb, kf, kb, vf, vb] + ([laf, lab] if learn else [])] + [sf, sb, seq(dof), seq(dob)]
    in_specs = ([kspec(fpos), kspec(bpos), kspec(fpos), kspec(bpos), vspec(fpos), vspec(bpos)]
                + ([kspec(fpos), kspec(bpos)] if learn else []) + [sspec(fpos), sspec(bpos), vspec(fpos), vspec(bpos)])
    out_specs = [kspec(fpos), kspec(bpos), kspec(fpos), kspec(bpos), vspec(fpos), vspec(bpos)] + ([kspec(fpos), kspec(bpos)] if learn else [])
    out_shape = ([jax.ShapeDtypeStruct((B, S, Wk), F32)] * 4 + [jax.ShapeDtypeStruct((B, S, Wv), F32)] * 2
                 + ([jax.ShapeDtypeStruct((B, S, Wk), F32)] * 2 if learn else []))
    outs = pl.pallas_call(
        body, name=name + "_bwd", grid=(nc,), in_specs=in_specs, out_specs=out_specs, out_shape=out_shape,
        scratch_shapes=[pltpu.VMEM((B, 2, H, dv, dk), F32)],
        compiler_params=_params("arbitrary"),
    )(*ins)
    return [t.reshape(T, t.shape[2]) for t in outs]


def make_scan(name, H, dk, dv, S, C=SCAN_CHUNK, const_lg=None):
    if const_lg is None:
        @jax.custom_vjp
        def op(qkv, laf, lab):
            return tuple(_scan_fwd(name, qkv, laf, lab, H, dk, dv, S, C, None)[:2])

        def fwd(qkv, laf, lab):
            of, ob, sf, sb = _scan_fwd(name, qkv, laf, lab, H, dk, dv, S, C, None)
            return (of, ob), (qkv, laf, lab, sf, sb)

        def bwd(res, g):
            qkv, laf, lab, sf, sb = res
            dqf, dqb, dkf, dkb, dvf, dvb, dlaf, dlab = _scan_bwd(name, qkv, laf, lab, sf, sb, g[0], g[1], H, dk, dv, S, C, None)
            return (dqf, dkf, dvf, dqb, dkb, dvb), dlaf, dlab
    else:
        @jax.custom_vjp
        def op(qkv):
            return tuple(_scan_fwd(name, qkv, None, None, H, dk, dv, S, C, const_lg)[:2])

        def fwd(qkv):
            of, ob, sf, sb = _scan_fwd(name, qkv, None, None, H, dk, dv, S, C, const_lg)
            return (of, ob), (qkv, sf, sb)

        def bwd(res, g):
            qkv, sf, sb = res
            dqf, dqb, dkf, dkb, dvf, dvb = _scan_bwd(name, qkv, None, None, sf, sb, g[0], g[1], H, dk, dv, S, C, const_lg)
            return ((dqf, dkf, dvf, dqb, dkb, dvb),)

    op.defvjp(fwd, bwd)
    return op


def _reorder_call(name, t, S, to_segments):
    T, w = t.shape
    n_it = S // S5_SEGMENTS

    def body(x_ref, o_ref):
        def step(i, carry):
            packed = pl.ds(pl.multiple_of(i * S5_SEGMENTS, S5_SEGMENTS), S5_SEGMENTS)
            spread = pl.ds(i, S5_SEGMENTS, stride=n_it)
            if to_segments:
                o_ref[packed, :] = x_ref[spread, :]
            else:
                o_ref[spread, :] = x_ref[packed, :]
            return carry

        lax.fori_loop(0, n_it, step, 0, unroll=8)

    blk = pl.BlockSpec((S, LANES), lambda b, j: (b, j))
    return pl.pallas_call(body, name=name, grid=(T // S, w // LANES), in_specs=[blk], out_specs=blk,
                          out_shape=jax.ShapeDtypeStruct(t.shape, t.dtype), compiler_params=_params("parallel", "parallel"))(t)


def make_join(name, S, wa, wb):
    n_it = S // S5_SEGMENTS

    def call(tag, forward, wa, wb, arrs):
        T = arrs[0].shape[0]
        na, nb = wa // LANES, wb // LANES

        def body(*refs):
            j = pl.program_id(1)
            a_ref, b_ref, y_ref = (refs[0], refs[1], refs[2]) if forward else (refs[1], refs[2], refs[0])

            @pl.when(j < na)
            def _():
                if forward:
                    y_ref[...] = a_ref[...]
                else:
                    a_ref[...] = y_ref[...]

            @pl.when(j >= na)
            def _():
                def step(i, carry):
                    packed = pl.ds(pl.multiple_of(i * S5_SEGMENTS, S5_SEGMENTS), S5_SEGMENTS)
                    spread = pl.ds(i, S5_SEGMENTS, stride=n_it)
                    if forward:
                        y_ref[spread, :] = b_ref[packed, :]
                    else:
                        b_ref[packed, :] = y_ref[spread, :]
                    return carry

                lax.fori_loop(0, n_it, step, 0, unroll=8)

        a_spec = pl.BlockSpec((S, LANES), lambda b, j: (b, jnp.minimum(j, na - 1)))
        b_spec = pl.BlockSpec((S, LANES), lambda b, j: (b, jnp.maximum(j - na, 0)))
        y_spec = pl.BlockSpec((S, LANES), lambda b, j: (b, j))
        shapes = [jax.ShapeDtypeStruct((T, wa), F32), jax.ShapeDtypeStruct((T, wb), F32), jax.ShapeDtypeStruct((T, wa + wb), F32)]
        return pl.pallas_call(
            body, name=name + tag, grid=(T // S, na + nb),
            in_specs=[a_spec, b_spec] if forward else [y_spec], out_specs=y_spec if forward else [a_spec, b_spec],
            out_shape=shapes[2] if forward else shapes[:2], compiler_params=_params("parallel", "arbitrary"),
        )(*arrs)

    @jax.custom_vjp
    def op(a, b):
        return call("_fwd", True, wa, wb, (a, b))

    op.defvjp(lambda a, b: (call("_fwd", True, wa, wb, (a, b)), None), lambda _, dy: tuple(call("_bwd", False, wa, wb, (dy,))))
    return op


def make_reorder(name, S, to_segments):
    @jax.custom_vjp
    def op(t):
        return _reorder_call(name, t, S, to_segments)

    op.defvjp(lambda t: (_reorder_call(name, t, S, to_segments), None),
              lambda _, g: (_reorder_call(name + "_bwd", g, S, not to_segments),))
    return op


def _s5_scan_call(name, Xs, A, S, dirs, prev=None):
    with_p = prev is not None
    T, N = Xs[0].shape
    B, nl = T // S, N // LANES
    n_it = S // S5_SEGMENTS
    assert n_it & (n_it - 1) == 0

    def cmul(ar, ai, br, bi):
        return ar * br - ai * bi, ar * bi + ai * br

    def body(*refs):
        x, a_ref = refs[0:4], refs[4]
        if with_p:
            h_prev, x_prev, h, p_ref = refs[5:9], refs[9:13], refs[13:17], refs[17]
        else:
            h = refs[5:9]
        seg = lax.broadcasted_iota(jnp.int32, (S5_SEGMENTS, 1), 0)
        zero = jnp.zeros((S5_SEGMENTS, LANES), F32)
        a = [(jnp.broadcast_to(a_ref[2 * k:2 * k + 1, :], (S5_SEGMENTS, LANES)),
              jnp.broadcast_to(a_ref[2 * k + 1:2 * k + 2, :], (S5_SEGMENTS, LANES))) for k in range(2)]

        def rows_of(k, i):
            return pl.ds(pl.multiple_of(((n_it - 1 - i) if dirs[k] == 1 else i) * S5_SEGMENTS, S5_SEGMENTS), S5_SEGMENTS)

        def local(i, carry):
            out = []
            for k, (sr, si) in enumerate(carry):
                rows = rows_of(k, i)
                pr, pi = cmul(*a[k], sr, si)
                sr, si = pr + x[2 * k][rows, :], pi + x[2 * k + 1][rows, :]
                h[2 * k][rows, :] = sr
                h[2 * k + 1][rows, :] = si
                out.append((sr, si))
            return tuple(out)

        ends = lax.fori_loop(0, n_it, local, ((zero, zero), (zero, zero)), unroll=8)
        inherit = []
        for k, (er, ei) in enumerate(ends):
            back = dirs[k] == 1
            pr, pi = a[k]
            for _ in range(n_it.bit_length() - 1):
                pr, pi = cmul(pr, pi, pr, pi)
            shift = (S5_SEGMENTS - 1) if back else 1
            tr, ti = er, ei
            for r in (range(S5_SEGMENTS - 2, -1, -1) if back else range(1, S5_SEGMENTS)):
                nr, ni = cmul(pr, pi, pltpu.roll(tr, shift, 0), pltpu.roll(ti, shift, 0))
                tr = jnp.where(seg == r, er + nr, tr)
                ti = jnp.where(seg == r, ei + ni, ti)
            edge = (S5_SEGMENTS - 1) if back else 0
            inherit.append((jnp.where(seg == edge, 0.0, pltpu.roll(tr, shift, 0)), jnp.where(seg == edge, 0.0, pltpu.roll(ti, shift, 0))))

        def fix(i, carry):
            powers, sums = carry
            new_powers, new_sums = [], []
            for k in range(2):
                rows = rows_of(k, i)
                fr, fi = cmul(*powers[k], *inherit[k])
                sr, si = h[2 * k][rows, :] + fr, h[2 * k + 1][rows, :] + fi
                h[2 * k][rows, :] = sr
                h[2 * k + 1][rows, :] = si
                new_powers.append(cmul(*powers[k], *a[k]))
                if with_p:
                    ur = h_prev[2 * k][rows, :] - x_prev[2 * k][rows, :]
                    ui = h_prev[2 * k + 1][rows, :] - x_prev[2 * k + 1][rows, :]
                    new_sums.append((sums[k][0] + sr * ur + si * ui, sums[k][1] + si * ur - sr * ui))
            return tuple(new_powers), tuple(new_sums)

        _, sums = lax.fori_loop(0, n_it, fix, ((a[0], a[1]), ((zero, zero), (zero, zero)) if with_p else ()), unroll=8)
        if with_p:
            for k in range(2):
                p_ref[2 * k:2 * k + 1, :] = jnp.sum(sums[k][0], axis=0, keepdims=True)
                p_ref[2 * k + 1:2 * k + 2, :] = jnp.sum(sums[k][1], axis=0, keepdims=True)

    col = pl.BlockSpec((S, LANES), lambda b, j: (b, j))
    outs = pl.pallas_call(
        body, name=name, grid=(B, nl),
        in_specs=[col] * 4 + [pl.BlockSpec((4, LANES), lambda b, j: (0, j))] + ([col] * 8 if with_p else []),
        out_specs=[col] * 4 + ([pl.BlockSpec((None, 4, LANES), lambda b, j: (b, 0, j))] if with_p else []),
        out_shape=[jax.ShapeDtypeStruct((T, N), F32)] * 4 + ([jax.ShapeDtypeStruct((B, 4, N), F32)] if with_p else []),
        compiler_params=_params("parallel", "parallel"),
    )(*Xs, A, *(prev[0] + prev[1] if with_p else ()))
    return (tuple(outs[:4]), outs[4]) if with_p else tuple(outs)


def make_s5_scan(name, S):
    @jax.custom_vjp
    def op(X, A):
        return _s5_scan_call(name + "_fwd", X, A, S, (0, 1))

    def fwd(X, A):
        H = _s5_scan_call(name + "_fwd", X, A, S, (0, 1))
        return H, (X, A, H)

    def bwd(res, G):
        X, A, H = res
        conj = A * jnp.array([[1.0], [-1.0], [1.0], [-1.0]], F32)
        lam, P = _s5_scan_call(name + "_bwd", tuple(G), conj, S, (1, 0), prev=(tuple(H), tuple(X)))
        P = jnp.sum(P, axis=0)
        ar, ai = A[0::2], A[1::2]
        pr, pi = P[0::2], P[1::2]
        den = ar * ar + ai * ai
        dar, dai = (pr * ar - pi * ai) / den, (pr * ai + pi * ar) / den
        return lam, jnp.stack([dar[0], dai[0], dar[1], dai[1]], axis=0)

    op.defvjp(fwd, bwd)
    return op


ANY = pl.BlockSpec(memory_space=pl.ANY)


def _place():
    x, y, c = lax.axis_index("x"), lax.axis_index("y"), lax.axis_index("c")
    return x, y, c, [(1 - x, y), (x, 1 - y), (1 - x, 1 - y)]


def all_gather(name, arrs):
    n = len(arrs)

    def body(*refs):
        ins, outs = refs[:n], refs[n:2 * n]
        send, recv, lsem = refs[2 * n:]
        x, y, c, chips = _place()
        me, sibling = (x, y, c), (x, y, 1 - c)

        def copy(a, k, block, to, src=None):
            slot = outs[a].at[4 * block[0] + 2 * block[1] + block[2]]
            return pltpu.make_async_remote_copy(src_ref=slot if src is None else src, dst_ref=slot, send_sem=send.at[a, k],
                                                recv_sem=recv.at[a, k], device_id=to, device_id_type=MESH)

        mine = [pltpu.make_async_copy(ins[a], outs[a].at[4 * x + 2 * y + c], lsem.at[a]) for a in range(n)]
        first = []
        for a in range(n):
            mine[a].start()
            first.append(copy(a, 0, me, sibling, src=ins[a]))
            first += [copy(a, 1 + j, me, (*chip, c), src=ins[a]) for j, chip in enumerate(chips)]
        for cp in first:
            cp.start()
        passed = []
        for a in range(n):
            for j, chip in enumerate(chips):
                copy(a, 1 + j, (*chip, c), me).wait_recv()
                fwd = copy(a, 4 + j, (*chip, c), sibling)
                fwd.start()
                passed.append(fwd)
        for a in range(n):
            copy(a, 0, sibling, me).wait_recv()
            for j, chip in enumerate(chips):
                copy(a, 4 + j, (*chip, 1 - c), me).wait_recv()
        for cp in first + passed:
            cp.wait_send()
        for cp in mine:
            cp.wait()

    return pl.pallas_call(
        body, name=name, in_specs=[ANY] * n, out_specs=[ANY] * n,
        out_shape=[jax.ShapeDtypeStruct((N_DEV,) + a.shape, a.dtype) for a in arrs],
        scratch_shapes=[pltpu.SemaphoreType.DMA((n, 7)), pltpu.SemaphoreType.DMA((n, 7)), pltpu.SemaphoreType.DMA((n,))],
    )(*arrs)


HBM = pl.BlockSpec(memory_space=pltpu.HBM)
SEM = pl.BlockSpec(memory_space=pltpu.SEMAPHORE)
EFFECT = pltpu.SideEffectType.DATAFLOW_SIDE_EFFECTING
GATHER_PEERS = (1, 2, 4, 6)
OTHER_CHIPS = (2, 4, 6)
COPIES_PER_ARRAY = {"scatter": N_DEV - 1, "gather": len(GATHER_PEERS), "forward": len(OTHER_CHIPS)}


def _split_plan(mode, srcs, lands, send, recv):
    x, y, c = lax.axis_index("x"), lax.axis_index("y"), lax.axis_index("c")

    def dev(k):
        return (1 - x if k & 4 else x), (1 - y if k & 2 else y), (1 - c if k & 1 else c)

    def idx(d):
        return 4 * d[0] + 2 * d[1] + d[2]

    me = idx((x, y, c))
    plan = []
    for a, land in enumerate(lands):
        if mode == "scatter":
            legs = [(srcs[a].at[idx(dev(k))], land.at[me], land.at[idx(dev(k))], dev(k)) for k in range(1, N_DEV)]
        elif mode == "gather":
            legs = [(srcs[a], land.at[me], land.at[idx(dev(k))], dev(k)) for k in GATHER_PEERS]
        else:
            legs = [(land.at[idx(dev(j))], land.at[idx(dev(j))], land.at[idx(dev(j ^ 1))], dev(1)) for j in OTHER_CHIPS]
        for i, (src, dst, arrival, to) in enumerate(legs):
            sem = a * len(legs) + i
            pair = tuple(pltpu.make_async_remote_copy(src_ref=src, dst_ref=d, send_sem=send.at[sem], recv_sem=recv.at[sem],
                                                      device_id=to, device_id_type=MESH) for d in (dst, arrival))
            plan.append(pair)
    return plan


def split_start(name, mode, srcs, lands, after):
    if lands is None:
        lands = [lax.empty((N_DEV,) + (s.shape[1:] if mode == "scatter" else s.shape), s.dtype) for s in srcs]
    ns, nl = len(srcs), len(lands)
    nsem = COPIES_PER_ARRAY[mode] * nl

    def body(*refs):
        ins, lnd = refs[:ns], refs[ns:ns + nl]
        send, recv = refs[ns + nl + 1], refs[ns + nl + 2]
        token = refs[-1]
        for out, _ in _split_plan(mode, ins, lnd, send, recv):
            out.start()
        token[...] = jnp.zeros_like(token)

    arrs = list(srcs) + list(lands)
    return pl.pallas_call(
        body, name=name,
        out_shape=(pltpu.SemaphoreType.DMA((nsem,)), pltpu.SemaphoreType.DMA((nsem,)))
        + tuple(pltpu.HBM(t.shape, t.dtype) for t in arrs) + (jax.ShapeDtypeStruct((SUBLANES, LANES), F32),),
        in_specs=[HBM] * len(arrs) + [ANY],
        out_specs=(SEM, SEM) + (HBM,) * len(arrs) + (pl.BlockSpec(memory_space=pltpu.VMEM),),
        input_output_aliases={i: 2 + i for i in range(len(arrs))},
        compiler_params=pltpu.CompilerParams(has_side_effects=EFFECT),
    )(*[pltpu.with_memory_space_constraint(t, pltpu.HBM) for t in arrs], after)


def split_wait(name, mode, handle, after):
    send, recv = handle[0], handle[1]
    arrs = list(handle[2:-1])
    nl = len(arrs) if mode == "forward" else len(arrs) // 2
    ns = len(arrs) - nl

    def body(*refs):
        ins, lnd = refs[:ns], refs[ns:ns + nl]
        s, r = refs[ns + nl], refs[ns + nl + 1]
        for out, arrival in _split_plan(mode, ins, lnd, s, r):
            out.wait_send()
            arrival.wait_recv()

    outs = pl.pallas_call(
        body, name=name,
        out_shape=tuple(pltpu.HBM(t.shape, t.dtype) for t in arrs),
        in_specs=[HBM] * len(arrs) + [SEM, SEM, ANY], out_specs=(HBM,) * len(arrs),
        input_output_aliases={i: i for i in range(len(arrs))},
        compiler_params=pltpu.CompilerParams(has_side_effects=EFFECT),
    )(*arrs, send, recv, after)
    return list(outs[ns:])


def _row_tile(rows, cols):
    cap = max(SUBLANES, (2**18 // cols) // SUBLANES * SUBLANES)
    if rows <= cap:
        return rows
    for t in range(cap, SUBLANES - 1, -SUBLANES):
        if rows % t == 0:
            return t
    return rows


def ordered_sum(name, parts):
    n, R, C = parts.shape
    rt = _row_tile(R, C)

    def body(p_ref, o_ref):
        s = p_ref[0]
        for k in range(1, n):
            s = s + p_ref[k]
        o_ref[...] = s

    return pl.pallas_call(
        body, name=name, grid=(R // rt,),
        in_specs=[pl.BlockSpec((n, rt, C), lambda r: (0, r, 0))], out_specs=pl.BlockSpec((rt, C), lambda r: (r, 0)),
        out_shape=jax.ShapeDtypeStruct((R, C), F32), compiler_params=_params("parallel"),
    )(parts)


def _adamw_update(w, m, v, g):
    bias1 = 1.0 - ADAM_B1 ** ADAM_STEP
    bias2 = 1.0 - ADAM_B2 ** ADAM_STEP
    m_new = ADAM_B1 * m + (1.0 - ADAM_B1) * g
    v_new = ADAM_B2 * v + (1.0 - ADAM_B2) * (g * g)
    delta = -ADAM_LR * ((m_new / bias1) / (jnp.sqrt(v_new / bias2) + ADAM_EPS) + ADAM_WD * w)
    return delta, m_new, v_new


def adamw(name, w, m, v, g):
    R, C = w.shape
    rt = _row_tile(R, C)

    def body(w_ref, m_ref, v_ref, g_ref, d_ref, mo_ref, vo_ref):
        d_ref[...], mo_ref[...], vo_ref[...] = _adamw_update(w_ref[...], m_ref[...], v_ref[...], g_ref[...])

    row = pl.BlockSpec((rt, C), lambda r: (r, 0))
    return pl.pallas_call(
        body, name=name, grid=(R // rt,), in_specs=[row] * 4, out_specs=[row] * 3,
        out_shape=[jax.ShapeDtypeStruct((R, C), F32)] * 3, compiler_params=_params("parallel"),
    )(w, m, v, g)


def adamw_sharded(name, layer, w, m, v, own, land, me, prev):
    _, R, C = own.shape
    rt = _row_tile(R, C)
    base = layer * (R // rt)

    def body(me_ref, w_ref, m_ref, v_ref, own_ref, land_ref, *rest):
        go_ref, d_ref, mo_ref, vo_ref = rest[-4:]
        g = own_ref[...]
        for k in range(N_DEV):
            g = g + jnp.where(me_ref[0] == k, 0.0, land_ref[k].astype(F32))
        go_ref[...] = g
        d_ref[...], mo_ref[...], vo_ref[...] = _adamw_update(w_ref[...], m_ref[...], v_ref[...], g)

    row = pl.BlockSpec((rt, C), lambda r, p: (base + r, 0))
    in_specs = [row, row, row, pl.BlockSpec((None, rt, C), lambda r, p: (p[0], r, 0)), pl.BlockSpec((N_DEV, rt, C), lambda r, p: (0, r, 0))]
    ins = [me, w, m, v, own, land]
    aliases = {}
    if prev is not None:
        in_specs += [ANY] * 4
        aliases = {len(ins) + k: k for k in range(4)}
        ins += list(prev)
    return pl.pallas_call(
        body, name=name,
        grid_spec=pltpu.PrefetchScalarGridSpec(num_scalar_prefetch=1, grid=(R // rt,), in_specs=in_specs, out_specs=[row] * 4),
        out_shape=[jax.ShapeDtypeStruct(w.shape, F32)] * 4, input_output_aliases=aliases,
        compiler_params=_params("arbitrary"),
    )(*ins)


def _hgrn_lower_bounds(lb_logits):
    p = jax.nn.softmax(lb_logits, axis=1)
    return jnp.cumsum(p, axis=1) - p[:, :1]


def _s5_discretise(lam_re, lam_im, log_dt, b_re, b_im):
    lr = jnp.minimum(lam_re, -1e-4)
    li = lam_im
    dt = jnp.exp(log_dt)[:, None]
    mag = jnp.exp(lr * dt)
    ar, ai = mag * jnp.cos(li * dt), mag * jnp.sin(li * dt)
    den = lr * lr + li * li
    nr = ar - 1.0
    cr = (nr * lr + ai * li) / den
    ci = (ai * lr - nr * li) / den
    bbr = cr[..., None] * b_re - ci[..., None] * b_im
    bbi = cr[..., None] * b_im + ci[..., None] * b_re
    return ar, ai, bbr, bbi


def _block_diag(t):
    G, a, b = t.shape
    eye = jnp.eye(G, dtype=F32)
    return (t[:, :, None, :] * eye[:, None, :, None]).reshape(G * a, G * b)


def _rope_tables(S):
    half = RET_DK // 2
    inv = ROPE_BASE ** (-jnp.arange(half, dtype=F32) / half)
    ang = jnp.arange(S, dtype=F32)[:, None] * inv[None, :]
    cos, sin = jnp.cos(ang), jnp.sin(ang)
    return jnp.concatenate([cos, cos], axis=1), jnp.concatenate([-sin, sin], axis=1)


def _ret_log_decays():
    f = tuple(float(np.log1p(-np.exp2(np.float32(-5.0 - h)))) for h in range(RET_HEADS))
    b = tuple(float(np.log1p(-np.exp2(np.float32(-5.5 - h)))) for h in range(RET_HEADS))
    return f, b


def assemble_weight(name, land, own, me, col_sharded):
    _, R, C = land.shape
    if col_sharded:
        tr = min(R, 256)

        def body(me_ref, land_ref, own_ref, o_ref):
            for d in range(N_DEV):
                o_ref[:, d * C:(d + 1) * C] = jnp.where(me_ref[0] == d, own_ref[...], land_ref[d])

        grid, out_shape = (R // tr,), (R, N_DEV * C)
        in_specs = [pl.BlockSpec((N_DEV, tr, C), lambda i, p: (0, i, 0)), pl.BlockSpec((tr, C), lambda i, p: (i, 0))]
        out_spec = pl.BlockSpec((tr, N_DEV * C), lambda i, p: (i, 0))
    else:
        def body(me_ref, land_ref, own_ref, o_ref):
            o_ref[...] = jnp.where(me_ref[0] == pl.program_id(0), own_ref[...], land_ref[...])

        grid, out_shape = (N_DEV,), (N_DEV * R, C)
        in_specs = [pl.BlockSpec((None, R, C), lambda d, p: (d, 0, 0)), pl.BlockSpec((R, C), lambda d, p: (0, 0))]
        out_spec = pl.BlockSpec((R, C), lambda d, p: (d, 0))
    return pl.pallas_call(
        body, name=name,
        grid_spec=pltpu.PrefetchScalarGridSpec(num_scalar_prefetch=1, grid=grid, in_specs=in_specs, out_specs=out_spec),
        out_shape=jax.ShapeDtypeStruct(out_shape, land.dtype), compiler_params=_params("parallel"),
    )(me, land, own)


def _row(t):
    return t.reshape(1, -1)


def mixer_stage(layer, S, resid, branch, w_in, w_out, carriers, small):
    j = layer // 2
    tag = f"l{layer}"
    g = _row(small["mix_norm_g"][layer])
    if branch is None:
        x = resid
        (h,) = make_rowop(tag + "_norm", norm_f, (D_MODEL,))((x,), (g,), ())
    else:
        x, h = make_rowop(tag + "_addnorm", addnorm_f, (D_MODEL, D_MODEL))((resid, branch), (g,), ())
    if layer % 2 == 0:
        lbs = _hgrn_lower_bounds(small["hgrn_lb_logits"])
        prm = (small["gla_wa2"][j, 0], small["gla_wa2"][j, 1], _row(small["gla_ba"][j, 0]), _row(small["gla_ba"][j, 1]),
               _row(lbs[0, j]), _row(lbs[1, j]))
        outs = make_proj_stage(tag + "_prep", even_prep_f, EVEN_PREP_WIDTHS)(h, w_in, carriers["in32"], carriers["in16"], prm, ())
        gla_qkv, (glaf, glab, gr), hgrn_qkv, (hlaf, hlab, hg) = outs[0:6], outs[6:9], outs[9:15], outs[15:18]
        of, ob = make_scan(tag + "_gla", GLA_HEADS, GLA_DK, GLA_DV, S, 2 * SCAN_CHUNK)(gla_qkv, glaf, glab)
        hof, hob = make_scan(tag + "_hgrn", HGRN_HEADS, HGRN_DK, HGRN_DV, S, SCAN_CHUNK)(hgrn_qkv, hlaf, hlab)
        (y,) = make_rowop(tag + "_post", even_post_f, (D_MODEL,))(
            (of, ob, hof, hob, gr, hg), (_row(small["gla_norm_g"][j]), _row(small["hgrn_norm_g"][j])), ())
    else:
        cosf, sinf = _rope_tables(S)
        outs = make_proj_stage(tag + "_prep", odd_prep_f, ODD_PREP_WIDTHS)(h, w_in, carriers["in32"], carriers["in16"], (), (cosf, sinf))
        ret_qkv, rg, su = outs[0:6], outs[6], outs[7]
        of, ob = make_scan(tag + "_ret", RET_HEADS, RET_DK, RET_DV, S, 2 * SCAN_CHUNK, const_lg=_ret_log_decays())(ret_qkv)
        (cm,) = make_rowop(tag + "_retpost", ret_post_f, (RET_HEADS * RET_DV,))((of, ob, rg), (_row(small["ret_norm_g"][j]),), ())
        disc = [_s5_discretise(small["s5_lam_re"][j, d], small["s5_lam_im"][j, d], small["s5_log_dt"][j, d],
                               small["s5_b_re"][j], small["s5_b_im"][j]) for d in range(2)]
        a4 = jnp.stack([t.reshape(-1) for d in range(2) for t in disc[d][:2]], axis=0)
        su_p = make_reorder(tag + "_s5seg", S, True)(su)
        bd = jnp.concatenate([_block_diag(jnp.swapaxes(disc[d][2 + i], 1, 2)) for d in range(2) for i in range(2)], axis=1)
        Xs = make_mm_groups(tag + "_s5in", 4)(su_p, bd)
        Hs = make_s5_scan(tag + "_s5scan", S)(Xs, a4)
        prm = (_block_diag(jnp.swapaxes(small["s5_c_re"][j], 1, 2)), _block_diag(jnp.swapaxes(small["s5_c_im"][j], 1, 2)),
               _row(small["s5_d"][j]), small["s5_glu_w"][j], _row(small["s5_glu_b"][j]))
        (dm_p,) = make_rowop(tag + "_s5post", s5_post_f, (S5_WIDTH,))((*Hs, su_p), prm, ())
        y = make_join(tag + "_join", S, RET_HEADS * RET_DV, S5_WIDTH)(cm, dm_p)
    return x, make_mm(tag + "_out")(y, w_out, carriers["out32"], carriers["out16"])


def ffn_stage(layer, S, resid, branch, w_up, w_down, carriers, small):
    tag = f"l{layer}"
    x, hf = make_rowop(tag + "_ffnnorm", addnorm_f, (D_MODEL, D_MODEL))((resid, branch), (_row(small["ffn_norm_g"][layer]),), ())
    out = make_ffn(tag + "_ffn", S)(hf, w_up, carriers["up32"], carriers["up16"], w_down, carriers["down32"], carriers["down16"],
                                   small["ffn_conv_w"][layer], _row(small["ffn_conv_b"][layer]))
    return x, out


BIG = {"w_in_even": 2, "w_out_even": 1, "w_in_odd": 2, "w_out_odd": 1, "ffn_w_up": 2, "ffn_w_down": 1}
SMALL_SHARDED = {"gla_wa2": 3, "gla_ba": 2, "hgrn_lb_logits": 2, "ret_norm_g": 1, "s5_d": 1, "s5_glu_w": 1, "s5_glu_b": 1,
                 "ffn_conv_w": 2}
REPLICATED = ("mix_norm_g", "ffn_norm_g", "final_norm_g", "gla_norm_g", "hgrn_norm_g", "s5_lam_re", "s5_lam_im", "s5_log_dt",
              "s5_b_re", "s5_b_im", "s5_c_re", "s5_c_im", "ffn_conv_b")
WEIGHTS = ("mix_norm_g", "ffn_norm_g", "final_norm_g", "w_in_even", "w_out_even", "gla_wa2", "gla_ba", "gla_norm_g",
           "hgrn_lb_logits", "hgrn_norm_g", "w_in_odd", "w_out_odd", "ret_norm_g", "s5_lam_re", "s5_lam_im", "s5_log_dt",
           "s5_b_re", "s5_b_im", "s5_c_re", "s5_c_im", "s5_d", "s5_glu_w", "s5_glu_b", "ffn_w_up", "ffn_conv_w", "ffn_conv_b",
           "ffn_w_down")
PACK_COLS = LANES
MIXER_SMALL = (("mix_norm_g", "hgrn_lb_logits", "gla_wa2", "gla_ba", "gla_norm_g", "hgrn_norm_g"),
               ("mix_norm_g", "ret_norm_g", "s5_lam_re", "s5_lam_im", "s5_log_dt", "s5_b_re", "s5_b_im", "s5_c_re", "s5_c_im",
                "s5_d", "s5_glu_w", "s5_glu_b"))
FFN_SMALL = ("ffn_norm_g", "ffn_conv_w", "ffn_conv_b")


def _unshard(g, axis):
    t = jnp.moveaxis(g, 0, axis)
    return t.reshape(t.shape[:axis] + (t.shape[axis] * t.shape[axis + 1],) + t.shape[axis + 2:])


def _to_blocks(full, axis):
    t = full.reshape(full.shape[:axis] + (N_DEV, full.shape[axis] // N_DEV) + full.shape[axis + 1:])
    return jnp.moveaxis(t, axis, 0)


def _pack_rows(shape):
    return -(-int(np.prod(shape)) // (SUBLANES * PACK_COLS)) * SUBLANES


def _pack(arrs):
    parts = []
    for a in arrs:
        rows = _pack_rows(a.shape)
        parts.append(jnp.pad(a.reshape(-1), (0, rows * PACK_COLS - a.size)).reshape(rows, PACK_COLS))
    return jnp.concatenate(parts, axis=0)


def _unpack(packed, shapes):
    lead = packed.shape[:-2]
    out, r = [], 0
    for s in shapes:
        rows, n = _pack_rows(s), int(np.prod(s))
        piece = packed[..., r:r + rows, :].reshape(lead + (rows * PACK_COLS,))
        out.append(piece[..., :n].reshape(lead + tuple(s)))
        r += rows
    return out


def _flat2d(a):
    return a.reshape(-1, a.shape[-1])


def kernel(x, mix_norm_g, ffn_norm_g, final_norm_g, w_in_even, w_out_even, gla_wa2, gla_ba, gla_norm_g, hgrn_lb_logits, hgrn_norm_g, w_in_odd, w_out_odd, ret_norm_g, s5_lam_re, s5_lam_im, s5_log_dt, s5_b_re, s5_b_im, s5_c_re, s5_c_im, s5_d, s5_glu_w, s5_glu_b, ffn_w_up, ffn_conv_w, ffn_conv_b, ffn_w_down, loss_target, m_mix_norm_g, m_ffn_norm_g, m_final_norm_g, m_w_in_even, m_w_out_even, m_gla_wa2, m_gla_ba, m_gla_norm_g, m_hgrn_lb_logits, m_hgrn_norm_g, m_w_in_odd, m_w_out_odd, m_ret_norm_g, m_s5_lam_re, m_s5_lam_im, m_s5_log_dt, m_s5_b_re, m_s5_b_im, m_s5_c_re, m_s5_c_im, m_s5_d, m_s5_glu_w, m_s5_glu_b, m_ffn_w_up, m_ffn_conv_w, m_ffn_conv_b, m_ffn_w_down, v_mix_norm_g, v_ffn_norm_g, v_final_norm_g, v_w_in_even, v_w_out_even, v_gla_wa2, v_gla_ba, v_gla_norm_g, v_hgrn_lb_logits, v_hgrn_norm_g, v_w_in_odd, v_w_out_odd, v_ret_norm_g, v_s5_lam_re, v_s5_lam_im, v_s5_log_dt, v_s5_b_re, v_s5_b_im, v_s5_c_re, v_s5_c_im, v_s5_d, v_s5_glu_w, v_s5_glu_b, v_ffn_w_up, v_ffn_conv_w, v_ffn_conv_b, v_ffn_w_down):
    args = locals()
    w = {n: args[n] for n in WEIGHTS}
    m = {n: args["m_" + n] for n in WEIGHTS}
    v = {n: args["v_" + n] for n in WEIGHTS}
    Bl, S, D = x.shape
    T = Bl * S
    ix, iy, ic = lax.axis_index("x"), lax.axis_index("y"), lax.axis_index("c")
    me = 4 * ix + 2 * iy + ic

    xt = x.reshape(T, D)
    me1 = me.reshape(1).astype(jnp.int32)
    stages = []
    for layer in range(DEPTH):
        j = layer // 2
        kin, kout = ("w_in_even", "w_out_even") if layer % 2 == 0 else ("w_in_odd", "w_out_odd")
        stages.append((mixer_stage, layer, ("in", "out"), ((kin, j, True), (kout, j, False)), MIXER_SMALL[layer % 2]))
        stages.append((ffn_stage, layer, ("up", "down"), (("ffn_w_up", layer, True), ("ffn_w_down", layer, False)), FFN_SMALL))

    gather, after = [], xt
    for s, (_, _, _, projs, _) in enumerate(stages):
        handle = split_start(f"gather{s}_start", "gather", [w[n][l].astype(BF16) for n, l, _ in projs], None, after)
        gather.append(handle)
        after = handle[-1]
    sm_names = list(SMALL_SHARDED)
    (sm_all8,) = all_gather("gather_small", [_pack([w[n] for n in sm_names])])
    small = {n: _unshard(t, SMALL_SHARDED[n]) for n, t in zip(sm_names, _unpack(sm_all8, [w[n].shape for n in sm_names]))}
    small.update({n: w[n] for n in REPLICATED})
    small["mix_norm_g"] = small["mix_norm_g"] + after[0, 0]

    resid, branch, pulls = xt, None, []
    second = [None] * len(stages)

    def second_level(s, after):
        lands = split_wait(f"gather{s}_wait", "gather", gather[s], after)
        second[s] = split_start(f"forward{s}_start", "forward", [], lands, after)

    for s, (fn, layer, keys, projs, sm_keys) in enumerate(stages):
        here = lax.stop_gradient(resid)
        if second[s] is None:
            second_level(s, here)
        lands = split_wait(f"forward{s}_wait", "forward", second[s], second[s][-1])
        sm = {n: small[n] for n in sm_keys}
        if 2 <= s < len(stages) - 1:
            second_level(s + 1, here)
            norm = "mix_norm_g" if fn is mixer_stage else "ffn_norm_g"
            sm[norm] = sm[norm] + second[s + 1][-1][0, 0]
        full, carriers = [], {}
        for key, land, (n, l, col) in zip(keys, lands, projs, strict=True):
            full.append(assemble_weight(f"weight{s}_{key}", land, w[n][l].astype(BF16), me1, col))
            carriers[key + "32"] = jnp.zeros(land.shape, F32)
            carriers[key + "16"] = jnp.zeros(land.shape, BF16)
        run = functools.partial(fn, layer, S)
        if branch is None:
            (resid, branch), pull = jax.vjp(lambda r, c, p, run=run, full=full: run(r, None, full[0], full[1], c, p), resid, carriers, sm)
        else:
            (resid, branch), pull = jax.vjp(lambda r, b, c, p, run=run, full=full: run(r, b, full[0], full[1], c, p), resid, branch, carriers, sm)
        pulls.append(pull)

    loss_acc, dxf, dgf = loss_head(resid, branch, small["final_norm_g"].reshape(1, D), loss_target.reshape(T, D))
    loss = lax.psum(loss_acc[0, 0], ("x", "y", "c"))
    g_small = {"final_norm_g": dgf.reshape(D)}
    d_resid, d_branch, token = dxf, dxf, None
    scatter, own32 = [None] * len(stages), [None] * len(stages)
    for s in reversed(range(len(stages))):
        if token is not None:
            d_resid = lax.dynamic_update_slice(d_resid, d_resid[:SUBLANES, :LANES] + token, (0, 0))
        if s == 0:
            d_resid, dcar, dsm = pulls[s]((d_resid, d_branch))
        else:
            d_resid, d_branch, dcar, dsm = pulls[s]((d_resid, d_branch))
        for n, g in dsm.items():
            g_small[n] = g_small[n] + g if n in g_small else g
        keys = stages[s][2]
        own32[s] = [dcar[k + "32"] for k in keys]
        scatter[s] = split_start(f"grads{s}_start", "scatter", [dcar[k + "16"] for k in keys], None, d_resid)
        token = scatter[s][-1]
    dx = d_resid

    out = {}
    sm_all = sm_names + list(REPLICATED)
    (g_all,) = all_gather("gather_small_grads", [_pack([g_small[n] for n in sm_all])])
    g_sum = _unpack(ordered_sum("sum_small_grads", g_all), [g_small[n].shape for n in sm_all])
    g_loc = []
    for n, g in zip(sm_all, g_sum):
        if n in SMALL_SHARDED:
            ax = SMALL_SHARDED[n]
            size = w[n].shape[ax]
            g = lax.dynamic_slice_in_dim(g, me * size, size, axis=ax)
        g_loc.append(g)
    shapes = [w[n].shape for n in sm_all]
    res = adamw("adamw_small", _pack([w[n] for n in sm_all]), _pack([m[n] for n in sm_all]), _pack([v[n] for n in sm_all]), _pack(g_loc))
    last_small = res[0]
    res = [g_loc] + [_unpack(r, shapes) for r in res]
    for k, n in enumerate(sm_all):
        out[n] = [r[k] for r in res]

    chain, behind = {}, last_small
    for s in reversed(range(len(stages))):
        lands = split_wait(f"grads{s}_wait", "scatter", scatter[s], behind if s == 0 else dx)
        for own, land, (n, l, _) in zip(own32[s], lands, stages[s][3], strict=True):
            chain[n] = adamw_sharded(f"adamw_{n}_{l}", l, _flat2d(w[n]), _flat2d(m[n]), _flat2d(v[n]), own, land, me1, chain.get(n))
            behind = chain[n][0]
    for n in BIG:
        out[n] = [t.reshape(w[n].shape) for t in chain[n]]

    grads, deltas, new_m, new_v = ([out[n][k] for n in WEIGHTS] for k in range(4))
    return (loss, dx.reshape(Bl, S, D), *grads, *deltas, *new_m, *new_v)
```

```python
import functools

import numpy as np
import jax
import jax.numpy as jnp
from jax import lax
from jax.experimental import pallas as pl
from jax.experimental.pallas import tpu as pltpu

F32 = jnp.float32
BF16 = jnp.bfloat16
HIGHEST = lax.Precision.HIGHEST
MESH = pl.DeviceIdType.MESH

D_MODEL = 1024
DEPTH = 4
N_EVEN = 2
N_ODD = 2
GLA_HEADS, GLA_DK, GLA_DV, GLA_RANK, GLA_GATE_NORM = 4, 64, 128, 16, 16.0
HGRN_HEADS, HGRN_DK, HGRN_DV, HGRN_MIN_F = 4, 64, 128, 1e-20
RET_HEADS, RET_DK, RET_DV = 4, 128, 192
ROPE_BASE = 10000.0
S5_WIDTH, S5_GROUP_CH, S5_GROUPS, S5_STATE = 256, 16, 16, 64
S5_N = S5_GROUPS * S5_STATE
FFN_DIM = 2816
EPS = 1e-6
EVEN_IN = 3360
ODD_IN = 2816
ADAM_LR, ADAM_B1, ADAM_B2, ADAM_EPS, ADAM_WD, ADAM_STEP = 0.001, 0.9, 0.999, 1e-08, 0.01, 10

N_DEV = 8
VMEM_LIMIT_BYTES = 56 * 1024 * 1024
ROW_TILE = 256
SCAN_CHUNK = 128
HGRN_SCAN_CHUNK = 64
S5_SEGMENTS = 8
LANES = 128


def _params(*sem):
    return pltpu.CompilerParams(dimension_semantics=sem, vmem_limit_bytes=VMEM_LIMIT_BYTES)


def _divisor_tile(n, cap):
    best = None
    for t in range(LANES, min(n, cap) + 1, LANES):
        if n % t == 0:
            best = t
    return best if best is not None else n


def _mm_nn(name, x, w):
    M, K = x.shape
    N = w.shape[1]
    tn = _divisor_tile(N, 3072) if K * N * 2 > 8 * 2**20 else N
    tm = 256 if tn * 4 * 512 > 6 * 2**20 else 512
    assert M % tm == 0 and N % tn == 0

    def body(x_ref, w_ref, o_ref):
        o_ref[...] = jnp.dot(x_ref[...].astype(BF16), w_ref[...], preferred_element_type=F32)

    return pl.pallas_call(
        body, name=name, grid=(N // tn, M // tm),
        in_specs=[pl.BlockSpec((tm, K), lambda j, i: (i, 0)), pl.BlockSpec((K, tn), lambda j, i: (0, j))],
        out_specs=pl.BlockSpec((tm, tn), lambda j, i: (i, j)),
        out_shape=jax.ShapeDtypeStruct((M, N), F32),
        compiler_params=_params("parallel", "parallel"),
    )(x, w)


def _mm_nt(name, dy, w, after=None):
    M, N = dy.shape
    K = w.shape[0]
    tk = _divisor_tile(K, 1024) if K * N * 2 > 8 * 2**20 else K
    tm = 256 if N >= 4096 else 512
    assert M % tm == 0 and K % tk == 0

    def body(dy_ref, w_ref, *rest):
        rest[-1][...] = lax.dot_general(dy_ref[...].astype(BF16), w_ref[...], (((1,), (1,)), ((), ())),
                                        preferred_element_type=F32)

    extra = [] if after is None else [after]
    return pl.pallas_call(
        body, name=name, grid=(K // tk, M // tm),
        in_specs=[pl.BlockSpec((tm, N), lambda j, i: (i, 0)), pl.BlockSpec((tk, N), lambda j, i: (j, 0))]
        + [pl.BlockSpec(memory_space=pl.ANY)] * len(extra),
        out_specs=pl.BlockSpec((tm, tk), lambda j, i: (i, j)),
        out_shape=jax.ShapeDtypeStruct((M, K), F32),
        compiler_params=_params("parallel", "parallel"),
    )(dy, w, *extra)


def _tick():
    return jnp.zeros((SUBLANES, LANES), F32)


MM_TN_VMEM_BUDGET = 46 * 2**20


def _pad_lanes(n):
    return -(-n // LANES) * LANES


def _mm_tn(name, x, dy, nblk, want16):
    M, K = x.shape
    N = dy.shape[1]
    n = N // nblk
    xb, yb = x.dtype.itemsize, dy.dtype.itemsize
    best = None
    for tk in [t for t in range(LANES, K + 1, LANES) if K % t == 0]:
        for tm in (512, 256):
            out_bytes = nblk * tk * _pad_lanes(n) * (6 if want16 else 4)
            vmem = 2 * out_bytes + 2 * tm * tk * xb + 2 * tm * _pad_lanes(N) * yb
            traffic = (K // tk) * M * N * yb + M * K * xb
            if vmem <= MM_TN_VMEM_BUDGET and M % tm == 0 and (best is None or (traffic, -tm) < best[0]):
                best = ((traffic, -tm), tk, tm)
    _, tk, tm = best
    last = M // tm - 1

    def body(x_ref, dy_ref, o32_ref, *o16_ref):
        m = pl.program_id(1)

        @pl.when(m == 0)
        def _():
            o32_ref[...] = jnp.zeros_like(o32_ref)

        dyv = dy_ref[...].astype(BF16)
        if nblk == 1:
            o32_ref[0] += lax.dot_general(x_ref[...].astype(BF16), dyv, (((0,), (0,)), ((), ())), preferred_element_type=F32)
        else:
            xt = x_ref[...].astype(F32).T.astype(BF16)
            for d in range(nblk):
                o32_ref[d] += jnp.dot(xt, dyv[:, d * n:(d + 1) * n], preferred_element_type=F32)
        if want16:
            @pl.when(m == last)
            def _():
                o16_ref[0][...] = o32_ref[...].astype(BF16)

    blk = pl.BlockSpec((nblk, tk, n), lambda a, m: (0, a, 0))
    return pl.pallas_call(
        body, name=name, grid=(K // tk, M // tm),
        in_specs=[pl.BlockSpec((tm, tk), lambda a, m: (m, a)), pl.BlockSpec((tm, N), lambda a, m: (m, 0))],
        out_specs=[blk, blk] if want16 else [blk],
        out_shape=[jax.ShapeDtypeStruct((nblk, K, n), F32)] + ([jax.ShapeDtypeStruct((nblk, K, n), BF16)] if want16 else []),
        compiler_params=_params("parallel", "arbitrary"),
    )(x, dy)


def _weight_grads(name, x, dy, col_sharded):
    if col_sharded:
        return _mm_tn(name, x, dy, N_DEV, True)
    d32, d16 = _mm_tn(name, x, dy, 1, True)
    K, N = d32.shape[1:]
    return d32.reshape(N_DEV, K // N_DEV, N), d16.reshape(N_DEV, K // N_DEV, N)


def make_mm(name, col_sharded=False):
    @jax.custom_vjp
    def mm(x, w16, c32, c16):
        return _mm_nn(name + "_fwd", x, w16), _tick()

    def fwd(x, w16, c32, c16):
        return (_mm_nn(name + "_fwd", x, w16), _tick()), (x, w16)

    def bwd(res, g):
        x, w16 = res
        dy, after = g
        dx = _mm_nt(name + "_dx", dy, w16, after)
        d32, d16 = _weight_grads(name + "_dw", x, dy, col_sharded)
        return dx, jnp.zeros_like(w16), d32, d16

    mm.defvjp(fwd, bwd)
    return mm


def make_mm_groups(name, G, tm=512):
    def products(x, w16):
        M, K = x.shape
        N = w16.shape[1] // G

        def body(x_ref, w_ref, *o_refs):
            xv = x_ref[...].astype(BF16)
            for g, o_ref in enumerate(o_refs):
                o_ref[...] = jnp.dot(xv, w_ref[:, g * N:(g + 1) * N], preferred_element_type=F32)

        row = pl.BlockSpec((tm, N), lambda i: (i, 0))
        return tuple(pl.pallas_call(
            body, name=name + "_fwd", grid=(M // tm,),
            in_specs=[pl.BlockSpec((tm, K), lambda i: (i, 0)), pl.BlockSpec((K, G * N), lambda i: (0, 0))],
            out_specs=[row] * G, out_shape=[jax.ShapeDtypeStruct((M, N), F32)] * G, compiler_params=_params("parallel"),
        )(x, w16))

    def grads(x, w16, dys):
        M, K = x.shape
        N = w16.shape[1] // G

        def body(x_ref, w_ref, *refs):
            dy_refs, dx_ref, dw_ref = refs[:G], refs[G], refs[G + 1]

            @pl.when(pl.program_id(0) == 0)
            def _():
                dw_ref[...] = jnp.zeros_like(dw_ref)

            xv = x_ref[...].astype(BF16)
            dx = jnp.zeros(dx_ref.shape, F32)
            for g in range(G):
                cols = slice(g * N, (g + 1) * N)
                dyv = dy_refs[g][...].astype(BF16)
                dx = dx + lax.dot_general(dyv, w_ref[:, cols], (((1,), (1,)), ((), ())), preferred_element_type=F32)
                dw_ref[:, cols] += lax.dot_general(xv, dyv, (((0,), (0,)), ((), ())), preferred_element_type=F32)
            dx_ref[...] = dx

        row = pl.BlockSpec((tm, N), lambda i: (i, 0))
        return pl.pallas_call(
            body, name=name + "_bwd", grid=(M // tm,),
            in_specs=[pl.BlockSpec((tm, K), lambda i: (i, 0)), pl.BlockSpec((K, G * N), lambda i: (0, 0))] + [row] * G,
            out_specs=[pl.BlockSpec((tm, K), lambda i: (i, 0)), pl.BlockSpec((K, G * N), lambda i: (0, 0))],
            out_shape=[jax.ShapeDtypeStruct((M, K), F32), jax.ShapeDtypeStruct((K, G * N), F32)],
            compiler_params=_params("arbitrary"),
        )(x, w16, *dys)

    @jax.custom_vjp
    def mm(x, w):
        return products(x, w.astype(BF16))

    def fwd(x, w):
        w16 = w.astype(BF16)
        return products(x, w16), (x, w16)

    def bwd(res, dys):
        x, w16 = res
        return tuple(grads(x, w16, tuple(dys)))

    mm.defvjp(fwd, bwd)
    return mm


def _row_specs(rows, params, consts, tile):
    specs = [pl.BlockSpec((tile, r.shape[1]), lambda i: (i, 0)) for r in rows]
    specs += [pl.BlockSpec(p.shape, lambda i: (0, 0)) for p in params]
    specs += [pl.BlockSpec((tile, c.shape[1]), lambda i, n=c.shape[0] // tile: (i % n, 0)) for c in consts]
    return specs


def _row_fwd(name, f, out_widths, tile, rows, params, consts):
    T = rows[0].shape[0]
    nr, npar, ncon = len(rows), len(params), len(consts)

    def body(*refs):
        r = tuple(x[...] for x in refs[:nr])
        p = tuple(x[...] for x in refs[nr:nr + npar])
        c = tuple(x[...] for x in refs[nr + npar:nr + npar + ncon])
        outs = f(r, p, c)
        for o_ref, o in zip(refs[nr + npar + ncon:], outs, strict=True):
            o_ref[...] = o

    return pl.pallas_call(
        body, name=name + "_fwd", grid=(T // tile,),
        in_specs=_row_specs(rows, params, consts, tile),
        out_specs=[pl.BlockSpec((tile, w), lambda i: (i, 0)) for w in out_widths],
        out_shape=[jax.ShapeDtypeStruct((T, w), F32) for w in out_widths],
        compiler_params=_params("parallel"),
    )(*rows, *params, *consts)


def _row_bwd(name, f, out_widths, tile, rows, params, consts, gouts, dr_dtype=F32):
    T = rows[0].shape[0]
    nr, npar, ncon, nout = len(rows), len(params), len(consts), len(out_widths)

    def body(*refs):
        r = tuple(x[...] for x in refs[:nr])
        p = tuple(x[...] for x in refs[nr:nr + npar])
        c = tuple(x[...] for x in refs[nr + npar:nr + npar + ncon])
        k = nr + npar + ncon
        g = tuple(x[...] for x in refs[k:k + nout])
        dr_refs = refs[k + nout:k + nout + nr]
        dp_refs = refs[k + nout + nr:]
        _, vjp = jax.vjp(lambda r_, p_: tuple(f(r_, p_, c)), r, p)
        dr, dp = vjp(g)
        for ref, val in zip(dr_refs, dr, strict=True):
            ref[...] = val.astype(ref.dtype)
        if npar:
            @pl.when(pl.program_id(0) == 0)
            def _():
                for ref in dp_refs:
                    ref[...] = jnp.zeros_like(ref)

            for ref, val in zip(dp_refs, dp, strict=True):
                ref[...] += val

    outs = pl.pallas_call(
        body, name=name + "_bwd", grid=(T // tile,),
        in_specs=_row_specs(rows, params, consts, tile) + [pl.BlockSpec((tile, w), lambda i: (i, 0)) for w in out_widths],
        out_specs=[pl.BlockSpec((tile, r.shape[1]), lambda i: (i, 0)) for r in rows]
        + [pl.BlockSpec(p.shape, lambda i: (0, 0)) for p in params],
        out_shape=[jax.ShapeDtypeStruct(r.shape, dr_dtype) for r in rows] + [jax.ShapeDtypeStruct(p.shape, F32) for p in params],
        compiler_params=_params("arbitrary"),
    )(*rows, *params, *consts, *gouts)
    return tuple(outs[:nr]), tuple(outs[nr:])


def make_proj_stage(name, f, out_widths, tile=ROW_TILE):
    def run(x, w16, params, consts):
        p = _mm_nn(name + "_mm", x, w16)
        return p, tuple(_row_fwd(name, f, out_widths, tile, (p,), params, consts))

    @jax.custom_vjp
    def op(x, w16, c32, c16, params, consts):
        return run(x, w16, params, consts)[1]

    def fwd(x, w16, c32, c16, params, consts):
        p, outs = run(x, w16, params, consts)
        return outs, (x, w16, p, params, consts)

    def bwd(res, g):
        x, w16, p, params, consts = res
        (dp,), dparams = _row_bwd(name, f, out_widths, tile, (p,), params, consts, tuple(g), dr_dtype=BF16)
        d32, d16 = _weight_grads(name + "_dw", x, dp, True)
        return _mm_nt(name + "_dx", dp, w16), jnp.zeros_like(w16), d32, d16, dparams, tuple(jnp.zeros_like(c) for c in consts)

    op.defvjp(fwd, bwd)
    return op


def make_rowop(name, f, out_widths, tile=ROW_TILE):
    @jax.custom_vjp
    def op(rows, params, consts):
        return tuple(_row_fwd(name, f, out_widths, tile, rows, params, consts))

    def fwd(rows, params, consts):
        return op(rows, params, consts), (rows, params, consts)

    def bwd(res, g):
        rows, params, consts = res
        dr, dp = _row_bwd(name, f, out_widths, tile, rows, params, consts, tuple(g))
        return dr, dp, tuple(jnp.zeros_like(c) for c in consts)

    op.defvjp(fwd, bwd)
    return op


def _rms(x, g):
    return x * lax.rsqrt(jnp.mean(x * x, axis=-1, keepdims=True) + EPS) * g


def _silu(x):
    return x * jax.nn.sigmoid(x)


def _bdot(a, b):
    return jnp.dot(a.astype(BF16), b.astype(BF16), preferred_element_type=F32)


def norm_f(rows, params, consts):
    return (_rms(rows[0], params[0]),)


def addnorm_f(rows, params, consts):
    x = rows[0] + rows[1]
    return x, _rms(x, params[0])


EVEN_GLA_END = 1568


def even_prep_f(rows, params, consts):
    (p,) = rows
    wa2f, wa2b, baf, bab, lbf, lbb = params
    gq = p[:, 0:256]
    gk = p[:, 256:512] * (GLA_DK ** -0.5)
    gv = p[:, 512:1024]
    gr = p[:, 1024:1536]
    glaf = jax.nn.log_sigmoid(_bdot(p[:, 1536:1552], wa2f) + baf) / GLA_GATE_NORM
    glab = jax.nn.log_sigmoid(_bdot(p[:, 1552:1568], wa2b) + bab) / GLA_GATE_NORM
    o = EVEN_GLA_END
    hq = _silu(p[:, o:o + 256])

    def gate(z, lb):
        f = lb + (1.0 - lb) * jax.nn.sigmoid(z)
        return jnp.log(jnp.maximum(f, HGRN_MIN_F)), (1.0 - lb) * jax.nn.sigmoid(-z)

    hlaf, hkf = gate(p[:, o + 256:o + 512], lbf)
    hlab, hkb = gate(p[:, o + 512:o + 768], lbb)
    hv = p[:, o + 768:o + 1280]
    hg = p[:, o + 1280:o + 1792]
    return gq, gk, gv, gq, gk, gv, glaf, glab, gr, hq, hkf, hv, hq, hkb, hv, hlaf, hlab, hg


EVEN_PREP_WIDTHS = (256, 256, 512, 256, 256, 512, 256, 256, 512, 256, 256, 512, 256, 256, 512, 256, 256, 512)


def _head_rms(o, g, heads, d):
    parts = []
    for h in range(heads):
        seg = o[:, h * d:(h + 1) * d]
        parts.append(seg * lax.rsqrt(jnp.mean(seg * seg, axis=-1, keepdims=True) + EPS))
    return jnp.concatenate(parts, axis=1) * g


def even_post_f(rows, params, consts):
    of, ob, hof, hob, gr, hg = rows
    a = _head_rms(of + ob, params[0], GLA_HEADS, GLA_DV) * _silu(gr)
    b = _head_rms(hof + hob, params[1], HGRN_HEADS, HGRN_DV) * _silu(hg)
    return (jnp.concatenate([a, b], axis=1),)


@jax.custom_vjp
def _roll_half(x):
    return pltpu.roll(x, RET_DK // 2, 1)


_roll_half.defvjp(lambda x: (_roll_half(x), None), lambda _, g: (_roll_half(g),))


def odd_prep_f(rows, params, consts):
    (p,) = rows
    cosf, sinf = consts

    def rot(t):
        parts = []
        for h in range(RET_HEADS):
            th = t[:, h * RET_DK:(h + 1) * RET_DK]
            parts.append(th * cosf + _roll_half(th) * sinf)
        return jnp.concatenate(parts, axis=1)

    rq = rot(p[:, 0:512])
    rk = rot(p[:, 512:1024]) * (RET_DK ** -0.5)
    rv = p[:, 1024:1792]
    return rq, rk, rv, rq, rk, rv, p[:, 1792:2560], p[:, 2560:2816]


ODD_PREP_WIDTHS = (512, 512, 768, 512, 512, 768, 768, 256)


def ret_post_f(rows, params, consts):
    of, ob, rg = rows
    o = of + ob
    parts = []
    for h in range(RET_HEADS):
        seg = o[:, h * RET_DV:(h + 1) * RET_DV]
        c = seg - jnp.mean(seg, axis=-1, keepdims=True)
        parts.append(c * lax.rsqrt(jnp.mean(c * c, axis=-1, keepdims=True) + EPS))
    return (jnp.concatenate(parts, axis=1) * params[0] * _silu(rg),)


def s5_post_f(rows, params, consts):
    h0r, h0i, h1r, h1i, u = rows
    c_re, c_im, d_skip, glu_w, glu_b = params
    hr = h0r + h1r
    hi = h0i + h1i
    y = _bdot(hr, c_re) - _bdot(hi, c_im) + d_skip * u
    g = jax.nn.gelu(y)
    return (g * jax.nn.sigmoid(_bdot(g, glu_w) + glu_b),)


def loss_head(x, r, g, target, tile=ROW_TILE):
    T, D = x.shape

    def body(x_ref, r_ref, g_ref, t_ref, loss_ref, dx_ref, dg_ref):
        t = t_ref[...]

        def lf(xv, gv):
            e = _rms(xv, gv) - t
            row = jnp.sum(e * e, axis=-1, keepdims=True) * (0.5 / D)
            return jnp.sum(row, axis=0, keepdims=True)

        l, vjp = jax.vjp(lf, x_ref[...] + r_ref[...], g_ref[...])
        dx, dg = vjp(jnp.ones((1, 1), F32))
        dx_ref[...] = dx

        @pl.when(pl.program_id(0) == 0)
        def _():
            loss_ref[...] = jnp.zeros_like(loss_ref)
            dg_ref[...] = jnp.zeros_like(dg_ref)

        loss_ref[...] += jnp.broadcast_to(l, loss_ref.shape)
        dg_ref[...] += dg

    row = pl.BlockSpec((tile, D), lambda i: (i, 0))
    vec = pl.BlockSpec((1, D), lambda i: (0, 0))
    return pl.pallas_call(
        body, name="loss_head", grid=(T // tile,),
        in_specs=[row, row, vec, row],
        out_specs=[pl.BlockSpec((1, LANES), lambda i: (0, 0)), row, vec],
        out_shape=[jax.ShapeDtypeStruct((1, LANES), F32), jax.ShapeDtypeStruct((T, D), F32), jax.ShapeDtypeStruct((1, D), F32)],
        compiler_params=_params("arbitrary"),
    )(x, r, g, target)


SUBLANES = 8


def _halo_specs(width, tile, T):
    n8 = tile // SUBLANES
    last = T // SUBLANES - 1
    return [pl.BlockSpec((tile, width), lambda i: (i, 0)),
            pl.BlockSpec((SUBLANES, width), lambda i: (jnp.maximum(i * n8 - 1, 0), 0)),
            pl.BlockSpec((SUBLANES, width), lambda i: (jnp.minimum((i + 1) * n8, last), 0))]


def _shift_rows(x, prev_row, next_row, tile):
    row = lax.broadcasted_iota(jnp.int32, (tile, 1), 0)
    down = jnp.where(row == 0, prev_row, pltpu.roll(x, 1, 0))
    up = jnp.where(row == tile - 1, next_row, pltpu.roll(x, tile - 1, 0))
    return down, up


def _conv_fwd(name, u, cw, cb, S, tile):
    T, F2 = u.shape
    F = F2 // 2
    per_seq = S // tile

    def body(u_ref, up_ref, un_ref, cw_ref, cb_ref, g_ref):
        pos = pl.program_id(0) % per_seq
        uv = u_ref[...]
        prev_row = jnp.where(pos == 0, 0.0, up_ref[SUBLANES - 1:SUBLANES, :])
        next_row = jnp.where(pos == per_seq - 1, 0.0, un_ref[0:1, :])
        down, up = _shift_rows(uv, prev_row, next_row, tile)
        c = _conv_taps(down, uv, up, cw_ref, cb_ref)
        g_ref[...] = (_silu(c[:, :F]) * c[:, F:]).astype(BF16)

    return pl.pallas_call(
        body, name=name + "_fwd", grid=(T // tile,),
        in_specs=_halo_specs(F2, tile, T) + [pl.BlockSpec((3, F2), lambda i: (0, 0)), pl.BlockSpec((1, F2), lambda i: (0, 0))],
        out_specs=pl.BlockSpec((tile, F), lambda i: (i, 0)),
        out_shape=jax.ShapeDtypeStruct((T, F), BF16),
        compiler_params=_params("parallel"),
    )(u, u, u, cw, cb)


def _conv_taps(down, mid, up, cw_ref, cb_ref):
    c = cb_ref[...] + down * cw_ref[0:1, :]
    c = c + mid * cw_ref[1:2, :]
    return c + up * cw_ref[2:3, :]


def _conv_bwd(name, u, dg, cw, cb, S, tile):
    T, F2 = u.shape
    F = F2 // 2
    per_seq = S // tile

    def dact(cv, dgv):
        a, v = cv[:, :F], cv[:, F:]
        sg = jax.nn.sigmoid(a)
        return jnp.concatenate([dgv * v * (sg * (1.0 + a * (1.0 - sg))), dgv * (a * sg)], axis=1)

    def body(u_ref, up_ref, un_ref, g_ref, gp_ref, gn_ref, cw_ref, cb_ref, du_ref, dw0_ref, dw1_ref, dw2_ref, db_ref):
        i = pl.program_id(0)
        pos = i % per_seq
        first, last = pos == 0, pos == per_seq - 1
        lo, hi = slice(SUBLANES - 1, SUBLANES), slice(0, 1)
        uv = u_ref[...]
        u_m1, u_p1 = jnp.where(first, 0.0, up_ref[lo, :]), jnp.where(last, 0.0, un_ref[hi, :])
        u_dn, u_up = _shift_rows(uv, u_m1, u_p1, tile)
        dc = dact(_conv_taps(u_dn, uv, u_up, cw_ref, cb_ref), g_ref[...])
        c_m1 = _conv_taps(up_ref[SUBLANES - 2:SUBLANES - 1, :], u_m1, u_ref[0:1, :], cw_ref, cb_ref)
        c_p1 = _conv_taps(u_ref[tile - 1:tile, :], u_p1, un_ref[1:2, :], cw_ref, cb_ref)
        dc_prev = jnp.where(first, 0.0, dact(c_m1, gp_ref[lo, :]))
        dc_next = jnp.where(last, 0.0, dact(c_p1, gn_ref[hi, :]))
        dc_dn, dc_up = _shift_rows(dc, dc_prev, dc_next, tile)
        du = dc_up * cw_ref[0:1, :]
        du = du + dc * cw_ref[1:2, :]
        du_ref[...] = (du + dc_dn * cw_ref[2:3, :]).astype(BF16)

        @pl.when(i == 0)
        def _():
            for ref in (dw0_ref, dw1_ref, dw2_ref, db_ref):
                ref[...] = jnp.zeros_like(ref)

        dw0_ref[...] += jnp.sum(dc * u_dn, axis=0, keepdims=True)
        dw1_ref[...] += jnp.sum(dc * uv, axis=0, keepdims=True)
        dw2_ref[...] += jnp.sum(dc * u_up, axis=0, keepdims=True)
        db_ref[...] += jnp.sum(dc, axis=0, keepdims=True)

    vec = pl.BlockSpec((1, F2), lambda i: (0, 0))
    du, dw0, dw1, dw2, db = pl.pallas_call(
        body, name=name + "_bwd", grid=(T // tile,),
        in_specs=_halo_specs(F2, tile, T) + _halo_specs(F, tile, T) + [pl.BlockSpec((3, F2), lambda i: (0, 0)), vec],
        out_specs=[pl.BlockSpec((tile, F2), lambda i: (i, 0)), vec, vec, vec, vec],
        out_shape=[jax.ShapeDtypeStruct((T, F2), BF16)] + [jax.ShapeDtypeStruct((1, F2), F32)] * 4,
        compiler_params=_params("arbitrary"),
    )(u, u, u, dg, dg, dg, cw, cb)
    return du, jnp.concatenate([dw0, dw1, dw2], axis=0), db


def make_ffn(name, S):
    def run(x, wup16, wdn16, cw, cb):
        u = _mm_nn(name + "_up", x, wup16)
        g16 = _conv_fwd(name + "_conv", u, cw, cb, S, ROW_TILE)
        return u, g16, _mm_nn(name + "_down", g16, wdn16)

    @jax.custom_vjp
    def op(x, wup16, cu32, cu16, wdn16, cd32, cd16, cw, cb):
        return run(x, wup16, wdn16, cw, cb)[2], _tick()

    def fwd(x, wup16, cu32, cu16, wdn16, cd32, cd16, cw, cb):
        u, g16, out = run(x, wup16, wdn16, cw, cb)
        return (out, _tick()), (x, wup16, wdn16, u, g16, cw, cb)

    def bwd(res, g):
        x, wup16, wdn16, u, g16, cw, cb = res
        dout, after = g
        dg = _mm_nt(name + "_down_dx", dout, wdn16, after)
        dd32, dd16 = _weight_grads(name + "_down_dw", g16, dout, False)
        du16, dcw, dcb = _conv_bwd(name + "_conv", u, dg, cw, cb, S, ROW_TILE)
        du32, du16w = _weight_grads(name + "_up_dw", x, du16, True)
        return (_mm_nt(name + "_up_dx", du16, wup16), jnp.zeros_like(wup16), du32, du16w, jnp.zeros_like(wdn16), dd32, dd16, dcw, dcb)

    op.defvjp(fwd, bwd)
    return op


def _dot_nt(a, b):
    return lax.dot_general(a.astype(BF16), b.astype(BF16), (((1,), (1,)), ((), ())), preferred_element_type=F32)


def _dot_tn(a, b):
    return lax.dot_general(a.astype(BF16), b.astype(BF16), (((0,), (0,)), ((), ())), preferred_element_type=F32)


def _chunk_decays(la, direction, C, width, dk, const_lg):
    row = lax.broadcasted_iota(jnp.int32, (C, C), 0)
    col = lax.broadcasted_iota(jnp.int32, (C, C), 1)
    keep = (row >= col) if direction == 0 else (row <= col)
    ridx = lax.broadcasted_iota(jnp.int32, (C, 1), 0)
    if const_lg is None:
        cum = jnp.dot(keep.astype(F32), la, precision=HIGHEST, preferred_element_type=F32)
    else:
        lane_head = lax.broadcasted_iota(jnp.int32, (1, width), 1) // dk
        lg = jnp.zeros((1, width), F32)
        for h, val in enumerate(const_lg):
            lg = jnp.where(lane_head == h, val, lg)
        steps = (ridx + 1) if direction == 0 else (C - ridx)
        cum = steps.astype(F32) * lg
    exit_row = C - 1 if direction == 0 else 0
    mid = jnp.sum(jnp.where(ridx == C // 2, cum, 0.0), axis=0, keepdims=True)
    last = jnp.sum(jnp.where(ridx == exit_row, cum, 0.0), axis=0, keepdims=True)
    return keep, ridx == exit_row, cum, mid, last


def _scan_fwd(name, qkv, laf, lab, H, dk, dv, S, C, const_lg):
    qf, kf, vf, qb, kb, vb = qkv
    T = qf.shape[0]
    B, nc = T // S, S // C
    Wk, Wv = H * dk, H * dv
    learn = const_lg is None

    def body(*refs):
        @pl.when(pl.program_id(0) == 0)
        def _():
            refs[-1][...] = jnp.zeros_like(refs[-1])

        for b in range(B):
            one_sequence(*[r.at[b] for r in refs])

    def one_sequence(*refs):
        if learn:
            qf_r, qb_r, kf_r, kb_r, vf_r, vb_r, laf_r, lab_r, of_r, ob_r, sf_r, sb_r, st = refs
            las = (laf_r[...], lab_r[...])
        else:
            qf_r, qb_r, kf_r, kb_r, vf_r, vb_r, of_r, ob_r, sf_r, sb_r, st = refs
            las = (None, None)

        for d, (q_r, k_r, v_r, o_r, s_r) in enumerate(((qf_r, kf_r, vf_r, of_r, sf_r), (qb_r, kb_r, vb_r, ob_r, sb_r))):
            keep, _, cum, mid, last = _chunk_decays(las[d], d, C, Wk, dk, None if learn else const_lg[d])
            qe = q_r[...] * jnp.exp(cum - mid)
            ke = k_r[...] * jnp.exp(mid - cum)
            q_in = qe * jnp.exp(mid)
            k_out = ke * jnp.exp(last - mid)
            e_last = jnp.exp(last)
            vv = v_r[...]
            for h in range(H):
                ks, vs = slice(h * dk, (h + 1) * dk), slice(h * dv, (h + 1) * dv)
                a = jnp.where(keep, _dot_nt(qe[:, ks], ke[:, ks]), 0.0)
                state = st[d, h]
                o_r[:, vs] = _bdot(a, vv[:, vs]) + _dot_nt(q_in[:, ks], state)
                s_r[h * dv:(h + 1) * dv, :] = state
                st[d, h] = state * e_last[:, ks] + _dot_tn(vv[:, vs], k_out[:, ks])

    fpos = lambda c: c
    bpos = lambda c: nc - 1 - c
    kspec = lambda pos: pl.BlockSpec((B, C, Wk), lambda c: (0, pos(c), 0))
    vspec = lambda pos: pl.BlockSpec((B, C, Wv), lambda c: (0, pos(c), 0))
    sspec = lambda pos: pl.BlockSpec((B, None, Wv, dk), lambda c: (0, pos(c), 0, 0))
    seq = lambda t: t.reshape(B, S, t.shape[1])
    ins = [seq(t) for t in [qf, qb, kf, kb, vf, vb] + ([laf, lab] if learn else [])]
    in_specs = [kspec(fpos), kspec(bpos), kspec(fpos), kspec(bpos), vspec(fpos), vspec(bpos)] + ([kspec(fpos), kspec(bpos)] if learn else [])
    of, ob, sf, sb = pl.pallas_call(
        body, name=name + "_fwd", grid=(nc,), in_specs=in_specs,
        out_specs=[vspec(fpos), vspec(bpos), sspec(fpos), sspec(bpos)],
        out_shape=[jax.ShapeDtypeStruct((B, S, Wv), F32)] * 2 + [jax.ShapeDtypeStruct((B, nc, Wv, dk), F32)] * 2,
        scratch_shapes=[pltpu.VMEM((B, 2, H, dv, dk), F32)],
        compiler_params=_params("arbitrary"),
    )(*ins)
    return of.reshape(T, Wv), ob.reshape(T, Wv), sf, sb


def _scan_bwd(name, qkv, laf, lab, sf, sb, dof, dob, H, dk, dv, S, C, const_lg):
    qf, kf, vf, qb, kb, vb = qkv
    T = qf.shape[0]
    B, nc = T // S, S // C
    Wk, Wv = H * dk, H * dv
    learn = const_lg is None

    def body(*refs):
        @pl.when(pl.program_id(0) == 0)
        def _():
            refs[-1][...] = jnp.zeros_like(refs[-1])

        for b in range(B):
            one_sequence(*[r.at[b] for r in refs])

    def one_sequence(*refs):
        if learn:
            (qf_r, qb_r, kf_r, kb_r, vf_r, vb_r, laf_r, lab_r, sf_r, sb_r, dof_r, dob_r,
             dqf_r, dqb_r, dkf_r, dkb_r, dvf_r, dvb_r, dlaf_r, dlab_r, dst) = refs
            las, dlas = (laf_r[...], lab_r[...]), (dlaf_r, dlab_r)
        else:
            (qf_r, qb_r, kf_r, kb_r, vf_r, vb_r, sf_r, sb_r, dof_r, dob_r,
             dqf_r, dqb_r, dkf_r, dkb_r, dvf_r, dvb_r, dst) = refs
            las, dlas = (None, None), (None, None)

        groups = ((qf_r, kf_r, vf_r, sf_r, dof_r, dqf_r, dkf_r, dvf_r), (qb_r, kb_r, vb_r, sb_r, dob_r, dqb_r, dkb_r, dvb_r))
        for d, (q_r, k_r, v_r, s_r, do_r, dq_r, dk_r, dv_r) in enumerate(groups):
            keep, is_exit, cum, mid, last = _chunk_decays(las[d], d, C, Wk, dk, None if learn else const_lg[d])
            eq, ek = jnp.exp(cum - mid), jnp.exp(mid - cum)
            e_in, e_out, e_last = jnp.exp(mid), jnp.exp(last - mid), jnp.exp(last)
            qe, ke = q_r[...] * eq, k_r[...] * ek
            q_in, k_out = qe * e_in, ke * e_out
            vv, do = v_r[...], do_r[...]
            dqe_parts, dke_parts, dlast_parts = [], [], []
            for h in range(H):
                ks, vs = slice(h * dk, (h + 1) * dk), slice(h * dv, (h + 1) * dv)
                a = jnp.where(keep, _dot_nt(qe[:, ks], ke[:, ks]), 0.0)
                dp = jnp.where(keep, _dot_nt(do[:, vs], vv[:, vs]), 0.0)
                s_prev = s_r[h * dv:(h + 1) * dv, :]
                ds = dst[d, h]
                dk_out = _bdot(vv[:, vs], ds)
                dqe_parts.append(_bdot(dp, ke[:, ks]) + _bdot(do[:, vs], s_prev) * e_in[:, ks])
                dke_parts.append(_dot_tn(dp, qe[:, ks]) + dk_out * e_out[:, ks])
                dv_r[:, vs] = _dot_tn(a, do[:, vs]) + _dot_nt(k_out[:, ks], ds)
                if learn:
                    dlast_parts.append(jnp.sum(dk_out * k_out[:, ks], axis=0, keepdims=True)
                                       + jnp.sum(ds * s_prev, axis=0, keepdims=True) * e_last[:, ks])
                dst[d, h] = ds * e_last[:, ks] + _dot_tn(do[:, vs], q_in[:, ks])
            dqe = jnp.concatenate(dqe_parts, axis=1)
            dke = jnp.concatenate(dke_parts, axis=1)
            dq_r[...] = dqe * eq
            dk_r[...] = dke * ek
            if learn:
                dcum = dqe * qe - dke * ke + jnp.where(is_exit, jnp.concatenate(dlast_parts, axis=1), 0.0)
                dlas[d][...] = lax.dot_general(keep.astype(F32), dcum, (((0,), (0,)), ((), ())), precision=HIGHEST,
                                               preferred_element_type=F32)

    fpos = lambda c: nc - 1 - c
    bpos = lambda c: c
    kspec = lambda pos: pl.BlockSpec((B, C, Wk), lambda c: (0, pos(c), 0))
    vspec = lambda pos: pl.BlockSpec((B, C, Wv), lambda c: (0, pos(c), 0))
    sspec = lambda pos: pl.BlockSpec((B, None, Wv, dk), lambda c: (0, pos(c), 0, 0))
    seq = lambda t: t.reshape(B, S, t.shape[1])
    ins = [seq(t) for t in [qf, qb, kf, kb, vf, vb] + ([laf, lab] if learn else [])] + [sf, sb, seq(dof), seq(dob)]
    in_specs = ([kspec(fpos), kspec(bpos), kspec(fpos), kspec(bpos), vspec(fpos), vspec(bpos)]
                + ([kspec(fpos), kspec(bpos)] if learn else []) + [sspec(fpos), sspec(bpos), vspec(fpos), vspec(bpos)])
    out_specs = [kspec(fpos), kspec(bpos), kspec(fpos), kspec(bpos), vspec(fpos), vspec(bpos)] + ([kspec(fpos), kspec(bpos)] if learn else [])
    out_shape = ([jax.ShapeDtypeStruct((B, S, Wk), F32)] * 4 + [jax.ShapeDtypeStruct((B, S, Wv), F32)] * 2
                 + ([jax.ShapeDtypeStruct((B, S, Wk), F32)] * 2 if learn else []))
    outs = pl.pallas_call(
        body, name=name + "_bwd", grid=(nc,), in_specs=in_specs, out_specs=out_specs, out_shape=out_shape,
        scratch_shapes=[pltpu.VMEM((B, 2, H, dv, dk), F32)],
        compiler_params=_params("arbitrary"),
    )(*ins)
    return [t.reshape(T, t.shape[2]) for t in outs]


def make_scan(name, H, dk, dv, S, C, const_lg=None):
    if const_lg is None:
        @jax.custom_vjp
        def op(qkv, laf, lab):
            return tuple(_scan_fwd(name, qkv, laf, lab, H, dk, dv, S, C, None)[:2])

        def fwd(qkv, laf, lab):
            of, ob, sf, sb = _scan_fwd(name, qkv, laf, lab, H, dk, dv, S, C, None)
            return (of, ob), (qkv, laf, lab, sf, sb)

        def bwd(res, g):
            qkv, laf, lab, sf, sb = res
            dqf, dqb, dkf, dkb, dvf, dvb, dlaf, dlab = _scan_bwd(name, qkv, laf, lab, sf, sb, g[0], g[1], H, dk, dv, S, C, None)
            return (dqf, dkf, dvf, dqb, dkb, dvb), dlaf, dlab
    else:
        @jax.custom_vjp
        def op(qkv):
            return tuple(_scan_fwd(name, qkv, None, None, H, dk, dv, S, C, const_lg)[:2])

        def fwd(qkv):
            of, ob, sf, sb = _scan_fwd(name, qkv, None, None, H, dk, dv, S, C, const_lg)
            return (of, ob), (qkv, sf, sb)

        def bwd(res, g):
            qkv, sf, sb = res
            dqf, dqb, dkf, dkb, dvf, dvb = _scan_bwd(name, qkv, None, None, sf, sb, g[0], g[1], H, dk, dv, S, C, const_lg)
            return ((dqf, dkf, dvf, dqb, dkb, dvb),)

    op.defvjp(fwd, bwd)
    return op


def _reorder_call(name, t, S, to_segments):
    T, w = t.shape
    n_it = S // S5_SEGMENTS

    def body(x_ref, o_ref):
        def step(i, carry):
            packed = pl.ds(pl.multiple_of(i * S5_SEGMENTS, S5_SEGMENTS), S5_SEGMENTS)
            spread = pl.ds(i, S5_SEGMENTS, stride=n_it)
            if to_segments:
                o_ref[packed, :] = x_ref[spread, :]
            else:
                o_ref[spread, :] = x_ref[packed, :]
            return carry

        lax.fori_loop(0, n_it, step, 0, unroll=8)

    blk = pl.BlockSpec((S, LANES), lambda b, j: (b, j))
    return pl.pallas_call(body, name=name, grid=(T // S, w // LANES), in_specs=[blk], out_specs=blk,
                          out_shape=jax.ShapeDtypeStruct(t.shape, t.dtype), compiler_params=_params("parallel", "parallel"))(t)


def make_join(name, S, wa, wb):
    n_it = S // S5_SEGMENTS

    def call(tag, forward, wa, wb, arrs):
        T = arrs[0].shape[0]
        na, nb = wa // LANES, wb // LANES

        def body(*refs):
            j = pl.program_id(1)
            a_ref, b_ref, y_ref = (refs[0], refs[1], refs[2]) if forward else (refs[1], refs[2], refs[0])

            @pl.when(j < na)
            def _():
                if forward:
                    y_ref[...] = a_ref[...]
                else:
                    a_ref[...] = y_ref[...]

            @pl.when(j >= na)
            def _():
                def step(i, carry):
                    packed = pl.ds(pl.multiple_of(i * S5_SEGMENTS, S5_SEGMENTS), S5_SEGMENTS)
                    spread = pl.ds(i, S5_SEGMENTS, stride=n_it)
                    if forward:
                        y_ref[spread, :] = b_ref[packed, :]
                    else:
                        b_ref[packed, :] = y_ref[spread, :]
                    return carry

                lax.fori_loop(0, n_it, step, 0, unroll=8)

        a_spec = pl.BlockSpec((S, LANES), lambda b, j: (b, jnp.minimum(j, na - 1)))
        b_spec = pl.BlockSpec((S, LANES), lambda b, j: (b, jnp.maximum(j - na, 0)))
        y_spec = pl.BlockSpec((S, LANES), lambda b, j: (b, j))
        shapes = [jax.ShapeDtypeStruct((T, wa), F32), jax.ShapeDtypeStruct((T, wb), F32), jax.ShapeDtypeStruct((T, wa + wb), F32)]
        return pl.pallas_call(
            body, name=name + tag, grid=(T // S, na + nb),
            in_specs=[a_spec, b_spec] if forward else [y_spec], out_specs=y_spec if forward else [a_spec, b_spec],
            out_shape=shapes[2] if forward else shapes[:2], compiler_params=_params("parallel", "arbitrary"),
        )(*arrs)

    @jax.custom_vjp
    def op(a, b):
        return call("_fwd", True, wa, wb, (a, b))

    op.defvjp(lambda a, b: (call("_fwd", True, wa, wb, (a, b)), None), lambda _, dy: tuple(call("_bwd", False, wa, wb, (dy,))))
    return op


def make_reorder(name, S, to_segments):
    @jax.custom_vjp
    def op(t):
        return _reorder_call(name, t, S, to_segments)

    op.defvjp(lambda t: (_reorder_call(name, t, S, to_segments), None),
              lambda _, g: (_reorder_call(name + "_bwd", g, S, not to_segments),))
    return op


def _s5_scan_call(name, Xs, A, S, dirs, prev=None):
    with_p = prev is not None
    T, N = Xs[0].shape
    B, nl = T // S, N // LANES
    n_it = S // S5_SEGMENTS
    assert n_it & (n_it - 1) == 0

    def cmul(ar, ai, br, bi):
        return ar * br - ai * bi, ar * bi + ai * br

    def body(*refs):
        x, a_ref = refs[0:4], refs[4]
        if with_p:
            h_prev, x_prev, h, p_ref = refs[5:9], refs[9:13], refs[13:17], refs[17]
        else:
            h = refs[5:9]
        seg = lax.broadcasted_iota(jnp.int32, (S5_SEGMENTS, 1), 0)
        zero = jnp.zeros((S5_SEGMENTS, LANES), F32)
        a = [(jnp.broadcast_to(a_ref[2 * k:2 * k + 1, :], (S5_SEGMENTS, LANES)),
              jnp.broadcast_to(a_ref[2 * k + 1:2 * k + 2, :], (S5_SEGMENTS, LANES))) for k in range(2)]

        def rows_of(k, i):
            return pl.ds(pl.multiple_of(((n_it - 1 - i) if dirs[k] == 1 else i) * S5_SEGMENTS, S5_SEGMENTS), S5_SEGMENTS)

        def local(i, carry):
            out = []
            for k, (sr, si) in enumerate(carry):
                rows = rows_of(k, i)
                pr, pi = cmul(*a[k], sr, si)
                sr, si = pr + x[2 * k][rows, :], pi + x[2 * k + 1][rows, :]
                h[2 * k][rows, :] = sr
                h[2 * k + 1][rows, :] = si
                out.append((sr, si))
            return tuple(out)

        ends = lax.fori_loop(0, n_it, local, ((zero, zero), (zero, zero)), unroll=8)
        inherit = []
        for k, (er, ei) in enumerate(ends):
            back = dirs[k] == 1
            pr, pi = a[k]
            for _ in range(n_it.bit_length() - 1):
                pr, pi = cmul(pr, pi, pr, pi)
            shift = (S5_SEGMENTS - 1) if back else 1
            tr, ti = er, ei
            for r in (range(S5_SEGMENTS - 2, -1, -1) if back else range(1, S5_SEGMENTS)):
                nr, ni = cmul(pr, pi, pltpu.roll(tr, shift, 0), pltpu.roll(ti, shift, 0))
                tr = jnp.where(seg == r, er + nr, tr)
                ti = jnp.where(seg == r, ei + ni, ti)
            edge = (S5_SEGMENTS - 1) if back else 0
            inherit.append((jnp.where(seg == edge, 0.0, pltpu.roll(tr, shift, 0)), jnp.where(seg == edge, 0.0, pltpu.roll(ti, shift, 0))))

        def fix(i, carry):
            powers, sums = carry
            new_powers, new_sums = [], []
            for k in range(2):
                rows = rows_of(k, i)
                fr, fi = cmul(*powers[k], *inherit[k])
                sr, si = h[2 * k][rows, :] + fr, h[2 * k + 1][rows, :] + fi
                h[2 * k][rows, :] = sr
                h[2 * k + 1][rows, :] = si
                new_powers.append(cmul(*powers[k], *a[k]))
                if with_p:
                    ur = h_prev[2 * k][rows, :] - x_prev[2 * k][rows, :]
                    ui = h_prev[2 * k + 1][rows, :] - x_prev[2 * k + 1][rows, :]
                    new_sums.append((sums[k][0] + sr * ur + si * ui, sums[k][1] + si * ur - sr * ui))
            return tuple(new_powers), tuple(new_sums)

        _, sums = lax.fori_loop(0, n_it, fix, ((a[0], a[1]), ((zero, zero), (zero, zero)) if with_p else ()), unroll=8)
        if with_p:
            for k in range(2):
                p_ref[2 * k:2 * k + 1, :] = jnp.sum(sums[k][0], axis=0, keepdims=True)
                p_ref[2 * k + 1:2 * k + 2, :] = jnp.sum(sums[k][1], axis=0, keepdims=True)

    col = pl.BlockSpec((S, LANES), lambda b, j: (b, j))
    outs = pl.pallas_call(
        body, name=name, grid=(B, nl),
        in_specs=[col] * 4 + [pl.BlockSpec((4, LANES), lambda b, j: (0, j))] + ([col] * 8 if with_p else []),
        out_specs=[col] * 4 + ([pl.BlockSpec((None, 4, LANES), lambda b, j: (b, 0, j))] if with_p else []),
        out_shape=[jax.ShapeDtypeStruct((T, N), F32)] * 4 + ([jax.ShapeDtypeStruct((B, 4, N), F32)] if with_p else []),
        compiler_params=_params("parallel", "parallel"),
    )(*Xs, A, *(prev[0] + prev[1] if with_p else ()))
    return (tuple(outs[:4]), outs[4]) if with_p else tuple(outs)


def make_s5_scan(name, S):
    @jax.custom_vjp
    def op(X, A):
        return _s5_scan_call(name + "_fwd", X, A, S, (0, 1))

    def fwd(X, A):
        H = _s5_scan_call(name + "_fwd", X, A, S, (0, 1))
        return H, (X, A, H)

    def bwd(res, G):
        X, A, H = res
        conj = A * jnp.array([[1.0], [-1.0], [1.0], [-1.0]], F32)
        lam, P = _s5_scan_call(name + "_bwd", tuple(G), conj, S, (1, 0), prev=(tuple(H), tuple(X)))
        P = jnp.sum(P, axis=0)
        ar, ai = A[0::2], A[1::2]
        pr, pi = P[0::2], P[1::2]
        den = ar * ar + ai * ai
        dar, dai = (pr * ar - pi * ai) / den, (pr * ai + pi * ar) / den
        return lam, jnp.stack([dar[0], dai[0], dar[1], dai[1]], axis=0)

    op.defvjp(fwd, bwd)
    return op


ANY = pl.BlockSpec(memory_space=pl.ANY)


def _place():
    x, y, c = lax.axis_index("x"), lax.axis_index("y"), lax.axis_index("c")
    return x, y, c, [(1 - x, y), (x, 1 - y), (1 - x, 1 - y)]


def all_gather(name, arrs):
    n = len(arrs)

    def body(*refs):
        ins, outs = refs[:n], refs[n:2 * n]
        send, recv, lsem = refs[2 * n:]
        x, y, c, chips = _place()
        me, sibling = (x, y, c), (x, y, 1 - c)

        def copy(a, k, block, to, src=None):
            slot = outs[a].at[4 * block[0] + 2 * block[1] + block[2]]
            return pltpu.make_async_remote_copy(src_ref=slot if src is None else src, dst_ref=slot, send_sem=send.at[a, k],
                                                recv_sem=recv.at[a, k], device_id=to, device_id_type=MESH)

        mine = [pltpu.make_async_copy(ins[a], outs[a].at[4 * x + 2 * y + c], lsem.at[a]) for a in range(n)]
        first = []
        for a in range(n):
            mine[a].start()
            first.append(copy(a, 0, me, sibling, src=ins[a]))
            first += [copy(a, 1 + j, me, (*chip, c), src=ins[a]) for j, chip in enumerate(chips)]
        for cp in first:
            cp.start()
        passed = []
        for a in range(n):
            for j, chip in enumerate(chips):
                copy(a, 1 + j, (*chip, c), me).wait_recv()
                fwd = copy(a, 4 + j, (*chip, c), sibling)
                fwd.start()
                passed.append(fwd)
        for a in range(n):
            copy(a, 0, sibling, me).wait_recv()
            for j, chip in enumerate(chips):
                copy(a, 4 + j, (*chip, 1 - c), me).wait_recv()
        for cp in first + passed:
            cp.wait_send()
        for cp in mine:
            cp.wait()

    return pl.pallas_call(
        body, name=name, in_specs=[ANY] * n, out_specs=[ANY] * n,
        out_shape=[jax.ShapeDtypeStruct((N_DEV,) + a.shape, a.dtype) for a in arrs],
        scratch_shapes=[pltpu.SemaphoreType.DMA((n, 7)), pltpu.SemaphoreType.DMA((n, 7)), pltpu.SemaphoreType.DMA((n,))],
    )(*arrs)


HBM = pl.BlockSpec(memory_space=pltpu.HBM)
SEM = pl.BlockSpec(memory_space=pltpu.SEMAPHORE)
EFFECT = pltpu.SideEffectType.DATAFLOW_SIDE_EFFECTING
GATHER_PEERS = (1, 2, 4, 6)
OTHER_CHIPS = (2, 4, 6)
COPIES_PER_ARRAY = {"scatter": N_DEV - 1, "gather": len(GATHER_PEERS), "forward": len(OTHER_CHIPS)}


def _split_plan(mode, srcs, lands, send, recv):
    x, y, c = lax.axis_index("x"), lax.axis_index("y"), lax.axis_index("c")

    def dev(k):
        return (1 - x if k & 4 else x), (1 - y if k & 2 else y), (1 - c if k & 1 else c)

    def idx(d):
        return 4 * d[0] + 2 * d[1] + d[2]

    me = idx((x, y, c))
    plan = []
    for a, land in enumerate(lands):
        if mode == "scatter":
            legs = [(srcs[a].at[idx(dev(k))], land.at[me], land.at[idx(dev(k))], dev(k)) for k in range(1, N_DEV)]
        elif mode == "gather":
            legs = [(srcs[a], land.at[me], land.at[idx(dev(k))], dev(k)) for k in GATHER_PEERS]
        else:
            legs = [(land.at[idx(dev(j))], land.at[idx(dev(j))], land.at[idx(dev(j ^ 1))], dev(1)) for j in OTHER_CHIPS]
        for i, (src, dst, arrival, to) in enumerate(legs):
            sem = a * len(legs) + i
            pair = tuple(pltpu.make_async_remote_copy(src_ref=src, dst_ref=d, send_sem=send.at[sem], recv_sem=recv.at[sem],
                                                      device_id=to, device_id_type=MESH) for d in (dst, arrival))
            plan.append(pair)
    return plan


def split_start(name, mode, srcs, lands, after):
    if lands is None:
        lands = [lax.empty((N_DEV,) + (s.shape[1:] if mode == "scatter" else s.shape), s.dtype) for s in srcs]
    ns, nl = len(srcs), len(lands)
    nsem = COPIES_PER_ARRAY[mode] * nl

    def body(*refs):
        ins, lnd = refs[:ns], refs[ns:ns + nl]
        send, recv = refs[ns + nl + 1], refs[ns + nl + 2]
        token = refs[-1]
        for out, _ in _split_plan(mode, ins, lnd, send, recv):
            out.start()
        token[...] = jnp.zeros_like(token)

    arrs = list(srcs) + list(lands)
    return pl.pallas_call(
        body, name=name,
        out_shape=(pltpu.SemaphoreType.DMA((nsem,)), pltpu.SemaphoreType.DMA((nsem,)))
        + tuple(pltpu.HBM(t.shape, t.dtype) for t in arrs) + (jax.ShapeDtypeStruct((SUBLANES, LANES), F32),),
        in_specs=[HBM] * len(arrs) + [ANY],
        out_specs=(SEM, SEM) + (HBM,) * len(arrs) + (pl.BlockSpec(memory_space=pltpu.VMEM),),
        input_output_aliases={i: 2 + i for i in range(len(arrs))},
        compiler_params=pltpu.CompilerParams(has_side_effects=EFFECT),
    )(*[pltpu.with_memory_space_constraint(t, pltpu.HBM) for t in arrs], after)


def split_wait(name, mode, handle, after):
    send, recv = handle[0], handle[1]
    arrs = list(handle[2:-1])
    nl = len(arrs) if mode == "forward" else len(arrs) // 2
    ns = len(arrs) - nl

    def body(*refs):
        ins, lnd = refs[:ns], refs[ns:ns + nl]
        s, r = refs[ns + nl], refs[ns + nl + 1]
        for out, arrival in _split_plan(mode, ins, lnd, s, r):
            out.wait_send()
            arrival.wait_recv()

    outs = pl.pallas_call(
        body, name=name,
        out_shape=tuple(pltpu.HBM(t.shape, t.dtype) for t in arrs),
        in_specs=[HBM] * len(arrs) + [SEM, SEM, ANY], out_specs=(HBM,) * len(arrs),
        input_output_aliases={i: i for i in range(len(arrs))},
        compiler_params=pltpu.CompilerParams(has_side_effects=EFFECT),
    )(*arrs, send, recv, after)
    return list(outs[ns:])


def _row_tile(rows, cols):
    cap = max(SUBLANES, (2**18 // cols) // SUBLANES * SUBLANES)
    if rows <= cap:
        return rows
    for t in range(cap, SUBLANES - 1, -SUBLANES):
        if rows % t == 0:
            return t
    return rows


def ordered_sum(name, parts):
    n, R, C = parts.shape
    rt = _row_tile(R, C)

    def body(p_ref, o_ref):
        s = p_ref[0]
        for k in range(1, n):
            s = s + p_ref[k]
        o_ref[...] = s

    return pl.pallas_call(
        body, name=name, grid=(R // rt,),
        in_specs=[pl.BlockSpec((n, rt, C), lambda r: (0, r, 0))], out_specs=pl.BlockSpec((rt, C), lambda r: (r, 0)),
        out_shape=jax.ShapeDtypeStruct((R, C), F32), compiler_params=_params("parallel"),
    )(parts)


def _adamw_update(w, m, v, g):
    bias1 = 1.0 - ADAM_B1 ** ADAM_STEP
    bias2 = 1.0 - ADAM_B2 ** ADAM_STEP
    m_new = ADAM_B1 * m + (1.0 - ADAM_B1) * g
    v_new = ADAM_B2 * v + (1.0 - ADAM_B2) * (g * g)
    delta = -ADAM_LR * ((m_new / bias1) / (jnp.sqrt(v_new / bias2) + ADAM_EPS) + ADAM_WD * w)
    return delta, m_new, v_new


def adamw(name, w, m, v, g):
    R, C = w.shape
    rt = _row_tile(R, C)

    def body(w_ref, m_ref, v_ref, g_ref, d_ref, mo_ref, vo_ref):
        d_ref[...], mo_ref[...], vo_ref[...] = _adamw_update(w_ref[...], m_ref[...], v_ref[...], g_ref[...])

    row = pl.BlockSpec((rt, C), lambda r: (r, 0))
    return pl.pallas_call(
        body, name=name, grid=(R // rt,), in_specs=[row] * 4, out_specs=[row] * 3,
        out_shape=[jax.ShapeDtypeStruct((R, C), F32)] * 3, compiler_params=_params("parallel"),
    )(w, m, v, g)


def adamw_sharded(name, layer, w, m, v, own, land, me, prev):
    _, R, C = own.shape
    rt = _row_tile(R, C)
    base = layer * (R // rt)

    def body(me_ref, w_ref, m_ref, v_ref, own_ref, land_ref, *rest):
        go_ref, d_ref, mo_ref, vo_ref = rest[-4:]
        g = own_ref[...]
        for k in range(N_DEV):
            g = g + jnp.where(me_ref[0] == k, 0.0, land_ref[k].astype(F32))
        go_ref[...] = g
        d_ref[...], mo_ref[...], vo_ref[...] = _adamw_update(w_ref[...], m_ref[...], v_ref[...], g)

    row = pl.BlockSpec((rt, C), lambda r, p: (base + r, 0))
    in_specs = [row, row, row, pl.BlockSpec((None, rt, C), lambda r, p: (p[0], r, 0)), pl.BlockSpec((N_DEV, rt, C), lambda r, p: (0, r, 0))]
    ins = [me, w, m, v, own, land]
    aliases = {}
    if prev is not None:
        in_specs += [ANY] * 4
        aliases = {len(ins) + k: k for k in range(4)}
        ins += list(prev)
    return pl.pallas_call(
        body, name=name,
        grid_spec=pltpu.PrefetchScalarGridSpec(num_scalar_prefetch=1, grid=(R // rt,), in_specs=in_specs, out_specs=[row] * 4),
        out_shape=[jax.ShapeDtypeStruct(w.shape, F32)] * 4, input_output_aliases=aliases,
        compiler_params=_params("arbitrary"),
    )(*ins)


def _hgrn_lower_bounds(lb_logits):
    p = jax.nn.softmax(lb_logits, axis=1)
    return jnp.cumsum(p, axis=1) - p[:, :1]


def _s5_discretise(lam_re, lam_im, log_dt, b_re, b_im):
    lr = jnp.minimum(lam_re, -1e-4)
    li = lam_im
    dt = jnp.exp(log_dt)[:, None]
    mag = jnp.exp(lr * dt)
    ar, ai = mag * jnp.cos(li * dt), mag * jnp.sin(li * dt)
    den = lr * lr + li * li
    nr = ar - 1.0
    cr = (nr * lr + ai * li) / den
    ci = (ai * lr - nr * li) / den
    bbr = cr[..., None] * b_re - ci[..., None] * b_im
    bbi = cr[..., None] * b_im + ci[..., None] * b_re
    return ar, ai, bbr, bbi


def _block_diag(t):
    G, a, b = t.shape
    eye = jnp.eye(G, dtype=F32)
    return (t[:, :, None, :] * eye[:, None, :, None]).reshape(G * a, G * b)


def _rope_tables(S):
    half = RET_DK // 2
    inv = ROPE_BASE ** (-jnp.arange(half, dtype=F32) / half)
    ang = jnp.arange(S, dtype=F32)[:, None] * inv[None, :]
    cos, sin = jnp.cos(ang), jnp.sin(ang)
    return jnp.concatenate([cos, cos], axis=1), jnp.concatenate([-sin, sin], axis=1)


def _ret_log_decays():
    f = tuple(float(np.log1p(-np.exp2(np.float32(-5.0 - h)))) for h in range(RET_HEADS))
    b = tuple(float(np.log1p(-np.exp2(np.float32(-5.5 - h)))) for h in range(RET_HEADS))
    return f, b


def assemble_weight(name, land, own, me, col_sharded):
    _, R, C = land.shape
    if col_sharded:
        tr = min(R, 256)

        def body(me_ref, land_ref, own_ref, o_ref):
            for d in range(N_DEV):
                o_ref[:, d * C:(d + 1) * C] = jnp.where(me_ref[0] == d, own_ref[...], land_ref[d])

        grid, out_shape = (R // tr,), (R, N_DEV * C)
        in_specs = [pl.BlockSpec((N_DEV, tr, C), lambda i, p: (0, i, 0)), pl.BlockSpec((tr, C), lambda i, p: (i, 0))]
        out_spec = pl.BlockSpec((tr, N_DEV * C), lambda i, p: (i, 0))
    else:
        def body(me_ref, land_ref, own_ref, o_ref):
            o_ref[...] = jnp.where(me_ref[0] == pl.program_id(0), own_ref[...], land_ref[...])

        grid, out_shape = (N_DEV,), (N_DEV * R, C)
        in_specs = [pl.BlockSpec((None, R, C), lambda d, p: (d, 0, 0)), pl.BlockSpec((R, C), lambda d, p: (0, 0))]
        out_spec = pl.BlockSpec((R, C), lambda d, p: (d, 0))
    return pl.pallas_call(
        body, name=name,
        grid_spec=pltpu.PrefetchScalarGridSpec(num_scalar_prefetch=1, grid=grid, in_specs=in_specs, out_specs=out_spec),
        out_shape=jax.ShapeDtypeStruct(out_shape, land.dtype), compiler_params=_params("parallel"),
    )(me, land, own)


def _row(t):
    return t.reshape(1, -1)


def mixer_stage(layer, S, resid, branch, w_in, w_out, carriers, small):
    j = layer // 2
    tag = f"l{layer}"
    g = _row(small["mix_norm_g"][layer])
    if branch is None:
        x = resid
        (h,) = make_rowop(tag + "_norm", norm_f, (D_MODEL,))((x,), (g,), ())
    else:
        x, h = make_rowop(tag + "_addnorm", addnorm_f, (D_MODEL, D_MODEL))((resid, branch), (g,), ())
    if layer % 2 == 0:
        lbs = _hgrn_lower_bounds(small["hgrn_lb_logits"])
        prm = (small["gla_wa2"][j, 0], small["gla_wa2"][j, 1], _row(small["gla_ba"][j, 0]), _row(small["gla_ba"][j, 1]),
               _row(lbs[0, j]), _row(lbs[1, j]))
        outs = make_proj_stage(tag + "_prep", even_prep_f, EVEN_PREP_WIDTHS)(h, w_in, carriers["in32"], carriers["in16"], prm, ())
        gla_qkv, (glaf, glab, gr), hgrn_qkv, (hlaf, hlab, hg) = outs[0:6], outs[6:9], outs[9:15], outs[15:18]
        of, ob = make_scan(tag + "_gla", GLA_HEADS, GLA_DK, GLA_DV, S, SCAN_CHUNK)(gla_qkv, glaf, glab)
        hof, hob = make_scan(tag + "_hgrn", HGRN_HEADS, HGRN_DK, HGRN_DV, S, HGRN_SCAN_CHUNK)(hgrn_qkv, hlaf, hlab)
        (y,) = make_rowop(tag + "_post", even_post_f, (D_MODEL,))(
            (of, ob, hof, hob, gr, hg), (_row(small["gla_norm_g"][j]), _row(small["hgrn_norm_g"][j])), ())
    else:
        cosf, sinf = _rope_tables(S)
        outs = make_proj_stage(tag + "_prep", odd_prep_f, ODD_PREP_WIDTHS)(h, w_in, carriers["in32"], carriers["in16"], (), (cosf, sinf))
        ret_qkv, rg, su = outs[0:6], outs[6], outs[7]
        of, ob = make_scan(tag + "_ret", RET_HEADS, RET_DK, RET_DV, S, SCAN_CHUNK, const_lg=_ret_log_decays())(ret_qkv)
        (cm,) = make_rowop(tag + "_retpost", ret_post_f, (RET_HEADS * RET_DV,))((of, ob, rg), (_row(small["ret_norm_g"][j]),), ())
        disc = [_s5_discretise(small["s5_lam_re"][j, d], small["s5_lam_im"][j, d], small["s5_log_dt"][j, d],
                               small["s5_b_re"][j], small["s5_b_im"][j]) for d in range(2)]
        a4 = jnp.stack([t.reshape(-1) for d in range(2) for t in disc[d][:2]], axis=0)
        su_p = make_reorder(tag + "_s5seg", S, True)(su)
        bd = jnp.concatenate([_block_diag(jnp.swapaxes(disc[d][2 + i], 1, 2)) for d in range(2) for i in range(2)], axis=1)
        Xs = make_mm_groups(tag + "_s5in", 4)(su_p, bd)
        Hs = make_s5_scan(tag + "_s5scan", S)(Xs, a4)
        prm = (_block_diag(jnp.swapaxes(small["s5_c_re"][j], 1, 2)), _block_diag(jnp.swapaxes(small["s5_c_im"][j], 1, 2)),
               _row(small["s5_d"][j]), small["s5_glu_w"][j], _row(small["s5_glu_b"][j]))
        (dm_p,) = make_rowop(tag + "_s5post", s5_post_f, (S5_WIDTH,))((*Hs, su_p), prm, ())
        y = make_join(tag + "_join", S, RET_HEADS * RET_DV, S5_WIDTH)(cm, dm_p)
    return (x, *make_mm(tag + "_out")(y, w_out, carriers["out32"], carriers["out16"]))


def ffn_stage(layer, S, resid, branch, w_up, w_down, carriers, small):
    tag = f"l{layer}"
    x, hf = make_rowop(tag + "_ffnnorm", addnorm_f, (D_MODEL, D_MODEL))((resid, branch), (_row(small["ffn_norm_g"][layer]),), ())
    out, tick = make_ffn(tag + "_ffn", S)(hf, w_up, carriers["up32"], carriers["up16"], w_down, carriers["down32"], carriers["down16"],
                                         small["ffn_conv_w"][layer], _row(small["ffn_conv_b"][layer]))
    return x, out, tick


BIG = {"w_in_even": 2, "w_out_even": 1, "w_in_odd": 2, "w_out_odd": 1, "ffn_w_up": 2, "ffn_w_down": 1}
SMALL_SHARDED = {"gla_wa2": 3, "gla_ba": 2, "hgrn_lb_logits": 2, "ret_norm_g": 1, "s5_d": 1, "s5_glu_w": 1, "s5_glu_b": 1,
                 "ffn_conv_w": 2}
REPLICATED = ("mix_norm_g", "ffn_norm_g", "final_norm_g", "gla_norm_g", "hgrn_norm_g", "s5_lam_re", "s5_lam_im", "s5_log_dt",
              "s5_b_re", "s5_b_im", "s5_c_re", "s5_c_im", "ffn_conv_b")
WEIGHTS = ("mix_norm_g", "ffn_norm_g", "final_norm_g", "w_in_even", "w_out_even", "gla_wa2", "gla_ba", "gla_norm_g",
           "hgrn_lb_logits", "hgrn_norm_g", "w_in_odd", "w_out_odd", "ret_norm_g", "s5_lam_re", "s5_lam_im", "s5_log_dt",
           "s5_b_re", "s5_b_im", "s5_c_re", "s5_c_im", "s5_d", "s5_glu_w", "s5_glu_b", "ffn_w_up", "ffn_conv_w", "ffn_conv_b",
           "ffn_w_down")
PACK_COLS = LANES
MIXER_SMALL = (("mix_norm_g", "hgrn_lb_logits", "gla_wa2", "gla_ba", "gla_norm_g", "hgrn_norm_g"),
               ("mix_norm_g", "ret_norm_g", "s5_lam_re", "s5_lam_im", "s5_log_dt", "s5_b_re", "s5_b_im", "s5_c_re", "s5_c_im",
                "s5_d", "s5_glu_w", "s5_glu_b"))
FFN_SMALL = ("ffn_norm_g", "ffn_conv_w", "ffn_conv_b")


def _unshard(g, axis):
    t = jnp.moveaxis(g, 0, axis)
    return t.reshape(t.shape[:axis] + (t.shape[axis] * t.shape[axis + 1],) + t.shape[axis + 2:])


def _pack_rows(shape):
    return -(-int(np.prod(shape)) // (SUBLANES * PACK_COLS)) * SUBLANES


def _pack(arrs):
    parts = []
    for a in arrs:
        rows = _pack_rows(a.shape)
        parts.append(jnp.pad(a.reshape(-1), (0, rows * PACK_COLS - a.size)).reshape(rows, PACK_COLS))
    return jnp.concatenate(parts, axis=0)


def _unpack(packed, shapes):
    lead = packed.shape[:-2]
    out, r = [], 0
    for s in shapes:
        rows, n = _pack_rows(s), int(np.prod(s))
        piece = packed[..., r:r + rows, :].reshape(lead + (rows * PACK_COLS,))
        out.append(piece[..., :n].reshape(lead + tuple(s)))
        r += rows
    return out


def _flat2d(a):
    return a.reshape(-1, a.shape[-1])


def kernel(x, mix_norm_g, ffn_norm_g, final_norm_g, w_in_even, w_out_even, gla_wa2, gla_ba, gla_norm_g, hgrn_lb_logits, hgrn_norm_g, w_in_odd, w_out_odd, ret_norm_g, s5_lam_re, s5_lam_im, s5_log_dt, s5_b_re, s5_b_im, s5_c_re, s5_c_im, s5_d, s5_glu_w, s5_glu_b, ffn_w_up, ffn_conv_w, ffn_conv_b, ffn_w_down, loss_target, m_mix_norm_g, m_ffn_norm_g, m_final_norm_g, m_w_in_even, m_w_out_even, m_gla_wa2, m_gla_ba, m_gla_norm_g, m_hgrn_lb_logits, m_hgrn_norm_g, m_w_in_odd, m_w_out_odd, m_ret_norm_g, m_s5_lam_re, m_s5_lam_im, m_s5_log_dt, m_s5_b_re, m_s5_b_im, m_s5_c_re, m_s5_c_im, m_s5_d, m_s5_glu_w, m_s5_glu_b, m_ffn_w_up, m_ffn_conv_w, m_ffn_conv_b, m_ffn_w_down, v_mix_norm_g, v_ffn_norm_g, v_final_norm_g, v_w_in_even, v_w_out_even, v_gla_wa2, v_gla_ba, v_gla_norm_g, v_hgrn_lb_logits, v_hgrn_norm_g, v_w_in_odd, v_w_out_odd, v_ret_norm_g, v_s5_lam_re, v_s5_lam_im, v_s5_log_dt, v_s5_b_re, v_s5_b_im, v_s5_c_re, v_s5_c_im, v_s5_d, v_s5_glu_w, v_s5_glu_b, v_ffn_w_up, v_ffn_conv_w, v_ffn_conv_b, v_ffn_w_down):
    args = locals()
    w = {n: args[n] for n in WEIGHTS}
    m = {n: args["m_" + n] for n in WEIGHTS}
    v = {n: args["v_" + n] for n in WEIGHTS}
    Bl, S, D = x.shape
    T = Bl * S
    ix, iy, ic = lax.axis_index("x"), lax.axis_index("y"), lax.axis_index("c")
    me = 4 * ix + 2 * iy + ic

    xt = x.reshape(T, D)
    me1 = me.reshape(1).astype(jnp.int32)
    stages = []
    for layer in range(DEPTH):
        j = layer // 2
        kin, kout = ("w_in_even", "w_out_even") if layer % 2 == 0 else ("w_in_odd", "w_out_odd")
        stages.append((mixer_stage, layer, ("in", "out"), ((kin, j, True), (kout, j, False)), MIXER_SMALL[layer % 2]))
        stages.append((ffn_stage, layer, ("up", "down"), (("ffn_w_up", layer, True), ("ffn_w_down", layer, False)), FFN_SMALL))

    gather, after = [], xt
    for s, (_, _, _, projs, _) in enumerate(stages):
        handle = split_start(f"gather{s}_start", "gather", [w[n][l].astype(BF16) for n, l, _ in projs], None, after)
        gather.append(handle)
        after = handle[-1]
    sm_names = list(SMALL_SHARDED)
    (sm_all8,) = all_gather("gather_small", [_pack([w[n] for n in sm_names])])
    small = {n: _unshard(t, SMALL_SHARDED[n]) for n, t in zip(sm_names, _unpack(sm_all8, [w[n].shape for n in sm_names]))}
    small.update({n: w[n] for n in REPLICATED})
    small["mix_norm_g"] = small["mix_norm_g"] + after[0, 0]

    resid, branch, pulls = xt, None, []
    second = [None] * len(stages)

    def second_level(s, after):
        lands = split_wait(f"gather{s}_wait", "gather", gather[s], after)
        second[s] = split_start(f"forward{s}_start", "forward", [], lands, after)

    for s, (fn, layer, keys, projs, sm_keys) in enumerate(stages):
        here = lax.stop_gradient(resid)
        if second[s] is None:
            second_level(s, here)
        lands = split_wait(f"forward{s}_wait", "forward", second[s], second[s][-1])
        sm = {n: small[n] for n in sm_keys}
        if 2 <= s < len(stages) - 1:
            second_level(s + 1, here)
            norm = "mix_norm_g" if fn is mixer_stage else "ffn_norm_g"
            sm[norm] = sm[norm] + second[s + 1][-1][0, 0]
        full, carriers = [], {}
        for key, land, (n, l, col) in zip(keys, lands, projs, strict=True):
            full.append(assemble_weight(f"weight{s}_{key}", land, w[n][l].astype(BF16), me1, col))
            carriers[key + "32"] = jnp.zeros(land.shape, F32)
            carriers[key + "16"] = jnp.zeros(land.shape, BF16)
        run = functools.partial(fn, layer, S)
        if branch is None:
            (resid, branch, _), pull = jax.vjp(lambda r, c, p, run=run, full=full: run(r, None, full[0], full[1], c, p), resid, carriers, sm)
        else:
            (resid, branch, _), pull = jax.vjp(lambda r, b, c, p, run=run, full=full: run(r, b, full[0], full[1], c, p), resid, branch, carriers, sm)
        pulls.append(pull)

    loss_acc, dxf, dgf = loss_head(resid, branch, small["final_norm_g"].reshape(1, D), loss_target.reshape(T, D))
    loss = lax.psum(loss_acc[0, 0], ("x", "y", "c"))
    g_small = {"final_norm_g": dgf.reshape(D)}
    d_resid, d_branch, token = dxf, dxf, _tick()
    scatter, own32 = [None] * len(stages), [None] * len(stages)
    for s in reversed(range(len(stages))):
        if s == 0:
            d_resid, dcar, dsm = pulls[s]((d_resid, d_branch, token))
        else:
            d_resid, d_branch, dcar, dsm = pulls[s]((d_resid, d_branch, token))
        for n, g in dsm.items():
            g_small[n] = g_small[n] + g if n in g_small else g
        keys = stages[s][2]
        own32[s] = [dcar[k + "32"] for k in keys]
        scatter[s] = split_start(f"grads{s}_start", "scatter", [dcar[k + "16"] for k in keys], None, d_resid)
        token = scatter[s][-1]
    dx = d_resid

    out = {}
    sm_all = sm_names + list(REPLICATED)
    (g_all,) = all_gather("gather_small_grads", [_pack([g_small[n] for n in sm_all])])
    g_sum = _unpack(ordered_sum("sum_small_grads", g_all), [g_small[n].shape for n in sm_all])
    g_loc = []
    for n, g in zip(sm_all, g_sum):
        if n in SMALL_SHARDED:
            ax = SMALL_SHARDED[n]
            size = w[n].shape[ax]
            g = lax.dynamic_slice_in_dim(g, me * size, size, axis=ax)
        g_loc.append(g)
    shapes = [w[n].shape for n in sm_all]
    res = adamw("adamw_small", _pack([w[n] for n in sm_all]), _pack([m[n] for n in sm_all]), _pack([v[n] for n in sm_all]), _pack(g_loc))
    last_small = res[0]
    res = [g_loc] + [_unpack(r, shapes) for r in res]
    for k, n in enumerate(sm_all):
        out[n] = [r[k] for r in res]

    chain, behind = {}, last_small
    for s in reversed(range(len(stages))):
        lands = split_wait(f"grads{s}_wait", "scatter", scatter[s], behind if s == 0 else dx)
        for own, land, (n, l, _) in zip(own32[s], lands, stages[s][3], strict=True):
            chain[n] = adamw_sharded(f"adamw_{n}_{l}", l, _flat2d(w[n]), _flat2d(m[n]), _flat2d(v[n]), own, land, me1, chain.get(n))
            behind = chain[n][0]
    for n in BIG:
        out[n] = [t.reshape(w[n].shape) for t in chain[n]]

    grads, deltas, new_m, new_v = ([out[n][k] for n in WEIGHTS] for k in range(4))
    return (loss, dx.reshape(Bl, S, D), *grads, *deltas, *new_m, *new_v)
```

```python
import functools

import numpy as np
import jax
import jax.numpy as jnp
from jax import lax
from jax.experimental import pallas as pl
from jax.experimental.pallas import tpu as pltpu

F32 = jnp.float32
BF16 = jnp.bfloat16
HIGHEST = lax.Precision.HIGHEST
MESH = pl.DeviceIdType.MESH

D_MODEL = 1024
DEPTH = 4
N_EVEN = 2
N_ODD = 2
GLA_HEADS, GLA_DK, GLA_DV, GLA_RANK, GLA_GATE_NORM = 4, 64, 128, 16, 16.0
HGRN_HEADS, HGRN_DK, HGRN_DV, HGRN_MIN_F = 4, 64, 128, 1e-20
RET_HEADS, RET_DK, RET_DV = 4, 128, 192
ROPE_BASE = 10000.0
S5_WIDTH, S5_GROUP_CH, S5_GROUPS, S5_STATE = 256, 16, 16, 64
S5_N = S5_GROUPS * S5_STATE
FFN_DIM = 2816
EPS = 1e-6
EVEN_IN = 3360
ODD_IN = 2816
ADAM_LR, ADAM_B1, ADAM_B2, ADAM_EPS, ADAM_WD, ADAM_STEP = 0.001, 0.9, 0.999, 1e-08, 0.01, 10

N_DEV = 8
VMEM_LIMIT_BYTES = 56 * 1024 * 1024
ROW_TILE = 256
SCAN_CHUNK = 128
HGRN_SCAN_CHUNK = 64
S5_SEGMENTS = 8
LANES = 128


def _params(*sem):
    return pltpu.CompilerParams(dimension_semantics=sem, vmem_limit_bytes=VMEM_LIMIT_BYTES)


def _divisor_tile(n, cap):
    best = None
    for t in range(LANES, min(n, cap) + 1, LANES):
        if n % t == 0:
            best = t
    return best if best is not None else n


def _mm_nn(name, x, w):
    M, K = x.shape
    N = w.shape[1]
    tn = _divisor_tile(N, 3072) if K * N * 2 > 8 * 2**20 else N
    tm = 256 if tn * 4 * 512 > 6 * 2**20 else 512
    assert M % tm == 0 and N % tn == 0

    def body(x_ref, w_ref, o_ref):
        o_ref[...] = jnp.dot(x_ref[...].astype(BF16), w_ref[...], preferred_element_type=F32)

    return pl.pallas_call(
        body, name=name, grid=(N // tn, M // tm),
        in_specs=[pl.BlockSpec((tm, K), lambda j, i: (i, 0)), pl.BlockSpec((K, tn), lambda j, i: (0, j))],
        out_specs=pl.BlockSpec((tm, tn), lambda j, i: (i, j)),
        out_shape=jax.ShapeDtypeStruct((M, N), F32),
        compiler_params=_params("parallel", "parallel"),
    )(x, w)


def _mm_nt(name, dy, w, after=None):
    M, N = dy.shape
    K = w.shape[0]
    tk = _divisor_tile(K, 1024) if K * N * 2 > 8 * 2**20 else K
    tm = 256 if N >= 4096 else 512
    assert M % tm == 0 and K % tk == 0

    def body(dy_ref, w_ref, *rest):
        rest[-1][...] = lax.dot_general(dy_ref[...].astype(BF16), w_ref[...], (((1,), (1,)), ((), ())),
                                        preferred_element_type=F32)

    extra = [] if after is None else [after]
    return pl.pallas_call(
        body, name=name, grid=(K // tk, M // tm),
        in_specs=[pl.BlockSpec((tm, N), lambda j, i: (i, 0)), pl.BlockSpec((tk, N), lambda j, i: (j, 0))]
        + [pl.BlockSpec(memory_space=pl.ANY)] * len(extra),
        out_specs=pl.BlockSpec((tm, tk), lambda j, i: (i, j)),
        out_shape=jax.ShapeDtypeStruct((M, K), F32),
        compiler_params=_params("parallel", "parallel"),
    )(dy, w, *extra)


def _tick():
    return jnp.zeros((SUBLANES, LANES), F32)


MM_TN_VMEM_BUDGET = 46 * 2**20


def _pad_lanes(n):
    return -(-n // LANES) * LANES


def _mm_tn(name, x, dy, nblk, want16):
    M, K = x.shape
    N = dy.shape[1]
    n = N // nblk
    xb, yb = x.dtype.itemsize, dy.dtype.itemsize
    best = None
    for tk in [t for t in range(LANES, K + 1, LANES) if K % t == 0]:
        for tm in (512, 256):
            out_bytes = nblk * tk * _pad_lanes(n) * (6 if want16 else 4)
            vmem = 2 * out_bytes + 2 * tm * tk * xb + 2 * tm * _pad_lanes(N) * yb
            traffic = (K // tk) * M * N * yb + M * K * xb
            if vmem <= MM_TN_VMEM_BUDGET and M % tm == 0 and (best is None or (traffic, -tm) < best[0]):
                best = ((traffic, -tm), tk, tm)
    _, tk, tm = best
    last = M // tm - 1

    def body(x_ref, dy_ref, o32_ref, *o16_ref):
        m = pl.program_id(1)

        @pl.when(m == 0)
        def _():
            o32_ref[...] = jnp.zeros_like(o32_ref)

        dyv = dy_ref[...].astype(BF16)
        if nblk == 1:
            o32_ref[0] += lax.dot_general(x_ref[...].astype(BF16), dyv, (((0,), (0,)), ((), ())), preferred_element_type=F32)
        else:
            xt = x_ref[...].astype(F32).T.astype(BF16)
            for d in range(nblk):
                o32_ref[d] += jnp.dot(xt, dyv[:, d * n:(d + 1) * n], preferred_element_type=F32)
        if want16:
            @pl.when(m == last)
            def _():
                o16_ref[0][...] = o32_ref[...].astype(BF16)

    blk = pl.BlockSpec((nblk, tk, n), lambda a, m: (0, a, 0))
    return pl.pallas_call(
        body, name=name, grid=(K // tk, M // tm),
        in_specs=[pl.BlockSpec((tm, tk), lambda a, m: (m, a)), pl.BlockSpec((tm, N), lambda a, m: (m, 0))],
        out_specs=[blk, blk] if want16 else [blk],
        out_shape=[jax.ShapeDtypeStruct((nblk, K, n), F32)] + ([jax.ShapeDtypeStruct((nblk, K, n), BF16)] if want16 else []),
        compiler_params=_params("parallel", "arbitrary"),
    )(x, dy)


def _weight_grads(name, x, dy, col_sharded):
    if col_sharded:
        return _mm_tn(name, x, dy, N_DEV, True)
    d32, d16 = _mm_tn(name, x, dy, 1, True)
    K, N = d32.shape[1:]
    return d32.reshape(N_DEV, K // N_DEV, N), d16.reshape(N_DEV, K // N_DEV, N)


def make_mm(name, col_sharded=False):
    @jax.custom_vjp
    def mm(x, w16, c32, c16):
        return _mm_nn(name + "_fwd", x, w16), _tick()

    def fwd(x, w16, c32, c16):
        return (_mm_nn(name + "_fwd", x, w16), _tick()), (x, w16)

    def bwd(res, g):
        x, w16 = res
        dy, after = g
        dx = _mm_nt(name + "_dx", dy, w16, after)
        d32, d16 = _weight_grads(name + "_dw", x, dy, col_sharded)
        return dx, jnp.zeros_like(w16), d32, d16

    mm.defvjp(fwd, bwd)
    return mm


def make_mm_groups(name, G, tm=512):
    def products(x, w16):
        M, K = x.shape
        N = w16.shape[1] // G

        def body(x_ref, w_ref, *o_refs):
            xv = x_ref[...].astype(BF16)
            for g, o_ref in enumerate(o_refs):
                o_ref[...] = jnp.dot(xv, w_ref[:, g * N:(g + 1) * N], preferred_element_type=F32)

        row = pl.BlockSpec((tm, N), lambda i: (i, 0))
        return tuple(pl.pallas_call(
            body, name=name + "_fwd", grid=(M // tm,),
            in_specs=[pl.BlockSpec((tm, K), lambda i: (i, 0)), pl.BlockSpec((K, G * N), lambda i: (0, 0))],
            out_specs=[row] * G, out_shape=[jax.ShapeDtypeStruct((M, N), F32)] * G, compiler_params=_params("parallel"),
        )(x, w16))

    def grads(x, w16, dys):
        M, K = x.shape
        N = w16.shape[1] // G

        def body(x_ref, w_ref, *refs):
            dy_refs, dx_ref, dw_ref = refs[:G], refs[G], refs[G + 1]

            @pl.when(pl.program_id(0) == 0)
            def _():
                dw_ref[...] = jnp.zeros_like(dw_ref)

            xv = x_ref[...].astype(BF16)
            dx = jnp.zeros(dx_ref.shape, F32)
            for g in range(G):
                cols = slice(g * N, (g + 1) * N)
                dyv = dy_refs[g][...].astype(BF16)
                dx = dx + lax.dot_general(dyv, w_ref[:, cols], (((1,), (1,)), ((), ())), preferred_element_type=F32)
                dw_ref[:, cols] += lax.dot_general(xv, dyv, (((0,), (0,)), ((), ())), preferred_element_type=F32)
            dx_ref[...] = dx

        row = pl.BlockSpec((tm, N), lambda i: (i, 0))
        return pl.pallas_call(
            body, name=name + "_bwd", grid=(M // tm,),
            in_specs=[pl.BlockSpec((tm, K), lambda i: (i, 0)), pl.BlockSpec((K, G * N), lambda i: (0, 0))] + [row] * G,
            out_specs=[pl.BlockSpec((tm, K), lambda i: (i, 0)), pl.BlockSpec((K, G * N), lambda i: (0, 0))],
            out_shape=[jax.ShapeDtypeStruct((M, K), F32), jax.ShapeDtypeStruct((K, G * N), F32)],
            compiler_params=_params("arbitrary"),
        )(x, w16, *dys)

    @jax.custom_vjp
    def mm(x, w):
        return products(x, w.astype(BF16))

    def fwd(x, w):
        w16 = w.astype(BF16)
        return products(x, w16), (x, w16)

    def bwd(res, dys):
        x, w16 = res
        return tuple(grads(x, w16, tuple(dys)))

    mm.defvjp(fwd, bwd)
    return mm


def _row_specs(rows, params, consts, tile):
    specs = [pl.BlockSpec((tile, r.shape[1]), lambda i: (i, 0)) for r in rows]
    specs += [pl.BlockSpec(p.shape, lambda i: (0, 0)) for p in params]
    specs += [pl.BlockSpec((tile, c.shape[1]), lambda i, n=c.shape[0] // tile: (i % n, 0)) for c in consts]
    return specs


def _row_fwd(name, f, out_widths, tile, rows, params, consts):
    T = rows[0].shape[0]
    nr, npar, ncon = len(rows), len(params), len(consts)

    def body(*refs):
        r = tuple(x[...] for x in refs[:nr])
        p = tuple(x[...] for x in refs[nr:nr + npar])
        c = tuple(x[...] for x in refs[nr + npar:nr + npar + ncon])
        outs = f(r, p, c)
        for o_ref, o in zip(refs[nr + npar + ncon:], outs, strict=True):
            o_ref[...] = o

    return pl.pallas_call(
        body, name=name + "_fwd", grid=(T // tile,),
        in_specs=_row_specs(rows, params, consts, tile),
        out_specs=[pl.BlockSpec((tile, w), lambda i: (i, 0)) for w in out_widths],
        out_shape=[jax.ShapeDtypeStruct((T, w), F32) for w in out_widths],
        compiler_params=_params("parallel"),
    )(*rows, *params, *consts)


def _row_bwd(name, f, out_widths, tile, rows, params, consts, gouts, dr_dtype=F32):
    T = rows[0].shape[0]
    nr, npar, ncon, nout = len(rows), len(params), len(consts), len(out_widths)

    def body(*refs):
        r = tuple(x[...] for x in refs[:nr])
        p = tuple(x[...] for x in refs[nr:nr + npar])
        c = tuple(x[...] for x in refs[nr + npar:nr + npar + ncon])
        k = nr + npar + ncon
        g = tuple(x[...] for x in refs[k:k + nout])
        dr_refs = refs[k + nout:k + nout + nr]
        dp_refs = refs[k + nout + nr:]
        _, vjp = jax.vjp(lambda r_, p_: tuple(f(r_, p_, c)), r, p)
        dr, dp = vjp(g)
        for ref, val in zip(dr_refs, dr, strict=True):
            ref[...] = val.astype(ref.dtype)
        if npar:
            @pl.when(pl.program_id(0) == 0)
            def _():
                for ref in dp_refs:
                    ref[...] = jnp.zeros_like(ref)

            for ref, val in zip(dp_refs, dp, strict=True):
                ref[...] += val

    outs = pl.pallas_call(
        body, name=name + "_bwd", grid=(T // tile,),
        in_specs=_row_specs(rows, params, consts, tile) + [pl.BlockSpec((tile, w), lambda i: (i, 0)) for w in out_widths],
        out_specs=[pl.BlockSpec((tile, r.shape[1]), lambda i: (i, 0)) for r in rows]
        + [pl.BlockSpec(p.shape, lambda i: (0, 0)) for p in params],
        out_shape=[jax.ShapeDtypeStruct(r.shape, dr_dtype) for r in rows] + [jax.ShapeDtypeStruct(p.shape, F32) for p in params],
        compiler_params=_params("arbitrary"),
    )(*rows, *params, *consts, *gouts)
    return tuple(outs[:nr]), tuple(outs[nr:])


def make_proj_stage(name, f, out_widths, tile=ROW_TILE):
    def run(x, w16, params, consts):
        p = _mm_nn(name + "_mm", x, w16)
        return p, tuple(_row_fwd(name, f, out_widths, tile, (p,), params, consts))

    @jax.custom_vjp
    def op(x, w16, c32, c16, params, consts):
        return run(x, w16, params, consts)[1]

    def fwd(x, w16, c32, c16, params, consts):
        p, outs = run(x, w16, params, consts)
        return outs, (x, w16, p, params, consts)

    def bwd(res, g):
        x, w16, p, params, consts = res
        (dp,), dparams = _row_bwd(name, f, out_widths, tile, (p,), params, consts, tuple(g), dr_dtype=BF16)
        d32, d16 = _weight_grads(name + "_dw", x, dp, True)
        return _mm_nt(name + "_dx", dp, w16), jnp.zeros_like(w16), d32, d16, dparams, tuple(jnp.zeros_like(c) for c in consts)

    op.defvjp(fwd, bwd)
    return op


def make_rowop(name, f, out_widths, tile=ROW_TILE):
    @jax.custom_vjp
    def op(rows, params, consts):
        return tuple(_row_fwd(name, f, out_widths, tile, rows, params, consts))

    def fwd(rows, params, consts):
        return op(rows, params, consts), (rows, params, consts)

    def bwd(res, g):
        rows, params, consts = res
        dr, dp = _row_bwd(name, f, out_widths, tile, rows, params, consts, tuple(g))
        return dr, dp, tuple(jnp.zeros_like(c) for c in consts)

    op.defvjp(fwd, bwd)
    return op


def _rms(x, g):
    return x * lax.rsqrt(jnp.mean(x * x, axis=-1, keepdims=True) + EPS) * g


def _silu(x):
    return x * jax.nn.sigmoid(x)


def _bdot(a, b):
    return jnp.dot(a.astype(BF16), b.astype(BF16), preferred_element_type=F32)


def norm_f(rows, params, consts):
    return (_rms(rows[0], params[0]),)


def addnorm_f(rows, params, consts):
    x = rows[0] + rows[1]
    return x, _rms(x, params[0])


EVEN_GLA_END = 1568


def even_prep_f(rows, params, consts):
    (p,) = rows
    wa2f, wa2b, baf, bab, lbf, lbb = params
    gq = p[:, 0:256]
    gk = p[:, 256:512] * (GLA_DK ** -0.5)
    gv = p[:, 512:1024]
    gr = p[:, 1024:1536]
    glaf = jax.nn.log_sigmoid(_bdot(p[:, 1536:1552], wa2f) + baf) / GLA_GATE_NORM
    glab = jax.nn.log_sigmoid(_bdot(p[:, 1552:1568], wa2b) + bab) / GLA_GATE_NORM
    o = EVEN_GLA_END
    hq = _silu(p[:, o:o + 256])

    def gate(z, lb):
        f = lb + (1.0 - lb) * jax.nn.sigmoid(z)
        return jnp.log(jnp.maximum(f, HGRN_MIN_F)), (1.0 - lb) * jax.nn.sigmoid(-z)

    hlaf, hkf = gate(p[:, o + 256:o + 512], lbf)
    hlab, hkb = gate(p[:, o + 512:o + 768], lbb)
    hv = p[:, o + 768:o + 1280]
    hg = p[:, o + 1280:o + 1792]
    return gq, gk, gv, gq, gk, gv, glaf, glab, gr, hq, hkf, hv, hq, hkb, hv, hlaf, hlab, hg


EVEN_PREP_WIDTHS = (256, 256, 512, 256, 256, 512, 256, 256, 512, 256, 256, 512, 256, 256, 512, 256, 256, 512)


def _head_rms(o, g, heads, d):
    parts = []
    for h in range(heads):
        seg = o[:, h * d:(h + 1) * d]
        parts.append(seg * lax.rsqrt(jnp.mean(seg * seg, axis=-1, keepdims=True) + EPS))
    return jnp.concatenate(parts, axis=1) * g


def even_post_f(rows, params, consts):
    of, ob, hof, hob, gr, hg = rows
    a = _head_rms(of + ob, params[0], GLA_HEADS, GLA_DV) * _silu(gr)
    b = _head_rms(hof + hob, params[1], HGRN_HEADS, HGRN_DV) * _silu(hg)
    return (jnp.concatenate([a, b], axis=1),)


@jax.custom_vjp
def _roll_half(x):
    return pltpu.roll(x, RET_DK // 2, 1)


_roll_half.defvjp(lambda x: (_roll_half(x), None), lambda _, g: (_roll_half(g),))


def odd_prep_f(rows, params, consts):
    (p,) = rows
    cosf, sinf = consts

    def rot(t):
        parts = []
        for h in range(RET_HEADS):
            th = t[:, h * RET_DK:(h + 1) * RET_DK]
            parts.append(th * cosf + _roll_half(th) * sinf)
        return jnp.concatenate(parts, axis=1)

    rq = rot(p[:, 0:512])
    rk = rot(p[:, 512:1024]) * (RET_DK ** -0.5)
    rv = p[:, 1024:1792]
    return rq, rk, rv, rq, rk, rv, p[:, 1792:2560], p[:, 2560:2816]


ODD_PREP_WIDTHS = (512, 512, 768, 512, 512, 768, 768, 256)


def ret_post_f(rows, params, consts):
    of, ob, rg = rows
    o = of + ob
    parts = []
    for h in range(RET_HEADS):
        seg = o[:, h * RET_DV:(h + 1) * RET_DV]
        c = seg - jnp.mean(seg, axis=-1, keepdims=True)
        parts.append(c * lax.rsqrt(jnp.mean(c * c, axis=-1, keepdims=True) + EPS))
    return (jnp.concatenate(parts, axis=1) * params[0] * _silu(rg),)


def s5_post_f(rows, params, consts):
    h0r, h0i, h1r, h1i, u = rows
    c_re, c_im, d_skip, glu_w, glu_b = params
    hr = h0r + h1r
    hi = h0i + h1i
    y = _bdot(hr, c_re) - _bdot(hi, c_im) + d_skip * u
    g = jax.nn.gelu(y)
    return (g * jax.nn.sigmoid(_bdot(g, glu_w) + glu_b),)


def loss_head(x, r, g, target, tile=ROW_TILE):
    T, D = x.shape

    def body(x_ref, r_ref, g_ref, t_ref, loss_ref, dx_ref, dg_ref):
        t = t_ref[...]

        def lf(xv, gv):
            e = _rms(xv, gv) - t
            row = jnp.sum(e * e, axis=-1, keepdims=True) * (0.5 / D)
            return jnp.sum(row, axis=0, keepdims=True)

        l, vjp = jax.vjp(lf, x_ref[...] + r_ref[...], g_ref[...])
        dx, dg = vjp(jnp.ones((1, 1), F32))
        dx_ref[...] = dx

        @pl.when(pl.program_id(0) == 0)
        def _():
            loss_ref[...] = jnp.zeros_like(loss_ref)
            dg_ref[...] = jnp.zeros_like(dg_ref)

        loss_ref[...] += jnp.broadcast_to(l, loss_ref.shape)
        dg_ref[...] += dg

    row = pl.BlockSpec((tile, D), lambda i: (i, 0))
    vec = pl.BlockSpec((1, D), lambda i: (0, 0))
    return pl.pallas_call(
        body, name="loss_head", grid=(T // tile,),
        in_specs=[row, row, vec, row],
        out_specs=[pl.BlockSpec((1, LANES), lambda i: (0, 0)), row, vec],
        out_shape=[jax.ShapeDtypeStruct((1, LANES), F32), jax.ShapeDtypeStruct((T, D), F32), jax.ShapeDtypeStruct((1, D), F32)],
        compiler_params=_params("arbitrary"),
    )(x, r, g, target)


SUBLANES = 8


def _halo_specs(width, tile, T):
    n8 = tile // SUBLANES
    last = T // SUBLANES - 1
    return [pl.BlockSpec((tile, width), lambda i: (i, 0)),
            pl.BlockSpec((SUBLANES, width), lambda i: (jnp.maximum(i * n8 - 1, 0), 0)),
            pl.BlockSpec((SUBLANES, width), lambda i: (jnp.minimum((i + 1) * n8, last), 0))]


def _shift_rows(x, prev_row, next_row, tile):
    row = lax.broadcasted_iota(jnp.int32, (tile, 1), 0)
    down = jnp.where(row == 0, prev_row, pltpu.roll(x, 1, 0))
    up = jnp.where(row == tile - 1, next_row, pltpu.roll(x, tile - 1, 0))
    return down, up


def _conv_fwd(name, u, cw, cb, S, tile):
    T, F2 = u.shape
    F = F2 // 2
    per_seq = S // tile

    def body(u_ref, up_ref, un_ref, cw_ref, cb_ref, g_ref):
        pos = pl.program_id(0) % per_seq
        uv = u_ref[...]
        prev_row = jnp.where(pos == 0, 0.0, up_ref[SUBLANES - 1:SUBLANES, :])
        next_row = jnp.where(pos == per_seq - 1, 0.0, un_ref[0:1, :])
        down, up = _shift_rows(uv, prev_row, next_row, tile)
        c = _conv_taps(down, uv, up, cw_ref, cb_ref)
        g_ref[...] = (_silu(c[:, :F]) * c[:, F:]).astype(BF16)

    return pl.pallas_call(
        body, name=name + "_fwd", grid=(T // tile,),
        in_specs=_halo_specs(F2, tile, T) + [pl.BlockSpec((3, F2), lambda i: (0, 0)), pl.BlockSpec((1, F2), lambda i: (0, 0))],
        out_specs=pl.BlockSpec((tile, F), lambda i: (i, 0)),
        out_shape=jax.ShapeDtypeStruct((T, F), BF16),
        compiler_params=_params("parallel"),
    )(u, u, u, cw, cb)


def _conv_taps(down, mid, up, cw_ref, cb_ref):
    c = cb_ref[...] + down * cw_ref[0:1, :]
    c = c + mid * cw_ref[1:2, :]
    return c + up * cw_ref[2:3, :]


def _conv_bwd(name, u, dg, cw, cb, S, tile):
    T, F2 = u.shape
    F = F2 // 2
    per_seq = S // tile

    def dact(cv, dgv):
        a, v = cv[:, :F], cv[:, F:]
        sg = jax.nn.sigmoid(a)
        return jnp.concatenate([dgv * v * (sg * (1.0 + a * (1.0 - sg))), dgv * (a * sg)], axis=1)

    def body(u_ref, up_ref, un_ref, g_ref, gp_ref, gn_ref, cw_ref, cb_ref, du_ref, dw0_ref, dw1_ref, dw2_ref, db_ref):
        i = pl.program_id(0)
        pos = i % per_seq
        first, last = pos == 0, pos == per_seq - 1
        lo, hi = slice(SUBLANES - 1, SUBLANES), slice(0, 1)
        uv = u_ref[...]
        u_m1, u_p1 = jnp.where(first, 0.0, up_ref[lo, :]), jnp.where(last, 0.0, un_ref[hi, :])
        u_dn, u_up = _shift_rows(uv, u_m1, u_p1, tile)
        dc = dact(_conv_taps(u_dn, uv, u_up, cw_ref, cb_ref), g_ref[...])
        c_m1 = _conv_taps(up_ref[SUBLANES - 2:SUBLANES - 1, :], u_m1, u_ref[0:1, :], cw_ref, cb_ref)
        c_p1 = _conv_taps(u_ref[tile - 1:tile, :], u_p1, un_ref[1:2, :], cw_ref, cb_ref)
        dc_prev = jnp.where(first, 0.0, dact(c_m1, gp_ref[lo, :]))
        dc_next = jnp.where(last, 0.0, dact(c_p1, gn_ref[hi, :]))
        dc_dn, dc_up = _shift_rows(dc, dc_prev, dc_next, tile)
        du = dc_up * cw_ref[0:1, :]
        du = du + dc * cw_ref[1:2, :]
        du_ref[...] = (du + dc_dn * cw_ref[2:3, :]).astype(BF16)

        @pl.when(i == 0)
        def _():
            for ref in (dw0_ref, dw1_ref, dw2_ref, db_ref):
                ref[...] = jnp.zeros_like(ref)

        dw0_ref[...] += jnp.sum(dc * u_dn, axis=0, keepdims=True)
        dw1_ref[...] += jnp.sum(dc * uv, axis=0, keepdims=True)
        dw2_ref[...] += jnp.sum(dc * u_up, axis=0, keepdims=True)
        db_ref[...] += jnp.sum(dc, axis=0, keepdims=True)

    vec = pl.BlockSpec((1, F2), lambda i: (0, 0))
    du, dw0, dw1, dw2, db = pl.pallas_call(
        body, name=name + "_bwd", grid=(T // tile,),
        in_specs=_halo_specs(F2, tile, T) + _halo_specs(F, tile, T) + [pl.BlockSpec((3, F2), lambda i: (0, 0)), vec],
        out_specs=[pl.BlockSpec((tile, F2), lambda i: (i, 0)), vec, vec, vec, vec],
        out_shape=[jax.ShapeDtypeStruct((T, F2), BF16)] + [jax.ShapeDtypeStruct((1, F2), F32)] * 4,
        compiler_params=_params("arbitrary"),
    )(u, u, u, dg, dg, dg, cw, cb)
    return du, jnp.concatenate([dw0, dw1, dw2], axis=0), db


def make_ffn(name, S):
    def run(x, wup16, wdn16, cw, cb):
        u = _mm_nn(name + "_up", x, wup16)
        g16 = _conv_fwd(name + "_conv", u, cw, cb, S, ROW_TILE)
        return u, g16, _mm_nn(name + "_down", g16, wdn16)

    @jax.custom_vjp
    def op(x, wup16, cu32, cu16, wdn16, cd32, cd16, cw, cb):
        return run(x, wup16, wdn16, cw, cb)[2], _tick()

    def fwd(x, wup16, cu32, cu16, wdn16, cd32, cd16, cw, cb):
        u, g16, out = run(x, wup16, wdn16, cw, cb)
        return (out, _tick()), (x, wup16, wdn16, u, g16, cw, cb)

    def bwd(res, g):
        x, wup16, wdn16, u, g16, cw, cb = res
        dout, after = g
        dg = _mm_nt(name + "_down_dx", dout, wdn16, after)
        dd32, dd16 = _weight_grads(name + "_down_dw", g16, dout, False)
        du16, dcw, dcb = _conv_bwd(name + "_conv", u, dg, cw, cb, S, ROW_TILE)
        du32, du16w = _weight_grads(name + "_up_dw", x, du16, True)
        return (_mm_nt(name + "_up_dx", du16, wup16), jnp.zeros_like(wup16), du32, du16w, jnp.zeros_like(wdn16), dd32, dd16, dcw, dcb)

    op.defvjp(fwd, bwd)
    return op


def _dot_nt(a, b):
    return lax.dot_general(a.astype(BF16), b.astype(BF16), (((1,), (1,)), ((), ())), preferred_element_type=F32)


def _dot_tn(a, b):
    return lax.dot_general(a.astype(BF16), b.astype(BF16), (((0,), (0,)), ((), ())), preferred_element_type=F32)


def _chunk_decays(la, direction, C, width, dk, const_lg):
    row = lax.broadcasted_iota(jnp.int32, (C, C), 0)
    col = lax.broadcasted_iota(jnp.int32, (C, C), 1)
    keep = (row >= col) if direction == 0 else (row <= col)
    ridx = lax.broadcasted_iota(jnp.int32, (C, 1), 0)
    if const_lg is None:
        cum = jnp.dot(keep.astype(F32), la, precision=HIGHEST, preferred_element_type=F32)
    else:
        lane_head = lax.broadcasted_iota(jnp.int32, (1, width), 1) // dk
        lg = jnp.zeros((1, width), F32)
        for h, val in enumerate(const_lg):
            lg = jnp.where(lane_head == h, val, lg)
        steps = (ridx + 1) if direction == 0 else (C - ridx)
        cum = steps.astype(F32) * lg
    exit_row = C - 1 if direction == 0 else 0
    mid = jnp.sum(jnp.where(ridx == C // 2, cum, 0.0), axis=0, keepdims=True)
    last = jnp.sum(jnp.where(ridx == exit_row, cum, 0.0), axis=0, keepdims=True)
    return keep, ridx == exit_row, cum, mid, last


def _scan_fwd(name, qkv, laf, lab, H, dk, dv, S, C, const_lg):
    qf, kf, vf, qb, kb, vb = qkv
    T = qf.shape[0]
    B, nc = T // S, S // C
    Wk, Wv = H * dk, H * dv
    learn = const_lg is None

    def body(*refs):
        @pl.when(pl.program_id(0) == 0)
        def _():
            refs[-1][...] = jnp.zeros_like(refs[-1])

        for b in range(B):
            one_sequence(*[r.at[b] for r in refs])

    def one_sequence(*refs):
        if learn:
            qf_r, qb_r, kf_r, kb_r, vf_r, vb_r, laf_r, lab_r, of_r, ob_r, sf_r, sb_r, st = refs
            las = (laf_r[...], lab_r[...])
        else:
            qf_r, qb_r, kf_r, kb_r, vf_r, vb_r, of_r, ob_r, sf_r, sb_r, st = refs
            las = (None, None)

        for d, (q_r, k_r, v_r, o_r, s_r) in enumerate(((qf_r, kf_r, vf_r, of_r, sf_r), (qb_r, kb_r, vb_r, ob_r, sb_r))):
            keep, _, cum, mid, last = _chunk_decays(las[d], d, C, Wk, dk, None if learn else const_lg[d])
            qe = q_r[...] * jnp.exp(cum - mid)
            ke = k_r[...] * jnp.exp(mid - cum)
            q_in = qe * jnp.exp(mid)
            k_out = ke * jnp.exp(last - mid)
            e_last = jnp.exp(last)
            vv = v_r[...]
            for h in range(H):
                ks, vs = slice(h * dk, (h + 1) * dk), slice(h * dv, (h + 1) * dv)
                a = jnp.where(keep, _dot_nt(qe[:, ks], ke[:, ks]), 0.0)
                state = st[d, h]
                o_r[:, vs] = _bdot(a, vv[:, vs]) + _dot_nt(q_in[:, ks], state)
                s_r[h * dv:(h + 1) * dv, :] = state
                st[d, h] = state * e_last[:, ks] + _dot_tn(vv[:, vs], k_out[:, ks])

    fpos = lambda c: c
    bpos = lambda c: nc - 1 - c
    kspec = lambda pos: pl.BlockSpec((B, C, Wk), lambda c: (0, pos(c), 0))
    vspec = lambda pos: pl.BlockSpec((B, C, Wv), lambda c: (0, pos(c), 0))
    sspec = lambda pos: pl.BlockSpec((B, None, Wv, dk), lambda c: (0, pos(c), 0, 0))
    seq = lambda t: t.reshape(B, S, t.shape[1])
    ins = [seq(t) for t in [qf, qb, kf, kb, vf, vb] + ([laf, lab] if learn else [])]
    in_specs = [kspec(fpos), kspec(bpos), kspec(fpos), kspec(bpos), vspec(fpos), vspec(bpos)] + ([kspec(fpos), kspec(bpos)] if learn else [])
    of, ob, sf, sb = pl.pallas_call(
        body, name=name + "_fwd", grid=(nc,), in_specs=in_specs,
        out_specs=[vspec(fpos), vspec(bpos), sspec(fpos), sspec(bpos)],
        out_shape=[jax.ShapeDtypeStruct((B, S, Wv), F32)] * 2 + [jax.ShapeDtypeStruct((B, nc, Wv, dk), F32)] * 2,
        scratch_shapes=[pltpu.VMEM((B, 2, H, dv, dk), F32)],
        compiler_params=_params("arbitrary"),
    )(*ins)
    return of.reshape(T, Wv), ob.reshape(T, Wv), sf, sb


def _scan_bwd(name, qkv, laf, lab, sf, sb, dof, dob, H, dk, dv, S, C, const_lg):
    qf, kf, vf, qb, kb, vb = qkv
    T = qf.shape[0]
    B, nc = T // S, S // C
    Wk, Wv = H * dk, H * dv
    learn = const_lg is None

    def body(*refs):
        @pl.when(pl.program_id(0) == 0)
        def _():
            refs[-1][...] = jnp.zeros_like(refs[-1])

        for b in range(B):
            one_sequence(*[r.at[b] for r in refs])

    def one_sequence(*refs):
        if learn:
            (qf_r, qb_r, kf_r, kb_r, vf_r, vb_r, laf_r, lab_r, sf_r, sb_r, dof_r, dob_r,
             dqf_r, dqb_r, dkf_r, dkb_r, dvf_r, dvb_r, dlaf_r, dlab_r, dst) = refs
            las, dlas = (laf_r[...], lab_r[...]), (dlaf_r, dlab_r)
        else:
            (qf_r, qb_r, kf_r, kb_r, vf_r, vb_r, sf_r, sb_r, dof_r, dob_r,
             dqf_r, dqb_r, dkf_r, dkb_r, dvf_r, dvb_r, dst) = refs
            las, dlas = (None, None), (None, None)

        groups = ((qf_r, kf_r, vf_r, sf_r, dof_r, dqf_r, dkf_r, dvf_r), (qb_r, kb_r, vb_r, sb_r, dob_r, dqb_r, dkb_r, dvb_r))
        for d, (q_r, k_r, v_r, s_r, do_r, dq_r, dk_r, dv_r) in enumerate(groups):
            keep, is_exit, cum, mid, last = _chunk_decays(las[d], d, C, Wk, dk, None if learn else const_lg[d])
            eq, ek = jnp.exp(cum - mid), jnp.exp(mid - cum)
            e_in, e_out, e_last = jnp.exp(mid), jnp.exp(last - mid), jnp.exp(last)
            qe, ke = q_r[...] * eq, k_r[...] * ek
            q_in, k_out = qe * e_in, ke * e_out
            vv, do = v_r[...], do_r[...]
            dqe_parts, dke_parts, dlast_parts = [], [], []
            for h in range(H):
                ks, vs = slice(h * dk, (h + 1) * dk), slice(h * dv, (h + 1) * dv)
                a = jnp.where(keep, _dot_nt(qe[:, ks], ke[:, ks]), 0.0)
                dp = jnp.where(keep, _dot_nt(do[:, vs], vv[:, vs]), 0.0)
                s_prev = s_r[h * dv:(h + 1) * dv, :]
                ds = dst[d, h]
                dk_out = _bdot(vv[:, vs], ds)
                dqe_parts.append(_bdot(dp, ke[:, ks]) + _bdot(do[:, vs], s_prev) * e_in[:, ks])
                dke_parts.append(_dot_tn(dp, qe[:, ks]) + dk_out * e_out[:, ks])
                dv_r[:, vs] = _dot_tn(a, do[:, vs]) + _dot_nt(k_out[:, ks], ds)
                if learn:
                    dlast_parts.append(jnp.sum(dk_out * k_out[:, ks], axis=0, keepdims=True)
                                       + jnp.sum(ds * s_prev, axis=0, keepdims=True) * e_last[:, ks])
                dst[d, h] = ds * e_last[:, ks] + _dot_tn(do[:, vs], q_in[:, ks])
            dqe = jnp.concatenate(dqe_parts, axis=1)
            dke = jnp.concatenate(dke_parts, axis=1)
            dq_r[...] = dqe * eq
            dk_r[...] = dke * ek
            if learn:
                dcum = dqe * qe - dke * ke + jnp.where(is_exit, jnp.concatenate(dlast_parts, axis=1), 0.0)
                dlas[d][...] = lax.dot_general(keep.astype(F32), dcum, (((0,), (0,)), ((), ())), precision=HIGHEST,
                                               preferred_element_type=F32)

    fpos = lambda c: nc - 1 - c
    bpos = lambda c: c
    kspec = lambda pos: pl.BlockSpec((B, C, Wk), lambda c: (0, pos(c), 0))
    vspec = lambda pos: pl.BlockSpec((B, C, Wv), lambda c: (0, pos(c), 0))
    sspec = lambda pos: pl.BlockSpec((B, None, Wv, dk), lambda c: (0, pos(c), 0, 0))
    seq = lambda t: t.reshape(B, S, t.shape[1])
    ins = [seq(t) for t in [qf, qb, kf, kb, vf, vb] + ([laf, lab] if learn else [])] + [sf, sb, seq(dof), seq(dob)]
    in_specs = ([kspec(fpos), kspec(bpos), kspec(fpos), kspec(bpos), vspec(fpos), vspec(bpos)]
                + ([kspec(fpos), kspec(bpos)] if learn else []) + [sspec(fpos), sspec(bpos), vspec(fpos), vspec(bpos)])
    out_specs = [kspec(fpos), kspec(bpos), kspec(fpos), kspec(bpos), vspec(fpos), vspec(bpos)] + ([kspec(fpos), kspec(bpos)] if learn else [])
    out_shape = ([jax.ShapeDtypeStruct((B, S, Wk), F32)] * 4 + [jax.ShapeDtypeStruct((B, S, Wv), F32)] * 2
                 + ([jax.ShapeDtypeStruct((B, S, Wk), F32)] * 2 if learn else []))
    outs = pl.pallas_call(
        body, name=name + "_bwd", grid=(nc,), in_specs=in_specs, out_specs=out_specs, out_shape=out_shape,
        scratch_shapes=[pltpu.VMEM((B, 2, H, dv, dk), F32)],
        compiler_params=_params("arbitrary"),
    )(*ins)
    return [t.reshape(T, t.shape[2]) for t in outs]


def make_scan(name, H, dk, dv, S, C, const_lg=None):
    if const_lg is None:
        @jax.custom_vjp
        def op(qkv, laf, lab):
            return tuple(_scan_fwd(name, qkv, laf, lab, H, dk, dv, S, C, None)[:2])

        def fwd(qkv, laf, lab):
            of, ob, sf, sb = _scan_fwd(name, qkv, laf, lab, H, dk, dv, S, C, None)
            return (of, ob), (qkv, laf, lab, sf, sb)

        def bwd(res, g):
            qkv, laf, lab, sf, sb = res
            dqf, dqb, dkf, dkb, dvf, dvb, dlaf, dlab = _scan_bwd(name, qkv, laf, lab, sf, sb, g[0], g[1], H, dk, dv, S, C, None)
            return (dqf, dkf, dvf, dqb, dkb, dvb), dlaf, dlab
    else:
        @jax.custom_vjp
        def op(qkv):
            return tuple(_scan_fwd(name, qkv, None, None, H, dk, dv, S, C, const_lg)[:2])

        def fwd(qkv):
            of, ob, sf, sb = _scan_fwd(name, qkv, None, None, H, dk, dv, S, C, const_lg)
            return (of, ob), (qkv, sf, sb)

        def bwd(res, g):
            qkv, sf, sb = res
            dqf, dqb, dkf, dkb, dvf, dvb = _scan_bwd(name, qkv, None, None, sf, sb, g[0], g[1], H, dk, dv, S, C, const_lg)
            return ((dqf, dkf, dvf, dqb, dkb, dvb),)

    op.defvjp(fwd, bwd)
    return op


def _reorder_call(name, t, S, to_segments):
    T, w = t.shape
    n_it = S // S5_SEGMENTS

    def body(x_ref, o_ref):
        def step(i, carry):
            packed = pl.ds(pl.multiple_of(i * S5_SEGMENTS, S5_SEGMENTS), S5_SEGMENTS)
            spread = pl.ds(i, S5_SEGMENTS, stride=n_it)
            if to_segments:
                o_ref[packed, :] = x_ref[spread, :]
            else:
                o_ref[spread, :] = x_ref[packed, :]
            return carry

        lax.fori_loop(0, n_it, step, 0, unroll=8)

    blk = pl.BlockSpec((S, LANES), lambda b, j: (b, j))
    return pl.pallas_call(body, name=name, grid=(T // S, w // LANES), in_specs=[blk], out_specs=blk,
                          out_shape=jax.ShapeDtypeStruct(t.shape, t.dtype), compiler_params=_params("parallel", "parallel"))(t)


def make_join(name, S, wa, wb):
    n_it = S // S5_SEGMENTS

    def call(tag, forward, wa, wb, arrs):
        T = arrs[0].shape[0]
        na, nb = wa // LANES, wb // LANES

        def body(*refs):
            j = pl.program_id(1)
            a_ref, b_ref, y_ref = (refs[0], refs[1], refs[2]) if forward else (refs[1], refs[2], refs[0])

            @pl.when(j < na)
            def _():
                if forward:
                    y_ref[...] = a_ref[...]
                else:
                    a_ref[...] = y_ref[...]

            @pl.when(j >= na)
            def _():
                def step(i, carry):
                    packed = pl.ds(pl.multiple_of(i * S5_SEGMENTS, S5_SEGMENTS), S5_SEGMENTS)
                    spread = pl.ds(i, S5_SEGMENTS, stride=n_it)
                    if forward:
                        y_ref[spread, :] = b_ref[packed, :]
                    else:
                        b_ref[packed, :] = y_ref[spread, :]
                    return carry

                lax.fori_loop(0, n_it, step, 0, unroll=8)

        a_spec = pl.BlockSpec((S, LANES), lambda b, j: (b, jnp.minimum(j, na - 1)))
        b_spec = pl.BlockSpec((S, LANES), lambda b, j: (b, jnp.maximum(j - na, 0)))
        y_spec = pl.BlockSpec((S, LANES), lambda b, j: (b, j))
        shapes = [jax.ShapeDtypeStruct((T, wa), F32), jax.ShapeDtypeStruct((T, wb), F32), jax.ShapeDtypeStruct((T, wa + wb), F32)]
        return pl.pallas_call(
            body, name=name + tag, grid=(T // S, na + nb),
            in_specs=[a_spec, b_spec] if forward else [y_spec], out_specs=y_spec if forward else [a_spec, b_spec],
            out_shape=shapes[2] if forward else shapes[:2], compiler_params=_params("parallel", "arbitrary"),
        )(*arrs)

    @jax.custom_vjp
    def op(a, b):
        return call("_fwd", True, wa, wb, (a, b))

    op.defvjp(lambda a, b: (call("_fwd", True, wa, wb, (a, b)), None), lambda _, dy: tuple(call("_bwd", False, wa, wb, (dy,))))
    return op


def make_reorder(name, S, to_segments):
    @jax.custom_vjp
    def op(t):
        return _reorder_call(name, t, S, to_segments)

    op.defvjp(lambda t: (_reorder_call(name, t, S, to_segments), None),
              lambda _, g: (_reorder_call(name + "_bwd", g, S, not to_segments),))
    return op


def _s5_scan_call(name, Xs, A, S, dirs, prev=None):
    with_p = prev is not None
    T, N = Xs[0].shape
    B, nl = T // S, N // LANES
    n_it = S // S5_SEGMENTS
    assert n_it & (n_it - 1) == 0

    def cmul(ar, ai, br, bi):
        return ar * br - ai * bi, ar * bi + ai * br

    def body(*refs):
        x, a_ref = refs[0:4], refs[4]
        if with_p:
            h_prev, x_prev, h, p_ref = refs[5:9], refs[9:13], refs[13:17], refs[17]
        else:
            h = refs[5:9]
        seg = lax.broadcasted_iota(jnp.int32, (S5_SEGMENTS, 1), 0)
        zero = jnp.zeros((S5_SEGMENTS, LANES), F32)
        a = [(jnp.broadcast_to(a_ref[2 * k:2 * k + 1, :], (S5_SEGMENTS, LANES)),
              jnp.broadcast_to(a_ref[2 * k + 1:2 * k + 2, :], (S5_SEGMENTS, LANES))) for k in range(2)]

        def rows_of(k, i):
            return pl.ds(pl.multiple_of(((n_it - 1 - i) if dirs[k] == 1 else i) * S5_SEGMENTS, S5_SEGMENTS), S5_SEGMENTS)

        def local(i, carry):
            out = []
            for k, (sr, si) in enumerate(carry):
                rows = rows_of(k, i)
                pr, pi = cmul(*a[k], sr, si)
                sr, si = pr + x[2 * k][rows, :], pi + x[2 * k + 1][rows, :]
                h[2 * k][rows, :] = sr
                h[2 * k + 1][rows, :] = si
                out.append((sr, si))
            return tuple(out)

        ends = lax.fori_loop(0, n_it, local, ((zero, zero), (zero, zero)), unroll=8)
        inherit = []
        for k, (er, ei) in enumerate(ends):
            back = dirs[k] == 1
            pr, pi = a[k]
            for _ in range(n_it.bit_length() - 1):
                pr, pi = cmul(pr, pi, pr, pi)
            shift = (S5_SEGMENTS - 1) if back else 1
            tr, ti = er, ei
            for r in (range(S5_SEGMENTS - 2, -1, -1) if back else range(1, S5_SEGMENTS)):
                nr, ni = cmul(pr, pi, pltpu.roll(tr, shift, 0), pltpu.roll(ti, shift, 0))
                tr = jnp.where(seg == r, er + nr, tr)
                ti = jnp.where(seg == r, ei + ni, ti)
            edge = (S5_SEGMENTS - 1) if back else 0
            inherit.append((jnp.where(seg == edge, 0.0, pltpu.roll(tr, shift, 0)), jnp.where(seg == edge, 0.0, pltpu.roll(ti, shift, 0))))

        def fix(i, carry):
            powers, sums = carry
            new_powers, new_sums = [], []
            for k in range(2):
                rows = rows_of(k, i)
                fr, fi = cmul(*powers[k], *inherit[k])
                sr, si = h[2 * k][rows, :] + fr, h[2 * k + 1][rows, :] + fi
                h[2 * k][rows, :] = sr
                h[2 * k + 1][rows, :] = si
                new_powers.append(cmul(*powers[k], *a[k]))
                if with_p:
                    ur = h_prev[2 * k][rows, :] - x_prev[2 * k][rows, :]
                    ui = h_prev[2 * k + 1][rows, :] - x_prev[2 * k + 1][rows, :]
                    new_sums.append((sums[k][0] + sr * ur + si * ui, sums[k][1] + si * ur - sr * ui))
            return tuple(new_powers), tuple(new_sums)

        _, sums = lax.fori_loop(0, n_it, fix, ((a[0], a[1]), ((zero, zero), (zero, zero)) if with_p else ()), unroll=8)
        if with_p:
            for k in range(2):
                p_ref[2 * k:2 * k + 1, :] = jnp.sum(sums[k][0], axis=0, keepdims=True)
                p_ref[2 * k + 1:2 * k + 2, :] = jnp.sum(sums[k][1], axis=0, keepdims=True)

    col = pl.BlockSpec((S, LANES), lambda b, j: (b, j))
    outs = pl.pallas_call(
        body, name=name, grid=(B, nl),
        in_specs=[col] * 4 + [pl.BlockSpec((4, LANES), lambda b, j: (0, j))] + ([col] * 8 if with_p else []),
        out_specs=[col] * 4 + ([pl.BlockSpec((None, 4, LANES), lambda b, j: (b, 0, j))] if with_p else []),
        out_shape=[jax.ShapeDtypeStruct((T, N), F32)] * 4 + ([jax.ShapeDtypeStruct((B, 4, N), F32)] if with_p else []),
        compiler_params=_params("parallel", "parallel"),
    )(*Xs, A, *(prev[0] + prev[1] if with_p else ()))
    return (tuple(outs[:4]), outs[4]) if with_p else tuple(outs)


def make_s5_scan(name, S):
    @jax.custom_vjp
    def op(X, A):
        return _s5_scan_call(name + "_fwd", X, A, S, (0, 1))

    def fwd(X, A):
        H = _s5_scan_call(name + "_fwd", X, A, S, (0, 1))
        return H, (X, A, H)

    def bwd(res, G):
        X, A, H = res
        conj = A * jnp.array([[1.0], [-1.0], [1.0], [-1.0]], F32)
        lam, P = _s5_scan_call(name + "_bwd", tuple(G), conj, S, (1, 0), prev=(tuple(H), tuple(X)))
        P = jnp.sum(P, axis=0)
        ar, ai = A[0::2], A[1::2]
        pr, pi = P[0::2], P[1::2]
        den = ar * ar + ai * ai
        dar, dai = (pr * ar - pi * ai) / den, (pr * ai + pi * ar) / den
        return lam, jnp.stack([dar[0], dai[0], dar[1], dai[1]], axis=0)

    op.defvjp(fwd, bwd)
    return op


ANY = pl.BlockSpec(memory_space=pl.ANY)


def _place():
    x, y, c = lax.axis_index("x"), lax.axis_index("y"), lax.axis_index("c")
    return x, y, c, [(1 - x, y), (x, 1 - y), (1 - x, 1 - y)]


def all_gather(name, arrs):
    n = len(arrs)

    def body(*refs):
        ins, outs = refs[:n], refs[n:2 * n]
        send, recv, lsem = refs[2 * n:]
        x, y, c, chips = _place()
        me, sibling = (x, y, c), (x, y, 1 - c)

        def copy(a, k, block, to, src=None):
            slot = outs[a].at[4 * block[0] + 2 * block[1] + block[2]]
            return pltpu.make_async_remote_copy(src_ref=slot if src is None else src, dst_ref=slot, send_sem=send.at[a, k],
                                                recv_sem=recv.at[a, k], device_id=to, device_id_type=MESH)

        mine = [pltpu.make_async_copy(ins[a], outs[a].at[4 * x + 2 * y + c], lsem.at[a]) for a in range(n)]
        first = []
        for a in range(n):
            mine[a].start()
            first.append(copy(a, 0, me, sibling, src=ins[a]))
            first += [copy(a, 1 + j, me, (*chip, c), src=ins[a]) for j, chip in enumerate(chips)]
        for cp in first:
            cp.start()
        passed = []
        for a in range(n):
            for j, chip in enumerate(chips):
                copy(a, 1 + j, (*chip, c), me).wait_recv()
                fwd = copy(a, 4 + j, (*chip, c), sibling)
                fwd.start()
                passed.append(fwd)
        for a in range(n):
            copy(a, 0, sibling, me).wait_recv()
            for j, chip in enumerate(chips):
                copy(a, 4 + j, (*chip, 1 - c), me).wait_recv()
        for cp in first + passed:
            cp.wait_send()
        for cp in mine:
            cp.wait()

    return pl.pallas_call(
        body, name=name, in_specs=[ANY] * n, out_specs=[ANY] * n,
        out_shape=[jax.ShapeDtypeStruct((N_DEV,) + a.shape, a.dtype) for a in arrs],
        scratch_shapes=[pltpu.SemaphoreType.DMA((n, 7)), pltpu.SemaphoreType.DMA((n, 7)), pltpu.SemaphoreType.DMA((n,))],
    )(*arrs)


HBM = pl.BlockSpec(memory_space=pltpu.HBM)
SEM = pl.BlockSpec(memory_space=pltpu.SEMAPHORE)
EFFECT = pltpu.SideEffectType.DATAFLOW_SIDE_EFFECTING
GATHER_PEERS = (1, 2, 4, 6)
OTHER_CHIPS = (2, 4, 6)
COPIES_PER_ARRAY = {"scatter": N_DEV - 1, "gather": len(GATHER_PEERS), "forward": len(OTHER_CHIPS)}


def _split_plan(mode, srcs, lands, send, recv):
    x, y, c = lax.axis_index("x"), lax.axis_index("y"), lax.axis_index("c")

    def dev(k):
        return (1 - x if k & 4 else x), (1 - y if k & 2 else y), (1 - c if k & 1 else c)

    def idx(d):
        return 4 * d[0] + 2 * d[1] + d[2]

    me = idx((x, y, c))
    plan = []
    for a, land in enumerate(lands):
        if mode == "scatter":
            legs = [(srcs[a].at[idx(dev(k))], land.at[me], land.at[idx(dev(k))], dev(k)) for k in range(1, N_DEV)]
        elif mode == "gather":
            legs = [(srcs[a], land.at[me], land.at[idx(dev(k))], dev(k)) for k in GATHER_PEERS]
        else:
            legs = [(land.at[idx(dev(j))], land.at[idx(dev(j))], land.at[idx(dev(j ^ 1))], dev(1)) for j in OTHER_CHIPS]
        for i, (src, dst, arrival, to) in enumerate(legs):
            sem = a * len(legs) + i
            pair = tuple(pltpu.make_async_remote_copy(src_ref=src, dst_ref=d, send_sem=send.at[sem], recv_sem=recv.at[sem],
                                                      device_id=to, device_id_type=MESH) for d in (dst, arrival))
            plan.append(pair)
    return plan


def split_start(name, mode, srcs, lands, after):
    if lands is None:
        lands = [lax.empty((N_DEV,) + (s.shape[1:] if mode == "scatter" else s.shape), s.dtype) for s in srcs]
    ns, nl = len(srcs), len(lands)
    nsem = COPIES_PER_ARRAY[mode] * nl

    def body(*refs):
        ins, lnd = refs[:ns], refs[ns:ns + nl]
        send, recv = refs[ns + nl + 1], refs[ns + nl + 2]
        token = refs[-1]
        for out, _ in _split_plan(mode, ins, lnd, send, recv):
            out.start()
        token[...] = jnp.zeros_like(token)

    arrs = list(srcs) + list(lands)
    return pl.pallas_call(
        body, name=name,
        out_shape=(pltpu.SemaphoreType.DMA((nsem,)), pltpu.SemaphoreType.DMA((nsem,)))
        + tuple(pltpu.HBM(t.shape, t.dtype) for t in arrs) + (jax.ShapeDtypeStruct((SUBLANES, LANES), F32),),
        in_specs=[HBM] * len(arrs) + [ANY],
        out_specs=(SEM, SEM) + (HBM,) * len(arrs) + (pl.BlockSpec(memory_space=pltpu.VMEM),),
        input_output_aliases={i: 2 + i for i in range(len(arrs))},
        compiler_params=pltpu.CompilerParams(has_side_effects=EFFECT),
    )(*[pltpu.with_memory_space_constraint(t, pltpu.HBM) for t in arrs], after)


def split_wait(name, mode, handle, after):
    send, recv = handle[0], handle[1]
    arrs = list(handle[2:-1])
    nl = len(arrs) if mode == "forward" else len(arrs) // 2
    ns = len(arrs) - nl

    def body(*refs):
        ins, lnd = refs[:ns], refs[ns:ns + nl]
        s, r = refs[ns + nl], refs[ns + nl + 1]
        for out, arrival in _split_plan(mode, ins, lnd, s, r):
            out.wait_send()
            arrival.wait_recv()

    outs = pl.pallas_call(
        body, name=name,
        out_shape=tuple(pltpu.HBM(t.shape, t.dtype) for t in arrs),
        in_specs=[HBM] * len(arrs) + [SEM, SEM, ANY], out_specs=(HBM,) * len(arrs),
        input_output_aliases={i: i for i in range(len(arrs))},
        compiler_params=pltpu.CompilerParams(has_side_effects=EFFECT),
    )(*arrs, send, recv, after)
    return list(outs[ns:])


def _row_tile(rows, cols):
    cap = max(SUBLANES, (2**18 // cols) // SUBLANES * SUBLANES)
    if rows <= cap:
        return rows
    for t in range(cap, SUBLANES - 1, -SUBLANES):
        if rows % t == 0:
            return t
    return rows


def ordered_sum(name, parts):
    n, R, C = parts.shape
    rt = _row_tile(R, C)

    def body(p_ref, o_ref):
        s = p_ref[0]
        for k in range(1, n):
            s = s + p_ref[k]
        o_ref[...] = s

    return pl.pallas_call(
        body, name=name, grid=(R // rt,),
        in_specs=[pl.BlockSpec((n, rt, C), lambda r: (0, r, 0))], out_specs=pl.BlockSpec((rt, C), lambda r: (r, 0)),
        out_shape=jax.ShapeDtypeStruct((R, C), F32), compiler_params=_params("parallel"),
    )(parts)


def _adamw_update(w, m, v, g):
    bias1 = 1.0 - ADAM_B1 ** ADAM_STEP
    bias2 = 1.0 - ADAM_B2 ** ADAM_STEP
    m_new = ADAM_B1 * m + (1.0 - ADAM_B1) * g
    v_new = ADAM_B2 * v + (1.0 - ADAM_B2) * (g * g)
    delta = -ADAM_LR * ((m_new / bias1) / (jnp.sqrt(v_new / bias2) + ADAM_EPS) + ADAM_WD * w)
    return delta, m_new, v_new


def adamw(name, w, m, v, g):
    R, C = w.shape
    rt = _row_tile(R, C)

    def body(w_ref, m_ref, v_ref, g_ref, d_ref, mo_ref, vo_ref):
        d_ref[...], mo_ref[...], vo_ref[...] = _adamw_update(w_ref[...], m_ref[...], v_ref[...], g_ref[...])

    row = pl.BlockSpec((rt, C), lambda r: (r, 0))
    return pl.pallas_call(
        body, name=name, grid=(R // rt,), in_specs=[row] * 4, out_specs=[row] * 3,
        out_shape=[jax.ShapeDtypeStruct((R, C), F32)] * 3, compiler_params=_params("parallel"),
    )(w, m, v, g)


def adamw_sharded(name, layer, w, m, v, own, land, me, prev):
    _, R, C = own.shape
    rt = _row_tile(R, C)

    def body(me_ref, w_ref, m_ref, v_ref, own_ref, land_ref, *rest):
        go_ref, d_ref, mo_ref, vo_ref = rest[-4:]
        g = own_ref[...]
        for k in range(N_DEV):
            g = g + jnp.where(me_ref[0] == k, 0.0, land_ref[k].astype(F32))
        go_ref[...] = g
        d_ref[...], mo_ref[...], vo_ref[...] = _adamw_update(w_ref[...], m_ref[...], v_ref[...], g)

    row = pl.BlockSpec((None, rt, C), lambda r, p: (layer, r, 0))
    in_specs = [row, row, row, pl.BlockSpec((None, rt, C), lambda r, p: (p[0], r, 0)), pl.BlockSpec((N_DEV, rt, C), lambda r, p: (0, r, 0))]
    ins = [me, w, m, v, own, land]
    aliases = {}
    if prev is not None:
        in_specs += [ANY] * 4
        aliases = {len(ins) + k: k for k in range(4)}
        ins += list(prev)
    return pl.pallas_call(
        body, name=name,
        grid_spec=pltpu.PrefetchScalarGridSpec(num_scalar_prefetch=1, grid=(R // rt,), in_specs=in_specs, out_specs=[row] * 4),
        out_shape=[jax.ShapeDtypeStruct(w.shape, F32)] * 4, input_output_aliases=aliases,
        compiler_params=_params("arbitrary"),
    )(*ins)


def _hgrn_lower_bounds(lb_logits):
    p = jax.nn.softmax(lb_logits, axis=1)
    return jnp.cumsum(p, axis=1) - p[:, :1]


def _s5_discretise(lam_re, lam_im, log_dt, b_re, b_im):
    lr = jnp.minimum(lam_re, -1e-4)
    li = lam_im
    dt = jnp.exp(log_dt)[:, None]
    mag = jnp.exp(lr * dt)
    ar, ai = mag * jnp.cos(li * dt), mag * jnp.sin(li * dt)
    den = lr * lr + li * li
    nr = ar - 1.0
    cr = (nr * lr + ai * li) / den
    ci = (ai * lr - nr * li) / den
    bbr = cr[..., None] * b_re - ci[..., None] * b_im
    bbi = cr[..., None] * b_im + ci[..., None] * b_re
    return ar, ai, bbr, bbi


def _block_diag(t):
    G, a, b = t.shape
    eye = jnp.eye(G, dtype=F32)
    return (t[:, :, None, :] * eye[:, None, :, None]).reshape(G * a, G * b)


def _rope_tables(S):
    half = RET_DK // 2
    inv = ROPE_BASE ** (-jnp.arange(half, dtype=F32) / half)
    ang = jnp.arange(S, dtype=F32)[:, None] * inv[None, :]
    cos, sin = jnp.cos(ang), jnp.sin(ang)
    return jnp.concatenate([cos, cos], axis=1), jnp.concatenate([-sin, sin], axis=1)


def _ret_log_decays():
    f = tuple(float(np.log1p(-np.exp2(np.float32(-5.0 - h)))) for h in range(RET_HEADS))
    b = tuple(float(np.log1p(-np.exp2(np.float32(-5.5 - h)))) for h in range(RET_HEADS))
    return f, b


def assemble_weight(name, land, own, me, col_sharded):
    _, R, C = land.shape
    if col_sharded:
        tr = min(R, 256)

        def body(me_ref, land_ref, own_ref, o_ref):
            for d in range(N_DEV):
                o_ref[:, d * C:(d + 1) * C] = jnp.where(me_ref[0] == d, own_ref[...], land_ref[d])

        grid, out_shape = (R // tr,), (R, N_DEV * C)
        in_specs = [pl.BlockSpec((N_DEV, tr, C), lambda i, p: (0, i, 0)), pl.BlockSpec((tr, C), lambda i, p: (i, 0))]
        out_spec = pl.BlockSpec((tr, N_DEV * C), lambda i, p: (i, 0))
    else:
        def body(me_ref, land_ref, own_ref, o_ref):
            o_ref[...] = jnp.where(me_ref[0] == pl.program_id(0), own_ref[...], land_ref[...])

        grid, out_shape = (N_DEV,), (N_DEV * R, C)
        in_specs = [pl.BlockSpec((None, R, C), lambda d, p: (d, 0, 0)), pl.BlockSpec((R, C), lambda d, p: (0, 0))]
        out_spec = pl.BlockSpec((R, C), lambda d, p: (d, 0))
    return pl.pallas_call(
        body, name=name,
        grid_spec=pltpu.PrefetchScalarGridSpec(num_scalar_prefetch=1, grid=grid, in_specs=in_specs, out_specs=out_spec),
        out_shape=jax.ShapeDtypeStruct(out_shape, land.dtype), compiler_params=_params("parallel"),
    )(me, land, own)


def _row(t):
    return t.reshape(1, -1)


def mixer_stage(layer, S, resid, branch, w_in, w_out, carriers, small):
    j = layer // 2
    tag = f"l{layer}"
    g = _row(small["mix_norm_g"][layer])
    if branch is None:
        x = resid
        (h,) = make_rowop(tag + "_norm", norm_f, (D_MODEL,))((x,), (g,), ())
    else:
        x, h = make_rowop(tag + "_addnorm", addnorm_f, (D_MODEL, D_MODEL))((resid, branch), (g,), ())
    if layer % 2 == 0:
        lbs = _hgrn_lower_bounds(small["hgrn_lb_logits"])
        prm = (small["gla_wa2"][j, 0], small["gla_wa2"][j, 1], _row(small["gla_ba"][j, 0]), _row(small["gla_ba"][j, 1]),
               _row(lbs[0, j]), _row(lbs[1, j]))
        outs = make_proj_stage(tag + "_prep", even_prep_f, EVEN_PREP_WIDTHS)(h, w_in, carriers["in32"], carriers["in16"], prm, ())
        gla_qkv, (glaf, glab, gr), hgrn_qkv, (hlaf, hlab, hg) = outs[0:6], outs[6:9], outs[9:15], outs[15:18]
        of, ob = make_scan(tag + "_gla", GLA_HEADS, GLA_DK, GLA_DV, S, SCAN_CHUNK)(gla_qkv, glaf, glab)
        hof, hob = make_scan(tag + "_hgrn", HGRN_HEADS, HGRN_DK, HGRN_DV, S, HGRN_SCAN_CHUNK)(hgrn_qkv, hlaf, hlab)
        (y,) = make_rowop(tag + "_post", even_post_f, (D_MODEL,))(
            (of, ob, hof, hob, gr, hg), (_row(small["gla_norm_g"][j]), _row(small["hgrn_norm_g"][j])), ())
    else:
        cosf, sinf = _rope_tables(S)
        outs = make_proj_stage(tag + "_prep", odd_prep_f, ODD_PREP_WIDTHS)(h, w_in, carriers["in32"], carriers["in16"], (), (cosf, sinf))
        ret_qkv, rg, su = outs[0:6], outs[6], outs[7]
        of, ob = make_scan(tag + "_ret", RET_HEADS, RET_DK, RET_DV, S, SCAN_CHUNK, const_lg=_ret_log_decays())(ret_qkv)
        (cm,) = make_rowop(tag + "_retpost", ret_post_f, (RET_HEADS * RET_DV,))((of, ob, rg), (_row(small["ret_norm_g"][j]),), ())
        disc = [_s5_discretise(small["s5_lam_re"][j, d], small["s5_lam_im"][j, d], small["s5_log_dt"][j, d],
                               small["s5_b_re"][j], small["s5_b_im"][j]) for d in range(2)]
        a4 = jnp.stack([t.reshape(-1) for d in range(2) for t in disc[d][:2]], axis=0)
        su_p = make_reorder(tag + "_s5seg", S, True)(su)
        bd = jnp.concatenate([_block_diag(jnp.swapaxes(disc[d][2 + i], 1, 2)) for d in range(2) for i in range(2)], axis=1)
        Xs = make_mm_groups(tag + "_s5in", 4)(su_p, bd)
        Hs = make_s5_scan(tag + "_s5scan", S)(Xs, a4)
        prm = (_block_diag(jnp.swapaxes(small["s5_c_re"][j], 1, 2)), _block_diag(jnp.swapaxes(small["s5_c_im"][j], 1, 2)),
               _row(small["s5_d"][j]), small["s5_glu_w"][j], _row(small["s5_glu_b"][j]))
        (dm_p,) = make_rowop(tag + "_s5post", s5_post_f, (S5_WIDTH,))((*Hs, su_p), prm, ())
        y = make_join(tag + "_join", S, RET_HEADS * RET_DV, S5_WIDTH)(cm, dm_p)
    return (x, *make_mm(tag + "_out")(y, w_out, carriers["out32"], carriers["out16"]))


def ffn_stage(layer, S, resid, branch, w_up, w_down, carriers, small):
    tag = f"l{layer}"
    x, hf = make_rowop(tag + "_ffnnorm", addnorm_f, (D_MODEL, D_MODEL))((resid, branch), (_row(small["ffn_norm_g"][layer]),), ())
    out, tick = make_ffn(tag + "_ffn", S)(hf, w_up, carriers["up32"], carriers["up16"], w_down, carriers["down32"], carriers["down16"],
                                         small["ffn_conv_w"][layer], _row(small["ffn_conv_b"][layer]))
    return x, out, tick


BIG = {"w_in_even": 2, "w_out_even": 1, "w_in_odd": 2, "w_out_odd": 1, "ffn_w_up": 2, "ffn_w_down": 1}
SMALL_SHARDED = {"gla_wa2": 3, "gla_ba": 2, "hgrn_lb_logits": 2, "ret_norm_g": 1, "s5_d": 1, "s5_glu_w": 1, "s5_glu_b": 1,
                 "ffn_conv_w": 2}
REPLICATED = ("mix_norm_g", "ffn_norm_g", "final_norm_g", "gla_norm_g", "hgrn_norm_g", "s5_lam_re", "s5_lam_im", "s5_log_dt",
              "s5_b_re", "s5_b_im", "s5_c_re", "s5_c_im", "ffn_conv_b")
WEIGHTS = ("mix_norm_g", "ffn_norm_g", "final_norm_g", "w_in_even", "w_out_even", "gla_wa2", "gla_ba", "gla_norm_g",
           "hgrn_lb_logits", "hgrn_norm_g", "w_in_odd", "w_out_odd", "ret_norm_g", "s5_lam_re", "s5_lam_im", "s5_log_dt",
           "s5_b_re", "s5_b_im", "s5_c_re", "s5_c_im", "s5_d", "s5_glu_w", "s5_glu_b", "ffn_w_up", "ffn_conv_w", "ffn_conv_b",
           "ffn_w_down")
PACK_COLS = LANES
MIXER_SMALL = (("mix_norm_g", "hgrn_lb_logits", "gla_wa2", "gla_ba", "gla_norm_g", "hgrn_norm_g"),
               ("mix_norm_g", "ret_norm_g", "s5_lam_re", "s5_lam_im", "s5_log_dt", "s5_b_re", "s5_b_im", "s5_c_re", "s5_c_im",
                "s5_d", "s5_glu_w", "s5_glu_b"))
FFN_SMALL = ("ffn_norm_g", "ffn_conv_w", "ffn_conv_b")


def _unshard(g, axis):
    t = jnp.moveaxis(g, 0, axis)
    return t.reshape(t.shape[:axis] + (t.shape[axis] * t.shape[axis + 1],) + t.shape[axis + 2:])


def _pack_rows(shape):
    return -(-int(np.prod(shape)) // (SUBLANES * PACK_COLS)) * SUBLANES


def _pack(arrs):
    parts = []
    for a in arrs:
        rows = _pack_rows(a.shape)
        parts.append(jnp.pad(a.reshape(-1), (0, rows * PACK_COLS - a.size)).reshape(rows, PACK_COLS))
    return jnp.concatenate(parts, axis=0)


def _unpack(packed, shapes):
    lead = packed.shape[:-2]
    out, r = [], 0
    for s in shapes:
        rows, n = _pack_rows(s), int(np.prod(s))
        piece = packed[..., r:r + rows, :].reshape(lead + (rows * PACK_COLS,))
        out.append(piece[..., :n].reshape(lead + tuple(s)))
        r += rows
    return out


def kernel(x, mix_norm_g, ffn_norm_g, final_norm_g, w_in_even, w_out_even, gla_wa2, gla_ba, gla_norm_g, hgrn_lb_logits, hgrn_norm_g, w_in_odd, w_out_odd, ret_norm_g, s5_lam_re, s5_lam_im, s5_log_dt, s5_b_re, s5_b_im, s5_c_re, s5_c_im, s5_d, s5_glu_w, s5_glu_b, ffn_w_up, ffn_conv_w, ffn_conv_b, ffn_w_down, loss_target, m_mix_norm_g, m_ffn_norm_g, m_final_norm_g, m_w_in_even, m_w_out_even, m_gla_wa2, m_gla_ba, m_gla_norm_g, m_hgrn_lb_logits, m_hgrn_norm_g, m_w_in_odd, m_w_out_odd, m_ret_norm_g, m_s5_lam_re, m_s5_lam_im, m_s5_log_dt, m_s5_b_re, m_s5_b_im, m_s5_c_re, m_s5_c_im, m_s5_d, m_s5_glu_w, m_s5_glu_b, m_ffn_w_up, m_ffn_conv_w, m_ffn_conv_b, m_ffn_w_down, v_mix_norm_g, v_ffn_norm_g, v_final_norm_g, v_w_in_even, v_w_out_even, v_gla_wa2, v_gla_ba, v_gla_norm_g, v_hgrn_lb_logits, v_hgrn_norm_g, v_w_in_odd, v_w_out_odd, v_ret_norm_g, v_s5_lam_re, v_s5_lam_im, v_s5_log_dt, v_s5_b_re, v_s5_b_im, v_s5_c_re, v_s5_c_im, v_s5_d, v_s5_glu_w, v_s5_glu_b, v_ffn_w_up, v_ffn_conv_w, v_ffn_conv_b, v_ffn_w_down):
    args = locals()
    w = {n: args[n] for n in WEIGHTS}
    m = {n: args["m_" + n] for n in WEIGHTS}
    v = {n: args["v_" + n] for n in WEIGHTS}
    Bl, S, D = x.shape
    T = Bl * S
    ix, iy, ic = lax.axis_index("x"), lax.axis_index("y"), lax.axis_index("c")
    me = 4 * ix + 2 * iy + ic

    xt = x.reshape(T, D)
    me1 = me.reshape(1).astype(jnp.int32)
    stages = []
    for layer in range(DEPTH):
        j = layer // 2
        kin, kout = ("w_in_even", "w_out_even") if layer % 2 == 0 else ("w_in_odd", "w_out_odd")
        stages.append((mixer_stage, layer, ("in", "out"), ((kin, j, True), (kout, j, False)), MIXER_SMALL[layer % 2]))
        stages.append((ffn_stage, layer, ("up", "down"), (("ffn_w_up", layer, True), ("ffn_w_down", layer, False)), FFN_SMALL))

    gather, after = [], xt
    for s, (_, _, _, projs, _) in enumerate(stages):
        handle = split_start(f"gather{s}_start", "gather", [w[n][l].astype(BF16) for n, l, _ in projs], None, after)
        gather.append(handle)
        after = handle[-1]
    sm_names = list(SMALL_SHARDED)
    (sm_all8,) = all_gather("gather_small", [_pack([w[n] for n in sm_names])])
    small = {n: _unshard(t, SMALL_SHARDED[n]) for n, t in zip(sm_names, _unpack(sm_all8, [w[n].shape for n in sm_names]))}
    small.update({n: w[n] for n in REPLICATED})
    small["mix_norm_g"] = small["mix_norm_g"] + after[0, 0]

    resid, branch, pulls = xt, None, []
    second = [None] * len(stages)

    def second_level(s, after):
        lands = split_wait(f"gather{s}_wait", "gather", gather[s], after)
        second[s] = split_start(f"forward{s}_start", "forward", [], lands, after)

    for s, (fn, layer, keys, projs, sm_keys) in enumerate(stages):
        here = lax.stop_gradient(resid)
        if second[s] is None:
            second_level(s, here)
        lands = split_wait(f"forward{s}_wait", "forward", second[s], second[s][-1])
        sm = {n: small[n] for n in sm_keys}
        if 2 <= s < len(stages) - 1:
            second_level(s + 1, here)
            norm = "mix_norm_g" if fn is mixer_stage else "ffn_norm_g"
            sm[norm] = sm[norm] + second[s + 1][-1][0, 0]
        full, carriers = [], {}
        for key, land, (n, l, col) in zip(keys, lands, projs, strict=True):
            full.append(assemble_weight(f"weight{s}_{key}", land, w[n][l].astype(BF16), me1, col))
            carriers[key + "32"] = jnp.zeros(land.shape, F32)
            carriers[key + "16"] = jnp.zeros(land.shape, BF16)
        run = functools.partial(fn, layer, S)
        if branch is None:
            (resid, branch, _), pull = jax.vjp(lambda r, c, p, run=run, full=full: run(r, None, full[0], full[1], c, p), resid, carriers, sm)
        else:
            (resid, branch, _), pull = jax.vjp(lambda r, b, c, p, run=run, full=full: run(r, b, full[0], full[1], c, p), resid, branch, carriers, sm)
        pulls.append(pull)

    loss_acc, dxf, dgf = loss_head(resid, branch, small["final_norm_g"].reshape(1, D), loss_target.reshape(T, D))
    loss = lax.psum(loss_acc[0, 0], ("x", "y", "c"))
    g_small = {"final_norm_g": dgf.reshape(D)}
    d_resid, d_branch, token = dxf, dxf, _tick()
    scatter, own32 = [None] * len(stages), [None] * len(stages)
    for s in reversed(range(len(stages))):
        if s == 0:
            d_resid, dcar, dsm = pulls[s]((d_resid, d_branch, token))
        else:
            d_resid, d_branch, dcar, dsm = pulls[s]((d_resid, d_branch, token))
        for n, g in dsm.items():
            g_small[n] = g_small[n] + g if n in g_small else g
        keys = stages[s][2]
        own32[s] = [dcar[k + "32"] for k in keys]
        scatter[s] = split_start(f"grads{s}_start", "scatter", [dcar[k + "16"] for k in keys], None, d_resid)
        token = scatter[s][-1]
    dx = d_resid

    out = {}
    sm_all = sm_names + list(REPLICATED)
    (g_all,) = all_gather("gather_small_grads", [_pack([g_small[n] for n in sm_all])])
    g_sum = _unpack(ordered_sum("sum_small_grads", g_all), [g_small[n].shape for n in sm_all])
    g_loc = []
    for n, g in zip(sm_all, g_sum):
        if n in SMALL_SHARDED:
            ax = SMALL_SHARDED[n]
            size = w[n].shape[ax]
            g = lax.dynamic_slice_in_dim(g, me * size, size, axis=ax)
        g_loc.append(g)
    shapes = [w[n].shape for n in sm_all]
    res = adamw("adamw_small", _pack([w[n] for n in sm_all]), _pack([m[n] for n in sm_all]), _pack([v[n] for n in sm_all]), _pack(g_loc))
    last_small = res[0]
    res = [g_loc] + [_unpack(r, shapes) for r in res]
    for k, n in enumerate(sm_all):
        out[n] = [r[k] for r in res]

    chain, behind = {}, last_small
    for s in reversed(range(len(stages))):
        lands = split_wait(f"grads{s}_wait", "scatter", scatter[s], behind if s == 0 else dx)
        for own, land, (n, l, _) in zip(own32[s], lands, stages[s][3], strict=True):
            chain[n] = adamw_sharded(f"adamw_{n}_{l}", l, w[n], m[n], v[n], own, land, me1, chain.get(n))
            behind = chain[n][0]
    for n in BIG:
        out[n] = [t.reshape(w[n].shape) for t in chain[n]]

    grads, deltas, new_m, new_v = ([out[n][k] for n in WEIGHTS] for k in range(4))
    return (loss, dx.reshape(Bl, S, D), *grads, *deltas, *new_m, *new_v)
```

```python
import functools

import numpy as np
import jax
import jax.numpy as jnp
from jax import lax
from jax.experimental import pallas as pl
from jax.experimental.pallas import tpu as pltpu

F32 = jnp.float32
BF16 = jnp.bfloat16
HIGHEST = lax.Precision.HIGHEST
MESH = pl.DeviceIdType.MESH

D_MODEL = 1024
DEPTH = 4
GLA_HEADS, GLA_DK, GLA_DV, GLA_GATE_NORM = 4, 64, 128, 16.0
HGRN_HEADS, HGRN_DK, HGRN_DV, HGRN_MIN_F = 4, 64, 128, 1e-20
RET_HEADS, RET_DK, RET_DV = 4, 128, 192
ROPE_BASE = 10000.0
S5_WIDTH, S5_GROUPS, S5_STATE = 256, 16, 64
S5_N = S5_GROUPS * S5_STATE
EPS = 1e-6
ADAM_LR, ADAM_B1, ADAM_B2, ADAM_EPS, ADAM_WD, ADAM_STEP = 0.001, 0.9, 0.999, 1e-08, 0.01, 10

N_DEV = 8
VMEM_LIMIT_BYTES = 56 * 1024 * 1024
ROW_TILE = 256
SCAN_CHUNK = 128
HGRN_SCAN_CHUNK = 64
S5_SEGMENTS = 8
LANES = 128


def _params(*sem):
    return pltpu.CompilerParams(dimension_semantics=sem, vmem_limit_bytes=VMEM_LIMIT_BYTES)


def _divisor_tile(n, cap):
    best = None
    for t in range(LANES, min(n, cap) + 1, LANES):
        if n % t == 0:
            best = t
    return best if best is not None else n


def _mm_nn(name, x, w):
    M, K = x.shape
    N = w.shape[1]
    tn = _divisor_tile(N, 3072) if K * N * 2 > 8 * 2**20 else N
    tm = 256 if tn * 4 * 512 > 6 * 2**20 else 512
    assert M % tm == 0 and N % tn == 0

    def body(x_ref, w_ref, o_ref):
        o_ref[...] = jnp.dot(x_ref[...].astype(BF16), w_ref[...], preferred_element_type=F32)

    return pl.pallas_call(
        body, name=name, grid=(N // tn, M // tm),
        in_specs=[pl.BlockSpec((tm, K), lambda j, i: (i, 0)), pl.BlockSpec((K, tn), lambda j, i: (0, j))],
        out_specs=pl.BlockSpec((tm, tn), lambda j, i: (i, j)),
        out_shape=jax.ShapeDtypeStruct((M, N), F32),
        compiler_params=_params("parallel", "parallel"),
    )(x, w)


def _mm_nt(name, dy, w, after=None):
    M, N = dy.shape
    K = w.shape[0]
    tk = _divisor_tile(K, 1024) if K * N * 2 > 8 * 2**20 else K
    tm = 256 if N >= 4096 else 512
    assert M % tm == 0 and K % tk == 0

    def body(dy_ref, w_ref, *rest):
        rest[-1][...] = lax.dot_general(dy_ref[...].astype(BF16), w_ref[...], (((1,), (1,)), ((), ())),
                                        preferred_element_type=F32)

    extra = [] if after is None else [after]
    return pl.pallas_call(
        body, name=name, grid=(K // tk, M // tm),
        in_specs=[pl.BlockSpec((tm, N), lambda j, i: (i, 0)), pl.BlockSpec((tk, N), lambda j, i: (j, 0))]
        + [pl.BlockSpec(memory_space=pl.ANY)] * len(extra),
        out_specs=pl.BlockSpec((tm, tk), lambda j, i: (i, j)),
        out_shape=jax.ShapeDtypeStruct((M, K), F32),
        compiler_params=_params("parallel", "parallel"),
    )(dy, w, *extra)


def _tick():
    return jnp.zeros((SUBLANES, LANES), F32)


MM_TN_VMEM_BUDGET = 46 * 2**20


def _pad_lanes(n):
    return -(-n // LANES) * LANES


def _mm_tn(name, x, dy, nblk, want16):
    M, K = x.shape
    N = dy.shape[1]
    n = N // nblk
    xb, yb = x.dtype.itemsize, dy.dtype.itemsize
    best = None
    for tk in [t for t in range(LANES, K + 1, LANES) if K % t == 0]:
        for tm in (512, 256):
            out_bytes = nblk * tk * _pad_lanes(n) * (6 if want16 else 4)
            vmem = 2 * out_bytes + 2 * tm * tk * xb + 2 * tm * _pad_lanes(N) * yb
            traffic = (K // tk) * M * N * yb + M * K * xb
            if vmem <= MM_TN_VMEM_BUDGET and M % tm == 0 and (best is None or (traffic, -tm) < best[0]):
                best = ((traffic, -tm), tk, tm)
    _, tk, tm = best
    last = M // tm - 1

    def body(x_ref, dy_ref, o32_ref, *o16_ref):
        m = pl.program_id(1)

        @pl.when(m == 0)
        def _():
            o32_ref[...] = jnp.zeros_like(o32_ref)

        dyv = dy_ref[...].astype(BF16)
        if nblk == 1:
            o32_ref[0] += lax.dot_general(x_ref[...].astype(BF16), dyv, (((0,), (0,)), ((), ())), preferred_element_type=F32)
        else:
            xt = x_ref[...].astype(F32).T.astype(BF16)
            for d in range(nblk):
                o32_ref[d] += jnp.dot(xt, dyv[:, d * n:(d + 1) * n], preferred_element_type=F32)
        if want16:
            @pl.when(m == last)
            def _():
                o16_ref[0][...] = o32_ref[...].astype(BF16)

    blk = pl.BlockSpec((nblk, tk, n), lambda a, m: (0, a, 0))
    return pl.pallas_call(
        body, name=name, grid=(K // tk, M // tm),
        in_specs=[pl.BlockSpec((tm, tk), lambda a, m: (m, a)), pl.BlockSpec((tm, N), lambda a, m: (m, 0))],
        out_specs=[blk, blk] if want16 else [blk],
        out_shape=[jax.ShapeDtypeStruct((nblk, K, n), F32)] + ([jax.ShapeDtypeStruct((nblk, K, n), BF16)] if want16 else []),
        compiler_params=_params("parallel", "arbitrary"),
    )(x, dy)


def _weight_grads(name, x, dy, col_sharded):
    if col_sharded:
        return _mm_tn(name, x, dy, N_DEV, True)
    d32, d16 = _mm_tn(name, x, dy, 1, True)
    K, N = d32.shape[1:]
    return d32.reshape(N_DEV, K // N_DEV, N), d16.reshape(N_DEV, K // N_DEV, N)


def make_mm(name, col_sharded=False):
    @jax.custom_vjp
    def mm(x, w16, c32, c16):
        return _mm_nn(name + "_fwd", x, w16), _tick()

    def fwd(x, w16, c32, c16):
        return (_mm_nn(name + "_fwd", x, w16), _tick()), (x, w16)

    def bwd(res, g):
        x, w16 = res
        dy, after = g
        dx = _mm_nt(name + "_dx", dy, w16, after)
        d32, d16 = _weight_grads(name + "_dw", x, dy, col_sharded)
        return dx, jnp.zeros_like(w16), d32, d16

    mm.defvjp(fwd, bwd)
    return mm


def make_mm_groups(name, G, tm=512):
    def products(x, w16):
        M, K = x.shape
        N = w16.shape[1] // G

        def body(x_ref, w_ref, *o_refs):
            xv = x_ref[...].astype(BF16)
            for g, o_ref in enumerate(o_refs):
                o_ref[...] = jnp.dot(xv, w_ref[:, g * N:(g + 1) * N], preferred_element_type=F32)

        row = pl.BlockSpec((tm, N), lambda i: (i, 0))
        return tuple(pl.pallas_call(
            body, name=name + "_fwd", grid=(M // tm,),
            in_specs=[pl.BlockSpec((tm, K), lambda i: (i, 0)), pl.BlockSpec((K, G * N), lambda i: (0, 0))],
            out_specs=[row] * G, out_shape=[jax.ShapeDtypeStruct((M, N), F32)] * G, compiler_params=_params("parallel"),
        )(x, w16))

    def grads(x, w16, dys):
        M, K = x.shape
        N = w16.shape[1] // G

        def body(x_ref, w_ref, *refs):
            dy_refs, dx_ref, dw_ref = refs[:G], refs[G], refs[G + 1]

            @pl.when(pl.program_id(0) == 0)
            def _():
                dw_ref[...] = jnp.zeros_like(dw_ref)

            xv = x_ref[...].astype(BF16)
            dx = jnp.zeros(dx_ref.shape, F32)
            for g in range(G):
                cols = slice(g * N, (g + 1) * N)
                dyv = dy_refs[g][...].astype(BF16)
                dx = dx + lax.dot_general(dyv, w_ref[:, cols], (((1,), (1,)), ((), ())), preferred_element_type=F32)
                dw_ref[:, cols] += lax.dot_general(xv, dyv, (((0,), (0,)), ((), ())), preferred_element_type=F32)
            dx_ref[...] = dx

        row = pl.BlockSpec((tm, N), lambda i: (i, 0))
        return pl.pallas_call(
            body, name=name + "_bwd", grid=(M // tm,),
            in_specs=[pl.BlockSpec((tm, K), lambda i: (i, 0)), pl.BlockSpec((K, G * N), lambda i: (0, 0))] + [row] * G,
            out_specs=[pl.BlockSpec((tm, K), lambda i: (i, 0)), pl.BlockSpec((K, G * N), lambda i: (0, 0))],
            out_shape=[jax.ShapeDtypeStruct((M, K), F32), jax.ShapeDtypeStruct((K, G * N), F32)],
            compiler_params=_params("arbitrary"),
        )(x, w16, *dys)

    @jax.custom_vjp
    def mm(x, w):
        return products(x, w.astype(BF16))

    def fwd(x, w):
        w16 = w.astype(BF16)
        return products(x, w16), (x, w16)

    def bwd(res, dys):
        x, w16 = res
        return tuple(grads(x, w16, tuple(dys)))

    mm.defvjp(fwd, bwd)
    return mm


def _row_specs(rows, params, consts, tile):
    specs = [pl.BlockSpec((tile, r.shape[1]), lambda i: (i, 0)) for r in rows]
    specs += [pl.BlockSpec(p.shape, lambda i: (0, 0)) for p in params]
    specs += [pl.BlockSpec((tile, c.shape[1]), lambda i, n=c.shape[0] // tile: (i % n, 0)) for c in consts]
    return specs


def _row_fwd(name, f, out_widths, tile, rows, params, consts):
    T = rows[0].shape[0]
    nr, npar, ncon = len(rows), len(params), len(consts)

    def body(*refs):
        r = tuple(x[...] for x in refs[:nr])
        p = tuple(x[...] for x in refs[nr:nr + npar])
        c = tuple(x[...] for x in refs[nr + npar:nr + npar + ncon])
        outs = f(r, p, c)
        for o_ref, o in zip(refs[nr + npar + ncon:], outs, strict=True):
            o_ref[...] = o

    return pl.pallas_call(
        body, name=name + "_fwd", grid=(T // tile,),
        in_specs=_row_specs(rows, params, consts, tile),
        out_specs=[pl.BlockSpec((tile, w), lambda i: (i, 0)) for w in out_widths],
        out_shape=[jax.ShapeDtypeStruct((T, w), F32) for w in out_widths],
        compiler_params=_params("parallel"),
    )(*rows, *params, *consts)


def _row_bwd(name, f, out_widths, tile, rows, params, consts, gouts, dr_dtype=F32):
    T = rows[0].shape[0]
    nr, npar, ncon, nout = len(rows), len(params), len(consts), len(out_widths)

    def body(*refs):
        r = tuple(x[...] for x in refs[:nr])
        p = tuple(x[...] for x in refs[nr:nr + npar])
        c = tuple(x[...] for x in refs[nr + npar:nr + npar + ncon])
        k = nr + npar + ncon
        g = tuple(x[...] for x in refs[k:k + nout])
        dr_refs = refs[k + nout:k + nout + nr]
        dp_refs = refs[k + nout + nr:]
        _, vjp = jax.vjp(lambda r_, p_: tuple(f(r_, p_, c)), r, p)
        dr, dp = vjp(g)
        for ref, val in zip(dr_refs, dr, strict=True):
            ref[...] = val.astype(ref.dtype)
        if npar:
            @pl.when(pl.program_id(0) == 0)
            def _():
                for ref in dp_refs:
                    ref[...] = jnp.zeros_like(ref)

            for ref, val in zip(dp_refs, dp, strict=True):
                ref[...] += val

    outs = pl.pallas_call(
        body, name=name + "_bwd", grid=(T // tile,),
        in_specs=_row_specs(rows, params, consts, tile) + [pl.BlockSpec((tile, w), lambda i: (i, 0)) for w in out_widths],
        out_specs=[pl.BlockSpec((tile, r.shape[1]), lambda i: (i, 0)) for r in rows]
        + [pl.BlockSpec(p.shape, lambda i: (0, 0)) for p in params],
        out_shape=[jax.ShapeDtypeStruct(r.shape, dr_dtype) for r in rows] + [jax.ShapeDtypeStruct(p.shape, F32) for p in params],
        compiler_params=_params("arbitrary"),
    )(*rows, *params, *consts, *gouts)
    return tuple(outs[:nr]), tuple(outs[nr:])


def make_proj_stage(name, f, out_widths, tile=ROW_TILE):
    def run(x, w16, params, consts):
        p = _mm_nn(name + "_mm", x, w16)
        return p, tuple(_row_fwd(name, f, out_widths, tile, (p,), params, consts))

    @jax.custom_vjp
    def op(x, w16, c32, c16, params, consts):
        return run(x, w16, params, consts)[1]

    def fwd(x, w16, c32, c16, params, consts):
        p, outs = run(x, w16, params, consts)
        return outs, (x, w16, p, params, consts)

    def bwd(res, g):
        x, w16, p, params, consts = res
        (dp,), dparams = _row_bwd(name, f, out_widths, tile, (p,), params, consts, tuple(g), dr_dtype=BF16)
        d32, d16 = _weight_grads(name + "_dw", x, dp, True)
        return _mm_nt(name + "_dx", dp, w16), jnp.zeros_like(w16), d32, d16, dparams, tuple(jnp.zeros_like(c) for c in consts)

    op.defvjp(fwd, bwd)
    return op


def make_rowop(name, f, out_widths, tile=ROW_TILE):
    @jax.custom_vjp
    def op(rows, params, consts):
        return tuple(_row_fwd(name, f, out_widths, tile, rows, params, consts))

    def fwd(rows, params, consts):
        return op(rows, params, consts), (rows, params, consts)

    def bwd(res, g):
        rows, params, consts = res
        dr, dp = _row_bwd(name, f, out_widths, tile, rows, params, consts, tuple(g))
        return dr, dp, tuple(jnp.zeros_like(c) for c in consts)

    op.defvjp(fwd, bwd)
    return op


def _rms(x, g):
    return x * lax.rsqrt(jnp.mean(x * x, axis=-1, keepdims=True) + EPS) * g


def _silu(x):
    return x * jax.nn.sigmoid(x)


def _bdot(a, b):
    return jnp.dot(a.astype(BF16), b.astype(BF16), preferred_element_type=F32)


def norm_f(rows, params, consts):
    return (_rms(rows[0], params[0]),)


def addnorm_f(rows, params, consts):
    x = rows[0] + rows[1]
    return x, _rms(x, params[0])


EVEN_GLA_END = 1568


def even_prep_f(rows, params, consts):
    (p,) = rows
    wa2f, wa2b, baf, bab, lbf, lbb = params
    gq = p[:, 0:256]
    gk = p[:, 256:512] * (GLA_DK ** -0.5)
    gv = p[:, 512:1024]
    gr = p[:, 1024:1536]
    glaf = jax.nn.log_sigmoid(_bdot(p[:, 1536:1552], wa2f) + baf) / GLA_GATE_NORM
    glab = jax.nn.log_sigmoid(_bdot(p[:, 1552:1568], wa2b) + bab) / GLA_GATE_NORM
    o = EVEN_GLA_END
    hq = _silu(p[:, o:o + 256])

    def gate(z, lb):
        f = lb + (1.0 - lb) * jax.nn.sigmoid(z)
        return jnp.log(jnp.maximum(f, HGRN_MIN_F)), (1.0 - lb) * jax.nn.sigmoid(-z)

    hlaf, hkf = gate(p[:, o + 256:o + 512], lbf)
    hlab, hkb = gate(p[:, o + 512:o + 768], lbb)
    hv = p[:, o + 768:o + 1280]
    hg = p[:, o + 1280:o + 1792]
    return gq, gk, gv, gq, gk, gv, glaf, glab, gr, hq, hkf, hv, hq, hkb, hv, hlaf, hlab, hg


EVEN_PREP_WIDTHS = (256, 256, 512, 256, 256, 512, 256, 256, 512, 256, 256, 512, 256, 256, 512, 256, 256, 512)


def _head_rms(o, g, heads, d):
    parts = []
    for h in range(heads):
        seg = o[:, h * d:(h + 1) * d]
        parts.append(seg * lax.rsqrt(jnp.mean(seg * seg, axis=-1, keepdims=True) + EPS))
    return jnp.concatenate(parts, axis=1) * g


def even_post_f(rows, params, consts):
    of, ob, hof, hob, gr, hg = rows
    a = _head_rms(of + ob, params[0], GLA_HEADS, GLA_DV) * _silu(gr)
    b = _head_rms(hof + hob, params[1], HGRN_HEADS, HGRN_DV) * _silu(hg)
    return (jnp.concatenate([a, b], axis=1),)


@jax.custom_vjp
def _roll_half(x):
    return pltpu.roll(x, RET_DK // 2, 1)


_roll_half.defvjp(lambda x: (_roll_half(x), None), lambda _, g: (_roll_half(g),))


def odd_prep_f(rows, params, consts):
    (p,) = rows
    cosf, sinf = consts

    def rot(t):
        parts = []
        for h in range(RET_HEADS):
            th = t[:, h * RET_DK:(h + 1) * RET_DK]
            parts.append(th * cosf + _roll_half(th) * sinf)
        return jnp.concatenate(parts, axis=1)

    rq = rot(p[:, 0:512])
    rk = rot(p[:, 512:1024]) * (RET_DK ** -0.5)
    rv = p[:, 1024:1792]
    return rq, rk, rv, rq, rk, rv, p[:, 1792:2560], p[:, 2560:2816]


ODD_PREP_WIDTHS = (512, 512, 768, 512, 512, 768, 768, 256)


def ret_post_f(rows, params, consts):
    of, ob, rg = rows
    o = of + ob
    parts = []
    for h in range(RET_HEADS):
        seg = o[:, h * RET_DV:(h + 1) * RET_DV]
        c = seg - jnp.mean(seg, axis=-1, keepdims=True)
        parts.append(c * lax.rsqrt(jnp.mean(c * c, axis=-1, keepdims=True) + EPS))
    return (jnp.concatenate(parts, axis=1) * params[0] * _silu(rg),)


def s5_post_f(rows, params, consts):
    h0r, h0i, h1r, h1i, u = rows
    c_re, c_im, d_skip, glu_w, glu_b = params
    hr = h0r + h1r
    hi = h0i + h1i
    y = _bdot(hr, c_re) - _bdot(hi, c_im) + d_skip * u
    g = jax.nn.gelu(y)
    return (g * jax.nn.sigmoid(_bdot(g, glu_w) + glu_b),)


def loss_head(x, r, g, target, tile=ROW_TILE):
    T, D = x.shape

    def body(x_ref, r_ref, g_ref, t_ref, loss_ref, dx_ref, dg_ref):
        t = t_ref[...]

        def lf(xv, gv):
            e = _rms(xv, gv) - t
            row = jnp.sum(e * e, axis=-1, keepdims=True) * (0.5 / D)
            return jnp.sum(row, axis=0, keepdims=True)

        l, vjp = jax.vjp(lf, x_ref[...] + r_ref[...], g_ref[...])
        dx, dg = vjp(jnp.ones((1, 1), F32))
        dx_ref[...] = dx

        @pl.when(pl.program_id(0) == 0)
        def _():
            loss_ref[...] = jnp.zeros_like(loss_ref)
            dg_ref[...] = jnp.zeros_like(dg_ref)

        loss_ref[...] += jnp.broadcast_to(l, loss_ref.shape)
        dg_ref[...] += dg

    row = pl.BlockSpec((tile, D), lambda i: (i, 0))
    vec = pl.BlockSpec((1, D), lambda i: (0, 0))
    return pl.pallas_call(
        body, name="loss_head", grid=(T // tile,),
        in_specs=[row, row, vec, row],
        out_specs=[pl.BlockSpec((1, LANES), lambda i: (0, 0)), row, vec],
        out_shape=[jax.ShapeDtypeStruct((1, LANES), F32), jax.ShapeDtypeStruct((T, D), F32), jax.ShapeDtypeStruct((1, D), F32)],
        compiler_params=_params("arbitrary"),
    )(x, r, g, target)


SUBLANES = 8


def _halo_specs(width, tile, T):
    n8 = tile // SUBLANES
    last = T // SUBLANES - 1
    return [pl.BlockSpec((tile, width), lambda i: (i, 0)),
            pl.BlockSpec((SUBLANES, width), lambda i: (jnp.maximum(i * n8 - 1, 0), 0)),
            pl.BlockSpec((SUBLANES, width), lambda i: (jnp.minimum((i + 1) * n8, last), 0))]


def _shift_rows(x, prev_row, next_row, tile):
    row = lax.broadcasted_iota(jnp.int32, (tile, 1), 0)
    down = jnp.where(row == 0, prev_row, pltpu.roll(x, 1, 0))
    up = jnp.where(row == tile - 1, next_row, pltpu.roll(x, tile - 1, 0))
    return down, up


def _conv_fwd(name, u, cw, cb, S, tile):
    T, F2 = u.shape
    F = F2 // 2
    per_seq = S // tile

    def body(u_ref, up_ref, un_ref, cw_ref, cb_ref, g_ref):
        pos = pl.program_id(0) % per_seq
        uv = u_ref[...]
        prev_row = jnp.where(pos == 0, 0.0, up_ref[SUBLANES - 1:SUBLANES, :])
        next_row = jnp.where(pos == per_seq - 1, 0.0, un_ref[0:1, :])
        down, up = _shift_rows(uv, prev_row, next_row, tile)
        c = _conv_taps(down, uv, up, cw_ref, cb_ref)
        g_ref[...] = (_silu(c[:, :F]) * c[:, F:]).astype(BF16)

    return pl.pallas_call(
        body, name=name + "_fwd", grid=(T // tile,),
        in_specs=_halo_specs(F2, tile, T) + [pl.BlockSpec((3, F2), lambda i: (0, 0)), pl.BlockSpec((1, F2), lambda i: (0, 0))],
        out_specs=pl.BlockSpec((tile, F), lambda i: (i, 0)),
        out_shape=jax.ShapeDtypeStruct((T, F), BF16),
        compiler_params=_params("parallel"),
    )(u, u, u, cw, cb)


def _conv_taps(down, mid, up, cw_ref, cb_ref):
    c = cb_ref[...] + down * cw_ref[0:1, :]
    c = c + mid * cw_ref[1:2, :]
    return c + up * cw_ref[2:3, :]


def _conv_bwd(name, u, dg, cw, cb, S, tile):
    T, F2 = u.shape
    F = F2 // 2
    per_seq = S // tile

    def dact(cv, dgv):
        a, v = cv[:, :F], cv[:, F:]
        sg = jax.nn.sigmoid(a)
        return jnp.concatenate([dgv * v * (sg * (1.0 + a * (1.0 - sg))), dgv * (a * sg)], axis=1)

    def body(u_ref, up_ref, un_ref, g_ref, gp_ref, gn_ref, cw_ref, cb_ref, du_ref, dw0_ref, dw1_ref, dw2_ref, db_ref):
        i = pl.program_id(0)
        pos = i % per_seq
        first, last = pos == 0, pos == per_seq - 1
        lo, hi = slice(SUBLANES - 1, SUBLANES), slice(0, 1)
        uv = u_ref[...]
        u_m1, u_p1 = jnp.where(first, 0.0, up_ref[lo, :]), jnp.where(last, 0.0, un_ref[hi, :])
        u_dn, u_up = _shift_rows(uv, u_m1, u_p1, tile)
        dc = dact(_conv_taps(u_dn, uv, u_up, cw_ref, cb_ref), g_ref[...])
        c_m1 = _conv_taps(up_ref[SUBLANES - 2:SUBLANES - 1, :], u_m1, u_ref[0:1, :], cw_ref, cb_ref)
        c_p1 = _conv_taps(u_ref[tile - 1:tile, :], u_p1, un_ref[1:2, :], cw_ref, cb_ref)
        dc_prev = jnp.where(first, 0.0, dact(c_m1, gp_ref[lo, :]))
        dc_next = jnp.where(last, 0.0, dact(c_p1, gn_ref[hi, :]))
        dc_dn, dc_up = _shift_rows(dc, dc_prev, dc_next, tile)
        du = dc_up * cw_ref[0:1, :]
        du = du + dc * cw_ref[1:2, :]
        du_ref[...] = (du + dc_dn * cw_ref[2:3, :]).astype(BF16)

        @pl.when(i == 0)
        def _():
            for ref in (dw0_ref, dw1_ref, dw2_ref, db_ref):
                ref[...] = jnp.zeros_like(ref)

        dw0_ref[...] += jnp.sum(dc * u_dn, axis=0, keepdims=True)
        dw1_ref[...] += jnp.sum(dc * uv, axis=0, keepdims=True)
        dw2_ref[...] += jnp.sum(dc * u_up, axis=0, keepdims=True)
        db_ref[...] += jnp.sum(dc, axis=0, keepdims=True)

    vec = pl.BlockSpec((1, F2), lambda i: (0, 0))
    du, dw0, dw1, dw2, db = pl.pallas_call(
        body, name=name + "_bwd", grid=(T // tile,),
        in_specs=_halo_specs(F2, tile, T) + _halo_specs(F, tile, T) + [pl.BlockSpec((3, F2), lambda i: (0, 0)), vec],
        out_specs=[pl.BlockSpec((tile, F2), lambda i: (i, 0)), vec, vec, vec, vec],
        out_shape=[jax.ShapeDtypeStruct((T, F2), BF16)] + [jax.ShapeDtypeStruct((1, F2), F32)] * 4,
        compiler_params=_params("arbitrary"),
    )(u, u, u, dg, dg, dg, cw, cb)
    return du, jnp.concatenate([dw0, dw1, dw2], axis=0), db


def make_ffn(name, S):
    def run(x, wup16, wdn16, cw, cb):
        u = _mm_nn(name + "_up", x, wup16)
        g16 = _conv_fwd(name + "_conv", u, cw, cb, S, ROW_TILE)
        return u, g16, _mm_nn(name + "_down", g16, wdn16)

    @jax.custom_vjp
    def op(x, wup16, cu32, cu16, wdn16, cd32, cd16, cw, cb):
        return run(x, wup16, wdn16, cw, cb)[2], _tick()

    def fwd(x, wup16, cu32, cu16, wdn16, cd32, cd16, cw, cb):
        u, g16, out = run(x, wup16, wdn16, cw, cb)
        return (out, _tick()), (x, wup16, wdn16, u, g16, cw, cb)

    def bwd(res, g):
        x, wup16, wdn16, u, g16, cw, cb = res
        dout, after = g
        dg = _mm_nt(name + "_down_dx", dout, wdn16, after)
        dd32, dd16 = _weight_grads(name + "_down_dw", g16, dout, False)
        du16, dcw, dcb = _conv_bwd(name + "_conv", u, dg, cw, cb, S, ROW_TILE)
        du32, du16w = _weight_grads(name + "_up_dw", x, du16, True)
        return (_mm_nt(name + "_up_dx", du16, wup16), jnp.zeros_like(wup16), du32, du16w, jnp.zeros_like(wdn16), dd32, dd16, dcw, dcb)

    op.defvjp(fwd, bwd)
    return op


def _dot_nt(a, b):
    return lax.dot_general(a.astype(BF16), b.astype(BF16), (((1,), (1,)), ((), ())), preferred_element_type=F32)


def _dot_tn(a, b):
    return lax.dot_general(a.astype(BF16), b.astype(BF16), (((0,), (0,)), ((), ())), preferred_element_type=F32)


def _chunk_decays(la, direction, C, width, dk, const_lg):
    row = lax.broadcasted_iota(jnp.int32, (C, C), 0)
    col = lax.broadcasted_iota(jnp.int32, (C, C), 1)
    keep = (row >= col) if direction == 0 else (row <= col)
    ridx = lax.broadcasted_iota(jnp.int32, (C, 1), 0)
    if const_lg is None:
        cum = jnp.dot(keep.astype(F32), la, precision=HIGHEST, preferred_element_type=F32)
    else:
        lane_head = lax.broadcasted_iota(jnp.int32, (1, width), 1) // dk
        lg = jnp.zeros((1, width), F32)
        for h, val in enumerate(const_lg):
            lg = jnp.where(lane_head == h, val, lg)
        steps = (ridx + 1) if direction == 0 else (C - ridx)
        cum = steps.astype(F32) * lg
    exit_row = C - 1 if direction == 0 else 0
    mid = jnp.sum(jnp.where(ridx == C // 2, cum, 0.0), axis=0, keepdims=True)
    last = jnp.sum(jnp.where(ridx == exit_row, cum, 0.0), axis=0, keepdims=True)
    return keep, ridx == exit_row, cum, mid, last


def _scan_fwd(name, qkv, laf, lab, H, dk, dv, S, C, const_lg):
    qf, kf, vf, qb, kb, vb = qkv
    T = qf.shape[0]
    B, nc = T // S, S // C
    Wk, Wv = H * dk, H * dv
    learn = const_lg is None

    def body(*refs):
        @pl.when(pl.program_id(0) == 0)
        def _():
            refs[-1][...] = jnp.zeros_like(refs[-1])

        for b in range(B):
            one_sequence(*[r.at[b] for r in refs])

    def one_sequence(*refs):
        if learn:
            qf_r, qb_r, kf_r, kb_r, vf_r, vb_r, laf_r, lab_r, of_r, ob_r, sf_r, sb_r, st = refs
            las = (laf_r[...], lab_r[...])
        else:
            qf_r, qb_r, kf_r, kb_r, vf_r, vb_r, of_r, ob_r, sf_r, sb_r, st = refs
            las = (None, None)

        for d, (q_r, k_r, v_r, o_r, s_r) in enumerate(((qf_r, kf_r, vf_r, of_r, sf_r), (qb_r, kb_r, vb_r, ob_r, sb_r))):
            keep, _, cum, mid, last = _chunk_decays(las[d], d, C, Wk, dk, None if learn else const_lg[d])
            qe = q_r[...] * jnp.exp(cum - mid)
            ke = k_r[...] * jnp.exp(mid - cum)
            q_in = qe * jnp.exp(mid)
            k_out = ke * jnp.exp(last - mid)
            e_last = jnp.exp(last)
            vv = v_r[...]
            for h in range(H):
                ks, vs = slice(h * dk, (h + 1) * dk), slice(h * dv, (h + 1) * dv)
                a = jnp.where(keep, _dot_nt(qe[:, ks], ke[:, ks]), 0.0)
                state = st[d, h]
                o_r[:, vs] = _bdot(a, vv[:, vs]) + _dot_nt(q_in[:, ks], state)
                s_r[h * dv:(h + 1) * dv, :] = state
                st[d, h] = state * e_last[:, ks] + _dot_tn(vv[:, vs], k_out[:, ks])

    fpos = lambda c: c
    bpos = lambda c: nc - 1 - c
    kspec = lambda pos: pl.BlockSpec((B, C, Wk), lambda c: (0, pos(c), 0))
    vspec = lambda pos: pl.BlockSpec((B, C, Wv), lambda c: (0, pos(c), 0))
    sspec = lambda pos: pl.BlockSpec((B, None, Wv, dk), lambda c: (0, pos(c), 0, 0))
    seq = lambda t: t.reshape(B, S, t.shape[1])
    ins = [seq(t) for t in [qf, qb, kf, kb, vf, vb] + ([laf, lab] if learn else [])]
    in_specs = [kspec(fpos), kspec(bpos), kspec(fpos), kspec(bpos), vspec(fpos), vspec(bpos)] + ([kspec(fpos), kspec(bpos)] if learn else [])
    of, ob, sf, sb = pl.pallas_call(
        body, name=name + "_fwd", grid=(nc,), in_specs=in_specs,
        out_specs=[vspec(fpos), vspec(bpos), sspec(fpos), sspec(bpos)],
        out_shape=[jax.ShapeDtypeStruct((B, S, Wv), F32)] * 2 + [jax.ShapeDtypeStruct((B, nc, Wv, dk), F32)] * 2,
        scratch_shapes=[pltpu.VMEM((B, 2, H, dv, dk), F32)],
        compiler_params=_params("arbitrary"),
    )(*ins)
    return of.reshape(T, Wv), ob.reshape(T, Wv), sf, sb


def _scan_bwd(name, qkv, laf, lab, sf, sb, dof, dob, H, dk, dv, S, C, const_lg):
    qf, kf, vf, qb, kb, vb = qkv
    T = qf.shape[0]
    B, nc = T // S, S // C
    Wk, Wv = H * dk, H * dv
    learn = const_lg is None

    def body(*refs):
        @pl.when(pl.program_id(0) == 0)
        def _():
            refs[-1][...] = jnp.zeros_like(refs[-1])

        for b in range(B):
            one_sequence(*[r.at[b] for r in refs])

    def one_sequence(*refs):
        if learn:
            (qf_r, qb_r, kf_r, kb_r, vf_r, vb_r, laf_r, lab_r, sf_r, sb_r, dof_r, dob_r,
             dqf_r, dqb_r, dkf_r, dkb_r, dvf_r, dvb_r, dlaf_r, dlab_r, dst) = refs
            las, dlas = (laf_r[...], lab_r[...]), (dlaf_r, dlab_r)
        else:
            (qf_r, qb_r, kf_r, kb_r, vf_r, vb_r, sf_r, sb_r, dof_r, dob_r,
             dqf_r, dqb_r, dkf_r, dkb_r, dvf_r, dvb_r, dst) = refs
            las, dlas = (None, None), (None, None)

        groups = ((qf_r, kf_r, vf_r, sf_r, dof_r, dqf_r, dkf_r, dvf_r), (qb_r, kb_r, vb_r, sb_r, dob_r, dqb_r, dkb_r, dvb_r))
        for d, (q_r, k_r, v_r, s_r, do_r, dq_r, dk_r, dv_r) in enumerate(groups):
            keep, is_exit, cum, mid, last = _chunk_decays(las[d], d, C, Wk, dk, None if learn else const_lg[d])
            eq, ek = jnp.exp(cum - mid), jnp.exp(mid - cum)
            e_in, e_out, e_last = jnp.exp(mid), jnp.exp(last - mid), jnp.exp(last)
            qe, ke = q_r[...] * eq, k_r[...] * ek
            q_in, k_out = qe * e_in, ke * e_out
            vv, do = v_r[...], do_r[...]
            dqe_parts, dke_parts, dlast_parts = [], [], []
            for h in range(H):
                ks, vs = slice(h * dk, (h + 1) * dk), slice(h * dv, (h + 1) * dv)
                a = jnp.where(keep, _dot_nt(qe[:, ks], ke[:, ks]), 0.0)
                dp = jnp.where(keep, _dot_nt(do[:, vs], vv[:, vs]), 0.0)
                s_prev = s_r[h * dv:(h + 1) * dv, :]
                ds = dst[d, h]
                dk_out = _bdot(vv[:, vs], ds)
                dqe_parts.append(_bdot(dp, ke[:, ks]) + _bdot(do[:, vs], s_prev) * e_in[:, ks])
                dke_parts.append(_dot_tn(dp, qe[:, ks]) + dk_out * e_out[:, ks])
                dv_r[:, vs] = _dot_tn(a, do[:, vs]) + _dot_nt(k_out[:, ks], ds)
                if learn:
                    dlast_parts.append(jnp.sum(dk_out * k_out[:, ks], axis=0, keepdims=True)
                                       + jnp.sum(ds * s_prev, axis=0, keepdims=True) * e_last[:, ks])
                dst[d, h] = ds * e_last[:, ks] + _dot_tn(do[:, vs], q_in[:, ks])
            dqe = jnp.concatenate(dqe_parts, axis=1)
            dke = jnp.concatenate(dke_parts, axis=1)
            dq_r[...] = dqe * eq
            dk_r[...] = dke * ek
            if learn:
                dcum = dqe * qe - dke * ke + jnp.where(is_exit, jnp.concatenate(dlast_parts, axis=1), 0.0)
                dlas[d][...] = lax.dot_general(keep.astype(F32), dcum, (((0,), (0,)), ((), ())), precision=HIGHEST,
                                               preferred_element_type=F32)

    fpos = lambda c: nc - 1 - c
    bpos = lambda c: c
    kspec = lambda pos: pl.BlockSpec((B, C, Wk), lambda c: (0, pos(c), 0))
    vspec = lambda pos: pl.BlockSpec((B, C, Wv), lambda c: (0, pos(c), 0))
    sspec = lambda pos: pl.BlockSpec((B, None, Wv, dk), lambda c: (0, pos(c), 0, 0))
    seq = lambda t: t.reshape(B, S, t.shape[1])
    ins = [seq(t) for t in [qf, qb, kf, kb, vf, vb] + ([laf, lab] if learn else [])] + [sf, sb, seq(dof), seq(dob)]
    in_specs = ([kspec(fpos), kspec(bpos), kspec(fpos), kspec(bpos), vspec(fpos), vspec(bpos)]
                + ([kspec(fpos), kspec(bpos)] if learn else []) + [sspec(fpos), sspec(bpos), vspec(fpos), vspec(bpos)])
    out_specs = [kspec(fpos), kspec(bpos), kspec(fpos), kspec(bpos), vspec(fpos), vspec(bpos)] + ([kspec(fpos), kspec(bpos)] if learn else [])
    out_shape = ([jax.ShapeDtypeStruct((B, S, Wk), F32)] * 4 + [jax.ShapeDtypeStruct((B, S, Wv), F32)] * 2
                 + ([jax.ShapeDtypeStruct((B, S, Wk), F32)] * 2 if learn else []))
    outs = pl.pallas_call(
        body, name=name + "_bwd", grid=(nc,), in_specs=in_specs, out_specs=out_specs, out_shape=out_shape,
        scratch_shapes=[pltpu.VMEM((B, 2, H, dv, dk), F32)],
        compiler_params=_params("arbitrary"),
    )(*ins)
    return [t.reshape(T, t.shape[2]) for t in outs]


def make_scan(name, H, dk, dv, S, C, const_lg=None):
    if const_lg is None:
        @jax.custom_vjp
        def op(qkv, laf, lab):
            return tuple(_scan_fwd(name, qkv, laf, lab, H, dk, dv, S, C, None)[:2])

        def fwd(qkv, laf, lab):
            of, ob, sf, sb = _scan_fwd(name, qkv, laf, lab, H, dk, dv, S, C, None)
            return (of, ob), (qkv, laf, lab, sf, sb)

        def bwd(res, g):
            qkv, laf, lab, sf, sb = res
            dqf, dqb, dkf, dkb, dvf, dvb, dlaf, dlab = _scan_bwd(name, qkv, laf, lab, sf, sb, g[0], g[1], H, dk, dv, S, C, None)
            return (dqf, dkf, dvf, dqb, dkb, dvb), dlaf, dlab
    else:
        @jax.custom_vjp
        def op(qkv):
            return tuple(_scan_fwd(name, qkv, None, None, H, dk, dv, S, C, const_lg)[:2])

        def fwd(qkv):
            of, ob, sf, sb = _scan_fwd(name, qkv, None, None, H, dk, dv, S, C, const_lg)
            return (of, ob), (qkv, sf, sb)

        def bwd(res, g):
            qkv, sf, sb = res
            dqf, dqb, dkf, dkb, dvf, dvb = _scan_bwd(name, qkv, None, None, sf, sb, g[0], g[1], H, dk, dv, S, C, const_lg)
            return ((dqf, dkf, dvf, dqb, dkb, dvb),)

    op.defvjp(fwd, bwd)
    return op


def _reorder_call(name, t, S, to_segments):
    T, w = t.shape
    n_it = S // S5_SEGMENTS

    def body(x_ref, o_ref):
        def step(i, carry):
            packed = pl.ds(pl.multiple_of(i * S5_SEGMENTS, S5_SEGMENTS), S5_SEGMENTS)
            spread = pl.ds(i, S5_SEGMENTS, stride=n_it)
            if to_segments:
                o_ref[packed, :] = x_ref[spread, :]
            else:
                o_ref[spread, :] = x_ref[packed, :]
            return carry

        lax.fori_loop(0, n_it, step, 0, unroll=8)

    blk = pl.BlockSpec((S, LANES), lambda b, j: (b, j))
    return pl.pallas_call(body, name=name, grid=(T // S, w // LANES), in_specs=[blk], out_specs=blk,
                          out_shape=jax.ShapeDtypeStruct(t.shape, t.dtype), compiler_params=_params("parallel", "parallel"))(t)


def make_join(name, S, wa, wb):
    n_it = S // S5_SEGMENTS

    def call(tag, forward, wa, wb, arrs):
        T = arrs[0].shape[0]
        na, nb = wa // LANES, wb // LANES

        def body(*refs):
            j = pl.program_id(1)
            a_ref, b_ref, y_ref = (refs[0], refs[1], refs[2]) if forward else (refs[1], refs[2], refs[0])

            @pl.when(j < na)
            def _():
                if forward:
                    y_ref[...] = a_ref[...]
                else:
                    a_ref[...] = y_ref[...]

            @pl.when(j >= na)
            def _():
                def step(i, carry):
                    packed = pl.ds(pl.multiple_of(i * S5_SEGMENTS, S5_SEGMENTS), S5_SEGMENTS)
                    spread = pl.ds(i, S5_SEGMENTS, stride=n_it)
                    if forward:
                        y_ref[spread, :] = b_ref[packed, :]
                    else:
                        b_ref[packed, :] = y_ref[spread, :]
                    return carry

                lax.fori_loop(0, n_it, step, 0, unroll=8)

        a_spec = pl.BlockSpec((S, LANES), lambda b, j: (b, jnp.minimum(j, na - 1)))
        b_spec = pl.BlockSpec((S, LANES), lambda b, j: (b, jnp.maximum(j - na, 0)))
        y_spec = pl.BlockSpec((S, LANES), lambda b, j: (b, j))
        shapes = [jax.ShapeDtypeStruct((T, wa), F32), jax.ShapeDtypeStruct((T, wb), F32), jax.ShapeDtypeStruct((T, wa + wb), F32)]
        return pl.pallas_call(
            body, name=name + tag, grid=(T // S, na + nb),
            in_specs=[a_spec, b_spec] if forward else [y_spec], out_specs=y_spec if forward else [a_spec, b_spec],
            out_shape=shapes[2] if forward else shapes[:2], compiler_params=_params("parallel", "arbitrary"),
        )(*arrs)

    @jax.custom_vjp
    def op(a, b):
        return call("_fwd", True, wa, wb, (a, b))

    op.defvjp(lambda a, b: (call("_fwd", True, wa, wb, (a, b)), None), lambda _, dy: tuple(call("_bwd", False, wa, wb, (dy,))))
    return op


def make_reorder(name, S, to_segments):
    @jax.custom_vjp
    def op(t):
        return _reorder_call(name, t, S, to_segments)

    op.defvjp(lambda t: (_reorder_call(name, t, S, to_segments), None),
              lambda _, g: (_reorder_call(name + "_bwd", g, S, not to_segments),))
    return op


def _s5_scan_call(name, Xs, A, S, dirs, prev=None):
    with_p = prev is not None
    T, N = Xs[0].shape
    B, nl = T // S, N // LANES
    n_it = S // S5_SEGMENTS
    assert n_it & (n_it - 1) == 0

    def cmul(ar, ai, br, bi):
        return ar * br - ai * bi, ar * bi + ai * br

    def body(*refs):
        x, a_ref = refs[0:4], refs[4]
        if with_p:
            h_prev, x_prev, h, p_ref = refs[5:9], refs[9:13], refs[13:17], refs[17]
        else:
            h = refs[5:9]
        seg = lax.broadcasted_iota(jnp.int32, (S5_SEGMENTS, 1), 0)
        zero = jnp.zeros((S5_SEGMENTS, LANES), F32)
        a = [(jnp.broadcast_to(a_ref[2 * k:2 * k + 1, :], (S5_SEGMENTS, LANES)),
              jnp.broadcast_to(a_ref[2 * k + 1:2 * k + 2, :], (S5_SEGMENTS, LANES))) for k in range(2)]

        def rows_of(k, i):
            return pl.ds(pl.multiple_of(((n_it - 1 - i) if dirs[k] == 1 else i) * S5_SEGMENTS, S5_SEGMENTS), S5_SEGMENTS)

        def local(i, carry):
            out = []
            for k, (sr, si) in enumerate(carry):
                rows = rows_of(k, i)
                pr, pi = cmul(*a[k], sr, si)
                sr, si = pr + x[2 * k][rows, :], pi + x[2 * k + 1][rows, :]
                h[2 * k][rows, :] = sr
                h[2 * k + 1][rows, :] = si
                out.append((sr, si))
            return tuple(out)

        ends = lax.fori_loop(0, n_it, local, ((zero, zero), (zero, zero)), unroll=8)
        inherit = []
        for k, (er, ei) in enumerate(ends):
            back = dirs[k] == 1
            pr, pi = a[k]
            for _ in range(n_it.bit_length() - 1):
                pr, pi = cmul(pr, pi, pr, pi)
            shift = (S5_SEGMENTS - 1) if back else 1
            tr, ti = er, ei
            for r in (range(S5_SEGMENTS - 2, -1, -1) if back else range(1, S5_SEGMENTS)):
                nr, ni = cmul(pr, pi, pltpu.roll(tr, shift, 0), pltpu.roll(ti, shift, 0))
                tr = jnp.where(seg == r, er + nr, tr)
                ti = jnp.where(seg == r, ei + ni, ti)
            edge = (S5_SEGMENTS - 1) if back else 0
            inherit.append((jnp.where(seg == edge, 0.0, pltpu.roll(tr, shift, 0)), jnp.where(seg == edge, 0.0, pltpu.roll(ti, shift, 0))))

        def fix(i, carry):
            powers, sums = carry
            new_powers, new_sums = [], []
            for k in range(2):
                rows = rows_of(k, i)
                fr, fi = cmul(*powers[k], *inherit[k])
                sr, si = h[2 * k][rows, :] + fr, h[2 * k + 1][rows, :] + fi
                h[2 * k][rows, :] = sr
                h[2 * k + 1][rows, :] = si
                new_powers.append(cmul(*powers[k], *a[k]))
                if with_p:
                    ur = h_prev[2 * k][rows, :] - x_prev[2 * k][rows, :]
                    ui = h_prev[2 * k + 1][rows, :] - x_prev[2 * k + 1][rows, :]
                    new_sums.append((sums[k][0] + sr * ur + si * ui, sums[k][1] + si * ur - sr * ui))
            return tuple(new_powers), tuple(new_sums)

        _, sums = lax.fori_loop(0, n_it, fix, ((a[0], a[1]), ((zero, zero), (zero, zero)) if with_p else ()), unroll=8)
        if with_p:
            for k in range(2):
                p_ref[2 * k:2 * k + 1, :] = jnp.sum(sums[k][0], axis=0, keepdims=True)
                p_ref[2 * k + 1:2 * k + 2, :] = jnp.sum(sums[k][1], axis=0, keepdims=True)

    col = pl.BlockSpec((S, LANES), lambda b, j: (b, j))
    outs = pl.pallas_call(
        body, name=name, grid=(B, nl),
        in_specs=[col] * 4 + [pl.BlockSpec((4, LANES), lambda b, j: (0, j))] + ([col] * 8 if with_p else []),
        out_specs=[col] * 4 + ([pl.BlockSpec((None, 4, LANES), lambda b, j: (b, 0, j))] if with_p else []),
        out_shape=[jax.ShapeDtypeStruct((T, N), F32)] * 4 + ([jax.ShapeDtypeStruct((B, 4, N), F32)] if with_p else []),
        compiler_params=_params("parallel", "parallel"),
    )(*Xs, A, *(prev[0] + prev[1] if with_p else ()))
    return (tuple(outs[:4]), outs[4]) if with_p else tuple(outs)


def make_s5_scan(name, S):
    @jax.custom_vjp
    def op(X, A):
        return _s5_scan_call(name + "_fwd", X, A, S, (0, 1))

    def fwd(X, A):
        H = _s5_scan_call(name + "_fwd", X, A, S, (0, 1))
        return H, (X, A, H)

    def bwd(res, G):
        X, A, H = res
        conj = A * jnp.array([[1.0], [-1.0], [1.0], [-1.0]], F32)
        lam, P = _s5_scan_call(name + "_bwd", tuple(G), conj, S, (1, 0), prev=(tuple(H), tuple(X)))
        P = jnp.sum(P, axis=0)
        ar, ai = A[0::2], A[1::2]
        pr, pi = P[0::2], P[1::2]
        den = ar * ar + ai * ai
        dar, dai = (pr * ar - pi * ai) / den, (pr * ai + pi * ar) / den
        return lam, jnp.stack([dar[0], dai[0], dar[1], dai[1]], axis=0)

    op.defvjp(fwd, bwd)
    return op


ANY = pl.BlockSpec(memory_space=pl.ANY)


def _place():
    x, y, c = lax.axis_index("x"), lax.axis_index("y"), lax.axis_index("c")
    return x, y, c, [(1 - x, y), (x, 1 - y), (1 - x, 1 - y)]


def all_gather(name, arrs):
    n = len(arrs)

    def body(*refs):
        ins, outs = refs[:n], refs[n:2 * n]
        send, recv, lsem = refs[2 * n:]
        x, y, c, chips = _place()
        me, sibling = (x, y, c), (x, y, 1 - c)

        def copy(a, k, block, to, src=None):
            slot = outs[a].at[4 * block[0] + 2 * block[1] + block[2]]
            return pltpu.make_async_remote_copy(src_ref=slot if src is None else src, dst_ref=slot, send_sem=send.at[a, k],
                                                recv_sem=recv.at[a, k], device_id=to, device_id_type=MESH)

        mine = [pltpu.make_async_copy(ins[a], outs[a].at[4 * x + 2 * y + c], lsem.at[a]) for a in range(n)]
        first = []
        for a in range(n):
            mine[a].start()
            first.append(copy(a, 0, me, sibling, src=ins[a]))
            first += [copy(a, 1 + j, me, (*chip, c), src=ins[a]) for j, chip in enumerate(chips)]
        for cp in first:
            cp.start()
        passed = []
        for a in range(n):
            for j, chip in enumerate(chips):
                copy(a, 1 + j, (*chip, c), me).wait_recv()
                fwd = copy(a, 4 + j, (*chip, c), sibling)
                fwd.start()
                passed.append(fwd)
        for a in range(n):
            copy(a, 0, sibling, me).wait_recv()
            for j, chip in enumerate(chips):
                copy(a, 4 + j, (*chip, 1 - c), me).wait_recv()
        for cp in first + passed:
            cp.wait_send()
        for cp in mine:
            cp.wait()

    return pl.pallas_call(
        body, name=name, in_specs=[ANY] * n, out_specs=[ANY] * n,
        out_shape=[jax.ShapeDtypeStruct((N_DEV,) + a.shape, a.dtype) for a in arrs],
        scratch_shapes=[pltpu.SemaphoreType.DMA((n, 7)), pltpu.SemaphoreType.DMA((n, 7)), pltpu.SemaphoreType.DMA((n,))],
    )(*arrs)


HBM = pl.BlockSpec(memory_space=pltpu.HBM)
SEM = pl.BlockSpec(memory_space=pltpu.SEMAPHORE)
EFFECT = pltpu.SideEffectType.DATAFLOW_SIDE_EFFECTING
GATHER_PEERS = (1, 2, 4, 6)
OTHER_CHIPS = (2, 4, 6)
COPIES_PER_ARRAY = {"scatter": N_DEV - 1, "gather": len(GATHER_PEERS), "forward": len(OTHER_CHIPS)}


def _split_plan(mode, srcs, lands, send, recv):
    x, y, c = lax.axis_index("x"), lax.axis_index("y"), lax.axis_index("c")

    def dev(k):
        return (1 - x if k & 4 else x), (1 - y if k & 2 else y), (1 - c if k & 1 else c)

    def idx(d):
        return 4 * d[0] + 2 * d[1] + d[2]

    me = idx((x, y, c))
    plan = []
    for a, land in enumerate(lands):
        if mode == "scatter":
            legs = [(srcs[a].at[idx(dev(k))], land.at[me], land.at[idx(dev(k))], dev(k)) for k in range(1, N_DEV)]
        elif mode == "gather":
            legs = [(srcs[a], land.at[me], land.at[idx(dev(k))], dev(k)) for k in GATHER_PEERS]
        else:
            legs = [(land.at[idx(dev(j))], land.at[idx(dev(j))], land.at[idx(dev(j ^ 1))], dev(1)) for j in OTHER_CHIPS]
        for i, (src, dst, arrival, to) in enumerate(legs):
            sem = a * len(legs) + i
            pair = tuple(pltpu.make_async_remote_copy(src_ref=src, dst_ref=d, send_sem=send.at[sem], recv_sem=recv.at[sem],
                                                      device_id=to, device_id_type=MESH) for d in (dst, arrival))
            plan.append(pair)
    return plan


def split_start(name, mode, srcs, lands, after):
    if lands is None:
        lands = [lax.empty((N_DEV,) + (s.shape[1:] if mode == "scatter" else s.shape), s.dtype) for s in srcs]
    ns, nl = len(srcs), len(lands)
    nsem = COPIES_PER_ARRAY[mode] * nl

    def body(*refs):
        ins, lnd = refs[:ns], refs[ns:ns + nl]
        send, recv = refs[ns + nl + 1], refs[ns + nl + 2]
        token = refs[-1]
        for out, _ in _split_plan(mode, ins, lnd, send, recv):
            out.start()
        token[...] = jnp.zeros_like(token)

    arrs = list(srcs) + list(lands)
    return pl.pallas_call(
        body, name=name,
        out_shape=(pltpu.SemaphoreType.DMA((nsem,)), pltpu.SemaphoreType.DMA((nsem,)))
        + tuple(pltpu.HBM(t.shape, t.dtype) for t in arrs) + (jax.ShapeDtypeStruct((SUBLANES, LANES), F32),),
        in_specs=[HBM] * len(arrs) + [ANY],
        out_specs=(SEM, SEM) + (HBM,) * len(arrs) + (pl.BlockSpec(memory_space=pltpu.VMEM),),
        input_output_aliases={i: 2 + i for i in range(len(arrs))},
        compiler_params=pltpu.CompilerParams(has_side_effects=EFFECT),
    )(*[pltpu.with_memory_space_constraint(t, pltpu.HBM) for t in arrs], after)


def split_wait(name, mode, handle, after):
    send, recv = handle[0], handle[1]
    arrs = list(handle[2:-1])
    nl = len(arrs) if mode == "forward" else len(arrs) // 2
    ns = len(arrs) - nl

    def body(*refs):
        ins, lnd = refs[:ns], refs[ns:ns + nl]
        s, r = refs[ns + nl], refs[ns + nl + 1]
        for out, arrival in _split_plan(mode, ins, lnd, s, r):
            out.wait_send()
            arrival.wait_recv()

    outs = pl.pallas_call(
        body, name=name,
        out_shape=tuple(pltpu.HBM(t.shape, t.dtype) for t in arrs),
        in_specs=[HBM] * len(arrs) + [SEM, SEM, ANY], out_specs=(HBM,) * len(arrs),
        input_output_aliases={i: i for i in range(len(arrs))},
        compiler_params=pltpu.CompilerParams(has_side_effects=EFFECT),
    )(*arrs, send, recv, after)
    return list(outs[ns:])


def _row_tile(rows, cols):
    cap = max(SUBLANES, (2**18 // cols) // SUBLANES * SUBLANES)
    if rows <= cap:
        return rows
    for t in range(cap, SUBLANES - 1, -SUBLANES):
        if rows % t == 0:
            return t
    return rows


def ordered_sum(name, parts, own, me):
    n, R, C = parts.shape
    rt = _row_tile(R, C)

    def body(me_ref, p_ref, own_ref, o_ref):
        s = None
        for k in range(n):
            t = jnp.where(me_ref[0] == k, own_ref[...], p_ref[k])
            s = t if s is None else s + t
        o_ref[...] = s

    return pl.pallas_call(
        body, name=name,
        grid_spec=pltpu.PrefetchScalarGridSpec(
            num_scalar_prefetch=1, grid=(R // rt,),
            in_specs=[pl.BlockSpec((n, rt, C), lambda r, p: (0, r, 0)), pl.BlockSpec((rt, C), lambda r, p: (r, 0))],
            out_specs=pl.BlockSpec((rt, C), lambda r, p: (r, 0))),
        out_shape=jax.ShapeDtypeStruct((R, C), F32), compiler_params=_params("parallel"),
    )(me, parts, own)


def _adamw_update(w, m, v, g):
    bias1 = 1.0 - ADAM_B1 ** ADAM_STEP
    bias2 = 1.0 - ADAM_B2 ** ADAM_STEP
    m_new = ADAM_B1 * m + (1.0 - ADAM_B1) * g
    v_new = ADAM_B2 * v + (1.0 - ADAM_B2) * (g * g)
    delta = -ADAM_LR * ((m_new / bias1) / (jnp.sqrt(v_new / bias2) + ADAM_EPS) + ADAM_WD * w)
    return delta, m_new, v_new


def adamw(name, w, m, v, g):
    R, C = w.shape
    rt = _row_tile(R, C)

    def body(w_ref, m_ref, v_ref, g_ref, d_ref, mo_ref, vo_ref):
        d_ref[...], mo_ref[...], vo_ref[...] = _adamw_update(w_ref[...], m_ref[...], v_ref[...], g_ref[...])

    row = pl.BlockSpec((rt, C), lambda r: (r, 0))
    return pl.pallas_call(
        body, name=name, grid=(R // rt,), in_specs=[row] * 4, out_specs=[row] * 3,
        out_shape=[jax.ShapeDtypeStruct((R, C), F32)] * 3, compiler_params=_params("parallel"),
    )(w, m, v, g)


def adamw_sharded(name, layer, w, m, v, own, land, me, prev):
    _, R, C = own.shape
    rt = _row_tile(R, C)

    def body(me_ref, w_ref, m_ref, v_ref, own_ref, land_ref, *rest):
        go_ref, d_ref, mo_ref, vo_ref = rest[-4:]
        g = own_ref[...]
        for k in range(N_DEV):
            g = g + jnp.where(me_ref[0] == k, 0.0, land_ref[k].astype(F32))
        go_ref[...] = g
        d_ref[...], mo_ref[...], vo_ref[...] = _adamw_update(w_ref[...], m_ref[...], v_ref[...], g)

    row = pl.BlockSpec((None, rt, C), lambda r, p: (layer, r, 0))
    in_specs = [row, row, row, pl.BlockSpec((None, rt, C), lambda r, p: (p[0], r, 0)), pl.BlockSpec((N_DEV, rt, C), lambda r, p: (0, r, 0))]
    ins = [me, w, m, v, own, land]
    aliases = {}
    if prev is not None:
        in_specs += [ANY] * 4
        aliases = {len(ins) + k: k for k in range(4)}
        ins += list(prev)
    return pl.pallas_call(
        body, name=name,
        grid_spec=pltpu.PrefetchScalarGridSpec(num_scalar_prefetch=1, grid=(R // rt,), in_specs=in_specs, out_specs=[row] * 4),
        out_shape=[jax.ShapeDtypeStruct(w.shape, F32)] * 4, input_output_aliases=aliases,
        compiler_params=_params("arbitrary"),
    )(*ins)


def _hgrn_lower_bounds(lb_logits):
    p = jax.nn.softmax(lb_logits, axis=1)
    return jnp.cumsum(p, axis=1) - p[:, :1]


def _s5_discretise(lam_re, lam_im, log_dt, b_re, b_im):
    lr = jnp.minimum(lam_re, -1e-4)
    li = lam_im
    dt = jnp.exp(log_dt)[:, None]
    mag = jnp.exp(lr * dt)
    ar, ai = mag * jnp.cos(li * dt), mag * jnp.sin(li * dt)
    den = lr * lr + li * li
    nr = ar - 1.0
    cr = (nr * lr + ai * li) / den
    ci = (ai * lr - nr * li) / den
    bbr = cr[..., None] * b_re - ci[..., None] * b_im
    bbi = cr[..., None] * b_im + ci[..., None] * b_re
    return ar, ai, bbr, bbi


def _block_diag(t):
    G, a, b = t.shape
    eye = jnp.eye(G, dtype=F32)
    return (t[:, :, None, :] * eye[:, None, :, None]).reshape(G * a, G * b)


def _rope_tables(S):
    half = RET_DK // 2
    inv = ROPE_BASE ** (-jnp.arange(half, dtype=F32) / half)
    ang = jnp.arange(S, dtype=F32)[:, None] * inv[None, :]
    cos, sin = jnp.cos(ang), jnp.sin(ang)
    return jnp.concatenate([cos, cos], axis=1), jnp.concatenate([-sin, sin], axis=1)


def _ret_log_decays():
    f = tuple(float(np.log1p(-np.exp2(np.float32(-5.0 - h)))) for h in range(RET_HEADS))
    b = tuple(float(np.log1p(-np.exp2(np.float32(-5.5 - h)))) for h in range(RET_HEADS))
    return f, b


def assemble_weight(name, land, own, me, col_sharded):
    _, R, C = land.shape
    if col_sharded:
        tr = min(R, 256)

        def body(me_ref, land_ref, own_ref, o_ref):
            for d in range(N_DEV):
                o_ref[:, d * C:(d + 1) * C] = jnp.where(me_ref[0] == d, own_ref[...], land_ref[d])

        grid, out_shape = (R // tr,), (R, N_DEV * C)
        in_specs = [pl.BlockSpec((N_DEV, tr, C), lambda i, p: (0, i, 0)), pl.BlockSpec((tr, C), lambda i, p: (i, 0))]
        out_spec = pl.BlockSpec((tr, N_DEV * C), lambda i, p: (i, 0))
    else:
        def body(me_ref, land_ref, own_ref, o_ref):
            o_ref[...] = jnp.where(me_ref[0] == pl.program_id(0), own_ref[...], land_ref[...])

        grid, out_shape = (N_DEV,), (N_DEV * R, C)
        in_specs = [pl.BlockSpec((None, R, C), lambda d, p: (d, 0, 0)), pl.BlockSpec((R, C), lambda d, p: (0, 0))]
        out_spec = pl.BlockSpec((R, C), lambda d, p: (d, 0))
    return pl.pallas_call(
        body, name=name,
        grid_spec=pltpu.PrefetchScalarGridSpec(num_scalar_prefetch=1, grid=grid, in_specs=in_specs, out_specs=out_spec),
        out_shape=jax.ShapeDtypeStruct(out_shape, land.dtype), compiler_params=_params("parallel"),
    )(me, land, own)


def _row(t):
    return t.reshape(1, -1)


def mixer_stage(layer, S, resid, branch, w_in, w_out, carriers, small):
    j = layer // 2
    tag = f"l{layer}"
    g = _row(small["mix_norm_g"][layer])
    if branch is None:
        x = resid
        (h,) = make_rowop(tag + "_norm", norm_f, (D_MODEL,))((x,), (g,), ())
    else:
        x, h = make_rowop(tag + "_addnorm", addnorm_f, (D_MODEL, D_MODEL))((resid, branch), (g,), ())
    if layer % 2 == 0:
        lbs = _hgrn_lower_bounds(small["hgrn_lb_logits"])
        prm = (small["gla_wa2"][j, 0], small["gla_wa2"][j, 1], _row(small["gla_ba"][j, 0]), _row(small["gla_ba"][j, 1]),
               _row(lbs[0, j]), _row(lbs[1, j]))
        outs = make_proj_stage(tag + "_prep", even_prep_f, EVEN_PREP_WIDTHS)(h, w_in, carriers["in32"], carriers["in16"], prm, ())
        gla_qkv, (glaf, glab, gr), hgrn_qkv, (hlaf, hlab, hg) = outs[0:6], outs[6:9], outs[9:15], outs[15:18]
        of, ob = make_scan(tag + "_gla", GLA_HEADS, GLA_DK, GLA_DV, S, SCAN_CHUNK)(gla_qkv, glaf, glab)
        hof, hob = make_scan(tag + "_hgrn", HGRN_HEADS, HGRN_DK, HGRN_DV, S, HGRN_SCAN_CHUNK)(hgrn_qkv, hlaf, hlab)
        (y,) = make_rowop(tag + "_post", even_post_f, (D_MODEL,))(
            (of, ob, hof, hob, gr, hg), (_row(small["gla_norm_g"][j]), _row(small["hgrn_norm_g"][j])), ())
    else:
        cosf, sinf = _rope_tables(S)
        outs = make_proj_stage(tag + "_prep", odd_prep_f, ODD_PREP_WIDTHS)(h, w_in, carriers["in32"], carriers["in16"], (), (cosf, sinf))
        ret_qkv, rg, su = outs[0:6], outs[6], outs[7]
        of, ob = make_scan(tag + "_ret", RET_HEADS, RET_DK, RET_DV, S, SCAN_CHUNK, const_lg=_ret_log_decays())(ret_qkv)
        (cm,) = make_rowop(tag + "_retpost", ret_post_f, (RET_HEADS * RET_DV,))((of, ob, rg), (_row(small["ret_norm_g"][j]),), ())
        disc = [_s5_discretise(small["s5_lam_re"][j, d], small["s5_lam_im"][j, d], small["s5_log_dt"][j, d],
                               small["s5_b_re"][j], small["s5_b_im"][j]) for d in range(2)]
        a4 = jnp.stack([t.reshape(-1) for d in range(2) for t in disc[d][:2]], axis=0)
        su_p = make_reorder(tag + "_s5seg", S, True)(su)
        bd = jnp.concatenate([_block_diag(jnp.swapaxes(disc[d][2 + i], 1, 2)) for d in range(2) for i in range(2)], axis=1)
        Xs = make_mm_groups(tag + "_s5in", 4)(su_p, bd)
        Hs = make_s5_scan(tag + "_s5scan", S)(Xs, a4)
        prm = (_block_diag(jnp.swapaxes(small["s5_c_re"][j], 1, 2)), _block_diag(jnp.swapaxes(small["s5_c_im"][j], 1, 2)),
               _row(small["s5_d"][j]), small["s5_glu_w"][j], _row(small["s5_glu_b"][j]))
        (dm_p,) = make_rowop(tag + "_s5post", s5_post_f, (S5_WIDTH,))((*Hs, su_p), prm, ())
        y = make_join(tag + "_join", S, RET_HEADS * RET_DV, S5_WIDTH)(cm, dm_p)
    return (x, *make_mm(tag + "_out")(y, w_out, carriers["out32"], carriers["out16"]))


def ffn_stage(layer, S, resid, branch, w_up, w_down, carriers, small):
    tag = f"l{layer}"
    x, hf = make_rowop(tag + "_ffnnorm", addnorm_f, (D_MODEL, D_MODEL))((resid, branch), (_row(small["ffn_norm_g"][layer]),), ())
    out, tick = make_ffn(tag + "_ffn", S)(hf, w_up, carriers["up32"], carriers["up16"], w_down, carriers["down32"], carriers["down16"],
                                         small["ffn_conv_w"][layer], _row(small["ffn_conv_b"][layer]))
    return x, out, tick


BIG = {"w_in_even": 2, "w_out_even": 1, "w_in_odd": 2, "w_out_odd": 1, "ffn_w_up": 2, "ffn_w_down": 1}
SMALL_SHARDED = {"gla_wa2": 3, "gla_ba": 2, "hgrn_lb_logits": 2, "ret_norm_g": 1, "s5_d": 1, "s5_glu_w": 1, "s5_glu_b": 1,
                 "ffn_conv_w": 2}
REPLICATED = ("mix_norm_g", "ffn_norm_g", "final_norm_g", "gla_norm_g", "hgrn_norm_g", "s5_lam_re", "s5_lam_im", "s5_log_dt",
              "s5_b_re", "s5_b_im", "s5_c_re", "s5_c_im", "ffn_conv_b")
WEIGHTS = ("mix_norm_g", "ffn_norm_g", "final_norm_g", "w_in_even", "w_out_even", "gla_wa2", "gla_ba", "gla_norm_g",
           "hgrn_lb_logits", "hgrn_norm_g", "w_in_odd", "w_out_odd", "ret_norm_g", "s5_lam_re", "s5_lam_im", "s5_log_dt",
           "s5_b_re", "s5_b_im", "s5_c_re", "s5_c_im", "s5_d", "s5_glu_w", "s5_glu_b", "ffn_w_up", "ffn_conv_w", "ffn_conv_b",
           "ffn_w_down")
PACK_COLS = LANES
MIXER_SMALL = (("mix_norm_g", "hgrn_lb_logits", "gla_wa2", "gla_ba", "gla_norm_g", "hgrn_norm_g"),
               ("mix_norm_g", "ret_norm_g", "s5_lam_re", "s5_lam_im", "s5_log_dt", "s5_b_re", "s5_b_im", "s5_c_re", "s5_c_im",
                "s5_d", "s5_glu_w", "s5_glu_b"))
FFN_SMALL = ("ffn_norm_g", "ffn_conv_w", "ffn_conv_b")


def _unshard(g, axis):
    t = jnp.moveaxis(g, 0, axis)
    return t.reshape(t.shape[:axis] + (t.shape[axis] * t.shape[axis + 1],) + t.shape[axis + 2:])


def _pack_rows(shape):
    return -(-int(np.prod(shape)) // (SUBLANES * PACK_COLS)) * SUBLANES


def _pack(arrs):
    parts = []
    for a in arrs:
        rows = _pack_rows(a.shape)
        parts.append(jnp.pad(a.reshape(-1), (0, rows * PACK_COLS - a.size)).reshape(rows, PACK_COLS))
    return jnp.concatenate(parts, axis=0)


def _unpack(packed, shapes):
    lead = packed.shape[:-2]
    out, r = [], 0
    for s in shapes:
        rows, n = _pack_rows(s), int(np.prod(s))
        piece = packed[..., r:r + rows, :].reshape(lead + (rows * PACK_COLS,))
        out.append(piece[..., :n].reshape(lead + tuple(s)))
        r += rows
    return out


def kernel(x, mix_norm_g, ffn_norm_g, final_norm_g, w_in_even, w_out_even, gla_wa2, gla_ba, gla_norm_g, hgrn_lb_logits, hgrn_norm_g, w_in_odd, w_out_odd, ret_norm_g, s5_lam_re, s5_lam_im, s5_log_dt, s5_b_re, s5_b_im, s5_c_re, s5_c_im, s5_d, s5_glu_w, s5_glu_b, ffn_w_up, ffn_conv_w, ffn_conv_b, ffn_w_down, loss_target, m_mix_norm_g, m_ffn_norm_g, m_final_norm_g, m_w_in_even, m_w_out_even, m_gla_wa2, m_gla_ba, m_gla_norm_g, m_hgrn_lb_logits, m_hgrn_norm_g, m_w_in_odd, m_w_out_odd, m_ret_norm_g, m_s5_lam_re, m_s5_lam_im, m_s5_log_dt, m_s5_b_re, m_s5_b_im, m_s5_c_re, m_s5_c_im, m_s5_d, m_s5_glu_w, m_s5_glu_b, m_ffn_w_up, m_ffn_conv_w, m_ffn_conv_b, m_ffn_w_down, v_mix_norm_g, v_ffn_norm_g, v_final_norm_g, v_w_in_even, v_w_out_even, v_gla_wa2, v_gla_ba, v_gla_norm_g, v_hgrn_lb_logits, v_hgrn_norm_g, v_w_in_odd, v_w_out_odd, v_ret_norm_g, v_s5_lam_re, v_s5_lam_im, v_s5_log_dt, v_s5_b_re, v_s5_b_im, v_s5_c_re, v_s5_c_im, v_s5_d, v_s5_glu_w, v_s5_glu_b, v_ffn_w_up, v_ffn_conv_w, v_ffn_conv_b, v_ffn_w_down):
    args = locals()
    w = {n: args[n] for n in WEIGHTS}
    m = {n: args["m_" + n] for n in WEIGHTS}
    v = {n: args["v_" + n] for n in WEIGHTS}
    Bl, S, D = x.shape
    T = Bl * S
    ix, iy, ic = lax.axis_index("x"), lax.axis_index("y"), lax.axis_index("c")
    me = 4 * ix + 2 * iy + ic

    xt = x.reshape(T, D)
    me1 = me.reshape(1).astype(jnp.int32)
    stages = []
    for layer in range(DEPTH):
        j = layer // 2
        kin, kout = ("w_in_even", "w_out_even") if layer % 2 == 0 else ("w_in_odd", "w_out_odd")
        stages.append((mixer_stage, layer, ("in", "out"), ((kin, j, True), (kout, j, False)), MIXER_SMALL[layer % 2]))
        stages.append((ffn_stage, layer, ("up", "down"), (("ffn_w_up", layer, True), ("ffn_w_down", layer, False)), FFN_SMALL))

    gather, after = [], xt
    for s, (_, _, _, projs, _) in enumerate(stages):
        handle = split_start(f"gather{s}_start", "gather", [w[n][l].astype(BF16) for n, l, _ in projs], None, after)
        gather.append(handle)
        after = handle[-1]
    sm_names = list(SMALL_SHARDED)
    (sm_all8,) = all_gather("gather_small", [_pack([w[n] for n in sm_names])])
    small = {n: _unshard(t, SMALL_SHARDED[n]) for n, t in zip(sm_names, _unpack(sm_all8, [w[n].shape for n in sm_names]))}
    small.update({n: w[n] for n in REPLICATED})
    small["mix_norm_g"] = small["mix_norm_g"] + after[0, 0]

    resid, branch, pulls = xt, None, []
    second = [None] * len(stages)

    def second_level(s, after):
        lands = split_wait(f"gather{s}_wait", "gather", gather[s], after)
        second[s] = split_start(f"forward{s}_start", "forward", [], lands, after)

    for s, (fn, layer, keys, projs, sm_keys) in enumerate(stages):
        here = lax.stop_gradient(resid)
        if second[s] is None:
            second_level(s, here)
        lands = split_wait(f"forward{s}_wait", "forward", second[s], second[s][-1])
        sm = {n: small[n] for n in sm_keys}
        if 2 <= s < len(stages) - 1:
            second_level(s + 1, here)
            norm = "mix_norm_g" if fn is mixer_stage else "ffn_norm_g"
            sm[norm] = sm[norm] + second[s + 1][-1][0, 0]
        full, carriers = [], {}
        for key, land, (n, l, col) in zip(keys, lands, projs, strict=True):
            full.append(assemble_weight(f"weight{s}_{key}", land, w[n][l].astype(BF16), me1, col))
            carriers[key + "32"] = jnp.zeros(land.shape, F32)
            carriers[key + "16"] = jnp.zeros(land.shape, BF16)
        run = functools.partial(fn, layer, S)
        if branch is None:
            (resid, branch, _), pull = jax.vjp(lambda r, c, p, run=run, full=full: run(r, None, full[0], full[1], c, p), resid, carriers, sm)
        else:
            (resid, branch, _), pull = jax.vjp(lambda r, b, c, p, run=run, full=full: run(r, b, full[0], full[1], c, p), resid, branch, carriers, sm)
        pulls.append(pull)

    loss_acc, dxf, dgf = loss_head(resid, branch, small["final_norm_g"].reshape(1, D), loss_target.reshape(T, D))
    loss = lax.psum(loss_acc[0, 0], ("x", "y", "c"))
    g_small = {"final_norm_g": dgf.reshape(D)}
    d_resid, d_branch, token = dxf, dxf, _tick()
    scatter, own32 = [None] * len(stages), [None] * len(stages)
    for s in reversed(range(len(stages))):
        if s == 0:
            d_resid, dcar, dsm = pulls[s]((d_resid, d_branch, token))
        else:
            d_resid, d_branch, dcar, dsm = pulls[s]((d_resid, d_branch, token))
        for n, g in dsm.items():
            g_small[n] = g_small[n] + g if n in g_small else g
        keys = stages[s][2]
        own32[s] = [dcar[k + "32"] for k in keys]
        scatter[s] = split_start(f"grads{s}_start", "scatter", [dcar[k + "16"] for k in keys], None, d_resid)
        token = scatter[s][-1]
    dx = d_resid

    out = {}
    sm_all = sm_names + list(REPLICATED)
    g_pack = _pack([g_small[n] for n in sm_all])
    small_first = split_start("small_grads_start", "gather", [g_pack], None, dx)
    chain = {}
    for s in reversed(range(1, len(stages))):
        lands = split_wait(f"grads{s}_wait", "scatter", scatter[s], small_first[-1] if s == len(stages) - 1 else dx)
        for own, land, (n, l, _) in zip(own32[s], lands, stages[s][3], strict=True):
            chain[n] = adamw_sharded(f"adamw_{n}_{l}", l, w[n], m[n], v[n], own, land, me1, chain.get(n))
            behind = chain[n][0]

    lands = split_wait("small_grads_wait", "gather", small_first, behind)
    small_second = split_start("small_forward_start", "forward", [], lands, behind)
    (g_all,) = split_wait("small_forward_wait", "forward", small_second, small_second[-1])
    g_sum = _unpack(ordered_sum("sum_small_grads", g_all, g_pack, me1), [g_small[n].shape for n in sm_all])
    g_loc = []
    for n, g in zip(sm_all, g_sum):
        if n in SMALL_SHARDED:
            ax = SMALL_SHARDED[n]
            size = w[n].shape[ax]
            g = lax.dynamic_slice_in_dim(g, me * size, size, axis=ax)
        g_loc.append(g)
    shapes = [w[n].shape for n in sm_all]
    res = adamw("adamw_small", _pack([w[n] for n in sm_all]), _pack([m[n] for n in sm_all]), _pack([v[n] for n in sm_all]), _pack(g_loc))
    behind = res[0]
    res = [g_loc] + [_unpack(r, shapes) for r in res]
    for k, n in enumerate(sm_all):
        out[n] = [r[k] for r in res]

    lands = split_wait("grads0_wait", "scatter", scatter[0], behind)
    for own, land, (n, l, _) in zip(own32[0], lands, stages[0][3], strict=True):
        chain[n] = adamw_sharded(f"adamw_{n}_{l}", l, w[n], m[n], v[n], own, land, me1, chain.get(n))
    for n in BIG:
        out[n] = [t.reshape(w[n].shape) for t in chain[n]]

    grads, deltas, new_m, new_v = ([out[n][k] for n in WEIGHTS] for k in range(4))
    return (loss, dx.reshape(Bl, S, D), *grads, *deltas, *new_m, *new_v)
```

```python
import functools

import numpy as np
import jax
import jax.numpy as jnp
from jax import lax
from jax.experimental import pallas as pl
from jax.experimental.pallas import tpu as pltpu

F32 = jnp.float32
BF16 = jnp.bfloat16
HIGHEST = lax.Precision.HIGHEST
MESH = pl.DeviceIdType.MESH

D_MODEL = 1024
DEPTH = 4
GLA_HEADS, GLA_DK, GLA_DV, GLA_GATE_NORM = 4, 64, 128, 16.0
HGRN_HEADS, HGRN_DK, HGRN_DV, HGRN_MIN_F = 4, 64, 128, 1e-20
RET_HEADS, RET_DK, RET_DV = 4, 128, 192
ROPE_BASE = 10000.0
S5_WIDTH, S5_GROUPS, S5_STATE = 256, 16, 64
S5_N = S5_GROUPS * S5_STATE
EPS = 1e-6
ADAM_LR, ADAM_B1, ADAM_B2, ADAM_EPS, ADAM_WD, ADAM_STEP = 0.001, 0.9, 0.999, 1e-08, 0.01, 10

N_DEV = 8
VMEM_LIMIT_BYTES = 56 * 1024 * 1024
ROW_TILE = 256
SCAN_CHUNK = 128
HGRN_SCAN_CHUNK = 64
S5_SEGMENTS = 8
LANES = 128


def _params(*sem):
    return pltpu.CompilerParams(dimension_semantics=sem, vmem_limit_bytes=VMEM_LIMIT_BYTES)


def _divisor_tile(n, cap):
    best = None
    for t in range(LANES, min(n, cap) + 1, LANES):
        if n % t == 0:
            best = t
    return best if best is not None else n


def _mm_nn(name, x, w):
    M, K = x.shape
    N = w.shape[1]
    tn = _divisor_tile(N, 3072) if K * N * 2 > 8 * 2**20 else N
    tm = 256 if tn * 4 * 512 > 6 * 2**20 else 512
    assert M % tm == 0 and N % tn == 0

    def body(x_ref, w_ref, o_ref):
        o_ref[...] = jnp.dot(x_ref[...].astype(BF16), w_ref[...], preferred_element_type=F32)

    return pl.pallas_call(
        body, name=name, grid=(N // tn, M // tm),
        in_specs=[pl.BlockSpec((tm, K), lambda j, i: (i, 0)), pl.BlockSpec((K, tn), lambda j, i: (0, j))],
        out_specs=pl.BlockSpec((tm, tn), lambda j, i: (i, j)),
        out_shape=jax.ShapeDtypeStruct((M, N), F32),
        compiler_params=_params("parallel", "parallel"),
    )(x, w)


def _mm_nt(name, dy, w, after=None):
    M, N = dy.shape
    K = w.shape[0]
    tk = _divisor_tile(K, 1024) if K * N * 2 > 8 * 2**20 else K
    tm = 256 if N >= 4096 else 512
    assert M % tm == 0 and K % tk == 0

    def body(dy_ref, w_ref, *rest):
        rest[-1][...] = lax.dot_general(dy_ref[...].astype(BF16), w_ref[...], (((1,), (1,)), ((), ())),
                                        preferred_element_type=F32)

    extra = [] if after is None else [after]
    return pl.pallas_call(
        body, name=name, grid=(K // tk, M // tm),
        in_specs=[pl.BlockSpec((tm, N), lambda j, i: (i, 0)), pl.BlockSpec((tk, N), lambda j, i: (j, 0))]
        + [pl.BlockSpec(memory_space=pl.ANY)] * len(extra),
        out_specs=pl.BlockSpec((tm, tk), lambda j, i: (i, j)),
        out_shape=jax.ShapeDtypeStruct((M, K), F32),
        compiler_params=_params("parallel", "parallel"),
    )(dy, w, *extra)


def _tick():
    return jnp.zeros((SUBLANES, LANES), F32)


MM_TN_VMEM_BUDGET = 46 * 2**20


def _pad_lanes(n):
    return -(-n // LANES) * LANES


def _mm_tn(name, x, dy, nblk, want16):
    M, K = x.shape
    N = dy.shape[1]
    n = N // nblk
    xb, yb = x.dtype.itemsize, dy.dtype.itemsize
    best = None
    for tk in [t for t in range(LANES, K + 1, LANES) if K % t == 0]:
        for tm in (512, 256):
            out_bytes = nblk * tk * _pad_lanes(n) * (6 if want16 else 4)
            vmem = 2 * out_bytes + 2 * tm * tk * xb + 2 * tm * _pad_lanes(N) * yb
            traffic = (K // tk) * M * N * yb + M * K * xb
            if vmem <= MM_TN_VMEM_BUDGET and M % tm == 0 and (best is None or (traffic, -tm) < best[0]):
                best = ((traffic, -tm), tk, tm)
    _, tk, tm = best
    last = M // tm - 1

    def body(x_ref, dy_ref, o32_ref, *o16_ref):
        m = pl.program_id(1)

        @pl.when(m == 0)
        def _():
            o32_ref[...] = jnp.zeros_like(o32_ref)

        dyv = dy_ref[...].astype(BF16)
        if nblk == 1:
            o32_ref[0] += lax.dot_general(x_ref[...].astype(BF16), dyv, (((0,), (0,)), ((), ())), preferred_element_type=F32)
        else:
            xt = x_ref[...].astype(F32).T.astype(BF16)
            for d in range(nblk):
                o32_ref[d] += jnp.dot(xt, dyv[:, d * n:(d + 1) * n], preferred_element_type=F32)
        if want16:
            @pl.when(m == last)
            def _():
                o16_ref[0][...] = o32_ref[...].astype(BF16)

    blk = pl.BlockSpec((nblk, tk, n), lambda a, m: (0, a, 0))
    return pl.pallas_call(
        body, name=name, grid=(K // tk, M // tm),
        in_specs=[pl.BlockSpec((tm, tk), lambda a, m: (m, a)), pl.BlockSpec((tm, N), lambda a, m: (m, 0))],
        out_specs=[blk, blk] if want16 else [blk],
        out_shape=[jax.ShapeDtypeStruct((nblk, K, n), F32)] + ([jax.ShapeDtypeStruct((nblk, K, n), BF16)] if want16 else []),
        compiler_params=_params("parallel", "arbitrary"),
    )(x, dy)


def _weight_grads(name, x, dy, col_sharded):
    if col_sharded:
        return _mm_tn(name, x, dy, N_DEV, True)
    d32, d16 = _mm_tn(name, x, dy, 1, True)
    K, N = d32.shape[1:]
    return d32.reshape(N_DEV, K // N_DEV, N), d16.reshape(N_DEV, K // N_DEV, N)


def make_mm(name, col_sharded=False):
    @jax.custom_vjp
    def mm(x, w16, c32, c16):
        return _mm_nn(name + "_fwd", x, w16), _tick()

    def fwd(x, w16, c32, c16):
        return (_mm_nn(name + "_fwd", x, w16), _tick()), (x, w16)

    def bwd(res, g):
        x, w16 = res
        dy, after = g
        dx = _mm_nt(name + "_dx", dy, w16, after)
        d32, d16 = _weight_grads(name + "_dw", x, dy, col_sharded)
        return dx, jnp.zeros_like(w16), d32, d16

    mm.defvjp(fwd, bwd)
    return mm


def make_mm_groups(name, G, tm=512):
    def products(x, w16):
        M, K = x.shape
        N = w16.shape[1] // G

        def body(x_ref, w_ref, *o_refs):
            xv = x_ref[...].astype(BF16)
            for g, o_ref in enumerate(o_refs):
                o_ref[...] = jnp.dot(xv, w_ref[:, g * N:(g + 1) * N], preferred_element_type=F32)

        row = pl.BlockSpec((tm, N), lambda i: (i, 0))
        return tuple(pl.pallas_call(
            body, name=name + "_fwd", grid=(M // tm,),
            in_specs=[pl.BlockSpec((tm, K), lambda i: (i, 0)), pl.BlockSpec((K, G * N), lambda i: (0, 0))],
            out_specs=[row] * G, out_shape=[jax.ShapeDtypeStruct((M, N), F32)] * G, compiler_params=_params("parallel"),
        )(x, w16))

    def grads(x, w16, dys):
        M, K = x.shape
        N = w16.shape[1] // G

        def body(x_ref, w_ref, *refs):
            dy_refs, dx_ref, dw_ref = refs[:G], refs[G], refs[G + 1]

            @pl.when(pl.program_id(0) == 0)
            def _():
                dw_ref[...] = jnp.zeros_like(dw_ref)

            xv = x_ref[...].astype(BF16)
            dx = jnp.zeros(dx_ref.shape, F32)
            for g in range(G):
                cols = slice(g * N, (g + 1) * N)
                dyv = dy_refs[g][...].astype(BF16)
                dx = dx + lax.dot_general(dyv, w_ref[:, cols], (((1,), (1,)), ((), ())), preferred_element_type=F32)
                dw_ref[:, cols] += lax.dot_general(xv, dyv, (((0,), (0,)), ((), ())), preferred_element_type=F32)
            dx_ref[...] = dx

        row = pl.BlockSpec((tm, N), lambda i: (i, 0))
        return pl.pallas_call(
            body, name=name + "_bwd", grid=(M // tm,),
            in_specs=[pl.BlockSpec((tm, K), lambda i: (i, 0)), pl.BlockSpec((K, G * N), lambda i: (0, 0))] + [row] * G,
            out_specs=[pl.BlockSpec((tm, K), lambda i: (i, 0)), pl.BlockSpec((K, G * N), lambda i: (0, 0))],
            out_shape=[jax.ShapeDtypeStruct((M, K), F32), jax.ShapeDtypeStruct((K, G * N), F32)],
            compiler_params=_params("arbitrary"),
        )(x, w16, *dys)

    @jax.custom_vjp
    def mm(x, w):
        return products(x, w.astype(BF16))

    def fwd(x, w):
        w16 = w.astype(BF16)
        return products(x, w16), (x, w16)

    def bwd(res, dys):
        x, w16 = res
        return tuple(grads(x, w16, tuple(dys)))

    mm.defvjp(fwd, bwd)
    return mm


def _row_specs(rows, params, consts, tile):
    specs = [pl.BlockSpec((tile, r.shape[1]), lambda i: (i, 0)) for r in rows]
    specs += [pl.BlockSpec(p.shape, lambda i: (0, 0)) for p in params]
    specs += [pl.BlockSpec((tile, c.shape[1]), lambda i, n=c.shape[0] // tile: (i % n, 0)) for c in consts]
    return specs


def _row_fwd(name, f, out_widths, tile, rows, params, consts):
    T = rows[0].shape[0]
    nr, npar, ncon = len(rows), len(params), len(consts)

    def body(*refs):
        r = tuple(x[...] for x in refs[:nr])
        p = tuple(x[...] for x in refs[nr:nr + npar])
        c = tuple(x[...] for x in refs[nr + npar:nr + npar + ncon])
        outs = f(r, p, c)
        for o_ref, o in zip(refs[nr + npar + ncon:], outs, strict=True):
            o_ref[...] = o

    return pl.pallas_call(
        body, name=name + "_fwd", grid=(T // tile,),
        in_specs=_row_specs(rows, params, consts, tile),
        out_specs=[pl.BlockSpec((tile, w), lambda i: (i, 0)) for w in out_widths],
        out_shape=[jax.ShapeDtypeStruct((T, w), F32) for w in out_widths],
        compiler_params=_params("parallel"),
    )(*rows, *params, *consts)


def _row_bwd(name, f, out_widths, tile, rows, params, consts, gouts, dr_dtype=F32):
    T = rows[0].shape[0]
    nr, npar, ncon, nout = len(rows), len(params), len(consts), len(out_widths)

    def body(*refs):
        r = tuple(x[...] for x in refs[:nr])
        p = tuple(x[...] for x in refs[nr:nr + npar])
        c = tuple(x[...] for x in refs[nr + npar:nr + npar + ncon])
        k = nr + npar + ncon
        g = tuple(x[...] for x in refs[k:k + nout])
        dr_refs = refs[k + nout:k + nout + nr]
        dp_refs = refs[k + nout + nr:]
        _, vjp = jax.vjp(lambda r_, p_: tuple(f(r_, p_, c)), r, p)
        dr, dp = vjp(g)
        for ref, val in zip(dr_refs, dr, strict=True):
            ref[...] = val.astype(ref.dtype)
        if npar:
            @pl.when(pl.program_id(0) == 0)
            def _():
                for ref in dp_refs:
                    ref[...] = jnp.zeros_like(ref)

            for ref, val in zip(dp_refs, dp, strict=True):
                ref[...] += val

    outs = pl.pallas_call(
        body, name=name + "_bwd", grid=(T // tile,),
        in_specs=_row_specs(rows, params, consts, tile) + [pl.BlockSpec((tile, w), lambda i: (i, 0)) for w in out_widths],
        out_specs=[pl.BlockSpec((tile, r.shape[1]), lambda i: (i, 0)) for r in rows]
        + [pl.BlockSpec(p.shape, lambda i: (0, 0)) for p in params],
        out_shape=[jax.ShapeDtypeStruct(r.shape, dr_dtype) for r in rows] + [jax.ShapeDtypeStruct(p.shape, F32) for p in params],
        compiler_params=_params("arbitrary"),
    )(*rows, *params, *consts, *gouts)
    return tuple(outs[:nr]), tuple(outs[nr:])


def make_proj_stage(name, f, out_widths, tile=ROW_TILE):
    def run(x, w16, params, consts):
        p = _mm_nn(name + "_mm", x, w16)
        return p, tuple(_row_fwd(name, f, out_widths, tile, (p,), params, consts))

    @jax.custom_vjp
    def op(x, w16, c32, c16, params, consts):
        return run(x, w16, params, consts)[1]

    def fwd(x, w16, c32, c16, params, consts):
        p, outs = run(x, w16, params, consts)
        return outs, (x, w16, p, params, consts)

    def bwd(res, g):
        x, w16, p, params, consts = res
        (dp,), dparams = _row_bwd(name, f, out_widths, tile, (p,), params, consts, tuple(g), dr_dtype=BF16)
        d32, d16 = _weight_grads(name + "_dw", x, dp, True)
        return _mm_nt(name + "_dx", dp, w16), jnp.zeros_like(w16), d32, d16, dparams, tuple(jnp.zeros_like(c) for c in consts)

    op.defvjp(fwd, bwd)
    return op


def make_rowop(name, f, out_widths, tile=ROW_TILE):
    @jax.custom_vjp
    def op(rows, params, consts):
        return tuple(_row_fwd(name, f, out_widths, tile, rows, params, consts))

    def fwd(rows, params, consts):
        return op(rows, params, consts), (rows, params, consts)

    def bwd(res, g):
        rows, params, consts = res
        dr, dp = _row_bwd(name, f, out_widths, tile, rows, params, consts, tuple(g))
        return dr, dp, tuple(jnp.zeros_like(c) for c in consts)

    op.defvjp(fwd, bwd)
    return op


def _rms(x, g):
    return x * lax.rsqrt(jnp.mean(x * x, axis=-1, keepdims=True) + EPS) * g


def _silu(x):
    return x * jax.nn.sigmoid(x)


def _bdot(a, b):
    return jnp.dot(a.astype(BF16), b.astype(BF16), preferred_element_type=F32)


def norm_f(rows, params, consts):
    return (_rms(rows[0], params[0]),)


def addnorm_f(rows, params, consts):
    x = rows[0] + rows[1]
    return x, _rms(x, params[0])


EVEN_GLA_END = 1568


def even_prep_f(rows, params, consts):
    (p,) = rows
    wa2f, wa2b, baf, bab, lbf, lbb = params
    gq = p[:, 0:256]
    gk = p[:, 256:512] * (GLA_DK ** -0.5)
    gv = p[:, 512:1024]
    gr = p[:, 1024:1536]
    glaf = jax.nn.log_sigmoid(_bdot(p[:, 1536:1552], wa2f) + baf) / GLA_GATE_NORM
    glab = jax.nn.log_sigmoid(_bdot(p[:, 1552:1568], wa2b) + bab) / GLA_GATE_NORM
    o = EVEN_GLA_END
    hq = _silu(p[:, o:o + 256])

    def gate(z, lb):
        f = lb + (1.0 - lb) * jax.nn.sigmoid(z)
        return jnp.log(jnp.maximum(f, HGRN_MIN_F)), (1.0 - lb) * jax.nn.sigmoid(-z)

    hlaf, hkf = gate(p[:, o + 256:o + 512], lbf)
    hlab, hkb = gate(p[:, o + 512:o + 768], lbb)
    hv = p[:, o + 768:o + 1280]
    hg = p[:, o + 1280:o + 1792]
    return gq, gk, gv, gq, gk, gv, glaf, glab, gr, hq, hkf, hv, hq, hkb, hv, hlaf, hlab, hg


EVEN_PREP_WIDTHS = (256, 256, 512, 256, 256, 512, 256, 256, 512, 256, 256, 512, 256, 256, 512, 256, 256, 512)


def _head_rms(o, g, heads, d):
    parts = []
    for h in range(heads):
        seg = o[:, h * d:(h + 1) * d]
        parts.append(seg * lax.rsqrt(jnp.mean(seg * seg, axis=-1, keepdims=True) + EPS))
    return jnp.concatenate(parts, axis=1) * g


def even_post_f(rows, params, consts):
    of, ob, hof, hob, gr, hg = rows
    a = _head_rms(of + ob, params[0], GLA_HEADS, GLA_DV) * _silu(gr)
    b = _head_rms(hof + hob, params[1], HGRN_HEADS, HGRN_DV) * _silu(hg)
    return (jnp.concatenate([a, b], axis=1),)


@jax.custom_vjp
def _roll_half(x):
    return pltpu.roll(x, RET_DK // 2, 1)


_roll_half.defvjp(lambda x: (_roll_half(x), None), lambda _, g: (_roll_half(g),))


def odd_prep_f(rows, params, consts):
    (p,) = rows
    cosf, sinf = consts

    def rot(t):
        parts = []
        for h in range(RET_HEADS):
            th = t[:, h * RET_DK:(h + 1) * RET_DK]
            parts.append(th * cosf + _roll_half(th) * sinf)
        return jnp.concatenate(parts, axis=1)

    rq = rot(p[:, 0:512])
    rk = rot(p[:, 512:1024]) * (RET_DK ** -0.5)
    rv = p[:, 1024:1792]
    return rq, rk, rv, rq, rk, rv, p[:, 1792:2560], p[:, 2560:2816]


ODD_PREP_WIDTHS = (512, 512, 768, 512, 512, 768, 768, 256)


def ret_post_f(rows, params, consts):
    of, ob, rg = rows
    o = of + ob
    parts = []
    for h in range(RET_HEADS):
        seg = o[:, h * RET_DV:(h + 1) * RET_DV]
        c = seg - jnp.mean(seg, axis=-1, keepdims=True)
        parts.append(c * lax.rsqrt(jnp.mean(c * c, axis=-1, keepdims=True) + EPS))
    return (jnp.concatenate(parts, axis=1) * params[0] * _silu(rg),)


def s5_post_f(rows, params, consts):
    h0r, h0i, h1r, h1i, u = rows
    c_re, c_im, d_skip, glu_w, glu_b = params
    hr = h0r + h1r
    hi = h0i + h1i
    y = _bdot(hr, c_re) - _bdot(hi, c_im) + d_skip * u
    g = jax.nn.gelu(y)
    return (g * jax.nn.sigmoid(_bdot(g, glu_w) + glu_b),)


def loss_head(x, r, g, target, tile=ROW_TILE):
    T, D = x.shape

    def body(x_ref, r_ref, g_ref, t_ref, loss_ref, dx_ref, dg_ref):
        t = t_ref[...]

        def lf(xv, gv):
            e = _rms(xv, gv) - t
            row = jnp.sum(e * e, axis=-1, keepdims=True) * (0.5 / D)
            return jnp.sum(row, axis=0, keepdims=True)

        l, vjp = jax.vjp(lf, x_ref[...] + r_ref[...], g_ref[...])
        dx, dg = vjp(jnp.ones((1, 1), F32))
        dx_ref[...] = dx

        @pl.when(pl.program_id(0) == 0)
        def _():
            loss_ref[...] = jnp.zeros_like(loss_ref)
            dg_ref[...] = jnp.zeros_like(dg_ref)

        loss_ref[...] += jnp.broadcast_to(l, loss_ref.shape)
        dg_ref[...] += dg

    row = pl.BlockSpec((tile, D), lambda i: (i, 0))
    vec = pl.BlockSpec((1, D), lambda i: (0, 0))
    return pl.pallas_call(
        body, name="loss_head", grid=(T // tile,),
        in_specs=[row, row, vec, row],
        out_specs=[pl.BlockSpec((1, LANES), lambda i: (0, 0)), row, vec],
        out_shape=[jax.ShapeDtypeStruct((1, LANES), F32), jax.ShapeDtypeStruct((T, D), F32), jax.ShapeDtypeStruct((1, D), F32)],
        compiler_params=_params("arbitrary"),
    )(x, r, g, target)


SUBLANES = 8


def _halo_specs(width, tile, T):
    n8 = tile // SUBLANES
    last = T // SUBLANES - 1
    return [pl.BlockSpec((tile, width), lambda i: (i, 0)),
            pl.BlockSpec((SUBLANES, width), lambda i: (jnp.maximum(i * n8 - 1, 0), 0)),
            pl.BlockSpec((SUBLANES, width), lambda i: (jnp.minimum((i + 1) * n8, last), 0))]


def _shift_rows(x, prev_row, next_row, tile):
    row = lax.broadcasted_iota(jnp.int32, (tile, 1), 0)
    down = jnp.where(row == 0, prev_row, pltpu.roll(x, 1, 0))
    up = jnp.where(row == tile - 1, next_row, pltpu.roll(x, tile - 1, 0))
    return down, up


def _conv_fwd(name, u, cw, cb, S, tile):
    T, F2 = u.shape
    F = F2 // 2
    per_seq = S // tile

    def body(u_ref, up_ref, un_ref, cw_ref, cb_ref, g_ref):
        pos = pl.program_id(0) % per_seq
        uv = u_ref[...]
        prev_row = jnp.where(pos == 0, 0.0, up_ref[SUBLANES - 1:SUBLANES, :])
        next_row = jnp.where(pos == per_seq - 1, 0.0, un_ref[0:1, :])
        down, up = _shift_rows(uv, prev_row, next_row, tile)
        c = _conv_taps(down, uv, up, cw_ref, cb_ref)
        g_ref[...] = (_silu(c[:, :F]) * c[:, F:]).astype(BF16)

    return pl.pallas_call(
        body, name=name + "_fwd", grid=(T // tile,),
        in_specs=_halo_specs(F2, tile, T) + [pl.BlockSpec((3, F2), lambda i: (0, 0)), pl.BlockSpec((1, F2), lambda i: (0, 0))],
        out_specs=pl.BlockSpec((tile, F), lambda i: (i, 0)),
        out_shape=jax.ShapeDtypeStruct((T, F), BF16),
        compiler_params=_params("parallel"),
    )(u, u, u, cw, cb)


def _conv_taps(down, mid, up, cw_ref, cb_ref):
    c = cb_ref[...] + down * cw_ref[0:1, :]
    c = c + mid * cw_ref[1:2, :]
    return c + up * cw_ref[2:3, :]


def _conv_bwd(name, u, dg, cw, cb, S, tile):
    T, F2 = u.shape
    F = F2 // 2
    per_seq = S // tile

    def dact(cv, dgv):
        a, v = cv[:, :F], cv[:, F:]
        sg = jax.nn.sigmoid(a)
        return jnp.concatenate([dgv * v * (sg * (1.0 + a * (1.0 - sg))), dgv * (a * sg)], axis=1)

    def body(u_ref, up_ref, un_ref, g_ref, gp_ref, gn_ref, cw_ref, cb_ref, du_ref, dw0_ref, dw1_ref, dw2_ref, db_ref):
        i = pl.program_id(0)
        pos = i % per_seq
        first, last = pos == 0, pos == per_seq - 1
        lo, hi = slice(SUBLANES - 1, SUBLANES), slice(0, 1)
        uv = u_ref[...]
        u_m1, u_p1 = jnp.where(first, 0.0, up_ref[lo, :]), jnp.where(last, 0.0, un_ref[hi, :])
        u_dn, u_up = _shift_rows(uv, u_m1, u_p1, tile)
        dc = dact(_conv_taps(u_dn, uv, u_up, cw_ref, cb_ref), g_ref[...])
        c_m1 = _conv_taps(up_ref[SUBLANES - 2:SUBLANES - 1, :], u_m1, u_ref[0:1, :], cw_ref, cb_ref)
        c_p1 = _conv_taps(u_ref[tile - 1:tile, :], u_p1, un_ref[1:2, :], cw_ref, cb_ref)
        dc_prev = jnp.where(first, 0.0, dact(c_m1, gp_ref[lo, :]))
        dc_next = jnp.where(last, 0.0, dact(c_p1, gn_ref[hi, :]))
        dc_dn, dc_up = _shift_rows(dc, dc_prev, dc_next, tile)
        du = dc_up * cw_ref[0:1, :]
        du = du + dc * cw_ref[1:2, :]
        du_ref[...] = (du + dc_dn * cw_ref[2:3, :]).astype(BF16)

        @pl.when(i == 0)
        def _():
            for ref in (dw0_ref, dw1_ref, dw2_ref, db_ref):
                ref[...] = jnp.zeros_like(ref)

        dw0_ref[...] += jnp.sum(dc * u_dn, axis=0, keepdims=True)
        dw1_ref[...] += jnp.sum(dc * uv, axis=0, keepdims=True)
        dw2_ref[...] += jnp.sum(dc * u_up, axis=0, keepdims=True)
        db_ref[...] += jnp.sum(dc, axis=0, keepdims=True)

    vec = pl.BlockSpec((1, F2), lambda i: (0, 0))
    du, dw0, dw1, dw2, db = pl.pallas_call(
        body, name=name + "_bwd", grid=(T // tile,),
        in_specs=_halo_specs(F2, tile, T) + _halo_specs(F, tile, T) + [pl.BlockSpec((3, F2), lambda i: (0, 0)), vec],
        out_specs=[pl.BlockSpec((tile, F2), lambda i: (i, 0)), vec, vec, vec, vec],
        out_shape=[jax.ShapeDtypeStruct((T, F2), BF16)] + [jax.ShapeDtypeStruct((1, F2), F32)] * 4,
        compiler_params=_params("arbitrary"),
    )(u, u, u, dg, dg, dg, cw, cb)
    return du, jnp.concatenate([dw0, dw1, dw2], axis=0), db


def make_ffn(name, S):
    def run(x, wup16, wdn16, cw, cb):
        u = _mm_nn(name + "_up", x, wup16)
        g16 = _conv_fwd(name + "_conv", u, cw, cb, S, ROW_TILE)
        return u, g16, _mm_nn(name + "_down", g16, wdn16)

    @jax.custom_vjp
    def op(x, wup16, cu32, cu16, wdn16, cd32, cd16, cw, cb):
        return run(x, wup16, wdn16, cw, cb)[2], _tick()

    def fwd(x, wup16, cu32, cu16, wdn16, cd32, cd16, cw, cb):
        u, g16, out = run(x, wup16, wdn16, cw, cb)
        return (out, _tick()), (x, wup16, wdn16, u, g16, cw, cb)

    def bwd(res, g):
        x, wup16, wdn16, u, g16, cw, cb = res
        dout, after = g
        dg = _mm_nt(name + "_down_dx", dout, wdn16, after)
        dd32, dd16 = _weight_grads(name + "_down_dw", g16, dout, False)
        du16, dcw, dcb = _conv_bwd(name + "_conv", u, dg, cw, cb, S, ROW_TILE)
        du32, du16w = _weight_grads(name + "_up_dw", x, du16, True)
        return (_mm_nt(name + "_up_dx", du16, wup16), jnp.zeros_like(wup16), du32, du16w, jnp.zeros_like(wdn16), dd32, dd16, dcw, dcb)

    op.defvjp(fwd, bwd)
    return op


def _dot_nt(a, b):
    return lax.dot_general(a.astype(BF16), b.astype(BF16), (((1,), (1,)), ((), ())), preferred_element_type=F32)


def _dot_tn(a, b):
    return lax.dot_general(a.astype(BF16), b.astype(BF16), (((0,), (0,)), ((), ())), preferred_element_type=F32)


def _chunk_decays(la, direction, C, width, dk, const_lg):
    row = lax.broadcasted_iota(jnp.int32, (C, C), 0)
    col = lax.broadcasted_iota(jnp.int32, (C, C), 1)
    keep = (row >= col) if direction == 0 else (row <= col)
    ridx = lax.broadcasted_iota(jnp.int32, (C, 1), 0)
    if const_lg is None:
        cum = jnp.dot(keep.astype(F32), la, precision=HIGHEST, preferred_element_type=F32)
    else:
        lane_head = lax.broadcasted_iota(jnp.int32, (1, width), 1) // dk
        lg = jnp.zeros((1, width), F32)
        for h, val in enumerate(const_lg):
            lg = jnp.where(lane_head == h, val, lg)
        steps = (ridx + 1) if direction == 0 else (C - ridx)
        cum = steps.astype(F32) * lg
    exit_row = C - 1 if direction == 0 else 0
    mid = jnp.sum(jnp.where(ridx == C // 2, cum, 0.0), axis=0, keepdims=True)
    last = jnp.sum(jnp.where(ridx == exit_row, cum, 0.0), axis=0, keepdims=True)
    return keep, ridx == exit_row, cum, mid, last


def _scan_fwd(name, qkv, laf, lab, H, dk, dv, S, C, const_lg):
    qf, kf, vf, qb, kb, vb = qkv
    T = qf.shape[0]
    B, nc = T // S, S // C
    Wk, Wv = H * dk, H * dv
    learn = const_lg is None

    def body(*refs):
        @pl.when(pl.program_id(0) == 0)
        def _():
            refs[-1][...] = jnp.zeros_like(refs[-1])

        for b in range(B):
            one_sequence(*[r.at[b] for r in refs])

    def one_sequence(*refs):
        if learn:
            qf_r, qb_r, kf_r, kb_r, vf_r, vb_r, laf_r, lab_r, of_r, ob_r, sf_r, sb_r, st = refs
            las = (laf_r[...], lab_r[...])
        else:
            qf_r, qb_r, kf_r, kb_r, vf_r, vb_r, of_r, ob_r, sf_r, sb_r, st = refs
            las = (None, None)

        for d, (q_r, k_r, v_r, o_r, s_r) in enumerate(((qf_r, kf_r, vf_r, of_r, sf_r), (qb_r, kb_r, vb_r, ob_r, sb_r))):
            keep, _, cum, mid, last = _chunk_decays(las[d], d, C, Wk, dk, None if learn else const_lg[d])
            qe = q_r[...] * jnp.exp(cum - mid)
            ke = k_r[...] * jnp.exp(mid - cum)
            q_in = qe * jnp.exp(mid)
            k_out = ke * jnp.exp(last - mid)
            e_last = jnp.exp(last)
            vv = v_r[...]
            for h in range(H):
                ks, vs = slice(h * dk, (h + 1) * dk), slice(h * dv, (h + 1) * dv)
                a = jnp.where(keep, _dot_nt(qe[:, ks], ke[:, ks]), 0.0)
                state = st[d, h]
                o_r[:, vs] = _bdot(a, vv[:, vs]) + _dot_nt(q_in[:, ks], state)
                s_r[h * dv:(h + 1) * dv, :] = state
                st[d, h] = state * e_last[:, ks] + _dot_tn(vv[:, vs], k_out[:, ks])

    fpos = lambda c: c
    bpos = lambda c: nc - 1 - c
    kspec = lambda pos: pl.BlockSpec((B, C, Wk), lambda c: (0, pos(c), 0))
    vspec = lambda pos: pl.BlockSpec((B, C, Wv), lambda c: (0, pos(c), 0))
    sspec = lambda pos: pl.BlockSpec((B, None, Wv, dk), lambda c: (0, pos(c), 0, 0))
    seq = lambda t: t.reshape(B, S, t.shape[1])
    ins = [seq(t) for t in [qf, qb, kf, kb, vf, vb] + ([laf, lab] if learn else [])]
    in_specs = [kspec(fpos), kspec(bpos), kspec(fpos), kspec(bpos), vspec(fpos), vspec(bpos)] + ([kspec(fpos), kspec(bpos)] if learn else [])
    of, ob, sf, sb = pl.pallas_call(
        body, name=name + "_fwd", grid=(nc,), in_specs=in_specs,
        out_specs=[vspec(fpos), vspec(bpos), sspec(fpos), sspec(bpos)],
        out_shape=[jax.ShapeDtypeStruct((B, S, Wv), F32)] * 2 + [jax.ShapeDtypeStruct((B, nc, Wv, dk), F32)] * 2,
        scratch_shapes=[pltpu.VMEM((B, 2, H, dv, dk), F32)],
        compiler_params=_params("arbitrary"),
    )(*ins)
    return of.reshape(T, Wv), ob.reshape(T, Wv), sf, sb


def _scan_bwd(name, qkv, laf, lab, sf, sb, dof, dob, H, dk, dv, S, C, const_lg):
    qf, kf, vf, qb, kb, vb = qkv
    T = qf.shape[0]
    B, nc = T // S, S // C
    Wk, Wv = H * dk, H * dv
    learn = const_lg is None

    def body(*refs):
        @pl.when(pl.program_id(0) == 0)
        def _():
            refs[-1][...] = jnp.zeros_like(refs[-1])

        for b in range(B):
            one_sequence(*[r.at[b] for r in refs])

    def one_sequence(*refs):
        if learn:
            (qf_r, qb_r, kf_r, kb_r, vf_r, vb_r, laf_r, lab_r, sf_r, sb_r, dof_r, dob_r,
             dqf_r, dqb_r, dkf_r, dkb_r, dvf_r, dvb_r, dlaf_r, dlab_r, dst) = refs
            las, dlas = (laf_r[...], lab_r[...]), (dlaf_r, dlab_r)
        else:
            (qf_r, qb_r, kf_r, kb_r, vf_r, vb_r, sf_r, sb_r, dof_r, dob_r,
             dqf_r, dqb_r, dkf_r, dkb_r, dvf_r, dvb_r, dst) = refs
            las, dlas = (None, None), (None, None)

        groups = ((qf_r, kf_r, vf_r, sf_r, dof_r, dqf_r, dkf_r, dvf_r), (qb_r, kb_r, vb_r, sb_r, dob_r, dqb_r, dkb_r, dvb_r))
        for d, (q_r, k_r, v_r, s_r, do_r, dq_r, dk_r, dv_r) in enumerate(groups):
            keep, is_exit, cum, mid, last = _chunk_decays(las[d], d, C, Wk, dk, None if learn else const_lg[d])
            eq, ek = jnp.exp(cum - mid), jnp.exp(mid - cum)
            e_in, e_out, e_last = jnp.exp(mid), jnp.exp(last - mid), jnp.exp(last)
            qe, ke = q_r[...] * eq, k_r[...] * ek
            q_in, k_out = qe * e_in, ke * e_out
            vv, do = v_r[...], do_r[...]
            dqe_parts, dke_parts, dlast_parts = [], [], []
            for h in range(H):
                ks, vs = slice(h * dk, (h + 1) * dk), slice(h * dv, (h + 1) * dv)
                a = jnp.where(keep, _dot_nt(qe[:, ks], ke[:, ks]), 0.0)
                dp = jnp.where(keep, _dot_nt(do[:, vs], vv[:, vs]), 0.0)
                s_prev = s_r[h * dv:(h + 1) * dv, :]
                ds = dst[d, h]
                dk_out = _bdot(vv[:, vs], ds)
                dqe_parts.append(_bdot(dp, ke[:, ks]) + _bdot(do[:, vs], s_prev) * e_in[:, ks])
                dke_parts.append(_dot_tn(dp, qe[:, ks]) + dk_out * e_out[:, ks])
                dv_r[:, vs] = _dot_tn(a, do[:, vs]) + _dot_nt(k_out[:, ks], ds)
                if learn:
                    dlast_parts.append(jnp.sum(dk_out * k_out[:, ks], axis=0, keepdims=True)
                                       + jnp.sum(ds * s_prev, axis=0, keepdims=True) * e_last[:, ks])
                dst[d, h] = ds * e_last[:, ks] + _dot_tn(do[:, vs], q_in[:, ks])
            dqe = jnp.concatenate(dqe_parts, axis=1)
            dke = jnp.concatenate(dke_parts, axis=1)
            dq_r[...] = dqe * eq
            dk_r[...] = dke * ek
            if learn:
                dcum = dqe * qe - dke * ke + jnp.where(is_exit, jnp.concatenate(dlast_parts, axis=1), 0.0)
                dlas[d][...] = lax.dot_general(keep.astype(F32), dcum, (((0,), (0,)), ((), ())), precision=HIGHEST,
                                               preferred_element_type=F32)

    fpos = lambda c: nc - 1 - c
    bpos = lambda c: c
    kspec = lambda pos: pl.BlockSpec((B, C, Wk), lambda c: (0, pos(c), 0))
    vspec = lambda pos: pl.BlockSpec((B, C, Wv), lambda c: (0, pos(c), 0))
    sspec = lambda pos: pl.BlockSpec((B, None, Wv, dk), lambda c: (0, pos(c), 0, 0))
    seq = lambda t: t.reshape(B, S, t.shape[1])
    ins = [seq(t) for t in [qf, qb, kf, kb, vf, vb] + ([laf, lab] if learn else [])] + [sf, sb, seq(dof), seq(dob)]
    in_specs = ([kspec(fpos), kspec(bpos), kspec(fpos), kspec(bpos), vspec(fpos), vspec(bpos)]
                + ([kspec(fpos), kspec(bpos)] if learn else []) + [sspec(fpos), sspec(bpos), vspec(fpos), vspec(bpos)])
    out_specs = [kspec(fpos), kspec(bpos), kspec(fpos), kspec(bpos), vspec(fpos), vspec(bpos)] + ([kspec(fpos), kspec(bpos)] if learn else [])
    out_shape = ([jax.ShapeDtypeStruct((B, S, Wk), F32)] * 4 + [jax.ShapeDtypeStruct((B, S, Wv), F32)] * 2
                 + ([jax.ShapeDtypeStruct((B, S, Wk), F32)] * 2 if learn else []))
    outs = pl.pallas_call(
        body, name=name + "_bwd", grid=(nc,), in_specs=in_specs, out_specs=out_specs, out_shape=out_shape,
        scratch_shapes=[pltpu.VMEM((B, 2, H, dv, dk), F32)],
        compiler_params=_params("arbitrary"),
    )(*ins)
    return [t.reshape(T, t.shape[2]) for t in outs]


def make_scan(name, H, dk, dv, S, C, const_lg=None):
    if const_lg is None:
        @jax.custom_vjp
        def op(qkv, laf, lab):
            return tuple(_scan_fwd(name, qkv, laf, lab, H, dk, dv, S, C, None)[:2])

        def fwd(qkv, laf, lab):
            of, ob, sf, sb = _scan_fwd(name, qkv, laf, lab, H, dk, dv, S, C, None)
            return (of, ob), (qkv, laf, lab, sf, sb)

        def bwd(res, g):
            qkv, laf, lab, sf, sb = res
            dqf, dqb, dkf, dkb, dvf, dvb, dlaf, dlab = _scan_bwd(name, qkv, laf, lab, sf, sb, g[0], g[1], H, dk, dv, S, C, None)
            return (dqf, dkf, dvf, dqb, dkb, dvb), dlaf, dlab
    else:
        @jax.custom_vjp
        def op(qkv):
            return tuple(_scan_fwd(name, qkv, None, None, H, dk, dv, S, C, const_lg)[:2])

        def fwd(qkv):
            of, ob, sf, sb = _scan_fwd(name, qkv, None, None, H, dk, dv, S, C, const_lg)
            return (of, ob), (qkv, sf, sb)

        def bwd(res, g):
            qkv, sf, sb = res
            dqf, dqb, dkf, dkb, dvf, dvb = _scan_bwd(name, qkv, None, None, sf, sb, g[0], g[1], H, dk, dv, S, C, const_lg)
            return ((dqf, dkf, dvf, dqb, dkb, dvb),)

    op.defvjp(fwd, bwd)
    return op


def _reorder_call(name, t, S, to_segments):
    T, w = t.shape
    n_it = S // S5_SEGMENTS

    def body(x_ref, o_ref):
        def step(i, carry):
            packed = pl.ds(pl.multiple_of(i * S5_SEGMENTS, S5_SEGMENTS), S5_SEGMENTS)
            spread = pl.ds(i, S5_SEGMENTS, stride=n_it)
            if to_segments:
                o_ref[packed, :] = x_ref[spread, :]
            else:
                o_ref[spread, :] = x_ref[packed, :]
            return carry

        lax.fori_loop(0, n_it, step, 0, unroll=8)

    blk = pl.BlockSpec((S, LANES), lambda b, j: (b, j))
    return pl.pallas_call(body, name=name, grid=(T // S, w // LANES), in_specs=[blk], out_specs=blk,
                          out_shape=jax.ShapeDtypeStruct(t.shape, t.dtype), compiler_params=_params("parallel", "parallel"))(t)


def make_join(name, S, wa, wb):
    n_it = S // S5_SEGMENTS

    def call(tag, forward, wa, wb, arrs):
        T = arrs[0].shape[0]
        na, nb = wa // LANES, wb // LANES

        def body(*refs):
            j = pl.program_id(1)
            a_ref, b_ref, y_ref = (refs[0], refs[1], refs[2]) if forward else (refs[1], refs[2], refs[0])

            @pl.when(j < na)
            def _():
                if forward:
                    y_ref[...] = a_ref[...]
                else:
                    a_ref[...] = y_ref[...]

            @pl.when(j >= na)
            def _():
                def step(i, carry):
                    packed = pl.ds(pl.multiple_of(i * S5_SEGMENTS, S5_SEGMENTS), S5_SEGMENTS)
                    spread = pl.ds(i, S5_SEGMENTS, stride=n_it)
                    if forward:
                        y_ref[spread, :] = b_ref[packed, :]
                    else:
                        b_ref[packed, :] = y_ref[spread, :]
                    return carry

                lax.fori_loop(0, n_it, step, 0, unroll=8)

        a_spec = pl.BlockSpec((S, LANES), lambda b, j: (b, jnp.minimum(j, na - 1)))
        b_spec = pl.BlockSpec((S, LANES), lambda b, j: (b, jnp.maximum(j - na, 0)))
        y_spec = pl.BlockSpec((S, LANES), lambda b, j: (b, j))
        shapes = [jax.ShapeDtypeStruct((T, wa), F32), jax.ShapeDtypeStruct((T, wb), F32), jax.ShapeDtypeStruct((T, wa + wb), F32)]
        return pl.pallas_call(
            body, name=name + tag, grid=(T // S, na + nb),
            in_specs=[a_spec, b_spec] if forward else [y_spec], out_specs=y_spec if forward else [a_spec, b_spec],
            out_shape=shapes[2] if forward else shapes[:2], compiler_params=_params("parallel", "arbitrary"),
        )(*arrs)

    @jax.custom_vjp
    def op(a, b):
        return call("_fwd", True, wa, wb, (a, b))

    op.defvjp(lambda a, b: (call("_fwd", True, wa, wb, (a, b)), None), lambda _, dy: tuple(call("_bwd", False, wa, wb, (dy,))))
    return op


def make_reorder(name, S, to_segments):
    @jax.custom_vjp
    def op(t):
        return _reorder_call(name, t, S, to_segments)

    op.defvjp(lambda t: (_reorder_call(name, t, S, to_segments), None),
              lambda _, g: (_reorder_call(name + "_bwd", g, S, not to_segments),))
    return op


def _s5_scan_call(name, Xs, A, S, dirs, prev=None):
    with_p = prev is not None
    T, N = Xs[0].shape
    B, nl = T // S, N // LANES
    n_it = S // S5_SEGMENTS
    assert n_it & (n_it - 1) == 0

    def cmul(ar, ai, br, bi):
        return ar * br - ai * bi, ar * bi + ai * br

    def body(*refs):
        x, a_ref = refs[0:4], refs[4]
        if with_p:
            h_prev, x_prev, h, p_ref = refs[5:9], refs[9:13], refs[13:17], refs[17]
        else:
            h = refs[5:9]
        seg = lax.broadcasted_iota(jnp.int32, (S5_SEGMENTS, 1), 0)
        zero = jnp.zeros((S5_SEGMENTS, LANES), F32)
        a = [(jnp.broadcast_to(a_ref[2 * k:2 * k + 1, :], (S5_SEGMENTS, LANES)),
              jnp.broadcast_to(a_ref[2 * k + 1:2 * k + 2, :], (S5_SEGMENTS, LANES))) for k in range(2)]

        def rows_of(k, i):
            return pl.ds(pl.multiple_of(((n_it - 1 - i) if dirs[k] == 1 else i) * S5_SEGMENTS, S5_SEGMENTS), S5_SEGMENTS)

        def local(i, carry):
            out = []
            for k, (sr, si) in enumerate(carry):
                rows = rows_of(k, i)
                pr, pi = cmul(*a[k], sr, si)
                sr, si = pr + x[2 * k][rows, :], pi + x[2 * k + 1][rows, :]
                h[2 * k][rows, :] = sr
                h[2 * k + 1][rows, :] = si
                out.append((sr, si))
            return tuple(out)

        ends = lax.fori_loop(0, n_it, local, ((zero, zero), (zero, zero)), unroll=8)
        inherit = []
        for k, (er, ei) in enumerate(ends):
            back = dirs[k] == 1
            pr, pi = a[k]
            for _ in range(n_it.bit_length() - 1):
                pr, pi = cmul(pr, pi, pr, pi)
            shift = (S5_SEGMENTS - 1) if back else 1
            tr, ti = er, ei
            for r in (range(S5_SEGMENTS - 2, -1, -1) if back else range(1, S5_SEGMENTS)):
                nr, ni = cmul(pr, pi, pltpu.roll(tr, shift, 0), pltpu.roll(ti, shift, 0))
                tr = jnp.where(seg == r, er + nr, tr)
                ti = jnp.where(seg == r, ei + ni, ti)
            edge = (S5_SEGMENTS - 1) if back else 0
            inherit.append((jnp.where(seg == edge, 0.0, pltpu.roll(tr, shift, 0)), jnp.where(seg == edge, 0.0, pltpu.roll(ti, shift, 0))))

        def fix(i, carry):
            powers, sums = carry
            new_powers, new_sums = [], []
            for k in range(2):
                rows = rows_of(k, i)
                fr, fi = cmul(*powers[k], *inherit[k])
                sr, si = h[2 * k][rows, :] + fr, h[2 * k + 1][rows, :] + fi
                h[2 * k][rows, :] = sr
                h[2 * k + 1][rows, :] = si
                new_powers.append(cmul(*powers[k], *a[k]))
                if with_p:
                    ur = h_prev[2 * k][rows, :] - x_prev[2 * k][rows, :]
                    ui = h_prev[2 * k + 1][rows, :] - x_prev[2 * k + 1][rows, :]
                    new_sums.append((sums[k][0] + sr * ur + si * ui, sums[k][1] + si * ur - sr * ui))
            return tuple(new_powers), tuple(new_sums)

        _, sums = lax.fori_loop(0, n_it, fix, ((a[0], a[1]), ((zero, zero), (zero, zero)) if with_p else ()), unroll=8)
        if with_p:
            for k in range(2):
                p_ref[2 * k:2 * k + 1, :] = jnp.sum(sums[k][0], axis=0, keepdims=True)
                p_ref[2 * k + 1:2 * k + 2, :] = jnp.sum(sums[k][1], axis=0, keepdims=True)

    col = pl.BlockSpec((S, LANES), lambda b, j: (b, j))
    outs = pl.pallas_call(
        body, name=name, grid=(B, nl),
        in_specs=[col] * 4 + [pl.BlockSpec((4, LANES), lambda b, j: (0, j))] + ([col] * 8 if with_p else []),
        out_specs=[col] * 4 + ([pl.BlockSpec((None, 4, LANES), lambda b, j: (b, 0, j))] if with_p else []),
        out_shape=[jax.ShapeDtypeStruct((T, N), F32)] * 4 + ([jax.ShapeDtypeStruct((B, 4, N), F32)] if with_p else []),
        compiler_params=_params("parallel", "parallel"),
    )(*Xs, A, *(prev[0] + prev[1] if with_p else ()))
    return (tuple(outs[:4]), outs[4]) if with_p else tuple(outs)


def make_s5_scan(name, S):
    @jax.custom_vjp
    def op(X, A):
        return _s5_scan_call(name + "_fwd", X, A, S, (0, 1))

    def fwd(X, A):
        H = _s5_scan_call(name + "_fwd", X, A, S, (0, 1))
        return H, (X, A, H)

    def bwd(res, G):
        X, A, H = res
        conj = A * jnp.array([[1.0], [-1.0], [1.0], [-1.0]], F32)
        lam, P = _s5_scan_call(name + "_bwd", tuple(G), conj, S, (1, 0), prev=(tuple(H), tuple(X)))
        P = jnp.sum(P, axis=0)
        ar, ai = A[0::2], A[1::2]
        pr, pi = P[0::2], P[1::2]
        den = ar * ar + ai * ai
        dar, dai = (pr * ar - pi * ai) / den, (pr * ai + pi * ar) / den
        return lam, jnp.stack([dar[0], dai[0], dar[1], dai[1]], axis=0)

    op.defvjp(fwd, bwd)
    return op


ANY = pl.BlockSpec(memory_space=pl.ANY)


def _place():
    x, y, c = lax.axis_index("x"), lax.axis_index("y"), lax.axis_index("c")
    return x, y, c, [(1 - x, y), (x, 1 - y), (1 - x, 1 - y)]


def all_gather(name, arrs):
    n = len(arrs)

    def body(*refs):
        ins, outs = refs[:n], refs[n:2 * n]
        send, recv, lsem = refs[2 * n:]
        x, y, c, chips = _place()
        me, sibling = (x, y, c), (x, y, 1 - c)

        def copy(a, k, block, to, src=None):
            slot = outs[a].at[4 * block[0] + 2 * block[1] + block[2]]
            return pltpu.make_async_remote_copy(src_ref=slot if src is None else src, dst_ref=slot, send_sem=send.at[a, k],
                                                recv_sem=recv.at[a, k], device_id=to, device_id_type=MESH)

        mine = [pltpu.make_async_copy(ins[a], outs[a].at[4 * x + 2 * y + c], lsem.at[a]) for a in range(n)]
        first = []
        for a in range(n):
            mine[a].start()
            first.append(copy(a, 0, me, sibling, src=ins[a]))
            first += [copy(a, 1 + j, me, (*chip, c), src=ins[a]) for j, chip in enumerate(chips)]
        for cp in first:
            cp.start()
        passed = []
        for a in range(n):
            for j, chip in enumerate(chips):
                copy(a, 1 + j, (*chip, c), me).wait_recv()
                fwd = copy(a, 4 + j, (*chip, c), sibling)
                fwd.start()
                passed.append(fwd)
        for a in range(n):
            copy(a, 0, sibling, me).wait_recv()
            for j, chip in enumerate(chips):
                copy(a, 4 + j, (*chip, 1 - c), me).wait_recv()
        for cp in first + passed:
            cp.wait_send()
        for cp in mine:
            cp.wait()

    return pl.pallas_call(
        body, name=name, in_specs=[ANY] * n, out_specs=[ANY] * n,
        out_shape=[jax.ShapeDtypeStruct((N_DEV,) + a.shape, a.dtype) for a in arrs],
        scratch_shapes=[pltpu.SemaphoreType.DMA((n, 7)), pltpu.SemaphoreType.DMA((n, 7)), pltpu.SemaphoreType.DMA((n,))],
    )(*arrs)


HBM = pl.BlockSpec(memory_space=pltpu.HBM)
SEM = pl.BlockSpec(memory_space=pltpu.SEMAPHORE)
EFFECT = pltpu.SideEffectType.DATAFLOW_SIDE_EFFECTING
GATHER_PEERS = (1, 2, 4, 6)
OTHER_CHIPS = (2, 4, 6)
COPIES_PER_ARRAY = {"scatter": N_DEV - 1, "gather": len(GATHER_PEERS), "forward": len(OTHER_CHIPS)}


def _split_plan(mode, srcs, lands, send, recv):
    x, y, c = lax.axis_index("x"), lax.axis_index("y"), lax.axis_index("c")

    def dev(k):
        return (1 - x if k & 4 else x), (1 - y if k & 2 else y), (1 - c if k & 1 else c)

    def idx(d):
        return 4 * d[0] + 2 * d[1] + d[2]

    me = idx((x, y, c))
    plan = []
    for a, land in enumerate(lands):
        if mode == "scatter":
            legs = [(srcs[a].at[idx(dev(k))], land.at[me], land.at[idx(dev(k))], dev(k)) for k in range(1, N_DEV)]
        elif mode == "gather":
            legs = [(srcs[a], land.at[me], land.at[idx(dev(k))], dev(k)) for k in GATHER_PEERS]
        else:
            legs = [(land.at[idx(dev(j))], land.at[idx(dev(j))], land.at[idx(dev(j ^ 1))], dev(1)) for j in OTHER_CHIPS]
        for i, (src, dst, arrival, to) in enumerate(legs):
            sem = a * len(legs) + i
            pair = tuple(pltpu.make_async_remote_copy(src_ref=src, dst_ref=d, send_sem=send.at[sem], recv_sem=recv.at[sem],
                                                      device_id=to, device_id_type=MESH) for d in (dst, arrival))
            plan.append(pair)
    return plan


def split_start(name, mode, srcs, lands, after):
    if lands is None:
        lands = [lax.empty((N_DEV,) + (s.shape[1:] if mode == "scatter" else s.shape), s.dtype) for s in srcs]
    ns, nl = len(srcs), len(lands)
    nsem = COPIES_PER_ARRAY[mode] * nl

    def body(*refs):
        ins, lnd = refs[:ns], refs[ns:ns + nl]
        send, recv = refs[ns + nl + 1], refs[ns + nl + 2]
        token = refs[-1]
        for out, _ in _split_plan(mode, ins, lnd, send, recv):
            out.start()
        token[...] = jnp.zeros_like(token)

    arrs = list(srcs) + list(lands)
    return pl.pallas_call(
        body, name=name,
        out_shape=(pltpu.SemaphoreType.DMA((nsem,)), pltpu.SemaphoreType.DMA((nsem,)))
        + tuple(pltpu.HBM(t.shape, t.dtype) for t in arrs) + (jax.ShapeDtypeStruct((SUBLANES, LANES), F32),),
        in_specs=[HBM] * len(arrs) + [ANY],
        out_specs=(SEM, SEM) + (HBM,) * len(arrs) + (pl.BlockSpec(memory_space=pltpu.VMEM),),
        input_output_aliases={i: 2 + i for i in range(len(arrs))},
        compiler_params=pltpu.CompilerParams(has_side_effects=EFFECT),
    )(*[pltpu.with_memory_space_constraint(t, pltpu.HBM) for t in arrs], after)


def split_wait(name, mode, handle, after):
    send, recv = handle[0], handle[1]
    arrs = list(handle[2:-1])
    nl = len(arrs) if mode == "forward" else len(arrs) // 2
    ns = len(arrs) - nl

    def body(*refs):
        ins, lnd = refs[:ns], refs[ns:ns + nl]
        s, r = refs[ns + nl], refs[ns + nl + 1]
        for out, arrival in _split_plan(mode, ins, lnd, s, r):
            out.wait_send()
            arrival.wait_recv()

    outs = pl.pallas_call(
        body, name=name,
        out_shape=tuple(pltpu.HBM(t.shape, t.dtype) for t in arrs),
        in_specs=[HBM] * len(arrs) + [SEM, SEM, ANY], out_specs=(HBM,) * len(arrs),
        input_output_aliases={i: i for i in range(len(arrs))},
        compiler_params=pltpu.CompilerParams(has_side_effects=EFFECT),
    )(*arrs, send, recv, after)
    return list(outs[ns:])


def _row_tile(rows, cols):
    cap = max(SUBLANES, (2**18 // cols) // SUBLANES * SUBLANES)
    if rows <= cap:
        return rows
    for t in range(cap, SUBLANES - 1, -SUBLANES):
        if rows % t == 0:
            return t
    return rows


def ordered_sum(name, parts, own, me):
    n, R, C = parts.shape
    rt = _row_tile(R, C)

    def body(me_ref, p_ref, own_ref, o_ref):
        s = None
        for k in range(n):
            t = jnp.where(me_ref[0] == k, own_ref[...], p_ref[k])
            s = t if s is None else s + t
        o_ref[...] = s

    return pl.pallas_call(
        body, name=name,
        grid_spec=pltpu.PrefetchScalarGridSpec(
            num_scalar_prefetch=1, grid=(R // rt,),
            in_specs=[pl.BlockSpec((n, rt, C), lambda r, p: (0, r, 0)), pl.BlockSpec((rt, C), lambda r, p: (r, 0))],
            out_specs=pl.BlockSpec((rt, C), lambda r, p: (r, 0))),
        out_shape=jax.ShapeDtypeStruct((R, C), F32), compiler_params=_params("parallel"),
    )(me, parts, own)


def _adamw_update(w, m, v, g):
    bias1 = 1.0 - ADAM_B1 ** ADAM_STEP
    bias2 = 1.0 - ADAM_B2 ** ADAM_STEP
    m_new = ADAM_B1 * m + (1.0 - ADAM_B1) * g
    v_new = ADAM_B2 * v + (1.0 - ADAM_B2) * (g * g)
    delta = -ADAM_LR * ((m_new / bias1) / (jnp.sqrt(v_new / bias2) + ADAM_EPS) + ADAM_WD * w)
    return delta, m_new, v_new


def adamw(name, w, m, v, g):
    R, C = w.shape
    rt = _row_tile(R, C)

    def body(w_ref, m_ref, v_ref, g_ref, d_ref, mo_ref, vo_ref):
        d_ref[...], mo_ref[...], vo_ref[...] = _adamw_update(w_ref[...], m_ref[...], v_ref[...], g_ref[...])

    row = pl.BlockSpec((rt, C), lambda r: (r, 0))
    return pl.pallas_call(
        body, name=name, grid=(R // rt,), in_specs=[row] * 4, out_specs=[row] * 3,
        out_shape=[jax.ShapeDtypeStruct((R, C), F32)] * 3, compiler_params=_params("parallel"),
    )(w, m, v, g)


def adamw_sharded(name, layer, w, m, v, own, land, me, prev):
    _, R, C = own.shape
    rt = _row_tile(R, C)

    def body(me_ref, w_ref, m_ref, v_ref, own_ref, land_ref, *rest):
        go_ref, d_ref, mo_ref, vo_ref = rest[-4:]
        g = own_ref[...]
        for k in range(N_DEV):
            g = g + jnp.where(me_ref[0] == k, 0.0, land_ref[k].astype(F32))
        go_ref[...] = g
        d_ref[...], mo_ref[...], vo_ref[...] = _adamw_update(w_ref[...], m_ref[...], v_ref[...], g)

    row = pl.BlockSpec((None, rt, C), lambda r, p: (layer, r, 0))
    in_specs = [row, row, row, pl.BlockSpec((None, rt, C), lambda r, p: (p[0], r, 0)), pl.BlockSpec((N_DEV, rt, C), lambda r, p: (0, r, 0))]
    ins = [me, w, m, v, own, land]
    aliases = {}
    if prev is not None:
        in_specs += [ANY] * 4
        aliases = {len(ins) + k: k for k in range(4)}
        ins += list(prev)
    return pl.pallas_call(
        body, name=name,
        grid_spec=pltpu.PrefetchScalarGridSpec(num_scalar_prefetch=1, grid=(R // rt,), in_specs=in_specs, out_specs=[row] * 4),
        out_shape=[jax.ShapeDtypeStruct(w.shape, F32)] * 4, input_output_aliases=aliases,
        compiler_params=_params("arbitrary"),
    )(*ins)


def _hgrn_lower_bounds(lb_logits):
    p = jax.nn.softmax(lb_logits, axis=1)
    return jnp.cumsum(p, axis=1) - p[:, :1]


def _s5_discretise(lam_re, lam_im, log_dt, b_re, b_im):
    lr = jnp.minimum(lam_re, -1e-4)
    li = lam_im
    dt = jnp.exp(log_dt)[:, None]
    mag = jnp.exp(lr * dt)
    ar, ai = mag * jnp.cos(li * dt), mag * jnp.sin(li * dt)
    den = lr * lr + li * li
    nr = ar - 1.0
    cr = (nr * lr + ai * li) / den
    ci = (ai * lr - nr * li) / den
    bbr = cr[..., None] * b_re - ci[..., None] * b_im
    bbi = cr[..., None] * b_im + ci[..., None] * b_re
    return ar, ai, bbr, bbi


def _block_diag(t):
    G, a, b = t.shape
    eye = jnp.eye(G, dtype=F32)
    return (t[:, :, None, :] * eye[:, None, :, None]).reshape(G * a, G * b)


def _rope_tables(S):
    half = RET_DK // 2
    inv = ROPE_BASE ** (-jnp.arange(half, dtype=F32) / half)
    ang = jnp.arange(S, dtype=F32)[:, None] * inv[None, :]
    cos, sin = jnp.cos(ang), jnp.sin(ang)
    return jnp.concatenate([cos, cos], axis=1), jnp.concatenate([-sin, sin], axis=1)


def _ret_log_decays():
    f = tuple(float(np.log1p(-np.exp2(np.float32(-5.0 - h)))) for h in range(RET_HEADS))
    b = tuple(float(np.log1p(-np.exp2(np.float32(-5.5 - h)))) for h in range(RET_HEADS))
    return f, b


def assemble_weight(name, land, own, me):
    _, R, C = land.shape
    tr = min(R, 256)

    def body(me_ref, land_ref, own_ref, o_ref):
        for d in range(N_DEV):
            o_ref[:, d * C:(d + 1) * C] = jnp.where(me_ref[0] == d, own_ref[...], land_ref[d])

    return pl.pallas_call(
        body, name=name,
        grid_spec=pltpu.PrefetchScalarGridSpec(
            num_scalar_prefetch=1, grid=(R // tr,),
            in_specs=[pl.BlockSpec((N_DEV, tr, C), lambda i, p: (0, i, 0)), pl.BlockSpec((tr, C), lambda i, p: (i, 0))],
            out_specs=pl.BlockSpec((tr, N_DEV * C), lambda i, p: (i, 0))),
        out_shape=jax.ShapeDtypeStruct((R, N_DEV * C), land.dtype), compiler_params=_params("parallel"),
    )(me, land, own)


def _row(t):
    return t.reshape(1, -1)


def mixer_stage(layer, S, resid, branch, w_in, w_out, carriers, small):
    j = layer // 2
    tag = f"l{layer}"
    g = _row(small["mix_norm_g"][layer])
    if branch is None:
        x = resid
        (h,) = make_rowop(tag + "_norm", norm_f, (D_MODEL,))((x,), (g,), ())
    else:
        x, h = make_rowop(tag + "_addnorm", addnorm_f, (D_MODEL, D_MODEL))((resid, branch), (g,), ())
    if layer % 2 == 0:
        lbs = _hgrn_lower_bounds(small["hgrn_lb_logits"])
        prm = (small["gla_wa2"][j, 0], small["gla_wa2"][j, 1], _row(small["gla_ba"][j, 0]), _row(small["gla_ba"][j, 1]),
               _row(lbs[0, j]), _row(lbs[1, j]))
        outs = make_proj_stage(tag + "_prep", even_prep_f, EVEN_PREP_WIDTHS)(h, w_in, carriers["in32"], carriers["in16"], prm, ())
        gla_qkv, (glaf, glab, gr), hgrn_qkv, (hlaf, hlab, hg) = outs[0:6], outs[6:9], outs[9:15], outs[15:18]
        of, ob = make_scan(tag + "_gla", GLA_HEADS, GLA_DK, GLA_DV, S, SCAN_CHUNK)(gla_qkv, glaf, glab)
        hof, hob = make_scan(tag + "_hgrn", HGRN_HEADS, HGRN_DK, HGRN_DV, S, HGRN_SCAN_CHUNK)(hgrn_qkv, hlaf, hlab)
        (y,) = make_rowop(tag + "_post", even_post_f, (D_MODEL,))(
            (of, ob, hof, hob, gr, hg), (_row(small["gla_norm_g"][j]), _row(small["hgrn_norm_g"][j])), ())
    else:
        cosf, sinf = _rope_tables(S)
        outs = make_proj_stage(tag + "_prep", odd_prep_f, ODD_PREP_WIDTHS)(h, w_in, carriers["in32"], carriers["in16"], (), (cosf, sinf))
        ret_qkv, rg, su = outs[0:6], outs[6], outs[7]
        of, ob = make_scan(tag + "_ret", RET_HEADS, RET_DK, RET_DV, S, SCAN_CHUNK, const_lg=_ret_log_decays())(ret_qkv)
        (cm,) = make_rowop(tag + "_retpost", ret_post_f, (RET_HEADS * RET_DV,))((of, ob, rg), (_row(small["ret_norm_g"][j]),), ())
        disc = [_s5_discretise(small["s5_lam_re"][j, d], small["s5_lam_im"][j, d], small["s5_log_dt"][j, d],
                               small["s5_b_re"][j], small["s5_b_im"][j]) for d in range(2)]
        a4 = jnp.stack([t.reshape(-1) for d in range(2) for t in disc[d][:2]], axis=0)
        su_p = make_reorder(tag + "_s5seg", S, True)(su)
        bd = jnp.concatenate([_block_diag(jnp.swapaxes(disc[d][2 + i], 1, 2)) for d in range(2) for i in range(2)], axis=1)
        Xs = make_mm_groups(tag + "_s5in", 4)(su_p, bd)
        Hs = make_s5_scan(tag + "_s5scan", S)(Xs, a4)
        prm = (_block_diag(jnp.swapaxes(small["s5_c_re"][j], 1, 2)), _block_diag(jnp.swapaxes(small["s5_c_im"][j], 1, 2)),
               _row(small["s5_d"][j]), small["s5_glu_w"][j], _row(small["s5_glu_b"][j]))
        (dm_p,) = make_rowop(tag + "_s5post", s5_post_f, (S5_WIDTH,))((*Hs, su_p), prm, ())
        y = make_join(tag + "_join", S, RET_HEADS * RET_DV, S5_WIDTH)(cm, dm_p)
    return (x, *make_mm(tag + "_out")(y, w_out, carriers["out32"], carriers["out16"]))


def ffn_stage(layer, S, resid, branch, w_up, w_down, carriers, small):
    tag = f"l{layer}"
    x, hf = make_rowop(tag + "_ffnnorm", addnorm_f, (D_MODEL, D_MODEL))((resid, branch), (_row(small["ffn_norm_g"][layer]),), ())
    out, tick = make_ffn(tag + "_ffn", S)(hf, w_up, carriers["up32"], carriers["up16"], w_down, carriers["down32"], carriers["down16"],
                                         small["ffn_conv_w"][layer], _row(small["ffn_conv_b"][layer]))
    return x, out, tick


BIG = {"w_in_even": 2, "w_out_even": 1, "w_in_odd": 2, "w_out_odd": 1, "ffn_w_up": 2, "ffn_w_down": 1}
SMALL_SHARDED = {"gla_wa2": 3, "gla_ba": 2, "hgrn_lb_logits": 2, "ret_norm_g": 1, "s5_d": 1, "s5_glu_w": 1, "s5_glu_b": 1,
                 "ffn_conv_w": 2}
REPLICATED = ("mix_norm_g", "ffn_norm_g", "final_norm_g", "gla_norm_g", "hgrn_norm_g", "s5_lam_re", "s5_lam_im", "s5_log_dt",
              "s5_b_re", "s5_b_im", "s5_c_re", "s5_c_im", "ffn_conv_b")
WEIGHTS = ("mix_norm_g", "ffn_norm_g", "final_norm_g", "w_in_even", "w_out_even", "gla_wa2", "gla_ba", "gla_norm_g",
           "hgrn_lb_logits", "hgrn_norm_g", "w_in_odd", "w_out_odd", "ret_norm_g", "s5_lam_re", "s5_lam_im", "s5_log_dt",
           "s5_b_re", "s5_b_im", "s5_c_re", "s5_c_im", "s5_d", "s5_glu_w", "s5_glu_b", "ffn_w_up", "ffn_conv_w", "ffn_conv_b",
           "ffn_w_down")
PACK_COLS = LANES
MIXER_SMALL = (("mix_norm_g", "hgrn_lb_logits", "gla_wa2", "gla_ba", "gla_norm_g", "hgrn_norm_g"),
               ("mix_norm_g", "ret_norm_g", "s5_lam_re", "s5_lam_im", "s5_log_dt", "s5_b_re", "s5_b_im", "s5_c_re", "s5_c_im",
                "s5_d", "s5_glu_w", "s5_glu_b"))
FFN_SMALL = ("ffn_norm_g", "ffn_conv_w", "ffn_conv_b")


def _unshard(g, axis):
    t = jnp.moveaxis(g, 0, axis)
    return t.reshape(t.shape[:axis] + (t.shape[axis] * t.shape[axis + 1],) + t.shape[axis + 2:])


def _pack_rows(shape):
    return -(-int(np.prod(shape)) // (SUBLANES * PACK_COLS)) * SUBLANES


def _pack(arrs):
    parts = []
    for a in arrs:
        rows = _pack_rows(a.shape)
        parts.append(jnp.pad(a.reshape(-1), (0, rows * PACK_COLS - a.size)).reshape(rows, PACK_COLS))
    return jnp.concatenate(parts, axis=0)


def _unpack(packed, shapes):
    lead = packed.shape[:-2]
    out, r = [], 0
    for s in shapes:
        rows, n = _pack_rows(s), int(np.prod(s))
        piece = packed[..., r:r + rows, :].reshape(lead + (rows * PACK_COLS,))
        out.append(piece[..., :n].reshape(lead + tuple(s)))
        r += rows
    return out


def kernel(x, mix_norm_g, ffn_norm_g, final_norm_g, w_in_even, w_out_even, gla_wa2, gla_ba, gla_norm_g, hgrn_lb_logits, hgrn_norm_g, w_in_odd, w_out_odd, ret_norm_g, s5_lam_re, s5_lam_im, s5_log_dt, s5_b_re, s5_b_im, s5_c_re, s5_c_im, s5_d, s5_glu_w, s5_glu_b, ffn_w_up, ffn_conv_w, ffn_conv_b, ffn_w_down, loss_target, m_mix_norm_g, m_ffn_norm_g, m_final_norm_g, m_w_in_even, m_w_out_even, m_gla_wa2, m_gla_ba, m_gla_norm_g, m_hgrn_lb_logits, m_hgrn_norm_g, m_w_in_odd, m_w_out_odd, m_ret_norm_g, m_s5_lam_re, m_s5_lam_im, m_s5_log_dt, m_s5_b_re, m_s5_b_im, m_s5_c_re, m_s5_c_im, m_s5_d, m_s5_glu_w, m_s5_glu_b, m_ffn_w_up, m_ffn_conv_w, m_ffn_conv_b, m_ffn_w_down, v_mix_norm_g, v_ffn_norm_g, v_final_norm_g, v_w_in_even, v_w_out_even, v_gla_wa2, v_gla_ba, v_gla_norm_g, v_hgrn_lb_logits, v_hgrn_norm_g, v_w_in_odd, v_w_out_odd, v_ret_norm_g, v_s5_lam_re, v_s5_lam_im, v_s5_log_dt, v_s5_b_re, v_s5_b_im, v_s5_c_re, v_s5_c_im, v_s5_d, v_s5_glu_w, v_s5_glu_b, v_ffn_w_up, v_ffn_conv_w, v_ffn_conv_b, v_ffn_w_down):
    args = locals()
    w = {n: args[n] for n in WEIGHTS}
    m = {n: args["m_" + n] for n in WEIGHTS}
    v = {n: args["v_" + n] for n in WEIGHTS}
    Bl, S, D = x.shape
    T = Bl * S
    ix, iy, ic = lax.axis_index("x"), lax.axis_index("y"), lax.axis_index("c")
    me = 4 * ix + 2 * iy + ic

    xt = x.reshape(T, D)
    me1 = me.reshape(1).astype(jnp.int32)
    stages = []
    for layer in range(DEPTH):
        j = layer // 2
        kin, kout = ("w_in_even", "w_out_even") if layer % 2 == 0 else ("w_in_odd", "w_out_odd")
        stages.append((mixer_stage, layer, ("in", "out"), ((kin, j, True), (kout, j, False)), MIXER_SMALL[layer % 2]))
        stages.append((ffn_stage, layer, ("up", "down"), (("ffn_w_up", layer, True), ("ffn_w_down", layer, False)), FFN_SMALL))

    gather, after = [], xt
    for s, (_, _, _, projs, _) in enumerate(stages):
        handle = split_start(f"gather{s}_start", "gather", [w[n][l].astype(BF16) for n, l, _ in projs], None, after)
        gather.append(handle)
        after = handle[-1]
    sm_names = list(SMALL_SHARDED)
    (sm_all8,) = all_gather("gather_small", [_pack([w[n] for n in sm_names])])
    small = {n: _unshard(t, SMALL_SHARDED[n]) for n, t in zip(sm_names, _unpack(sm_all8, [w[n].shape for n in sm_names]))}
    small.update({n: w[n] for n in REPLICATED})
    small["mix_norm_g"] = small["mix_norm_g"] + after[0, 0]

    resid, branch, pulls = xt, None, []
    second = [None] * len(stages)

    def second_level(s, after):
        lands = split_wait(f"gather{s}_wait", "gather", gather[s], after)
        second[s] = split_start(f"forward{s}_start", "forward", [], lands, after)

    for s, (fn, layer, keys, projs, sm_keys) in enumerate(stages):
        here = lax.stop_gradient(resid)
        if second[s] is None:
            second_level(s, here)
        lands = split_wait(f"forward{s}_wait", "forward", second[s], second[s][-1])
        sm = {n: small[n] for n in sm_keys}
        if 2 <= s < len(stages) - 1:
            second_level(s + 1, here)
            norm = "mix_norm_g" if fn is mixer_stage else "ffn_norm_g"
            sm[norm] = sm[norm] + second[s + 1][-1][0, 0]
        full, carriers = [], {}
        for key, land, (n, l, col) in zip(keys, lands, projs, strict=True):
            if col:
                full.append(assemble_weight(f"weight{s}_{key}", land, w[n][l].astype(BF16), me1))
            else:
                blocks = lax.dynamic_update_index_in_dim(land, w[n][l].astype(BF16), me, 0)
                full.append(blocks.reshape(N_DEV * land.shape[1], land.shape[2]))
            carriers[key + "32"] = jnp.zeros(land.shape, F32)
            carriers[key + "16"] = jnp.zeros(land.shape, BF16)
        run = functools.partial(fn, layer, S)
        if branch is None:
            (resid, branch, _), pull = jax.vjp(lambda r, c, p, run=run, full=full: run(r, None, full[0], full[1], c, p), resid, carriers, sm)
        else:
            (resid, branch, _), pull = jax.vjp(lambda r, b, c, p, run=run, full=full: run(r, b, full[0], full[1], c, p), resid, branch, carriers, sm)
        pulls.append(pull)

    loss_acc, dxf, dgf = loss_head(resid, branch, small["final_norm_g"].reshape(1, D), loss_target.reshape(T, D))
    loss = lax.psum(loss_acc[0, 0], ("x", "y", "c"))
    g_small = {"final_norm_g": dgf.reshape(D)}
    d_resid, d_branch, token = dxf, dxf, _tick()
    scatter, own32 = [None] * len(stages), [None] * len(stages)
    for s in reversed(range(len(stages))):
        if s == 0:
            d_resid, dcar, dsm = pulls[s]((d_resid, d_branch, token))
        else:
            d_resid, d_branch, dcar, dsm = pulls[s]((d_resid, d_branch, token))
        for n, g in dsm.items():
            g_small[n] = g_small[n] + g if n in g_small else g
        keys = stages[s][2]
        own32[s] = [dcar[k + "32"] for k in keys]
        scatter[s] = split_start(f"grads{s}_start", "scatter", [dcar[k + "16"] for k in keys], None, d_resid)
        token = scatter[s][-1]
    dx = d_resid

    out = {}
    sm_all = sm_names + list(REPLICATED)
    g_pack = _pack([g_small[n] for n in sm_all])
    small_first = split_start("small_grads_start", "gather", [g_pack], None, dx)
    chain = {}
    for s in reversed(range(1, len(stages))):
        lands = split_wait(f"grads{s}_wait", "scatter", scatter[s], small_first[-1] if s == len(stages) - 1 else dx)
        for own, land, (n, l, _) in zip(own32[s], lands, stages[s][3], strict=True):
            chain[n] = adamw_sharded(f"adamw_{n}_{l}", l, w[n], m[n], v[n], own, land, me1, chain.get(n))
            behind = chain[n][0]

    lands = split_wait("small_grads_wait", "gather", small_first, behind)
    small_second = split_start("small_forward_start", "forward", [], lands, behind)
    (g_all,) = split_wait("small_forward_wait", "forward", small_second, small_second[-1])
    g_sum = _unpack(ordered_sum("sum_small_grads", g_all, g_pack, me1), [g_small[n].shape for n in sm_all])
    g_loc = []
    for n, g in zip(sm_all, g_sum):
        if n in SMALL_SHARDED:
            ax = SMALL_SHARDED[n]
            size = w[n].shape[ax]
            g = lax.dynamic_slice_in_dim(g, me * size, size, axis=ax)
        g_loc.append(g)
    shapes = [w[n].shape for n in sm_all]
    res = adamw("adamw_small", _pack([w[n] for n in sm_all]), _pack([m[n] for n in sm_all]), _pack([v[n] for n in sm_all]), _pack(g_loc))
    behind = res[0]
    res = [g_loc] + [_unpack(r, shapes) for r in res]
    for k, n in enumerate(sm_all):
        out[n] = [r[k] for r in res]

    lands = split_wait("grads0_wait", "scatter", scatter[0], behind)
    for own, land, (n, l, _) in zip(own32[0], lands, stages[0][3], strict=True):
        chain[n] = adamw_sharded(f"adamw_{n}_{l}", l, w[n], m[n], v[n], own, land, me1, chain.get(n))
    for n in BIG:
        out[n] = [t.reshape(w[n].shape) for t in chain[n]]

    grads, deltas, new_m, new_v = ([out[n][k] for n in WEIGHTS] for k in range(4))
    return (loss, dx.reshape(Bl, S, D), *grads, *deltas, *new_m, *new_v)
```

```python
import functools

import numpy as np
import jax
import jax.numpy as jnp
from jax import lax
from jax.experimental import pallas as pl
from jax.experimental.pallas import tpu as pltpu

F32 = jnp.float32
BF16 = jnp.bfloat16
HIGHEST = lax.Precision.HIGHEST
MESH = pl.DeviceIdType.MESH

D_MODEL = 1024
DEPTH = 4
GLA_HEADS, GLA_DK, GLA_DV, GLA_GATE_NORM = 4, 64, 128, 16.0
HGRN_HEADS, HGRN_DK, HGRN_DV, HGRN_MIN_F = 4, 64, 128, 1e-20
RET_HEADS, RET_DK, RET_DV = 4, 128, 192
ROPE_BASE = 10000.0
S5_WIDTH, S5_GROUPS, S5_STATE = 256, 16, 64
S5_N = S5_GROUPS * S5_STATE
EPS = 1e-6
ADAM_LR, ADAM_B1, ADAM_B2, ADAM_EPS, ADAM_WD, ADAM_STEP = 0.001, 0.9, 0.999, 1e-08, 0.01, 10

N_DEV = 8
VMEM_LIMIT_BYTES = 56 * 1024 * 1024
ROW_TILE = 256
SCAN_CHUNK = 128
HGRN_SCAN_CHUNK = 64
S5_SEGMENTS = 8
LANES = 128


def _params(*sem):
    return pltpu.CompilerParams(dimension_semantics=sem, vmem_limit_bytes=VMEM_LIMIT_BYTES)


def _divisor_tile(n, cap):
    best = None
    for t in range(LANES, min(n, cap) + 1, LANES):
        if n % t == 0:
            best = t
    return best if best is not None else n


def _mm_nn(name, x, w):
    M, K = x.shape
    N = w.shape[1]
    tn = _divisor_tile(N, 3072) if K * N * 2 > 8 * 2**20 else N
    tm = 256 if tn * 4 * 512 > 6 * 2**20 else 512
    assert M % tm == 0 and N % tn == 0

    def body(x_ref, w_ref, o_ref):
        o_ref[...] = jnp.dot(x_ref[...].astype(BF16), w_ref[...], preferred_element_type=F32)

    return pl.pallas_call(
        body, name=name, grid=(N // tn, M // tm),
        in_specs=[pl.BlockSpec((tm, K), lambda j, i: (i, 0)), pl.BlockSpec((K, tn), lambda j, i: (0, j))],
        out_specs=pl.BlockSpec((tm, tn), lambda j, i: (i, j)),
        out_shape=jax.ShapeDtypeStruct((M, N), F32),
        compiler_params=_params("parallel", "parallel"),
    )(x, w)


def _mm_nt(name, dy, w, after=None):
    M, N = dy.shape
    K = w.shape[0]
    tk = _divisor_tile(K, 1024) if K * N * 2 > 8 * 2**20 else K
    tm = 256 if N >= 4096 else 512
    assert M % tm == 0 and K % tk == 0

    def body(dy_ref, w_ref, *rest):
        rest[-1][...] = lax.dot_general(dy_ref[...].astype(BF16), w_ref[...], (((1,), (1,)), ((), ())),
                                        preferred_element_type=F32)

    extra = [] if after is None else [after]
    return pl.pallas_call(
        body, name=name, grid=(K // tk, M // tm),
        in_specs=[pl.BlockSpec((tm, N), lambda j, i: (i, 0)), pl.BlockSpec((tk, N), lambda j, i: (j, 0))]
        + [pl.BlockSpec(memory_space=pl.ANY)] * len(extra),
        out_specs=pl.BlockSpec((tm, tk), lambda j, i: (i, j)),
        out_shape=jax.ShapeDtypeStruct((M, K), F32),
        compiler_params=_params("parallel", "parallel"),
    )(dy, w, *extra)


def _tick():
    return jnp.zeros((SUBLANES, LANES), F32)


MM_TN_VMEM_BUDGET = 46 * 2**20


def _pad_lanes(n):
    return -(-n // LANES) * LANES


def _mm_tn(name, x, dy, nblk, want16):
    M, K = x.shape
    N = dy.shape[1]
    n = N // nblk
    xb, yb = x.dtype.itemsize, dy.dtype.itemsize
    best = None
    for tk in [t for t in range(LANES, K + 1, LANES) if K % t == 0]:
        for tm in (512, 256):
            out_bytes = nblk * tk * _pad_lanes(n) * (6 if want16 else 4)
            vmem = 2 * out_bytes + 2 * tm * tk * xb + 2 * tm * _pad_lanes(N) * yb
            traffic = (K // tk) * M * N * yb + M * K * xb
            if vmem <= MM_TN_VMEM_BUDGET and M % tm == 0 and (best is None or (traffic, -tm) < best[0]):
                best = ((traffic, -tm), tk, tm)
    _, tk, tm = best
    last = M // tm - 1

    def body(x_ref, dy_ref, o32_ref, *o16_ref):
        m = pl.program_id(1)

        @pl.when(m == 0)
        def _():
            o32_ref[...] = jnp.zeros_like(o32_ref)

        dyv = dy_ref[...].astype(BF16)
        if nblk == 1:
            o32_ref[0] += lax.dot_general(x_ref[...].astype(BF16), dyv, (((0,), (0,)), ((), ())), preferred_element_type=F32)
        else:
            xt = x_ref[...].astype(F32).T.astype(BF16)
            for d in range(nblk):
                o32_ref[d] += jnp.dot(xt, dyv[:, d * n:(d + 1) * n], preferred_element_type=F32)
        if want16:
            @pl.when(m == last)
            def _():
                o16_ref[0][...] = o32_ref[...].astype(BF16)

    blk = pl.BlockSpec((nblk, tk, n), lambda a, m: (0, a, 0))
    return pl.pallas_call(
        body, name=name, grid=(K // tk, M // tm),
        in_specs=[pl.BlockSpec((tm, tk), lambda a, m: (m, a)), pl.BlockSpec((tm, N), lambda a, m: (m, 0))],
        out_specs=[blk, blk] if want16 else [blk],
        out_shape=[jax.ShapeDtypeStruct((nblk, K, n), F32)] + ([jax.ShapeDtypeStruct((nblk, K, n), BF16)] if want16 else []),
        compiler_params=_params("parallel", "arbitrary"),
    )(x, dy)


def _weight_grads(name, x, dy, col_sharded):
    if col_sharded:
        return _mm_tn(name, x, dy, N_DEV, True)
    d32, d16 = _mm_tn(name, x, dy, 1, True)
    K, N = d32.shape[1:]
    return d32.reshape(N_DEV, K // N_DEV, N), d16.reshape(N_DEV, K // N_DEV, N)


def make_mm(name, col_sharded=False):
    @jax.custom_vjp
    def mm(x, w16, c32, c16):
        return _mm_nn(name + "_fwd", x, w16), _tick()

    def fwd(x, w16, c32, c16):
        return (_mm_nn(name + "_fwd", x, w16), _tick()), (x, w16)

    def bwd(res, g):
        x, w16 = res
        dy, after = g
        dx = _mm_nt(name + "_dx", dy, w16, after)
        d32, d16 = _weight_grads(name + "_dw", x, dy, col_sharded)
        return dx, jnp.zeros_like(w16), d32, d16

    mm.defvjp(fwd, bwd)
    return mm


def make_mm_groups(name, G, tm=512):
    def products(x, w16):
        M, K = x.shape
        N = w16.shape[1] // G

        def body(x_ref, w_ref, *o_refs):
            xv = x_ref[...].astype(BF16)
            for g, o_ref in enumerate(o_refs):
                o_ref[...] = jnp.dot(xv, w_ref[:, g * N:(g + 1) * N], preferred_element_type=F32)

        row = pl.BlockSpec((tm, N), lambda i: (i, 0))
        return tuple(pl.pallas_call(
            body, name=name + "_fwd", grid=(M // tm,),
            in_specs=[pl.BlockSpec((tm, K), lambda i: (i, 0)), pl.BlockSpec((K, G * N), lambda i: (0, 0))],
            out_specs=[row] * G, out_shape=[jax.ShapeDtypeStruct((M, N), F32)] * G, compiler_params=_params("parallel"),
        )(x, w16))

    def grads(x, w16, dys):
        M, K = x.shape
        N = w16.shape[1] // G

        def body(x_ref, w_ref, *refs):
            dy_refs, dx_ref, dw_ref = refs[:G], refs[G], refs[G + 1]

            @pl.when(pl.program_id(0) == 0)
            def _():
                dw_ref[...] = jnp.zeros_like(dw_ref)

            xv = x_ref[...].astype(BF16)
            dx = jnp.zeros(dx_ref.shape, F32)
            for g in range(G):
                cols = slice(g * N, (g + 1) * N)
                dyv = dy_refs[g][...].astype(BF16)
                dx = dx + lax.dot_general(dyv, w_ref[:, cols], (((1,), (1,)), ((), ())), preferred_element_type=F32)
                dw_ref[:, cols] += lax.dot_general(xv, dyv, (((0,), (0,)), ((), ())), preferred_element_type=F32)
            dx_ref[...] = dx

        row = pl.BlockSpec((tm, N), lambda i: (i, 0))
        return pl.pallas_call(
            body, name=name + "_bwd", grid=(M // tm,),
            in_specs=[pl.BlockSpec((tm, K), lambda i: (i, 0)), pl.BlockSpec((K, G * N), lambda i: (0, 0))] + [row] * G,
            out_specs=[pl.BlockSpec((tm, K), lambda i: (i, 0)), pl.BlockSpec((K, G * N), lambda i: (0, 0))],
            out_shape=[jax.ShapeDtypeStruct((M, K), F32), jax.ShapeDtypeStruct((K, G * N), F32)],
            compiler_params=_params("arbitrary"),
        )(x, w16, *dys)

    @jax.custom_vjp
    def mm(x, w):
        return products(x, w.astype(BF16))

    def fwd(x, w):
        w16 = w.astype(BF16)
        return products(x, w16), (x, w16)

    def bwd(res, dys):
        x, w16 = res
        return tuple(grads(x, w16, tuple(dys)))

    mm.defvjp(fwd, bwd)
    return mm


def _row_specs(rows, params, consts, tile):
    specs = [pl.BlockSpec((tile, r.shape[1]), lambda i: (i, 0)) for r in rows]
    specs += [pl.BlockSpec(p.shape, lambda i: (0, 0)) for p in params]
    specs += [pl.BlockSpec((tile, c.shape[1]), lambda i, n=c.shape[0] // tile: (i % n, 0)) for c in consts]
    return specs


def _row_fwd(name, f, out_widths, tile, rows, params, consts):
    T = rows[0].shape[0]
    nr, npar, ncon = len(rows), len(params), len(consts)

    def body(*refs):
        r = tuple(x[...] for x in refs[:nr])
        p = tuple(x[...] for x in refs[nr:nr + npar])
        c = tuple(x[...] for x in refs[nr + npar:nr + npar + ncon])
        outs = f(r, p, c)
        for o_ref, o in zip(refs[nr + npar + ncon:], outs, strict=True):
            o_ref[...] = o

    return pl.pallas_call(
        body, name=name + "_fwd", grid=(T // tile,),
        in_specs=_row_specs(rows, params, consts, tile),
        out_specs=[pl.BlockSpec((tile, w), lambda i: (i, 0)) for w in out_widths],
        out_shape=[jax.ShapeDtypeStruct((T, w), F32) for w in out_widths],
        compiler_params=_params("parallel"),
    )(*rows, *params, *consts)


def _row_bwd(name, f, out_widths, tile, rows, params, consts, gouts, dr_dtype=F32):
    T = rows[0].shape[0]
    nr, npar, ncon, nout = len(rows), len(params), len(consts), len(out_widths)

    def body(*refs):
        r = tuple(x[...] for x in refs[:nr])
        p = tuple(x[...] for x in refs[nr:nr + npar])
        c = tuple(x[...] for x in refs[nr + npar:nr + npar + ncon])
        k = nr + npar + ncon
        g = tuple(x[...] for x in refs[k:k + nout])
        dr_refs = refs[k + nout:k + nout + nr]
        dp_refs = refs[k + nout + nr:]
        _, vjp = jax.vjp(lambda r_, p_: tuple(f(r_, p_, c)), r, p)
        dr, dp = vjp(g)
        for ref, val in zip(dr_refs, dr, strict=True):
            ref[...] = val.astype(ref.dtype)
        if npar:
            @pl.when(pl.program_id(0) == 0)
            def _():
                for ref in dp_refs:
                    ref[...] = jnp.zeros_like(ref)

            for ref, val in zip(dp_refs, dp, strict=True):
                ref[...] += val

    outs = pl.pallas_call(
        body, name=name + "_bwd", grid=(T // tile,),
        in_specs=_row_specs(rows, params, consts, tile) + [pl.BlockSpec((tile, w), lambda i: (i, 0)) for w in out_widths],
        out_specs=[pl.BlockSpec((tile, r.shape[1]), lambda i: (i, 0)) for r in rows]
        + [pl.BlockSpec(p.shape, lambda i: (0, 0)) for p in params],
        out_shape=[jax.ShapeDtypeStruct(r.shape, dr_dtype) for r in rows] + [jax.ShapeDtypeStruct(p.shape, F32) for p in params],
        compiler_params=_params("arbitrary"),
    )(*rows, *params, *consts, *gouts)
    return tuple(outs[:nr]), tuple(outs[nr:])


def make_proj_stage(name, f, out_widths, tile=ROW_TILE):
    def run(x, w16, params, consts):
        p = _mm_nn(name + "_mm", x, w16)
        return p, tuple(_row_fwd(name, f, out_widths, tile, (p,), params, consts))

    @jax.custom_vjp
    def op(x, w16, c32, c16, params, consts):
        return run(x, w16, params, consts)[1]

    def fwd(x, w16, c32, c16, params, consts):
        p, outs = run(x, w16, params, consts)
        return outs, (x, w16, p, params, consts)

    def bwd(res, g):
        x, w16, p, params, consts = res
        (dp,), dparams = _row_bwd(name, f, out_widths, tile, (p,), params, consts, tuple(g), dr_dtype=BF16)
        d32, d16 = _weight_grads(name + "_dw", x, dp, True)
        return _mm_nt(name + "_dx", dp, w16), jnp.zeros_like(w16), d32, d16, dparams, tuple(jnp.zeros_like(c) for c in consts)

    op.defvjp(fwd, bwd)
    return op


def make_rowop(name, f, out_widths, tile=ROW_TILE):
    @jax.custom_vjp
    def op(rows, params, consts):
        return tuple(_row_fwd(name, f, out_widths, tile, rows, params, consts))

    def fwd(rows, params, consts):
        return op(rows, params, consts), (rows, params, consts)

    def bwd(res, g):
        rows, params, consts = res
        dr, dp = _row_bwd(name, f, out_widths, tile, rows, params, consts, tuple(g))
        return dr, dp, tuple(jnp.zeros_like(c) for c in consts)

    op.defvjp(fwd, bwd)
    return op


def _rms(x, g):
    return x * lax.rsqrt(jnp.mean(x * x, axis=-1, keepdims=True) + EPS) * g


def _silu(x):
    return x * jax.nn.sigmoid(x)


def _bdot(a, b):
    return jnp.dot(a.astype(BF16), b.astype(BF16), preferred_element_type=F32)


def norm_f(rows, params, consts):
    return (_rms(rows[0], params[0]),)


def addnorm_f(rows, params, consts):
    x = rows[0] + rows[1]
    return x, _rms(x, params[0])


EVEN_GLA_END = 1568


def even_prep_f(rows, params, consts):
    (p,) = rows
    wa2f, wa2b, baf, bab, lbf, lbb = params
    gq = p[:, 0:256]
    gk = p[:, 256:512] * (GLA_DK ** -0.5)
    gv = p[:, 512:1024]
    gr = p[:, 1024:1536]
    glaf = jax.nn.log_sigmoid(_bdot(p[:, 1536:1552], wa2f) + baf) / GLA_GATE_NORM
    glab = jax.nn.log_sigmoid(_bdot(p[:, 1552:1568], wa2b) + bab) / GLA_GATE_NORM
    o = EVEN_GLA_END
    hq = _silu(p[:, o:o + 256])

    def gate(z, lb):
        f = lb + (1.0 - lb) * jax.nn.sigmoid(z)
        return jnp.log(jnp.maximum(f, HGRN_MIN_F)), (1.0 - lb) * jax.nn.sigmoid(-z)

    hlaf, hkf = gate(p[:, o + 256:o + 512], lbf)
    hlab, hkb = gate(p[:, o + 512:o + 768], lbb)
    hv = p[:, o + 768:o + 1280]
    hg = p[:, o + 1280:o + 1792]
    return gq, gk, gv, gq, gk, gv, glaf, glab, gr, hq, hkf, hv, hq, hkb, hv, hlaf, hlab, hg


EVEN_PREP_WIDTHS = (256, 256, 512, 256, 256, 512, 256, 256, 512, 256, 256, 512, 256, 256, 512, 256, 256, 512)


def _head_rms(o, g, heads, d):
    parts = []
    for h in range(heads):
        seg = o[:, h * d:(h + 1) * d]
        parts.append(seg * lax.rsqrt(jnp.mean(seg * seg, axis=-1, keepdims=True) + EPS))
    return jnp.concatenate(parts, axis=1) * g


def even_post_f(rows, params, consts):
    of, ob, hof, hob, gr, hg = rows
    a = _head_rms(of + ob, params[0], GLA_HEADS, GLA_DV) * _silu(gr)
    b = _head_rms(hof + hob, params[1], HGRN_HEADS, HGRN_DV) * _silu(hg)
    return (jnp.concatenate([a, b], axis=1),)


@jax.custom_vjp
def _roll_half(x):
    return pltpu.roll(x, RET_DK // 2, 1)


_roll_half.defvjp(lambda x: (_roll_half(x), None), lambda _, g: (_roll_half(g),))


def odd_prep_f(rows, params, consts):
    (p,) = rows
    cosf, sinf = consts

    def rot(t):
        parts = []
        for h in range(RET_HEADS):
            th = t[:, h * RET_DK:(h + 1) * RET_DK]
            parts.append(th * cosf + _roll_half(th) * sinf)
        return jnp.concatenate(parts, axis=1)

    rq = rot(p[:, 0:512])
    rk = rot(p[:, 512:1024]) * (RET_DK ** -0.5)
    rv = p[:, 1024:1792]
    return rq, rk, rv, rq, rk, rv, p[:, 1792:2560], p[:, 2560:2816]


ODD_PREP_WIDTHS = (512, 512, 768, 512, 512, 768, 768, 256)


def ret_post_f(rows, params, consts):
    of, ob, rg = rows
    o = of + ob
    parts = []
    for h in range(RET_HEADS):
        seg = o[:, h * RET_DV:(h + 1) * RET_DV]
        c = seg - jnp.mean(seg, axis=-1, keepdims=True)
        parts.append(c * lax.rsqrt(jnp.mean(c * c, axis=-1, keepdims=True) + EPS))
    return (jnp.concatenate(parts, axis=1) * params[0] * _silu(rg),)


def s5_post_f(rows, params, consts):
    h0r, h0i, h1r, h1i, u = rows
    c_re, c_im, d_skip, glu_w, glu_b = params
    hr = h0r + h1r
    hi = h0i + h1i
    y = _bdot(hr, c_re) - _bdot(hi, c_im) + d_skip * u
    g = jax.nn.gelu(y)
    return (g * jax.nn.sigmoid(_bdot(g, glu_w) + glu_b),)


def loss_head(x, r, g, target, tile=ROW_TILE):
    T, D = x.shape

    def body(x_ref, r_ref, g_ref, t_ref, loss_ref, dx_ref, dg_ref):
        t = t_ref[...]

        def lf(xv, gv):
            e = _rms(xv, gv) - t
            row = jnp.sum(e * e, axis=-1, keepdims=True) * (0.5 / D)
            return jnp.sum(row, axis=0, keepdims=True)

        l, vjp = jax.vjp(lf, x_ref[...] + r_ref[...], g_ref[...])
        dx, dg = vjp(jnp.ones((1, 1), F32))
        dx_ref[...] = dx

        @pl.when(pl.program_id(0) == 0)
        def _():
            loss_ref[...] = jnp.zeros_like(loss_ref)
            dg_ref[...] = jnp.zeros_like(dg_ref)

        loss_ref[...] += jnp.broadcast_to(l, loss_ref.shape)
        dg_ref[...] += dg

    row = pl.BlockSpec((tile, D), lambda i: (i, 0))
    vec = pl.BlockSpec((1, D), lambda i: (0, 0))
    return pl.pallas_call(
        body, name="loss_head", grid=(T // tile,),
        in_specs=[row, row, vec, row],
        out_specs=[pl.BlockSpec((1, LANES), lambda i: (0, 0)), row, vec],
        out_shape=[jax.ShapeDtypeStruct((1, LANES), F32), jax.ShapeDtypeStruct((T, D), F32), jax.ShapeDtypeStruct((1, D), F32)],
        compiler_params=_params("arbitrary"),
    )(x, r, g, target)


SUBLANES = 8


def _halo_specs(width, tile, T):
    n8 = tile // SUBLANES
    last = T // SUBLANES - 1
    return [pl.BlockSpec((tile, width), lambda i: (i, 0)),
            pl.BlockSpec((SUBLANES, width), lambda i: (jnp.maximum(i * n8 - 1, 0), 0)),
            pl.BlockSpec((SUBLANES, width), lambda i: (jnp.minimum((i + 1) * n8, last), 0))]


def _shift_rows(x, prev_row, next_row, tile):
    row = lax.broadcasted_iota(jnp.int32, (tile, 1), 0)
    down = jnp.where(row == 0, prev_row, pltpu.roll(x, 1, 0))
    up = jnp.where(row == tile - 1, next_row, pltpu.roll(x, tile - 1, 0))
    return down, up


def _conv_fwd(name, u, cw, cb, S, tile):
    T, F2 = u.shape
    F = F2 // 2
    per_seq = S // tile

    def body(u_ref, up_ref, un_ref, cw_ref, cb_ref, g_ref):
        pos = pl.program_id(0) % per_seq
        uv = u_ref[...]
        prev_row = jnp.where(pos == 0, 0.0, up_ref[SUBLANES - 1:SUBLANES, :])
        next_row = jnp.where(pos == per_seq - 1, 0.0, un_ref[0:1, :])
        down, up = _shift_rows(uv, prev_row, next_row, tile)
        c = _conv_taps(down, uv, up, cw_ref, cb_ref)
        g_ref[...] = (_silu(c[:, :F]) * c[:, F:]).astype(BF16)

    return pl.pallas_call(
        body, name=name + "_fwd", grid=(T // tile,),
        in_specs=_halo_specs(F2, tile, T) + [pl.BlockSpec((3, F2), lambda i: (0, 0)), pl.BlockSpec((1, F2), lambda i: (0, 0))],
        out_specs=pl.BlockSpec((tile, F), lambda i: (i, 0)),
        out_shape=jax.ShapeDtypeStruct((T, F), BF16),
        compiler_params=_params("parallel"),
    )(u, u, u, cw, cb)


def _conv_taps(down, mid, up, cw_ref, cb_ref):
    c = cb_ref[...] + down * cw_ref[0:1, :]
    c = c + mid * cw_ref[1:2, :]
    return c + up * cw_ref[2:3, :]


def _conv_bwd(name, u, dg, cw, cb, S, tile):
    T, F2 = u.shape
    F = F2 // 2
    per_seq = S // tile

    def dact(cv, dgv):
        a, v = cv[:, :F], cv[:, F:]
        sg = jax.nn.sigmoid(a)
        return jnp.concatenate([dgv * v * (sg * (1.0 + a * (1.0 - sg))), dgv * (a * sg)], axis=1)

    def body(u_ref, up_ref, un_ref, g_ref, gp_ref, gn_ref, cw_ref, cb_ref, du_ref, dw0_ref, dw1_ref, dw2_ref, db_ref):
        i = pl.program_id(0)
        pos = i % per_seq
        first, last = pos == 0, pos == per_seq - 1
        lo, hi = slice(SUBLANES - 1, SUBLANES), slice(0, 1)
        uv = u_ref[...]
        u_m1, u_p1 = jnp.where(first, 0.0, up_ref[lo, :]), jnp.where(last, 0.0, un_ref[hi, :])
        u_dn, u_up = _shift_rows(uv, u_m1, u_p1, tile)
        dc = dact(_conv_taps(u_dn, uv, u_up, cw_ref, cb_ref), g_ref[...])
        c_m1 = _conv_taps(up_ref[SUBLANES - 2:SUBLANES - 1, :], u_m1, u_ref[0:1, :], cw_ref, cb_ref)
        c_p1 = _conv_taps(u_ref[tile - 1:tile, :], u_p1, un_ref[1:2, :], cw_ref, cb_ref)
        dc_prev = jnp.where(first, 0.0, dact(c_m1, gp_ref[lo, :]))
        dc_next = jnp.where(last, 0.0, dact(c_p1, gn_ref[hi, :]))
        dc_dn, dc_up = _shift_rows(dc, dc_prev, dc_next, tile)
        du = dc_up * cw_ref[0:1, :]
        du = du + dc * cw_ref[1:2, :]
        du_ref[...] = (du + dc_dn * cw_ref[2:3, :]).astype(BF16)

        @pl.when(i == 0)
        def _():
            for ref in (dw0_ref, dw1_ref, dw2_ref, db_ref):
                ref[...] = jnp.zeros_like(ref)

        dw0_ref[...] += jnp.sum(dc * u_dn, axis=0, keepdims=True)
        dw1_ref[...] += jnp.sum(dc * uv, axis=0, keepdims=True)
        dw2_ref[...] += jnp.sum(dc * u_up, axis=0, keepdims=True)
        db_ref[...] += jnp.sum(dc, axis=0, keepdims=True)

    vec = pl.BlockSpec((1, F2), lambda i: (0, 0))
    du, dw0, dw1, dw2, db = pl.pallas_call(
        body, name=name + "_bwd", grid=(T // tile,),
        in_specs=_halo_specs(F2, tile, T) + _halo_specs(F, tile, T) + [pl.BlockSpec((3, F2), lambda i: (0, 0)), vec],
        out_specs=[pl.BlockSpec((tile, F2), lambda i: (i, 0)), vec, vec, vec, vec],
        out_shape=[jax.ShapeDtypeStruct((T, F2), BF16)] + [jax.ShapeDtypeStruct((1, F2), F32)] * 4,
        compiler_params=_params("arbitrary"),
    )(u, u, u, dg, dg, dg, cw, cb)
    return du, jnp.concatenate([dw0, dw1, dw2], axis=0), db


def make_ffn(name, S):
    def run(x, wup16, wdn16, cw, cb):
        u = _mm_nn(name + "_up", x, wup16)
        g16 = _conv_fwd(name + "_conv", u, cw, cb, S, ROW_TILE)
        return u, g16, _mm_nn(name + "_down", g16, wdn16)

    @jax.custom_vjp
    def op(x, wup16, cu32, cu16, wdn16, cd32, cd16, cw, cb):
        return run(x, wup16, wdn16, cw, cb)[2], _tick()

    def fwd(x, wup16, cu32, cu16, wdn16, cd32, cd16, cw, cb):
        u, g16, out = run(x, wup16, wdn16, cw, cb)
        return (out, _tick()), (x, wup16, wdn16, u, g16, cw, cb)

    def bwd(res, g):
        x, wup16, wdn16, u, g16, cw, cb = res
        dout, after = g
        dg = _mm_nt(name + "_down_dx", dout, wdn16, after)
        dd32, dd16 = _weight_grads(name + "_down_dw", g16, dout, False)
        du16, dcw, dcb = _conv_bwd(name + "_conv", u, dg, cw, cb, S, ROW_TILE)
        du32, du16w = _weight_grads(name + "_up_dw", x, du16, True)
        return (_mm_nt(name + "_up_dx", du16, wup16), jnp.zeros_like(wup16), du32, du16w, jnp.zeros_like(wdn16), dd32, dd16, dcw, dcb)

    op.defvjp(fwd, bwd)
    return op


def _dot_nt(a, b):
    return lax.dot_general(a.astype(BF16), b.astype(BF16), (((1,), (1,)), ((), ())), preferred_element_type=F32)


def _dot_tn(a, b):
    return lax.dot_general(a.astype(BF16), b.astype(BF16), (((0,), (0,)), ((), ())), preferred_element_type=F32)


def _chunk_decays(la, direction, C, width, dk, const_lg):
    row = lax.broadcasted_iota(jnp.int32, (C, C), 0)
    col = lax.broadcasted_iota(jnp.int32, (C, C), 1)
    keep = (row >= col) if direction == 0 else (row <= col)
    ridx = lax.broadcasted_iota(jnp.int32, (C, 1), 0)
    if const_lg is None:
        cum = jnp.dot(keep.astype(F32), la, precision=HIGHEST, preferred_element_type=F32)
    else:
        lane_head = lax.broadcasted_iota(jnp.int32, (1, width), 1) // dk
        lg = jnp.zeros((1, width), F32)
        for h, val in enumerate(const_lg):
            lg = jnp.where(lane_head == h, val, lg)
        steps = (ridx + 1) if direction == 0 else (C - ridx)
        cum = steps.astype(F32) * lg
    exit_row = C - 1 if direction == 0 else 0
    mid = jnp.sum(jnp.where(ridx == C // 2, cum, 0.0), axis=0, keepdims=True)
    last = jnp.sum(jnp.where(ridx == exit_row, cum, 0.0), axis=0, keepdims=True)
    return keep, ridx == exit_row, cum, mid, last


def _scan_fwd(name, qkv, laf, lab, H, dk, dv, S, C, const_lg):
    qf, kf, vf, qb, kb, vb = qkv
    T = qf.shape[0]
    B, nc = T // S, S // C
    Wk, Wv = H * dk, H * dv
    learn = const_lg is None

    def body(*refs):
        @pl.when(pl.program_id(0) == 0)
        def _():
            refs[-1][...] = jnp.zeros_like(refs[-1])

        for b in range(B):
            one_sequence(*[r.at[b] for r in refs])

    def one_sequence(*refs):
        if learn:
            qf_r, qb_r, kf_r, kb_r, vf_r, vb_r, laf_r, lab_r, of_r, ob_r, sf_r, sb_r, st = refs
            las = (laf_r[...], lab_r[...])
        else:
            qf_r, qb_r, kf_r, kb_r, vf_r, vb_r, of_r, ob_r, sf_r, sb_r, st = refs
            las = (None, None)

        for d, (q_r, k_r, v_r, o_r, s_r) in enumerate(((qf_r, kf_r, vf_r, of_r, sf_r), (qb_r, kb_r, vb_r, ob_r, sb_r))):
            keep, _, cum, mid, last = _chunk_decays(las[d], d, C, Wk, dk, None if learn else const_lg[d])
            qe = q_r[...] * jnp.exp(cum - mid)
            ke = k_r[...] * jnp.exp(mid - cum)
            q_in = qe * jnp.exp(mid)
            k_out = ke * jnp.exp(last - mid)
            e_last = jnp.exp(last)
            vv = v_r[...]
            for h in range(H):
                ks, vs = slice(h * dk, (h + 1) * dk), slice(h * dv, (h + 1) * dv)
                a = jnp.where(keep, _dot_nt(qe[:, ks], ke[:, ks]), 0.0)
                state = st[d, h]
                o_r[:, vs] = _bdot(a, vv[:, vs]) + _dot_nt(q_in[:, ks], state)
                s_r[h * dv:(h + 1) * dv, :] = state
                st[d, h] = state * e_last[:, ks] + _dot_tn(vv[:, vs], k_out[:, ks])

    fpos = lambda c: c
    bpos = lambda c: nc - 1 - c
    kspec = lambda pos: pl.BlockSpec((B, C, Wk), lambda c: (0, pos(c), 0))
    vspec = lambda pos: pl.BlockSpec((B, C, Wv), lambda c: (0, pos(c), 0))
    sspec = lambda pos: pl.BlockSpec((B, None, Wv, dk), lambda c: (0, pos(c), 0, 0))
    seq = lambda t: t.reshape(B, S, t.shape[1])
    ins = [seq(t) for t in [qf, qb, kf, kb, vf, vb] + ([laf, lab] if learn else [])]
    in_specs = [kspec(fpos), kspec(bpos), kspec(fpos), kspec(bpos), vspec(fpos), vspec(bpos)] + ([kspec(fpos), kspec(bpos)] if learn else [])
    of, ob, sf, sb = pl.pallas_call(
        body, name=name + "_fwd", grid=(nc,), in_specs=in_specs,
        out_specs=[vspec(fpos), vspec(bpos), sspec(fpos), sspec(bpos)],
        out_shape=[jax.ShapeDtypeStruct((B, S, Wv), F32)] * 2 + [jax.ShapeDtypeStruct((B, nc, Wv, dk), F32)] * 2,
        scratch_shapes=[pltpu.VMEM((B, 2, H, dv, dk), F32)],
        compiler_params=_params("arbitrary"),
    )(*ins)
    return of.reshape(T, Wv), ob.reshape(T, Wv), sf, sb


def _scan_bwd(name, qkv, laf, lab, sf, sb, dof, dob, H, dk, dv, S, C, const_lg):
    qf, kf, vf, qb, kb, vb = qkv
    T = qf.shape[0]
    B, nc = T // S, S // C
    Wk, Wv = H * dk, H * dv
    learn = const_lg is None

    def body(*refs):
        @pl.when(pl.program_id(0) == 0)
        def _():
            refs[-1][...] = jnp.zeros_like(refs[-1])

        for b in range(B):
            one_sequence(*[r.at[b] for r in refs])

    def one_sequence(*refs):
        if learn:
            (qf_r, qb_r, kf_r, kb_r, vf_r, vb_r, laf_r, lab_r, sf_r, sb_r, dof_r, dob_r,
             dqf_r, dqb_r, dkf_r, dkb_r, dvf_r, dvb_r, dlaf_r, dlab_r, dst) = refs
            las, dlas = (laf_r[...], lab_r[...]), (dlaf_r, dlab_r)
        else:
            (qf_r, qb_r, kf_r, kb_r, vf_r, vb_r, sf_r, sb_r, dof_r, dob_r,
             dqf_r, dqb_r, dkf_r, dkb_r, dvf_r, dvb_r, dst) = refs
            las, dlas = (None, None), (None, None)

        groups = ((qf_r, kf_r, vf_r, sf_r, dof_r, dqf_r, dkf_r, dvf_r), (qb_r, kb_r, vb_r, sb_r, dob_r, dqb_r, dkb_r, dvb_r))
        for d, (q_r, k_r, v_r, s_r, do_r, dq_r, dk_r, dv_r) in enumerate(groups):
            keep, is_exit, cum, mid, last = _chunk_decays(las[d], d, C, Wk, dk, None if learn else const_lg[d])
            eq, ek = jnp.exp(cum - mid), jnp.exp(mid - cum)
            e_in, e_out, e_last = jnp.exp(mid), jnp.exp(last - mid), jnp.exp(last)
            qe, ke = q_r[...] * eq, k_r[...] * ek
            q_in, k_out = qe * e_in, ke * e_out
            vv, do = v_r[...], do_r[...]
            dqe_parts, dke_parts, dlast_parts = [], [], []
            for h in range(H):
                ks, vs = slice(h * dk, (h + 1) * dk), slice(h * dv, (h + 1) * dv)
                a = jnp.where(keep, _dot_nt(qe[:, ks], ke[:, ks]), 0.0)
                dp = jnp.where(keep, _dot_nt(do[:, vs], vv[:, vs]), 0.0)
                s_prev = s_r[h * dv:(h + 1) * dv, :]
                ds = dst[d, h]
                dk_out = _bdot(vv[:, vs], ds)
                dqe_parts.append(_bdot(dp, ke[:, ks]) + _bdot(do[:, vs], s_prev) * e_in[:, ks])
                dke_parts.append(_dot_tn(dp, qe[:, ks]) + dk_out * e_out[:, ks])
                dv_r[:, vs] = _dot_tn(a, do[:, vs]) + _dot_nt(k_out[:, ks], ds)
                if learn:
                    dlast_parts.append(jnp.sum(dk_out * k_out[:, ks], axis=0, keepdims=True)
                                       + jnp.sum(ds * s_prev, axis=0, keepdims=True) * e_last[:, ks])
                dst[d, h] = ds * e_last[:, ks] + _dot_tn(do[:, vs], q_in[:, ks])
            dqe = jnp.concatenate(dqe_parts, axis=1)
            dke = jnp.concatenate(dke_parts, axis=1)
            dq_r[...] = dqe * eq
            dk_r[...] = dke * ek
            if learn:
                dcum = dqe * qe - dke * ke + jnp.where(is_exit, jnp.concatenate(dlast_parts, axis=1), 0.0)
                dlas[d][...] = lax.dot_general(keep.astype(F32), dcum, (((0,), (0,)), ((), ())), precision=HIGHEST,
                                               preferred_element_type=F32)

    fpos = lambda c: nc - 1 - c
    bpos = lambda c: c
    kspec = lambda pos: pl.BlockSpec((B, C, Wk), lambda c: (0, pos(c), 0))
    vspec = lambda pos: pl.BlockSpec((B, C, Wv), lambda c: (0, pos(c), 0))
    sspec = lambda pos: pl.BlockSpec((B, None, Wv, dk), lambda c: (0, pos(c), 0, 0))
    seq = lambda t: t.reshape(B, S, t.shape[1])
    ins = [seq(t) for t in [qf, qb, kf, kb, vf, vb] + ([laf, lab] if learn else [])] + [sf, sb, seq(dof), seq(dob)]
    in_specs = ([kspec(fpos), kspec(bpos), kspec(fpos), kspec(bpos), vspec(fpos), vspec(bpos)]
                + ([kspec(fpos), kspec(bpos)] if learn else []) + [sspec(fpos), sspec(bpos), vspec(fpos), vspec(bpos)])
    out_specs = [kspec(fpos), kspec(bpos), kspec(fpos), kspec(bpos), vspec(fpos), vspec(bpos)] + ([kspec(fpos), kspec(bpos)] if learn else [])
    out_shape = ([jax.ShapeDtypeStruct((B, S, Wk), F32)] * 4 + [jax.ShapeDtypeStruct((B, S, Wv), F32)] * 2
                 + ([jax.ShapeDtypeStruct((B, S, Wk), F32)] * 2 if learn else []))
    outs = pl.pallas_call(
        body, name=name + "_bwd", grid=(nc,), in_specs=in_specs, out_specs=out_specs, out_shape=out_shape,
        scratch_shapes=[pltpu.VMEM((B, 2, H, dv, dk), F32)],
        compiler_params=_params("arbitrary"),
    )(*ins)
    return [t.reshape(T, t.shape[2]) for t in outs]


def make_scan(name, H, dk, dv, S, C, const_lg=None):
    if const_lg is None:
        @jax.custom_vjp
        def op(qkv, laf, lab):
            return tuple(_scan_fwd(name, qkv, laf, lab, H, dk, dv, S, C, None)[:2])

        def fwd(qkv, laf, lab):
            of, ob, sf, sb = _scan_fwd(name, qkv, laf, lab, H, dk, dv, S, C, None)
            return (of, ob), (qkv, laf, lab, sf, sb)

        def bwd(res, g):
            qkv, laf, lab, sf, sb = res
            dqf, dqb, dkf, dkb, dvf, dvb, dlaf, dlab = _scan_bwd(name, qkv, laf, lab, sf, sb, g[0], g[1], H, dk, dv, S, C, None)
            return (dqf, dkf, dvf, dqb, dkb, dvb), dlaf, dlab
    else:
        @jax.custom_vjp
        def op(qkv):
            return tuple(_scan_fwd(name, qkv, None, None, H, dk, dv, S, C, const_lg)[:2])

        def fwd(qkv):
            of, ob, sf, sb = _scan_fwd(name, qkv, None, None, H, dk, dv, S, C, const_lg)
            return (of, ob), (qkv, sf, sb)

        def bwd(res, g):
            qkv, sf, sb = res
            dqf, dqb, dkf, dkb, dvf, dvb = _scan_bwd(name, qkv, None, None, sf, sb, g[0], g[1], H, dk, dv, S, C, const_lg)
            return ((dqf, dkf, dvf, dqb, dkb, dvb),)

    op.defvjp(fwd, bwd)
    return op


def _reorder_call(name, t, S, to_segments):
    T, w = t.shape
    n_it = S // S5_SEGMENTS

    def body(x_ref, o_ref):
        def step(i, carry):
            packed = pl.ds(pl.multiple_of(i * S5_SEGMENTS, S5_SEGMENTS), S5_SEGMENTS)
            spread = pl.ds(i, S5_SEGMENTS, stride=n_it)
            if to_segments:
                o_ref[packed, :] = x_ref[spread, :]
            else:
                o_ref[spread, :] = x_ref[packed, :]
            return carry

        lax.fori_loop(0, n_it, step, 0, unroll=8)

    blk = pl.BlockSpec((S, LANES), lambda b, j: (b, j))
    return pl.pallas_call(body, name=name, grid=(T // S, w // LANES), in_specs=[blk], out_specs=blk,
                          out_shape=jax.ShapeDtypeStruct(t.shape, t.dtype), compiler_params=_params("parallel", "parallel"))(t)


def make_join(name, S, wa, wb):
    n_it = S // S5_SEGMENTS

    def call(tag, forward, wa, wb, arrs):
        T = arrs[0].shape[0]
        na, nb = wa // LANES, wb // LANES

        def body(*refs):
            j = pl.program_id(1)
            a_ref, b_ref, y_ref = (refs[0], refs[1], refs[2]) if forward else (refs[1], refs[2], refs[0])

            @pl.when(j < na)
            def _():
                if forward:
                    y_ref[...] = a_ref[...]
                else:
                    a_ref[...] = y_ref[...]

            @pl.when(j >= na)
            def _():
                def step(i, carry):
                    packed = pl.ds(pl.multiple_of(i * S5_SEGMENTS, S5_SEGMENTS), S5_SEGMENTS)
                    spread = pl.ds(i, S5_SEGMENTS, stride=n_it)
                    if forward:
                        y_ref[spread, :] = b_ref[packed, :]
                    else:
                        b_ref[packed, :] = y_ref[spread, :]
                    return carry

                lax.fori_loop(0, n_it, step, 0, unroll=8)

        a_spec = pl.BlockSpec((S, LANES), lambda b, j: (b, jnp.minimum(j, na - 1)))
        b_spec = pl.BlockSpec((S, LANES), lambda b, j: (b, jnp.maximum(j - na, 0)))
        y_spec = pl.BlockSpec((S, LANES), lambda b, j: (b, j))
        shapes = [jax.ShapeDtypeStruct((T, wa), F32), jax.ShapeDtypeStruct((T, wb), F32), jax.ShapeDtypeStruct((T, wa + wb), F32)]
        return pl.pallas_call(
            body, name=name + tag, grid=(T // S, na + nb),
            in_specs=[a_spec, b_spec] if forward else [y_spec], out_specs=y_spec if forward else [a_spec, b_spec],
            out_shape=shapes[2] if forward else shapes[:2], compiler_params=_params("parallel", "arbitrary"),
        )(*arrs)

    @jax.custom_vjp
    def op(a, b):
        return call("_fwd", True, wa, wb, (a, b))

    op.defvjp(lambda a, b: (call("_fwd", True, wa, wb, (a, b)), None), lambda _, dy: tuple(call("_bwd", False, wa, wb, (dy,))))
    return op


def make_reorder(name, S, to_segments):
    @jax.custom_vjp
    def op(t):
        return _reorder_call(name, t, S, to_segments)

    op.defvjp(lambda t: (_reorder_call(name, t, S, to_segments), None),
              lambda _, g: (_reorder_call(name + "_bwd", g, S, not to_segments),))
    return op


def _s5_scan_call(name, Xs, A, S, dirs, prev=None):
    with_p = prev is not None
    T, N = Xs[0].shape
    B, nl = T // S, N // LANES
    n_it = S // S5_SEGMENTS
    assert n_it & (n_it - 1) == 0

    def cmul(ar, ai, br, bi):
        return ar * br - ai * bi, ar * bi + ai * br

    def body(*refs):
        x, a_ref = refs[0:4], refs[4]
        if with_p:
            h_prev, x_prev, h, p_ref = refs[5:9], refs[9:13], refs[13:17], refs[17]
        else:
            h = refs[5:9]
        seg = lax.broadcasted_iota(jnp.int32, (S5_SEGMENTS, 1), 0)
        zero = jnp.zeros((S5_SEGMENTS, LANES), F32)
        a = [(jnp.broadcast_to(a_ref[2 * k:2 * k + 1, :], (S5_SEGMENTS, LANES)),
              jnp.broadcast_to(a_ref[2 * k + 1:2 * k + 2, :], (S5_SEGMENTS, LANES))) for k in range(2)]

        def rows_of(k, i):
            return pl.ds(pl.multiple_of(((n_it - 1 - i) if dirs[k] == 1 else i) * S5_SEGMENTS, S5_SEGMENTS), S5_SEGMENTS)

        def local(i, carry):
            out = []
            for k, (sr, si) in enumerate(carry):
                rows = rows_of(k, i)
                pr, pi = cmul(*a[k], sr, si)
                sr, si = pr + x[2 * k][rows, :], pi + x[2 * k + 1][rows, :]
                h[2 * k][rows, :] = sr
                h[2 * k + 1][rows, :] = si
                out.append((sr, si))
            return tuple(out)

        ends = lax.fori_loop(0, n_it, local, ((zero, zero), (zero, zero)), unroll=8)
        inherit = []
        for k, (er, ei) in enumerate(ends):
            back = dirs[k] == 1
            pr, pi = a[k]
            for _ in range(n_it.bit_length() - 1):
                pr, pi = cmul(pr, pi, pr, pi)
            shift = (S5_SEGMENTS - 1) if back else 1
            tr, ti = er, ei
            for r in (range(S5_SEGMENTS - 2, -1, -1) if back else range(1, S5_SEGMENTS)):
                nr, ni = cmul(pr, pi, pltpu.roll(tr, shift, 0), pltpu.roll(ti, shift, 0))
                tr = jnp.where(seg == r, er + nr, tr)
                ti = jnp.where(seg == r, ei + ni, ti)
            edge = (S5_SEGMENTS - 1) if back else 0
            inherit.append((jnp.where(seg == edge, 0.0, pltpu.roll(tr, shift, 0)), jnp.where(seg == edge, 0.0, pltpu.roll(ti, shift, 0))))

        def fix(i, carry):
            powers, sums = carry
            new_powers, new_sums = [], []
            for k in range(2):
                rows = rows_of(k, i)
                fr, fi = cmul(*powers[k], *inherit[k])
                sr, si = h[2 * k][rows, :] + fr, h[2 * k + 1][rows, :] + fi
                h[2 * k][rows, :] = sr
                h[2 * k + 1][rows, :] = si
                new_powers.append(cmul(*powers[k], *a[k]))
                if with_p:
                    ur = h_prev[2 * k][rows, :] - x_prev[2 * k][rows, :]
                    ui = h_prev[2 * k + 1][rows, :] - x_prev[2 * k + 1][rows, :]
                    new_sums.append((sums[k][0] + sr * ur + si * ui, sums[k][1] + si * ur - sr * ui))
            return tuple(new_powers), tuple(new_sums)

        _, sums = lax.fori_loop(0, n_it, fix, ((a[0], a[1]), ((zero, zero), (zero, zero)) if with_p else ()), unroll=8)
        if with_p:
            for k in range(2):
                p_ref[2 * k:2 * k + 1, :] = jnp.sum(sums[k][0], axis=0, keepdims=True)
                p_ref[2 * k + 1:2 * k + 2, :] = jnp.sum(sums[k][1], axis=0, keepdims=True)

    col = pl.BlockSpec((S, LANES), lambda b, j: (b, j))
    outs = pl.pallas_call(
        body, name=name, grid=(B, nl),
        in_specs=[col] * 4 + [pl.BlockSpec((4, LANES), lambda b, j: (0, j))] + ([col] * 8 if with_p else []),
        out_specs=[col] * 4 + ([pl.BlockSpec((None, 4, LANES), lambda b, j: (b, 0, j))] if with_p else []),
        out_shape=[jax.ShapeDtypeStruct((T, N), F32)] * 4 + ([jax.ShapeDtypeStruct((B, 4, N), F32)] if with_p else []),
        compiler_params=_params("parallel", "parallel"),
    )(*Xs, A, *(prev[0] + prev[1] if with_p else ()))
    return (tuple(outs[:4]), outs[4]) if with_p else tuple(outs)


def make_s5_scan(name, S):
    @jax.custom_vjp
    def op(X, A):
        return _s5_scan_call(name + "_fwd", X, A, S, (0, 1))

    def fwd(X, A):
        H = _s5_scan_call(name + "_fwd", X, A, S, (0, 1))
        return H, (X, A, H)

    def bwd(res, G):
        X, A, H = res
        conj = A * jnp.array([[1.0], [-1.0], [1.0], [-1.0]], F32)
        lam, P = _s5_scan_call(name + "_bwd", tuple(G), conj, S, (1, 0), prev=(tuple(H), tuple(X)))
        P = jnp.sum(P, axis=0)
        ar, ai = A[0::2], A[1::2]
        pr, pi = P[0::2], P[1::2]
        den = ar * ar + ai * ai
        dar, dai = (pr * ar - pi * ai) / den, (pr * ai + pi * ar) / den
        return lam, jnp.stack([dar[0], dai[0], dar[1], dai[1]], axis=0)

    op.defvjp(fwd, bwd)
    return op


ANY = pl.BlockSpec(memory_space=pl.ANY)


def _place():
    x, y, c = lax.axis_index("x"), lax.axis_index("y"), lax.axis_index("c")
    return x, y, c, [(1 - x, y), (x, 1 - y), (1 - x, 1 - y)]


def all_gather(name, arrs):
    n = len(arrs)

    def body(*refs):
        ins, outs = refs[:n], refs[n:2 * n]
        send, recv, lsem = refs[2 * n:]
        x, y, c, chips = _place()
        me, sibling = (x, y, c), (x, y, 1 - c)

        def copy(a, k, block, to, src=None):
            slot = outs[a].at[4 * block[0] + 2 * block[1] + block[2]]
            return pltpu.make_async_remote_copy(src_ref=slot if src is None else src, dst_ref=slot, send_sem=send.at[a, k],
                                                recv_sem=recv.at[a, k], device_id=to, device_id_type=MESH)

        mine = [pltpu.make_async_copy(ins[a], outs[a].at[4 * x + 2 * y + c], lsem.at[a]) for a in range(n)]
        first = []
        for a in range(n):
            mine[a].start()
            first.append(copy(a, 0, me, sibling, src=ins[a]))
            first += [copy(a, 1 + j, me, (*chip, c), src=ins[a]) for j, chip in enumerate(chips)]
        for cp in first:
            cp.start()
        passed = []
        for a in range(n):
            for j, chip in enumerate(chips):
                copy(a, 1 + j, (*chip, c), me).wait_recv()
                fwd = copy(a, 4 + j, (*chip, c), sibling)
                fwd.start()
                passed.append(fwd)
        for a in range(n):
            copy(a, 0, sibling, me).wait_recv()
            for j, chip in enumerate(chips):
                copy(a, 4 + j, (*chip, 1 - c), me).wait_recv()
        for cp in first + passed:
            cp.wait_send()
        for cp in mine:
            cp.wait()

    return pl.pallas_call(
        body, name=name, in_specs=[ANY] * n, out_specs=[ANY] * n,
        out_shape=[jax.ShapeDtypeStruct((N_DEV,) + a.shape, a.dtype) for a in arrs],
        scratch_shapes=[pltpu.SemaphoreType.DMA((n, 7)), pltpu.SemaphoreType.DMA((n, 7)), pltpu.SemaphoreType.DMA((n,))],
    )(*arrs)


HBM = pl.BlockSpec(memory_space=pltpu.HBM)
SEM = pl.BlockSpec(memory_space=pltpu.SEMAPHORE)
EFFECT = pltpu.SideEffectType.DATAFLOW_SIDE_EFFECTING
GATHER_PEERS = (1, 2, 4, 6)
OTHER_CHIPS = (2, 4, 6)
COPIES_PER_ARRAY = {"scatter": N_DEV - 1, "gather": len(GATHER_PEERS), "forward": len(OTHER_CHIPS)}


def _split_plan(mode, srcs, lands, send, recv):
    x, y, c = lax.axis_index("x"), lax.axis_index("y"), lax.axis_index("c")

    def dev(k):
        return (1 - x if k & 4 else x), (1 - y if k & 2 else y), (1 - c if k & 1 else c)

    def idx(d):
        return 4 * d[0] + 2 * d[1] + d[2]

    me = idx((x, y, c))
    plan = []
    for a, land in enumerate(lands):
        if mode == "scatter":
            legs = [(srcs[a].at[idx(dev(k))], land.at[me], land.at[idx(dev(k))], dev(k)) for k in range(1, N_DEV)]
        elif mode == "gather":
            legs = [(srcs[a], land.at[me], land.at[idx(dev(k))], dev(k)) for k in GATHER_PEERS]
        else:
            legs = [(land.at[idx(dev(j))], land.at[idx(dev(j))], land.at[idx(dev(j ^ 1))], dev(1)) for j in OTHER_CHIPS]
        for i, (src, dst, arrival, to) in enumerate(legs):
            sem = a * len(legs) + i
            pair = tuple(pltpu.make_async_remote_copy(src_ref=src, dst_ref=d, send_sem=send.at[sem], recv_sem=recv.at[sem],
                                                      device_id=to, device_id_type=MESH) for d in (dst, arrival))
            plan.append(pair)
    return plan


def split_start(name, mode, srcs, lands, after):
    if lands is None:
        lands = [lax.empty((N_DEV,) + (s.shape[1:] if mode == "scatter" else s.shape), s.dtype) for s in srcs]
    ns, nl = len(srcs), len(lands)
    nsem = COPIES_PER_ARRAY[mode] * nl

    def body(*refs):
        ins, lnd = refs[:ns], refs[ns:ns + nl]
        send, recv = refs[ns + nl + 1], refs[ns + nl + 2]
        token = refs[-1]
        for out, _ in _split_plan(mode, ins, lnd, send, recv):
            out.start()
        token[...] = jnp.zeros_like(token)

    arrs = list(srcs) + list(lands)
    return pl.pallas_call(
        body, name=name,
        out_shape=(pltpu.SemaphoreType.DMA((nsem,)), pltpu.SemaphoreType.DMA((nsem,)))
        + tuple(pltpu.HBM(t.shape, t.dtype) for t in arrs) + (jax.ShapeDtypeStruct((SUBLANES, LANES), F32),),
        in_specs=[HBM] * len(arrs) + [ANY],
        out_specs=(SEM, SEM) + (HBM,) * len(arrs) + (pl.BlockSpec(memory_space=pltpu.VMEM),),
        input_output_aliases={i: 2 + i for i in range(len(arrs))},
        compiler_params=pltpu.CompilerParams(has_side_effects=EFFECT),
    )(*[pltpu.with_memory_space_constraint(t, pltpu.HBM) for t in arrs], after)


def split_wait(name, mode, handle, after):
    send, recv = handle[0], handle[1]
    arrs = list(handle[2:-1])
    nl = len(arrs) if mode == "forward" else len(arrs) // 2
    ns = len(arrs) - nl

    def body(*refs):
        ins, lnd = refs[:ns], refs[ns:ns + nl]
        s, r = refs[ns + nl], refs[ns + nl + 1]
        for out, arrival in _split_plan(mode, ins, lnd, s, r):
            out.wait_send()
            arrival.wait_recv()

    outs = pl.pallas_call(
        body, name=name,
        out_shape=tuple(pltpu.HBM(t.shape, t.dtype) for t in arrs),
        in_specs=[HBM] * len(arrs) + [SEM, SEM, ANY], out_specs=(HBM,) * len(arrs),
        input_output_aliases={i: i for i in range(len(arrs))},
        compiler_params=pltpu.CompilerParams(has_side_effects=EFFECT),
    )(*arrs, send, recv, after)
    return list(outs[ns:])


def _row_tile(rows, cols):
    cap = max(SUBLANES, (2**18 // cols) // SUBLANES * SUBLANES)
    if rows <= cap:
        return rows
    for t in range(cap, SUBLANES - 1, -SUBLANES):
        if rows % t == 0:
            return t
    return rows


def ordered_sum(name, parts, own, me):
    n, R, C = parts.shape
    rt = _row_tile(R, C)

    def body(me_ref, p_ref, own_ref, o_ref):
        s = None
        for k in range(n):
            t = jnp.where(me_ref[0] == k, own_ref[...], p_ref[k])
            s = t if s is None else s + t
        o_ref[...] = s

    return pl.pallas_call(
        body, name=name,
        grid_spec=pltpu.PrefetchScalarGridSpec(
            num_scalar_prefetch=1, grid=(R // rt,),
            in_specs=[pl.BlockSpec((n, rt, C), lambda r, p: (0, r, 0)), pl.BlockSpec((rt, C), lambda r, p: (r, 0))],
            out_specs=pl.BlockSpec((rt, C), lambda r, p: (r, 0))),
        out_shape=jax.ShapeDtypeStruct((R, C), F32), compiler_params=_params("parallel"),
    )(me, parts, own)


def _adamw_update(w, m, v, g):
    bias1 = 1.0 - ADAM_B1 ** ADAM_STEP
    bias2 = 1.0 - ADAM_B2 ** ADAM_STEP
    m_new = ADAM_B1 * m + (1.0 - ADAM_B1) * g
    v_new = ADAM_B2 * v + (1.0 - ADAM_B2) * (g * g)
    delta = -ADAM_LR * ((m_new / bias1) / (jnp.sqrt(v_new / bias2) + ADAM_EPS) + ADAM_WD * w)
    return delta, m_new, v_new


def adamw(name, w, m, v, g):
    R, C = w.shape
    rt = _row_tile(R, C)

    def body(w_ref, m_ref, v_ref, g_ref, d_ref, mo_ref, vo_ref):
        d_ref[...], mo_ref[...], vo_ref[...] = _adamw_update(w_ref[...], m_ref[...], v_ref[...], g_ref[...])

    row = pl.BlockSpec((rt, C), lambda r: (r, 0))
    return pl.pallas_call(
        body, name=name, grid=(R // rt,), in_specs=[row] * 4, out_specs=[row] * 3,
        out_shape=[jax.ShapeDtypeStruct((R, C), F32)] * 3, compiler_params=_params("parallel"),
    )(w, m, v, g)


def adamw_sharded(name, layer, w, m, v, own, land, me, prev, transposed):
    _, R, C = own.shape
    rt = _row_tile(R, C)

    def body(me_ref, w_ref, m_ref, v_ref, own_ref, land_ref, *rest):
        go_ref, d_ref, mo_ref, vo_ref = rest[-4:]
        g = own_ref[...]
        for k in range(N_DEV):
            g = g + jnp.where(me_ref[0] == k, 0.0, land_ref[k].astype(F32))
        if transposed:
            g = g.T
        go_ref[...] = g
        d_ref[...], mo_ref[...], vo_ref[...] = _adamw_update(w_ref[...], m_ref[...], v_ref[...], g)

    if transposed:
        row = pl.BlockSpec((None, C, rt), lambda r, p: (layer, 0, r))
    else:
        row = pl.BlockSpec((None, rt, C), lambda r, p: (layer, r, 0))
    in_specs = [row, row, row, pl.BlockSpec((None, rt, C), lambda r, p: (p[0], r, 0)), pl.BlockSpec((N_DEV, rt, C), lambda r, p: (0, r, 0))]
    ins = [me, w, m, v, own, land]
    aliases = {}
    if prev is not None:
        in_specs += [ANY] * 4
        aliases = {len(ins) + k: k for k in range(4)}
        ins += list(prev)
    return pl.pallas_call(
        body, name=name,
        grid_spec=pltpu.PrefetchScalarGridSpec(num_scalar_prefetch=1, grid=(R // rt,), in_specs=in_specs, out_specs=[row] * 4),
        out_shape=[jax.ShapeDtypeStruct(w.shape, F32)] * 4, input_output_aliases=aliases,
        compiler_params=_params("arbitrary"),
    )(*ins)


def _hgrn_lower_bounds(lb_logits):
    p = jax.nn.softmax(lb_logits, axis=1)
    return jnp.cumsum(p, axis=1) - p[:, :1]


def _s5_discretise(lam_re, lam_im, log_dt, b_re, b_im):
    lr = jnp.minimum(lam_re, -1e-4)
    li = lam_im
    dt = jnp.exp(log_dt)[:, None]
    mag = jnp.exp(lr * dt)
    ar, ai = mag * jnp.cos(li * dt), mag * jnp.sin(li * dt)
    den = lr * lr + li * li
    nr = ar - 1.0
    cr = (nr * lr + ai * li) / den
    ci = (ai * lr - nr * li) / den
    bbr = cr[..., None] * b_re - ci[..., None] * b_im
    bbi = cr[..., None] * b_im + ci[..., None] * b_re
    return ar, ai, bbr, bbi


def _block_diag(t):
    G, a, b = t.shape
    eye = jnp.eye(G, dtype=F32)
    return (t[:, :, None, :] * eye[:, None, :, None]).reshape(G * a, G * b)


def _rope_tables(S):
    half = RET_DK // 2
    inv = ROPE_BASE ** (-jnp.arange(half, dtype=F32) / half)
    ang = jnp.arange(S, dtype=F32)[:, None] * inv[None, :]
    cos, sin = jnp.cos(ang), jnp.sin(ang)
    return jnp.concatenate([cos, cos], axis=1), jnp.concatenate([-sin, sin], axis=1)


def _ret_log_decays():
    f = tuple(float(np.log1p(-np.exp2(np.float32(-5.0 - h)))) for h in range(RET_HEADS))
    b = tuple(float(np.log1p(-np.exp2(np.float32(-5.5 - h)))) for h in range(RET_HEADS))
    return f, b


def assemble_weight(name, land, own, me):
    _, R, C = land.shape
    tr = min(R, 256)

    def body(me_ref, land_ref, own_ref, o_ref):
        for d in range(N_DEV):
            o_ref[:, d * C:(d + 1) * C] = jnp.where(me_ref[0] == d, own_ref[...], land_ref[d])

    return pl.pallas_call(
        body, name=name,
        grid_spec=pltpu.PrefetchScalarGridSpec(
            num_scalar_prefetch=1, grid=(R // tr,),
            in_specs=[pl.BlockSpec((N_DEV, tr, C), lambda i, p: (0, i, 0)), pl.BlockSpec((tr, C), lambda i, p: (i, 0))],
            out_specs=pl.BlockSpec((tr, N_DEV * C), lambda i, p: (i, 0))),
        out_shape=jax.ShapeDtypeStruct((R, N_DEV * C), land.dtype), compiler_params=_params("parallel"),
    )(me, land, own)


def _row(t):
    return t.reshape(1, -1)


def mixer_stage(layer, S, resid, branch, w_in, w_out, carriers, small):
    j = layer // 2
    tag = f"l{layer}"
    g = _row(small["mix_norm_g"][layer])
    if branch is None:
        x = resid
        (h,) = make_rowop(tag + "_norm", norm_f, (D_MODEL,))((x,), (g,), ())
    else:
        x, h = make_rowop(tag + "_addnorm", addnorm_f, (D_MODEL, D_MODEL))((resid, branch), (g,), ())
    if layer % 2 == 0:
        lbs = _hgrn_lower_bounds(small["hgrn_lb_logits"])
        prm = (small["gla_wa2"][j, 0], small["gla_wa2"][j, 1], _row(small["gla_ba"][j, 0]), _row(small["gla_ba"][j, 1]),
               _row(lbs[0, j]), _row(lbs[1, j]))
        outs = make_proj_stage(tag + "_prep", even_prep_f, EVEN_PREP_WIDTHS)(h, w_in, carriers["in32"], carriers["in16"], prm, ())
        gla_qkv, (glaf, glab, gr), hgrn_qkv, (hlaf, hlab, hg) = outs[0:6], outs[6:9], outs[9:15], outs[15:18]
        of, ob = make_scan(tag + "_gla", GLA_HEADS, GLA_DK, GLA_DV, S, SCAN_CHUNK)(gla_qkv, glaf, glab)
        hof, hob = make_scan(tag + "_hgrn", HGRN_HEADS, HGRN_DK, HGRN_DV, S, HGRN_SCAN_CHUNK)(hgrn_qkv, hlaf, hlab)
        (y,) = make_rowop(tag + "_post", even_post_f, (D_MODEL,))(
            (of, ob, hof, hob, gr, hg), (_row(small["gla_norm_g"][j]), _row(small["hgrn_norm_g"][j])), ())
    else:
        cosf, sinf = _rope_tables(S)
        outs = make_proj_stage(tag + "_prep", odd_prep_f, ODD_PREP_WIDTHS)(h, w_in, carriers["in32"], carriers["in16"], (), (cosf, sinf))
        ret_qkv, rg, su = outs[0:6], outs[6], outs[7]
        of, ob = make_scan(tag + "_ret", RET_HEADS, RET_DK, RET_DV, S, SCAN_CHUNK, const_lg=_ret_log_decays())(ret_qkv)
        (cm,) = make_rowop(tag + "_retpost", ret_post_f, (RET_HEADS * RET_DV,))((of, ob, rg), (_row(small["ret_norm_g"][j]),), ())
        disc = [_s5_discretise(small["s5_lam_re"][j, d], small["s5_lam_im"][j, d], small["s5_log_dt"][j, d],
                               small["s5_b_re"][j], small["s5_b_im"][j]) for d in range(2)]
        a4 = jnp.stack([t.reshape(-1) for d in range(2) for t in disc[d][:2]], axis=0)
        su_p = make_reorder(tag + "_s5seg", S, True)(su)
        bd = jnp.concatenate([_block_diag(jnp.swapaxes(disc[d][2 + i], 1, 2)) for d in range(2) for i in range(2)], axis=1)
        Xs = make_mm_groups(tag + "_s5in", 4)(su_p, bd)
        Hs = make_s5_scan(tag + "_s5scan", S)(Xs, a4)
        prm = (_block_diag(jnp.swapaxes(small["s5_c_re"][j], 1, 2)), _block_diag(jnp.swapaxes(small["s5_c_im"][j], 1, 2)),
               _row(small["s5_d"][j]), small["s5_glu_w"][j], _row(small["s5_glu_b"][j]))
        (dm_p,) = make_rowop(tag + "_s5post", s5_post_f, (S5_WIDTH,))((*Hs, su_p), prm, ())
        y = make_join(tag + "_join", S, RET_HEADS * RET_DV, S5_WIDTH)(cm, dm_p)
    return (x, *make_mm(tag + "_out")(y, w_out, carriers["out32"], carriers["out16"]))


def ffn_stage(layer, S, resid, branch, w_up, w_down, carriers, small):
    tag = f"l{layer}"
    x, hf = make_rowop(tag + "_ffnnorm", addnorm_f, (D_MODEL, D_MODEL))((resid, branch), (_row(small["ffn_norm_g"][layer]),), ())
    out, tick = make_ffn(tag + "_ffn", S)(hf, w_up, carriers["up32"], carriers["up16"], w_down, carriers["down32"], carriers["down16"],
                                         small["ffn_conv_w"][layer], _row(small["ffn_conv_b"][layer]))
    return x, out, tick


BIG = {"w_in_even": 2, "w_out_even": 1, "w_in_odd": 2, "w_out_odd": 1, "ffn_w_up": 2, "ffn_w_down": 1}
SMALL_SHARDED = {"gla_wa2": 3, "gla_ba": 2, "hgrn_lb_logits": 2, "ret_norm_g": 1, "s5_d": 1, "s5_glu_w": 1, "s5_glu_b": 1,
                 "ffn_conv_w": 2}
REPLICATED = ("mix_norm_g", "ffn_norm_g", "final_norm_g", "gla_norm_g", "hgrn_norm_g", "s5_lam_re", "s5_lam_im", "s5_log_dt",
              "s5_b_re", "s5_b_im", "s5_c_re", "s5_c_im", "ffn_conv_b")
WEIGHTS = ("mix_norm_g", "ffn_norm_g", "final_norm_g", "w_in_even", "w_out_even", "gla_wa2", "gla_ba", "gla_norm_g",
           "hgrn_lb_logits", "hgrn_norm_g", "w_in_odd", "w_out_odd", "ret_norm_g", "s5_lam_re", "s5_lam_im", "s5_log_dt",
           "s5_b_re", "s5_b_im", "s5_c_re", "s5_c_im", "s5_d", "s5_glu_w", "s5_glu_b", "ffn_w_up", "ffn_conv_w", "ffn_conv_b",
           "ffn_w_down")
PACK_COLS = LANES
MIXER_SMALL = (("mix_norm_g", "hgrn_lb_logits", "gla_wa2", "gla_ba", "gla_norm_g", "hgrn_norm_g"),
               ("mix_norm_g", "ret_norm_g", "s5_lam_re", "s5_lam_im", "s5_log_dt", "s5_b_re", "s5_b_im", "s5_c_re", "s5_c_im",
                "s5_d", "s5_glu_w", "s5_glu_b"))
FFN_SMALL = ("ffn_norm_g", "ffn_conv_w", "ffn_conv_b")


def _unshard(g, axis):
    t = jnp.moveaxis(g, 0, axis)
    return t.reshape(t.shape[:axis] + (t.shape[axis] * t.shape[axis + 1],) + t.shape[axis + 2:])


def _pack_rows(shape):
    return -(-int(np.prod(shape)) // (SUBLANES * PACK_COLS)) * SUBLANES


def _pack(arrs):
    parts = []
    for a in arrs:
        rows = _pack_rows(a.shape)
        parts.append(jnp.pad(a.reshape(-1), (0, rows * PACK_COLS - a.size)).reshape(rows, PACK_COLS))
    return jnp.concatenate(parts, axis=0)


def _unpack(packed, shapes):
    lead = packed.shape[:-2]
    out, r = [], 0
    for s in shapes:
        rows, n = _pack_rows(s), int(np.prod(s))
        piece = packed[..., r:r + rows, :].reshape(lead + (rows * PACK_COLS,))
        out.append(piece[..., :n].reshape(lead + tuple(s)))
        r += rows
    return out


def kernel(x, mix_norm_g, ffn_norm_g, final_norm_g, w_in_even, w_out_even, gla_wa2, gla_ba, gla_norm_g, hgrn_lb_logits, hgrn_norm_g, w_in_odd, w_out_odd, ret_norm_g, s5_lam_re, s5_lam_im, s5_log_dt, s5_b_re, s5_b_im, s5_c_re, s5_c_im, s5_d, s5_glu_w, s5_glu_b, ffn_w_up, ffn_conv_w, ffn_conv_b, ffn_w_down, loss_target, m_mix_norm_g, m_ffn_norm_g, m_final_norm_g, m_w_in_even, m_w_out_even, m_gla_wa2, m_gla_ba, m_gla_norm_g, m_hgrn_lb_logits, m_hgrn_norm_g, m_w_in_odd, m_w_out_odd, m_ret_norm_g, m_s5_lam_re, m_s5_lam_im, m_s5_log_dt, m_s5_b_re, m_s5_b_im, m_s5_c_re, m_s5_c_im, m_s5_d, m_s5_glu_w, m_s5_glu_b, m_ffn_w_up, m_ffn_conv_w, m_ffn_conv_b, m_ffn_w_down, v_mix_norm_g, v_ffn_norm_g, v_final_norm_g, v_w_in_even, v_w_out_even, v_gla_wa2, v_gla_ba, v_gla_norm_g, v_hgrn_lb_logits, v_hgrn_norm_g, v_w_in_odd, v_w_out_odd, v_ret_norm_g, v_s5_lam_re, v_s5_lam_im, v_s5_log_dt, v_s5_b_re, v_s5_b_im, v_s5_c_re, v_s5_c_im, v_s5_d, v_s5_glu_w, v_s5_glu_b, v_ffn_w_up, v_ffn_conv_w, v_ffn_conv_b, v_ffn_w_down):
    args = locals()
    w = {n: args[n] for n in WEIGHTS}
    m = {n: args["m_" + n] for n in WEIGHTS}
    v = {n: args["v_" + n] for n in WEIGHTS}
    Bl, S, D = x.shape
    T = Bl * S
    ix, iy, ic = lax.axis_index("x"), lax.axis_index("y"), lax.axis_index("c")
    me = 4 * ix + 2 * iy + ic

    xt = x.reshape(T, D)
    me1 = me.reshape(1).astype(jnp.int32)
    stages = []
    for layer in range(DEPTH):
        j = layer // 2
        kin, kout = ("w_in_even", "w_out_even") if layer % 2 == 0 else ("w_in_odd", "w_out_odd")
        stages.append((mixer_stage, layer, ("in", "out"), ((kin, j, True), (kout, j, False)), MIXER_SMALL[layer % 2]))
        stages.append((ffn_stage, layer, ("up", "down"), (("ffn_w_up", layer, True), ("ffn_w_down", layer, False)), FFN_SMALL))

    gather, after = [], xt
    for s, (_, _, _, projs, _) in enumerate(stages):
        handle = split_start(f"gather{s}_start", "gather", [w[n][l].astype(BF16) for n, l, _ in projs], None, after)
        gather.append(handle)
        after = handle[-1]
    sm_names = list(SMALL_SHARDED)
    (sm_all8,) = all_gather("gather_small", [_pack([w[n] for n in sm_names])])
    small = {n: _unshard(t, SMALL_SHARDED[n]) for n, t in zip(sm_names, _unpack(sm_all8, [w[n].shape for n in sm_names]))}
    small.update({n: w[n] for n in REPLICATED})
    small["mix_norm_g"] = small["mix_norm_g"] + after[0, 0]

    resid, branch, pulls = xt, None, []
    second = [None] * len(stages)

    def second_level(s, after):
        lands = split_wait(f"gather{s}_wait", "gather", gather[s], after)
        second[s] = split_start(f"forward{s}_start", "forward", [], lands, after)

    for s, (fn, layer, keys, projs, sm_keys) in enumerate(stages):
        here = lax.stop_gradient(resid)
        if second[s] is None:
            second_level(s, here)
        lands = split_wait(f"forward{s}_wait", "forward", second[s], second[s][-1])
        sm = {n: small[n] for n in sm_keys}
        if 2 <= s < len(stages) - 1:
            second_level(s + 1, here)
            norm = "mix_norm_g" if fn is mixer_stage else "ffn_norm_g"
            sm[norm] = sm[norm] + second[s + 1][-1][0, 0]
        full, carriers = [], {}
        for key, land, (n, l, col) in zip(keys, lands, projs, strict=True):
            if col:
                full.append(assemble_weight(f"weight{s}_{key}", land, w[n][l].astype(BF16), me1))
            else:
                blocks = lax.dynamic_update_index_in_dim(land, w[n][l].astype(BF16), me, 0)
                full.append(blocks.reshape(N_DEV * land.shape[1], land.shape[2]))
            carriers[key + "32"] = jnp.zeros(land.shape, F32)
            carriers[key + "16"] = jnp.zeros(land.shape, BF16)
        run = functools.partial(fn, layer, S)
        if branch is None:
            (resid, branch, _), pull = jax.vjp(lambda r, c, p, run=run, full=full: run(r, None, full[0], full[1], c, p), resid, carriers, sm)
        else:
            (resid, branch, _), pull = jax.vjp(lambda r, b, c, p, run=run, full=full: run(r, b, full[0], full[1], c, p), resid, branch, carriers, sm)
        pulls.append(pull)

    loss_acc, dxf, dgf = loss_head(resid, branch, small["final_norm_g"].reshape(1, D), loss_target.reshape(T, D))
    loss = lax.psum(loss_acc[0, 0], ("x", "y", "c"))
    g_small = {"final_norm_g": dgf.reshape(D)}
    d_resid, d_branch, token = dxf, dxf, _tick()
    scatter, own32 = [None] * len(stages), [None] * len(stages)
    for s in reversed(range(len(stages))):
        if s == 0:
            d_resid, dcar, dsm = pulls[s]((d_resid, d_branch, token))
        else:
            d_resid, d_branch, dcar, dsm = pulls[s]((d_resid, d_branch, token))
        for n, g in dsm.items():
            g_small[n] = g_small[n] + g if n in g_small else g
        keys = stages[s][2]
        own32[s] = [dcar[k + "32"] for k in keys]
        scatter[s] = split_start(f"grads{s}_start", "scatter", [dcar[k + "16"] for k in keys], None, d_resid)
        token = scatter[s][-1]
    dx = d_resid

    out = {}
    sm_all = sm_names + list(REPLICATED)
    g_pack = _pack([g_small[n] for n in sm_all])
    small_first = split_start("small_grads_start", "gather", [g_pack], None, dx)
    chain = {}

    def stored(t, col):
        return jnp.swapaxes(t, 1, 2) if col else t

    for s in reversed(range(1, len(stages))):
        lands = split_wait(f"grads{s}_wait", "scatter", scatter[s], small_first[-1] if s == len(stages) - 1 else dx)
        for own, land, (n, l, col) in zip(own32[s], lands, stages[s][3], strict=True):
            chain[n] = adamw_sharded(f"adamw_{n}_{l}", l, stored(w[n], col), stored(m[n], col), stored(v[n], col), own, land, me1,
                                     chain.get(n), col)
            behind = chain[n][0]

    lands = split_wait("small_grads_wait", "gather", small_first, behind)
    small_second = split_start("small_forward_start", "forward", [], lands, behind)
    (g_all,) = split_wait("small_forward_wait", "forward", small_second, small_second[-1])
    g_sum = _unpack(ordered_sum("sum_small_grads", g_all, g_pack, me1), [g_small[n].shape for n in sm_all])
    g_loc = []
    for n, g in zip(sm_all, g_sum):
        if n in SMALL_SHARDED:
            ax = SMALL_SHARDED[n]
            size = w[n].shape[ax]
            g = lax.dynamic_slice_in_dim(g, me * size, size, axis=ax)
        g_loc.append(g)
    shapes = [w[n].shape for n in sm_all]
    res = adamw("adamw_small", _pack([w[n] for n in sm_all]), _pack([m[n] for n in sm_all]), _pack([v[n] for n in sm_all]), _pack(g_loc))
    behind = res[0]
    res = [g_loc] + [_unpack(r, shapes) for r in res]
    for k, n in enumerate(sm_all):
        out[n] = [r[k] for r in res]

    lands = split_wait("grads0_wait", "scatter", scatter[0], behind)
    for own, land, (n, l, col) in zip(own32[0], lands, stages[0][3], strict=True):
        chain[n] = adamw_sharded(f"adamw_{n}_{l}", l, stored(w[n], col), stored(m[n], col), stored(v[n], col), own, land, me1,
                                 chain.get(n), col)
    for n in BIG:
        out[n] = [stored(t, BIG[n] == 2) for t in chain[n]]

    grads, deltas, new_m, new_v = ([out[n][k] for n in WEIGHTS] for k in range(4))
    return (loss, dx.reshape(Bl, S, D), *grads, *deltas, *new_m, *new_v)
```

```python
import functools

import numpy as np
import jax
import jax.numpy as jnp
from jax import lax
from jax.experimental import pallas as pl
from jax.experimental.pallas import tpu as pltpu

F32 = jnp.float32
BF16 = jnp.bfloat16
HIGHEST = lax.Precision.HIGHEST
MESH = pl.DeviceIdType.MESH

D_MODEL = 1024
DEPTH = 4
GLA_HEADS, GLA_DK, GLA_DV, GLA_GATE_NORM = 4, 64, 128, 16.0
HGRN_HEADS, HGRN_DK, HGRN_DV, HGRN_MIN_F = 4, 64, 128, 1e-20
RET_HEADS, RET_DK, RET_DV = 4, 128, 192
ROPE_BASE = 10000.0
S5_WIDTH, S5_GROUPS, S5_STATE = 256, 16, 64
S5_N = S5_GROUPS * S5_STATE
EPS = 1e-6
ADAM_LR, ADAM_B1, ADAM_B2, ADAM_EPS, ADAM_WD, ADAM_STEP = 0.001, 0.9, 0.999, 1e-08, 0.01, 10

N_DEV = 8
VMEM_LIMIT_BYTES = 56 * 1024 * 1024
ROW_TILE = 256
SCAN_CHUNK = 128
HGRN_SCAN_CHUNK = 64
S5_SEGMENTS = 8
LANES = 128


def _params(*sem):
    return pltpu.CompilerParams(dimension_semantics=sem, vmem_limit_bytes=VMEM_LIMIT_BYTES)


def _divisor_tile(n, cap):
    best = None
    for t in range(LANES, min(n, cap) + 1, LANES):
        if n % t == 0:
            best = t
    return best if best is not None else n


def _mm_nn(name, x, w):
    M, K = x.shape
    N = w.shape[1]
    tn = _divisor_tile(N, 3072) if K * N * 2 > 8 * 2**20 else N
    tm = 256 if tn * 4 * 512 > 8 * 2**20 else 512
    assert M % tm == 0 and N % tn == 0

    def body(x_ref, w_ref, o_ref):
        o_ref[...] = jnp.dot(x_ref[...].astype(BF16), w_ref[...], preferred_element_type=F32)

    return pl.pallas_call(
        body, name=name, grid=(N // tn, M // tm),
        in_specs=[pl.BlockSpec((tm, K), lambda j, i: (i, 0)), pl.BlockSpec((K, tn), lambda j, i: (0, j))],
        out_specs=pl.BlockSpec((tm, tn), lambda j, i: (i, j)),
        out_shape=jax.ShapeDtypeStruct((M, N), F32),
        compiler_params=_params("parallel", "parallel"),
    )(x, w)


def _mm_nt(name, dy, w, after=None):
    M, N = dy.shape
    K = w.shape[0]
    tk = _divisor_tile(K, 1024) if K * N * 2 > 8 * 2**20 else K
    tm = 256 if N >= 4096 else 512
    assert M % tm == 0 and K % tk == 0

    def body(dy_ref, w_ref, *rest):
        rest[-1][...] = lax.dot_general(dy_ref[...].astype(BF16), w_ref[...], (((1,), (1,)), ((), ())),
                                        preferred_element_type=F32)

    extra = [] if after is None else [after]
    return pl.pallas_call(
        body, name=name, grid=(K // tk, M // tm),
        in_specs=[pl.BlockSpec((tm, N), lambda j, i: (i, 0)), pl.BlockSpec((tk, N), lambda j, i: (j, 0))]
        + [pl.BlockSpec(memory_space=pl.ANY)] * len(extra),
        out_specs=pl.BlockSpec((tm, tk), lambda j, i: (i, j)),
        out_shape=jax.ShapeDtypeStruct((M, K), F32),
        compiler_params=_params("parallel", "parallel"),
    )(dy, w, *extra)


def _tick():
    return jnp.zeros((SUBLANES, LANES), F32)


MM_TN_VMEM_BUDGET = 46 * 2**20


def _pad_lanes(n):
    return -(-n // LANES) * LANES


def _mm_tn(name, x, dy, nblk, want16):
    M, K = x.shape
    N = dy.shape[1]
    n = N // nblk
    xb, yb = x.dtype.itemsize, dy.dtype.itemsize
    best = None
    for tk in [t for t in range(LANES, K + 1, LANES) if K % t == 0]:
        for tm in (512, 256):
            out_bytes = nblk * tk * _pad_lanes(n) * (6 if want16 else 4)
            vmem = 2 * out_bytes + 2 * tm * tk * xb + 2 * tm * _pad_lanes(N) * yb
            traffic = (K // tk) * M * N * yb + M * K * xb
            if vmem <= MM_TN_VMEM_BUDGET and M % tm == 0 and (best is None or (traffic, -tm) < best[0]):
                best = ((traffic, -tm), tk, tm)
    _, tk, tm = best
    last = M // tm - 1

    def body(x_ref, dy_ref, o32_ref, *o16_ref):
        m = pl.program_id(1)

        @pl.when(m == 0)
        def _():
            o32_ref[...] = jnp.zeros_like(o32_ref)

        dyv = dy_ref[...].astype(BF16)
        if nblk == 1:
            o32_ref[0] += lax.dot_general(x_ref[...].astype(BF16), dyv, (((0,), (0,)), ((), ())), preferred_element_type=F32)
        else:
            xt = x_ref[...].astype(F32).T.astype(BF16)
            for d in range(nblk):
                o32_ref[d] += jnp.dot(xt, dyv[:, d * n:(d + 1) * n], preferred_element_type=F32)
        if want16:
            @pl.when(m == last)
            def _():
                o16_ref[0][...] = o32_ref[...].astype(BF16)

    blk = pl.BlockSpec((nblk, tk, n), lambda a, m: (0, a, 0))
    return pl.pallas_call(
        body, name=name, grid=(K // tk, M // tm),
        in_specs=[pl.BlockSpec((tm, tk), lambda a, m: (m, a)), pl.BlockSpec((tm, N), lambda a, m: (m, 0))],
        out_specs=[blk, blk] if want16 else [blk],
        out_shape=[jax.ShapeDtypeStruct((nblk, K, n), F32)] + ([jax.ShapeDtypeStruct((nblk, K, n), BF16)] if want16 else []),
        compiler_params=_params("parallel", "arbitrary"),
    )(x, dy)


def _weight_grads(name, x, dy, col_sharded):
    if col_sharded:
        return _mm_tn(name, x, dy, N_DEV, True)
    d32, d16 = _mm_tn(name, x, dy, 1, True)
    K, N = d32.shape[1:]
    return d32.reshape(N_DEV, K // N_DEV, N), d16.reshape(N_DEV, K // N_DEV, N)


def make_mm(name, col_sharded=False):
    @jax.custom_vjp
    def mm(x, w16, c32, c16):
        return _mm_nn(name + "_fwd", x, w16), _tick()

    def fwd(x, w16, c32, c16):
        return (_mm_nn(name + "_fwd", x, w16), _tick()), (x, w16)

    def bwd(res, g):
        x, w16 = res
        dy, after = g
        dx = _mm_nt(name + "_dx", dy, w16, after)
        d32, d16 = _weight_grads(name + "_dw", x, dy, col_sharded)
        return dx, jnp.zeros_like(w16), d32, d16

    mm.defvjp(fwd, bwd)
    return mm


def make_mm_groups(name, G, tm=512):
    def products(x, w16):
        M, K = x.shape
        N = w16.shape[1] // G

        def body(x_ref, w_ref, *o_refs):
            xv = x_ref[...].astype(BF16)
            for g, o_ref in enumerate(o_refs):
                o_ref[...] = jnp.dot(xv, w_ref[:, g * N:(g + 1) * N], preferred_element_type=F32)

        row = pl.BlockSpec((tm, N), lambda i: (i, 0))
        return tuple(pl.pallas_call(
            body, name=name + "_fwd", grid=(M // tm,),
            in_specs=[pl.BlockSpec((tm, K), lambda i: (i, 0)), pl.BlockSpec((K, G * N), lambda i: (0, 0))],
            out_specs=[row] * G, out_shape=[jax.ShapeDtypeStruct((M, N), F32)] * G, compiler_params=_params("parallel"),
        )(x, w16))

    def grads(x, w16, dys):
        M, K = x.shape
        N = w16.shape[1] // G

        def body(x_ref, w_ref, *refs):
            dy_refs, dx_ref, dw_ref = refs[:G], refs[G], refs[G + 1]

            @pl.when(pl.program_id(0) == 0)
            def _():
                dw_ref[...] = jnp.zeros_like(dw_ref)

            xv = x_ref[...].astype(BF16)
            dx = jnp.zeros(dx_ref.shape, F32)
            for g in range(G):
                cols = slice(g * N, (g + 1) * N)
                dyv = dy_refs[g][...].astype(BF16)
                dx = dx + lax.dot_general(dyv, w_ref[:, cols], (((1,), (1,)), ((), ())), preferred_element_type=F32)
                dw_ref[:, cols] += lax.dot_general(xv, dyv, (((0,), (0,)), ((), ())), preferred_element_type=F32)
            dx_ref[...] = dx

        row = pl.BlockSpec((tm, N), lambda i: (i, 0))
        return pl.pallas_call(
            body, name=name + "_bwd", grid=(M // tm,),
            in_specs=[pl.BlockSpec((tm, K), lambda i: (i, 0)), pl.BlockSpec((K, G * N), lambda i: (0, 0))] + [row] * G,
            out_specs=[pl.BlockSpec((tm, K), lambda i: (i, 0)), pl.BlockSpec((K, G * N), lambda i: (0, 0))],
            out_shape=[jax.ShapeDtypeStruct((M, K), F32), jax.ShapeDtypeStruct((K, G * N), F32)],
            compiler_params=_params("arbitrary"),
        )(x, w16, *dys)

    @jax.custom_vjp
    def mm(x, w):
        return products(x, w.astype(BF16))

    def fwd(x, w):
        w16 = w.astype(BF16)
        return products(x, w16), (x, w16)

    def bwd(res, dys):
        x, w16 = res
        return tuple(grads(x, w16, tuple(dys)))

    mm.defvjp(fwd, bwd)
    return mm


def _row_specs(rows, params, consts, tile):
    specs = [pl.BlockSpec((tile, r.shape[1]), lambda i: (i, 0)) for r in rows]
    specs += [pl.BlockSpec(p.shape, lambda i: (0, 0)) for p in params]
    specs += [pl.BlockSpec((tile, c.shape[1]), lambda i, n=c.shape[0] // tile: (i % n, 0)) for c in consts]
    return specs


def _row_fwd(name, f, out_widths, tile, rows, params, consts):
    T = rows[0].shape[0]
    nr, npar, ncon = len(rows), len(params), len(consts)

    def body(*refs):
        r = tuple(x[...] for x in refs[:nr])
        p = tuple(x[...] for x in refs[nr:nr + npar])
        c = tuple(x[...] for x in refs[nr + npar:nr + npar + ncon])
        outs = f(r, p, c)
        for o_ref, o in zip(refs[nr + npar + ncon:], outs, strict=True):
            o_ref[...] = o

    return pl.pallas_call(
        body, name=name + "_fwd", grid=(T // tile,),
        in_specs=_row_specs(rows, params, consts, tile),
        out_specs=[pl.BlockSpec((tile, w), lambda i: (i, 0)) for w in out_widths],
        out_shape=[jax.ShapeDtypeStruct((T, w), F32) for w in out_widths],
        compiler_params=_params("parallel"),
    )(*rows, *params, *consts)


def _row_bwd(name, f, out_widths, tile, rows, params, consts, gouts, dr_dtype=F32):
    T = rows[0].shape[0]
    nr, npar, ncon, nout = len(rows), len(params), len(consts), len(out_widths)

    def body(*refs):
        r = tuple(x[...] for x in refs[:nr])
        p = tuple(x[...] for x in refs[nr:nr + npar])
        c = tuple(x[...] for x in refs[nr + npar:nr + npar + ncon])
        k = nr + npar + ncon
        g = tuple(x[...] for x in refs[k:k + nout])
        dr_refs = refs[k + nout:k + nout + nr]
        dp_refs = refs[k + nout + nr:]
        _, vjp = jax.vjp(lambda r_, p_: tuple(f(r_, p_, c)), r, p)
        dr, dp = vjp(g)
        for ref, val in zip(dr_refs, dr, strict=True):
            ref[...] = val.astype(ref.dtype)
        if npar:
            @pl.when(pl.program_id(0) == 0)
            def _():
                for ref in dp_refs:
                    ref[...] = jnp.zeros_like(ref)

            for ref, val in zip(dp_refs, dp, strict=True):
                ref[...] += val

    outs = pl.pallas_call(
        body, name=name + "_bwd", grid=(T // tile,),
        in_specs=_row_specs(rows, params, consts, tile) + [pl.BlockSpec((tile, w), lambda i: (i, 0)) for w in out_widths],
        out_specs=[pl.BlockSpec((tile, r.shape[1]), lambda i: (i, 0)) for r in rows]
        + [pl.BlockSpec(p.shape, lambda i: (0, 0)) for p in params],
        out_shape=[jax.ShapeDtypeStruct(r.shape, dr_dtype) for r in rows] + [jax.ShapeDtypeStruct(p.shape, F32) for p in params],
        compiler_params=_params("arbitrary"),
    )(*rows, *params, *consts, *gouts)
    return tuple(outs[:nr]), tuple(outs[nr:])


def make_proj_stage(name, f, out_widths, tile=ROW_TILE):
    def run(x, w16, params, consts):
        p = _mm_nn(name + "_mm", x, w16)
        return p, tuple(_row_fwd(name, f, out_widths, tile, (p,), params, consts))

    @jax.custom_vjp
    def op(x, w16, c32, c16, params, consts):
        return run(x, w16, params, consts)[1]

    def fwd(x, w16, c32, c16, params, consts):
        p, outs = run(x, w16, params, consts)
        return outs, (x, w16, p, params, consts)

    def bwd(res, g):
        x, w16, p, params, consts = res
        (dp,), dparams = _row_bwd(name, f, out_widths, tile, (p,), params, consts, tuple(g), dr_dtype=BF16)
        d32, d16 = _weight_grads(name + "_dw", x, dp, True)
        return _mm_nt(name + "_dx", dp, w16), jnp.zeros_like(w16), d32, d16, dparams, tuple(jnp.zeros_like(c) for c in consts)

    op.defvjp(fwd, bwd)
    return op


def make_rowop(name, f, out_widths, tile=ROW_TILE):
    @jax.custom_vjp
    def op(rows, params, consts):
        return tuple(_row_fwd(name, f, out_widths, tile, rows, params, consts))

    def fwd(rows, params, consts):
        return op(rows, params, consts), (rows, params, consts)

    def bwd(res, g):
        rows, params, consts = res
        dr, dp = _row_bwd(name, f, out_widths, tile, rows, params, consts, tuple(g))
        return dr, dp, tuple(jnp.zeros_like(c) for c in consts)

    op.defvjp(fwd, bwd)
    return op


def _rms(x, g):
    return x * lax.rsqrt(jnp.mean(x * x, axis=-1, keepdims=True) + EPS) * g


def _silu(x):
    return x * jax.nn.sigmoid(x)


def _bdot(a, b):
    return jnp.dot(a.astype(BF16), b.astype(BF16), preferred_element_type=F32)


def norm_f(rows, params, consts):
    return (_rms(rows[0], params[0]),)


def addnorm_f(rows, params, consts):
    x = rows[0] + rows[1]
    return x, _rms(x, params[0])


EVEN_GLA_END = 1568


def even_prep_f(rows, params, consts):
    (p,) = rows
    wa2f, wa2b, baf, bab, lbf, lbb = params
    gq = p[:, 0:256]
    gk = p[:, 256:512] * (GLA_DK ** -0.5)
    gv = p[:, 512:1024]
    gr = p[:, 1024:1536]
    glaf = jax.nn.log_sigmoid(_bdot(p[:, 1536:1552], wa2f) + baf) / GLA_GATE_NORM
    glab = jax.nn.log_sigmoid(_bdot(p[:, 1552:1568], wa2b) + bab) / GLA_GATE_NORM
    o = EVEN_GLA_END
    hq = _silu(p[:, o:o + 256])

    def gate(z, lb):
        f = lb + (1.0 - lb) * jax.nn.sigmoid(z)
        return jnp.log(jnp.maximum(f, HGRN_MIN_F)), (1.0 - lb) * jax.nn.sigmoid(-z)

    hlaf, hkf = gate(p[:, o + 256:o + 512], lbf)
    hlab, hkb = gate(p[:, o + 512:o + 768], lbb)
    hv = p[:, o + 768:o + 1280]
    hg = p[:, o + 1280:o + 1792]
    return gq, gk, gv, gq, gk, gv, glaf, glab, gr, hq, hkf, hv, hq, hkb, hv, hlaf, hlab, hg


EVEN_PREP_WIDTHS = (256, 256, 512, 256, 256, 512, 256, 256, 512, 256, 256, 512, 256, 256, 512, 256, 256, 512)


def _head_rms(o, g, heads, d):
    parts = []
    for h in range(heads):
        seg = o[:, h * d:(h + 1) * d]
        parts.append(seg * lax.rsqrt(jnp.mean(seg * seg, axis=-1, keepdims=True) + EPS))
    return jnp.concatenate(parts, axis=1) * g


def even_post_f(rows, params, consts):
    of, ob, hof, hob, gr, hg = rows
    a = _head_rms(of + ob, params[0], GLA_HEADS, GLA_DV) * _silu(gr)
    b = _head_rms(hof + hob, params[1], HGRN_HEADS, HGRN_DV) * _silu(hg)
    return (jnp.concatenate([a, b], axis=1),)


@jax.custom_vjp
def _roll_half(x):
    return pltpu.roll(x, RET_DK // 2, 1)


_roll_half.defvjp(lambda x: (_roll_half(x), None), lambda _, g: (_roll_half(g),))


def odd_prep_f(rows, params, consts):
    (p,) = rows
    cosf, sinf = consts

    def rot(t):
        parts = []
        for h in range(RET_HEADS):
            th = t[:, h * RET_DK:(h + 1) * RET_DK]
            parts.append(th * cosf + _roll_half(th) * sinf)
        return jnp.concatenate(parts, axis=1)

    rq = rot(p[:, 0:512])
    rk = rot(p[:, 512:1024]) * (RET_DK ** -0.5)
    rv = p[:, 1024:1792]
    return rq, rk, rv, rq, rk, rv, p[:, 1792:2560], p[:, 2560:2816]


ODD_PREP_WIDTHS = (512, 512, 768, 512, 512, 768, 768, 256)


def ret_post_f(rows, params, consts):
    of, ob, rg = rows
    o = of + ob
    parts = []
    for h in range(RET_HEADS):
        seg = o[:, h * RET_DV:(h + 1) * RET_DV]
        c = seg - jnp.mean(seg, axis=-1, keepdims=True)
        parts.append(c * lax.rsqrt(jnp.mean(c * c, axis=-1, keepdims=True) + EPS))
    return (jnp.concatenate(parts, axis=1) * params[0] * _silu(rg),)


def s5_post_f(rows, params, consts):
    h0r, h0i, h1r, h1i, u = rows
    c_re, c_im, d_skip, glu_w, glu_b = params
    hr = h0r + h1r
    hi = h0i + h1i
    y = _bdot(hr, c_re) - _bdot(hi, c_im) + d_skip * u
    g = jax.nn.gelu(y)
    return (g * jax.nn.sigmoid(_bdot(g, glu_w) + glu_b),)


def loss_head(x, r, g, target, tile=ROW_TILE):
    T, D = x.shape

    def body(x_ref, r_ref, g_ref, t_ref, loss_ref, dx_ref, dg_ref):
        t = t_ref[...]

        def lf(xv, gv):
            e = _rms(xv, gv) - t
            row = jnp.sum(e * e, axis=-1, keepdims=True) * (0.5 / D)
            return jnp.sum(row, axis=0, keepdims=True)

        l, vjp = jax.vjp(lf, x_ref[...] + r_ref[...], g_ref[...])
        dx, dg = vjp(jnp.ones((1, 1), F32))
        dx_ref[...] = dx

        @pl.when(pl.program_id(0) == 0)
        def _():
            loss_ref[...] = jnp.zeros_like(loss_ref)
            dg_ref[...] = jnp.zeros_like(dg_ref)

        loss_ref[...] += jnp.broadcast_to(l, loss_ref.shape)
        dg_ref[...] += dg

    row = pl.BlockSpec((tile, D), lambda i: (i, 0))
    vec = pl.BlockSpec((1, D), lambda i: (0, 0))
    return pl.pallas_call(
        body, name="loss_head", grid=(T // tile,),
        in_specs=[row, row, vec, row],
        out_specs=[pl.BlockSpec((1, LANES), lambda i: (0, 0)), row, vec],
        out_shape=[jax.ShapeDtypeStruct((1, LANES), F32), jax.ShapeDtypeStruct((T, D), F32), jax.ShapeDtypeStruct((1, D), F32)],
        compiler_params=_params("arbitrary"),
    )(x, r, g, target)


SUBLANES = 8


def _halo_specs(width, tile, T):
    n8 = tile // SUBLANES
    last = T // SUBLANES - 1
    return [pl.BlockSpec((tile, width), lambda i: (i, 0)),
            pl.BlockSpec((SUBLANES, width), lambda i: (jnp.maximum(i * n8 - 1, 0), 0)),
            pl.BlockSpec((SUBLANES, width), lambda i: (jnp.minimum((i + 1) * n8, last), 0))]


def _shift_rows(x, prev_row, next_row, tile):
    row = lax.broadcasted_iota(jnp.int32, (tile, 1), 0)
    down = jnp.where(row == 0, prev_row, pltpu.roll(x, 1, 0))
    up = jnp.where(row == tile - 1, next_row, pltpu.roll(x, tile - 1, 0))
    return down, up


def _conv_fwd(name, u, cw, cb, S, tile):
    T, F2 = u.shape
    F = F2 // 2
    per_seq = S // tile

    def body(u_ref, up_ref, un_ref, cw_ref, cb_ref, g_ref):
        pos = pl.program_id(0) % per_seq
        uv = u_ref[...]
        prev_row = jnp.where(pos == 0, 0.0, up_ref[SUBLANES - 1:SUBLANES, :])
        next_row = jnp.where(pos == per_seq - 1, 0.0, un_ref[0:1, :])
        down, up = _shift_rows(uv, prev_row, next_row, tile)
        c = _conv_taps(down, uv, up, cw_ref, cb_ref)
        g_ref[...] = (_silu(c[:, :F]) * c[:, F:]).astype(BF16)

    return pl.pallas_call(
        body, name=name + "_fwd", grid=(T // tile,),
        in_specs=_halo_specs(F2, tile, T) + [pl.BlockSpec((3, F2), lambda i: (0, 0)), pl.BlockSpec((1, F2), lambda i: (0, 0))],
        out_specs=pl.BlockSpec((tile, F), lambda i: (i, 0)),
        out_shape=jax.ShapeDtypeStruct((T, F), BF16),
        compiler_params=_params("parallel"),
    )(u, u, u, cw, cb)


def _conv_taps(down, mid, up, cw_ref, cb_ref):
    c = cb_ref[...] + down * cw_ref[0:1, :]
    c = c + mid * cw_ref[1:2, :]
    return c + up * cw_ref[2:3, :]


def _conv_bwd(name, u, dg, cw, cb, S, tile):
    T, F2 = u.shape
    F = F2 // 2
    per_seq = S // tile

    def dact(cv, dgv):
        a, v = cv[:, :F], cv[:, F:]
        sg = jax.nn.sigmoid(a)
        return jnp.concatenate([dgv * v * (sg * (1.0 + a * (1.0 - sg))), dgv * (a * sg)], axis=1)

    def body(u_ref, up_ref, un_ref, g_ref, gp_ref, gn_ref, cw_ref, cb_ref, du_ref, dw0_ref, dw1_ref, dw2_ref, db_ref):
        i = pl.program_id(0)
        pos = i % per_seq
        first, last = pos == 0, pos == per_seq - 1
        lo, hi = slice(SUBLANES - 1, SUBLANES), slice(0, 1)
        uv = u_ref[...]
        u_m1, u_p1 = jnp.where(first, 0.0, up_ref[lo, :]), jnp.where(last, 0.0, un_ref[hi, :])
        u_dn, u_up = _shift_rows(uv, u_m1, u_p1, tile)
        dc = dact(_conv_taps(u_dn, uv, u_up, cw_ref, cb_ref), g_ref[...])
        c_m1 = _conv_taps(up_ref[SUBLANES - 2:SUBLANES - 1, :], u_m1, u_ref[0:1, :], cw_ref, cb_ref)
        c_p1 = _conv_taps(u_ref[tile - 1:tile, :], u_p1, un_ref[1:2, :], cw_ref, cb_ref)
        dc_prev = jnp.where(first, 0.0, dact(c_m1, gp_ref[lo, :]))
        dc_next = jnp.where(last, 0.0, dact(c_p1, gn_ref[hi, :]))
        dc_dn, dc_up = _shift_rows(dc, dc_prev, dc_next, tile)
        du = dc_up * cw_ref[0:1, :]
        du = du + dc * cw_ref[1:2, :]
        du_ref[...] = (du + dc_dn * cw_ref[2:3, :]).astype(BF16)

        @pl.when(i == 0)
        def _():
            for ref in (dw0_ref, dw1_ref, dw2_ref, db_ref):
                ref[...] = jnp.zeros_like(ref)

        dw0_ref[...] += jnp.sum(dc * u_dn, axis=0, keepdims=True)
        dw1_ref[...] += jnp.sum(dc * uv, axis=0, keepdims=True)
        dw2_ref[...] += jnp.sum(dc * u_up, axis=0, keepdims=True)
        db_ref[...] += jnp.sum(dc, axis=0, keepdims=True)

    vec = pl.BlockSpec((1, F2), lambda i: (0, 0))
    du, dw0, dw1, dw2, db = pl.pallas_call(
        body, name=name + "_bwd", grid=(T // tile,),
        in_specs=_halo_specs(F2, tile, T) + _halo_specs(F, tile, T) + [pl.BlockSpec((3, F2), lambda i: (0, 0)), vec],
        out_specs=[pl.BlockSpec((tile, F2), lambda i: (i, 0)), vec, vec, vec, vec],
        out_shape=[jax.ShapeDtypeStruct((T, F2), BF16)] + [jax.ShapeDtypeStruct((1, F2), F32)] * 4,
        compiler_params=_params("arbitrary"),
    )(u, u, u, dg, dg, dg, cw, cb)
    return du, jnp.concatenate([dw0, dw1, dw2], axis=0), db


def make_ffn(name, S):
    def run(x, wup16, wdn16, cw, cb):
        u = _mm_nn(name + "_up", x, wup16)
        g16 = _conv_fwd(name + "_conv", u, cw, cb, S, ROW_TILE)
        return u, g16, _mm_nn(name + "_down", g16, wdn16)

    @jax.custom_vjp
    def op(x, wup16, cu32, cu16, wdn16, cd32, cd16, cw, cb):
        return run(x, wup16, wdn16, cw, cb)[2], _tick()

    def fwd(x, wup16, cu32, cu16, wdn16, cd32, cd16, cw, cb):
        u, g16, out = run(x, wup16, wdn16, cw, cb)
        return (out, _tick()), (x, wup16, wdn16, u, g16, cw, cb)

    def bwd(res, g):
        x, wup16, wdn16, u, g16, cw, cb = res
        dout, after = g
        dg = _mm_nt(name + "_down_dx", dout, wdn16, after)
        dd32, dd16 = _weight_grads(name + "_down_dw", g16, dout, False)
        du16, dcw, dcb = _conv_bwd(name + "_conv", u, dg, cw, cb, S, ROW_TILE)
        du32, du16w = _weight_grads(name + "_up_dw", x, du16, True)
        return (_mm_nt(name + "_up_dx", du16, wup16), jnp.zeros_like(wup16), du32, du16w, jnp.zeros_like(wdn16), dd32, dd16, dcw, dcb)

    op.defvjp(fwd, bwd)
    return op


def _dot_nt(a, b):
    return lax.dot_general(a.astype(BF16), b.astype(BF16), (((1,), (1,)), ((), ())), preferred_element_type=F32)


def _dot_tn(a, b):
    return lax.dot_general(a.astype(BF16), b.astype(BF16), (((0,), (0,)), ((), ())), preferred_element_type=F32)


def _chunk_decays(la, direction, C, width, dk, const_lg):
    row = lax.broadcasted_iota(jnp.int32, (C, C), 0)
    col = lax.broadcasted_iota(jnp.int32, (C, C), 1)
    keep = (row >= col) if direction == 0 else (row <= col)
    ridx = lax.broadcasted_iota(jnp.int32, (C, 1), 0)
    if const_lg is None:
        cum = jnp.dot(keep.astype(F32), la, precision=HIGHEST, preferred_element_type=F32)
    else:
        lane_head = lax.broadcasted_iota(jnp.int32, (1, width), 1) // dk
        lg = jnp.zeros((1, width), F32)
        for h, val in enumerate(const_lg):
            lg = jnp.where(lane_head == h, val, lg)
        steps = (ridx + 1) if direction == 0 else (C - ridx)
        cum = steps.astype(F32) * lg
    exit_row = C - 1 if direction == 0 else 0
    mid = jnp.sum(jnp.where(ridx == C // 2, cum, 0.0), axis=0, keepdims=True)
    last = jnp.sum(jnp.where(ridx == exit_row, cum, 0.0), axis=0, keepdims=True)
    return keep, ridx == exit_row, cum, mid, last


def _scan_fwd(name, qkv, laf, lab, H, dk, dv, S, C, const_lg):
    qf, kf, vf, qb, kb, vb = qkv
    T = qf.shape[0]
    B, nc = T // S, S // C
    Wk, Wv = H * dk, H * dv
    learn = const_lg is None

    def body(*refs):
        @pl.when(pl.program_id(0) == 0)
        def _():
            refs[-1][...] = jnp.zeros_like(refs[-1])

        for b in range(B):
            one_sequence(*[r.at[b] for r in refs])

    def one_sequence(*refs):
        if learn:
            qf_r, qb_r, kf_r, kb_r, vf_r, vb_r, laf_r, lab_r, of_r, ob_r, sf_r, sb_r, st = refs
            las = (laf_r[...], lab_r[...])
        else:
            qf_r, qb_r, kf_r, kb_r, vf_r, vb_r, of_r, ob_r, sf_r, sb_r, st = refs
            las = (None, None)

        for d, (q_r, k_r, v_r, o_r, s_r) in enumerate(((qf_r, kf_r, vf_r, of_r, sf_r), (qb_r, kb_r, vb_r, ob_r, sb_r))):
            keep, _, cum, mid, last = _chunk_decays(las[d], d, C, Wk, dk, None if learn else const_lg[d])
            qe = q_r[...] * jnp.exp(cum - mid)
            ke = k_r[...] * jnp.exp(mid - cum)
            q_in = qe * jnp.exp(mid)
            k_out = ke * jnp.exp(last - mid)
            e_last = jnp.exp(last)
            vv = v_r[...]
            for h in range(H):
                ks, vs = slice(h * dk, (h + 1) * dk), slice(h * dv, (h + 1) * dv)
                a = jnp.where(keep, _dot_nt(qe[:, ks], ke[:, ks]), 0.0)
                state = st[d, h]
                o_r[:, vs] = _bdot(a, vv[:, vs]) + _dot_nt(q_in[:, ks], state)
                s_r[h * dv:(h + 1) * dv, :] = state
                st[d, h] = state * e_last[:, ks] + _dot_tn(vv[:, vs], k_out[:, ks])

    fpos = lambda c: c
    bpos = lambda c: nc - 1 - c
    kspec = lambda pos: pl.BlockSpec((B, C, Wk), lambda c: (0, pos(c), 0))
    vspec = lambda pos: pl.BlockSpec((B, C, Wv), lambda c: (0, pos(c), 0))
    sspec = lambda pos: pl.BlockSpec((B, None, Wv, dk), lambda c: (0, pos(c), 0, 0))
    seq = lambda t: t.reshape(B, S, t.shape[1])
    ins = [seq(t) for t in [qf, qb, kf, kb, vf, vb] + ([laf, lab] if learn else [])]
    in_specs = [kspec(fpos), kspec(bpos), kspec(fpos), kspec(bpos), vspec(fpos), vspec(bpos)] + ([kspec(fpos), kspec(bpos)] if learn else [])
    of, ob, sf, sb = pl.pallas_call(
        body, name=name + "_fwd", grid=(nc,), in_specs=in_specs,
        out_specs=[vspec(fpos), vspec(bpos), sspec(fpos), sspec(bpos)],
        out_shape=[jax.ShapeDtypeStruct((B, S, Wv), F32)] * 2 + [jax.ShapeDtypeStruct((B, nc, Wv, dk), F32)] * 2,
        scratch_shapes=[pltpu.VMEM((B, 2, H, dv, dk), F32)],
        compiler_params=_params("arbitrary"),
    )(*ins)
    return of.reshape(T, Wv), ob.reshape(T, Wv), sf, sb


def _scan_bwd(name, qkv, laf, lab, sf, sb, dof, dob, H, dk, dv, S, C, const_lg):
    qf, kf, vf, qb, kb, vb = qkv
    T = qf.shape[0]
    B, nc = T // S, S // C
    Wk, Wv = H * dk, H * dv
    learn = const_lg is None

    def body(*refs):
        @pl.when(pl.program_id(0) == 0)
        def _():
            refs[-1][...] = jnp.zeros_like(refs[-1])

        for b in range(B):
            one_sequence(*[r.at[b] for r in refs])

    def one_sequence(*refs):
        if learn:
            (qf_r, qb_r, kf_r, kb_r, vf_r, vb_r, laf_r, lab_r, sf_r, sb_r, dof_r, dob_r,
             dqf_r, dqb_r, dkf_r, dkb_r, dvf_r, dvb_r, dlaf_r, dlab_r, dst) = refs
            las, dlas = (laf_r[...], lab_r[...]), (dlaf_r, dlab_r)
        else:
            (qf_r, qb_r, kf_r, kb_r, vf_r, vb_r, sf_r, sb_r, dof_r, dob_r,
             dqf_r, dqb_r, dkf_r, dkb_r, dvf_r, dvb_r, dst) = refs
            las, dlas = (None, None), (None, None)

        groups = ((qf_r, kf_r, vf_r, sf_r, dof_r, dqf_r, dkf_r, dvf_r), (qb_r, kb_r, vb_r, sb_r, dob_r, dqb_r, dkb_r, dvb_r))
        for d, (q_r, k_r, v_r, s_r, do_r, dq_r, dk_r, dv_r) in enumerate(groups):
            keep, is_exit, cum, mid, last = _chunk_decays(las[d], d, C, Wk, dk, None if learn else const_lg[d])
            eq, ek = jnp.exp(cum - mid), jnp.exp(mid - cum)
            e_in, e_out, e_last = jnp.exp(mid), jnp.exp(last - mid), jnp.exp(last)
            qe, ke = q_r[...] * eq, k_r[...] * ek
            q_in, k_out = qe * e_in, ke * e_out
            vv, do = v_r[...], do_r[...]
            dqe_parts, dke_parts, dlast_parts = [], [], []
            for h in range(H):
                ks, vs = slice(h * dk, (h + 1) * dk), slice(h * dv, (h + 1) * dv)
                a = jnp.where(keep, _dot_nt(qe[:, ks], ke[:, ks]), 0.0)
                dp = jnp.where(keep, _dot_nt(do[:, vs], vv[:, vs]), 0.0)
                s_prev = s_r[h * dv:(h + 1) * dv, :]
                ds = dst[d, h]
                dk_out = _bdot(vv[:, vs], ds)
                dqe_parts.append(_bdot(dp, ke[:, ks]) + _bdot(do[:, vs], s_prev) * e_in[:, ks])
                dke_parts.append(_dot_tn(dp, qe[:, ks]) + dk_out * e_out[:, ks])
                dv_r[:, vs] = _dot_tn(a, do[:, vs]) + _dot_nt(k_out[:, ks], ds)
                if learn:
                    dlast_parts.append(jnp.sum(dk_out * k_out[:, ks], axis=0, keepdims=True)
                                       + jnp.sum(ds * s_prev, axis=0, keepdims=True) * e_last[:, ks])
                dst[d, h] = ds * e_last[:, ks] + _dot_tn(do[:, vs], q_in[:, ks])
            dqe = jnp.concatenate(dqe_parts, axis=1)
            dke = jnp.concatenate(dke_parts, axis=1)
            dq_r[...] = dqe * eq
            dk_r[...] = dke * ek
            if learn:
                dcum = dqe * qe - dke * ke + jnp.where(is_exit, jnp.concatenate(dlast_parts, axis=1), 0.0)
                dlas[d][...] = lax.dot_general(keep.astype(F32), dcum, (((0,), (0,)), ((), ())), precision=HIGHEST,
                                               preferred_element_type=F32)

    fpos = lambda c: nc - 1 - c
    bpos = lambda c: c
    kspec = lambda pos: pl.BlockSpec((B, C, Wk), lambda c: (0, pos(c), 0))
    vspec = lambda pos: pl.BlockSpec((B, C, Wv), lambda c: (0, pos(c), 0))
    sspec = lambda pos: pl.BlockSpec((B, None, Wv, dk), lambda c: (0, pos(c), 0, 0))
    seq = lambda t: t.reshape(B, S, t.shape[1])
    ins = [seq(t) for t in [qf, qb, kf, kb, vf, vb] + ([laf, lab] if learn else [])] + [sf, sb, seq(dof), seq(dob)]
    in_specs = ([kspec(fpos), kspec(bpos), kspec(fpos), kspec(bpos), vspec(fpos), vspec(bpos)]
                + ([kspec(fpos), kspec(bpos)] if learn else []) + [sspec(fpos), sspec(bpos), vspec(fpos), vspec(bpos)])
    out_specs = [kspec(fpos), kspec(bpos), kspec(fpos), kspec(bpos), vspec(fpos), vspec(bpos)] + ([kspec(fpos), kspec(bpos)] if learn else [])
    out_shape = ([jax.ShapeDtypeStruct((B, S, Wk), F32)] * 4 + [jax.ShapeDtypeStruct((B, S, Wv), F32)] * 2
                 + ([jax.ShapeDtypeStruct((B, S, Wk), F32)] * 2 if learn else []))
    outs = pl.pallas_call(
        body, name=name + "_bwd", grid=(nc,), in_specs=in_specs, out_specs=out_specs, out_shape=out_shape,
        scratch_shapes=[pltpu.VMEM((B, 2, H, dv, dk), F32)],
        compiler_params=_params("arbitrary"),
    )(*ins)
    return [t.reshape(T, t.shape[2]) for t in outs]


def make_scan(name, H, dk, dv, S, C, const_lg=None):
    if const_lg is None:
        @jax.custom_vjp
        def op(qkv, laf, lab):
            return tuple(_scan_fwd(name, qkv, laf, lab, H, dk, dv, S, C, None)[:2])

        def fwd(qkv, laf, lab):
            of, ob, sf, sb = _scan_fwd(name, qkv, laf, lab, H, dk, dv, S, C, None)
            return (of, ob), (qkv, laf, lab, sf, sb)

        def bwd(res, g):
            qkv, laf, lab, sf, sb = res
            dqf, dqb, dkf, dkb, dvf, dvb, dlaf, dlab = _scan_bwd(name, qkv, laf, lab, sf, sb, g[0], g[1], H, dk, dv, S, C, None)
            return (dqf, dkf, dvf, dqb, dkb, dvb), dlaf, dlab
    else:
        @jax.custom_vjp
        def op(qkv):
            return tuple(_scan_fwd(name, qkv, None, None, H, dk, dv, S, C, const_lg)[:2])

        def fwd(qkv):
            of, ob, sf, sb = _scan_fwd(name, qkv, None, None, H, dk, dv, S, C, const_lg)
            return (of, ob), (qkv, sf, sb)

        def bwd(res, g):
            qkv, sf, sb = res
            dqf, dqb, dkf, dkb, dvf, dvb = _scan_bwd(name, qkv, None, None, sf, sb, g[0], g[1], H, dk, dv, S, C, const_lg)
            return ((dqf, dkf, dvf, dqb, dkb, dvb),)

    op.defvjp(fwd, bwd)
    return op


def _reorder_call(name, t, S, to_segments):
    T, w = t.shape
    n_it = S // S5_SEGMENTS

    def body(x_ref, o_ref):
        def step(i, carry):
            packed = pl.ds(pl.multiple_of(i * S5_SEGMENTS, S5_SEGMENTS), S5_SEGMENTS)
            spread = pl.ds(i, S5_SEGMENTS, stride=n_it)
            if to_segments:
                o_ref[packed, :] = x_ref[spread, :]
            else:
                o_ref[spread, :] = x_ref[packed, :]
            return carry

        lax.fori_loop(0, n_it, step, 0, unroll=8)

    blk = pl.BlockSpec((S, LANES), lambda b, j: (b, j))
    return pl.pallas_call(body, name=name, grid=(T // S, w // LANES), in_specs=[blk], out_specs=blk,
                          out_shape=jax.ShapeDtypeStruct(t.shape, t.dtype), compiler_params=_params("parallel", "parallel"))(t)


def make_join(name, S, wa, wb):
    n_it = S // S5_SEGMENTS

    def call(tag, forward, wa, wb, arrs):
        T = arrs[0].shape[0]
        na, nb = wa // LANES, wb // LANES

        def body(*refs):
            j = pl.program_id(1)
            a_ref, b_ref, y_ref = (refs[0], refs[1], refs[2]) if forward else (refs[1], refs[2], refs[0])

            @pl.when(j < na)
            def _():
                if forward:
                    y_ref[...] = a_ref[...]
                else:
                    a_ref[...] = y_ref[...]

            @pl.when(j >= na)
            def _():
                def step(i, carry):
                    packed = pl.ds(pl.multiple_of(i * S5_SEGMENTS, S5_SEGMENTS), S5_SEGMENTS)
                    spread = pl.ds(i, S5_SEGMENTS, stride=n_it)
                    if forward:
                        y_ref[spread, :] = b_ref[packed, :]
                    else:
                        b_ref[packed, :] = y_ref[spread, :]
                    return carry

                lax.fori_loop(0, n_it, step, 0, unroll=8)

        a_spec = pl.BlockSpec((S, LANES), lambda b, j: (b, jnp.minimum(j, na - 1)))
        b_spec = pl.BlockSpec((S, LANES), lambda b, j: (b, jnp.maximum(j - na, 0)))
        y_spec = pl.BlockSpec((S, LANES), lambda b, j: (b, j))
        shapes = [jax.ShapeDtypeStruct((T, wa), F32), jax.ShapeDtypeStruct((T, wb), F32), jax.ShapeDtypeStruct((T, wa + wb), F32)]
        return pl.pallas_call(
            body, name=name + tag, grid=(T // S, na + nb),
            in_specs=[a_spec, b_spec] if forward else [y_spec], out_specs=y_spec if forward else [a_spec, b_spec],
            out_shape=shapes[2] if forward else shapes[:2], compiler_params=_params("parallel", "arbitrary"),
        )(*arrs)

    @jax.custom_vjp
    def op(a, b):
        return call("_fwd", True, wa, wb, (a, b))

    op.defvjp(lambda a, b: (call("_fwd", True, wa, wb, (a, b)), None), lambda _, dy: tuple(call("_bwd", False, wa, wb, (dy,))))
    return op


def make_reorder(name, S, to_segments):
    @jax.custom_vjp
    def op(t):
        return _reorder_call(name, t, S, to_segments)

    op.defvjp(lambda t: (_reorder_call(name, t, S, to_segments), None),
              lambda _, g: (_reorder_call(name + "_bwd", g, S, not to_segments),))
    return op


def _s5_scan_call(name, Xs, A, S, dirs, prev=None):
    with_p = prev is not None
    T, N = Xs[0].shape
    B, nl = T // S, N // LANES
    n_it = S // S5_SEGMENTS
    assert n_it & (n_it - 1) == 0

    def cmul(ar, ai, br, bi):
        return ar * br - ai * bi, ar * bi + ai * br

    def body(*refs):
        x, a_ref = refs[0:4], refs[4]
        if with_p:
            h_prev, x_prev, h, p_ref = refs[5:9], refs[9:13], refs[13:17], refs[17]
        else:
            h = refs[5:9]
        seg = lax.broadcasted_iota(jnp.int32, (S5_SEGMENTS, 1), 0)
        zero = jnp.zeros((S5_SEGMENTS, LANES), F32)
        a = [(jnp.broadcast_to(a_ref[2 * k:2 * k + 1, :], (S5_SEGMENTS, LANES)),
              jnp.broadcast_to(a_ref[2 * k + 1:2 * k + 2, :], (S5_SEGMENTS, LANES))) for k in range(2)]

        def rows_of(k, i):
            return pl.ds(pl.multiple_of(((n_it - 1 - i) if dirs[k] == 1 else i) * S5_SEGMENTS, S5_SEGMENTS), S5_SEGMENTS)

        def local(i, carry):
            out = []
            for k, (sr, si) in enumerate(carry):
                rows = rows_of(k, i)
                pr, pi = cmul(*a[k], sr, si)
                sr, si = pr + x[2 * k][rows, :], pi + x[2 * k + 1][rows, :]
                h[2 * k][rows, :] = sr
                h[2 * k + 1][rows, :] = si
                out.append((sr, si))
            return tuple(out)

        ends = lax.fori_loop(0, n_it, local, ((zero, zero), (zero, zero)), unroll=8)
        inherit = []
        for k, (er, ei) in enumerate(ends):
            back = dirs[k] == 1
            pr, pi = a[k]
            for _ in range(n_it.bit_length() - 1):
                pr, pi = cmul(pr, pi, pr, pi)
            shift = (S5_SEGMENTS - 1) if back else 1
            tr, ti = er, ei
            for r in (range(S5_SEGMENTS - 2, -1, -1) if back else range(1, S5_SEGMENTS)):
                nr, ni = cmul(pr, pi, pltpu.roll(tr, shift, 0), pltpu.roll(ti, shift, 0))
                tr = jnp.where(seg == r, er + nr, tr)
                ti = jnp.where(seg == r, ei + ni, ti)
            edge = (S5_SEGMENTS - 1) if back else 0
            inherit.append((jnp.where(seg == edge, 0.0, pltpu.roll(tr, shift, 0)), jnp.where(seg == edge, 0.0, pltpu.roll(ti, shift, 0))))

        def fix(i, carry):
            powers, sums = carry
            new_powers, new_sums = [], []
            for k in range(2):
                rows = rows_of(k, i)
                fr, fi = cmul(*powers[k], *inherit[k])
                sr, si = h[2 * k][rows, :] + fr, h[2 * k + 1][rows, :] + fi
                h[2 * k][rows, :] = sr
                h[2 * k + 1][rows, :] = si
                new_powers.append(cmul(*powers[k], *a[k]))
                if with_p:
                    ur = h_prev[2 * k][rows, :] - x_prev[2 * k][rows, :]
                    ui = h_prev[2 * k + 1][rows, :] - x_prev[2 * k + 1][rows, :]
                    new_sums.append((sums[k][0] + sr * ur + si * ui, sums[k][1] + si * ur - sr * ui))
            return tuple(new_powers), tuple(new_sums)

        _, sums = lax.fori_loop(0, n_it, fix, ((a[0], a[1]), ((zero, zero), (zero, zero)) if with_p else ()), unroll=8)
        if with_p:
            for k in range(2):
                p_ref[2 * k:2 * k + 1, :] = jnp.sum(sums[k][0], axis=0, keepdims=True)
                p_ref[2 * k + 1:2 * k + 2, :] = jnp.sum(sums[k][1], axis=0, keepdims=True)

    col = pl.BlockSpec((S, LANES), lambda b, j: (b, j))
    outs = pl.pallas_call(
        body, name=name, grid=(B, nl),
        in_specs=[col] * 4 + [pl.BlockSpec((4, LANES), lambda b, j: (0, j))] + ([col] * 8 if with_p else []),
        out_specs=[col] * 4 + ([pl.BlockSpec((None, 4, LANES), lambda b, j: (b, 0, j))] if with_p else []),
        out_shape=[jax.ShapeDtypeStruct((T, N), F32)] * 4 + ([jax.ShapeDtypeStruct((B, 4, N), F32)] if with_p else []),
        compiler_params=_params("parallel", "parallel"),
    )(*Xs, A, *(prev[0] + prev[1] if with_p else ()))
    return (tuple(outs[:4]), outs[4]) if with_p else tuple(outs)


def make_s5_scan(name, S):
    @jax.custom_vjp
    def op(X, A):
        return _s5_scan_call(name + "_fwd", X, A, S, (0, 1))

    def fwd(X, A):
        H = _s5_scan_call(name + "_fwd", X, A, S, (0, 1))
        return H, (X, A, H)

    def bwd(res, G):
        X, A, H = res
        conj = A * jnp.array([[1.0], [-1.0], [1.0], [-1.0]], F32)
        lam, P = _s5_scan_call(name + "_bwd", tuple(G), conj, S, (1, 0), prev=(tuple(H), tuple(X)))
        P = jnp.sum(P, axis=0)
        ar, ai = A[0::2], A[1::2]
        pr, pi = P[0::2], P[1::2]
        den = ar * ar + ai * ai
        dar, dai = (pr * ar - pi * ai) / den, (pr * ai + pi * ar) / den
        return lam, jnp.stack([dar[0], dai[0], dar[1], dai[1]], axis=0)

    op.defvjp(fwd, bwd)
    return op


ANY = pl.BlockSpec(memory_space=pl.ANY)


def _place():
    x, y, c = lax.axis_index("x"), lax.axis_index("y"), lax.axis_index("c")
    return x, y, c, [(1 - x, y), (x, 1 - y), (1 - x, 1 - y)]


def all_gather(name, arrs):
    n = len(arrs)

    def body(*refs):
        ins, outs = refs[:n], refs[n:2 * n]
        send, recv, lsem = refs[2 * n:]
        x, y, c, chips = _place()
        me, sibling = (x, y, c), (x, y, 1 - c)

        def copy(a, k, block, to, src=None):
            slot = outs[a].at[4 * block[0] + 2 * block[1] + block[2]]
            return pltpu.make_async_remote_copy(src_ref=slot if src is None else src, dst_ref=slot, send_sem=send.at[a, k],
                                                recv_sem=recv.at[a, k], device_id=to, device_id_type=MESH)

        mine = [pltpu.make_async_copy(ins[a], outs[a].at[4 * x + 2 * y + c], lsem.at[a]) for a in range(n)]
        first = []
        for a in range(n):
            mine[a].start()
            first.append(copy(a, 0, me, sibling, src=ins[a]))
            first += [copy(a, 1 + j, me, (*chip, c), src=ins[a]) for j, chip in enumerate(chips)]
        for cp in first:
            cp.start()
        passed = []
        for a in range(n):
            for j, chip in enumerate(chips):
                copy(a, 1 + j, (*chip, c), me).wait_recv()
                fwd = copy(a, 4 + j, (*chip, c), sibling)
                fwd.start()
                passed.append(fwd)
        for a in range(n):
            copy(a, 0, sibling, me).wait_recv()
            for j, chip in enumerate(chips):
                copy(a, 4 + j, (*chip, 1 - c), me).wait_recv()
        for cp in first + passed:
            cp.wait_send()
        for cp in mine:
            cp.wait()

    return pl.pallas_call(
        body, name=name, in_specs=[ANY] * n, out_specs=[ANY] * n,
        out_shape=[jax.ShapeDtypeStruct((N_DEV,) + a.shape, a.dtype) for a in arrs],
        scratch_shapes=[pltpu.SemaphoreType.DMA((n, 7)), pltpu.SemaphoreType.DMA((n, 7)), pltpu.SemaphoreType.DMA((n,))],
    )(*arrs)


HBM = pl.BlockSpec(memory_space=pltpu.HBM)
SEM = pl.BlockSpec(memory_space=pltpu.SEMAPHORE)
EFFECT = pltpu.SideEffectType.DATAFLOW_SIDE_EFFECTING
GATHER_PEERS = (1, 2, 4, 6)
OTHER_CHIPS = (2, 4, 6)
COPIES_PER_ARRAY = {"scatter": N_DEV - 1, "gather": len(GATHER_PEERS), "forward": len(OTHER_CHIPS)}


def _split_plan(mode, srcs, lands, send, recv):
    x, y, c = lax.axis_index("x"), lax.axis_index("y"), lax.axis_index("c")

    def dev(k):
        return (1 - x if k & 4 else x), (1 - y if k & 2 else y), (1 - c if k & 1 else c)

    def idx(d):
        return 4 * d[0] + 2 * d[1] + d[2]

    me = idx((x, y, c))
    plan = []
    for a, land in enumerate(lands):
        if mode == "scatter":
            legs = [(srcs[a].at[idx(dev(k))], land.at[me], land.at[idx(dev(k))], dev(k)) for k in range(1, N_DEV)]
        elif mode == "gather":
            legs = [(srcs[a], land.at[me], land.at[idx(dev(k))], dev(k)) for k in GATHER_PEERS]
        else:
            legs = [(land.at[idx(dev(j))], land.at[idx(dev(j))], land.at[idx(dev(j ^ 1))], dev(1)) for j in OTHER_CHIPS]
        for i, (src, dst, arrival, to) in enumerate(legs):
            sem = a * len(legs) + i
            pair = tuple(pltpu.make_async_remote_copy(src_ref=src, dst_ref=d, send_sem=send.at[sem], recv_sem=recv.at[sem],
                                                      device_id=to, device_id_type=MESH) for d in (dst, arrival))
            plan.append(pair)
    return plan


def split_start(name, mode, srcs, lands, after):
    if lands is None:
        lands = [lax.empty((N_DEV,) + (s.shape[1:] if mode == "scatter" else s.shape), s.dtype) for s in srcs]
    ns, nl = len(srcs), len(lands)
    nsem = COPIES_PER_ARRAY[mode] * nl

    def body(*refs):
        ins, lnd = refs[:ns], refs[ns:ns + nl]
        send, recv = refs[ns + nl + 1], refs[ns + nl + 2]
        token = refs[-1]
        for out, _ in _split_plan(mode, ins, lnd, send, recv):
            out.start()
        token[...] = jnp.zeros_like(token)

    arrs = list(srcs) + list(lands)
    return pl.pallas_call(
        body, name=name,
        out_shape=(pltpu.SemaphoreType.DMA((nsem,)), pltpu.SemaphoreType.DMA((nsem,)))
        + tuple(pltpu.HBM(t.shape, t.dtype) for t in arrs) + (jax.ShapeDtypeStruct((SUBLANES, LANES), F32),),
        in_specs=[HBM] * len(arrs) + [ANY],
        out_specs=(SEM, SEM) + (HBM,) * len(arrs) + (pl.BlockSpec(memory_space=pltpu.VMEM),),
        input_output_aliases={i: 2 + i for i in range(len(arrs))},
        compiler_params=pltpu.CompilerParams(has_side_effects=EFFECT),
    )(*[pltpu.with_memory_space_constraint(t, pltpu.HBM) for t in arrs], after)


def split_wait(name, mode, handle, after):
    send, recv = handle[0], handle[1]
    arrs = list(handle[2:-1])
    nl = len(arrs) if mode == "forward" else len(arrs) // 2
    ns = len(arrs) - nl

    def body(*refs):
        ins, lnd = refs[:ns], refs[ns:ns + nl]
        s, r = refs[ns + nl], refs[ns + nl + 1]
        for out, arrival in _split_plan(mode, ins, lnd, s, r):
            out.wait_send()
            arrival.wait_recv()

    outs = pl.pallas_call(
        body, name=name,
        out_shape=tuple(pltpu.HBM(t.shape, t.dtype) for t in arrs),
        in_specs=[HBM] * len(arrs) + [SEM, SEM, ANY], out_specs=(HBM,) * len(arrs),
        input_output_aliases={i: i for i in range(len(arrs))},
        compiler_params=pltpu.CompilerParams(has_side_effects=EFFECT),
    )(*arrs, send, recv, after)
    return list(outs[ns:])


def _row_tile(rows, cols):
    cap = max(SUBLANES, (2**18 // cols) // SUBLANES * SUBLANES)
    if rows <= cap:
        return rows
    for t in range(cap, SUBLANES - 1, -SUBLANES):
        if rows % t == 0:
            return t
    return rows


def ordered_sum(name, parts, own, me):
    n, R, C = parts.shape
    rt = _row_tile(R, C)

    def body(me_ref, p_ref, own_ref, o_ref):
        s = None
        for k in range(n):
            t = jnp.where(me_ref[0] == k, own_ref[...], p_ref[k])
            s = t if s is None else s + t
        o_ref[...] = s

    return pl.pallas_call(
        body, name=name,
        grid_spec=pltpu.PrefetchScalarGridSpec(
            num_scalar_prefetch=1, grid=(R // rt,),
            in_specs=[pl.BlockSpec((n, rt, C), lambda r, p: (0, r, 0)), pl.BlockSpec((rt, C), lambda r, p: (r, 0))],
            out_specs=pl.BlockSpec((rt, C), lambda r, p: (r, 0))),
        out_shape=jax.ShapeDtypeStruct((R, C), F32), compiler_params=_params("parallel"),
    )(me, parts, own)


def _adamw_update(w, m, v, g):
    bias1 = 1.0 - ADAM_B1 ** ADAM_STEP
    bias2 = 1.0 - ADAM_B2 ** ADAM_STEP
    m_new = ADAM_B1 * m + (1.0 - ADAM_B1) * g
    v_new = ADAM_B2 * v + (1.0 - ADAM_B2) * (g * g)
    delta = -ADAM_LR * ((m_new / bias1) / (jnp.sqrt(v_new / bias2) + ADAM_EPS) + ADAM_WD * w)
    return delta, m_new, v_new


def adamw(name, w, m, v, g):
    R, C = w.shape
    rt = _row_tile(R, C)

    def body(w_ref, m_ref, v_ref, g_ref, d_ref, mo_ref, vo_ref):
        d_ref[...], mo_ref[...], vo_ref[...] = _adamw_update(w_ref[...], m_ref[...], v_ref[...], g_ref[...])

    row = pl.BlockSpec((rt, C), lambda r: (r, 0))
    return pl.pallas_call(
        body, name=name, grid=(R // rt,), in_specs=[row] * 4, out_specs=[row] * 3,
        out_shape=[jax.ShapeDtypeStruct((R, C), F32)] * 3, compiler_params=_params("parallel"),
    )(w, m, v, g)


def adamw_sharded(name, layer, w, m, v, own, land, me, prev, transposed):
    _, R, C = own.shape
    rt = _row_tile(R, C)

    def body(me_ref, w_ref, m_ref, v_ref, own_ref, land_ref, *rest):
        go_ref, d_ref, mo_ref, vo_ref = rest[-4:]
        g = own_ref[...]
        for k in range(N_DEV):
            g = g + jnp.where(me_ref[0] == k, 0.0, land_ref[k].astype(F32))
        if transposed:
            g = g.T
        go_ref[...] = g
        d_ref[...], mo_ref[...], vo_ref[...] = _adamw_update(w_ref[...], m_ref[...], v_ref[...], g)

    if transposed:
        row = pl.BlockSpec((None, C, rt), lambda r, p: (layer, 0, r))
    else:
        row = pl.BlockSpec((None, rt, C), lambda r, p: (layer, r, 0))
    in_specs = [row, row, row, pl.BlockSpec((None, rt, C), lambda r, p: (p[0], r, 0)), pl.BlockSpec((N_DEV, rt, C), lambda r, p: (0, r, 0))]
    ins = [me, w, m, v, own, land]
    aliases = {}
    if prev is not None:
        in_specs += [ANY] * 4
        aliases = {len(ins) + k: k for k in range(4)}
        ins += list(prev)
    return pl.pallas_call(
        body, name=name,
        grid_spec=pltpu.PrefetchScalarGridSpec(num_scalar_prefetch=1, grid=(R // rt,), in_specs=in_specs, out_specs=[row] * 4),
        out_shape=[jax.ShapeDtypeStruct(w.shape, F32)] * 4, input_output_aliases=aliases,
        compiler_params=_params("arbitrary"),
    )(*ins)


def _hgrn_lower_bounds(lb_logits):
    p = jax.nn.softmax(lb_logits, axis=1)
    return jnp.cumsum(p, axis=1) - p[:, :1]


def _s5_discretise(lam_re, lam_im, log_dt, b_re, b_im):
    lr = jnp.minimum(lam_re, -1e-4)
    li = lam_im
    dt = jnp.exp(log_dt)[:, None]
    mag = jnp.exp(lr * dt)
    ar, ai = mag * jnp.cos(li * dt), mag * jnp.sin(li * dt)
    den = lr * lr + li * li
    nr = ar - 1.0
    cr = (nr * lr + ai * li) / den
    ci = (ai * lr - nr * li) / den
    bbr = cr[..., None] * b_re - ci[..., None] * b_im
    bbi = cr[..., None] * b_im + ci[..., None] * b_re
    return ar, ai, bbr, bbi


def _block_diag(t):
    G, a, b = t.shape
    eye = jnp.eye(G, dtype=F32)
    return (t[:, :, None, :] * eye[:, None, :, None]).reshape(G * a, G * b)


def _rope_tables(S):
    half = RET_DK // 2
    inv = ROPE_BASE ** (-jnp.arange(half, dtype=F32) / half)
    ang = jnp.arange(S, dtype=F32)[:, None] * inv[None, :]
    cos, sin = jnp.cos(ang), jnp.sin(ang)
    return jnp.concatenate([cos, cos], axis=1), jnp.concatenate([-sin, sin], axis=1)


def _ret_log_decays():
    f = tuple(float(np.log1p(-np.exp2(np.float32(-5.0 - h)))) for h in range(RET_HEADS))
    b = tuple(float(np.log1p(-np.exp2(np.float32(-5.5 - h)))) for h in range(RET_HEADS))
    return f, b


def assemble_weight(name, land, own, me):
    _, R, C = land.shape
    tr = min(R, 256)

    def body(me_ref, land_ref, own_ref, o_ref):
        for d in range(N_DEV):
            o_ref[:, d * C:(d + 1) * C] = jnp.where(me_ref[0] == d, own_ref[...], land_ref[d])

    return pl.pallas_call(
        body, name=name,
        grid_spec=pltpu.PrefetchScalarGridSpec(
            num_scalar_prefetch=1, grid=(R // tr,),
            in_specs=[pl.BlockSpec((N_DEV, tr, C), lambda i, p: (0, i, 0)), pl.BlockSpec((tr, C), lambda i, p: (i, 0))],
            out_specs=pl.BlockSpec((tr, N_DEV * C), lambda i, p: (i, 0))),
        out_shape=jax.ShapeDtypeStruct((R, N_DEV * C), land.dtype), compiler_params=_params("parallel"),
    )(me, land, own)


def _row(t):
    return t.reshape(1, -1)


def mixer_stage(layer, S, resid, branch, w_in, w_out, carriers, small):
    j = layer // 2
    tag = f"l{layer}"
    g = _row(small["mix_norm_g"][layer])
    if branch is None:
        x = resid
        (h,) = make_rowop(tag + "_norm", norm_f, (D_MODEL,))((x,), (g,), ())
    else:
        x, h = make_rowop(tag + "_addnorm", addnorm_f, (D_MODEL, D_MODEL))((resid, branch), (g,), ())
    if layer % 2 == 0:
        lbs = _hgrn_lower_bounds(small["hgrn_lb_logits"])
        prm = (small["gla_wa2"][j, 0], small["gla_wa2"][j, 1], _row(small["gla_ba"][j, 0]), _row(small["gla_ba"][j, 1]),
               _row(lbs[0, j]), _row(lbs[1, j]))
        outs = make_proj_stage(tag + "_prep", even_prep_f, EVEN_PREP_WIDTHS)(h, w_in, carriers["in32"], carriers["in16"], prm, ())
        gla_qkv, (glaf, glab, gr), hgrn_qkv, (hlaf, hlab, hg) = outs[0:6], outs[6:9], outs[9:15], outs[15:18]
        of, ob = make_scan(tag + "_gla", GLA_HEADS, GLA_DK, GLA_DV, S, SCAN_CHUNK)(gla_qkv, glaf, glab)
        hof, hob = make_scan(tag + "_hgrn", HGRN_HEADS, HGRN_DK, HGRN_DV, S, HGRN_SCAN_CHUNK)(hgrn_qkv, hlaf, hlab)
        (y,) = make_rowop(tag + "_post", even_post_f, (D_MODEL,))(
            (of, ob, hof, hob, gr, hg), (_row(small["gla_norm_g"][j]), _row(small["hgrn_norm_g"][j])), ())
    else:
        cosf, sinf = _rope_tables(S)
        outs = make_proj_stage(tag + "_prep", odd_prep_f, ODD_PREP_WIDTHS)(h, w_in, carriers["in32"], carriers["in16"], (), (cosf, sinf))
        ret_qkv, rg, su = outs[0:6], outs[6], outs[7]
        of, ob = make_scan(tag + "_ret", RET_HEADS, RET_DK, RET_DV, S, SCAN_CHUNK, const_lg=_ret_log_decays())(ret_qkv)
        (cm,) = make_rowop(tag + "_retpost", ret_post_f, (RET_HEADS * RET_DV,))((of, ob, rg), (_row(small["ret_norm_g"][j]),), ())
        disc = [_s5_discretise(small["s5_lam_re"][j, d], small["s5_lam_im"][j, d], small["s5_log_dt"][j, d],
                               small["s5_b_re"][j], small["s5_b_im"][j]) for d in range(2)]
        a4 = jnp.stack([t.reshape(-1) for d in range(2) for t in disc[d][:2]], axis=0)
        su_p = make_reorder(tag + "_s5seg", S, True)(su)
        bd = jnp.concatenate([_block_diag(jnp.swapaxes(disc[d][2 + i], 1, 2)) for d in range(2) for i in range(2)], axis=1)
        Xs = make_mm_groups(tag + "_s5in", 4)(su_p, bd)
        Hs = make_s5_scan(tag + "_s5scan", S)(Xs, a4)
        prm = (_block_diag(jnp.swapaxes(small["s5_c_re"][j], 1, 2)), _block_diag(jnp.swapaxes(small["s5_c_im"][j], 1, 2)),
               _row(small["s5_d"][j]), small["s5_glu_w"][j], _row(small["s5_glu_b"][j]))
        (dm_p,) = make_rowop(tag + "_s5post", s5_post_f, (S5_WIDTH,))((*Hs, su_p), prm, ())
        y = make_join(tag + "_join", S, RET_HEADS * RET_DV, S5_WIDTH)(cm, dm_p)
    return (x, *make_mm(tag + "_out")(y, w_out, carriers["out32"], carriers["out16"]))


def ffn_stage(layer, S, resid, branch, w_up, w_down, carriers, small):
    tag = f"l{layer}"
    x, hf = make_rowop(tag + "_ffnnorm", addnorm_f, (D_MODEL, D_MODEL))((resid, branch), (_row(small["ffn_norm_g"][layer]),), ())
    out, tick = make_ffn(tag + "_ffn", S)(hf, w_up, carriers["up32"], carriers["up16"], w_down, carriers["down32"], carriers["down16"],
                                         small["ffn_conv_w"][layer], _row(small["ffn_conv_b"][layer]))
    return x, out, tick


BIG = {"w_in_even": 2, "w_out_even": 1, "w_in_odd": 2, "w_out_odd": 1, "ffn_w_up": 2, "ffn_w_down": 1}
SMALL_SHARDED = {"gla_wa2": 3, "gla_ba": 2, "hgrn_lb_logits": 2, "ret_norm_g": 1, "s5_d": 1, "s5_glu_w": 1, "s5_glu_b": 1,
                 "ffn_conv_w": 2}
REPLICATED = ("mix_norm_g", "ffn_norm_g", "final_norm_g", "gla_norm_g", "hgrn_norm_g", "s5_lam_re", "s5_lam_im", "s5_log_dt",
              "s5_b_re", "s5_b_im", "s5_c_re", "s5_c_im", "ffn_conv_b")
WEIGHTS = ("mix_norm_g", "ffn_norm_g", "final_norm_g", "w_in_even", "w_out_even", "gla_wa2", "gla_ba", "gla_norm_g",
           "hgrn_lb_logits", "hgrn_norm_g", "w_in_odd", "w_out_odd", "ret_norm_g", "s5_lam_re", "s5_lam_im", "s5_log_dt",
           "s5_b_re", "s5_b_im", "s5_c_re", "s5_c_im", "s5_d", "s5_glu_w", "s5_glu_b", "ffn_w_up", "ffn_conv_w", "ffn_conv_b",
           "ffn_w_down")
PACK_COLS = LANES
MIXER_SMALL = (("mix_norm_g", "hgrn_lb_logits", "gla_wa2", "gla_ba", "gla_norm_g", "hgrn_norm_g"),
               ("mix_norm_g", "ret_norm_g", "s5_lam_re", "s5_lam_im", "s5_log_dt", "s5_b_re", "s5_b_im", "s5_c_re", "s5_c_im",
                "s5_d", "s5_glu_w", "s5_glu_b"))
FFN_SMALL = ("ffn_norm_g", "ffn_conv_w", "ffn_conv_b")


def _unshard(g, axis):
    t = jnp.moveaxis(g, 0, axis)
    return t.reshape(t.shape[:axis] + (t.shape[axis] * t.shape[axis + 1],) + t.shape[axis + 2:])


def _pack_rows(shape):
    return -(-int(np.prod(shape)) // (SUBLANES * PACK_COLS)) * SUBLANES


def _pack(arrs):
    parts = []
    for a in arrs:
        rows = _pack_rows(a.shape)
        parts.append(jnp.pad(a.reshape(-1), (0, rows * PACK_COLS - a.size)).reshape(rows, PACK_COLS))
    return jnp.concatenate(parts, axis=0)


def _unpack(packed, shapes):
    lead = packed.shape[:-2]
    out, r = [], 0
    for s in shapes:
        rows, n = _pack_rows(s), int(np.prod(s))
        piece = packed[..., r:r + rows, :].reshape(lead + (rows * PACK_COLS,))
        out.append(piece[..., :n].reshape(lead + tuple(s)))
        r += rows
    return out


def kernel(x, mix_norm_g, ffn_norm_g, final_norm_g, w_in_even, w_out_even, gla_wa2, gla_ba, gla_norm_g, hgrn_lb_logits, hgrn_norm_g, w_in_odd, w_out_odd, ret_norm_g, s5_lam_re, s5_lam_im, s5_log_dt, s5_b_re, s5_b_im, s5_c_re, s5_c_im, s5_d, s5_glu_w, s5_glu_b, ffn_w_up, ffn_conv_w, ffn_conv_b, ffn_w_down, loss_target, m_mix_norm_g, m_ffn_norm_g, m_final_norm_g, m_w_in_even, m_w_out_even, m_gla_wa2, m_gla_ba, m_gla_norm_g, m_hgrn_lb_logits, m_hgrn_norm_g, m_w_in_odd, m_w_out_odd, m_ret_norm_g, m_s5_lam_re, m_s5_lam_im, m_s5_log_dt, m_s5_b_re, m_s5_b_im, m_s5_c_re, m_s5_c_im, m_s5_d, m_s5_glu_w, m_s5_glu_b, m_ffn_w_up, m_ffn_conv_w, m_ffn_conv_b, m_ffn_w_down, v_mix_norm_g, v_ffn_norm_g, v_final_norm_g, v_w_in_even, v_w_out_even, v_gla_wa2, v_gla_ba, v_gla_norm_g, v_hgrn_lb_logits, v_hgrn_norm_g, v_w_in_odd, v_w_out_odd, v_ret_norm_g, v_s5_lam_re, v_s5_lam_im, v_s5_log_dt, v_s5_b_re, v_s5_b_im, v_s5_c_re, v_s5_c_im, v_s5_d, v_s5_glu_w, v_s5_glu_b, v_ffn_w_up, v_ffn_conv_w, v_ffn_conv_b, v_ffn_w_down):
    args = locals()
    w = {n: args[n] for n in WEIGHTS}
    m = {n: args["m_" + n] for n in WEIGHTS}
    v = {n: args["v_" + n] for n in WEIGHTS}
    Bl, S, D = x.shape
    T = Bl * S
    ix, iy, ic = lax.axis_index("x"), lax.axis_index("y"), lax.axis_index("c")
    me = 4 * ix + 2 * iy + ic

    xt = x.reshape(T, D)
    me1 = me.reshape(1).astype(jnp.int32)
    stages = []
    for layer in range(DEPTH):
        j = layer // 2
        kin, kout = ("w_in_even", "w_out_even") if layer % 2 == 0 else ("w_in_odd", "w_out_odd")
        stages.append((mixer_stage, layer, ("in", "out"), ((kin, j, True), (kout, j, False)), MIXER_SMALL[layer % 2]))
        stages.append((ffn_stage, layer, ("up", "down"), (("ffn_w_up", layer, True), ("ffn_w_down", layer, False)), FFN_SMALL))

    gather, after = [], xt
    for s, (_, _, _, projs, _) in enumerate(stages):
        handle = split_start(f"gather{s}_start", "gather", [w[n][l].astype(BF16) for n, l, _ in projs], None, after)
        gather.append(handle)
        after = handle[-1]
    sm_names = list(SMALL_SHARDED)
    (sm_all8,) = all_gather("gather_small", [_pack([w[n] for n in sm_names])])
    small = {n: _unshard(t, SMALL_SHARDED[n]) for n, t in zip(sm_names, _unpack(sm_all8, [w[n].shape for n in sm_names]))}
    small.update({n: w[n] for n in REPLICATED})
    small["mix_norm_g"] = small["mix_norm_g"] + after[0, 0]

    resid, branch, pulls = xt, None, []
    second = [None] * len(stages)

    def second_level(s, after):
        lands = split_wait(f"gather{s}_wait", "gather", gather[s], after)
        second[s] = split_start(f"forward{s}_start", "forward", [], lands, after)

    for s, (fn, layer, keys, projs, sm_keys) in enumerate(stages):
        here = lax.stop_gradient(resid)
        if second[s] is None:
            second_level(s, here)
        lands = split_wait(f"forward{s}_wait", "forward", second[s], second[s][-1])
        sm = {n: small[n] for n in sm_keys}
        if 2 <= s < len(stages) - 1:
            second_level(s + 1, here)
            norm = "mix_norm_g" if fn is mixer_stage else "ffn_norm_g"
            sm[norm] = sm[norm] + second[s + 1][-1][0, 0]
        full, carriers = [], {}
        for key, land, (n, l, col) in zip(keys, lands, projs, strict=True):
            if col:
                full.append(assemble_weight(f"weight{s}_{key}", land, w[n][l].astype(BF16), me1))
            else:
                blocks = lax.dynamic_update_index_in_dim(land, w[n][l].astype(BF16), me, 0)
                full.append(blocks.reshape(N_DEV * land.shape[1], land.shape[2]))
            carriers[key + "32"] = jnp.zeros(land.shape, F32)
            carriers[key + "16"] = jnp.zeros(land.shape, BF16)
        run = functools.partial(fn, layer, S)
        if branch is None:
            (resid, branch, _), pull = jax.vjp(lambda r, c, p, run=run, full=full: run(r, None, full[0], full[1], c, p), resid, carriers, sm)
        else:
            (resid, branch, _), pull = jax.vjp(lambda r, b, c, p, run=run, full=full: run(r, b, full[0], full[1], c, p), resid, branch, carriers, sm)
        pulls.append(pull)

    loss_acc, dxf, dgf = loss_head(resid, branch, small["final_norm_g"].reshape(1, D), loss_target.reshape(T, D))
    loss = lax.psum(loss_acc[0, 0], ("x", "y", "c"))
    g_small = {"final_norm_g": dgf.reshape(D)}
    d_resid, d_branch, token = dxf, dxf, _tick()
    scatter, own32 = [None] * len(stages), [None] * len(stages)
    for s in reversed(range(len(stages))):
        if s == 0:
            d_resid, dcar, dsm = pulls[s]((d_resid, d_branch, token))
        else:
            d_resid, d_branch, dcar, dsm = pulls[s]((d_resid, d_branch, token))
        for n, g in dsm.items():
            g_small[n] = g_small[n] + g if n in g_small else g
        keys = stages[s][2]
        own32[s] = [dcar[k + "32"] for k in keys]
        scatter[s] = split_start(f"grads{s}_start", "scatter", [dcar[k + "16"] for k in keys], None, d_resid)
        token = scatter[s][-1]
    dx = d_resid

    out = {}
    sm_all = sm_names + list(REPLICATED)
    g_pack = _pack([g_small[n] for n in sm_all])
    small_first = split_start("small_grads_start", "gather", [g_pack], None, dx)
    chain = {}

    def stored(t, col):
        return jnp.swapaxes(t, 1, 2) if col else t

    for s in reversed(range(1, len(stages))):
        lands = split_wait(f"grads{s}_wait", "scatter", scatter[s], small_first[-1] if s == len(stages) - 1 else dx)
        for own, land, (n, l, col) in zip(own32[s], lands, stages[s][3], strict=True):
            chain[n] = adamw_sharded(f"adamw_{n}_{l}", l, stored(w[n], col), stored(m[n], col), stored(v[n], col), own, land, me1,
                                     chain.get(n), col)
            behind = chain[n][0]

    lands = split_wait("small_grads_wait", "gather", small_first, behind)
    small_second = split_start("small_forward_start", "forward", [], lands, behind)
    (g_all,) = split_wait("small_forward_wait", "forward", small_second, small_second[-1])
    g_sum = _unpack(ordered_sum("sum_small_grads", g_all, g_pack, me1), [g_small[n].shape for n in sm_all])
    g_loc = []
    for n, g in zip(sm_all, g_sum):
        if n in SMALL_SHARDED:
            ax = SMALL_SHARDED[n]
            size = w[n].shape[ax]
            g = lax.dynamic_slice_in_dim(g, me * size, size, axis=ax)
        g_loc.append(g)
    shapes = [w[n].shape for n in sm_all]
    res = adamw("adamw_small", _pack([w[n] for n in sm_all]), _pack([m[n] for n in sm_all]), _pack([v[n] for n in sm_all]), _pack(g_loc))
    behind = res[0]
    res = [g_loc] + [_unpack(r, shapes) for r in res]
    for k, n in enumerate(sm_all):
        out[n] = [r[k] for r in res]

    lands = split_wait("grads0_wait", "scatter", scatter[0], behind)
    for own, land, (n, l, col) in zip(own32[0], lands, stages[0][3], strict=True):
        chain[n] = adamw_sharded(f"adamw_{n}_{l}", l, stored(w[n], col), stored(m[n], col), stored(v[n], col), own, land, me1,
                                 chain.get(n), col)
    for n in BIG:
        out[n] = [stored(t, BIG[n] == 2) for t in chain[n]]

    grads, deltas, new_m, new_v = ([out[n][k] for n in WEIGHTS] for k in range(4))
    return (loss, dx.reshape(Bl, S, D), *grads, *deltas, *new_m, *new_v)
```

```python
import functools

import numpy as np
import jax
import jax.numpy as jnp
from jax import lax
from jax.experimental import pallas as pl
from jax.experimental.pallas import tpu as pltpu

F32 = jnp.float32
BF16 = jnp.bfloat16
HIGHEST = lax.Precision.HIGHEST
MESH = pl.DeviceIdType.MESH

D_MODEL = 1024
DEPTH = 4
GLA_HEADS, GLA_DK, GLA_DV, GLA_GATE_NORM = 4, 64, 128, 16.0
HGRN_HEADS, HGRN_DK, HGRN_DV, HGRN_MIN_F = 4, 64, 128, 1e-20
RET_HEADS, RET_DK, RET_DV = 4, 128, 192
ROPE_BASE = 10000.0
S5_WIDTH, S5_GROUPS, S5_STATE = 256, 16, 64
S5_N = S5_GROUPS * S5_STATE
EPS = 1e-6
ADAM_LR, ADAM_B1, ADAM_B2, ADAM_EPS, ADAM_WD, ADAM_STEP = 0.001, 0.9, 0.999, 1e-08, 0.01, 10

N_DEV = 8
VMEM_LIMIT_BYTES = 56 * 1024 * 1024
ROW_TILE = 256
SCAN_CHUNK = 128
HGRN_SCAN_CHUNK = 64
S5_SEGMENTS = 8
LANES = 128


def _params(*sem):
    return pltpu.CompilerParams(dimension_semantics=sem, vmem_limit_bytes=VMEM_LIMIT_BYTES)


def _divisor_tile(n, cap):
    best = None
    for t in range(LANES, min(n, cap) + 1, LANES):
        if n % t == 0:
            best = t
    return best if best is not None else n


def _mm_nn(name, x, w):
    M, K = x.shape
    N = w.shape[1]
    tn = _divisor_tile(N, 3072) if K * N * 2 > 8 * 2**20 else N
    tm = 256 if tn * 4 * 512 > 6 * 2**20 else 512
    assert M % tm == 0 and N % tn == 0

    def body(x_ref, w_ref, o_ref):
        o_ref[...] = jnp.dot(x_ref[...].astype(BF16), w_ref[...], preferred_element_type=F32)

    return pl.pallas_call(
        body, name=name, grid=(N // tn, M // tm),
        in_specs=[pl.BlockSpec((tm, K), lambda j, i: (i, 0)), pl.BlockSpec((K, tn), lambda j, i: (0, j))],
        out_specs=pl.BlockSpec((tm, tn), lambda j, i: (i, j)),
        out_shape=jax.ShapeDtypeStruct((M, N), F32),
        compiler_params=_params("parallel", "parallel"),
    )(x, w)


def _mm_nt(name, dy, w, after=None):
    M, N = dy.shape
    K = w.shape[0]
    tk = _divisor_tile(K, 1024) if K * N * 2 > 8 * 2**20 else K
    tm = 256 if N >= 4096 else 512
    assert M % tm == 0 and K % tk == 0

    def body(dy_ref, w_ref, *rest):
        rest[-1][...] = lax.dot_general(dy_ref[...].astype(BF16), w_ref[...], (((1,), (1,)), ((), ())),
                                        preferred_element_type=F32)

    extra = [] if after is None else [after]
    return pl.pallas_call(
        body, name=name, grid=(K // tk, M // tm),
        in_specs=[pl.BlockSpec((tm, N), lambda j, i: (i, 0)), pl.BlockSpec((tk, N), lambda j, i: (j, 0))]
        + [pl.BlockSpec(memory_space=pl.ANY)] * len(extra),
        out_specs=pl.BlockSpec((tm, tk), lambda j, i: (i, j)),
        out_shape=jax.ShapeDtypeStruct((M, K), F32),
        compiler_params=_params("parallel", "parallel"),
    )(dy, w, *extra)


def _tick():
    return jnp.zeros((SUBLANES, LANES), F32)


MM_TN_VMEM_BUDGET = 46 * 2**20


def _pad_lanes(n):
    return -(-n // LANES) * LANES


def _mm_tn(name, x, dy, nblk, want16):
    M, K = x.shape
    N = dy.shape[1]
    n = N // nblk
    xb, yb = x.dtype.itemsize, dy.dtype.itemsize
    best = None
    for tk in [t for t in range(LANES, K + 1, LANES) if K % t == 0]:
        for tm in (512, 256):
            out_bytes = nblk * tk * _pad_lanes(n) * (6 if want16 else 4)
            vmem = 2 * out_bytes + 2 * tm * tk * xb + 2 * tm * _pad_lanes(N) * yb
            traffic = (K // tk) * M * N * yb + M * K * xb
            if vmem <= MM_TN_VMEM_BUDGET and M % tm == 0 and (best is None or (traffic, -tm) < best[0]):
                best = ((traffic, -tm), tk, tm)
    _, tk, tm = best
    last = M // tm - 1

    def body(x_ref, dy_ref, o32_ref, *o16_ref):
        m = pl.program_id(1)

        @pl.when(m == 0)
        def _():
            o32_ref[...] = jnp.zeros_like(o32_ref)

        dyv = dy_ref[...].astype(BF16)
        if nblk == 1:
            o32_ref[0] += lax.dot_general(x_ref[...].astype(BF16), dyv, (((0,), (0,)), ((), ())), preferred_element_type=F32)
        else:
            xt = x_ref[...].astype(F32).T.astype(BF16)
            for d in range(nblk):
                o32_ref[d] += jnp.dot(xt, dyv[:, d * n:(d + 1) * n], preferred_element_type=F32)
        if want16:
            @pl.when(m == last)
            def _():
                o16_ref[0][...] = o32_ref[...].astype(BF16)

    blk = pl.BlockSpec((nblk, tk, n), lambda a, m: (0, a, 0))
    return pl.pallas_call(
        body, name=name, grid=(K // tk, M // tm),
        in_specs=[pl.BlockSpec((tm, tk), lambda a, m: (m, a)), pl.BlockSpec((tm, N), lambda a, m: (m, 0))],
        out_specs=[blk, blk] if want16 else [blk],
        out_shape=[jax.ShapeDtypeStruct((nblk, K, n), F32)] + ([jax.ShapeDtypeStruct((nblk, K, n), BF16)] if want16 else []),
        compiler_params=_params("parallel", "arbitrary"),
    )(x, dy)


def _weight_grads(name, x, dy, col_sharded):
    if col_sharded:
        return _mm_tn(name, x, dy, N_DEV, True)
    d32, d16 = _mm_tn(name, x, dy, 1, True)
    K, N = d32.shape[1:]
    return d32.reshape(N_DEV, K // N_DEV, N), d16.reshape(N_DEV, K // N_DEV, N)


def make_mm(name, col_sharded=False):
    @jax.custom_vjp
    def mm(x, w16, c32, c16):
        return _mm_nn(name + "_fwd", x, w16), _tick()

    def fwd(x, w16, c32, c16):
        return (_mm_nn(name + "_fwd", x, w16), _tick()), (x, w16)

    def bwd(res, g):
        x, w16 = res
        dy, after = g
        dx = _mm_nt(name + "_dx", dy, w16, after)
        d32, d16 = _weight_grads(name + "_dw", x, dy, col_sharded)
        return dx, jnp.zeros_like(w16), d32, d16

    mm.defvjp(fwd, bwd)
    return mm


def make_mm_groups(name, G, tm=512):
    def products(x, w16):
        M, K = x.shape
        N = w16.shape[1] // G

        def body(x_ref, w_ref, *o_refs):
            xv = x_ref[...].astype(BF16)
            for g, o_ref in enumerate(o_refs):
                o_ref[...] = jnp.dot(xv, w_ref[:, g * N:(g + 1) * N], preferred_element_type=F32)

        row = pl.BlockSpec((tm, N), lambda i: (i, 0))
        return tuple(pl.pallas_call(
            body, name=name + "_fwd", grid=(M // tm,),
            in_specs=[pl.BlockSpec((tm, K), lambda i: (i, 0)), pl.BlockSpec((K, G * N), lambda i: (0, 0))],
            out_specs=[row] * G, out_shape=[jax.ShapeDtypeStruct((M, N), F32)] * G, compiler_params=_params("parallel"),
        )(x, w16))

    def grads(x, w16, dys):
        M, K = x.shape
        N = w16.shape[1] // G

        def body(x_ref, w_ref, *refs):
            dy_refs, dx_ref, dw_ref = refs[:G], refs[G], refs[G + 1]

            @pl.when(pl.program_id(0) == 0)
            def _():
                dw_ref[...] = jnp.zeros_like(dw_ref)

            xv = x_ref[...].astype(BF16)
            dx = jnp.zeros(dx_ref.shape, F32)
            for g in range(G):
                cols = slice(g * N, (g + 1) * N)
                dyv = dy_refs[g][...].astype(BF16)
                dx = dx + lax.dot_general(dyv, w_ref[:, cols], (((1,), (1,)), ((), ())), preferred_element_type=F32)
                dw_ref[:, cols] += lax.dot_general(xv, dyv, (((0,), (0,)), ((), ())), preferred_element_type=F32)
            dx_ref[...] = dx

        row = pl.BlockSpec((tm, N), lambda i: (i, 0))
        return pl.pallas_call(
            body, name=name + "_bwd", grid=(M // tm,),
            in_specs=[pl.BlockSpec((tm, K), lambda i: (i, 0)), pl.BlockSpec((K, G * N), lambda i: (0, 0))] + [row] * G,
            out_specs=[pl.BlockSpec((tm, K), lambda i: (i, 0)), pl.BlockSpec((K, G * N), lambda i: (0, 0))],
            out_shape=[jax.ShapeDtypeStruct((M, K), F32), jax.ShapeDtypeStruct((K, G * N), F32)],
            compiler_params=_params("arbitrary"),
        )(x, w16, *dys)

    @jax.custom_vjp
    def mm(x, w):
        return products(x, w.astype(BF16))

    def fwd(x, w):
        w16 = w.astype(BF16)
        return products(x, w16), (x, w16)

    def bwd(res, dys):
        x, w16 = res
        return tuple(grads(x, w16, tuple(dys)))

    mm.defvjp(fwd, bwd)
    return mm


def _row_specs(rows, params, consts, tile):
    specs = [pl.BlockSpec((tile, r.shape[1]), lambda i: (i, 0)) for r in rows]
    specs += [pl.BlockSpec(p.shape, lambda i: (0, 0)) for p in params]
    specs += [pl.BlockSpec((tile, c.shape[1]), lambda i, n=c.shape[0] // tile: (i % n, 0)) for c in consts]
    return specs


def _row_fwd(name, f, out_widths, tile, rows, params, consts):
    T = rows[0].shape[0]
    nr, npar, ncon = len(rows), len(params), len(consts)

    def body(*refs):
        r = tuple(x[...] for x in refs[:nr])
        p = tuple(x[...] for x in refs[nr:nr + npar])
        c = tuple(x[...] for x in refs[nr + npar:nr + npar + ncon])
        outs = f(r, p, c)
        for o_ref, o in zip(refs[nr + npar + ncon:], outs, strict=True):
            o_ref[...] = o

    return pl.pallas_call(
        body, name=name + "_fwd", grid=(T // tile,),
        in_specs=_row_specs(rows, params, consts, tile),
        out_specs=[pl.BlockSpec((tile, w), lambda i: (i, 0)) for w in out_widths],
        out_shape=[jax.ShapeDtypeStruct((T, w), F32) for w in out_widths],
        compiler_params=_params("parallel"),
    )(*rows, *params, *consts)


def _row_bwd(name, f, out_widths, tile, rows, params, consts, gouts, dr_dtype=F32):
    T = rows[0].shape[0]
    nr, npar, ncon, nout = len(rows), len(params), len(consts), len(out_widths)

    def body(*refs):
        r = tuple(x[...] for x in refs[:nr])
        p = tuple(x[...] for x in refs[nr:nr + npar])
        c = tuple(x[...] for x in refs[nr + npar:nr + npar + ncon])
        k = nr + npar + ncon
        g = tuple(x[...] for x in refs[k:k + nout])
        dr_refs = refs[k + nout:k + nout + nr]
        dp_refs = refs[k + nout + nr:]
        _, vjp = jax.vjp(lambda r_, p_: tuple(f(r_, p_, c)), r, p)
        dr, dp = vjp(g)
        for ref, val in zip(dr_refs, dr, strict=True):
            ref[...] = val.astype(ref.dtype)
        if npar:
            @pl.when(pl.program_id(0) == 0)
            def _():
                for ref in dp_refs:
                    ref[...] = jnp.zeros_like(ref)

            for ref, val in zip(dp_refs, dp, strict=True):
                ref[...] += val

    outs = pl.pallas_call(
        body, name=name + "_bwd", grid=(T // tile,),
        in_specs=_row_specs(rows, params, consts, tile) + [pl.BlockSpec((tile, w), lambda i: (i, 0)) for w in out_widths],
        out_specs=[pl.BlockSpec((tile, r.shape[1]), lambda i: (i, 0)) for r in rows]
        + [pl.BlockSpec(p.shape, lambda i: (0, 0)) for p in params],
        out_shape=[jax.ShapeDtypeStruct(r.shape, dr_dtype) for r in rows] + [jax.ShapeDtypeStruct(p.shape, F32) for p in params],
        compiler_params=_params("arbitrary"),
    )(*rows, *params, *consts, *gouts)
    return tuple(outs[:nr]), tuple(outs[nr:])


def make_proj_stage(name, f, out_widths, tile=ROW_TILE):
    def run(x, w16, params, consts):
        p = _mm_nn(name + "_mm", x, w16)
        return p, tuple(_row_fwd(name, f, out_widths, tile, (p,), params, consts))

    @jax.custom_vjp
    def op(x, w16, c32, c16, params, consts):
        return run(x, w16, params, consts)[1]

    def fwd(x, w16, c32, c16, params, consts):
        p, outs = run(x, w16, params, consts)
        return outs, (x, w16, p, params, consts)

    def bwd(res, g):
        x, w16, p, params, consts = res
        (dp,), dparams = _row_bwd(name, f, out_widths, tile, (p,), params, consts, tuple(g), dr_dtype=BF16)
        d32, d16 = _weight_grads(name + "_dw", x, dp, True)
        return _mm_nt(name + "_dx", dp, w16), jnp.zeros_like(w16), d32, d16, dparams, tuple(jnp.zeros_like(c) for c in consts)

    op.defvjp(fwd, bwd)
    return op


def make_rowop(name, f, out_widths, tile=ROW_TILE):
    @jax.custom_vjp
    def op(rows, params, consts):
        return tuple(_row_fwd(name, f, out_widths, tile, rows, params, consts))

    def fwd(rows, params, consts):
        return op(rows, params, consts), (rows, params, consts)

    def bwd(res, g):
        rows, params, consts = res
        dr, dp = _row_bwd(name, f, out_widths, tile, rows, params, consts, tuple(g))
        return dr, dp, tuple(jnp.zeros_like(c) for c in consts)

    op.defvjp(fwd, bwd)
    return op


def _rms(x, g):
    return x * lax.rsqrt(jnp.mean(x * x, axis=-1, keepdims=True) + EPS) * g


def _silu(x):
    return x * jax.nn.sigmoid(x)


def _bdot(a, b):
    return jnp.dot(a.astype(BF16), b.astype(BF16), preferred_element_type=F32)


def norm_f(rows, params, consts):
    return (_rms(rows[0], params[0]),)


def addnorm_f(rows, params, consts):
    x = rows[0] + rows[1]
    return x, _rms(x, params[0])


EVEN_GLA_END = 1568


def even_prep_f(rows, params, consts):
    (p,) = rows
    wa2f, wa2b, baf, bab, lbf, lbb = params
    gq = p[:, 0:256]
    gk = p[:, 256:512] * (GLA_DK ** -0.5)
    gv = p[:, 512:1024]
    gr = p[:, 1024:1536]
    glaf = jax.nn.log_sigmoid(_bdot(p[:, 1536:1552], wa2f) + baf) / GLA_GATE_NORM
    glab = jax.nn.log_sigmoid(_bdot(p[:, 1552:1568], wa2b) + bab) / GLA_GATE_NORM
    o = EVEN_GLA_END
    hq = _silu(p[:, o:o + 256])

    def gate(z, lb):
        f = lb + (1.0 - lb) * jax.nn.sigmoid(z)
        return jnp.log(jnp.maximum(f, HGRN_MIN_F)), (1.0 - lb) * jax.nn.sigmoid(-z)

    hlaf, hkf = gate(p[:, o + 256:o + 512], lbf)
    hlab, hkb = gate(p[:, o + 512:o + 768], lbb)
    hv = p[:, o + 768:o + 1280]
    hg = p[:, o + 1280:o + 1792]
    return gq, gk, gv, gq, gk, gv, glaf, glab, gr, hq, hkf, hv, hq, hkb, hv, hlaf, hlab, hg


EVEN_PREP_WIDTHS = (256, 256, 512, 256, 256, 512, 256, 256, 512, 256, 256, 512, 256, 256, 512, 256, 256, 512)


def _head_rms(o, g, heads, d):
    parts = []
    for h in range(heads):
        seg = o[:, h * d:(h + 1) * d]
        parts.append(seg * lax.rsqrt(jnp.mean(seg * seg, axis=-1, keepdims=True) + EPS))
    return jnp.concatenate(parts, axis=1) * g


def even_post_f(rows, params, consts):
    of, ob, hof, hob, gr, hg = rows
    a = _head_rms(of + ob, params[0], GLA_HEADS, GLA_DV) * _silu(gr)
    b = _head_rms(hof + hob, params[1], HGRN_HEADS, HGRN_DV) * _silu(hg)
    return (jnp.concatenate([a, b], axis=1),)


@jax.custom_vjp
def _roll_half(x):
    return pltpu.roll(x, RET_DK // 2, 1)


_roll_half.defvjp(lambda x: (_roll_half(x), None), lambda _, g: (_roll_half(g),))


def odd_prep_f(rows, params, consts):
    (p,) = rows
    cosf, sinf = consts

    def rot(t):
        parts = []
        for h in range(RET_HEADS):
            th = t[:, h * RET_DK:(h + 1) * RET_DK]
            parts.append(th * cosf + _roll_half(th) * sinf)
        return jnp.concatenate(parts, axis=1)

    rq = rot(p[:, 0:512])
    rk = rot(p[:, 512:1024]) * (RET_DK ** -0.5)
    rv = p[:, 1024:1792]
    return rq, rk, rv, rq, rk, rv, p[:, 1792:2560], p[:, 2560:2816]


ODD_PREP_WIDTHS = (512, 512, 768, 512, 512, 768, 768, 256)


def ret_post_f(rows, params, consts):
    of, ob, rg = rows
    o = of + ob
    parts = []
    for h in range(RET_HEADS):
        seg = o[:, h * RET_DV:(h + 1) * RET_DV]
        c = seg - jnp.mean(seg, axis=-1, keepdims=True)
        parts.append(c * lax.rsqrt(jnp.mean(c * c, axis=-1, keepdims=True) + EPS))
    return (jnp.concatenate(parts, axis=1) * params[0] * _silu(rg),)


def s5_post_f(rows, params, consts):
    h0r, h0i, h1r, h1i, u = rows
    c_re, c_im, d_skip, glu_w, glu_b = params
    hr = h0r + h1r
    hi = h0i + h1i
    y = _bdot(hr, c_re) - _bdot(hi, c_im) + d_skip * u
    g = jax.nn.gelu(y)
    return (g * jax.nn.sigmoid(_bdot(g, glu_w) + glu_b),)


def loss_head(x, r, g, target, tile=ROW_TILE):
    T, D = x.shape

    def body(x_ref, r_ref, g_ref, t_ref, loss_ref, dx_ref, dg_ref):
        t = t_ref[...]

        def lf(xv, gv):
            e = _rms(xv, gv) - t
            row = jnp.sum(e * e, axis=-1, keepdims=True) * (0.5 / D)
            return jnp.sum(row, axis=0, keepdims=True)

        l, vjp = jax.vjp(lf, x_ref[...] + r_ref[...], g_ref[...])
        dx, dg = vjp(jnp.ones((1, 1), F32))
        dx_ref[...] = dx

        @pl.when(pl.program_id(0) == 0)
        def _():
            loss_ref[...] = jnp.zeros_like(loss_ref)
            dg_ref[...] = jnp.zeros_like(dg_ref)

        loss_ref[...] += jnp.broadcast_to(l, loss_ref.shape)
        dg_ref[...] += dg

    row = pl.BlockSpec((tile, D), lambda i: (i, 0))
    vec = pl.BlockSpec((1, D), lambda i: (0, 0))
    return pl.pallas_call(
        body, name="loss_head", grid=(T // tile,),
        in_specs=[row, row, vec, row],
        out_specs=[pl.BlockSpec((1, LANES), lambda i: (0, 0)), row, vec],
        out_shape=[jax.ShapeDtypeStruct((1, LANES), F32), jax.ShapeDtypeStruct((T, D), F32), jax.ShapeDtypeStruct((1, D), F32)],
        compiler_params=_params("arbitrary"),
    )(x, r, g, target)


SUBLANES = 8


def _halo_specs(width, tile, T):
    n8 = tile // SUBLANES
    last = T // SUBLANES - 1
    return [pl.BlockSpec((tile, width), lambda i: (i, 0)),
            pl.BlockSpec((SUBLANES, width), lambda i: (jnp.maximum(i * n8 - 1, 0), 0)),
            pl.BlockSpec((SUBLANES, width), lambda i: (jnp.minimum((i + 1) * n8, last), 0))]


def _shift_rows(x, prev_row, next_row, tile):
    row = lax.broadcasted_iota(jnp.int32, (tile, 1), 0)
    down = jnp.where(row == 0, prev_row, pltpu.roll(x, 1, 0))
    up = jnp.where(row == tile - 1, next_row, pltpu.roll(x, tile - 1, 0))
    return down, up


def _conv_fwd(name, u, cw, cb, S, tile):
    T, F2 = u.shape
    F = F2 // 2
    per_seq = S // tile

    def body(u_ref, up_ref, un_ref, cw_ref, cb_ref, g_ref):
        pos = pl.program_id(0) % per_seq
        uv = u_ref[...]
        prev_row = jnp.where(pos == 0, 0.0, up_ref[SUBLANES - 1:SUBLANES, :])
        next_row = jnp.where(pos == per_seq - 1, 0.0, un_ref[0:1, :])
        down, up = _shift_rows(uv, prev_row, next_row, tile)
        c = _conv_taps(down, uv, up, cw_ref, cb_ref)
        g_ref[...] = (_silu(c[:, :F]) * c[:, F:]).astype(BF16)

    return pl.pallas_call(
        body, name=name + "_fwd", grid=(T // tile,),
        in_specs=_halo_specs(F2, tile, T) + [pl.BlockSpec((3, F2), lambda i: (0, 0)), pl.BlockSpec((1, F2), lambda i: (0, 0))],
        out_specs=pl.BlockSpec((tile, F), lambda i: (i, 0)),
        out_shape=jax.ShapeDtypeStruct((T, F), BF16),
        compiler_params=_params("parallel"),
    )(u, u, u, cw, cb)


def _conv_taps(down, mid, up, cw_ref, cb_ref):
    c = cb_ref[...] + down * cw_ref[0:1, :]
    c = c + mid * cw_ref[1:2, :]
    return c + up * cw_ref[2:3, :]


def _conv_bwd(name, u, dg, cw, cb, S, tile):
    T, F2 = u.shape
    F = F2 // 2
    per_seq = S // tile

    def dact(cv, dgv):
        a, v = cv[:, :F], cv[:, F:]
        sg = jax.nn.sigmoid(a)
        return jnp.concatenate([dgv * v * (sg * (1.0 + a * (1.0 - sg))), dgv * (a * sg)], axis=1)

    def body(u_ref, up_ref, un_ref, g_ref, gp_ref, gn_ref, cw_ref, cb_ref, du_ref, dw0_ref, dw1_ref, dw2_ref, db_ref):
        i = pl.program_id(0)
        pos = i % per_seq
        first, last = pos == 0, pos == per_seq - 1
        lo, hi = slice(SUBLANES - 1, SUBLANES), slice(0, 1)
        uv = u_ref[...]
        u_m1, u_p1 = jnp.where(first, 0.0, up_ref[lo, :]), jnp.where(last, 0.0, un_ref[hi, :])
        u_dn, u_up = _shift_rows(uv, u_m1, u_p1, tile)
        dc = dact(_conv_taps(u_dn, uv, u_up, cw_ref, cb_ref), g_ref[...])
        c_m1 = _conv_taps(up_ref[SUBLANES - 2:SUBLANES - 1, :], u_m1, u_ref[0:1, :], cw_ref, cb_ref)
        c_p1 = _conv_taps(u_ref[tile - 1:tile, :], u_p1, un_ref[1:2, :], cw_ref, cb_ref)
        dc_prev = jnp.where(first, 0.0, dact(c_m1, gp_ref[lo, :]))
        dc_next = jnp.where(last, 0.0, dact(c_p1, gn_ref[hi, :]))
        dc_dn, dc_up = _shift_rows(dc, dc_prev, dc_next, tile)
        du = dc_up * cw_ref[0:1, :]
        du = du + dc * cw_ref[1:2, :]
        du_ref[...] = (du + dc_dn * cw_ref[2:3, :]).astype(BF16)

        @pl.when(i == 0)
        def _():
            for ref in (dw0_ref, dw1_ref, dw2_ref, db_ref):
                ref[...] = jnp.zeros_like(ref)

        dw0_ref[...] += jnp.sum(dc * u_dn, axis=0, keepdims=True)
        dw1_ref[...] += jnp.sum(dc * uv, axis=0, keepdims=True)
        dw2_ref[...] += jnp.sum(dc * u_up, axis=0, keepdims=True)
        db_ref[...] += jnp.sum(dc, axis=0, keepdims=True)

    vec = pl.BlockSpec((1, F2), lambda i: (0, 0))
    du, dw0, dw1, dw2, db = pl.pallas_call(
        body, name=name + "_bwd", grid=(T // tile,),
        in_specs=_halo_specs(F2, tile, T) + _halo_specs(F, tile, T) + [pl.BlockSpec((3, F2), lambda i: (0, 0)), vec],
        out_specs=[pl.BlockSpec((tile, F2), lambda i: (i, 0)), vec, vec, vec, vec],
        out_shape=[jax.ShapeDtypeStruct((T, F2), BF16)] + [jax.ShapeDtypeStruct((1, F2), F32)] * 4,
        compiler_params=_params("arbitrary"),
    )(u, u, u, dg, dg, dg, cw, cb)
    return du, jnp.concatenate([dw0, dw1, dw2], axis=0), db


def make_ffn(name, S):
    def run(x, wup16, wdn16, cw, cb):
        u = _mm_nn(name + "_up", x, wup16)
        g16 = _conv_fwd(name + "_conv", u, cw, cb, S, ROW_TILE)
        return u, g16, _mm_nn(name + "_down", g16, wdn16)

    @jax.custom_vjp
    def op(x, wup16, cu32, cu16, wdn16, cd32, cd16, cw, cb):
        return run(x, wup16, wdn16, cw, cb)[2], _tick()

    def fwd(x, wup16, cu32, cu16, wdn16, cd32, cd16, cw, cb):
        u, g16, out = run(x, wup16, wdn16, cw, cb)
        return (out, _tick()), (x, wup16, wdn16, u, g16, cw, cb)

    def bwd(res, g):
        x, wup16, wdn16, u, g16, cw, cb = res
        dout, after = g
        dg = _mm_nt(name + "_down_dx", dout, wdn16, after)
        dd32, dd16 = _weight_grads(name + "_down_dw", g16, dout, False)
        du16, dcw, dcb = _conv_bwd(name + "_conv", u, dg, cw, cb, S, ROW_TILE // 2)
        du32, du16w = _weight_grads(name + "_up_dw", x, du16, True)
        return (_mm_nt(name + "_up_dx", du16, wup16), jnp.zeros_like(wup16), du32, du16w, jnp.zeros_like(wdn16), dd32, dd16, dcw, dcb)

    op.defvjp(fwd, bwd)
    return op


def _dot_nt(a, b):
    return lax.dot_general(a.astype(BF16), b.astype(BF16), (((1,), (1,)), ((), ())), preferred_element_type=F32)


def _dot_tn(a, b):
    return lax.dot_general(a.astype(BF16), b.astype(BF16), (((0,), (0,)), ((), ())), preferred_element_type=F32)


def _chunk_decays(la, direction, C, width, dk, const_lg):
    row = lax.broadcasted_iota(jnp.int32, (C, C), 0)
    col = lax.broadcasted_iota(jnp.int32, (C, C), 1)
    keep = (row >= col) if direction == 0 else (row <= col)
    ridx = lax.broadcasted_iota(jnp.int32, (C, 1), 0)
    if const_lg is None:
        cum = jnp.dot(keep.astype(F32), la, precision=HIGHEST, preferred_element_type=F32)
    else:
        lane_head = lax.broadcasted_iota(jnp.int32, (1, width), 1) // dk
        lg = jnp.zeros((1, width), F32)
        for h, val in enumerate(const_lg):
            lg = jnp.where(lane_head == h, val, lg)
        steps = (ridx + 1) if direction == 0 else (C - ridx)
        cum = steps.astype(F32) * lg
    exit_row = C - 1 if direction == 0 else 0
    mid = jnp.sum(jnp.where(ridx == C // 2, cum, 0.0), axis=0, keepdims=True)
    last = jnp.sum(jnp.where(ridx == exit_row, cum, 0.0), axis=0, keepdims=True)
    return keep, ridx == exit_row, cum, mid, last


def _scan_fwd(name, qkv, laf, lab, H, dk, dv, S, C, const_lg):
    qf, kf, vf, qb, kb, vb = qkv
    T = qf.shape[0]
    B, nc = T // S, S // C
    Wk, Wv = H * dk, H * dv
    learn = const_lg is None

    def body(*refs):
        @pl.when(pl.program_id(0) == 0)
        def _():
            refs[-1][...] = jnp.zeros_like(refs[-1])

        for b in range(B):
            one_sequence(*[r.at[b] for r in refs])

    def one_sequence(*refs):
        if learn:
            qf_r, qb_r, kf_r, kb_r, vf_r, vb_r, laf_r, lab_r, of_r, ob_r, sf_r, sb_r, st = refs
            las = (laf_r[...], lab_r[...])
        else:
            qf_r, qb_r, kf_r, kb_r, vf_r, vb_r, of_r, ob_r, sf_r, sb_r, st = refs
            las = (None, None)

        for d, (q_r, k_r, v_r, o_r, s_r) in enumerate(((qf_r, kf_r, vf_r, of_r, sf_r), (qb_r, kb_r, vb_r, ob_r, sb_r))):
            keep, _, cum, mid, last = _chunk_decays(las[d], d, C, Wk, dk, None if learn else const_lg[d])
            qe = q_r[...] * jnp.exp(cum - mid)
            ke = k_r[...] * jnp.exp(mid - cum)
            q_in = qe * jnp.exp(mid)
            k_out = ke * jnp.exp(last - mid)
            e_last = jnp.exp(last)
            vv = v_r[...]
            for h in range(H):
                ks, vs = slice(h * dk, (h + 1) * dk), slice(h * dv, (h + 1) * dv)
                a = jnp.where(keep, _dot_nt(qe[:, ks], ke[:, ks]), 0.0)
                state = st[d, h]
                o_r[:, vs] = _bdot(a, vv[:, vs]) + _dot_nt(q_in[:, ks], state)
                s_r[h * dv:(h + 1) * dv, :] = state
                st[d, h] = state * e_last[:, ks] + _dot_tn(vv[:, vs], k_out[:, ks])

    fpos = lambda c: c
    bpos = lambda c: nc - 1 - c
    kspec = lambda pos: pl.BlockSpec((B, C, Wk), lambda c: (0, pos(c), 0))
    vspec = lambda pos: pl.BlockSpec((B, C, Wv), lambda c: (0, pos(c), 0))
    sspec = lambda pos: pl.BlockSpec((B, None, Wv, dk), lambda c: (0, pos(c), 0, 0))
    seq = lambda t: t.reshape(B, S, t.shape[1])
    ins = [seq(t) for t in [qf, qb, kf, kb, vf, vb] + ([laf, lab] if learn else [])]
    in_specs = [kspec(fpos), kspec(bpos), kspec(fpos), kspec(bpos), vspec(fpos), vspec(bpos)] + ([kspec(fpos), kspec(bpos)] if learn else [])
    of, ob, sf, sb = pl.pallas_call(
        body, name=name + "_fwd", grid=(nc,), in_specs=in_specs,
        out_specs=[vspec(fpos), vspec(bpos), sspec(fpos), sspec(bpos)],
        out_shape=[jax.ShapeDtypeStruct((B, S, Wv), F32)] * 2 + [jax.ShapeDtypeStruct((B, nc, Wv, dk), F32)] * 2,
        scratch_shapes=[pltpu.VMEM((B, 2, H, dv, dk), F32)],
        compiler_params=_params("arbitrary"),
    )(*ins)
    return of.reshape(T, Wv), ob.reshape(T, Wv), sf, sb


def _scan_bwd(name, qkv, laf, lab, sf, sb, dof, dob, H, dk, dv, S, C, const_lg):
    qf, kf, vf, qb, kb, vb = qkv
    T = qf.shape[0]
    B, nc = T // S, S // C
    Wk, Wv = H * dk, H * dv
    learn = const_lg is None

    def body(*refs):
        @pl.when(pl.program_id(0) == 0)
        def _():
            refs[-1][...] = jnp.zeros_like(refs[-1])

        for b in range(B):
            one_sequence(*[r.at[b] for r in refs])

    def one_sequence(*refs):
        if learn:
            (qf_r, qb_r, kf_r, kb_r, vf_r, vb_r, laf_r, lab_r, sf_r, sb_r, dof_r, dob_r,
             dqf_r, dqb_r, dkf_r, dkb_r, dvf_r, dvb_r, dlaf_r, dlab_r, dst) = refs
            las, dlas = (laf_r[...], lab_r[...]), (dlaf_r, dlab_r)
        else:
            (qf_r, qb_r, kf_r, kb_r, vf_r, vb_r, sf_r, sb_r, dof_r, dob_r,
             dqf_r, dqb_r, dkf_r, dkb_r, dvf_r, dvb_r, dst) = refs
            las, dlas = (None, None), (None, None)

        groups = ((qf_r, kf_r, vf_r, sf_r, dof_r, dqf_r, dkf_r, dvf_r), (qb_r, kb_r, vb_r, sb_r, dob_r, dqb_r, dkb_r, dvb_r))
        for d, (q_r, k_r, v_r, s_r, do_r, dq_r, dk_r, dv_r) in enumerate(groups):
            keep, is_exit, cum, mid, last = _chunk_decays(las[d], d, C, Wk, dk, None if learn else const_lg[d])
            eq, ek = jnp.exp(cum - mid), jnp.exp(mid - cum)
            e_in, e_out, e_last = jnp.exp(mid), jnp.exp(last - mid), jnp.exp(last)
            qe, ke = q_r[...] * eq, k_r[...] * ek
            q_in, k_out = qe * e_in, ke * e_out
            vv, do = v_r[...], do_r[...]
            dqe_parts, dke_parts, dlast_parts = [], [], []
            for h in range(H):
                ks, vs = slice(h * dk, (h + 1) * dk), slice(h * dv, (h + 1) * dv)
                a = jnp.where(keep, _dot_nt(qe[:, ks], ke[:, ks]), 0.0)
                dp = jnp.where(keep, _dot_nt(do[:, vs], vv[:, vs]), 0.0)
                s_prev = s_r[h * dv:(h + 1) * dv, :]
                ds = dst[d, h]
                dk_out = _bdot(vv[:, vs], ds)
                dqe_parts.append(_bdot(dp, ke[:, ks]) + _bdot(do[:, vs], s_prev) * e_in[:, ks])
                dke_parts.append(_dot_tn(dp, qe[:, ks]) + dk_out * e_out[:, ks])
                dv_r[:, vs] = _dot_tn(a, do[:, vs]) + _dot_nt(k_out[:, ks], ds)
                if learn:
                    dlast_parts.append(jnp.sum(dk_out * k_out[:, ks], axis=0, keepdims=True)
                                       + jnp.sum(ds * s_prev, axis=0, keepdims=True) * e_last[:, ks])
                dst[d, h] = ds * e_last[:, ks] + _dot_tn(do[:, vs], q_in[:, ks])
            dqe = jnp.concatenate(dqe_parts, axis=1)
            dke = jnp.concatenate(dke_parts, axis=1)
            dq_r[...] = dqe * eq
            dk_r[...] = dke * ek
            if learn:
                dcum = dqe * qe - dke * ke + jnp.where(is_exit, jnp.concatenate(dlast_parts, axis=1), 0.0)
                dlas[d][...] = lax.dot_general(keep.astype(F32), dcum, (((0,), (0,)), ((), ())), precision=HIGHEST,
                                               preferred_element_type=F32)

    fpos = lambda c: nc - 1 - c
    bpos = lambda c: c
    kspec = lambda pos: pl.BlockSpec((B, C, Wk), lambda c: (0, pos(c), 0))
    vspec = lambda pos: pl.BlockSpec((B, C, Wv), lambda c: (0, pos(c), 0))
    sspec = lambda pos: pl.BlockSpec((B, None, Wv, dk), lambda c: (0, pos(c), 0, 0))
    seq = lambda t: t.reshape(B, S, t.shape[1])
    ins = [seq(t) for t in [qf, qb, kf, kb, vf, vb] + ([laf, lab] if learn else [])] + [sf, sb, seq(dof), seq(dob)]
    in_specs = ([kspec(fpos), kspec(bpos), kspec(fpos), kspec(bpos), vspec(fpos), vspec(bpos)]
                + ([kspec(fpos), kspec(bpos)] if learn else []) + [sspec(fpos), sspec(bpos), vspec(fpos), vspec(bpos)])
    out_specs = [kspec(fpos), kspec(bpos), kspec(fpos), kspec(bpos), vspec(fpos), vspec(bpos)] + ([kspec(fpos), kspec(bpos)] if learn else [])
    out_shape = ([jax.ShapeDtypeStruct((B, S, Wk), F32)] * 4 + [jax.ShapeDtypeStruct((B, S, Wv), F32)] * 2
                 + ([jax.ShapeDtypeStruct((B, S, Wk), F32)] * 2 if learn else []))
    outs = pl.pallas_call(
        body, name=name + "_bwd", grid=(nc,), in_specs=in_specs, out_specs=out_specs, out_shape=out_shape,
        scratch_shapes=[pltpu.VMEM((B, 2, H, dv, dk), F32)],
        compiler_params=_params("arbitrary"),
    )(*ins)
    return [t.reshape(T, t.shape[2]) for t in outs]


def make_scan(name, H, dk, dv, S, C, const_lg=None):
    if const_lg is None:
        @jax.custom_vjp
        def op(qkv, laf, lab):
            return tuple(_scan_fwd(name, qkv, laf, lab, H, dk, dv, S, C, None)[:2])

        def fwd(qkv, laf, lab):
            of, ob, sf, sb = _scan_fwd(name, qkv, laf, lab, H, dk, dv, S, C, None)
            return (of, ob), (qkv, laf, lab, sf, sb)

        def bwd(res, g):
            qkv, laf, lab, sf, sb = res
            dqf, dqb, dkf, dkb, dvf, dvb, dlaf, dlab = _scan_bwd(name, qkv, laf, lab, sf, sb, g[0], g[1], H, dk, dv, S, C, None)
            return (dqf, dkf, dvf, dqb, dkb, dvb), dlaf, dlab
    else:
        @jax.custom_vjp
        def op(qkv):
            return tuple(_scan_fwd(name, qkv, None, None, H, dk, dv, S, C, const_lg)[:2])

        def fwd(qkv):
            of, ob, sf, sb = _scan_fwd(name, qkv, None, None, H, dk, dv, S, C, const_lg)
            return (of, ob), (qkv, sf, sb)

        def bwd(res, g):
            qkv, sf, sb = res
            dqf, dqb, dkf, dkb, dvf, dvb = _scan_bwd(name, qkv, None, None, sf, sb, g[0], g[1], H, dk, dv, S, C, const_lg)
            return ((dqf, dkf, dvf, dqb, dkb, dvb),)

    op.defvjp(fwd, bwd)
    return op


def _reorder_call(name, t, S, to_segments):
    T, w = t.shape
    n_it = S // S5_SEGMENTS

    def body(x_ref, o_ref):
        def step(i, carry):
            packed = pl.ds(pl.multiple_of(i * S5_SEGMENTS, S5_SEGMENTS), S5_SEGMENTS)
            spread = pl.ds(i, S5_SEGMENTS, stride=n_it)
            if to_segments:
                o_ref[packed, :] = x_ref[spread, :]
            else:
                o_ref[spread, :] = x_ref[packed, :]
            return carry

        lax.fori_loop(0, n_it, step, 0, unroll=8)

    blk = pl.BlockSpec((S, LANES), lambda b, j: (b, j))
    return pl.pallas_call(body, name=name, grid=(T // S, w // LANES), in_specs=[blk], out_specs=blk,
                          out_shape=jax.ShapeDtypeStruct(t.shape, t.dtype), compiler_params=_params("parallel", "parallel"))(t)


def make_join(name, S, wa, wb):
    n_it = S // S5_SEGMENTS

    def call(tag, forward, wa, wb, arrs):
        T = arrs[0].shape[0]
        na, nb = wa // LANES, wb // LANES

        def body(*refs):
            j = pl.program_id(1)
            a_ref, b_ref, y_ref = (refs[0], refs[1], refs[2]) if forward else (refs[1], refs[2], refs[0])

            @pl.when(j < na)
            def _():
                if forward:
                    y_ref[...] = a_ref[...]
                else:
                    a_ref[...] = y_ref[...]

            @pl.when(j >= na)
            def _():
                def step(i, carry):
                    packed = pl.ds(pl.multiple_of(i * S5_SEGMENTS, S5_SEGMENTS), S5_SEGMENTS)
                    spread = pl.ds(i, S5_SEGMENTS, stride=n_it)
                    if forward:
                        y_ref[spread, :] = b_ref[packed, :]
                    else:
                        b_ref[packed, :] = y_ref[spread, :]
                    return carry

                lax.fori_loop(0, n_it, step, 0, unroll=8)

        a_spec = pl.BlockSpec((S, LANES), lambda b, j: (b, jnp.minimum(j, na - 1)))
        b_spec = pl.BlockSpec((S, LANES), lambda b, j: (b, jnp.maximum(j - na, 0)))
        y_spec = pl.BlockSpec((S, LANES), lambda b, j: (b, j))
        shapes = [jax.ShapeDtypeStruct((T, wa), F32), jax.ShapeDtypeStruct((T, wb), F32), jax.ShapeDtypeStruct((T, wa + wb), F32)]
        return pl.pallas_call(
            body, name=name + tag, grid=(T // S, na + nb),
            in_specs=[a_spec, b_spec] if forward else [y_spec], out_specs=y_spec if forward else [a_spec, b_spec],
            out_shape=shapes[2] if forward else shapes[:2], compiler_params=_params("parallel", "arbitrary"),
        )(*arrs)

    @jax.custom_vjp
    def op(a, b):
        return call("_fwd", True, wa, wb, (a, b))

    op.defvjp(lambda a, b: (call("_fwd", True, wa, wb, (a, b)), None), lambda _, dy: tuple(call("_bwd", False, wa, wb, (dy,))))
    return op


def make_reorder(name, S, to_segments):
    @jax.custom_vjp
    def op(t):
        return _reorder_call(name, t, S, to_segments)

    op.defvjp(lambda t: (_reorder_call(name, t, S, to_segments), None),
              lambda _, g: (_reorder_call(name + "_bwd", g, S, not to_segments),))
    return op


def _s5_scan_call(name, Xs, A, S, dirs, prev=None):
    with_p = prev is not None
    T, N = Xs[0].shape
    B, nl = T // S, N // LANES
    n_it = S // S5_SEGMENTS
    assert n_it & (n_it - 1) == 0

    def cmul(ar, ai, br, bi):
        return ar * br - ai * bi, ar * bi + ai * br

    def body(*refs):
        x, a_ref = refs[0:4], refs[4]
        if with_p:
            h_prev, x_prev, h, p_ref = refs[5:9], refs[9:13], refs[13:17], refs[17]
        else:
            h = refs[5:9]
        seg = lax.broadcasted_iota(jnp.int32, (S5_SEGMENTS, 1), 0)
        zero = jnp.zeros((S5_SEGMENTS, LANES), F32)
        a = [(jnp.broadcast_to(a_ref[2 * k:2 * k + 1, :], (S5_SEGMENTS, LANES)),
              jnp.broadcast_to(a_ref[2 * k + 1:2 * k + 2, :], (S5_SEGMENTS, LANES))) for k in range(2)]

        def rows_of(k, i):
            return pl.ds(pl.multiple_of(((n_it - 1 - i) if dirs[k] == 1 else i) * S5_SEGMENTS, S5_SEGMENTS), S5_SEGMENTS)

        def local(i, carry):
            out = []
            for k, (sr, si) in enumerate(carry):
                rows = rows_of(k, i)
                pr, pi = cmul(*a[k], sr, si)
                sr, si = pr + x[2 * k][rows, :], pi + x[2 * k + 1][rows, :]
                h[2 * k][rows, :] = sr
                h[2 * k + 1][rows, :] = si
                out.append((sr, si))
            return tuple(out)

        ends = lax.fori_loop(0, n_it, local, ((zero, zero), (zero, zero)), unroll=8)
        inherit = []
        for k, (er, ei) in enumerate(ends):
            back = dirs[k] == 1
            pr, pi = a[k]
            for _ in range(n_it.bit_length() - 1):
                pr, pi = cmul(pr, pi, pr, pi)
            shift = (S5_SEGMENTS - 1) if back else 1
            tr, ti = er, ei
            for r in (range(S5_SEGMENTS - 2, -1, -1) if back else range(1, S5_SEGMENTS)):
                nr, ni = cmul(pr, pi, pltpu.roll(tr, shift, 0), pltpu.roll(ti, shift, 0))
                tr = jnp.where(seg == r, er + nr, tr)
                ti = jnp.where(seg == r, ei + ni, ti)
            edge = (S5_SEGMENTS - 1) if back else 0
            inherit.append((jnp.where(seg == edge, 0.0, pltpu.roll(tr, shift, 0)), jnp.where(seg == edge, 0.0, pltpu.roll(ti, shift, 0))))

        def fix(i, carry):
            powers, sums = carry
            new_powers, new_sums = [], []
            for k in range(2):
                rows = rows_of(k, i)
                fr, fi = cmul(*powers[k], *inherit[k])
                sr, si = h[2 * k][rows, :] + fr, h[2 * k + 1][rows, :] + fi
                h[2 * k][rows, :] = sr
                h[2 * k + 1][rows, :] = si
                new_powers.append(cmul(*powers[k], *a[k]))
                if with_p:
                    ur = h_prev[2 * k][rows, :] - x_prev[2 * k][rows, :]
                    ui = h_prev[2 * k + 1][rows, :] - x_prev[2 * k + 1][rows, :]
                    new_sums.append((sums[k][0] + sr * ur + si * ui, sums[k][1] + si * ur - sr * ui))
            return tuple(new_powers), tuple(new_sums)

        _, sums = lax.fori_loop(0, n_it, fix, ((a[0], a[1]), ((zero, zero), (zero, zero)) if with_p else ()), unroll=8)
        if with_p:
            for k in range(2):
                p_ref[2 * k:2 * k + 1, :] = jnp.sum(sums[k][0], axis=0, keepdims=True)
                p_ref[2 * k + 1:2 * k + 2, :] = jnp.sum(sums[k][1], axis=0, keepdims=True)

    col = pl.BlockSpec((S, LANES), lambda b, j: (b, j))
    outs = pl.pallas_call(
        body, name=name, grid=(B, nl),
        in_specs=[col] * 4 + [pl.BlockSpec((4, LANES), lambda b, j: (0, j))] + ([col] * 8 if with_p else []),
        out_specs=[col] * 4 + ([pl.BlockSpec((None, 4, LANES), lambda b, j: (b, 0, j))] if with_p else []),
        out_shape=[jax.ShapeDtypeStruct((T, N), F32)] * 4 + ([jax.ShapeDtypeStruct((B, 4, N), F32)] if with_p else []),
        compiler_params=_params("parallel", "parallel"),
    )(*Xs, A, *(prev[0] + prev[1] if with_p else ()))
    return (tuple(outs[:4]), outs[4]) if with_p else tuple(outs)


def make_s5_scan(name, S):
    @jax.custom_vjp
    def op(X, A):
        return _s5_scan_call(name + "_fwd", X, A, S, (0, 1))

    def fwd(X, A):
        H = _s5_scan_call(name + "_fwd", X, A, S, (0, 1))
        return H, (X, A, H)

    def bwd(res, G):
        X, A, H = res
        conj = A * jnp.array([[1.0], [-1.0], [1.0], [-1.0]], F32)
        lam, P = _s5_scan_call(name + "_bwd", tuple(G), conj, S, (1, 0), prev=(tuple(H), tuple(X)))
        P = jnp.sum(P, axis=0)
        ar, ai = A[0::2], A[1::2]
        pr, pi = P[0::2], P[1::2]
        den = ar * ar + ai * ai
        dar, dai = (pr * ar - pi * ai) / den, (pr * ai + pi * ar) / den
        return lam, jnp.stack([dar[0], dai[0], dar[1], dai[1]], axis=0)

    op.defvjp(fwd, bwd)
    return op


ANY = pl.BlockSpec(memory_space=pl.ANY)


def _place():
    x, y, c = lax.axis_index("x"), lax.axis_index("y"), lax.axis_index("c")
    return x, y, c, [(1 - x, y), (x, 1 - y), (1 - x, 1 - y)]


def all_gather(name, arrs):
    n = len(arrs)

    def body(*refs):
        ins, outs = refs[:n], refs[n:2 * n]
        send, recv, lsem = refs[2 * n:]
        x, y, c, chips = _place()
        me, sibling = (x, y, c), (x, y, 1 - c)

        def copy(a, k, block, to, src=None):
            slot = outs[a].at[4 * block[0] + 2 * block[1] + block[2]]
            return pltpu.make_async_remote_copy(src_ref=slot if src is None else src, dst_ref=slot, send_sem=send.at[a, k],
                                                recv_sem=recv.at[a, k], device_id=to, device_id_type=MESH)

        mine = [pltpu.make_async_copy(ins[a], outs[a].at[4 * x + 2 * y + c], lsem.at[a]) for a in range(n)]
        first = []
        for a in range(n):
            mine[a].start()
            first.append(copy(a, 0, me, sibling, src=ins[a]))
            first += [copy(a, 1 + j, me, (*chip, c), src=ins[a]) for j, chip in enumerate(chips)]
        for cp in first:
            cp.start()
        passed = []
        for a in range(n):
            for j, chip in enumerate(chips):
                copy(a, 1 + j, (*chip, c), me).wait_recv()
                fwd = copy(a, 4 + j, (*chip, c), sibling)
                fwd.start()
                passed.append(fwd)
        for a in range(n):
            copy(a, 0, sibling, me).wait_recv()
            for j, chip in enumerate(chips):
                copy(a, 4 + j, (*chip, 1 - c), me).wait_recv()
        for cp in first + passed:
            cp.wait_send()
        for cp in mine:
            cp.wait()

    return pl.pallas_call(
        body, name=name, in_specs=[ANY] * n, out_specs=[ANY] * n,
        out_shape=[jax.ShapeDtypeStruct((N_DEV,) + a.shape, a.dtype) for a in arrs],
        scratch_shapes=[pltpu.SemaphoreType.DMA((n, 7)), pltpu.SemaphoreType.DMA((n, 7)), pltpu.SemaphoreType.DMA((n,))],
    )(*arrs)


HBM = pl.BlockSpec(memory_space=pltpu.HBM)
SEM = pl.BlockSpec(memory_space=pltpu.SEMAPHORE)
EFFECT = pltpu.SideEffectType.DATAFLOW_SIDE_EFFECTING
GATHER_PEERS = (1, 2, 4, 6)
OTHER_CHIPS = (2, 4, 6)
COPIES_PER_ARRAY = {"scatter": N_DEV - 1, "gather": len(GATHER_PEERS), "forward": len(OTHER_CHIPS)}


def _split_plan(mode, srcs, lands, send, recv):
    x, y, c = lax.axis_index("x"), lax.axis_index("y"), lax.axis_index("c")

    def dev(k):
        return (1 - x if k & 4 else x), (1 - y if k & 2 else y), (1 - c if k & 1 else c)

    def idx(d):
        return 4 * d[0] + 2 * d[1] + d[2]

    me = idx((x, y, c))
    plan = []
    for a, land in enumerate(lands):
        if mode == "scatter":
            legs = [(srcs[a].at[idx(dev(k))], land.at[me], land.at[idx(dev(k))], dev(k)) for k in range(1, N_DEV)]
        elif mode == "gather":
            legs = [(srcs[a], land.at[me], land.at[idx(dev(k))], dev(k)) for k in GATHER_PEERS]
        else:
            legs = [(land.at[idx(dev(j))], land.at[idx(dev(j))], land.at[idx(dev(j ^ 1))], dev(1)) for j in OTHER_CHIPS]
        for i, (src, dst, arrival, to) in enumerate(legs):
            sem = a * len(legs) + i
            pair = tuple(pltpu.make_async_remote_copy(src_ref=src, dst_ref=d, send_sem=send.at[sem], recv_sem=recv.at[sem],
                                                      device_id=to, device_id_type=MESH) for d in (dst, arrival))
            plan.append(pair)
    return plan


def split_start(name, mode, srcs, lands, after):
    if lands is None:
        lands = [lax.empty((N_DEV,) + (s.shape[1:] if mode == "scatter" else s.shape), s.dtype) for s in srcs]
    ns, nl = len(srcs), len(lands)
    nsem = COPIES_PER_ARRAY[mode] * nl

    def body(*refs):
        ins, lnd = refs[:ns], refs[ns:ns + nl]
        send, recv = refs[ns + nl + 1], refs[ns + nl + 2]
        token = refs[-1]
        for out, _ in _split_plan(mode, ins, lnd, send, recv):
            out.start()
        token[...] = jnp.zeros_like(token)

    arrs = list(srcs) + list(lands)
    return pl.pallas_call(
        body, name=name,
        out_shape=(pltpu.SemaphoreType.DMA((nsem,)), pltpu.SemaphoreType.DMA((nsem,)))
        + tuple(pltpu.HBM(t.shape, t.dtype) for t in arrs) + (jax.ShapeDtypeStruct((SUBLANES, LANES), F32),),
        in_specs=[HBM] * len(arrs) + [ANY],
        out_specs=(SEM, SEM) + (HBM,) * len(arrs) + (pl.BlockSpec(memory_space=pltpu.VMEM),),
        input_output_aliases={i: 2 + i for i in range(len(arrs))},
        compiler_params=pltpu.CompilerParams(has_side_effects=EFFECT),
    )(*[pltpu.with_memory_space_constraint(t, pltpu.HBM) for t in arrs], after)


def split_wait(name, mode, handle, after):
    send, recv = handle[0], handle[1]
    arrs = list(handle[2:-1])
    nl = len(arrs) if mode == "forward" else len(arrs) // 2
    ns = len(arrs) - nl

    def body(*refs):
        ins, lnd = refs[:ns], refs[ns:ns + nl]
        s, r = refs[ns + nl], refs[ns + nl + 1]
        for out, arrival in _split_plan(mode, ins, lnd, s, r):
            out.wait_send()
            arrival.wait_recv()

    outs = pl.pallas_call(
        body, name=name,
        out_shape=tuple(pltpu.HBM(t.shape, t.dtype) for t in arrs),
        in_specs=[HBM] * len(arrs) + [SEM, SEM, ANY], out_specs=(HBM,) * len(arrs),
        input_output_aliases={i: i for i in range(len(arrs))},
        compiler_params=pltpu.CompilerParams(has_side_effects=EFFECT),
    )(*arrs, send, recv, after)
    return list(outs[ns:])


def _row_tile(rows, cols):
    cap = max(SUBLANES, (2**18 // cols) // SUBLANES * SUBLANES)
    if rows <= cap:
        return rows
    for t in range(cap, SUBLANES - 1, -SUBLANES):
        if rows % t == 0:
            return t
    return rows


def ordered_sum(name, parts, own, me):
    n, R, C = parts.shape
    rt = _row_tile(R, C)

    def body(me_ref, p_ref, own_ref, o_ref):
        s = None
        for k in range(n):
            t = jnp.where(me_ref[0] == k, own_ref[...], p_ref[k])
            s = t if s is None else s + t
        o_ref[...] = s

    return pl.pallas_call(
        body, name=name,
        grid_spec=pltpu.PrefetchScalarGridSpec(
            num_scalar_prefetch=1, grid=(R // rt,),
            in_specs=[pl.BlockSpec((n, rt, C), lambda r, p: (0, r, 0)), pl.BlockSpec((rt, C), lambda r, p: (r, 0))],
            out_specs=pl.BlockSpec((rt, C), lambda r, p: (r, 0))),
        out_shape=jax.ShapeDtypeStruct((R, C), F32), compiler_params=_params("parallel"),
    )(me, parts, own)


def _adamw_update(w, m, v, g):
    bias1 = 1.0 - ADAM_B1 ** ADAM_STEP
    bias2 = 1.0 - ADAM_B2 ** ADAM_STEP
    m_new = ADAM_B1 * m + (1.0 - ADAM_B1) * g
    v_new = ADAM_B2 * v + (1.0 - ADAM_B2) * (g * g)
    delta = -ADAM_LR * ((m_new / bias1) / (jnp.sqrt(v_new / bias2) + ADAM_EPS) + ADAM_WD * w)
    return delta, m_new, v_new


def adamw(name, w, m, v, g):
    R, C = w.shape
    rt = _row_tile(R, C)

    def body(w_ref, m_ref, v_ref, g_ref, d_ref, mo_ref, vo_ref):
        d_ref[...], mo_ref[...], vo_ref[...] = _adamw_update(w_ref[...], m_ref[...], v_ref[...], g_ref[...])

    row = pl.BlockSpec((rt, C), lambda r: (r, 0))
    return pl.pallas_call(
        body, name=name, grid=(R // rt,), in_specs=[row] * 4, out_specs=[row] * 3,
        out_shape=[jax.ShapeDtypeStruct((R, C), F32)] * 3, compiler_params=_params("parallel"),
    )(w, m, v, g)


def adamw_sharded(name, layer, w, m, v, own, land, me, prev, transposed):
    _, R, C = own.shape
    rt = _row_tile(R, C)

    def body(me_ref, w_ref, m_ref, v_ref, own_ref, land_ref, *rest):
        go_ref, d_ref, mo_ref, vo_ref = rest[-4:]
        g = own_ref[...]
        for k in range(N_DEV):
            g = g + jnp.where(me_ref[0] == k, 0.0, land_ref[k].astype(F32))
        if transposed:
            g = g.T
        go_ref[...] = g
        d_ref[...], mo_ref[...], vo_ref[...] = _adamw_update(w_ref[...], m_ref[...], v_ref[...], g)

    if transposed:
        row = pl.BlockSpec((None, C, rt), lambda r, p: (layer, 0, r))
    else:
        row = pl.BlockSpec((None, rt, C), lambda r, p: (layer, r, 0))
    in_specs = [row, row, row, pl.BlockSpec((None, rt, C), lambda r, p: (p[0], r, 0)), pl.BlockSpec((N_DEV, rt, C), lambda r, p: (0, r, 0))]
    ins = [me, w, m, v, own, land]
    aliases = {}
    if prev is not None:
        in_specs += [ANY] * 4
        aliases = {len(ins) + k: k for k in range(4)}
        ins += list(prev)
    return pl.pallas_call(
        body, name=name,
        grid_spec=pltpu.PrefetchScalarGridSpec(num_scalar_prefetch=1, grid=(R // rt,), in_specs=in_specs, out_specs=[row] * 4),
        out_shape=[jax.ShapeDtypeStruct(w.shape, F32)] * 4, input_output_aliases=aliases,
        compiler_params=_params("arbitrary"),
    )(*ins)


def _hgrn_lower_bounds(lb_logits):
    p = jax.nn.softmax(lb_logits, axis=1)
    return jnp.cumsum(p, axis=1) - p[:, :1]


def _s5_discretise(lam_re, lam_im, log_dt, b_re, b_im):
    lr = jnp.minimum(lam_re, -1e-4)
    li = lam_im
    dt = jnp.exp(log_dt)[:, None]
    mag = jnp.exp(lr * dt)
    ar, ai = mag * jnp.cos(li * dt), mag * jnp.sin(li * dt)
    den = lr * lr + li * li
    nr = ar - 1.0
    cr = (nr * lr + ai * li) / den
    ci = (ai * lr - nr * li) / den
    bbr = cr[..., None] * b_re - ci[..., None] * b_im
    bbi = cr[..., None] * b_im + ci[..., None] * b_re
    return ar, ai, bbr, bbi


def _block_diag(t):
    G, a, b = t.shape
    eye = jnp.eye(G, dtype=F32)
    return (t[:, :, None, :] * eye[:, None, :, None]).reshape(G * a, G * b)


def _rope_tables(S):
    half = RET_DK // 2
    inv = ROPE_BASE ** (-jnp.arange(half, dtype=F32) / half)
    ang = jnp.arange(S, dtype=F32)[:, None] * inv[None, :]
    cos, sin = jnp.cos(ang), jnp.sin(ang)
    return jnp.concatenate([cos, cos], axis=1), jnp.concatenate([-sin, sin], axis=1)


def _ret_log_decays():
    f = tuple(float(np.log1p(-np.exp2(np.float32(-5.0 - h)))) for h in range(RET_HEADS))
    b = tuple(float(np.log1p(-np.exp2(np.float32(-5.5 - h)))) for h in range(RET_HEADS))
    return f, b


def assemble_weight(name, land, own, me):
    _, R, C = land.shape
    tr = min(R, 256)

    def body(me_ref, land_ref, own_ref, o_ref):
        for d in range(N_DEV):
            o_ref[:, d * C:(d + 1) * C] = jnp.where(me_ref[0] == d, own_ref[...], land_ref[d])

    return pl.pallas_call(
        body, name=name,
        grid_spec=pltpu.PrefetchScalarGridSpec(
            num_scalar_prefetch=1, grid=(R // tr,),
            in_specs=[pl.BlockSpec((N_DEV, tr, C), lambda i, p: (0, i, 0)), pl.BlockSpec((tr, C), lambda i, p: (i, 0))],
            out_specs=pl.BlockSpec((tr, N_DEV * C), lambda i, p: (i, 0))),
        out_shape=jax.ShapeDtypeStruct((R, N_DEV * C), land.dtype), compiler_params=_params("parallel"),
    )(me, land, own)


def _row(t):
    return t.reshape(1, -1)


def mixer_stage(layer, S, resid, branch, w_in, w_out, carriers, small):
    j = layer // 2
    tag = f"l{layer}"
    g = _row(small["mix_norm_g"][layer])
    if branch is None:
        x = resid
        (h,) = make_rowop(tag + "_norm", norm_f, (D_MODEL,))((x,), (g,), ())
    else:
        x, h = make_rowop(tag + "_addnorm", addnorm_f, (D_MODEL, D_MODEL))((resid, branch), (g,), ())
    if layer % 2 == 0:
        lbs = _hgrn_lower_bounds(small["hgrn_lb_logits"])
        prm = (small["gla_wa2"][j, 0], small["gla_wa2"][j, 1], _row(small["gla_ba"][j, 0]), _row(small["gla_ba"][j, 1]),
               _row(lbs[0, j]), _row(lbs[1, j]))
        outs = make_proj_stage(tag + "_prep", even_prep_f, EVEN_PREP_WIDTHS)(h, w_in, carriers["in32"], carriers["in16"], prm, ())
        gla_qkv, (glaf, glab, gr), hgrn_qkv, (hlaf, hlab, hg) = outs[0:6], outs[6:9], outs[9:15], outs[15:18]
        of, ob = make_scan(tag + "_gla", GLA_HEADS, GLA_DK, GLA_DV, S, SCAN_CHUNK)(gla_qkv, glaf, glab)
        hof, hob = make_scan(tag + "_hgrn", HGRN_HEADS, HGRN_DK, HGRN_DV, S, HGRN_SCAN_CHUNK)(hgrn_qkv, hlaf, hlab)
        (y,) = make_rowop(tag + "_post", even_post_f, (D_MODEL,))(
            (of, ob, hof, hob, gr, hg), (_row(small["gla_norm_g"][j]), _row(small["hgrn_norm_g"][j])), ())
    else:
        cosf, sinf = _rope_tables(S)
        outs = make_proj_stage(tag + "_prep", odd_prep_f, ODD_PREP_WIDTHS)(h, w_in, carriers["in32"], carriers["in16"], (), (cosf, sinf))
        ret_qkv, rg, su = outs[0:6], outs[6], outs[7]
        of, ob = make_scan(tag + "_ret", RET_HEADS, RET_DK, RET_DV, S, SCAN_CHUNK, const_lg=_ret_log_decays())(ret_qkv)
        (cm,) = make_rowop(tag + "_retpost", ret_post_f, (RET_HEADS * RET_DV,))((of, ob, rg), (_row(small["ret_norm_g"][j]),), ())
        disc = [_s5_discretise(small["s5_lam_re"][j, d], small["s5_lam_im"][j, d], small["s5_log_dt"][j, d],
                               small["s5_b_re"][j], small["s5_b_im"][j]) for d in range(2)]
        a4 = jnp.stack([t.reshape(-1) for d in range(2) for t in disc[d][:2]], axis=0)
        su_p = make_reorder(tag + "_s5seg", S, True)(su)
        bd = jnp.concatenate([_block_diag(jnp.swapaxes(disc[d][2 + i], 1, 2)) for d in range(2) for i in range(2)], axis=1)
        Xs = make_mm_groups(tag + "_s5in", 4)(su_p, bd)
        Hs = make_s5_scan(tag + "_s5scan", S)(Xs, a4)
        prm = (_block_diag(jnp.swapaxes(small["s5_c_re"][j], 1, 2)), _block_diag(jnp.swapaxes(small["s5_c_im"][j], 1, 2)),
               _row(small["s5_d"][j]), small["s5_glu_w"][j], _row(small["s5_glu_b"][j]))
        (dm_p,) = make_rowop(tag + "_s5post", s5_post_f, (S5_WIDTH,))((*Hs, su_p), prm, ())
        y = make_join(tag + "_join", S, RET_HEADS * RET_DV, S5_WIDTH)(cm, dm_p)
    return (x, *make_mm(tag + "_out")(y, w_out, carriers["out32"], carriers["out16"]))


def ffn_stage(layer, S, resid, branch, w_up, w_down, carriers, small):
    tag = f"l{layer}"
    x, hf = make_rowop(tag + "_ffnnorm", addnorm_f, (D_MODEL, D_MODEL))((resid, branch), (_row(small["ffn_norm_g"][layer]),), ())
    out, tick = make_ffn(tag + "_ffn", S)(hf, w_up, carriers["up32"], carriers["up16"], w_down, carriers["down32"], carriers["down16"],
                                         small["ffn_conv_w"][layer], _row(small["ffn_conv_b"][layer]))
    return x, out, tick


BIG = {"w_in_even": 2, "w_out_even": 1, "w_in_odd": 2, "w_out_odd": 1, "ffn_w_up": 2, "ffn_w_down": 1}
SMALL_SHARDED = {"gla_wa2": 3, "gla_ba": 2, "hgrn_lb_logits": 2, "ret_norm_g": 1, "s5_d": 1, "s5_glu_w": 1, "s5_glu_b": 1,
                 "ffn_conv_w": 2}
REPLICATED = ("mix_norm_g", "ffn_norm_g", "final_norm_g", "gla_norm_g", "hgrn_norm_g", "s5_lam_re", "s5_lam_im", "s5_log_dt",
              "s5_b_re", "s5_b_im", "s5_c_re", "s5_c_im", "ffn_conv_b")
WEIGHTS = ("mix_norm_g", "ffn_norm_g", "final_norm_g", "w_in_even", "w_out_even", "gla_wa2", "gla_ba", "gla_norm_g",
           "hgrn_lb_logits", "hgrn_norm_g", "w_in_odd", "w_out_odd", "ret_norm_g", "s5_lam_re", "s5_lam_im", "s5_log_dt",
           "s5_b_re", "s5_b_im", "s5_c_re", "s5_c_im", "s5_d", "s5_glu_w", "s5_glu_b", "ffn_w_up", "ffn_conv_w", "ffn_conv_b",
           "ffn_w_down")
PACK_COLS = LANES
MIXER_SMALL = (("mix_norm_g", "hgrn_lb_logits", "gla_wa2", "gla_ba", "gla_norm_g", "hgrn_norm_g"),
               ("mix_norm_g", "ret_norm_g", "s5_lam_re", "s5_lam_im", "s5_log_dt", "s5_b_re", "s5_b_im", "s5_c_re", "s5_c_im",
                "s5_d", "s5_glu_w", "s5_glu_b"))
FFN_SMALL = ("ffn_norm_g", "ffn_conv_w", "ffn_conv_b")


def _unshard(g, axis):
    t = jnp.moveaxis(g, 0, axis)
    return t.reshape(t.shape[:axis] + (t.shape[axis] * t.shape[axis + 1],) + t.shape[axis + 2:])


def _pack_rows(shape):
    return -(-int(np.prod(shape)) // (SUBLANES * PACK_COLS)) * SUBLANES


def _pack(arrs):
    parts = []
    for a in arrs:
        rows = _pack_rows(a.shape)
        parts.append(jnp.pad(a.reshape(-1), (0, rows * PACK_COLS - a.size)).reshape(rows, PACK_COLS))
    return jnp.concatenate(parts, axis=0)


def _unpack(packed, shapes):
    lead = packed.shape[:-2]
    out, r = [], 0
    for s in shapes:
        rows, n = _pack_rows(s), int(np.prod(s))
        piece = packed[..., r:r + rows, :].reshape(lead + (rows * PACK_COLS,))
        out.append(piece[..., :n].reshape(lead + tuple(s)))
        r += rows
    return out


def kernel(x, mix_norm_g, ffn_norm_g, final_norm_g, w_in_even, w_out_even, gla_wa2, gla_ba, gla_norm_g, hgrn_lb_logits, hgrn_norm_g, w_in_odd, w_out_odd, ret_norm_g, s5_lam_re, s5_lam_im, s5_log_dt, s5_b_re, s5_b_im, s5_c_re, s5_c_im, s5_d, s5_glu_w, s5_glu_b, ffn_w_up, ffn_conv_w, ffn_conv_b, ffn_w_down, loss_target, m_mix_norm_g, m_ffn_norm_g, m_final_norm_g, m_w_in_even, m_w_out_even, m_gla_wa2, m_gla_ba, m_gla_norm_g, m_hgrn_lb_logits, m_hgrn_norm_g, m_w_in_odd, m_w_out_odd, m_ret_norm_g, m_s5_lam_re, m_s5_lam_im, m_s5_log_dt, m_s5_b_re, m_s5_b_im, m_s5_c_re, m_s5_c_im, m_s5_d, m_s5_glu_w, m_s5_glu_b, m_ffn_w_up, m_ffn_conv_w, m_ffn_conv_b, m_ffn_w_down, v_mix_norm_g, v_ffn_norm_g, v_final_norm_g, v_w_in_even, v_w_out_even, v_gla_wa2, v_gla_ba, v_gla_norm_g, v_hgrn_lb_logits, v_hgrn_norm_g, v_w_in_odd, v_w_out_odd, v_ret_norm_g, v_s5_lam_re, v_s5_lam_im, v_s5_log_dt, v_s5_b_re, v_s5_b_im, v_s5_c_re, v_s5_c_im, v_s5_d, v_s5_glu_w, v_s5_glu_b, v_ffn_w_up, v_ffn_conv_w, v_ffn_conv_b, v_ffn_w_down):
    args = locals()
    w = {n: args[n] for n in WEIGHTS}
    m = {n: args["m_" + n] for n in WEIGHTS}
    v = {n: args["v_" + n] for n in WEIGHTS}
    Bl, S, D = x.shape
    T = Bl * S
    ix, iy, ic = lax.axis_index("x"), lax.axis_index("y"), lax.axis_index("c")
    me = 4 * ix + 2 * iy + ic

    xt = x.reshape(T, D)
    me1 = me.reshape(1).astype(jnp.int32)
    stages = []
    for layer in range(DEPTH):
        j = layer // 2
        kin, kout = ("w_in_even", "w_out_even") if layer % 2 == 0 else ("w_in_odd", "w_out_odd")
        stages.append((mixer_stage, layer, ("in", "out"), ((kin, j, True), (kout, j, False)), MIXER_SMALL[layer % 2]))
        stages.append((ffn_stage, layer, ("up", "down"), (("ffn_w_up", layer, True), ("ffn_w_down", layer, False)), FFN_SMALL))

    gather, after = [], xt
    for s, (_, _, _, projs, _) in enumerate(stages):
        handle = split_start(f"gather{s}_start", "gather", [w[n][l].astype(BF16) for n, l, _ in projs], None, after)
        gather.append(handle)
        after = handle[-1]
    sm_names = list(SMALL_SHARDED)
    (sm_all8,) = all_gather("gather_small", [_pack([w[n] for n in sm_names])])
    small = {n: _unshard(t, SMALL_SHARDED[n]) for n, t in zip(sm_names, _unpack(sm_all8, [w[n].shape for n in sm_names]))}
    small.update({n: w[n] for n in REPLICATED})
    small["mix_norm_g"] = small["mix_norm_g"] + after[0, 0]

    resid, branch, pulls = xt, None, []
    second = [None] * len(stages)

    def second_level(s, after):
        lands = split_wait(f"gather{s}_wait", "gather", gather[s], after)
        second[s] = split_start(f"forward{s}_start", "forward", [], lands, after)

    for s, (fn, layer, keys, projs, sm_keys) in enumerate(stages):
        here = lax.stop_gradient(resid)
        if second[s] is None:
            second_level(s, here)
        lands = split_wait(f"forward{s}_wait", "forward", second[s], second[s][-1])
        sm = {n: small[n] for n in sm_keys}
        if 2 <= s < len(stages) - 1:
            second_level(s + 1, here)
            norm = "mix_norm_g" if fn is mixer_stage else "ffn_norm_g"
            sm[norm] = sm[norm] + second[s + 1][-1][0, 0]
        full, carriers = [], {}
        for key, land, (n, l, col) in zip(keys, lands, projs, strict=True):
            if col:
                full.append(assemble_weight(f"weight{s}_{key}", land, w[n][l].astype(BF16), me1))
            else:
                blocks = lax.dynamic_update_index_in_dim(land, w[n][l].astype(BF16), me, 0)
                full.append(blocks.reshape(N_DEV * land.shape[1], land.shape[2]))
            carriers[key + "32"] = jnp.zeros(land.shape, F32)
            carriers[key + "16"] = jnp.zeros(land.shape, BF16)
        run = functools.partial(fn, layer, S)
        if branch is None:
            (resid, branch, _), pull = jax.vjp(lambda r, c, p, run=run, full=full: run(r, None, full[0], full[1], c, p), resid, carriers, sm)
        else:
            (resid, branch, _), pull = jax.vjp(lambda r, b, c, p, run=run, full=full: run(r, b, full[0], full[1], c, p), resid, branch, carriers, sm)
        pulls.append(pull)

    loss_acc, dxf, dgf = loss_head(resid, branch, small["final_norm_g"].reshape(1, D), loss_target.reshape(T, D))
    loss = lax.psum(loss_acc[0, 0], ("x", "y", "c"))
    g_small = {"final_norm_g": dgf.reshape(D)}
    d_resid, d_branch, token = dxf, dxf, _tick()
    scatter, own32 = [None] * len(stages), [None] * len(stages)
    for s in reversed(range(len(stages))):
        if s == 0:
            d_resid, dcar, dsm = pulls[s]((d_resid, d_branch, token))
        else:
            d_resid, d_branch, dcar, dsm = pulls[s]((d_resid, d_branch, token))
        for n, g in dsm.items():
            g_small[n] = g_small[n] + g if n in g_small else g
        keys = stages[s][2]
        own32[s] = [dcar[k + "32"] for k in keys]
        scatter[s] = split_start(f"grads{s}_start", "scatter", [dcar[k + "16"] for k in keys], None, d_resid)
        token = scatter[s][-1]
    dx = d_resid

    out = {}
    sm_all = sm_names + list(REPLICATED)
    g_pack = _pack([g_small[n] for n in sm_all])
    small_first = split_start("small_grads_start", "gather", [g_pack], None, dx)
    chain = {}

    def stored(t, col):
        return jnp.swapaxes(t, 1, 2) if col else t

    for s in reversed(range(1, len(stages))):
        lands = split_wait(f"grads{s}_wait", "scatter", scatter[s], small_first[-1] if s == len(stages) - 1 else dx)
        for own, land, (n, l, col) in zip(own32[s], lands, stages[s][3], strict=True):
            chain[n] = adamw_sharded(f"adamw_{n}_{l}", l, stored(w[n], col), stored(m[n], col), stored(v[n], col), own, land, me1,
                                     chain.get(n), col)
            behind = chain[n][0]

    lands = split_wait("small_grads_wait", "gather", small_first, behind)
    small_second = split_start("small_forward_start", "forward", [], lands, behind)
    (g_all,) = split_wait("small_forward_wait", "forward", small_second, small_second[-1])
    g_sum = _unpack(ordered_sum("sum_small_grads", g_all, g_pack, me1), [g_small[n].shape for n in sm_all])
    g_loc = []
    for n, g in zip(sm_all, g_sum):
        if n in SMALL_SHARDED:
            ax = SMALL_SHARDED[n]
            size = w[n].shape[ax]
            g = lax.dynamic_slice_in_dim(g, me * size, size, axis=ax)
        g_loc.append(g)
    shapes = [w[n].shape for n in sm_all]
    res = adamw("adamw_small", _pack([w[n] for n in sm_all]), _pack([m[n] for n in sm_all]), _pack([v[n] for n in sm_all]), _pack(g_loc))
    behind = res[0]
    res = [g_loc] + [_unpack(r, shapes) for r in res]
    for k, n in enumerate(sm_all):
        out[n] = [r[k] for r in res]

    lands = split_wait("grads0_wait", "scatter", scatter[0], behind)
    for own, land, (n, l, col) in zip(own32[0], lands, stages[0][3], strict=True):
        chain[n] = adamw_sharded(f"adamw_{n}_{l}", l, stored(w[n], col), stored(m[n], col), stored(v[n], col), own, land, me1,
                                 chain.get(n), col)
    for n in BIG:
        out[n] = [stored(t, BIG[n] == 2) for t in chain[n]]

    grads, deltas, new_m, new_v = ([out[n][k] for n in WEIGHTS] for k in range(4))
    return (loss, dx.reshape(Bl, S, D), *grads, *deltas, *new_m, *new_v)
```
